```python
import math
import jax, jax.numpy as jnp
from jax import lax
import numpy as np

D_MODEL = 1024
BATCH = 8
SEQ = 4096
DEPTH = 4

MLA_HEADS = 8
MLA_Q_LORA = 256
MLA_KV_LORA = 128
MLA_NOPE_DIM = 64
MLA_ROPE_DIM = 32
MLA_V_DIM = 64
ROPE_THETA = 10000.0
Q_BLOCK = 128
SC_DIM = 256
SC_WIDTH = 3
SSD_HEADS = 4
SSD_HEAD_DIM = 64
SSD_GROUPS = 2
SSD_STATE = 128
SSD_CONV_WIDTH = 4
SSD_CHUNK = 128
FFN_DIM = 2816
FFN_CONV_WIDTH = 3
NORM_EPS = 1e-6

MLA_QK_DIM = MLA_NOPE_DIM + MLA_ROPE_DIM
MLA_OUT = MLA_HEADS * MLA_V_DIM
SSD_DIM = SSD_HEADS * SSD_HEAD_DIM
SSD_CONV_DIM = SSD_DIM + 2 * SSD_GROUPS * SSD_STATE
SSD_IN = SSD_DIM + SSD_CONV_DIM + SSD_HEADS
IN_WIDTHS = (MLA_Q_LORA, MLA_KV_LORA, MLA_ROPE_DIM, SC_DIM, SC_DIM, SC_DIM, SSD_IN)
IN_SPLITS = tuple(int(v) for v in np.cumsum(IN_WIDTHS)[:-1])
D_IN = sum(IN_WIDTHS)
D_MIX = MLA_OUT + SC_DIM + SSD_DIM

kernel_name = "hybrid_mla_shortconv_ssd_convffn"


def rms_norm(x, w):
    xf = x.astype(jnp.float32)
    y = xf * lax.rsqrt(jnp.mean(xf * xf, axis=-1, keepdims=True) + NORM_EPS)
    return (y * w.astype(jnp.float32)).astype(x.dtype)


def causal_dwconv(u, w):
    width = w.shape[0]
    s = u.shape[1]
    up = jnp.pad(u, ((0, 0), (width - 1, 0), (0, 0)))
    out = up[:, 0:s] * w[0]
    for i in range(1, width):
        out = out + up[:, i:i + s] * w[i]
    return out


def rope(x, cos, sin):
    x1, x2 = jnp.split(x, 2, axis=-1)
    return jnp.concatenate([x1 * cos - x2 * sin, x2 * cos + x1 * sin], axis=-1).astype(x.dtype)


def mla_mixer(c_q, c_kv, k_rope, cos, sin, q_norm, w_q_up, kv_norm, w_kv_up):
    b, s, _ = c_q.shape
    q = (rms_norm(c_q, q_norm) @ w_q_up).reshape(b, s, MLA_HEADS, MLA_QK_DIM)
    q_nope = q[..., :MLA_NOPE_DIM]
    q_rope = rope(q[..., MLA_NOPE_DIM:], cos[:, :, None, :], sin[:, :, None, :])
    kv = (rms_norm(c_kv, kv_norm) @ w_kv_up).reshape(b, s, MLA_HEADS, MLA_NOPE_DIM + MLA_V_DIM)
    k_nope = kv[..., :MLA_NOPE_DIM]
    v = kv[..., MLA_NOPE_DIM:]
    k_rope = rope(k_rope, cos, sin)
    scale = MLA_QK_DIM ** -0.5
    key_idx = jnp.arange(s)

    def block(i):
        start = i * Q_BLOCK
        qn = lax.dynamic_slice_in_dim(q_nope, start, Q_BLOCK, axis=1)
        qr = lax.dynamic_slice_in_dim(q_rope, start, Q_BLOCK, axis=1)
        sc = jnp.einsum("bqhd,bkhd->bhqk", qn, k_nope) + jnp.einsum("bqhd,bkd->bhqk", qr, k_rope)
        sc = sc.astype(jnp.float32) * scale
        causal = (start + jnp.arange(Q_BLOCK))[:, None] >= key_idx[None, :]
        p = jax.nn.softmax(jnp.where(causal, sc, -jnp.inf), axis=-1)
        return jnp.einsum("bhqk,bkhd->bqhd", p.astype(v.dtype), v)

    out = lax.map(block, jnp.arange(s // Q_BLOCK))
    return out.transpose(1, 0, 2, 3, 4).reshape(b, s, MLA_OUT)


def short_conv_mixer(gate_b, gate_c, h, conv_w):
    return gate_b * causal_dwconv(gate_c * h, conv_w)


def ssd_scan(xdt, a_dt, bh, ch):
    b, s, h, p = xdt.shape
    n = bh.shape[-1]
    L = SSD_CHUNK
    c = s // L
    xdt = xdt.reshape(b, c, L, h, p)
    bh = bh.reshape(b, c, L, h, n)
    ch = ch.reshape(b, c, L, h, n)
    a_cs = jnp.cumsum(a_dt.reshape(b, c, L, h).transpose(0, 3, 1, 2), axis=-1)
    diff = a_cs[..., :, None] - a_cs[..., None, :]
    tri = jnp.tril(jnp.ones((L, L), dtype=bool))
    decay_in = jnp.exp(jnp.where(tri, diff, -jnp.inf))
    scores = jnp.einsum("bclhn,bcshn->bhcls", ch, bh) * decay_in
    y_diag = jnp.einsum("bhcls,bcshp->bclhp", scores, xdt)
    decay_to_end = jnp.exp(a_cs[..., -1:] - a_cs).transpose(0, 2, 3, 1)
    chunk_states = jnp.einsum("bclhn,bclhp->bchpn", bh * decay_to_end[..., None], xdt)
    chunk_decay = jnp.exp(a_cs[..., -1]).transpose(2, 0, 1)

    def step(state, inp):
        st, dec = inp
        return state * dec[..., None, None] + st, state

    init = jnp.zeros((b, h, p, n), chunk_states.dtype)
    _, prev = lax.scan(step, init, (chunk_states.transpose(1, 0, 2, 3, 4), chunk_decay))
    prev = prev.transpose(1, 0, 2, 3, 4)
    decay_from_start = jnp.exp(a_cs).transpose(0, 2, 3, 1)
    y_off = jnp.einsum("bclhn,bchpn->bclhp", ch, prev) * decay_from_start[..., None]
    return (y_diag + y_off).reshape(b, s, h, p)


def ssd_mixer(zxbcdt, conv_w, conv_b, dt_bias, a_log, d_skip, norm_w):
    b, s, _ = zxbcdt.shape
    z = zxbcdt[..., :SSD_DIM]
    xbc = zxbcdt[..., SSD_DIM:SSD_DIM + SSD_CONV_DIM]
    dt = zxbcdt[..., SSD_DIM + SSD_CONV_DIM:]
    xbc = jax.nn.silu(causal_dwconv(xbc, conv_w) + conv_b)
    xs = xbc[..., :SSD_DIM].reshape(b, s, SSD_HEADS, SSD_HEAD_DIM)
    heads_per_group = SSD_HEADS // SSD_GROUPS
    gn = SSD_GROUPS * SSD_STATE
    bm = jnp.repeat(xbc[..., SSD_DIM:SSD_DIM + gn].reshape(b, s, SSD_GROUPS, SSD_STATE), heads_per_group, axis=2)
    cm = jnp.repeat(xbc[..., SSD_DIM + gn:].reshape(b, s, SSD_GROUPS, SSD_STATE), heads_per_group, axis=2)
    dt = jax.nn.softplus(dt.astype(jnp.float32) + dt_bias.astype(jnp.float32))
    a = -jnp.exp(a_log.astype(jnp.float32))
    y = ssd_scan(xs * dt[..., None], dt * a, bm, cm)
    y = y + xs * d_skip[:, None]
    y = y.reshape(b, s, SSD_DIM).astype(zxbcdt.dtype)
    return rms_norm(y * jax.nn.silu(z), norm_w)


def _fwd_setup_inputs(seed: int = 0) -> dict:
    key = jax.random.key(seed)
    ks = jax.random.split(key, 24)
    L = DEPTH

    def normal(k, shape, scale):
        return scale * jax.random.normal(k, shape, jnp.float32)

    def gain(k, n):
        return 1.0 + normal(k, (L, n), 0.02)

    dt0 = jnp.exp(jax.random.uniform(ks[15], (L, SSD_HEADS), jnp.float32, math.log(1e-3), math.log(1e-1)))
    return {
        "x": normal(ks[0], (BATCH, SEQ, D_MODEL), 1.0),
        "positions": jnp.broadcast_to(jnp.arange(SEQ, dtype=jnp.int32), (BATCH, SEQ)),
        "norm_mix_pre": gain(ks[1], D_MODEL),
        "norm_mix_post": gain(ks[2], D_MODEL),
        "norm_ffn_pre": gain(ks[3], D_MODEL),
        "norm_ffn_post": gain(ks[4], D_MODEL),
        "w_in": normal(ks[5], (L, D_MODEL, D_IN), D_MODEL ** -0.5),
        "mla_q_norm": gain(ks[6], MLA_Q_LORA),
        "mla_w_q_up": normal(ks[7], (L, MLA_Q_LORA, MLA_HEADS * MLA_QK_DIM), MLA_Q_LORA ** -0.5),
        "mla_kv_norm": gain(ks[8], MLA_KV_LORA),
        "mla_w_kv_up": normal(ks[9], (L, MLA_KV_LORA, MLA_HEADS * (MLA_NOPE_DIM + MLA_V_DIM)), MLA_KV_LORA ** -0.5),
        "sc_conv_w": normal(ks[10], (L, SC_WIDTH, SC_DIM), SC_WIDTH ** -0.5),
        "ssd_conv_w": normal(ks[11], (L, SSD_CONV_WIDTH, SSD_CONV_DIM), SSD_CONV_WIDTH ** -0.5),
        "ssd_conv_b": normal(ks[12], (L, SSD_CONV_DIM), 0.02),
        "ssd_dt_bias": dt0 + jnp.log(-jnp.expm1(-dt0)),
        "ssd_a_log": jnp.log(jax.random.uniform(ks[13], (L, SSD_HEADS), jnp.float32, 1.0, 16.0)),
        "ssd_d": 1.0 + normal(ks[14], (L, SSD_HEADS), 0.1),
        "ssd_norm": gain(ks[16], SSD_DIM),
        "w_out": normal(ks[17], (L, D_MIX, D_MODEL), D_MIX ** -0.5),
        "ffn_w_up": normal(ks[18], (L, D_MODEL, 2 * FFN_DIM), D_MODEL ** -0.5),
        "ffn_conv_w": normal(ks[19], (L, FFN_CONV_WIDTH, 2 * FFN_DIM), FFN_CONV_WIDTH ** -0.5),
        "ffn_conv_b": normal(ks[20], (L, 2 * FFN_DIM), 0.02),
        "ffn_w_down": normal(ks[21], (L, FFN_DIM, D_MODEL), FFN_DIM ** -0.5),
    }


def _fwd_reference(x, positions, norm_mix_pre, norm_mix_post, norm_ffn_pre, norm_ffn_post, w_in,
              mla_q_norm, mla_w_q_up, mla_kv_norm, mla_w_kv_up, sc_conv_w, ssd_conv_w, ssd_conv_b,
              ssd_dt_bias, ssd_a_log, ssd_d, ssd_norm, w_out, ffn_w_up, ffn_conv_w, ffn_conv_b,
              ffn_w_down):
    inv_freq = 1.0 / (ROPE_THETA ** (jnp.arange(0, MLA_ROPE_DIM, 2, dtype=jnp.float32) / MLA_ROPE_DIM))
    ang = positions.astype(jnp.float32)[..., None] * inv_freq
    cos = jnp.cos(ang).astype(x.dtype)
    sin = jnp.sin(ang).astype(x.dtype)
    for l in range(DEPTH):
        h = rms_norm(x, norm_mix_pre[l])
        c_q, c_kv, k_rope, sc_b, sc_c, sc_h, ssd_in = jnp.split(h @ w_in[l], IN_SPLITS, axis=-1)
        y_att = mla_mixer(c_q, c_kv, k_rope, cos, sin, mla_q_norm[l], mla_w_q_up[l], mla_kv_norm[l], mla_w_kv_up[l])
        y_conv = short_conv_mixer(sc_b, sc_c, sc_h, sc_conv_w[l])
        y_ssd = ssd_mixer(ssd_in, ssd_conv_w[l], ssd_conv_b[l], ssd_dt_bias[l], ssd_a_log[l], ssd_d[l], ssd_norm[l])
        mixed = jnp.concatenate([y_att, y_conv, y_ssd], axis=-1) @ w_out[l]
        x = x + rms_norm(mixed, norm_mix_post[l])
        h = rms_norm(x, norm_ffn_pre[l])
        u = causal_dwconv(h @ ffn_w_up[l], ffn_conv_w[l]) + ffn_conv_b[l]
        gate, up = jnp.split(u, 2, axis=-1)
        x = x + rms_norm((jax.nn.silu(gate) * up) @ ffn_w_down[l], norm_ffn_post[l])
    return x


import jax as _jax
import jax.numpy as _jnp

TWIN_FORMAT = 'train_step'
FWD_PARAMS = ['x', 'positions', 'norm_mix_pre', 'norm_mix_post', 'norm_ffn_pre', 'norm_ffn_post', 'w_in', 'mla_q_norm', 'mla_w_q_up', 'mla_kv_norm', 'mla_w_kv_up', 'sc_conv_w', 'ssd_conv_w', 'ssd_conv_b', 'ssd_dt_bias', 'ssd_a_log', 'ssd_d', 'ssd_norm', 'w_out', 'ffn_w_up', 'ffn_conv_w', 'ffn_conv_b', 'ffn_w_down']
TWIN_WEIGHTS = ['norm_mix_pre', 'norm_mix_post', 'norm_ffn_pre', 'norm_ffn_post', 'w_in', 'mla_q_norm', 'mla_w_q_up', 'mla_kv_norm', 'mla_w_kv_up', 'sc_conv_w', 'ssd_conv_w', 'ssd_conv_b', 'ssd_dt_bias', 'ssd_a_log', 'ssd_d', 'ssd_norm', 'w_out', 'ffn_w_up', 'ffn_conv_w', 'ffn_conv_b', 'ffn_w_down']
TWIN_DIFF_INPUT = 'x'
TWIN_INPUTS = ['x', 'positions', 'norm_mix_pre', 'norm_mix_post', 'norm_ffn_pre', 'norm_ffn_post', 'w_in', 'mla_q_norm', 'mla_w_q_up', 'mla_kv_norm', 'mla_w_kv_up', 'sc_conv_w', 'ssd_conv_w', 'ssd_conv_b', 'ssd_dt_bias', 'ssd_a_log', 'ssd_d', 'ssd_norm', 'w_out', 'ffn_w_up', 'ffn_conv_w', 'ffn_conv_b', 'ffn_w_down', 'loss_target', 'm_norm_mix_pre', 'm_norm_mix_post', 'm_norm_ffn_pre', 'm_norm_ffn_post', 'm_w_in', 'm_mla_q_norm', 'm_mla_w_q_up', 'm_mla_kv_norm', 'm_mla_w_kv_up', 'm_sc_conv_w', 'm_ssd_conv_w', 'm_ssd_conv_b', 'm_ssd_dt_bias', 'm_ssd_a_log', 'm_ssd_d', 'm_ssd_norm', 'm_w_out', 'm_ffn_w_up', 'm_ffn_conv_w', 'm_ffn_conv_b', 'm_ffn_w_down', 'v_norm_mix_pre', 'v_norm_mix_post', 'v_norm_ffn_pre', 'v_norm_ffn_post', 'v_w_in', 'v_mla_q_norm', 'v_mla_w_q_up', 'v_mla_kv_norm', 'v_mla_w_kv_up', 'v_sc_conv_w', 'v_ssd_conv_w', 'v_ssd_conv_b', 'v_ssd_dt_bias', 'v_ssd_a_log', 'v_ssd_d', 'v_ssd_norm', 'v_w_out', 'v_ffn_w_up', 'v_ffn_conv_w', 'v_ffn_conv_b', 'v_ffn_w_down']
TWIN_OUTPUTS = ['loss', 'grad_x', 'grad_norm_mix_pre', 'grad_norm_mix_post', 'grad_norm_ffn_pre', 'grad_norm_ffn_post', 'grad_w_in', 'grad_mla_q_norm', 'grad_mla_w_q_up', 'grad_mla_kv_norm', 'grad_mla_w_kv_up', 'grad_sc_conv_w', 'grad_ssd_conv_w', 'grad_ssd_conv_b', 'grad_ssd_dt_bias', 'grad_ssd_a_log', 'grad_ssd_d', 'grad_ssd_norm', 'grad_w_out', 'grad_ffn_w_up', 'grad_ffn_conv_w', 'grad_ffn_conv_b', 'grad_ffn_w_down', 'delta_norm_mix_pre', 'delta_norm_mix_post', 'delta_norm_ffn_pre', 'delta_norm_ffn_post', 'delta_w_in', 'delta_mla_q_norm', 'delta_mla_w_q_up', 'delta_mla_kv_norm', 'delta_mla_w_kv_up', 'delta_sc_conv_w', 'delta_ssd_conv_w', 'delta_ssd_conv_b', 'delta_ssd_dt_bias', 'delta_ssd_a_log', 'delta_ssd_d', 'delta_ssd_norm', 'delta_w_out', 'delta_ffn_w_up', 'delta_ffn_conv_w', 'delta_ffn_conv_b', 'delta_ffn_w_down', 'new_m_norm_mix_pre', 'new_m_norm_mix_post', 'new_m_norm_ffn_pre', 'new_m_norm_ffn_post', 'new_m_w_in', 'new_m_mla_q_norm', 'new_m_mla_w_q_up', 'new_m_mla_kv_norm', 'new_m_mla_w_kv_up', 'new_m_sc_conv_w', 'new_m_ssd_conv_w', 'new_m_ssd_conv_b', 'new_m_ssd_dt_bias', 'new_m_ssd_a_log', 'new_m_ssd_d', 'new_m_ssd_norm', 'new_m_w_out', 'new_m_ffn_w_up', 'new_m_ffn_conv_w', 'new_m_ffn_conv_b', 'new_m_ffn_w_down', 'new_v_norm_mix_pre', 'new_v_norm_mix_post', 'new_v_norm_ffn_pre', 'new_v_norm_ffn_post', 'new_v_w_in', 'new_v_mla_q_norm', 'new_v_mla_w_q_up', 'new_v_mla_kv_norm', 'new_v_mla_w_kv_up', 'new_v_sc_conv_w', 'new_v_ssd_conv_w', 'new_v_ssd_conv_b', 'new_v_ssd_dt_bias', 'new_v_ssd_a_log', 'new_v_ssd_d', 'new_v_ssd_norm', 'new_v_w_out', 'new_v_ffn_w_up', 'new_v_ffn_conv_w', 'new_v_ffn_conv_b', 'new_v_ffn_w_down']
TWIN_LEAF_KINDS = {'loss': 'loss', 'grad_x': 'grad_x', 'grad_norm_mix_pre': 'grad_w', 'grad_norm_mix_post': 'grad_w', 'grad_norm_ffn_pre': 'grad_w', 'grad_norm_ffn_post': 'grad_w', 'grad_w_in': 'grad_w', 'grad_mla_q_norm': 'grad_w', 'grad_mla_w_q_up': 'grad_w', 'grad_mla_kv_norm': 'grad_w', 'grad_mla_w_kv_up': 'grad_w', 'grad_sc_conv_w': 'grad_w', 'grad_ssd_conv_w': 'grad_w', 'grad_ssd_conv_b': 'grad_w', 'grad_ssd_dt_bias': 'grad_w', 'grad_ssd_a_log': 'grad_w', 'grad_ssd_d': 'grad_w', 'grad_ssd_norm': 'grad_w', 'grad_w_out': 'grad_w', 'grad_ffn_w_up': 'grad_w', 'grad_ffn_conv_w': 'grad_w', 'grad_ffn_conv_b': 'grad_w', 'grad_ffn_w_down': 'grad_w', 'delta_norm_mix_pre': 'delta_w', 'delta_norm_mix_post': 'delta_w', 'delta_norm_ffn_pre': 'delta_w', 'delta_norm_ffn_post': 'delta_w', 'delta_w_in': 'delta_w', 'delta_mla_q_norm': 'delta_w', 'delta_mla_w_q_up': 'delta_w', 'delta_mla_kv_norm': 'delta_w', 'delta_mla_w_kv_up': 'delta_w', 'delta_sc_conv_w': 'delta_w', 'delta_ssd_conv_w': 'delta_w', 'delta_ssd_conv_b': 'delta_w', 'delta_ssd_dt_bias': 'delta_w', 'delta_ssd_a_log': 'delta_w', 'delta_ssd_d': 'delta_w', 'delta_ssd_norm': 'delta_w', 'delta_w_out': 'delta_w', 'delta_ffn_w_up': 'delta_w', 'delta_ffn_conv_w': 'delta_w', 'delta_ffn_conv_b': 'delta_w', 'delta_ffn_w_down': 'delta_w', 'new_m_norm_mix_pre': 'new_m', 'new_m_norm_mix_post': 'new_m', 'new_m_norm_ffn_pre': 'new_m', 'new_m_norm_ffn_post': 'new_m', 'new_m_w_in': 'new_m', 'new_m_mla_q_norm': 'new_m', 'new_m_mla_w_q_up': 'new_m', 'new_m_mla_kv_norm': 'new_m', 'new_m_mla_w_kv_up': 'new_m', 'new_m_sc_conv_w': 'new_m', 'new_m_ssd_conv_w': 'new_m', 'new_m_ssd_conv_b': 'new_m', 'new_m_ssd_dt_bias': 'new_m', 'new_m_ssd_a_log': 'new_m', 'new_m_ssd_d': 'new_m', 'new_m_ssd_norm': 'new_m', 'new_m_w_out': 'new_m', 'new_m_ffn_w_up': 'new_m', 'new_m_ffn_conv_w': 'new_m', 'new_m_ffn_conv_b': 'new_m', 'new_m_ffn_w_down': 'new_m', 'new_v_norm_mix_pre': 'new_v', 'new_v_norm_mix_post': 'new_v', 'new_v_norm_ffn_pre': 'new_v', 'new_v_norm_ffn_post': 'new_v', 'new_v_w_in': 'new_v', 'new_v_mla_q_norm': 'new_v', 'new_v_mla_w_q_up': 'new_v', 'new_v_mla_kv_norm': 'new_v', 'new_v_mla_w_kv_up': 'new_v', 'new_v_sc_conv_w': 'new_v', 'new_v_ssd_conv_w': 'new_v', 'new_v_ssd_conv_b': 'new_v', 'new_v_ssd_dt_bias': 'new_v', 'new_v_ssd_a_log': 'new_v', 'new_v_ssd_d': 'new_v', 'new_v_ssd_norm': 'new_v', 'new_v_w_out': 'new_v', 'new_v_ffn_w_up': 'new_v', 'new_v_ffn_conv_w': 'new_v', 'new_v_ffn_conv_b': 'new_v', 'new_v_ffn_w_down': 'new_v'}


def _forward(args):
    return _fwd_reference(*[args[k] for k in FWD_PARAMS])


def _output_shape():
    out = _jax.eval_shape(lambda: _forward(_fwd_setup_inputs(0)))
    return out.shape, out.dtype

N_MICROBATCH = 1
ADAM_LR = 0.001
ADAM_B1 = 0.9
ADAM_B2 = 0.999
ADAM_EPS = 1e-08
ADAM_WD = 0.01
ADAM_STEP = 10
PER_EXAMPLE_BATCH_AXIS = {'x': 0, 'positions': 0, 'loss_target': 0}
SHARED_INPUTS = []
_WEIGHT_DTYPES = {'norm_mix_pre': _jnp.float32, 'norm_mix_post': _jnp.float32, 'norm_ffn_pre': _jnp.float32, 'norm_ffn_post': _jnp.float32, 'w_in': _jnp.float32, 'mla_q_norm': _jnp.float32, 'mla_w_q_up': _jnp.float32, 'mla_kv_norm': _jnp.float32, 'mla_w_kv_up': _jnp.float32, 'sc_conv_w': _jnp.float32, 'ssd_conv_w': _jnp.float32, 'ssd_conv_b': _jnp.float32, 'ssd_dt_bias': _jnp.float32, 'ssd_a_log': _jnp.float32, 'ssd_d': _jnp.float32, 'ssd_norm': _jnp.float32, 'w_out': _jnp.float32, 'ffn_w_up': _jnp.float32, 'ffn_conv_w': _jnp.float32, 'ffn_conv_b': _jnp.float32, 'ffn_w_down': _jnp.float32}
MOMENT_SCALE = {'norm_mix_pre': 2.489747e+00, 'norm_mix_post': 3.146824e+01, 'norm_ffn_pre': 1.611442e+00, 'norm_ffn_post': 3.164393e+01, 'w_in': 1.696888e+00, 'mla_q_norm': 5.274611e-01, 'mla_w_q_up': 3.095609e-01, 'mla_kv_norm': 1.408331e+00, 'mla_w_kv_up': 4.671094e-01, 'sc_conv_w': 2.084694e+00, 'ssd_conv_w': 1.564639e+00, 'ssd_conv_b': 3.586432e+00, 'ssd_dt_bias': 4.870190e+00, 'ssd_a_log': 2.627886e+01, 'ssd_d': 1.705529e+01, 'ssd_norm': 3.035315e+00, 'w_out': 1.830392e+00, 'ffn_w_up': 6.988736e-01, 'ffn_conv_w': 7.158310e-01, 'ffn_conv_b': 1.626703e+00, 'ffn_w_down': 1.199837e+00}


def _to_microbatches(a, axis):
    t = _jnp.moveaxis(a, axis, 0)
    t = t.reshape((N_MICROBATCH, t.shape[0] // N_MICROBATCH) + t.shape[1:])
    return _jnp.moveaxis(t, 1, axis + 1)


def setup_inputs(seed: int = 0) -> dict:
    inp = _fwd_setup_inputs(seed)
    key = _jax.random.fold_in(_jax.random.key(seed), 7919)
    shape, _ = _output_shape()
    out = dict(inp)
    out["loss_target"] = _jax.random.normal(_jax.random.fold_in(key, 0), shape, _jnp.float32)
    for i, name in enumerate(TWIN_WEIGHTS):
        w = inp[name].astype(_jnp.float32)
        if MOMENT_SCALE is None:
            s = _jnp.sqrt(_jnp.mean(_jnp.square(w)) + 1e-30)
        else:
            s = MOMENT_SCALE[name]
        km, kv = _jax.random.split(_jax.random.fold_in(key, i + 1))
        out[name] = w
        out["m_" + name] = s * _jax.random.normal(km, w.shape, _jnp.float32)
        out["v_" + name] = (s * s) * _jax.random.uniform(kv, w.shape, _jnp.float32, 0.5, 1.5)
    if N_MICROBATCH > 1:
        for name, axis in PER_EXAMPLE_BATCH_AXIS.items():
            out[name] = _to_microbatches(out[name], axis)
    return {'x': out['x'], 'positions': out['positions'], 'norm_mix_pre': out['norm_mix_pre'], 'norm_mix_post': out['norm_mix_post'], 'norm_ffn_pre': out['norm_ffn_pre'], 'norm_ffn_post': out['norm_ffn_post'], 'w_in': out['w_in'], 'mla_q_norm': out['mla_q_norm'], 'mla_w_q_up': out['mla_w_q_up'], 'mla_kv_norm': out['mla_kv_norm'], 'mla_w_kv_up': out['mla_w_kv_up'], 'sc_conv_w': out['sc_conv_w'], 'ssd_conv_w': out['ssd_conv_w'], 'ssd_conv_b': out['ssd_conv_b'], 'ssd_dt_bias': out['ssd_dt_bias'], 'ssd_a_log': out['ssd_a_log'], 'ssd_d': out['ssd_d'], 'ssd_norm': out['ssd_norm'], 'w_out': out['w_out'], 'ffn_w_up': out['ffn_w_up'], 'ffn_conv_w': out['ffn_conv_w'], 'ffn_conv_b': out['ffn_conv_b'], 'ffn_w_down': out['ffn_w_down'], 'loss_target': out['loss_target'], 'm_norm_mix_pre': out['m_norm_mix_pre'], 'm_norm_mix_post': out['m_norm_mix_post'], 'm_norm_ffn_pre': out['m_norm_ffn_pre'], 'm_norm_ffn_post': out['m_norm_ffn_post'], 'm_w_in': out['m_w_in'], 'm_mla_q_norm': out['m_mla_q_norm'], 'm_mla_w_q_up': out['m_mla_w_q_up'], 'm_mla_kv_norm': out['m_mla_kv_norm'], 'm_mla_w_kv_up': out['m_mla_w_kv_up'], 'm_sc_conv_w': out['m_sc_conv_w'], 'm_ssd_conv_w': out['m_ssd_conv_w'], 'm_ssd_conv_b': out['m_ssd_conv_b'], 'm_ssd_dt_bias': out['m_ssd_dt_bias'], 'm_ssd_a_log': out['m_ssd_a_log'], 'm_ssd_d': out['m_ssd_d'], 'm_ssd_norm': out['m_ssd_norm'], 'm_w_out': out['m_w_out'], 'm_ffn_w_up': out['m_ffn_w_up'], 'm_ffn_conv_w': out['m_ffn_conv_w'], 'm_ffn_conv_b': out['m_ffn_conv_b'], 'm_ffn_w_down': out['m_ffn_w_down'], 'v_norm_mix_pre': out['v_norm_mix_pre'], 'v_norm_mix_post': out['v_norm_mix_post'], 'v_norm_ffn_pre': out['v_norm_ffn_pre'], 'v_norm_ffn_post': out['v_norm_ffn_post'], 'v_w_in': out['v_w_in'], 'v_mla_q_norm': out['v_mla_q_norm'], 'v_mla_w_q_up': out['v_mla_w_q_up'], 'v_mla_kv_norm': out['v_mla_kv_norm'], 'v_mla_w_kv_up': out['v_mla_w_kv_up'], 'v_sc_conv_w': out['v_sc_conv_w'], 'v_ssd_conv_w': out['v_ssd_conv_w'], 'v_ssd_conv_b': out['v_ssd_conv_b'], 'v_ssd_dt_bias': out['v_ssd_dt_bias'], 'v_ssd_a_log': out['v_ssd_a_log'], 'v_ssd_d': out['v_ssd_d'], 'v_ssd_norm': out['v_ssd_norm'], 'v_w_out': out['v_w_out'], 'v_ffn_w_up': out['v_ffn_w_up'], 'v_ffn_conv_w': out['v_ffn_conv_w'], 'v_ffn_conv_b': out['v_ffn_conv_b'], 'v_ffn_w_down': out['v_ffn_w_down']}


def _loss(weights, diff, rest, loss_target):
    with _jax.named_scope("forward"):
        args = {**rest, TWIN_DIFF_INPUT: diff, **{k: w.astype(_WEIGHT_DTYPES[k]) for k, w in weights.items()}}
        y = _forward(args)
    with _jax.named_scope("loss_head"):
        err = _jnp.square(y.astype(_jnp.float32) - loss_target)
        return 0.5 * _jnp.sum(_jnp.mean(err, axis=-1)) if err.ndim else 0.5 * err


def _adamw(w, g, m, v):
    m = ADAM_B1 * m + (1.0 - ADAM_B1) * g
    v = ADAM_B2 * v + (1.0 - ADAM_B2) * _jnp.square(g)
    m_hat = m / (1.0 - ADAM_B1 ** ADAM_STEP)
    v_hat = v / (1.0 - ADAM_B2 ** ADAM_STEP)
    delta = -ADAM_LR * (m_hat / (_jnp.sqrt(v_hat) + ADAM_EPS) + ADAM_WD * w)
    return delta, m, v


def reference(x, positions, norm_mix_pre, norm_mix_post, norm_ffn_pre, norm_ffn_post, w_in, mla_q_norm, mla_w_q_up, mla_kv_norm, mla_w_kv_up, sc_conv_w, ssd_conv_w, ssd_conv_b, ssd_dt_bias, ssd_a_log, ssd_d, ssd_norm, w_out, ffn_w_up, ffn_conv_w, ffn_conv_b, ffn_w_down, loss_target, m_norm_mix_pre, m_norm_mix_post, m_norm_ffn_pre, m_norm_ffn_post, m_w_in, m_mla_q_norm, m_mla_w_q_up, m_mla_kv_norm, m_mla_w_kv_up, m_sc_conv_w, m_ssd_conv_w, m_ssd_conv_b, m_ssd_dt_bias, m_ssd_a_log, m_ssd_d, m_ssd_norm, m_w_out, m_ffn_w_up, m_ffn_conv_w, m_ffn_conv_b, m_ffn_w_down, v_norm_mix_pre, v_norm_mix_post, v_norm_ffn_pre, v_norm_ffn_post, v_w_in, v_mla_q_norm, v_mla_w_q_up, v_mla_kv_norm, v_mla_w_kv_up, v_sc_conv_w, v_ssd_conv_w, v_ssd_conv_b, v_ssd_dt_bias, v_ssd_a_log, v_ssd_d, v_ssd_norm, v_w_out, v_ffn_w_up, v_ffn_conv_w, v_ffn_conv_b, v_ffn_w_down):
    given = dict(x=x, positions=positions, norm_mix_pre=norm_mix_pre, norm_mix_post=norm_mix_post, norm_ffn_pre=norm_ffn_pre, norm_ffn_post=norm_ffn_post, w_in=w_in, mla_q_norm=mla_q_norm, mla_w_q_up=mla_w_q_up, mla_kv_norm=mla_kv_norm, mla_w_kv_up=mla_w_kv_up, sc_conv_w=sc_conv_w, ssd_conv_w=ssd_conv_w, ssd_conv_b=ssd_conv_b, ssd_dt_bias=ssd_dt_bias, ssd_a_log=ssd_a_log, ssd_d=ssd_d, ssd_norm=ssd_norm, w_out=w_out, ffn_w_up=ffn_w_up, ffn_conv_w=ffn_conv_w, ffn_conv_b=ffn_conv_b, ffn_w_down=ffn_w_down, loss_target=loss_target, m_norm_mix_pre=m_norm_mix_pre, m_norm_mix_post=m_norm_mix_post, m_norm_ffn_pre=m_norm_ffn_pre, m_norm_ffn_post=m_norm_ffn_post, m_w_in=m_w_in, m_mla_q_norm=m_mla_q_norm, m_mla_w_q_up=m_mla_w_q_up, m_mla_kv_norm=m_mla_kv_norm, m_mla_w_kv_up=m_mla_w_kv_up, m_sc_conv_w=m_sc_conv_w, m_ssd_conv_w=m_ssd_conv_w, m_ssd_conv_b=m_ssd_conv_b, m_ssd_dt_bias=m_ssd_dt_bias, m_ssd_a_log=m_ssd_a_log, m_ssd_d=m_ssd_d, m_ssd_norm=m_ssd_norm, m_w_out=m_w_out, m_ffn_w_up=m_ffn_w_up, m_ffn_conv_w=m_ffn_conv_w, m_ffn_conv_b=m_ffn_conv_b, m_ffn_w_down=m_ffn_w_down, v_norm_mix_pre=v_norm_mix_pre, v_norm_mix_post=v_norm_mix_post, v_norm_ffn_pre=v_norm_ffn_pre, v_norm_ffn_post=v_norm_ffn_post, v_w_in=v_w_in, v_mla_q_norm=v_mla_q_norm, v_mla_w_q_up=v_mla_w_q_up, v_mla_kv_norm=v_mla_kv_norm, v_mla_w_kv_up=v_mla_w_kv_up, v_sc_conv_w=v_sc_conv_w, v_ssd_conv_w=v_ssd_conv_w, v_ssd_conv_b=v_ssd_conv_b, v_ssd_dt_bias=v_ssd_dt_bias, v_ssd_a_log=v_ssd_a_log, v_ssd_d=v_ssd_d, v_ssd_norm=v_ssd_norm, v_w_out=v_w_out, v_ffn_w_up=v_ffn_w_up, v_ffn_conv_w=v_ffn_conv_w, v_ffn_conv_b=v_ffn_conv_b, v_ffn_w_down=v_ffn_w_down)
    weights = {n: given[n] for n in TWIN_WEIGHTS}
    shared = {n: given[n] for n in SHARED_INPUTS}
    per_example = {n: given[n] for n in ['x', 'positions']}
    grad_fn = _jax.value_and_grad(_loss, argnums=(0, 1))

    def one_microbatch(ex, loss_target):
        ex = dict(ex)
        diff = ex.pop(TWIN_DIFF_INPUT)
        return grad_fn(weights, diff, {**shared, **ex}, loss_target)

    if N_MICROBATCH == 1:
        loss, (grad_w, grad_x) = one_microbatch(per_example, given["loss_target"])
    else:
        def body(carry, xs):
            loss_sum, grad_sum = carry
            l_k, (gw_k, gx_k) = one_microbatch(xs[0], xs[1])
            with _jax.named_scope("update"):
                return (loss_sum + l_k, _jax.tree.map(_jnp.add, grad_sum, gw_k)), gx_k

        init = (_jnp.zeros((), _jnp.float32), _jax.tree.map(_jnp.zeros_like, weights))
        (loss, grad_w), grad_x = _jax.lax.scan(body, init, (per_example, given["loss_target"]))
    with _jax.named_scope("update"):
        delta_w, new_m, new_v = {}, {}, {}
        for n in TWIN_WEIGHTS:
            delta_w[n], new_m[n], new_v[n] = _adamw(weights[n], grad_w[n], given["m_" + n], given["v_" + n])
    return (loss, grad_x, *[grad_w[n] for n in TWIN_WEIGHTS], *[delta_w[n] for n in TWIN_WEIGHTS],
            *[new_m[n] for n in TWIN_WEIGHTS], *[new_v[n] for n in TWIN_WEIGHTS])
```

```python
import functools
import math

import jax
import jax.numpy as jnp
from jax import lax
from jax.experimental import pallas as pl
from jax.experimental.pallas import tpu as pltpu

F32 = jnp.float32
BF16 = jnp.bfloat16
MXU_DTYPE = jnp.bfloat16
HIGHEST = lax.Precision.HIGHEST
MESH = pl.DeviceIdType.MESH

D_MODEL = 1024
DEPTH = 4
HEADS = 8
Q_LORA = 256
KV_LORA = 128
NOPE = 64
ROPE = 32
VDIM = 64
ROPE_THETA = 10000.0
SC_DIM = 256
SSD_HEADS = 4
SSD_HEAD_DIM = 64
SSD_STATE = 128
SSD_DIM = 256
SSD_CONV_DIM = 768
SSD_CHUNK = 128
FFN_DIM = 2816
NORM_EPS = 1e-6
QK_SCALE = (NOPE + ROPE) ** -0.5
LANE = 128
HP = 128

ZIN = 2560
Z_CQ, Z_CKV, Z_KR, Z_SCB, Z_SCC, Z_SCH, Z_SSZ, Z_XBC, Z_DT = 0, 256, 384, 512, 768, 1024, 1280, 1536, 2304
KR_LANE = 64
YCAT = HEADS * HP + SC_DIM + SSD_DIM
FFN_TILE = 256

ADAM_LR, ADAM_B1, ADAM_B2, ADAM_EPS, ADAM_WD, ADAM_STEP = 0.001, 0.9, 0.999, 1e-08, 0.01, 10

PACK_COLS = 1024


def _tile(n, pref):
    if n <= pref:
        return n
    t = (pref // LANE) * LANE
    while t >= LANE:
        if n % t == 0:
            return t
        t -= LANE
    raise ValueError(f"no tile for {n}")


def _mm(a, b, mode, out_dtype, name, tm=512, tn=512, tkmax=1536):
    if mode == "nn":
        (M, K), (_, N) = a.shape, b.shape
    elif mode == "nt":
        (M, K), (N, _) = a.shape, b.shape
    else:
        (K, M), (_, N) = a.shape, b.shape
    tm, tn, tk = _tile(M, tm), _tile(N, tn), _tile(K, tkmax)
    nk = K // tk
    if mode == "nn":
        a_spec = pl.BlockSpec((tm, tk), lambda i, j, k: (i, k))
        b_spec = pl.BlockSpec((tk, tn), lambda i, j, k: (k, j))
        dims = (((1,), (0,)), ((), ()))
    elif mode == "nt":
        a_spec = pl.BlockSpec((tm, tk), lambda i, j, k: (i, k))
        b_spec = pl.BlockSpec((tn, tk), lambda i, j, k: (j, k))
        dims = (((1,), (1,)), ((), ()))
    else:
        a_spec = pl.BlockSpec((tk, tm), lambda i, j, k: (k, i))
        b_spec = pl.BlockSpec((tk, tn), lambda i, j, k: (k, j))
        dims = (((0,), (0,)), ((), ()))

    def body(a_ref, b_ref, o_ref, acc_ref):
        k = pl.program_id(2)

        @pl.when(k == 0)
        def _():
            acc_ref[...] = jnp.zeros_like(acc_ref)

        acc_ref[...] += lax.dot_general(a_ref[...].astype(MXU_DTYPE), b_ref[...].astype(MXU_DTYPE), dims,
                                        preferred_element_type=F32)

        @pl.when(k == nk - 1)
        def _():
            o_ref[...] = acc_ref[...].astype(o_ref.dtype)

    return pl.pallas_call(
        body, name=name, grid=(M // tm, N // tn, nk),
        in_specs=[a_spec, b_spec], out_specs=pl.BlockSpec((tm, tn), lambda i, j, k: (i, j)),
        out_shape=jax.ShapeDtypeStruct((M, N), out_dtype),
        scratch_shapes=[pltpu.VMEM((tm, tn), F32)],
        compiler_params=pltpu.CompilerParams(dimension_semantics=("parallel", "parallel", "arbitrary")),
    )(a, b)


HALO = 8


def _const(j, v):
    return v


def _rows(fn, T, tm, ins, consts, outs, accs, name, ncol=1):
    n = T // tm
    hb = tm // HALO
    last = T // HALO - 1
    in_specs, args = [], []
    for arr, bc, cb, kind in ins:
        if isinstance(kind, int):
            in_specs.append(pl.BlockSpec((tm, bc), lambda j, i, cb=cb, off=kind: (i + off, cb(j))))
        elif kind == "cur":
            in_specs.append(pl.BlockSpec((tm, bc), lambda j, i, cb=cb: (i, cb(j))))
        elif kind == "prev":
            in_specs.append(pl.BlockSpec((HALO, bc), lambda j, i, cb=cb: (jnp.maximum(i * hb - 1, 0), cb(j))))
        else:
            in_specs.append(pl.BlockSpec((HALO, bc), lambda j, i, cb=cb: (jnp.minimum((i + 1) * hb, last), cb(j))))
        args.append(arr)
    for arr, bc, cb in consts:
        in_specs.append(pl.BlockSpec((arr.shape[0], bc), lambda j, i, cb=cb: (0, cb(j))))
        args.append(arr)
    out_specs, out_shape = [], []
    for tc, dt, bc, cb in outs:
        out_specs.append(pl.BlockSpec((tm, bc), lambda j, i, cb=cb: (i, cb(j))))
        out_shape.append(jax.ShapeDtypeStruct((T, tc), dt))
    for r, tc, bc, cb in accs:
        out_specs.append(pl.BlockSpec((r, bc), lambda j, i, cb=cb: (0, cb(j))))
        out_shape.append(jax.ShapeDtypeStruct((r, tc), F32))
    nin, nout, nacc = len(args), len(outs), len(accs)

    def body(*refs):
        i = pl.program_id(1)
        res = fn(i, n, *[r[...] for r in refs[:nin]])
        for r, v in zip(refs[nin:nin + nout], res[:nout]):
            r[...] = v.astype(r.dtype)
        if nacc:
            acc_refs = refs[nin + nout:nin + nout + nacc]

            @pl.when(i == 0)
            def _():
                for r in acc_refs:
                    r[...] = jnp.zeros_like(r)

            for r, v in zip(acc_refs, res[nout:]):
                r[...] += v.astype(F32)

    res = pl.pallas_call(
        body, name=name, grid=(ncol, n), in_specs=in_specs, out_specs=out_specs, out_shape=out_shape,
        compiler_params=pltpu.CompilerParams(dimension_semantics=("arbitrary", "arbitrary")),
    )(*args)
    return res


def _cur(arr, bc=None, blk=0):
    bc = arr.shape[1] if bc is None else bc
    return (arr, bc, functools.partial(_const, v=blk), "cur")


def _halo(arr, kind, bc=None, blk=0):
    bc = arr.shape[1] if bc is None else bc
    return (arr, bc, functools.partial(_const, v=blk), kind)


def _cst(arr):
    return (arr, arr.shape[1], functools.partial(_const, v=0))


def _out(cols, dt):
    return (cols, dt, cols, functools.partial(_const, v=0))


def _acc(rows, cols):
    return (rows, cols, cols, functools.partial(_const, v=0))


def _rms(x, w):
    return x * lax.rsqrt(jnp.mean(x * x, axis=-1, keepdims=True) + NORM_EPS) * w


def _silu(x):
    return x * (1.0 / (1.0 + jnp.exp(-x)))


def _dsilu(x):
    s = 1.0 / (1.0 + jnp.exp(-x))
    return s * (1.0 + x * (1.0 - s))


def _softplus(x):
    return jnp.maximum(x, 0.0) + jnp.log1p(jnp.exp(-jnp.abs(x)))


def _shift(a, k):
    return pltpu.roll(a, k % a.shape[0], 0)


def _lroll(a, k):
    return pltpu.roll(a, k % a.shape[1], 1)


def _vjp_wrap(f, nrow, nconst, add_first=False):
    def g(i, n, *vals):
        rows, consts, mid = vals[:nrow], vals[len(vals) - nconst:], vals[nrow:len(vals) - nconst]
        cots = mid[:-1] if add_first else mid
        outs, pull = jax.vjp(f, *rows, *consts)
        grads = list(pull(tuple(c.astype(o.dtype) for c, o in zip(cots, outs))))
        if add_first:
            grads[0] = grads[0] + mid[-1]
        return tuple(grads)
    return g


def _rows_vjp(f, T, tm, rows, consts, cots, out_dtypes, name):
    return _rows(_vjp_wrap(f, len(rows), len(consts)), T, tm, [_cur(r) for r in rows] + [_cur(c) for c in cots],
                 [_cst(c) for c in consts], [_out(r.shape[1], dt) for r, dt in zip(rows, out_dtypes)],
                 [_acc(1, c.shape[1]) for c in consts], name)


def _f_premix(x, g):
    return (_rms(x, g),)


def _f_mla_pre(cq, ckv, qn, kvn):
    return _rms(cq, qn), _rms(ckv, kvn)


def _f_ssd_gate(y, z, nw):
    return (_rms(y * _silu(z), nw),)


def _f_post_mix(x, mixed, gpost, gffn):
    x1 = x + _rms(mixed, gpost)
    return x1, _rms(x1, gffn)


def _f_post_ffn(x1, d, gpost):
    return (x1 + _rms(d, gpost),)


def _rope_fwd(v, cosf, sina, sinb):
    return v * cosf + _lroll(v, -16) * sina + _lroll(v, 16) * sinb


def _rope_bwd(g, cosf, sina, sinb):
    return g * cosf + _lroll(g * sina, 16) + _lroll(g * sinb, -16)


def _k_rope_fwd(i, n, qpad, kvpad, kr, cosf, sina, sinb):
    qs, ks = [], []
    krr = _rope_fwd(kr, cosf, sina, sinb)
    for h in range(HEADS):
        sl = slice(h * HP, (h + 1) * HP)
        qs.append(_rope_fwd(qpad[:, sl], cosf, sina, sinb))
        ks.append(kvpad[:, sl].astype(F32) + krr)
    return jnp.concatenate(qs, axis=1), jnp.concatenate(ks, axis=1)


def _k_rope_bwd(i, n, dq, dk, dv, cosf, sina, sinb):
    lane = lax.broadcasted_iota(jnp.int32, (1, HP), 1)
    rmask = ((lane >= KR_LANE) & (lane < KR_LANE + ROPE)).astype(F32)
    dqs, dks = [], []
    dkr = jnp.zeros((dq.shape[0], HP), F32)
    for h in range(HEADS):
        sl = slice(h * HP, (h + 1) * HP)
        dqs.append(_rope_bwd(dq[:, sl], cosf, sina, sinb))
        dkh = dk[:, sl]
        dkr = dkr + dkh * rmask
        dks.append(dkh * (1.0 - rmask))
    dkr = _rope_bwd(dkr, cosf, sina, sinb) * rmask
    return jnp.concatenate(dqs, axis=1), jnp.concatenate(dks + [dv], axis=1), dkr


def _k_sconv_fwd(i, n, b, c, h, cp, hp, w):
    m = b.shape[0]
    up = jnp.where(i > 0, cp * hp, 0.0)
    ue = jnp.concatenate([up, c * h], axis=0)
    conv = w[2:3] * ue + w[1:2] * _shift(ue, 1) + w[0:1] * _shift(ue, 2)
    return (b * conv[HALO:],)


def _k_sconv_bwd(i, n, b, c, h, dy, cp, hp, bn, dyn, w):
    m = b.shape[0]
    up = jnp.where(i > 0, cp * hp, 0.0)
    ue = jnp.concatenate([up, c * h], axis=0)
    u1, u2 = _shift(ue, 1), _shift(ue, 2)
    conv = (w[2:3] * ue + w[1:2] * u1 + w[0:1] * u2)[HALO:]
    dc_cur = dy * b
    dce = jnp.concatenate([dc_cur, jnp.where(i < n - 1, dyn * bn, 0.0)], axis=0)
    du = (w[2:3] * dce + w[1:2] * _shift(dce, -1) + w[0:1] * _shift(dce, -2))[:m]
    dw = jnp.concatenate([
        jnp.sum(dc_cur * u2[HALO:], axis=0, keepdims=True),
        jnp.sum(dc_cur * u1[HALO:], axis=0, keepdims=True),
        jnp.sum(dc_cur * ue[HALO:], axis=0, keepdims=True),
        jnp.zeros((HALO - 3, b.shape[1]), F32)], axis=0)
    return dy * conv, du * h, du * c, dw


def _conv4(ue, w):
    return w[3:4] * ue + w[2:3] * _shift(ue, 1) + w[1:2] * _shift(ue, 2) + w[0:1] * _shift(ue, 3)


def _k_ssdconv_fwd(i, n, u, up, w, bias):
    ue = jnp.concatenate([jnp.where(i > 0, up, 0.0), u], axis=0)
    return (_silu(_conv4(ue, w)[HALO:] + bias),)


def _k_ssdconv_bwd(i, n, u, dout, up, un, doutn, w, bias):
    m = u.shape[0]
    ue = jnp.concatenate([jnp.where(i > 0, up, 0.0), u, un], axis=0)
    u1, u2, u3 = _shift(ue, 1), _shift(ue, 2), _shift(ue, 3)
    pre = (w[3:4] * ue + w[2:3] * u1 + w[1:2] * u2 + w[0:1] * u3)[HALO:] + bias
    doe = jnp.concatenate([dout, jnp.where(i < n - 1, doutn, 0.0)], axis=0)
    dpre = doe * _dsilu(pre)
    du = (w[3:4] * dpre + w[2:3] * _shift(dpre, -1) + w[1:2] * _shift(dpre, -2) + w[0:1] * _shift(dpre, -3))[:m]
    dp = dpre[:m]
    cur = slice(HALO, HALO + m)
    dw = jnp.concatenate([
        jnp.sum(dp * u3[cur], axis=0, keepdims=True),
        jnp.sum(dp * u2[cur], axis=0, keepdims=True),
        jnp.sum(dp * u1[cur], axis=0, keepdims=True),
        jnp.sum(dp * ue[cur], axis=0, keepdims=True),
        jnp.zeros((HALO - 4, u.shape[1]), F32)], axis=0)
    db = jnp.sum(dp, axis=0, keepdims=True)
    return du, dw, db


def _conv3(ue, w):
    return w[2:3] * ue + w[1:2] * _shift(ue, 1) + w[0:1] * _shift(ue, 2)


def _k_ffnact_fwd(i, n, u, up, w, bias):
    ue = jnp.concatenate([jnp.where(i > 0, up, 0.0), u], axis=0)
    uc = _conv3(ue, w)[HALO:] + bias
    return (_silu(uc[:, :FFN_TILE]) * uc[:, FFN_TILE:],)


def _k_ffnact_bwd(i, n, u, dact, up, un, dactn, w, bias):
    m = u.shape[0]
    ue = jnp.concatenate([jnp.where(i > 0, up, 0.0), u, un], axis=0)
    u1, u2 = _shift(ue, 1), _shift(ue, 2)
    uc = (w[2:3] * ue + w[1:2] * u1 + w[0:1] * u2)[HALO:] + bias
    dae = jnp.concatenate([dact, jnp.where(i < n - 1, dactn, 0.0)], axis=0)
    gate, upv = uc[:, :FFN_TILE], uc[:, FFN_TILE:]
    duc = jnp.concatenate([dae * upv * _dsilu(gate), dae * _silu(gate)], axis=1)
    du = (w[2:3] * duc + w[1:2] * _shift(duc, -1) + w[0:1] * _shift(duc, -2))[:m]
    dp = duc[:m]
    cur = slice(HALO, HALO + m)
    dw = jnp.concatenate([
        jnp.sum(dp * u2[cur], axis=0, keepdims=True),
        jnp.sum(dp * u1[cur], axis=0, keepdims=True),
        jnp.sum(dp * ue[cur], axis=0, keepdims=True),
        jnp.zeros((HALO - 3, u.shape[1]), F32)], axis=0)
    db = jnp.sum(dp, axis=0, keepdims=True)
    return du, dw, db


def _k_loss(i, n, y, tgt):
    e = y - tgt
    part = 0.5 * jnp.sum(jnp.sum(e * e, axis=1, keepdims=True) / D_MODEL, axis=0, keepdims=True)
    return e * (1.0 / D_MODEL), jnp.broadcast_to(part, (1, LANE))


def _k_adam(i, n, w, g, m, v):
    m = ADAM_B1 * m + (1.0 - ADAM_B1) * g
    v = ADAM_B2 * v + (1.0 - ADAM_B2) * (g * g)
    m_hat = m / (1.0 - ADAM_B1 ** ADAM_STEP)
    v_hat = v / (1.0 - ADAM_B2 ** ADAM_STEP)
    delta = -ADAM_LR * (m_hat / (jnp.sqrt(v_hat) + ADAM_EPS) + ADAM_WD * w)
    return delta, m, v


def _dotf(a, b, dims):
    return lax.dot_general(a.astype(MXU_DTYPE), b.astype(MXU_DTYPE), dims, preferred_element_type=F32)


NN = (((1,), (0,)), ((), ()))
NT = (((1,), (1,)), ((), ()))
TN = (((0,), (0,)), ((), ()))


def _ssd_chunk(x0, x1, x2, x3, b0, b1, c0, c1, dtraw, p0, p1, p2, p3, dtb, alog, dsk):
    xs, bs, cs_, ps = (x0, x1, x2, x3), (b0, b1), (c0, c1), (p0, p1, p2, p3)
    L = dtraw.shape[0]
    dt = _softplus(dtraw + dtb)
    adt = dt * (-jnp.exp(alog))
    row = lax.broadcasted_iota(jnp.int32, (L, L), 0)
    col = lax.broadcasted_iota(jnp.int32, (L, L), 1)
    tril = row >= col
    cum = jnp.dot(tril.astype(F32), adt, precision=HIGHEST, preferred_element_type=F32)
    cum_t = cum.T
    lane = lax.broadcasted_iota(jnp.int32, (1, LANE), 1)
    sub = lax.broadcasted_iota(jnp.int32, (LANE, 1), 0)
    lastcol = (lax.broadcasted_iota(jnp.int32, (1, L), 1) == L - 1).astype(F32)
    ys, news = [], []
    for h in range(SSD_HEADS):
        g = h // (SSD_HEADS // 2)
        oh = (lane == h).astype(F32)
        dth = jnp.sum(dt * oh, axis=1, keepdims=True)
        csh = jnp.sum(cum * oh, axis=1, keepdims=True)
        csr = jnp.sum(cum_t * (sub == h).astype(F32), axis=0, keepdims=True)
        cl = jnp.sum(csr * lastcol, axis=1, keepdims=True)
        dskh = jnp.sum(dsk * oh, axis=1, keepdims=True)
        x, bm, cm, prev = xs[h], bs[g], cs_[g], ps[h]
        xdt = x * dth
        decay = jnp.exp(jnp.where(tril, csh - csr, -jnp.inf))
        scores = _dotf(cm, bm, NT) * decay
        y_diag = _dotf(scores, xdt, NN)
        bd = bm * jnp.exp(cl - csh)
        cst = _dotf(xdt, bd, TN)
        news.append(prev * jnp.exp(cl) + cst)
        y_off = _dotf(cm, prev, NT) * jnp.exp(csh)
        ys.append(y_diag + y_off + x * dskh)
    return (*ys, *news)


def _ssd_operands(x_ref, dt_ref, par_ref, prev):
    xs = [x_ref[:, h * SSD_HEAD_DIM:(h + 1) * SSD_HEAD_DIM] for h in range(SSD_HEADS)]
    bs = [x_ref[:, SSD_DIM + g * SSD_STATE:SSD_DIM + (g + 1) * SSD_STATE] for g in range(2)]
    cs_ = [x_ref[:, SSD_DIM + 2 * SSD_STATE + g * SSD_STATE:SSD_DIM + 2 * SSD_STATE + (g + 1) * SSD_STATE] for g in range(2)]
    return (*xs, *bs, *cs_, dt_ref[...], *prev, par_ref[0:1, :], par_ref[1:2, :], par_ref[2:3, :])


def _ssd_fwd(xbc, dtraw, par, T):
    L = SSD_CHUNK
    nc = T // L
    P = SSD_HEAD_DIM

    def body(x_ref, dt_ref, par_ref, y_ref, st_ref, state):
        @pl.when(pl.program_id(0) == 0)
        def _():
            state[...] = jnp.zeros_like(state)

        st_ref[0] = state[...]
        prev = [state[h * P:(h + 1) * P, :] for h in range(SSD_HEADS)]
        res = _ssd_chunk(*_ssd_operands(x_ref, dt_ref, par_ref, prev))
        for h in range(SSD_HEADS):
            y_ref[:, h * P:(h + 1) * P] = res[h]
            state[h * P:(h + 1) * P, :] = res[SSD_HEADS + h]

    return pl.pallas_call(
        body, name="ssd_scan_fwd", grid=(nc,),
        in_specs=[pl.BlockSpec((L, SSD_CONV_DIM), lambda c: (c, 0)), pl.BlockSpec((L, LANE), lambda c: (c, 0)),
                  pl.BlockSpec((8, LANE), lambda c: (0, 0))],
        out_specs=[pl.BlockSpec((L, SSD_DIM), lambda c: (c, 0)), pl.BlockSpec((1, SSD_DIM, SSD_STATE), lambda c: (c, 0, 0))],
        out_shape=[jax.ShapeDtypeStruct((T, SSD_DIM), F32), jax.ShapeDtypeStruct((nc, SSD_DIM, SSD_STATE), F32)],
        scratch_shapes=[pltpu.VMEM((SSD_DIM, SSD_STATE), F32)],
        compiler_params=pltpu.CompilerParams(dimension_semantics=("arbitrary",)),
    )(xbc, dtraw, par)


def _ssd_bwd(xbc, dtraw, par, states, dy, T):
    L = SSD_CHUNK
    nc = T // L
    P = SSD_HEAD_DIM

    def body(x_ref, dt_ref, par_ref, st_ref, dy_ref, dx_ref, ddt_ref, dpar_ref, dstate):
        @pl.when(pl.program_id(0) == 0)
        def _():
            dstate[...] = jnp.zeros_like(dstate)
            dpar_ref[...] = jnp.zeros_like(dpar_ref)

        prev = [st_ref[0, h * P:(h + 1) * P, :] for h in range(SSD_HEADS)]
        prim = _ssd_operands(x_ref, dt_ref, par_ref, prev)
        _, pull = jax.vjp(_ssd_chunk, *prim)
        cots = tuple(dy_ref[:, h * P:(h + 1) * P] for h in range(SSD_HEADS)) + tuple(
            dstate[h * P:(h + 1) * P, :] for h in range(SSD_HEADS))
        g = pull(cots)
        for h in range(SSD_HEADS):
            dx_ref[:, h * P:(h + 1) * P] = g[h]
            dstate[h * P:(h + 1) * P, :] = g[9 + h]
        for k in range(2):
            dx_ref[:, SSD_DIM + k * SSD_STATE:SSD_DIM + (k + 1) * SSD_STATE] = g[4 + k]
            dx_ref[:, SSD_DIM + 2 * SSD_STATE + k * SSD_STATE:SSD_DIM + 2 * SSD_STATE + (k + 1) * SSD_STATE] = g[6 + k]
        ddt_ref[...] = g[8]
        for r in range(3):
            dpar_ref[r:r + 1, :] += g[13 + r]

    rev = lambda c: (nc - 1 - c, 0)
    return pl.pallas_call(
        body, name="ssd_scan_bwd", grid=(nc,),
        in_specs=[pl.BlockSpec((L, SSD_CONV_DIM), rev), pl.BlockSpec((L, LANE), rev), pl.BlockSpec((8, LANE), lambda c: (0, 0)),
                  pl.BlockSpec((1, SSD_DIM, SSD_STATE), lambda c: (nc - 1 - c, 0, 0)), pl.BlockSpec((L, SSD_DIM), rev)],
        out_specs=[pl.BlockSpec((L, SSD_CONV_DIM), rev), pl.BlockSpec((L, LANE), rev), pl.BlockSpec((8, LANE), lambda c: (0, 0))],
        out_shape=[jax.ShapeDtypeStruct((T, SSD_CONV_DIM), F32), jax.ShapeDtypeStruct((T, LANE), F32),
                   jax.ShapeDtypeStruct((8, LANE), F32)],
        scratch_shapes=[pltpu.VMEM((SSD_DIM, SSD_STATE), F32)],
        compiler_params=pltpu.CompilerParams(dimension_semantics=("arbitrary",)),
    )(xbc, dtraw, par, states, dy)


def _flash_fwd(q, k, kv, T):
    tq = tk = min(512, T)
    nq = T // tq

    def body(q_ref, k_ref, v_ref, o_ref, m_ref, l_ref, acc_ref):
        i, j = pl.program_id(1), pl.program_id(2)

        @pl.when(j == 0)
        def _():
            m_ref[...] = jnp.full_like(m_ref, -jnp.inf)
            l_ref[...] = jnp.zeros_like(l_ref)
            acc_ref[...] = jnp.zeros_like(acc_ref)

        @pl.when(j <= i)
        def _():
            s = _dotf(q_ref[...], k_ref[...], NT) * QK_SCALE
            rows = i * tq + lax.broadcasted_iota(jnp.int32, (tq, tk), 0)
            cols = j * tk + lax.broadcasted_iota(jnp.int32, (tq, tk), 1)
            s = jnp.where(rows >= cols, s, -jnp.inf)
            m_old = m_ref[...]
            m_new = jnp.maximum(m_old, jnp.max(s, axis=1, keepdims=True))
            p = jnp.exp(s - m_new)
            alpha = jnp.exp(m_old - m_new)
            l_ref[...] = alpha * l_ref[...] + jnp.sum(p, axis=1, keepdims=True)
            acc_ref[...] = alpha * acc_ref[...] + _dotf(p, v_ref[...], NN)
            m_ref[...] = m_new

        @pl.when(j == i)
        def _():
            l = l_ref[...]
            lse = m_ref[...] + jnp.log(l)
            lane = lax.broadcasted_iota(jnp.int32, (tq, HP), 1)
            o_ref[...] = jnp.where(lane < VDIM, acc_ref[...] / l, lse)

    return pl.pallas_call(
        body, name="mla_flash_fwd", grid=(HEADS, nq, nq),
        in_specs=[pl.BlockSpec((tq, HP), lambda h, i, j: (i, h)),
                  pl.BlockSpec((tk, HP), lambda h, i, j: (jnp.minimum(j, i), h)),
                  pl.BlockSpec((tk, HP), lambda h, i, j: (jnp.minimum(j, i), HEADS + h))],
        out_specs=pl.BlockSpec((tq, HP), lambda h, i, j: (i, h)),
        out_shape=jax.ShapeDtypeStruct((T, HEADS * HP), F32),
        scratch_shapes=[pltpu.VMEM((tq, 1), F32), pltpu.VMEM((tq, 1), F32), pltpu.VMEM((tq, HP), F32)],
        compiler_params=pltpu.CompilerParams(dimension_semantics=("parallel", "parallel", "arbitrary")),
    )(q, k, kv)


def _flash_bwd(q, k, kv, o, dycat, T):
    tq = tk = min(512, T)
    nq = T // tq

    def body(q_ref, k_ref, v_ref, o_ref, do_ref, dq_ref, dk_ref, dv_ref):
        j, i = pl.program_id(1), pl.program_id(2)

        @pl.when((j == 0) & (i == 0))
        def _():
            dq_ref[...] = jnp.zeros_like(dq_ref)

        @pl.when(i == 0)
        def _():
            dk_ref[...] = jnp.zeros_like(dk_ref)
            dv_ref[...] = jnp.zeros_like(dv_ref)

        @pl.when(i >= j)
        def _():
            qv, kv, vv, ov, dov = q_ref[...], k_ref[...], v_ref[...], o_ref[...], do_ref[...]
            s = _dotf(qv, kv, NT) * QK_SCALE
            rows = i * tq + lax.broadcasted_iota(jnp.int32, (tq, tk), 0)
            cols = j * tk + lax.broadcasted_iota(jnp.int32, (tq, tk), 1)
            lse = ov[:, VDIM:VDIM + 1]
            p = jnp.where(rows >= cols, jnp.exp(s - lse), 0.0)
            dsum = jnp.sum(dov * ov, axis=1, keepdims=True)
            dv_ref[...] += _dotf(p, dov, TN)
            dp = _dotf(dov, vv, NT)
            ds = p * (dp - dsum) * QK_SCALE
            dk_ref[...] += _dotf(ds, qv, TN)
            r0 = pl.multiple_of(i * tq, tq)
            dq_ref[pl.ds(r0, tq), :] += _dotf(ds, kv, NN)

    qmap = lambda h, j, i: (jnp.maximum(i, j), h)
    kmap = lambda h, j, i: (j, h)
    vmap = lambda h, j, i: (j, HEADS + h)
    return pl.pallas_call(
        body, name="mla_flash_bwd", grid=(HEADS, nq, nq),
        in_specs=[pl.BlockSpec((tq, HP), qmap), pl.BlockSpec((tk, HP), kmap), pl.BlockSpec((tk, HP), vmap),
                  pl.BlockSpec((tq, HP), qmap), pl.BlockSpec((tq, HP), qmap)],
        out_specs=[pl.BlockSpec((T, HP), lambda h, j, i: (0, h)), pl.BlockSpec((tk, HP), kmap), pl.BlockSpec((tk, HP), kmap)],
        out_shape=[jax.ShapeDtypeStruct((T, HEADS * HP), F32)] * 3,
        compiler_params=pltpu.CompilerParams(dimension_semantics=("parallel", "arbitrary", "arbitrary")),
    )(q, k, kv, o, dycat)


_IN_SRC = (0, 256, 384, 416, 672, 928, 1184, 1440, 2208, 2212)
_IN_DST = (Z_CQ, Z_CKV, Z_KR + KR_LANE, Z_SCB, Z_SCC, Z_SCH, Z_SSZ, Z_XBC, Z_DT)


def _pad_cols_in(w):
    parts, at = [], 0
    for s0, s1, d0 in zip(_IN_SRC[:-1], _IN_SRC[1:], _IN_DST):
        if d0 > at:
            parts.append(jnp.zeros(w.shape[:-1] + (d0 - at,), w.dtype))
        parts.append(w[..., s0:s1])
        at = d0 + (s1 - s0)
    parts.append(jnp.zeros(w.shape[:-1] + (ZIN - at,), w.dtype))
    return jnp.concatenate(parts, axis=-1)


def _unpad_cols_in(w):
    return jnp.concatenate([w[..., d0:d0 + (s1 - s0)] for s0, s1, d0 in zip(_IN_SRC[:-1], _IN_SRC[1:], _IN_DST)], axis=-1)


def _pad_heads(w, width):
    w = w.reshape(w.shape[:-1] + (HEADS, width))
    w = jnp.pad(w, [(0, 0)] * (w.ndim - 1) + [(0, HP - width)])
    return w.reshape(w.shape[:-2] + (HEADS * HP,))


def _unpad_heads(w, width):
    w = w.reshape(w.shape[:-1] + (HEADS, HP))[..., :width]
    return w.reshape(w.shape[:-2] + (HEADS * width,))


def _pad_kv(w):
    w = w.reshape(w.shape[:-1] + (HEADS, NOPE + VDIM))
    return jnp.concatenate([_pad_heads(w[..., :NOPE].reshape(w.shape[:-2] + (HEADS * NOPE,)), NOPE),
                            _pad_heads(w[..., NOPE:].reshape(w.shape[:-2] + (HEADS * VDIM,)), VDIM)], axis=-1)


def _unpad_kv(w):
    k = _unpad_heads(w[..., :HEADS * HP], NOPE).reshape(w.shape[:-1] + (HEADS, NOPE))
    v = _unpad_heads(w[..., HEADS * HP:], VDIM).reshape(w.shape[:-1] + (HEADS, VDIM))
    return jnp.concatenate([k, v], axis=-1).reshape(w.shape[:-1] + (HEADS * (NOPE + VDIM),))


def _pad_out_rows(w):
    att = jnp.swapaxes(_pad_heads(jnp.swapaxes(w[:HEADS * VDIM], 0, 1), VDIM), 0, 1)
    return jnp.concatenate([att, w[HEADS * VDIM:]], axis=0)


def _unpad_out_rows(w):
    att = jnp.swapaxes(_unpad_heads(jnp.swapaxes(w[:HEADS * HP], 0, 1), VDIM), 0, 1)
    return jnp.concatenate([att, w[HEADS * HP:]], axis=0)


def _interleave_ffn(w):
    nt = FFN_DIM // FFN_TILE
    w = w.reshape(w.shape[:-1] + (2, nt, FFN_TILE))
    return jnp.swapaxes(w, -3, -2).reshape(w.shape[:-3] + (2 * FFN_DIM,))


def _deinterleave_ffn(w):
    nt = FFN_DIM // FFN_TILE
    w = w.reshape(w.shape[:-1] + (nt, 2, FFN_TILE))
    return jnp.swapaxes(w, -3, -2).reshape(w.shape[:-3] + (2 * FFN_DIM,))


def _row8(*vecs):
    c = vecs[0].shape[-1]
    return jnp.concatenate([v.reshape(1, c).astype(F32) for v in vecs] + [jnp.zeros((8 - len(vecs), c), F32)], axis=0)


def _lanes(v):
    return jnp.pad(v.astype(F32), (0, LANE - v.shape[0])).reshape(1, LANE)


def _rope_tables(positions):
    inv_freq = 1.0 / (ROPE_THETA ** (jnp.arange(0, ROPE, 2, dtype=F32) / ROPE))
    ang = positions.astype(F32)[:, None] * inv_freq
    cos, sin = jnp.cos(ang), jnp.sin(ang)
    T = positions.shape[0]
    half = ROPE // 2
    one = jnp.ones((T, KR_LANE), F32)
    zero = jnp.zeros((T, KR_LANE), F32)
    tail1 = jnp.ones((T, HP - KR_LANE - ROPE), F32)
    tail0 = jnp.zeros((T, HP - KR_LANE - ROPE), F32)
    z16 = jnp.zeros((T, half), F32)
    cosf = jnp.concatenate([one, cos, cos, tail1], axis=1)
    sina = jnp.concatenate([zero, -sin, z16, tail0], axis=1)
    sinb = jnp.concatenate([zero, z16, sin, tail0], axis=1)
    return cosf, sina, sinb


def _layer_weights(W, l):
    c = lambda a: a.astype(MXU_DTYPE)
    return dict(
        w_in=c(_pad_cols_in(W["w_in"][l])),
        w_q=c(_pad_heads(W["mla_w_q_up"][l], NOPE + ROPE)),
        w_kv=c(_pad_kv(W["mla_w_kv_up"][l])),
        w_out=c(_pad_out_rows(W["w_out"][l])),
        w_up=c(_interleave_ffn(W["ffn_w_up"][l])),
        w_down=c(W["ffn_w_down"][l]),
        sc_w=_row8(*W["sc_conv_w"][l].astype(F32)),
        ssd_w=_row8(*W["ssd_conv_w"][l].astype(F32)),
        ffn_w=_interleave_ffn(_row8(*W["ffn_conv_w"][l].astype(F32))),
    )


def _local_step(x, positions, target, W, S):
    T = x.shape[0]
    tm = min(256, T)
    cosf, sina, sinb = _rope_tables(positions)
    saved = []
    xl = x
    for l in range(DEPTH):
        lw = _layer_weights(W, l)
        g_pre = S["norm_mix_pre"][l].reshape(1, -1)
        g_post = S["norm_mix_post"][l].reshape(1, -1)
        g_fpre = S["norm_ffn_pre"][l].reshape(1, -1)
        g_fpost = S["norm_ffn_post"][l].reshape(1, -1)
        qn = S["mla_q_norm"][l].reshape(1, -1)
        kvn = S["mla_kv_norm"][l].reshape(1, -1)
        ssd_b = S["ssd_conv_b"][l].reshape(1, -1)
        ssd_par = _row8(jnp.pad(S["ssd_dt_bias"][l], (0, LANE - SSD_HEADS)), jnp.pad(S["ssd_a_log"][l], (0, LANE - SSD_HEADS)),
                        jnp.pad(S["ssd_d"][l], (0, LANE - SSD_HEADS)))
        ssd_nw = S["ssd_norm"][l].reshape(1, -1)
        ffn_b = _interleave_ffn(S["ffn_conv_b"][l].reshape(1, -1))

        (h1,) = _rows(lambda i, n, *v: _f_premix(*v), T, tm, [_cur(xl)], [_cst(g_pre)], [_out(D_MODEL, BF16)], [], "pre_mix_norm")
        zin = _mm(h1, lw["w_in"], "nn", F32, "mm_in")
        qlat, kvlat = _rows(lambda i, n, *v: _f_mla_pre(*v), T, tm, [_cur(zin, Q_LORA, 0), _cur(zin, KV_LORA, Z_CKV // KV_LORA)],
                            [_cst(qn), _cst(kvn)], [_out(Q_LORA, BF16), _out(KV_LORA, BF16)], [], "mla_pre_norm")
        qpad = _mm(qlat, lw["w_q"], "nn", F32, "mm_q_up")
        kvpad = _mm(kvlat, lw["w_kv"], "nn", BF16, "mm_kv_up")
        qr, kr = _rows(_k_rope_fwd, T, tm, [_cur(qpad), _cur(kvpad, HEADS * HP, 0), _cur(zin, LANE, Z_KR // LANE),
                                            _cur(cosf), _cur(sina), _cur(sinb)], [],
                       [_out(HEADS * HP, BF16), _out(HEADS * HP, BF16)], [], "mla_rope")
        o = _flash_fwd(qr, kr, kvpad, T)
        (yconv,) = _rows(_k_sconv_fwd, T, tm, [_cur(zin, SC_DIM, Z_SCB // SC_DIM), _cur(zin, SC_DIM, Z_SCC // SC_DIM),
                                               _cur(zin, SC_DIM, Z_SCH // SC_DIM), _halo(zin, "prev", SC_DIM, Z_SCC // SC_DIM),
                                               _halo(zin, "prev", SC_DIM, Z_SCH // SC_DIM)], [_cst(lw["sc_w"])],
                         [_out(SC_DIM, F32)], [], "short_conv_fwd")
        (xbc,) = _rows(_k_ssdconv_fwd, T, tm, [_cur(zin, SSD_CONV_DIM, Z_XBC // SSD_CONV_DIM),
                                               _halo(zin, "prev", SSD_CONV_DIM, Z_XBC // SSD_CONV_DIM)],
                       [_cst(lw["ssd_w"]), _cst(ssd_b)], [_out(SSD_CONV_DIM, F32)], [], "ssd_conv_fwd")
        dtraw = zin[:, Z_DT:Z_DT + LANE]
        yscan, states = _ssd_fwd(xbc, dtraw, ssd_par, T)
        (yssd,) = _rows(lambda i, n, *v: _f_ssd_gate(*v), T, tm, [_cur(yscan), _cur(zin, SSD_DIM, Z_SSZ // SSD_DIM)], [_cst(ssd_nw)],
                        [_out(SSD_DIM, F32)], [], "ssd_gate_fwd")
        ycat = jnp.concatenate([o.astype(BF16), yconv.astype(BF16), yssd.astype(BF16)], axis=1)
        mixed = _mm(ycat, lw["w_out"], "nn", F32, "mm_out")
        x1, h2 = _rows(lambda i, n, *v: _f_post_mix(*v), T, tm, [_cur(xl), _cur(mixed)], [_cst(g_post), _cst(g_fpre)],
                       [_out(D_MODEL, F32), _out(D_MODEL, BF16)], [], "post_mix_fwd")
        upre = _mm(h2, lw["w_up"], "nn", F32, "mm_up")
        nt = FFN_DIM // FFN_TILE
        ident = lambda j: j
        (act,) = _rows(_k_ffnact_fwd, T, tm, [(upre, 2 * FFN_TILE, ident, "cur"), (upre, 2 * FFN_TILE, ident, "prev")],
                       [(lw["ffn_w"], 2 * FFN_TILE, ident), (ffn_b, 2 * FFN_TILE, ident)],
                       [(FFN_DIM, BF16, FFN_TILE, ident)], [], "ffn_act_fwd", ncol=nt)
        dn = _mm(act, lw["w_down"], "nn", F32, "mm_down")
        (x2,) = _rows(lambda i, n, *v: _f_post_ffn(*v), T, tm, [_cur(x1), _cur(dn)], [_cst(g_fpost)], [_out(D_MODEL, F32)], [], "post_ffn_fwd")
        saved.append(dict(lw=lw, x=xl, h1=h1, zin=zin, qlat=qlat, kvlat=kvlat, qr=qr, kr=kr, kvpad=kvpad, o=o, xbc=xbc, dtraw=dtraw,
                          yscan=yscan, states=states, ycat=ycat, mixed=mixed, x1=x1, h2=h2, upre=upre, act=act, dn=dn,
                          g_pre=g_pre, g_post=g_post, g_fpre=g_fpre, g_fpost=g_fpost, qn=qn, kvn=kvn, ssd_b=ssd_b,
                          ssd_par=ssd_par, ssd_nw=ssd_nw, ffn_b=ffn_b))
        xl = x2

    gx, loss_part = _rows(_k_loss, T, tm, [_cur(xl), _cur(target)], [], [_out(D_MODEL, F32)], [_acc(1, LANE)], "loss_head")

    GW = {k: [None] * DEPTH for k in ("w_in", "mla_w_q_up", "mla_w_kv_up", "sc_conv_w", "ssd_conv_w", "w_out", "ffn_w_up",
                                      "ffn_conv_w", "ffn_w_down")}
    GS = {k: [None] * DEPTH for k in ("norm_mix_pre", "norm_mix_post", "norm_ffn_pre", "norm_ffn_post", "mla_q_norm", "mla_kv_norm",
                                      "ssd_conv_b", "ssd_dt_bias", "ssd_a_log", "ssd_d", "ssd_norm", "ffn_conv_b")}
    ident = lambda j: j
    nt = FFN_DIM // FFN_TILE
    for l in reversed(range(DEPTH)):
        s = saved[l]
        lw = s["lw"]
        gx1, ddn, dgf = _rows_vjp(_f_post_ffn, T, tm, [s["x1"], s["dn"]], [s["g_fpost"]], [gx], [F32, BF16], "post_ffn_bwd")
        GS["norm_ffn_post"][l] = dgf[0]
        dact = _mm(ddn, lw["w_down"], "nt", F32, "mm_down_dx")
        GW["ffn_w_down"][l] = _mm(s["act"], ddn, "tn", F32, "mm_down_dw")
        du, dfw, dfb = _rows(_k_ffnact_bwd, T, tm,
                             [(s["upre"], 2 * FFN_TILE, ident, "cur"), (dact, FFN_TILE, ident, "cur"),
                              (s["upre"], 2 * FFN_TILE, ident, "prev"), (s["upre"], 2 * FFN_TILE, ident, "next"),
                              (dact, FFN_TILE, ident, "next")],
                             [(lw["ffn_w"], 2 * FFN_TILE, ident), (s["ffn_b"], 2 * FFN_TILE, ident)],
                             [(2 * FFN_DIM, BF16, 2 * FFN_TILE, ident)],
                             [(HALO, 2 * FFN_DIM, 2 * FFN_TILE, ident), (1, 2 * FFN_DIM, 2 * FFN_TILE, ident)], "ffn_act_bwd", ncol=nt)
        GW["ffn_conv_w"][l] = _deinterleave_ffn(dfw[:3])
        GS["ffn_conv_b"][l] = _deinterleave_ffn(dfb[0])
        dh2 = _mm(du, lw["w_up"], "nt", F32, "mm_up_dx")
        GW["ffn_w_up"][l] = _deinterleave_ffn(_mm(s["h2"], du, "tn", F32, "mm_up_dw"))
        gx0, dmixed, dgp, dgf = _rows_vjp(_f_post_mix, T, tm, [s["x"], s["mixed"]], [s["g_post"], s["g_fpre"]], [gx1, dh2],
                                          [F32, BF16], "post_mix_bwd")
        GS["norm_mix_post"][l], GS["norm_ffn_pre"][l] = dgp[0], dgf[0]
        dycat = _mm(dmixed, lw["w_out"], "nt", F32, "mm_out_dx")
        GW["w_out"][l] = _unpad_out_rows(_mm(s["ycat"], dmixed, "tn", F32, "mm_out_dw"))
        zin = s["zin"]
        dyscan, dz, dnw = _rows(_vjp_wrap(_f_ssd_gate, 2, 1), T, tm,
                                [_cur(s["yscan"]), _cur(zin, SSD_DIM, Z_SSZ // SSD_DIM), _cur(dycat, SSD_DIM, (HEADS * HP + SC_DIM) // SSD_DIM)],
                                [_cst(s["ssd_nw"])], [_out(SSD_DIM, F32), _out(SSD_DIM, BF16)], [_acc(1, SSD_DIM)], "ssd_gate_bwd")
        GS["ssd_norm"][l] = dnw[0]
        dxbc, ddtraw, dpar = _ssd_bwd(s["xbc"], s["dtraw"], s["ssd_par"], s["states"], dyscan, T)
        GS["ssd_dt_bias"][l], GS["ssd_a_log"][l], GS["ssd_d"][l] = dpar[0, :SSD_HEADS], dpar[1, :SSD_HEADS], dpar[2, :SSD_HEADS]
        xb = Z_XBC // SSD_CONV_DIM
        dxraw, dsw, dsb = _rows(_k_ssdconv_bwd, T, tm,
                                [_cur(zin, SSD_CONV_DIM, xb), _cur(dxbc), _halo(zin, "prev", SSD_CONV_DIM, xb),
                                 _halo(zin, "next", SSD_CONV_DIM, xb), _halo(dxbc, "next")],
                                [_cst(lw["ssd_w"]), _cst(s["ssd_b"])], [_out(SSD_CONV_DIM, BF16)],
                                [_acc(HALO, SSD_CONV_DIM), _acc(1, SSD_CONV_DIM)], "ssd_conv_bwd")
        GW["ssd_conv_w"][l] = dsw[:4]
        GS["ssd_conv_b"][l] = dsb[0]
        cb = (HEADS * HP) // SC_DIM
        dscb, dscc, dsch, dscw = _rows(_k_sconv_bwd, T, tm,
                                       [_cur(zin, SC_DIM, Z_SCB // SC_DIM), _cur(zin, SC_DIM, Z_SCC // SC_DIM),
                                        _cur(zin, SC_DIM, Z_SCH // SC_DIM), _cur(dycat, SC_DIM, cb),
                                        _halo(zin, "prev", SC_DIM, Z_SCC // SC_DIM), _halo(zin, "prev", SC_DIM, Z_SCH // SC_DIM),
                                        _halo(zin, "next", SC_DIM, Z_SCB // SC_DIM), _halo(dycat, "next", SC_DIM, cb)],
                                       [_cst(lw["sc_w"])], [_out(SC_DIM, BF16)] * 3, [_acc(HALO, SC_DIM)], "short_conv_bwd")
        GW["sc_conv_w"][l] = dscw[:3]
        dq, dk, dv = _flash_bwd(s["qr"], s["kr"], s["kvpad"], s["o"], dycat, T)
        dqpad, dkvpad, dkr = _rows(_k_rope_bwd, T, tm, [_cur(dq), _cur(dk), _cur(dv), _cur(cosf), _cur(sina), _cur(sinb)], [],
                                   [_out(HEADS * HP, BF16), _out(2 * HEADS * HP, BF16), _out(LANE, BF16)], [], "mla_rope_bwd")
        dqlat = _mm(dqpad, lw["w_q"], "nt", F32, "mm_q_dx")
        GW["mla_w_q_up"][l] = _unpad_heads(_mm(s["qlat"], dqpad, "tn", F32, "mm_q_dw"), NOPE + ROPE)
        dkvlat = _mm(dkvpad, lw["w_kv"], "nt", F32, "mm_kv_dx")
        GW["mla_w_kv_up"][l] = _unpad_kv(_mm(s["kvlat"], dkvpad, "tn", F32, "mm_kv_dw"))
        dcq, dckv, dqn, dkvn = _rows(_vjp_wrap(_f_mla_pre, 2, 2), T, tm,
                                     [_cur(zin, Q_LORA, 0), _cur(zin, KV_LORA, Z_CKV // KV_LORA), _cur(dqlat), _cur(dkvlat)],
                                     [_cst(s["qn"]), _cst(s["kvn"])], [_out(Q_LORA, BF16), _out(KV_LORA, BF16)],
                                     [_acc(1, Q_LORA), _acc(1, KV_LORA)], "mla_pre_bwd")
        GS["mla_q_norm"][l], GS["mla_kv_norm"][l] = dqn[0], dkvn[0]
        dzin = jnp.concatenate([dcq, dckv, dkr, dscb, dscc, dsch, dz, dxraw, ddtraw.astype(BF16), jnp.zeros((T, ZIN - Z_DT - LANE), BF16)], axis=1)
        dh1 = _mm(dzin, lw["w_in"], "nt", F32, "mm_in_dx")
        GW["w_in"][l] = _unpad_cols_in(_mm(s["h1"], dzin, "tn", F32, "mm_in_dw"))
        gx, dgp = _rows(_vjp_wrap(_f_premix, 1, 1, add_first=True), T, tm, [_cur(s["x"]), _cur(dh1), _cur(gx0)], [_cst(s["g_pre"])],
                        [_out(D_MODEL, F32)], [_acc(1, D_MODEL)], "pre_mix_bwd")
        GS["norm_mix_pre"][l] = dgp[0]
    GW = {k: jnp.stack(v) for k, v in GW.items()}
    GS = {k: jnp.stack(v) for k, v in GS.items()}
    return loss_part[0, 0], gx, GW, GS


WEIGHTS = ("norm_mix_pre", "norm_mix_post", "norm_ffn_pre", "norm_ffn_post", "w_in", "mla_q_norm", "mla_w_q_up", "mla_kv_norm",
           "mla_w_kv_up", "sc_conv_w", "ssd_conv_w", "ssd_conv_b", "ssd_dt_bias", "ssd_a_log", "ssd_d", "ssd_norm", "w_out",
           "ffn_w_up", "ffn_conv_w", "ffn_conv_b", "ffn_w_down")
SHARDED = (("w_in", 2), ("mla_w_q_up", 2), ("mla_w_kv_up", 2), ("sc_conv_w", 2), ("ssd_conv_w", 2), ("w_out", 1),
           ("ffn_w_up", 2), ("ffn_conv_w", 2), ("ffn_w_down", 1))
SMALL = tuple(n for n in WEIGHTS if n not in dict(SHARDED))
N_CHIPS = 4
N_DEV = 8
ROW_ALIGN = 64


def _pack(parts, rows, dtype):
    flat = jnp.concatenate([p.astype(dtype).reshape(-1) for p in parts])
    return jnp.pad(flat, (0, rows * PACK_COLS - flat.shape[0])).reshape(rows, PACK_COLS)


def _unpack(buf, shapes):
    flat, out, off = buf.reshape(-1), [], 0
    for shp in shapes:
        n = math.prod(shp)
        out.append(flat[off:off + n].reshape(shp))
        off += n
    return out


def _pack_rows(shapes):
    n = sum(math.prod(s) for s in shapes)
    return -(-n // (PACK_COLS * ROW_ALIGN)) * ROW_ALIGN


ANY = pl.BlockSpec(memory_space=pl.ANY)


def _pos():
    return lax.axis_index("x"), lax.axis_index("y"), lax.axis_index("c")


def _other_chips(x, y):
    return ((1 - x, y), (x, 1 - y), (1 - x, 1 - y))


def _remote(src, dst, ssem, rsem, dev):
    return pltpu.make_async_remote_copy(src_ref=src, dst_ref=dst, send_sem=ssem, recv_sem=rsem, device_id=dev, device_id_type=MESH)


AG_CHUNKS = 2


def _all_gather_weights(wpk):
    R, C = wpk.shape
    H = R // 2
    CH = H // AG_CHUNKS
    n = 3 * AG_CHUNKS

    def body(w_ref, out_ref, isend, irecv, dsend, drecv, lsem):
        x, y, c = _pos()
        k = 2 * x + y
        sib = (x, y, 1 - c)
        chips = _other_chips(x, y)

        def rows(kk, half, ch):
            return out_ref.at[kk, pl.ds(half * H + ch * CH, CH), :]

        mine = pltpu.make_async_copy(w_ref, out_ref.at[k], lsem)
        mine.start()
        first = []
        for p, (cx, cy) in enumerate(chips):
            for ch in range(AG_CHUNKS):
                s = p * AG_CHUNKS + ch
                cp = _remote(w_ref.at[pl.ds(c * H + ch * CH, CH), :], rows(k, c, ch), isend.at[s], irecv.at[s], (cx, cy, c))
                cp.start()
                first.append(cp)
        passed = []
        for p, (cx, cy) in enumerate(chips):
            for ch in range(AG_CHUNKS):
                s = p * AG_CHUNKS + ch
                land = rows(2 * cx + cy, c, ch)
                _remote(land, land, isend.at[s], irecv.at[s], (cx, cy, c)).wait_recv()
                fw = _remote(land, land, dsend.at[s], drecv.at[s], sib)
                fw.start()
                passed.append(fw)
        for p, (cx, cy) in enumerate(chips):
            for ch in range(AG_CHUNKS):
                s = p * AG_CHUNKS + ch
                land = rows(2 * cx + cy, 1 - c, ch)
                _remote(land, land, dsend.at[s], drecv.at[s], sib).wait_recv()
        for cp in first + passed:
            cp.wait_send()
        mine.wait()

    return pl.pallas_call(
        body, name="all_gather_weights", in_specs=[ANY], out_specs=ANY,
        out_shape=jax.ShapeDtypeStruct((N_CHIPS, R, C), wpk.dtype),
        scratch_shapes=[pltpu.SemaphoreType.DMA((n,)), pltpu.SemaphoreType.DMA((n,)), pltpu.SemaphoreType.DMA((n,)),
                        pltpu.SemaphoreType.DMA((n,)), pltpu.SemaphoreType.DMA],
    )(wpk)


def _rs_pair_exchange(g):
    _, R, C = g.shape
    H = R // 2

    def body(g_ref, own_ref, got_ref, ssem, rsem, lsem):
        x, y, c = _pos()
        sib = (x, y, 1 - c)
        cps = []
        for kk in range(N_CHIPS):
            loc = pltpu.make_async_copy(g_ref.at[kk, pl.ds(c * H, H), :], own_ref.at[kk], lsem.at[kk])
            loc.start()
            cp = _remote(g_ref.at[kk, pl.ds((1 - c) * H, H), :], got_ref.at[kk], ssem.at[kk], rsem.at[kk], sib)
            cp.start()
            cps += [loc, cp]
        for cp in cps:
            cp.wait()

    return pl.pallas_call(
        body, name="rs_pair_exchange", in_specs=[ANY], out_specs=[ANY, ANY],
        out_shape=[jax.ShapeDtypeStruct((N_CHIPS, H, C), g.dtype)] * 2,
        scratch_shapes=[pltpu.SemaphoreType.DMA((N_CHIPS,))] * 3,
    )(g)


def _rs_chip_exchange(p):
    _, H, C = p.shape

    def body(p_ref, out_ref, ssem, rsem, lsem):
        x, y, c = _pos()
        k = 2 * x + y
        chips = _other_chips(x, y)
        loc = pltpu.make_async_copy(p_ref.at[k], out_ref.at[k], lsem)
        loc.start()
        cps = []
        for s, (cx, cy) in enumerate(chips):
            cp = _remote(p_ref.at[2 * cx + cy], out_ref.at[k], ssem.at[s], rsem.at[s], (cx, cy, c))
            cp.start()
            cps.append(cp)
        for s, (cx, cy) in enumerate(chips):
            land = out_ref.at[2 * cx + cy]
            _remote(land, land, ssem.at[s], rsem.at[s], (cx, cy, c)).wait_recv()
        for cp in cps:
            cp.wait_send()
        loc.wait()

    return pl.pallas_call(
        body, name="rs_chip_exchange", in_specs=[ANY], out_specs=ANY,
        out_shape=jax.ShapeDtypeStruct(p.shape, p.dtype),
        scratch_shapes=[pltpu.SemaphoreType.DMA((3,)), pltpu.SemaphoreType.DMA((3,)), pltpu.SemaphoreType.DMA],
    )(p)


def _rs_pair_share(f):
    H, C = f.shape

    def body(f_ref, out_ref, ssem, rsem, lsem):
        x, y, c = _pos()
        loc = pltpu.make_async_copy(f_ref, out_ref.at[c], lsem)
        loc.start()
        cp = _remote(f_ref, out_ref.at[c], ssem, rsem, (x, y, 1 - c))
        cp.start()
        land = out_ref.at[1 - c]
        _remote(land, land, ssem, rsem, (x, y, 1 - c)).wait_recv()
        cp.wait_send()
        loc.wait()

    return pl.pallas_call(
        body, name="rs_pair_share", in_specs=[ANY], out_specs=ANY,
        out_shape=jax.ShapeDtypeStruct((2, H, C), f.dtype),
        scratch_shapes=[pltpu.SemaphoreType.DMA, pltpu.SemaphoreType.DMA, pltpu.SemaphoreType.DMA],
    )(f)


def _all_reduce_small(s):
    r, C = s.shape

    def body(s_ref, o_ref, buf, ssem, rsem):
        x, y, c = _pos()
        me = 4 * x + 2 * y + c
        buf[me] = s_ref[...]
        cps = []
        for m in range(1, N_DEV):
            mx, my, mc = (m >> 2) & 1, (m >> 1) & 1, m & 1
            peer = (x ^ mx, y ^ my, c ^ mc)
            cp = _remote(s_ref, buf.at[me], ssem.at[m - 1], rsem.at[m - 1], peer)
            cp.start()
            cps.append(cp)
        for m in range(1, N_DEV):
            mx, my, mc = (m >> 2) & 1, (m >> 1) & 1, m & 1
            src = 4 * (x ^ mx) + 2 * (y ^ my) + (c ^ mc)
            _remote(s_ref, buf.at[src], ssem.at[m - 1], rsem.at[m - 1], (x ^ mx, y ^ my, c ^ mc)).wait_recv()
        for cp in cps:
            cp.wait_send()
        acc = buf[0]
        for j in range(1, N_DEV):
            acc = acc + buf[j]
        o_ref[...] = acc

    return pl.pallas_call(
        body, name="all_reduce_small", in_specs=[pl.BlockSpec(memory_space=pltpu.VMEM)],
        out_specs=pl.BlockSpec(memory_space=pltpu.VMEM), out_shape=jax.ShapeDtypeStruct((r, C), F32),
        scratch_shapes=[pltpu.VMEM((N_DEV, r, C), F32), pltpu.SemaphoreType.DMA((N_DEV - 1,)), pltpu.SemaphoreType.DMA((N_DEV - 1,))],
    )(s)


def _rtile(n, pref):
    if n <= pref:
        return n
    t = (pref // 16) * 16
    while t >= 16:
        if n % t == 0:
            return t
        t -= 16
    raise ValueError(f"no row tile for {n}")


def _reduce_scatter_grads(gpk):
    _, R, C = gpk.shape
    H = R // 2
    own, got = _rs_pair_exchange(gpk)
    tm = _rtile(N_CHIPS * H, 512)
    (part,) = _rows(lambda i, n, a, b: (a.astype(F32) + b.astype(F32),), N_CHIPS * H, tm,
                    [_cur(own.reshape(N_CHIPS * H, C)), _cur(got.reshape(N_CHIPS * H, C))], [], [_out(C, BF16)], [], "rs_pair_add")
    parts = _rs_chip_exchange(part.reshape(N_CHIPS, H, C)).reshape(N_CHIPS * H, C)
    tm = _rtile(H, 512)
    hb = H // tm

    def add4(i, n, a, b, c, d):
        return (((a.astype(F32) + b.astype(F32)) + c.astype(F32)) + d.astype(F32),)

    (red,) = _rows(add4, H, tm, [(parts, C, functools.partial(_const, v=0), j * hb) for j in range(N_CHIPS)], [], [_out(C, F32)], [],
                   "rs_chip_add")
    return _rs_pair_share(red).reshape(R, C)


def _adam(w, g, m, v, name):
    shp = w.shape
    two = lambda a: a.reshape(-1, shp[-1])
    rows = math.prod(shp[:-1])
    tm = _rtile(rows, 256)
    d, nm, nv = _rows(_k_adam, rows, tm, [_cur(two(w)), _cur(two(g)), _cur(two(m)), _cur(two(v))], [],
                      [_out(shp[-1], F32)] * 3, [], name)
    return d.reshape(shp), nm.reshape(shp), nv.reshape(shp)


def kernel(x, positions, norm_mix_pre, norm_mix_post, norm_ffn_pre, norm_ffn_post, w_in, mla_q_norm, mla_w_q_up, mla_kv_norm, mla_w_kv_up, sc_conv_w, ssd_conv_w, ssd_conv_b, ssd_dt_bias, ssd_a_log, ssd_d, ssd_norm, w_out, ffn_w_up, ffn_conv_w, ffn_conv_b, ffn_w_down, loss_target, m_norm_mix_pre, m_norm_mix_post, m_norm_ffn_pre, m_norm_ffn_post, m_w_in, m_mla_q_norm, m_mla_w_q_up, m_mla_kv_norm, m_mla_w_kv_up, m_sc_conv_w, m_ssd_conv_w, m_ssd_conv_b, m_ssd_dt_bias, m_ssd_a_log, m_ssd_d, m_ssd_norm, m_w_out, m_ffn_w_up, m_ffn_conv_w, m_ffn_conv_b, m_ffn_w_down, v_norm_mix_pre, v_norm_mix_post, v_norm_ffn_pre, v_norm_ffn_post, v_w_in, v_mla_q_norm, v_mla_w_q_up, v_mla_kv_norm, v_mla_w_kv_up, v_sc_conv_w, v_ssd_conv_w, v_ssd_conv_b, v_ssd_dt_bias, v_ssd_a_log, v_ssd_d, v_ssd_norm, v_w_out, v_ffn_w_up, v_ffn_conv_w, v_ffn_conv_b, v_ffn_w_down):
    a = dict(locals())
    sharded = [n for n, _ in SHARDED]
    shard_shapes = [a[n].shape for n in sharded]
    R = _pack_rows(shard_shapes)

    convs = [n for n in sharded if n.endswith("conv_w")]
    resid = [a[n] - a[n].astype(BF16).astype(F32) for n in convs]
    ag_shapes = shard_shapes + [a[n].shape for n in convs]
    RA = _pack_rows(ag_shapes)
    gathered = _all_gather_weights(_pack([a[n] for n in sharded] + resid, RA, BF16))
    per_chip = [_unpack(gathered[k], ag_shapes) for k in range(N_CHIPS)]
    W = {n: jnp.concatenate([per_chip[k][t] for k in range(N_CHIPS)], axis=ax) for t, (n, ax) in enumerate(SHARDED)}
    for t, n in enumerate(convs):
        lo = jnp.concatenate([per_chip[k][len(sharded) + t] for k in range(N_CHIPS)], axis=dict(SHARDED)[n])
        W[n] = W[n].astype(F32) + lo.astype(F32)
    S = {n: a[n] for n in SMALL}

    loss_part, gx, GW, GS = _local_step(a["x"][0], a["positions"][0], a["loss_target"][0], W, S)

    gpk = jnp.stack([_pack([jnp.split(GW[n], N_CHIPS, axis=ax)[k] for n, ax in SHARDED], R, BF16) for k in range(N_CHIPS)])
    grads = dict(zip(sharded, _unpack(_reduce_scatter_grads(gpk), shard_shapes)))

    small_shapes = [a[n].shape for n in SMALL]
    rs = _pack_rows(small_shapes + [(1,)])
    red = _unpack(_all_reduce_small(_pack([GS[n] for n in SMALL] + [loss_part.reshape(1)], rs, F32)), small_shapes + [(1,)])
    loss = red[-1][0]
    grads.update(zip(SMALL, red[:-1]))

    delta, new_m, new_v = {}, {}, {}
    for n in sharded:
        delta[n], new_m[n], new_v[n] = _adam(a[n], grads[n], a["m_" + n], a["v_" + n], "adamw_" + n)
    pk = lambda pre: _pack([a[pre + n] for n in SMALL], rs, F32)
    ds, ms, vs = _adam(pk(""), _pack([grads[n] for n in SMALL], rs, F32), pk("m_"), pk("v_"), "adamw_small")
    for dst, buf in ((delta, ds), (new_m, ms), (new_v, vs)):
        dst.update(zip(SMALL, _unpack(buf, small_shapes)))

    return (loss, gx[None], *[grads[n] for n in WEIGHTS], *[delta[n] for n in WEIGHTS], *[new_m[n] for n in WEIGHTS],
            *[new_v[n] for n in WEIGHTS])
```

```python
import functools
import math

import jax
import jax.numpy as jnp
from jax import lax
from jax.experimental import pallas as pl
from jax.experimental.pallas import tpu as pltpu

F32 = jnp.float32
BF16 = jnp.bfloat16
MXU_DTYPE = jnp.bfloat16
HIGHEST = lax.Precision.HIGHEST
MESH = pl.DeviceIdType.MESH

D_MODEL = 1024
DEPTH = 4
HEADS = 8
Q_LORA = 256
KV_LORA = 128
NOPE = 64
ROPE = 32
VDIM = 64
ROPE_THETA = 10000.0
SC_DIM = 256
SSD_HEADS = 4
SSD_HEAD_DIM = 64
SSD_STATE = 128
SSD_DIM = 256
SSD_CONV_DIM = 768
SSD_CHUNK = 128
FFN_DIM = 2816
NORM_EPS = 1e-6
QK_SCALE = (NOPE + ROPE) ** -0.5
LANE = 128
HP = 128
FLASH_HEADS = 2

ZIN = 2560
Z_CQ, Z_CKV, Z_KR, Z_SCB, Z_SCC, Z_SCH, Z_SSZ, Z_XBC, Z_DT = 0, 256, 384, 512, 768, 1024, 1280, 1536, 2304
KR_LANE = 64
YCAT = HEADS * HP + SC_DIM + SSD_DIM
FFN_TILE = 256

ADAM_LR, ADAM_B1, ADAM_B2, ADAM_EPS, ADAM_WD, ADAM_STEP = 0.001, 0.9, 0.999, 1e-08, 0.01, 10

PACK_COLS = 1024


def _tile(n, pref):
    if n <= pref:
        return n
    t = (pref // LANE) * LANE
    while t >= LANE:
        if n % t == 0:
            return t
        t -= LANE
    raise ValueError(f"no tile for {n}")


def _mm(a, b, mode, out_dtype, name, tm=512, tn=512, tkmax=1536, omap=None):
    omap = (lambda j: j) if omap is None else omap
    if mode == "nn":
        (M, K), (_, N) = a.shape, b.shape
    elif mode == "nt":
        (M, K), (N, _) = a.shape, b.shape
    else:
        (K, M), (_, N) = a.shape, b.shape
    tm, tn, tk = _tile(M, tm), _tile(N, tn), _tile(K, tkmax)
    nk = K // tk
    if mode == "nn":
        a_spec = pl.BlockSpec((tm, tk), lambda i, j, k: (i, k))
        b_spec = pl.BlockSpec((tk, tn), lambda i, j, k: (k, j))
        dims = (((1,), (0,)), ((), ()))
    elif mode == "nt":
        a_spec = pl.BlockSpec((tm, tk), lambda i, j, k: (i, k))
        b_spec = pl.BlockSpec((tn, tk), lambda i, j, k: (j, k))
        dims = (((1,), (1,)), ((), ()))
    else:
        a_spec = pl.BlockSpec((tk, tm), lambda i, j, k: (k, i))
        b_spec = pl.BlockSpec((tk, tn), lambda i, j, k: (k, j))
        dims = (((0,), (0,)), ((), ()))

    def body(a_ref, b_ref, o_ref, acc_ref):
        k = pl.program_id(2)

        @pl.when(k == 0)
        def _():
            acc_ref[...] = jnp.zeros_like(acc_ref)

        acc_ref[...] += lax.dot_general(a_ref[...].astype(MXU_DTYPE), b_ref[...].astype(MXU_DTYPE), dims,
                                        preferred_element_type=F32)

        @pl.when(k == nk - 1)
        def _():
            o_ref[...] = acc_ref[...].astype(o_ref.dtype)

    return pl.pallas_call(
        body, name=name, grid=(M // tm, N // tn, nk),
        in_specs=[a_spec, b_spec], out_specs=pl.BlockSpec((tm, tn), lambda i, j, k: (i, omap(j))),
        out_shape=jax.ShapeDtypeStruct((M, N), out_dtype),
        scratch_shapes=[pltpu.VMEM((tm, tn), F32)],
        compiler_params=pltpu.CompilerParams(dimension_semantics=("parallel", "parallel", "arbitrary")),
    )(a, b)


HALO = 8


def _const(j, v):
    return v


def _rows(fn, T, tm, ins, consts, outs, accs, name, ncol=1):
    n = T // tm
    hb = tm // HALO
    last = T // HALO - 1
    in_specs, args = [], []
    for arr, bc, cb, kind in ins:
        if isinstance(kind, int):
            in_specs.append(pl.BlockSpec((tm, bc), lambda j, i, cb=cb, off=kind: (i + off, cb(j))))
        elif kind == "cur":
            in_specs.append(pl.BlockSpec((tm, bc), lambda j, i, cb=cb: (i, cb(j))))
        elif kind == "prev":
            in_specs.append(pl.BlockSpec((HALO, bc), lambda j, i, cb=cb: (jnp.maximum(i * hb - 1, 0), cb(j))))
        else:
            in_specs.append(pl.BlockSpec((HALO, bc), lambda j, i, cb=cb: (jnp.minimum((i + 1) * hb, last), cb(j))))
        args.append(arr)
    for arr, bc, cb in consts:
        in_specs.append(pl.BlockSpec((arr.shape[0], bc), lambda j, i, cb=cb: (0, cb(j))))
        args.append(arr)
    out_specs, out_shape = [], []
    for tc, dt, bc, cb in outs:
        out_specs.append(pl.BlockSpec((tm, bc), lambda j, i, cb=cb: (i, cb(j))))
        out_shape.append(jax.ShapeDtypeStruct((T, tc), dt))
    for r, tc, bc, cb in accs:
        out_specs.append(pl.BlockSpec((r, bc), lambda j, i, cb=cb: (0, cb(j))))
        out_shape.append(jax.ShapeDtypeStruct((r, tc), F32))
    nin, nout, nacc = len(args), len(outs), len(accs)

    def body(*refs):
        i = pl.program_id(1)
        res = fn(i, n, *[r[...] for r in refs[:nin]])
        for r, v in zip(refs[nin:nin + nout], res[:nout]):
            r[...] = v.astype(r.dtype)
        if nacc:
            acc_refs = refs[nin + nout:nin + nout + nacc]

            @pl.when(i == 0)
            def _():
                for r in acc_refs:
                    r[...] = jnp.zeros_like(r)

            for r, v in zip(acc_refs, res[nout:]):
                r[...] += v.astype(F32)

    res = pl.pallas_call(
        body, name=name, grid=(ncol, n), in_specs=in_specs, out_specs=out_specs, out_shape=out_shape,
        compiler_params=pltpu.CompilerParams(dimension_semantics=("arbitrary", "arbitrary")),
    )(*args)
    return res


def _cur(arr, bc=None, blk=0):
    bc = arr.shape[1] if bc is None else bc
    return (arr, bc, functools.partial(_const, v=blk), "cur")


def _halo(arr, kind, bc=None, blk=0):
    bc = arr.shape[1] if bc is None else bc
    return (arr, bc, functools.partial(_const, v=blk), kind)


def _cst(arr):
    return (arr, arr.shape[1], functools.partial(_const, v=0))


def _out(cols, dt):
    return (cols, dt, cols, functools.partial(_const, v=0))


def _acc(rows, cols):
    return (rows, cols, cols, functools.partial(_const, v=0))


def _rms(x, w):
    return x * lax.rsqrt(jnp.mean(x * x, axis=-1, keepdims=True) + NORM_EPS) * w


def _silu(x):
    return x * (1.0 / (1.0 + jnp.exp(-x)))


def _dsilu(x):
    s = 1.0 / (1.0 + jnp.exp(-x))
    return s * (1.0 + x * (1.0 - s))


def _softplus(x):
    return jnp.maximum(x, 0.0) + jnp.log1p(jnp.exp(-jnp.abs(x)))


def _shift(a, k):
    return pltpu.roll(a, k % a.shape[0], 0)


def _lroll(a, k):
    return pltpu.roll(a, k % a.shape[1], 1)


def _vjp_wrap(f, nrow, nconst, add_first=False):
    def g(i, n, *vals):
        rows, consts, mid = vals[:nrow], vals[len(vals) - nconst:], vals[nrow:len(vals) - nconst]
        cots = mid[:-1] if add_first else mid
        outs, pull = jax.vjp(f, *rows, *consts)
        grads = list(pull(tuple(c.astype(o.dtype) for c, o in zip(cots, outs))))
        if add_first:
            grads[0] = grads[0] + mid[-1]
        return tuple(grads)
    return g


def _rows_vjp(f, T, tm, rows, consts, cots, out_dtypes, name):
    return _rows(_vjp_wrap(f, len(rows), len(consts)), T, tm, [_cur(r) for r in rows] + [_cur(c) for c in cots],
                 [_cst(c) for c in consts], [_out(r.shape[1], dt) for r, dt in zip(rows, out_dtypes)],
                 [_acc(1, c.shape[1]) for c in consts], name)


def _f_premix(x, g):
    return (_rms(x, g),)


def _f_mla_pre(cq, ckv, qn, kvn):
    return _rms(cq, qn), _rms(ckv, kvn)


def _f_ssd_gate(y, z, nw):
    return (_rms(y * _silu(z), nw),)


def _f_post_mix(x, mixed, gpost, gffn):
    x1 = x + _rms(mixed, gpost)
    return x1, _rms(x1, gffn)


def _f_post_ffn(x1, d, gpost):
    return (x1 + _rms(d, gpost),)


def _rope_fwd(v, cosf, sina, sinb):
    return v * cosf + _lroll(v, -16) * sina + _lroll(v, 16) * sinb


def _rope_bwd(g, cosf, sina, sinb):
    return g * cosf + _lroll(g * sina, 16) + _lroll(g * sinb, -16)


def _k_rope_fwd(i, n, qpad, kvpad, kr, cosf, sina, sinb):
    qs, ks = [], []
    krr = _rope_fwd(kr, cosf, sina, sinb)
    for h in range(HEADS):
        sl = slice(h * HP, (h + 1) * HP)
        qs.append(_rope_fwd(qpad[:, sl], cosf, sina, sinb))
        ks.append(kvpad[:, sl].astype(F32) + krr)
    return jnp.concatenate(qs, axis=1), jnp.concatenate(ks, axis=1)


def _k_rope_bwd(i, n, dq, dk, dv, cosf, sina, sinb):
    lane = lax.broadcasted_iota(jnp.int32, (1, HP), 1)
    rmask = ((lane >= KR_LANE) & (lane < KR_LANE + ROPE)).astype(F32)
    dqs, dks = [], []
    dkr = jnp.zeros((dq.shape[0], HP), F32)
    for h in range(HEADS):
        sl = slice(h * HP, (h + 1) * HP)
        dqs.append(_rope_bwd(dq[:, sl], cosf, sina, sinb))
        dkh = dk[:, sl]
        dkr = dkr + dkh * rmask
        dks.append(dkh * (1.0 - rmask))
    dkr = _rope_bwd(dkr, cosf, sina, sinb) * rmask
    return jnp.concatenate(dqs, axis=1), jnp.concatenate(dks + [dv], axis=1), dkr


def _k_sconv_fwd(i, n, b, c, h, cp, hp, w):
    m = b.shape[0]
    up = jnp.where(i > 0, cp * hp, 0.0)
    ue = jnp.concatenate([up, c * h], axis=0)
    conv = w[2:3] * ue + w[1:2] * _shift(ue, 1) + w[0:1] * _shift(ue, 2)
    return (b * conv[HALO:],)


def _k_sconv_bwd(i, n, b, c, h, dy, cp, hp, bn, dyn, w):
    m = b.shape[0]
    up = jnp.where(i > 0, cp * hp, 0.0)
    ue = jnp.concatenate([up, c * h], axis=0)
    u1, u2 = _shift(ue, 1), _shift(ue, 2)
    conv = (w[2:3] * ue + w[1:2] * u1 + w[0:1] * u2)[HALO:]
    dc_cur = dy * b
    dce = jnp.concatenate([dc_cur, jnp.where(i < n - 1, dyn * bn, 0.0)], axis=0)
    du = (w[2:3] * dce + w[1:2] * _shift(dce, -1) + w[0:1] * _shift(dce, -2))[:m]
    dw = jnp.concatenate([
        jnp.sum(dc_cur * u2[HALO:], axis=0, keepdims=True),
        jnp.sum(dc_cur * u1[HALO:], axis=0, keepdims=True),
        jnp.sum(dc_cur * ue[HALO:], axis=0, keepdims=True),
        jnp.zeros((HALO - 3, b.shape[1]), F32)], axis=0)
    return dy * conv, du * h, du * c, dw


def _conv4(ue, w):
    return w[3:4] * ue + w[2:3] * _shift(ue, 1) + w[1:2] * _shift(ue, 2) + w[0:1] * _shift(ue, 3)


def _k_ssdconv_fwd(i, n, u, up, w, bias):
    ue = jnp.concatenate([jnp.where(i > 0, up, 0.0), u], axis=0)
    return (_silu(_conv4(ue, w)[HALO:] + bias),)


def _k_ssdconv_bwd(i, n, u, dout, up, un, doutn, w, bias):
    m = u.shape[0]
    ue = jnp.concatenate([jnp.where(i > 0, up, 0.0), u, un], axis=0)
    u1, u2, u3 = _shift(ue, 1), _shift(ue, 2), _shift(ue, 3)
    pre = (w[3:4] * ue + w[2:3] * u1 + w[1:2] * u2 + w[0:1] * u3)[HALO:] + bias
    doe = jnp.concatenate([dout, jnp.where(i < n - 1, doutn, 0.0)], axis=0)
    dpre = doe * _dsilu(pre)
    du = (w[3:4] * dpre + w[2:3] * _shift(dpre, -1) + w[1:2] * _shift(dpre, -2) + w[0:1] * _shift(dpre, -3))[:m]
    dp = dpre[:m]
    cur = slice(HALO, HALO + m)
    dw = jnp.concatenate([
        jnp.sum(dp * u3[cur], axis=0, keepdims=True),
        jnp.sum(dp * u2[cur], axis=0, keepdims=True),
        jnp.sum(dp * u1[cur], axis=0, keepdims=True),
        jnp.sum(dp * ue[cur], axis=0, keepdims=True),
        jnp.zeros((HALO - 4, u.shape[1]), F32)], axis=0)
    db = jnp.sum(dp, axis=0, keepdims=True)
    return du, dw, db


def _conv3(ue, w):
    return w[2:3] * ue + w[1:2] * _shift(ue, 1) + w[0:1] * _shift(ue, 2)


def _k_ffnact_fwd(i, n, u, up, w, bias):
    ue = jnp.concatenate([jnp.where(i > 0, up, 0.0), u], axis=0)
    uc = _conv3(ue, w)[HALO:] + bias
    return (_silu(uc[:, :FFN_TILE]) * uc[:, FFN_TILE:],)


def _k_ffnact_bwd(i, n, u, dact, up, un, dactn, w, bias):
    m = u.shape[0]
    ue = jnp.concatenate([jnp.where(i > 0, up, 0.0), u, un], axis=0)
    u1, u2 = _shift(ue, 1), _shift(ue, 2)
    uc = (w[2:3] * ue + w[1:2] * u1 + w[0:1] * u2)[HALO:] + bias
    dae = jnp.concatenate([dact, jnp.where(i < n - 1, dactn, 0.0)], axis=0)
    gate, upv = uc[:, :FFN_TILE], uc[:, FFN_TILE:]
    duc = jnp.concatenate([dae * upv * _dsilu(gate), dae * _silu(gate)], axis=1)
    du = (w[2:3] * duc + w[1:2] * _shift(duc, -1) + w[0:1] * _shift(duc, -2))[:m]
    dp = duc[:m]
    cur = slice(HALO, HALO + m)
    dw = jnp.concatenate([
        jnp.sum(dp * u2[cur], axis=0, keepdims=True),
        jnp.sum(dp * u1[cur], axis=0, keepdims=True),
        jnp.sum(dp * ue[cur], axis=0, keepdims=True),
        jnp.zeros((HALO - 3, u.shape[1]), F32)], axis=0)
    db = jnp.sum(dp, axis=0, keepdims=True)
    return du, dw, db


def _k_loss(i, n, y, tgt):
    e = y - tgt
    part = 0.5 * jnp.sum(jnp.sum(e * e, axis=1, keepdims=True) / D_MODEL, axis=0, keepdims=True)
    return e * (1.0 / D_MODEL), jnp.broadcast_to(part, (1, LANE))


def _k_adam(i, n, w, g, m, v):
    m = ADAM_B1 * m + (1.0 - ADAM_B1) * g
    v = ADAM_B2 * v + (1.0 - ADAM_B2) * (g * g)
    m_hat = m / (1.0 - ADAM_B1 ** ADAM_STEP)
    v_hat = v / (1.0 - ADAM_B2 ** ADAM_STEP)
    delta = -ADAM_LR * (m_hat / (jnp.sqrt(v_hat) + ADAM_EPS) + ADAM_WD * w)
    return delta, m, v


def _dotf(a, b, dims):
    return lax.dot_general(a.astype(MXU_DTYPE), b.astype(MXU_DTYPE), dims, preferred_element_type=F32)


NN = (((1,), (0,)), ((), ()))
NT = (((1,), (1,)), ((), ()))
TN = (((0,), (0,)), ((), ()))


def _ssd_chunk(x0, x1, x2, x3, b0, b1, c0, c1, dtraw, p0, p1, p2, p3, dtb, alog, dsk):
    xs, bs, cs_, ps = (x0, x1, x2, x3), (b0, b1), (c0, c1), (p0, p1, p2, p3)
    L = dtraw.shape[0]
    dt = _softplus(dtraw + dtb)
    adt = dt * (-jnp.exp(alog))
    row = lax.broadcasted_iota(jnp.int32, (L, L), 0)
    col = lax.broadcasted_iota(jnp.int32, (L, L), 1)
    tril = row >= col
    cum = jnp.dot(tril.astype(F32), adt, precision=HIGHEST, preferred_element_type=F32)
    cum_t = cum.T
    lane = lax.broadcasted_iota(jnp.int32, (1, LANE), 1)
    sub = lax.broadcasted_iota(jnp.int32, (LANE, 1), 0)
    lastcol = (lax.broadcasted_iota(jnp.int32, (1, L), 1) == L - 1).astype(F32)
    ys, news = [], []
    for h in range(SSD_HEADS):
        g = h // (SSD_HEADS // 2)
        oh = (lane == h).astype(F32)
        dth = jnp.sum(dt * oh, axis=1, keepdims=True)
        csh = jnp.sum(cum * oh, axis=1, keepdims=True)
        csr = jnp.sum(cum_t * (sub == h).astype(F32), axis=0, keepdims=True)
        cl = jnp.sum(csr * lastcol, axis=1, keepdims=True)
        dskh = jnp.sum(dsk * oh, axis=1, keepdims=True)
        x, bm, cm, prev = xs[h], bs[g], cs_[g], ps[h]
        xdt = x * dth
        decay = jnp.exp(jnp.where(tril, csh - csr, -jnp.inf))
        scores = _dotf(cm, bm, NT) * decay
        y_diag = _dotf(scores, xdt, NN)
        bd = bm * jnp.exp(cl - csh)
        cst = _dotf(xdt, bd, TN)
        news.append(prev * jnp.exp(cl) + cst)
        y_off = _dotf(cm, prev, NT) * jnp.exp(csh)
        ys.append(y_diag + y_off + x * dskh)
    return (*ys, *news)


def _ssd_operands(x_ref, dt_ref, par_ref, prev):
    xs = [x_ref[:, h * SSD_HEAD_DIM:(h + 1) * SSD_HEAD_DIM] for h in range(SSD_HEADS)]
    bs = [x_ref[:, SSD_DIM + g * SSD_STATE:SSD_DIM + (g + 1) * SSD_STATE] for g in range(2)]
    cs_ = [x_ref[:, SSD_DIM + 2 * SSD_STATE + g * SSD_STATE:SSD_DIM + 2 * SSD_STATE + (g + 1) * SSD_STATE] for g in range(2)]
    return (*xs, *bs, *cs_, dt_ref[...], *prev, par_ref[0:1, :], par_ref[1:2, :], par_ref[2:3, :])


def _ssd_fwd(xbc, dtraw, par, T):
    L = SSD_CHUNK
    nc = T // L
    P = SSD_HEAD_DIM

    def body(x_ref, dt_ref, par_ref, y_ref, st_ref, state):
        @pl.when(pl.program_id(0) == 0)
        def _():
            state[...] = jnp.zeros_like(state)

        st_ref[0] = state[...]
        prev = [state[h * P:(h + 1) * P, :] for h in range(SSD_HEADS)]
        res = _ssd_chunk(*_ssd_operands(x_ref, dt_ref, par_ref, prev))
        for h in range(SSD_HEADS):
            y_ref[:, h * P:(h + 1) * P] = res[h]
            state[h * P:(h + 1) * P, :] = res[SSD_HEADS + h]

    return pl.pallas_call(
        body, name="ssd_scan_fwd", grid=(nc,),
        in_specs=[pl.BlockSpec((L, SSD_CONV_DIM), lambda c: (c, 0)), pl.BlockSpec((L, LANE), lambda c: (c, 0)),
                  pl.BlockSpec((8, LANE), lambda c: (0, 0))],
        out_specs=[pl.BlockSpec((L, SSD_DIM), lambda c: (c, 0)), pl.BlockSpec((1, SSD_DIM, SSD_STATE), lambda c: (c, 0, 0))],
        out_shape=[jax.ShapeDtypeStruct((T, SSD_DIM), F32), jax.ShapeDtypeStruct((nc, SSD_DIM, SSD_STATE), F32)],
        scratch_shapes=[pltpu.VMEM((SSD_DIM, SSD_STATE), F32)],
        compiler_params=pltpu.CompilerParams(dimension_semantics=("arbitrary",)),
    )(xbc, dtraw, par)


def _ssd_bwd(xbc, dtraw, par, states, dy, T):
    L = SSD_CHUNK
    nc = T // L
    P = SSD_HEAD_DIM

    def body(x_ref, dt_ref, par_ref, st_ref, dy_ref, dx_ref, ddt_ref, dpar_ref, dstate):
        @pl.when(pl.program_id(0) == 0)
        def _():
            dstate[...] = jnp.zeros_like(dstate)
            dpar_ref[...] = jnp.zeros_like(dpar_ref)

        prev = [st_ref[0, h * P:(h + 1) * P, :] for h in range(SSD_HEADS)]
        prim = _ssd_operands(x_ref, dt_ref, par_ref, prev)
        _, pull = jax.vjp(_ssd_chunk, *prim)
        cots = tuple(dy_ref[:, h * P:(h + 1) * P] for h in range(SSD_HEADS)) + tuple(
            dstate[h * P:(h + 1) * P, :] for h in range(SSD_HEADS))
        g = pull(cots)
        for h in range(SSD_HEADS):
            dx_ref[:, h * P:(h + 1) * P] = g[h]
            dstate[h * P:(h + 1) * P, :] = g[9 + h]
        for k in range(2):
            dx_ref[:, SSD_DIM + k * SSD_STATE:SSD_DIM + (k + 1) * SSD_STATE] = g[4 + k]
            dx_ref[:, SSD_DIM + 2 * SSD_STATE + k * SSD_STATE:SSD_DIM + 2 * SSD_STATE + (k + 1) * SSD_STATE] = g[6 + k]
        ddt_ref[...] = g[8]
        for r in range(3):
            dpar_ref[r:r + 1, :] += g[13 + r]

    rev = lambda c: (nc - 1 - c, 0)
    return pl.pallas_call(
        body, name="ssd_scan_bwd", grid=(nc,),
        in_specs=[pl.BlockSpec((L, SSD_CONV_DIM), rev), pl.BlockSpec((L, LANE), rev), pl.BlockSpec((8, LANE), lambda c: (0, 0)),
                  pl.BlockSpec((1, SSD_DIM, SSD_STATE), lambda c: (nc - 1 - c, 0, 0)), pl.BlockSpec((L, SSD_DIM), rev)],
        out_specs=[pl.BlockSpec((L, SSD_CONV_DIM), rev), pl.BlockSpec((L, LANE), rev), pl.BlockSpec((8, LANE), lambda c: (0, 0))],
        out_shape=[jax.ShapeDtypeStruct((T, SSD_CONV_DIM), F32), jax.ShapeDtypeStruct((T, LANE), F32),
                   jax.ShapeDtypeStruct((8, LANE), F32)],
        scratch_shapes=[pltpu.VMEM((SSD_DIM, SSD_STATE), F32)],
        compiler_params=pltpu.CompilerParams(dimension_semantics=("arbitrary",)),
    )(xbc, dtraw, par, states, dy)


def _flash_fwd(q, k, kv, T):
    tq = tk = min(512, T)
    nq = T // tq
    G = FLASH_HEADS
    rep = tk // HP

    def body(q_ref, k_ref, v_ref, o_ref, m_ref, l_ref, acc_ref):
        i, j = pl.program_id(1), pl.program_id(2)

        @pl.when(j == 0)
        def _():
            m_ref[...] = jnp.full_like(m_ref, -jnp.inf)
            l_ref[...] = jnp.zeros_like(l_ref)
            acc_ref[...] = jnp.zeros_like(acc_ref)

        def step(diagonal):
            for g in range(G):
                sl = slice(g * HP, (g + 1) * HP)
                s = _dotf(q_ref[:, sl], k_ref[:, sl], NT) * QK_SCALE
                if diagonal:
                    rows = lax.broadcasted_iota(jnp.int32, (tq, tk), 0)
                    cols = lax.broadcasted_iota(jnp.int32, (tq, tk), 1)
                    s = jnp.where(rows >= cols, s, -jnp.inf)
                m_old = m_ref[:, sl]
                m_new = jnp.maximum(m_old, jnp.max(s, axis=1, keepdims=True))
                p = jnp.exp(s - jnp.tile(m_new, (1, rep)))
                alpha = jnp.exp(m_old - m_new)
                l_ref[:, sl] = alpha * l_ref[:, sl] + jnp.sum(p, axis=1, keepdims=True)
                acc_ref[:, sl] = alpha * acc_ref[:, sl] + _dotf(p, v_ref[:, sl], NN)
                m_ref[:, sl] = m_new

        @pl.when(j < i)
        def _():
            step(False)

        @pl.when(j == i)
        def _():
            step(True)
            lane = lax.broadcasted_iota(jnp.int32, (tq, HP), 1)
            for g in range(G):
                sl = slice(g * HP, (g + 1) * HP)
                l = l_ref[:, sl]
                o_ref[:, sl] = jnp.where(lane < VDIM, acc_ref[:, sl] / l, m_ref[:, sl] + jnp.log(l))

    W = G * HP
    return pl.pallas_call(
        body, name="mla_flash_fwd", grid=(HEADS // G, nq, nq),
        in_specs=[pl.BlockSpec((tq, W), lambda h, i, j: (i, h)),
                  pl.BlockSpec((tk, W), lambda h, i, j: (jnp.minimum(j, i), h)),
                  pl.BlockSpec((tk, W), lambda h, i, j: (jnp.minimum(j, i), HEADS // G + h))],
        out_specs=pl.BlockSpec((tq, W), lambda h, i, j: (i, h)),
        out_shape=jax.ShapeDtypeStruct((T, HEADS * HP), F32),
        scratch_shapes=[pltpu.VMEM((tq, W), F32), pltpu.VMEM((tq, W), F32), pltpu.VMEM((tq, W), F32)],
        compiler_params=pltpu.CompilerParams(dimension_semantics=("parallel", "parallel", "arbitrary")),
    )(q, k, kv)


def _flash_bwd(q, k, kv, o, dycat, T):
    tq = tk = min(512, T)
    nq = T // tq
    G = FLASH_HEADS

    def body(q_ref, k_ref, v_ref, o_ref, do_ref, dq_ref, dk_ref, dv_ref):
        j, i = pl.program_id(1), pl.program_id(2)

        @pl.when((j == 0) & (i == 0))
        def _():
            dq_ref[...] = jnp.zeros_like(dq_ref)

        @pl.when(i == 0)
        def _():
            dk_ref[...] = jnp.zeros_like(dk_ref)
            dv_ref[...] = jnp.zeros_like(dv_ref)

        def step(diagonal):
            r0 = pl.multiple_of(i * tq, tq)
            for g in range(G):
                sl = slice(g * HP, (g + 1) * HP)
                qv, kv, vv, ov, dov = q_ref[:, sl], k_ref[:, sl], v_ref[:, sl], o_ref[:, sl], do_ref[:, sl]
                s = _dotf(qv, kv, NT) * QK_SCALE
                p = jnp.exp(s - ov[:, VDIM:VDIM + 1])
                if diagonal:
                    rows = lax.broadcasted_iota(jnp.int32, (tq, tk), 0)
                    cols = lax.broadcasted_iota(jnp.int32, (tq, tk), 1)
                    p = jnp.where(rows >= cols, p, 0.0)
                dsum = jnp.sum(dov * ov, axis=1, keepdims=True)
                dv_ref[:, sl] += _dotf(p, dov, TN)
                dp = _dotf(dov, vv, NT)
                ds = p * (dp - dsum) * QK_SCALE
                dk_ref[:, sl] += _dotf(ds, qv, TN)
                dq_ref[pl.ds(r0, tq), sl] += _dotf(ds, kv, NN)

        @pl.when(i > j)
        def _():
            step(False)

        @pl.when(i == j)
        def _():
            step(True)

    W = G * HP
    qmap = lambda h, j, i: (jnp.maximum(i, j), h)
    kmap = lambda h, j, i: (j, h)
    vmap = lambda h, j, i: (j, HEADS // G + h)
    return pl.pallas_call(
        body, name="mla_flash_bwd", grid=(HEADS // G, nq, nq),
        in_specs=[pl.BlockSpec((tq, W), qmap), pl.BlockSpec((tk, W), kmap), pl.BlockSpec((tk, W), vmap),
                  pl.BlockSpec((tq, W), qmap), pl.BlockSpec((tq, W), qmap)],
        out_specs=[pl.BlockSpec((T, W), lambda h, j, i: (0, h)), pl.BlockSpec((tk, W), kmap), pl.BlockSpec((tk, W), kmap)],
        out_shape=[jax.ShapeDtypeStruct((T, HEADS * HP), F32)] * 3,
        compiler_params=pltpu.CompilerParams(dimension_semantics=("parallel", "arbitrary", "arbitrary")),
    )(q, k, kv, o, dycat)


_IN_SRC = (0, 256, 384, 416, 672, 928, 1184, 1440, 2208, 2212)
_IN_DST = (Z_CQ, Z_CKV, Z_KR + KR_LANE, Z_SCB, Z_SCC, Z_SCH, Z_SSZ, Z_XBC, Z_DT)


def _pad_cols_in(w):
    parts, at = [], 0
    for s0, s1, d0 in zip(_IN_SRC[:-1], _IN_SRC[1:], _IN_DST):
        if d0 > at:
            parts.append(jnp.zeros(w.shape[:-1] + (d0 - at,), w.dtype))
        parts.append(w[..., s0:s1])
        at = d0 + (s1 - s0)
    parts.append(jnp.zeros(w.shape[:-1] + (ZIN - at,), w.dtype))
    return jnp.concatenate(parts, axis=-1)


def _unpad_cols_in(w):
    return jnp.concatenate([w[..., d0:d0 + (s1 - s0)] for s0, s1, d0 in zip(_IN_SRC[:-1], _IN_SRC[1:], _IN_DST)], axis=-1)


def _pad_heads(w, width):
    w = w.reshape(w.shape[:-1] + (HEADS, width))
    w = jnp.pad(w, [(0, 0)] * (w.ndim - 1) + [(0, HP - width)])
    return w.reshape(w.shape[:-2] + (HEADS * HP,))


def _unpad_heads(w, width):
    w = w.reshape(w.shape[:-1] + (HEADS, HP))[..., :width]
    return w.reshape(w.shape[:-2] + (HEADS * width,))


def _pad_kv(w):
    w = w.reshape(w.shape[:-1] + (HEADS, NOPE + VDIM))
    return jnp.concatenate([_pad_heads(w[..., :NOPE].reshape(w.shape[:-2] + (HEADS * NOPE,)), NOPE),
                            _pad_heads(w[..., NOPE:].reshape(w.shape[:-2] + (HEADS * VDIM,)), VDIM)], axis=-1)


def _unpad_kv(w):
    k = _unpad_heads(w[..., :HEADS * HP], NOPE).reshape(w.shape[:-1] + (HEADS, NOPE))
    v = _unpad_heads(w[..., HEADS * HP:], VDIM).reshape(w.shape[:-1] + (HEADS, VDIM))
    return jnp.concatenate([k, v], axis=-1).reshape(w.shape[:-1] + (HEADS * (NOPE + VDIM),))


def _pad_out_rows(w):
    att = jnp.swapaxes(_pad_heads(jnp.swapaxes(w[:HEADS * VDIM], 0, 1), VDIM), 0, 1)
    return jnp.concatenate([att, w[HEADS * VDIM:]], axis=0)


def _unpad_out_rows(w):
    att = jnp.swapaxes(_unpad_heads(jnp.swapaxes(w[:HEADS * HP], 0, 1), VDIM), 0, 1)
    return jnp.concatenate([att, w[HEADS * HP:]], axis=0)


def _interleave_ffn(w):
    nt = FFN_DIM // FFN_TILE
    w = w.reshape(w.shape[:-1] + (2, nt, FFN_TILE))
    return jnp.swapaxes(w, -3, -2).reshape(w.shape[:-3] + (2 * FFN_DIM,))


def _deinterleave_ffn(w):
    nt = FFN_DIM // FFN_TILE
    w = w.reshape(w.shape[:-1] + (nt, 2, FFN_TILE))
    return jnp.swapaxes(w, -3, -2).reshape(w.shape[:-3] + (2 * FFN_DIM,))


def _row8(*vecs):
    c = vecs[0].shape[-1]
    return jnp.concatenate([v.reshape(1, c).astype(F32) for v in vecs] + [jnp.zeros((8 - len(vecs), c), F32)], axis=0)


def _lanes(v):
    return jnp.pad(v.astype(F32), (0, LANE - v.shape[0])).reshape(1, LANE)


def _rope_tables(positions):
    inv_freq = 1.0 / (ROPE_THETA ** (jnp.arange(0, ROPE, 2, dtype=F32) / ROPE))
    ang = positions.astype(F32)[:, None] * inv_freq
    cos, sin = jnp.cos(ang), jnp.sin(ang)
    T = positions.shape[0]
    half = ROPE // 2
    one = jnp.ones((T, KR_LANE), F32)
    zero = jnp.zeros((T, KR_LANE), F32)
    tail1 = jnp.ones((T, HP - KR_LANE - ROPE), F32)
    tail0 = jnp.zeros((T, HP - KR_LANE - ROPE), F32)
    z16 = jnp.zeros((T, half), F32)
    cosf = jnp.concatenate([one, cos, cos, tail1], axis=1)
    sina = jnp.concatenate([zero, -sin, z16, tail0], axis=1)
    sinb = jnp.concatenate([zero, z16, sin, tail0], axis=1)
    return cosf, sina, sinb


def _layer_weights(W, l):
    c = lambda a: a.astype(MXU_DTYPE)
    return dict(
        w_in=c(_pad_cols_in(W["w_in"][l])),
        w_q=c(_pad_heads(W["mla_w_q_up"][l], NOPE + ROPE)),
        w_kv=c(_pad_kv(W["mla_w_kv_up"][l])),
        w_out=c(_pad_out_rows(W["w_out"][l])),
        w_up=c(_interleave_ffn(W["ffn_w_up"][l])),
        w_down=c(W["ffn_w_down"][l]),
        sc_w=_row8(*W["sc_conv_w"][l].astype(F32)),
        ssd_w=_row8(*W["ssd_conv_w"][l].astype(F32)),
        ffn_w=_interleave_ffn(_row8(*W["ffn_conv_w"][l].astype(F32))),
    )


def _local_step(x, positions, target, W, S):
    T = x.shape[0]
    tm = min(256, T)
    cosf, sina, sinb = _rope_tables(positions)
    saved = []
    xl = x
    for l in range(DEPTH):
        lw = _layer_weights(W, l)
        g_pre = S["norm_mix_pre"][l].reshape(1, -1)
        g_post = S["norm_mix_post"][l].reshape(1, -1)
        g_fpre = S["norm_ffn_pre"][l].reshape(1, -1)
        g_fpost = S["norm_ffn_post"][l].reshape(1, -1)
        qn = S["mla_q_norm"][l].reshape(1, -1)
        kvn = S["mla_kv_norm"][l].reshape(1, -1)
        ssd_b = S["ssd_conv_b"][l].reshape(1, -1)
        ssd_par = _row8(jnp.pad(S["ssd_dt_bias"][l], (0, LANE - SSD_HEADS)), jnp.pad(S["ssd_a_log"][l], (0, LANE - SSD_HEADS)),
                        jnp.pad(S["ssd_d"][l], (0, LANE - SSD_HEADS)))
        ssd_nw = S["ssd_norm"][l].reshape(1, -1)
        ffn_b = _interleave_ffn(S["ffn_conv_b"][l].reshape(1, -1))

        (h1,) = _rows(lambda i, n, *v: _f_premix(*v), T, tm, [_cur(xl)], [_cst(g_pre)], [_out(D_MODEL, BF16)], [], "pre_mix_norm")
        zin = _mm(h1, lw["w_in"], "nn", F32, "mm_in")
        qlat, kvlat = _rows(lambda i, n, *v: _f_mla_pre(*v), T, tm, [_cur(zin, Q_LORA, 0), _cur(zin, KV_LORA, Z_CKV // KV_LORA)],
                            [_cst(qn), _cst(kvn)], [_out(Q_LORA, BF16), _out(KV_LORA, BF16)], [], "mla_pre_norm")
        qpad = _mm(qlat, lw["w_q"], "nn", F32, "mm_q_up")
        kvpad = _mm(kvlat, lw["w_kv"], "nn", BF16, "mm_kv_up")
        qr, kr = _rows(_k_rope_fwd, T, tm, [_cur(qpad), _cur(kvpad, HEADS * HP, 0), _cur(zin, LANE, Z_KR // LANE),
                                            _cur(cosf), _cur(sina), _cur(sinb)], [],
                       [_out(HEADS * HP, BF16), _out(HEADS * HP, BF16)], [], "mla_rope")
        o = _flash_fwd(qr, kr, kvpad, T)
        (yconv,) = _rows(_k_sconv_fwd, T, tm, [_cur(zin, SC_DIM, Z_SCB // SC_DIM), _cur(zin, SC_DIM, Z_SCC // SC_DIM),
                                               _cur(zin, SC_DIM, Z_SCH // SC_DIM), _halo(zin, "prev", SC_DIM, Z_SCC // SC_DIM),
                                               _halo(zin, "prev", SC_DIM, Z_SCH // SC_DIM)], [_cst(lw["sc_w"])],
                         [_out(SC_DIM, F32)], [], "short_conv_fwd")
        (xbc,) = _rows(_k_ssdconv_fwd, T, tm, [_cur(zin, SSD_CONV_DIM, Z_XBC // SSD_CONV_DIM),
                                               _halo(zin, "prev", SSD_CONV_DIM, Z_XBC // SSD_CONV_DIM)],
                       [_cst(lw["ssd_w"]), _cst(ssd_b)], [_out(SSD_CONV_DIM, F32)], [], "ssd_conv_fwd")
        dtraw = zin[:, Z_DT:Z_DT + LANE]
        yscan, states = _ssd_fwd(xbc, dtraw, ssd_par, T)
        (yssd,) = _rows(lambda i, n, *v: _f_ssd_gate(*v), T, tm, [_cur(yscan), _cur(zin, SSD_DIM, Z_SSZ // SSD_DIM)], [_cst(ssd_nw)],
                        [_out(SSD_DIM, F32)], [], "ssd_gate_fwd")
        ycat = jnp.concatenate([o.astype(BF16), yconv.astype(BF16), yssd.astype(BF16)], axis=1)
        mixed = _mm(ycat, lw["w_out"], "nn", F32, "mm_out")
        x1, h2 = _rows(lambda i, n, *v: _f_post_mix(*v), T, tm, [_cur(xl), _cur(mixed)], [_cst(g_post), _cst(g_fpre)],
                       [_out(D_MODEL, F32), _out(D_MODEL, BF16)], [], "post_mix_fwd")
        upre = _mm(h2, lw["w_up"], "nn", F32, "mm_up")
        nt = FFN_DIM // FFN_TILE
        ident = lambda j: j
        (act,) = _rows(_k_ffnact_fwd, T, tm, [(upre, 2 * FFN_TILE, ident, "cur"), (upre, 2 * FFN_TILE, ident, "prev")],
                       [(lw["ffn_w"], 2 * FFN_TILE, ident), (ffn_b, 2 * FFN_TILE, ident)],
                       [(FFN_DIM, BF16, FFN_TILE, ident)], [], "ffn_act_fwd", ncol=nt)
        dn = _mm(act, lw["w_down"], "nn", F32, "mm_down")
        (x2,) = _rows(lambda i, n, *v: _f_post_ffn(*v), T, tm, [_cur(x1), _cur(dn)], [_cst(g_fpost)], [_out(D_MODEL, F32)], [], "post_ffn_fwd")
        saved.append(dict(lw=lw, x=xl, h1=h1, zin=zin, qlat=qlat, kvlat=kvlat, qr=qr, kr=kr, kvpad=kvpad, o=o, xbc=xbc, dtraw=dtraw,
                          yscan=yscan, states=states, ycat=ycat, mixed=mixed, x1=x1, h2=h2, upre=upre, act=act, dn=dn,
                          g_pre=g_pre, g_post=g_post, g_fpre=g_fpre, g_fpost=g_fpost, qn=qn, kvn=kvn, ssd_b=ssd_b,
                          ssd_par=ssd_par, ssd_nw=ssd_nw, ffn_b=ffn_b))
        xl = x2

    gx, loss_part = _rows(_k_loss, T, tm, [_cur(xl), _cur(target)], [], [_out(D_MODEL, F32)], [_acc(1, LANE)], "loss_head")

    GW = {k: [None] * DEPTH for k in ("w_in", "mla_w_q_up", "mla_w_kv_up", "sc_conv_w", "ssd_conv_w", "w_out", "ffn_w_up",
                                      "ffn_conv_w", "ffn_w_down")}
    GS = {k: [None] * DEPTH for k in ("norm_mix_pre", "norm_mix_post", "norm_ffn_pre", "norm_ffn_post", "mla_q_norm", "mla_kv_norm",
                                      "ssd_conv_b", "ssd_dt_bias", "ssd_a_log", "ssd_d", "ssd_norm", "ffn_conv_b")}
    ident = lambda j: j
    nt = FFN_DIM // FFN_TILE
    for l in reversed(range(DEPTH)):
        s = saved[l]
        lw = s["lw"]
        gx1, ddn, dgf = _rows_vjp(_f_post_ffn, T, tm, [s["x1"], s["dn"]], [s["g_fpost"]], [gx], [F32, BF16], "post_ffn_bwd")
        GS["norm_ffn_post"][l] = dgf[0]
        dact = _mm(ddn, lw["w_down"], "nt", F32, "mm_down_dx")
        GW["ffn_w_down"][l] = _mm(s["act"], ddn, "tn", BF16, "mm_down_dw")
        du, dfw, dfb = _rows(_k_ffnact_bwd, T, tm,
                             [(s["upre"], 2 * FFN_TILE, ident, "cur"), (dact, FFN_TILE, ident, "cur"),
                              (s["upre"], 2 * FFN_TILE, ident, "prev"), (s["upre"], 2 * FFN_TILE, ident, "next"),
                              (dact, FFN_TILE, ident, "next")],
                             [(lw["ffn_w"], 2 * FFN_TILE, ident), (s["ffn_b"], 2 * FFN_TILE, ident)],
                             [(2 * FFN_DIM, BF16, 2 * FFN_TILE, ident)],
                             [(HALO, 2 * FFN_DIM, 2 * FFN_TILE, ident), (1, 2 * FFN_DIM, 2 * FFN_TILE, ident)], "ffn_act_bwd", ncol=nt)
        GW["ffn_conv_w"][l] = _deinterleave_ffn(dfw[:3])
        GS["ffn_conv_b"][l] = _deinterleave_ffn(dfb[0])
        dh2 = _mm(du, lw["w_up"], "nt", F32, "mm_up_dx")
        GW["ffn_w_up"][l] = _mm(s["h2"], du, "tn", BF16, "mm_up_dw", tn=FFN_TILE, omap=lambda j: (j & 1) * nt + (j >> 1))
        gx0, dmixed, dgp, dgf = _rows_vjp(_f_post_mix, T, tm, [s["x"], s["mixed"]], [s["g_post"], s["g_fpre"]], [gx1, dh2],
                                          [F32, BF16], "post_mix_bwd")
        GS["norm_mix_post"][l], GS["norm_ffn_pre"][l] = dgp[0], dgf[0]
        dycat = _mm(dmixed, lw["w_out"], "nt", F32, "mm_out_dx")
        GW["w_out"][l] = _unpad_out_rows(_mm(s["ycat"], dmixed, "tn", BF16, "mm_out_dw"))
        zin = s["zin"]
        dyscan, dz, dnw = _rows(_vjp_wrap(_f_ssd_gate, 2, 1), T, tm,
                                [_cur(s["yscan"]), _cur(zin, SSD_DIM, Z_SSZ // SSD_DIM), _cur(dycat, SSD_DIM, (HEADS * HP + SC_DIM) // SSD_DIM)],
                                [_cst(s["ssd_nw"])], [_out(SSD_DIM, F32), _out(SSD_DIM, BF16)], [_acc(1, SSD_DIM)], "ssd_gate_bwd")
        GS["ssd_norm"][l] = dnw[0]
        dxbc, ddtraw, dpar = _ssd_bwd(s["xbc"], s["dtraw"], s["ssd_par"], s["states"], dyscan, T)
        GS["ssd_dt_bias"][l], GS["ssd_a_log"][l], GS["ssd_d"][l] = dpar[0, :SSD_HEADS], dpar[1, :SSD_HEADS], dpar[2, :SSD_HEADS]
        xb = Z_XBC // SSD_CONV_DIM
        dxraw, dsw, dsb = _rows(_k_ssdconv_bwd, T, tm,
                                [_cur(zin, SSD_CONV_DIM, xb), _cur(dxbc), _halo(zin, "prev", SSD_CONV_DIM, xb),
                                 _halo(zin, "next", SSD_CONV_DIM, xb), _halo(dxbc, "next")],
                                [_cst(lw["ssd_w"]), _cst(s["ssd_b"])], [_out(SSD_CONV_DIM, BF16)],
                                [_acc(HALO, SSD_CONV_DIM), _acc(1, SSD_CONV_DIM)], "ssd_conv_bwd")
        GW["ssd_conv_w"][l] = dsw[:4]
        GS["ssd_conv_b"][l] = dsb[0]
        cb = (HEADS * HP) // SC_DIM
        dscb, dscc, dsch, dscw = _rows(_k_sconv_bwd, T, tm,
                                       [_cur(zin, SC_DIM, Z_SCB // SC_DIM), _cur(zin, SC_DIM, Z_SCC // SC_DIM),
                                        _cur(zin, SC_DIM, Z_SCH // SC_DIM), _cur(dycat, SC_DIM, cb),
                                        _halo(zin, "prev", SC_DIM, Z_SCC // SC_DIM), _halo(zin, "prev", SC_DIM, Z_SCH // SC_DIM),
                                        _halo(zin, "next", SC_DIM, Z_SCB // SC_DIM), _halo(dycat, "next", SC_DIM, cb)],
                                       [_cst(lw["sc_w"])], [_out(SC_DIM, BF16)] * 3, [_acc(HALO, SC_DIM)], "short_conv_bwd")
        GW["sc_conv_w"][l] = dscw[:3]
        dq, dk, dv = _flash_bwd(s["qr"], s["kr"], s["kvpad"], s["o"], dycat, T)
        dqpad, dkvpad, dkr = _rows(_k_rope_bwd, T, tm, [_cur(dq), _cur(dk), _cur(dv), _cur(cosf), _cur(sina), _cur(sinb)], [],
                                   [_out(HEADS * HP, BF16), _out(2 * HEADS * HP, BF16), _out(LANE, BF16)], [], "mla_rope_bwd")
        dqlat = _mm(dqpad, lw["w_q"], "nt", F32, "mm_q_dx")
        GW["mla_w_q_up"][l] = _unpad_heads(_mm(s["qlat"], dqpad, "tn", BF16, "mm_q_dw"), NOPE + ROPE)
        dkvlat = _mm(dkvpad, lw["w_kv"], "nt", F32, "mm_kv_dx")
        GW["mla_w_kv_up"][l] = _unpad_kv(_mm(s["kvlat"], dkvpad, "tn", BF16, "mm_kv_dw"))
        dcq, dckv, dqn, dkvn = _rows(_vjp_wrap(_f_mla_pre, 2, 2), T, tm,
                                     [_cur(zin, Q_LORA, 0), _cur(zin, KV_LORA, Z_CKV // KV_LORA), _cur(dqlat), _cur(dkvlat)],
                                     [_cst(s["qn"]), _cst(s["kvn"])], [_out(Q_LORA, BF16), _out(KV_LORA, BF16)],
                                     [_acc(1, Q_LORA), _acc(1, KV_LORA)], "mla_pre_bwd")
        GS["mla_q_norm"][l], GS["mla_kv_norm"][l] = dqn[0], dkvn[0]
        dzin = jnp.concatenate([dcq, dckv, dkr, dscb, dscc, dsch, dz, dxraw, ddtraw.astype(BF16), jnp.zeros((T, ZIN - Z_DT - LANE), BF16)], axis=1)
        dh1 = _mm(dzin, lw["w_in"], "nt", F32, "mm_in_dx")
        GW["w_in"][l] = _unpad_cols_in(_mm(s["h1"], dzin, "tn", BF16, "mm_in_dw"))
        gx, dgp = _rows(_vjp_wrap(_f_premix, 1, 1, add_first=True), T, tm, [_cur(s["x"]), _cur(dh1), _cur(gx0)], [_cst(s["g_pre"])],
                        [_out(D_MODEL, F32)], [_acc(1, D_MODEL)], "pre_mix_bwd")
        GS["norm_mix_pre"][l] = dgp[0]
    GS = {k: jnp.stack(v) for k, v in GS.items()}
    return loss_part[0, 0], gx, GW, GS


WEIGHTS = ("norm_mix_pre", "norm_mix_post", "norm_ffn_pre", "norm_ffn_post", "w_in", "mla_q_norm", "mla_w_q_up", "mla_kv_norm",
           "mla_w_kv_up", "sc_conv_w", "ssd_conv_w", "ssd_conv_b", "ssd_dt_bias", "ssd_a_log", "ssd_d", "ssd_norm", "w_out",
           "ffn_w_up", "ffn_conv_w", "ffn_conv_b", "ffn_w_down")
SHARDED = (("w_in", 2), ("mla_w_q_up", 2), ("mla_w_kv_up", 2), ("sc_conv_w", 2), ("ssd_conv_w", 2), ("w_out", 1),
           ("ffn_w_up", 2), ("ffn_conv_w", 2), ("ffn_w_down", 1))
SMALL = tuple(n for n in WEIGHTS if n not in dict(SHARDED))
N_CHIPS = 4
N_DEV = 8
ROW_ALIGN = 64


def _pack(parts, rows, dtype):
    flat = jnp.concatenate([p.astype(dtype).reshape(-1) for p in parts])
    return jnp.pad(flat, (0, rows * PACK_COLS - flat.shape[0])).reshape(rows, PACK_COLS)


def _unpack(buf, shapes):
    flat, out, off = buf.reshape(-1), [], 0
    for shp in shapes:
        n = math.prod(shp)
        out.append(flat[off:off + n].reshape(shp))
        off += n
    return out


def _pack_rows(shapes):
    n = sum(math.prod(s) for s in shapes)
    return -(-n // (PACK_COLS * ROW_ALIGN)) * ROW_ALIGN


ANY = pl.BlockSpec(memory_space=pl.ANY)


def _pos():
    return lax.axis_index("x"), lax.axis_index("y"), lax.axis_index("c")


def _other_chips(x, y):
    return ((1 - x, y), (x, 1 - y), (1 - x, 1 - y))


def _remote(src, dst, ssem, rsem, dev):
    return pltpu.make_async_remote_copy(src_ref=src, dst_ref=dst, send_sem=ssem, recv_sem=rsem, device_id=dev, device_id_type=MESH)


AG_CHUNKS = 2


def _all_gather_weights(wpk):
    R, C = wpk.shape
    H = R // 2
    CH = H // AG_CHUNKS
    n = 3 * AG_CHUNKS

    def body(w_ref, out_ref, isend, irecv, dsend, drecv):
        x, y, c = _pos()
        k = 2 * x + y
        sib = (x, y, 1 - c)
        chips = _other_chips(x, y)

        def rows(kk, half, ch):
            return out_ref.at[kk, pl.ds(half * H + ch * CH, CH), :]

        first = []
        for p, (cx, cy) in enumerate(chips):
            for ch in range(AG_CHUNKS):
                s = p * AG_CHUNKS + ch
                cp = _remote(w_ref.at[pl.ds(c * H + ch * CH, CH), :], rows(k, c, ch), isend.at[s], irecv.at[s], (cx, cy, c))
                cp.start()
                first.append(cp)
        passed = []
        for p, (cx, cy) in enumerate(chips):
            for ch in range(AG_CHUNKS):
                s = p * AG_CHUNKS + ch
                land = rows(2 * cx + cy, c, ch)
                _remote(land, land, isend.at[s], irecv.at[s], (cx, cy, c)).wait_recv()
                fw = _remote(land, land, dsend.at[s], drecv.at[s], sib)
                fw.start()
                passed.append(fw)
        for p, (cx, cy) in enumerate(chips):
            for ch in range(AG_CHUNKS):
                s = p * AG_CHUNKS + ch
                land = rows(2 * cx + cy, 1 - c, ch)
                _remote(land, land, dsend.at[s], drecv.at[s], sib).wait_recv()
        for cp in first + passed:
            cp.wait_send()

    got = pl.pallas_call(
        body, name="all_gather_weights", in_specs=[ANY], out_specs=ANY,
        out_shape=jax.ShapeDtypeStruct((N_CHIPS, R, C), wpk.dtype),
        scratch_shapes=[pltpu.SemaphoreType.DMA((n,))] * 4,
    )(wpk)
    return lax.dynamic_update_slice(got, wpk[None], (2 * lax.axis_index("x") + lax.axis_index("y"), 0, 0))


def _rs_pair_exchange(g):
    _, R, C = g.shape
    H = R // 2

    def body(g_ref, got_ref, ssem, rsem):
        x, y, c = _pos()
        sib = (x, y, 1 - c)
        cps = []
        for kk in range(N_CHIPS):
            cp = _remote(g_ref.at[kk, pl.ds((1 - c) * H, H), :], got_ref.at[kk], ssem.at[kk], rsem.at[kk], sib)
            cp.start()
            cps.append(cp)
        for cp in cps:
            cp.wait()

    return pl.pallas_call(
        body, name="rs_pair_exchange", in_specs=[ANY], out_specs=ANY,
        out_shape=jax.ShapeDtypeStruct((N_CHIPS, H, C), g.dtype),
        scratch_shapes=[pltpu.SemaphoreType.DMA((N_CHIPS,))] * 2,
    )(g)


def _rs_chip_exchange(p):
    _, H, C = p.shape

    def body(p_ref, out_ref, ssem, rsem):
        x, y, c = _pos()
        k = 2 * x + y
        chips = _other_chips(x, y)
        cps = []
        for s, (cx, cy) in enumerate(chips):
            cp = _remote(p_ref.at[2 * cx + cy], out_ref.at[k], ssem.at[s], rsem.at[s], (cx, cy, c))
            cp.start()
            cps.append(cp)
        for s, (cx, cy) in enumerate(chips):
            land = out_ref.at[2 * cx + cy]
            _remote(land, land, ssem.at[s], rsem.at[s], (cx, cy, c)).wait_recv()
        for cp in cps:
            cp.wait_send()

    k = 2 * lax.axis_index("x") + lax.axis_index("y")
    got = pl.pallas_call(
        body, name="rs_chip_exchange", in_specs=[ANY], out_specs=ANY,
        out_shape=jax.ShapeDtypeStruct(p.shape, p.dtype),
        scratch_shapes=[pltpu.SemaphoreType.DMA((3,)), pltpu.SemaphoreType.DMA((3,))],
    )(p)
    return lax.dynamic_update_slice(got, lax.dynamic_slice_in_dim(p, k, 1, axis=0), (k, 0, 0))


def _rs_pair_share(f):
    H, C = f.shape

    def body(f_ref, out_ref, ssem, rsem):
        x, y, c = _pos()
        cp = _remote(f_ref, out_ref.at[c], ssem, rsem, (x, y, 1 - c))
        cp.start()
        land = out_ref.at[1 - c]
        _remote(land, land, ssem, rsem, (x, y, 1 - c)).wait_recv()
        cp.wait_send()

    got = pl.pallas_call(
        body, name="rs_pair_share", in_specs=[ANY], out_specs=ANY,
        out_shape=jax.ShapeDtypeStruct((2, H, C), f.dtype),
        scratch_shapes=[pltpu.SemaphoreType.DMA, pltpu.SemaphoreType.DMA],
    )(f)
    return lax.dynamic_update_slice(got, f[None], (lax.axis_index("c"), 0, 0))


def _all_reduce_small(s):
    r, C = s.shape

    def body(s_ref, o_ref, buf, ssem, rsem):
        x, y, c = _pos()
        me = 4 * x + 2 * y + c
        buf[me] = s_ref[...]
        cps = []
        for m in range(1, N_DEV):
            mx, my, mc = (m >> 2) & 1, (m >> 1) & 1, m & 1
            peer = (x ^ mx, y ^ my, c ^ mc)
            cp = _remote(s_ref, buf.at[me], ssem.at[m - 1], rsem.at[m - 1], peer)
            cp.start()
            cps.append(cp)
        for m in range(1, N_DEV):
            mx, my, mc = (m >> 2) & 1, (m >> 1) & 1, m & 1
            src = 4 * (x ^ mx) + 2 * (y ^ my) + (c ^ mc)
            _remote(s_ref, buf.at[src], ssem.at[m - 1], rsem.at[m - 1], (x ^ mx, y ^ my, c ^ mc)).wait_recv()
        for cp in cps:
            cp.wait_send()
        acc = buf[0]
        for j in range(1, N_DEV):
            acc = acc + buf[j]
        o_ref[...] = acc

    return pl.pallas_call(
        body, name="all_reduce_small", in_specs=[pl.BlockSpec(memory_space=pltpu.VMEM)],
        out_specs=pl.BlockSpec(memory_space=pltpu.VMEM), out_shape=jax.ShapeDtypeStruct((r, C), F32),
        scratch_shapes=[pltpu.VMEM((N_DEV, r, C), F32), pltpu.SemaphoreType.DMA((N_DEV - 1,)), pltpu.SemaphoreType.DMA((N_DEV - 1,))],
    )(s)


def _rtile(n, pref):
    if n <= pref:
        return n
    t = (pref // 16) * 16
    while t >= 16:
        if n % t == 0:
            return t
        t -= 16
    raise ValueError(f"no row tile for {n}")


def _reduce_scatter_grads(gpk):
    _, R, C = gpk.shape
    H = R // 2
    got = _rs_pair_exchange(gpk)
    own = lax.dynamic_index_in_dim(gpk.reshape(N_CHIPS, 2, H, C), lax.axis_index("c"), axis=1, keepdims=False)
    tm = _rtile(N_CHIPS * H, 512)
    (part,) = _rows(lambda i, n, a, b: (a.astype(F32) + b.astype(F32),), N_CHIPS * H, tm,
                    [_cur(own.reshape(N_CHIPS * H, C)), _cur(got.reshape(N_CHIPS * H, C))], [], [_out(C, BF16)], [], "rs_pair_add")
    parts = _rs_chip_exchange(part.reshape(N_CHIPS, H, C)).reshape(N_CHIPS * H, C)
    tm = _rtile(H, 1024)
    hb = H // tm

    def add4(i, n, a, b, c, d):
        return (((a.astype(F32) + b.astype(F32)) + c.astype(F32)) + d.astype(F32),)

    (red,) = _rows(add4, H, tm, [(parts, C, functools.partial(_const, v=0), j * hb) for j in range(N_CHIPS)], [], [_out(C, F32)], [],
                   "rs_chip_add")
    return _rs_pair_share(red).reshape(R, C)


def _adam(w, g, m, v, name):
    shp = w.shape
    two = lambda a: a.reshape(-1, shp[-1])
    rows = math.prod(shp[:-1])
    tm = _rtile(rows, 256)
    d, nm, nv = _rows(_k_adam, rows, tm, [_cur(two(w)), _cur(two(g)), _cur(two(m)), _cur(two(v))], [],
                      [_out(shp[-1], F32)] * 3, [], name)
    return d.reshape(shp), nm.reshape(shp), nv.reshape(shp)


def kernel(x, positions, norm_mix_pre, norm_mix_post, norm_ffn_pre, norm_ffn_post, w_in, mla_q_norm, mla_w_q_up, mla_kv_norm, mla_w_kv_up, sc_conv_w, ssd_conv_w, ssd_conv_b, ssd_dt_bias, ssd_a_log, ssd_d, ssd_norm, w_out, ffn_w_up, ffn_conv_w, ffn_conv_b, ffn_w_down, loss_target, m_norm_mix_pre, m_norm_mix_post, m_norm_ffn_pre, m_norm_ffn_post, m_w_in, m_mla_q_norm, m_mla_w_q_up, m_mla_kv_norm, m_mla_w_kv_up, m_sc_conv_w, m_ssd_conv_w, m_ssd_conv_b, m_ssd_dt_bias, m_ssd_a_log, m_ssd_d, m_ssd_norm, m_w_out, m_ffn_w_up, m_ffn_conv_w, m_ffn_conv_b, m_ffn_w_down, v_norm_mix_pre, v_norm_mix_post, v_norm_ffn_pre, v_norm_ffn_post, v_w_in, v_mla_q_norm, v_mla_w_q_up, v_mla_kv_norm, v_mla_w_kv_up, v_sc_conv_w, v_ssd_conv_w, v_ssd_conv_b, v_ssd_dt_bias, v_ssd_a_log, v_ssd_d, v_ssd_norm, v_w_out, v_ffn_w_up, v_ffn_conv_w, v_ffn_conv_b, v_ffn_w_down):
    a = dict(locals())
    sharded = [n for n, _ in SHARDED]
    shard_shapes = [a[n].shape for n in sharded]
    R = _pack_rows(shard_shapes)

    convs = [n for n in sharded if n.endswith("conv_w")]
    resid = [a[n] - a[n].astype(BF16).astype(F32) for n in convs]
    ag_shapes = shard_shapes + [a[n].shape for n in convs]
    RA = _pack_rows(ag_shapes)
    gathered = _all_gather_weights(_pack([a[n] for n in sharded] + resid, RA, BF16))
    per_chip = [_unpack(gathered[k], ag_shapes) for k in range(N_CHIPS)]
    W = {n: jnp.concatenate([per_chip[k][t] for k in range(N_CHIPS)], axis=ax) for t, (n, ax) in enumerate(SHARDED)}
    for t, n in enumerate(convs):
        lo = jnp.concatenate([per_chip[k][len(sharded) + t] for k in range(N_CHIPS)], axis=dict(SHARDED)[n])
        W[n] = W[n].astype(F32) + lo.astype(F32)
    S = {n: a[n] for n in SMALL}

    loss_part, gx, GW, GS = _local_step(a["x"][0], a["positions"][0], a["loss_target"][0], W, S)

    def chip_part(g, ax, k):
        size = g.shape[ax - 1] // N_CHIPS
        return lax.slice_in_dim(g, k * size, (k + 1) * size, axis=ax - 1)

    gpk = jnp.stack([_pack([chip_part(GW[n][l], ax, k) for n, ax in SHARDED for l in range(DEPTH)], R, BF16)
                     for k in range(N_CHIPS)])
    grads = dict(zip(sharded, _unpack(_reduce_scatter_grads(gpk), shard_shapes)))

    small_shapes = [a[n].shape for n in SMALL]
    rs = _pack_rows(small_shapes + [(1,)])
    red = _unpack(_all_reduce_small(_pack([GS[n] for n in SMALL] + [loss_part.reshape(1)], rs, F32)), small_shapes + [(1,)])
    loss = red[-1][0]
    grads.update(zip(SMALL, red[:-1]))

    delta, new_m, new_v = {}, {}, {}
    for n in sharded:
        delta[n], new_m[n], new_v[n] = _adam(a[n], grads[n], a["m_" + n], a["v_" + n], "adamw_" + n)
    pk = lambda pre: _pack([a[pre + n] for n in SMALL], rs, F32)
    ds, ms, vs = _adam(pk(""), _pack([grads[n] for n in SMALL], rs, F32), pk("m_"), pk("v_"), "adamw_small")
    for dst, buf in ((delta, ds), (new_m, ms), (new_v, vs)):
        dst.update(zip(SMALL, _unpack(buf, small_shapes)))

    return (loss, gx[None], *[grads[n] for n in WEIGHTS], *[delta[n] for n in WEIGHTS], *[new_m[n] for n in WEIGHTS],
            *[new_v[n] for n in WEIGHTS])
```

```python
import functools
import math

import jax
import jax.numpy as jnp
from jax import lax
from jax.experimental import pallas as pl
from jax.experimental.pallas import tpu as pltpu

F32 = jnp.float32
BF16 = jnp.bfloat16
MXU_DTYPE = jnp.bfloat16
HIGHEST = lax.Precision.HIGHEST
MESH = pl.DeviceIdType.MESH

D_MODEL = 1024
DEPTH = 4
HEADS = 8
Q_LORA = 256
KV_LORA = 128
NOPE = 64
ROPE = 32
VDIM = 64
ROPE_THETA = 10000.0
SC_DIM = 256
SSD_HEADS = 4
SSD_HEAD_DIM = 64
SSD_STATE = 128
SSD_DIM = 256
SSD_CONV_DIM = 768
SSD_CHUNK = 128
FFN_DIM = 2816
NORM_EPS = 1e-6
QK_SCALE = (NOPE + ROPE) ** -0.5
LANE = 128
HP = 128
FLASH_HEADS = 2

ZIN = 2560
Z_CQ, Z_CKV, Z_KR, Z_SCB, Z_SCC, Z_SCH, Z_SSZ, Z_XBC, Z_DT = 0, 256, 384, 512, 768, 1024, 1280, 1536, 2304
KR_LANE = 64
YCAT = HEADS * HP + SC_DIM + SSD_DIM
FFN_TILE = 256

ADAM_LR, ADAM_B1, ADAM_B2, ADAM_EPS, ADAM_WD, ADAM_STEP = 0.001, 0.9, 0.999, 1e-08, 0.01, 10

PACK_COLS = 1024


def _tile(n, pref):
    if n <= pref:
        return n
    t = (pref // LANE) * LANE
    while t >= LANE:
        if n % t == 0:
            return t
        t -= LANE
    raise ValueError(f"no tile for {n}")


def _mm(a, b, mode, out_dtype, name, tm=512, tn=512, tkmax=1536):
    pair = isinstance(a, tuple)
    a_list = list(a) if pair else [a]
    if mode == "nn":
        (M, Ka), (_, N) = a_list[0].shape, b.shape
    elif mode == "nt":
        (M, Ka), (N, _) = a_list[0].shape, b.shape
    else:
        (Ka, M), (_, N) = a_list[0].shape, b.shape
    tm, tn, tk = _tile(M, tm), _tile(N, tn), _tile(Ka, tkmax)
    nka = Ka // tk
    nk = nka * len(a_list)
    if mode == "nn":
        a_specs = [pl.BlockSpec((tm, tk), lambda i, j, k: (i, jnp.minimum(k, nka - 1))),
                   pl.BlockSpec((tm, tk), lambda i, j, k: (i, jnp.maximum(k - nka, 0)))][:len(a_list)]
        b_spec = pl.BlockSpec((tk, tn), lambda i, j, k: (k, j))
        dims = NN
    elif mode == "nt":
        a_specs = [pl.BlockSpec((tm, tk), lambda i, j, k: (i, jnp.minimum(k, nka - 1))),
                   pl.BlockSpec((tm, tk), lambda i, j, k: (i, jnp.maximum(k - nka, 0)))][:len(a_list)]
        b_spec = pl.BlockSpec((tn, tk), lambda i, j, k: (j, k))
        dims = NT
    else:
        a_specs = [pl.BlockSpec((tk, tm), lambda i, j, k: (k, i))]
        b_spec = pl.BlockSpec((tk, tn), lambda i, j, k: (k, j))
        dims = TN
    na = len(a_list)

    def body(*refs):
        a_refs, b_ref, o_ref = refs[:na], refs[na], refs[na + 1]
        k = pl.program_id(2)

        def prod(a_ref):
            return lax.dot_general(a_ref[...].astype(MXU_DTYPE), b_ref[...].astype(MXU_DTYPE), dims, preferred_element_type=F32)

        if nk == 1:
            o_ref[...] = prod(a_refs[0]).astype(o_ref.dtype)
            return
        acc_ref = refs[na + 2]

        @pl.when(k == 0)
        def _():
            acc_ref[...] = prod(a_refs[0])

        @pl.when((k > 0) & (k < nka))
        def _():
            acc_ref[...] += prod(a_refs[0])

        if pair:
            @pl.when(k >= nka)
            def _():
                acc_ref[...] += prod(a_refs[1])

        @pl.when(k == nk - 1)
        def _():
            o_ref[...] = acc_ref[...].astype(o_ref.dtype)

    return pl.pallas_call(
        body, name=name, grid=(M // tm, N // tn, nk),
        in_specs=a_specs + [b_spec], out_specs=pl.BlockSpec((tm, tn), lambda i, j, k: (i, j)),
        out_shape=jax.ShapeDtypeStruct((M, N), out_dtype),
        scratch_shapes=[pltpu.VMEM((tm, tn), F32)] if nk > 1 else [],
        compiler_params=pltpu.CompilerParams(dimension_semantics=("parallel", "parallel", "arbitrary")),
    )(*a_list, b)


HALO = 8


def _const(j, v):
    return v


def _rows(fn, T, tm, ins, consts, outs, accs, name, ncol=1):
    n = T // tm
    hb = tm // HALO
    last = T // HALO - 1
    in_specs, args = [], []
    for arr, bc, cb, kind in ins:
        if isinstance(kind, int):
            in_specs.append(pl.BlockSpec((tm, bc), lambda j, i, cb=cb, off=kind: (i + off, cb(j))))
        elif kind == "cur":
            in_specs.append(pl.BlockSpec((tm, bc), lambda j, i, cb=cb: (i, cb(j))))
        elif kind == "prev":
            in_specs.append(pl.BlockSpec((HALO, bc), lambda j, i, cb=cb: (jnp.maximum(i * hb - 1, 0), cb(j))))
        else:
            in_specs.append(pl.BlockSpec((HALO, bc), lambda j, i, cb=cb: (jnp.minimum((i + 1) * hb, last), cb(j))))
        args.append(arr)
    for arr, bc, cb in consts:
        in_specs.append(pl.BlockSpec((arr.shape[0], bc), lambda j, i, cb=cb: (0, cb(j))))
        args.append(arr)
    out_specs, out_shape = [], []
    for tc, dt, bc, cb in outs:
        out_specs.append(pl.BlockSpec((tm, bc), lambda j, i, cb=cb: (i, cb(j))))
        out_shape.append(jax.ShapeDtypeStruct((T, tc), dt))
    for r, tc, bc, cb in accs:
        out_specs.append(pl.BlockSpec((r, bc), lambda j, i, cb=cb: (0, cb(j))))
        out_shape.append(jax.ShapeDtypeStruct((r, tc), F32))
    nin, nout, nacc = len(args), len(outs), len(accs)

    def body(*refs):
        i = pl.program_id(1)
        res = fn(i, n, *[r[...] for r in refs[:nin]])
        for r, v in zip(refs[nin:nin + nout], res[:nout]):
            r[...] = v.astype(r.dtype)
        if nacc:
            acc_refs = refs[nin + nout:nin + nout + nacc]

            @pl.when(i == 0)
            def _():
                for r in acc_refs:
                    r[...] = jnp.zeros_like(r)

            for r, v in zip(acc_refs, res[nout:]):
                r[...] += v.astype(F32)

    res = pl.pallas_call(
        body, name=name, grid=(ncol, n), in_specs=in_specs, out_specs=out_specs, out_shape=out_shape,
        compiler_params=pltpu.CompilerParams(dimension_semantics=("arbitrary", "arbitrary")),
    )(*args)
    return res


def _cur(arr, bc=None, blk=0):
    bc = arr.shape[1] if bc is None else bc
    return (arr, bc, functools.partial(_const, v=blk), "cur")


def _halo(arr, kind, bc=None, blk=0):
    bc = arr.shape[1] if bc is None else bc
    return (arr, bc, functools.partial(_const, v=blk), kind)


def _cst(arr):
    return (arr, arr.shape[1], functools.partial(_const, v=0))


def _out(cols, dt):
    return (cols, dt, cols, functools.partial(_const, v=0))


def _acc(rows, cols):
    return (rows, cols, cols, functools.partial(_const, v=0))


def _rms(x, w):
    return x * lax.rsqrt(jnp.mean(x * x, axis=-1, keepdims=True) + NORM_EPS) * w


def _silu(x):
    return x * (1.0 / (1.0 + jnp.exp(-x)))


def _dsilu(x):
    s = 1.0 / (1.0 + jnp.exp(-x))
    return s * (1.0 + x * (1.0 - s))


def _softplus(x):
    return jnp.maximum(x, 0.0) + jnp.log1p(jnp.exp(-jnp.abs(x)))


def _shift(a, k):
    return pltpu.roll(a, k % a.shape[0], 0)


def _lroll(a, k):
    return pltpu.roll(a, k % a.shape[1], 1)


def _vjp_wrap(f, nrow, nconst, add_first=False):
    def g(i, n, *vals):
        rows, consts, mid = vals[:nrow], vals[len(vals) - nconst:], vals[nrow:len(vals) - nconst]
        cots = mid[:-1] if add_first else mid
        outs, pull = jax.vjp(f, *rows, *consts)
        grads = list(pull(tuple(c.astype(o.dtype) for c, o in zip(cots, outs))))
        if add_first:
            grads[0] = grads[0] + mid[-1]
        return tuple(grads)
    return g


def _rows_vjp(f, T, tm, rows, consts, cots, out_dtypes, name):
    return _rows(_vjp_wrap(f, len(rows), len(consts)), T, tm, [_cur(r) for r in rows] + [_cur(c) for c in cots],
                 [_cst(c) for c in consts], [_out(r.shape[1], dt) for r, dt in zip(rows, out_dtypes)],
                 [_acc(1, c.shape[1]) for c in consts], name)


def _f_premix(x, g):
    return (_rms(x, g),)


def _f_mla_pre(cq, ckv, qn, kvn):
    return _rms(cq, qn), _rms(ckv, kvn)


def _f_ssd_gate(y, z, nw):
    return (_rms(y * _silu(z), nw),)


def _f_post_mix(x, mixed, gpost, gffn):
    x1 = x + _rms(mixed, gpost)
    return x1, _rms(x1, gffn)


def _f_post_ffn(x1, d, gpost):
    return (x1 + _rms(d, gpost),)


def _rope_fwd(v, cosf, sina, sinb):
    return v * cosf + _lroll(v, -16) * sina + _lroll(v, 16) * sinb


def _rope_bwd(g, cosf, sina, sinb):
    return g * cosf + _lroll(g * sina, 16) + _lroll(g * sinb, -16)


def _k_rope_fwd(i, n, qpad, kvpad, kr, cosf, sina, sinb):
    qs, ks = [], []
    krr = _rope_fwd(kr, cosf, sina, sinb)
    for h in range(HEADS):
        sl = slice(h * HP, (h + 1) * HP)
        qs.append(_rope_fwd(qpad[:, sl], cosf, sina, sinb))
        ks.append(kvpad[:, sl].astype(F32) + krr)
    return jnp.concatenate(qs, axis=1), jnp.concatenate(ks, axis=1)


def _k_rope_bwd(i, n, dq, dk, dv, cosf, sina, sinb):
    lane = lax.broadcasted_iota(jnp.int32, (1, HP), 1)
    rmask = ((lane >= KR_LANE) & (lane < KR_LANE + ROPE)).astype(F32)
    dqs, dks = [], []
    dkr = jnp.zeros((dq.shape[0], HP), F32)
    for h in range(HEADS):
        sl = slice(h * HP, (h + 1) * HP)
        dqs.append(_rope_bwd(dq[:, sl], cosf, sina, sinb))
        dkh = dk[:, sl]
        dkr = dkr + dkh * rmask
        dks.append(dkh * (1.0 - rmask))
    dkr = _rope_bwd(dkr, cosf, sina, sinb) * rmask
    return jnp.concatenate(dqs, axis=1), jnp.concatenate(dks + [dv], axis=1), dkr


def _k_sconv_fwd(i, n, b, c, h, cp, hp, w):
    m = b.shape[0]
    up = jnp.where(i > 0, cp * hp, 0.0)
    ue = jnp.concatenate([up, c * h], axis=0)
    conv = w[2:3] * ue + w[1:2] * _shift(ue, 1) + w[0:1] * _shift(ue, 2)
    return (b * conv[HALO:],)


def _k_sconv_bwd(i, n, b, c, h, dy, cp, hp, bn, dyn, w):
    m = b.shape[0]
    up = jnp.where(i > 0, cp * hp, 0.0)
    ue = jnp.concatenate([up, c * h], axis=0)
    u1, u2 = _shift(ue, 1), _shift(ue, 2)
    conv = (w[2:3] * ue + w[1:2] * u1 + w[0:1] * u2)[HALO:]
    dc_cur = dy * b
    dce = jnp.concatenate([dc_cur, jnp.where(i < n - 1, dyn * bn, 0.0)], axis=0)
    du = (w[2:3] * dce + w[1:2] * _shift(dce, -1) + w[0:1] * _shift(dce, -2))[:m]
    dw = jnp.concatenate([
        jnp.sum(dc_cur * u2[HALO:], axis=0, keepdims=True),
        jnp.sum(dc_cur * u1[HALO:], axis=0, keepdims=True),
        jnp.sum(dc_cur * ue[HALO:], axis=0, keepdims=True),
        jnp.zeros((HALO - 3, b.shape[1]), F32)], axis=0)
    return dy * conv, du * h, du * c, dw


def _conv4(ue, w):
    return w[3:4] * ue + w[2:3] * _shift(ue, 1) + w[1:2] * _shift(ue, 2) + w[0:1] * _shift(ue, 3)


def _k_ssdconv_fwd(i, n, u, up, w, bias):
    ue = jnp.concatenate([jnp.where(i > 0, up, 0.0), u], axis=0)
    return (_silu(_conv4(ue, w)[HALO:] + bias),)


def _k_ssdconv_bwd(i, n, u, dout, up, un, doutn, w, bias):
    m = u.shape[0]
    ue = jnp.concatenate([jnp.where(i > 0, up, 0.0), u, un], axis=0)
    u1, u2, u3 = _shift(ue, 1), _shift(ue, 2), _shift(ue, 3)
    pre = (w[3:4] * ue + w[2:3] * u1 + w[1:2] * u2 + w[0:1] * u3)[HALO:] + bias
    doe = jnp.concatenate([dout, jnp.where(i < n - 1, doutn, 0.0)], axis=0)
    dpre = doe * _dsilu(pre)
    du = (w[3:4] * dpre + w[2:3] * _shift(dpre, -1) + w[1:2] * _shift(dpre, -2) + w[0:1] * _shift(dpre, -3))[:m]
    dp = dpre[:m]
    cur = slice(HALO, HALO + m)
    dw = jnp.concatenate([
        jnp.sum(dp * u3[cur], axis=0, keepdims=True),
        jnp.sum(dp * u2[cur], axis=0, keepdims=True),
        jnp.sum(dp * u1[cur], axis=0, keepdims=True),
        jnp.sum(dp * ue[cur], axis=0, keepdims=True),
        jnp.zeros((HALO - 4, u.shape[1]), F32)], axis=0)
    db = jnp.sum(dp, axis=0, keepdims=True)
    return du, dw, db


def _conv3(ue, w):
    return w[2:3] * ue + w[1:2] * _shift(ue, 1) + w[0:1] * _shift(ue, 2)


def _k_ffnact_fwd(i, n, ug, uu, ugp, uup, wg, wu, bg, bu):
    gate = _conv3(jnp.concatenate([jnp.where(i > 0, ugp, 0.0), ug], axis=0), wg)[HALO:] + bg
    upv = _conv3(jnp.concatenate([jnp.where(i > 0, uup, 0.0), uu], axis=0), wu)[HALO:] + bu
    return (_silu(gate) * upv,)


def _k_ffnact_bwd(i, n, ug, uu, dact, ugp, uup, ugn, uun, dactn, wg, wu, bg, bu):
    m = ug.shape[0]
    cur = slice(HALO, HALO + m)

    def taps(p, c, nx):
        e = jnp.concatenate([jnp.where(i > 0, p, 0.0), c, nx], axis=0)
        return e, _shift(e, 1), _shift(e, 2)

    def back(d, w):
        return (w[2:3] * d + w[1:2] * _shift(d, -1) + w[0:1] * _shift(d, -2))[:m]

    def wgrad(d, t):
        return jnp.concatenate([jnp.sum(d[:m] * t[2][cur], axis=0, keepdims=True), jnp.sum(d[:m] * t[1][cur], axis=0, keepdims=True),
                                jnp.sum(d[:m] * t[0][cur], axis=0, keepdims=True), jnp.zeros((HALO - 3, d.shape[1]), F32)], axis=0)

    tg, tu = taps(ugp, ug, ugn), taps(uup, uu, uun)
    gate = (wg[2:3] * tg[0] + wg[1:2] * tg[1] + wg[0:1] * tg[2])[HALO:] + bg
    upv = (wu[2:3] * tu[0] + wu[1:2] * tu[1] + wu[0:1] * tu[2])[HALO:] + bu
    dae = jnp.concatenate([dact, jnp.where(i < n - 1, dactn, 0.0)], axis=0)
    sg = 1.0 / (1.0 + jnp.exp(-gate))
    dg = dae * upv * (sg * (1.0 + gate * (1.0 - sg)))
    dup = dae * (gate * sg)
    return (back(dg, wg), back(dup, wu), wgrad(dg, tg), wgrad(dup, tu),
            jnp.sum(dg[:m], axis=0, keepdims=True), jnp.sum(dup[:m], axis=0, keepdims=True))


def _k_loss(i, n, y, tgt):
    e = y - tgt
    part = 0.5 * jnp.sum(jnp.sum(e * e, axis=1, keepdims=True) / D_MODEL, axis=0, keepdims=True)
    return e * (1.0 / D_MODEL), jnp.broadcast_to(part, (1, LANE))


def _k_adam(i, n, w, g, m, v):
    m = ADAM_B1 * m + (1.0 - ADAM_B1) * g
    v = ADAM_B2 * v + (1.0 - ADAM_B2) * (g * g)
    m_hat = m / (1.0 - ADAM_B1 ** ADAM_STEP)
    v_hat = v / (1.0 - ADAM_B2 ** ADAM_STEP)
    delta = -ADAM_LR * (m_hat / (jnp.sqrt(v_hat) + ADAM_EPS) + ADAM_WD * w)
    return g, delta, m, v


def _dotf(a, b, dims):
    return lax.dot_general(a.astype(MXU_DTYPE), b.astype(MXU_DTYPE), dims, preferred_element_type=F32)


NN = (((1,), (0,)), ((), ()))
NT = (((1,), (1,)), ((), ()))
TN = (((0,), (0,)), ((), ()))


def _ssd_chunk(x0, x1, x2, x3, b0, b1, c0, c1, dtraw, p0, p1, p2, p3, dtb, alog, dsk):
    xs, bs, cs_, ps = (x0, x1, x2, x3), (b0, b1), (c0, c1), (p0, p1, p2, p3)
    L = dtraw.shape[0]
    dt = _softplus(dtraw + dtb)
    adt = dt * (-jnp.exp(alog))
    row = lax.broadcasted_iota(jnp.int32, (L, L), 0)
    col = lax.broadcasted_iota(jnp.int32, (L, L), 1)
    tril = row >= col
    cum = jnp.dot(tril.astype(F32), adt, precision=HIGHEST, preferred_element_type=F32)
    cum_t = cum.T
    lane = lax.broadcasted_iota(jnp.int32, (1, LANE), 1)
    sub = lax.broadcasted_iota(jnp.int32, (LANE, 1), 0)
    lastcol = (lax.broadcasted_iota(jnp.int32, (1, L), 1) == L - 1).astype(F32)
    ys, news = [], []
    for h in range(SSD_HEADS):
        g = h // (SSD_HEADS // 2)
        oh = (lane == h).astype(F32)
        dth = jnp.sum(dt * oh, axis=1, keepdims=True)
        csh = jnp.sum(cum * oh, axis=1, keepdims=True)
        csr = jnp.sum(cum_t * (sub == h).astype(F32), axis=0, keepdims=True)
        cl = jnp.sum(csr * lastcol, axis=1, keepdims=True)
        dskh = jnp.sum(dsk * oh, axis=1, keepdims=True)
        x, bm, cm, prev = xs[h], bs[g], cs_[g], ps[h]
        xdt = x * dth
        decay = jnp.exp(jnp.where(tril, csh - csr, -jnp.inf))
        scores = _dotf(cm, bm, NT) * decay
        y_diag = _dotf(scores, xdt, NN)
        bd = bm * jnp.exp(cl - csh)
        cst = _dotf(xdt, bd, TN)
        news.append(prev * jnp.exp(cl) + cst)
        y_off = _dotf(cm, prev, NT) * jnp.exp(csh)
        ys.append(y_diag + y_off + x * dskh)
    return (*ys, *news)


def _ssd_operands(x_ref, dt_ref, par_ref, prev):
    xs = [x_ref[:, h * SSD_HEAD_DIM:(h + 1) * SSD_HEAD_DIM] for h in range(SSD_HEADS)]
    bs = [x_ref[:, SSD_DIM + g * SSD_STATE:SSD_DIM + (g + 1) * SSD_STATE] for g in range(2)]
    cs_ = [x_ref[:, SSD_DIM + 2 * SSD_STATE + g * SSD_STATE:SSD_DIM + 2 * SSD_STATE + (g + 1) * SSD_STATE] for g in range(2)]
    return (*xs, *bs, *cs_, dt_ref[...], *prev, par_ref[0:1, :], par_ref[1:2, :], par_ref[2:3, :])


def _ssd_fwd(xbc, dtraw, par, T):
    L = SSD_CHUNK
    nc = T // L
    P = SSD_HEAD_DIM

    def body(x_ref, dt_ref, par_ref, y_ref, st_ref, state):
        @pl.when(pl.program_id(0) == 0)
        def _():
            state[...] = jnp.zeros_like(state)

        st_ref[0] = state[...]
        prev = [state[h * P:(h + 1) * P, :] for h in range(SSD_HEADS)]
        res = _ssd_chunk(*_ssd_operands(x_ref, dt_ref, par_ref, prev))
        for h in range(SSD_HEADS):
            y_ref[:, h * P:(h + 1) * P] = res[h]
            state[h * P:(h + 1) * P, :] = res[SSD_HEADS + h]

    return pl.pallas_call(
        body, name="ssd_scan_fwd", grid=(nc,),
        in_specs=[pl.BlockSpec((L, SSD_CONV_DIM), lambda c: (c, 0)), pl.BlockSpec((L, LANE), lambda c: (c, 0)),
                  pl.BlockSpec((8, LANE), lambda c: (0, 0))],
        out_specs=[pl.BlockSpec((L, SSD_DIM), lambda c: (c, 0)), pl.BlockSpec((1, SSD_DIM, SSD_STATE), lambda c: (c, 0, 0))],
        out_shape=[jax.ShapeDtypeStruct((T, SSD_DIM), F32), jax.ShapeDtypeStruct((nc, SSD_DIM, SSD_STATE), F32)],
        scratch_shapes=[pltpu.VMEM((SSD_DIM, SSD_STATE), F32)],
        compiler_params=pltpu.CompilerParams(dimension_semantics=("arbitrary",)),
    )(xbc, dtraw, par)


def _ssd_bwd(xbc, dtraw, par, states, dy, T):
    L = SSD_CHUNK
    nc = T // L
    P = SSD_HEAD_DIM

    def body(x_ref, dt_ref, par_ref, st_ref, dy_ref, dx_ref, ddt_ref, dpar_ref, dstate):
        @pl.when(pl.program_id(0) == 0)
        def _():
            dstate[...] = jnp.zeros_like(dstate)
            dpar_ref[...] = jnp.zeros_like(dpar_ref)

        prev = [st_ref[0, h * P:(h + 1) * P, :] for h in range(SSD_HEADS)]
        prim = _ssd_operands(x_ref, dt_ref, par_ref, prev)
        _, pull = jax.vjp(_ssd_chunk, *prim)
        cots = tuple(dy_ref[:, h * P:(h + 1) * P] for h in range(SSD_HEADS)) + tuple(
            dstate[h * P:(h + 1) * P, :] for h in range(SSD_HEADS))
        g = pull(cots)
        for h in range(SSD_HEADS):
            dx_ref[:, h * P:(h + 1) * P] = g[h]
            dstate[h * P:(h + 1) * P, :] = g[9 + h]
        for k in range(2):
            dx_ref[:, SSD_DIM + k * SSD_STATE:SSD_DIM + (k + 1) * SSD_STATE] = g[4 + k]
            dx_ref[:, SSD_DIM + 2 * SSD_STATE + k * SSD_STATE:SSD_DIM + 2 * SSD_STATE + (k + 1) * SSD_STATE] = g[6 + k]
        ddt_ref[...] = g[8]
        for r in range(3):
            dpar_ref[r:r + 1, :] += g[13 + r]

    rev = lambda c: (nc - 1 - c, 0)
    return pl.pallas_call(
        body, name="ssd_scan_bwd", grid=(nc,),
        in_specs=[pl.BlockSpec((L, SSD_CONV_DIM), rev), pl.BlockSpec((L, LANE), rev), pl.BlockSpec((8, LANE), lambda c: (0, 0)),
                  pl.BlockSpec((1, SSD_DIM, SSD_STATE), lambda c: (nc - 1 - c, 0, 0)), pl.BlockSpec((L, SSD_DIM), rev)],
        out_specs=[pl.BlockSpec((L, SSD_CONV_DIM), rev), pl.BlockSpec((L, LANE), rev), pl.BlockSpec((8, LANE), lambda c: (0, 0))],
        out_shape=[jax.ShapeDtypeStruct((T, SSD_CONV_DIM), F32), jax.ShapeDtypeStruct((T, LANE), F32),
                   jax.ShapeDtypeStruct((8, LANE), F32)],
        scratch_shapes=[pltpu.VMEM((SSD_DIM, SSD_STATE), F32)],
        compiler_params=pltpu.CompilerParams(dimension_semantics=("arbitrary",)),
    )(xbc, dtraw, par, states, dy)


def _flash_fwd(q, k, kv, T):
    tq = tk = min(512, T)
    nq = T // tq
    G = FLASH_HEADS
    rep = tk // HP

    def body(q_ref, k_ref, v_ref, o_ref, m_ref, l_ref, acc_ref):
        i, j = pl.program_id(1), pl.program_id(2)

        @pl.when(j == 0)
        def _():
            m_ref[...] = jnp.full_like(m_ref, -jnp.inf)
            l_ref[...] = jnp.zeros_like(l_ref)
            acc_ref[...] = jnp.zeros_like(acc_ref)

        def step(diagonal):
            for g in range(G):
                sl = slice(g * HP, (g + 1) * HP)
                s = _dotf(q_ref[:, sl], k_ref[:, sl], NT) * QK_SCALE
                if diagonal:
                    rows = lax.broadcasted_iota(jnp.int32, (tq, tk), 0)
                    cols = lax.broadcasted_iota(jnp.int32, (tq, tk), 1)
                    s = jnp.where(rows >= cols, s, -jnp.inf)
                m_old = m_ref[:, sl]
                m_new = jnp.maximum(m_old, jnp.max(s, axis=1, keepdims=True))
                p = jnp.exp(s - jnp.tile(m_new, (1, rep)))
                alpha = jnp.exp(m_old - m_new)
                l_ref[:, sl] = alpha * l_ref[:, sl] + jnp.sum(p, axis=1, keepdims=True)
                acc_ref[:, sl] = alpha * acc_ref[:, sl] + _dotf(p, v_ref[:, sl], NN)
                m_ref[:, sl] = m_new

        @pl.when(j < i)
        def _():
            step(False)

        @pl.when(j == i)
        def _():
            step(True)
            lane = lax.broadcasted_iota(jnp.int32, (tq, HP), 1)
            for g in range(G):
                sl = slice(g * HP, (g + 1) * HP)
                l = l_ref[:, sl]
                o_ref[:, sl] = jnp.where(lane < VDIM, acc_ref[:, sl] / l, m_ref[:, sl] + jnp.log(l))

    W = G * HP
    return pl.pallas_call(
        body, name="mla_flash_fwd", grid=(HEADS // G, nq, nq),
        in_specs=[pl.BlockSpec((tq, W), lambda h, i, j: (i, h)),
                  pl.BlockSpec((tk, W), lambda h, i, j: (jnp.minimum(j, i), h)),
                  pl.BlockSpec((tk, W), lambda h, i, j: (jnp.minimum(j, i), HEADS // G + h))],
        out_specs=pl.BlockSpec((tq, W), lambda h, i, j: (i, h)),
        out_shape=jax.ShapeDtypeStruct((T, HEADS * HP), F32),
        scratch_shapes=[pltpu.VMEM((tq, W), F32), pltpu.VMEM((tq, W), F32), pltpu.VMEM((tq, W), F32)],
        compiler_params=pltpu.CompilerParams(dimension_semantics=("parallel", "parallel", "arbitrary")),
    )(q, k, kv)


def _flash_bwd(q, k, kv, o, dycat, T):
    tq = tk = min(512, T)
    nq = T // tq
    G = FLASH_HEADS

    def body(q_ref, k_ref, v_ref, o_ref, do_ref, dq_ref, dk_ref, dv_ref):
        j, i = pl.program_id(1), pl.program_id(2)

        @pl.when((j == 0) & (i == 0))
        def _():
            dq_ref[...] = jnp.zeros_like(dq_ref)

        @pl.when(i == 0)
        def _():
            dk_ref[...] = jnp.zeros_like(dk_ref)
            dv_ref[...] = jnp.zeros_like(dv_ref)

        def step(diagonal):
            r0 = pl.multiple_of(i * tq, tq)
            for g in range(G):
                sl = slice(g * HP, (g + 1) * HP)
                qv, kv, vv, ov, dov = q_ref[:, sl], k_ref[:, sl], v_ref[:, sl], o_ref[:, sl], do_ref[:, sl]
                s = _dotf(qv, kv, NT) * QK_SCALE
                p = jnp.exp(s - ov[:, VDIM:VDIM + 1])
                if diagonal:
                    rows = lax.broadcasted_iota(jnp.int32, (tq, tk), 0)
                    cols = lax.broadcasted_iota(jnp.int32, (tq, tk), 1)
                    p = jnp.where(rows >= cols, p, 0.0)
                dsum = jnp.sum(dov * ov, axis=1, keepdims=True)
                dv_ref[:, sl] += _dotf(p, dov, TN)
                dp = _dotf(dov, vv, NT)
                ds = p * (dp - dsum) * QK_SCALE
                dk_ref[:, sl] += _dotf(ds, qv, TN)
                dq_ref[pl.ds(r0, tq), sl] += _dotf(ds, kv, NN)

        @pl.when(i > j)
        def _():
            step(False)

        @pl.when(i == j)
        def _():
            step(True)

    W = G * HP
    qmap = lambda h, j, i: (jnp.maximum(i, j), h)
    kmap = lambda h, j, i: (j, h)
    vmap = lambda h, j, i: (j, HEADS // G + h)
    return pl.pallas_call(
        body, name="mla_flash_bwd", grid=(HEADS // G, nq, nq),
        in_specs=[pl.BlockSpec((tq, W), qmap), pl.BlockSpec((tk, W), kmap), pl.BlockSpec((tk, W), vmap),
                  pl.BlockSpec((tq, W), qmap), pl.BlockSpec((tq, W), qmap)],
        out_specs=[pl.BlockSpec((T, W), lambda h, j, i: (0, h)), pl.BlockSpec((tk, W), kmap), pl.BlockSpec((tk, W), kmap)],
        out_shape=[jax.ShapeDtypeStruct((T, HEADS * HP), F32)] * 3,
        compiler_params=pltpu.CompilerParams(dimension_semantics=("parallel", "arbitrary", "arbitrary")),
    )(q, k, kv, o, dycat)


_IN_SRC = (0, 256, 384, 416, 672, 928, 1184, 1440, 2208, 2212)
_IN_DST = (Z_CQ, Z_CKV, Z_KR + KR_LANE, Z_SCB, Z_SCC, Z_SCH, Z_SSZ, Z_XBC, Z_DT)


def _pad_cols_in(w):
    parts, at = [], 0
    for s0, s1, d0 in zip(_IN_SRC[:-1], _IN_SRC[1:], _IN_DST):
        if d0 > at:
            parts.append(jnp.zeros(w.shape[:-1] + (d0 - at,), w.dtype))
        parts.append(w[..., s0:s1])
        at = d0 + (s1 - s0)
    parts.append(jnp.zeros(w.shape[:-1] + (ZIN - at,), w.dtype))
    return jnp.concatenate(parts, axis=-1)


def _unpad_cols_in(w):
    return jnp.concatenate([w[..., d0:d0 + (s1 - s0)] for s0, s1, d0 in zip(_IN_SRC[:-1], _IN_SRC[1:], _IN_DST)], axis=-1)


def _pad_heads(w, width):
    w = w.reshape(w.shape[:-1] + (HEADS, width))
    w = jnp.pad(w, [(0, 0)] * (w.ndim - 1) + [(0, HP - width)])
    return w.reshape(w.shape[:-2] + (HEADS * HP,))


def _unpad_heads(w, width):
    w = w.reshape(w.shape[:-1] + (HEADS, HP))[..., :width]
    return w.reshape(w.shape[:-2] + (HEADS * width,))


def _pad_kv(w):
    w = w.reshape(w.shape[:-1] + (HEADS, NOPE + VDIM))
    return jnp.concatenate([_pad_heads(w[..., :NOPE].reshape(w.shape[:-2] + (HEADS * NOPE,)), NOPE),
                            _pad_heads(w[..., NOPE:].reshape(w.shape[:-2] + (HEADS * VDIM,)), VDIM)], axis=-1)


def _unpad_kv(w):
    k = _unpad_heads(w[..., :HEADS * HP], NOPE).reshape(w.shape[:-1] + (HEADS, NOPE))
    v = _unpad_heads(w[..., HEADS * HP:], VDIM).reshape(w.shape[:-1] + (HEADS, VDIM))
    return jnp.concatenate([k, v], axis=-1).reshape(w.shape[:-1] + (HEADS * (NOPE + VDIM),))


def _pad_out_rows(w):
    att = jnp.swapaxes(_pad_heads(jnp.swapaxes(w[:HEADS * VDIM], 0, 1), VDIM), 0, 1)
    return jnp.concatenate([att, w[HEADS * VDIM:]], axis=0)


def _unpad_out_rows(w):
    att = jnp.swapaxes(_unpad_heads(jnp.swapaxes(w[:HEADS * HP], 0, 1), VDIM), 0, 1)
    return jnp.concatenate([att, w[HEADS * HP:]], axis=0)


def _row8(*vecs):
    c = vecs[0].shape[-1]
    return jnp.concatenate([v.reshape(1, c).astype(F32) for v in vecs] + [jnp.zeros((8 - len(vecs), c), F32)], axis=0)


def _lanes(v):
    return jnp.pad(v.astype(F32), (0, LANE - v.shape[0])).reshape(1, LANE)


def _rope_tables(positions):
    inv_freq = 1.0 / (ROPE_THETA ** (jnp.arange(0, ROPE, 2, dtype=F32) / ROPE))
    ang = positions.astype(F32)[:, None] * inv_freq
    cos, sin = jnp.cos(ang), jnp.sin(ang)
    T = positions.shape[0]
    half = ROPE // 2
    one = jnp.ones((T, KR_LANE), F32)
    zero = jnp.zeros((T, KR_LANE), F32)
    tail1 = jnp.ones((T, HP - KR_LANE - ROPE), F32)
    tail0 = jnp.zeros((T, HP - KR_LANE - ROPE), F32)
    z16 = jnp.zeros((T, half), F32)
    cosf = jnp.concatenate([one, cos, cos, tail1], axis=1)
    sina = jnp.concatenate([zero, -sin, z16, tail0], axis=1)
    sinb = jnp.concatenate([zero, z16, sin, tail0], axis=1)
    return cosf, sina, sinb


def _layer_weights(W, l):
    c = lambda a: a.astype(MXU_DTYPE)
    return dict(
        w_in=c(_pad_cols_in(W["w_in"][l])),
        w_q=c(_pad_heads(W["mla_w_q_up"][l], NOPE + ROPE)),
        w_kv=c(_pad_kv(W["mla_w_kv_up"][l])),
        w_out=c(_pad_out_rows(W["w_out"][l])),
        w_up=c(W["ffn_w_up"][l]),
        w_down=c(W["ffn_w_down"][l]),
        sc_w=_row8(*W["sc_conv_w"][l].astype(F32)),
        ssd_w=_row8(*W["ssd_conv_w"][l].astype(F32)),
        ffn_w=_row8(*W["ffn_conv_w"][l].astype(F32)),
    )


def _local_step(x, positions, target, W, S):
    T = x.shape[0]
    tm = min(256, T)
    cosf, sina, sinb = _rope_tables(positions)
    saved = []
    xl = x
    for l in range(DEPTH):
        lw = _layer_weights(W, l)
        g_pre = S["norm_mix_pre"][l].reshape(1, -1)
        g_post = S["norm_mix_post"][l].reshape(1, -1)
        g_fpre = S["norm_ffn_pre"][l].reshape(1, -1)
        g_fpost = S["norm_ffn_post"][l].reshape(1, -1)
        qn = S["mla_q_norm"][l].reshape(1, -1)
        kvn = S["mla_kv_norm"][l].reshape(1, -1)
        ssd_b = S["ssd_conv_b"][l].reshape(1, -1)
        ssd_par = _row8(jnp.pad(S["ssd_dt_bias"][l], (0, LANE - SSD_HEADS)), jnp.pad(S["ssd_a_log"][l], (0, LANE - SSD_HEADS)),
                        jnp.pad(S["ssd_d"][l], (0, LANE - SSD_HEADS)))
        ssd_nw = S["ssd_norm"][l].reshape(1, -1)
        ffn_b = S["ffn_conv_b"][l].reshape(1, -1)

        (h1,) = _rows(lambda i, n, *v: _f_premix(*v), T, tm, [_cur(xl)], [_cst(g_pre)], [_out(D_MODEL, BF16)], [], "pre_mix_norm")
        zin = _mm(h1, lw["w_in"], "nn", F32, "mm_in")
        qlat, kvlat = _rows(lambda i, n, *v: _f_mla_pre(*v), T, tm, [_cur(zin, Q_LORA, 0), _cur(zin, KV_LORA, Z_CKV // KV_LORA)],
                            [_cst(qn), _cst(kvn)], [_out(Q_LORA, BF16), _out(KV_LORA, BF16)], [], "mla_pre_norm")
        qpad = _mm(qlat, lw["w_q"], "nn", F32, "mm_q_up")
        kvpad = _mm(kvlat, lw["w_kv"], "nn", BF16, "mm_kv_up")
        qr, kr = _rows(_k_rope_fwd, T, tm, [_cur(qpad), _cur(kvpad, HEADS * HP, 0), _cur(zin, LANE, Z_KR // LANE),
                                            _cur(cosf), _cur(sina), _cur(sinb)], [],
                       [_out(HEADS * HP, BF16), _out(HEADS * HP, BF16)], [], "mla_rope")
        o = _flash_fwd(qr, kr, kvpad, T)
        (yconv,) = _rows(_k_sconv_fwd, T, tm, [_cur(zin, SC_DIM, Z_SCB // SC_DIM), _cur(zin, SC_DIM, Z_SCC // SC_DIM),
                                               _cur(zin, SC_DIM, Z_SCH // SC_DIM), _halo(zin, "prev", SC_DIM, Z_SCC // SC_DIM),
                                               _halo(zin, "prev", SC_DIM, Z_SCH // SC_DIM)], [_cst(lw["sc_w"])],
                         [_out(SC_DIM, F32)], [], "short_conv_fwd")
        (xbc,) = _rows(_k_ssdconv_fwd, T, tm, [_cur(zin, SSD_CONV_DIM, Z_XBC // SSD_CONV_DIM),
                                               _halo(zin, "prev", SSD_CONV_DIM, Z_XBC // SSD_CONV_DIM)],
                       [_cst(lw["ssd_w"]), _cst(ssd_b)], [_out(SSD_CONV_DIM, F32)], [], "ssd_conv_fwd")
        dtraw = zin[:, Z_DT:Z_DT + LANE]
        yscan, states = _ssd_fwd(xbc, dtraw, ssd_par, T)
        (yssd,) = _rows(lambda i, n, *v: _f_ssd_gate(*v), T, tm, [_cur(yscan), _cur(zin, SSD_DIM, Z_SSZ // SSD_DIM)], [_cst(ssd_nw)],
                        [_out(SSD_DIM, F32)], [], "ssd_gate_fwd")
        ycat = jnp.concatenate([o.astype(BF16), yconv.astype(BF16), yssd.astype(BF16)], axis=1)
        mixed = _mm(ycat, lw["w_out"], "nn", F32, "mm_out")
        x1, h2 = _rows(lambda i, n, *v: _f_post_mix(*v), T, tm, [_cur(xl), _cur(mixed)], [_cst(g_post), _cst(g_fpre)],
                       [_out(D_MODEL, F32), _out(D_MODEL, BF16)], [], "post_mix_fwd")
        upre = _mm(h2, lw["w_up"], "nn", F32, "mm_up")
        nt = FFN_DIM // FFN_TILE
        gcol, ucol = (lambda j: j), (lambda j: j + nt)
        (act,) = _rows(_k_ffnact_fwd, T, tm,
                       [(upre, FFN_TILE, gcol, "cur"), (upre, FFN_TILE, ucol, "cur"), (upre, FFN_TILE, gcol, "prev"),
                        (upre, FFN_TILE, ucol, "prev")],
                       [(lw["ffn_w"], FFN_TILE, gcol), (lw["ffn_w"], FFN_TILE, ucol), (ffn_b, FFN_TILE, gcol), (ffn_b, FFN_TILE, ucol)],
                       [(FFN_DIM, BF16, FFN_TILE, gcol)], [], "ffn_act_fwd", ncol=nt)
        dn = _mm(act, lw["w_down"], "nn", F32, "mm_down")
        (x2,) = _rows(lambda i, n, *v: _f_post_ffn(*v), T, tm, [_cur(x1), _cur(dn)], [_cst(g_fpost)], [_out(D_MODEL, F32)], [], "post_ffn_fwd")
        saved.append(dict(lw=lw, x=xl, h1=h1, zin=zin, qlat=qlat, kvlat=kvlat, qr=qr, kr=kr, kvpad=kvpad, o=o, xbc=xbc, dtraw=dtraw,
                          yscan=yscan, states=states, ycat=ycat, mixed=mixed, x1=x1, h2=h2, upre=upre, act=act, dn=dn,
                          g_pre=g_pre, g_post=g_post, g_fpre=g_fpre, g_fpost=g_fpost, qn=qn, kvn=kvn, ssd_b=ssd_b,
                          ssd_par=ssd_par, ssd_nw=ssd_nw, ffn_b=ffn_b))
        xl = x2

    gx, loss_part = _rows(_k_loss, T, tm, [_cur(xl), _cur(target)], [], [_out(D_MODEL, F32)], [_acc(1, LANE)], "loss_head")

    GW = {k: [None] * DEPTH for k in ("w_in", "mla_w_q_up", "mla_w_kv_up", "sc_conv_w", "ssd_conv_w", "w_out", "ffn_w_up",
                                      "ffn_conv_w", "ffn_w_down")}
    GS = {k: [None] * DEPTH for k in ("norm_mix_pre", "norm_mix_post", "norm_ffn_pre", "norm_ffn_post", "mla_q_norm", "mla_kv_norm",
                                      "ssd_conv_b", "ssd_dt_bias", "ssd_a_log", "ssd_d", "ssd_norm", "ffn_conv_b")}
    nt = FFN_DIM // FFN_TILE
    gcol, ucol = (lambda j: j), (lambda j: j + nt)
    for l in reversed(range(DEPTH)):
        s = saved[l]
        lw = s["lw"]
        gx1, ddn, dgf = _rows_vjp(_f_post_ffn, T, tm, [s["x1"], s["dn"]], [s["g_fpost"]], [gx], [F32, BF16], "post_ffn_bwd")
        GS["norm_ffn_post"][l] = dgf[0]
        dact = _mm(ddn, lw["w_down"], "nt", F32, "mm_down_dx")
        GW["ffn_w_down"][l] = _mm(s["act"], ddn, "tn", BF16, "mm_down_dw")
        up = s["upre"]
        dug, duu, dwg, dwu, dbg, dbu = _rows(
            _k_ffnact_bwd, T, tm,
            [(up, FFN_TILE, gcol, "cur"), (up, FFN_TILE, ucol, "cur"), (dact, FFN_TILE, gcol, "cur"), (up, FFN_TILE, gcol, "prev"),
             (up, FFN_TILE, ucol, "prev"), (up, FFN_TILE, gcol, "next"), (up, FFN_TILE, ucol, "next"), (dact, FFN_TILE, gcol, "next")],
            [(lw["ffn_w"], FFN_TILE, gcol), (lw["ffn_w"], FFN_TILE, ucol), (s["ffn_b"], FFN_TILE, gcol), (s["ffn_b"], FFN_TILE, ucol)],
            [(FFN_DIM, BF16, FFN_TILE, gcol)] * 2,
            [(HALO, FFN_DIM, FFN_TILE, gcol)] * 2 + [(1, FFN_DIM, FFN_TILE, gcol)] * 2, "ffn_act_bwd", ncol=nt)
        GW["ffn_conv_w"][l] = jnp.concatenate([dwg[:3], dwu[:3]], axis=1)
        GS["ffn_conv_b"][l] = jnp.concatenate([dbg[0], dbu[0]])
        dh2 = _mm((dug, duu), lw["w_up"], "nt", F32, "mm_up_dx")
        GW["ffn_w_up"][l] = (_mm(s["h2"], dug, "tn", BF16, "mm_up_dw_gate"), _mm(s["h2"], duu, "tn", BF16, "mm_up_dw_up"))
        gx0, dmixed, dgp, dgf = _rows_vjp(_f_post_mix, T, tm, [s["x"], s["mixed"]], [s["g_post"], s["g_fpre"]], [gx1, dh2],
                                          [F32, BF16], "post_mix_bwd")
        GS["norm_mix_post"][l], GS["norm_ffn_pre"][l] = dgp[0], dgf[0]
        dycat = _mm(dmixed, lw["w_out"], "nt", F32, "mm_out_dx")
        GW["w_out"][l] = _unpad_out_rows(_mm(s["ycat"], dmixed, "tn", BF16, "mm_out_dw"))
        zin = s["zin"]
        dyscan, dz, dnw = _rows(_vjp_wrap(_f_ssd_gate, 2, 1), T, tm,
                                [_cur(s["yscan"]), _cur(zin, SSD_DIM, Z_SSZ // SSD_DIM), _cur(dycat, SSD_DIM, (HEADS * HP + SC_DIM) // SSD_DIM)],
                                [_cst(s["ssd_nw"])], [_out(SSD_DIM, F32), _out(SSD_DIM, BF16)], [_acc(1, SSD_DIM)], "ssd_gate_bwd")
        GS["ssd_norm"][l] = dnw[0]
        dxbc, ddtraw, dpar = _ssd_bwd(s["xbc"], s["dtraw"], s["ssd_par"], s["states"], dyscan, T)
        GS["ssd_dt_bias"][l], GS["ssd_a_log"][l], GS["ssd_d"][l] = dpar[0, :SSD_HEADS], dpar[1, :SSD_HEADS], dpar[2, :SSD_HEADS]
        xb = Z_XBC // SSD_CONV_DIM
        dxraw, dsw, dsb = _rows(_k_ssdconv_bwd, T, tm,
                                [_cur(zin, SSD_CONV_DIM, xb), _cur(dxbc), _halo(zin, "prev", SSD_CONV_DIM, xb),
                                 _halo(zin, "next", SSD_CONV_DIM, xb), _halo(dxbc, "next")],
                                [_cst(lw["ssd_w"]), _cst(s["ssd_b"])], [_out(SSD_CONV_DIM, BF16)],
                                [_acc(HALO, SSD_CONV_DIM), _acc(1, SSD_CONV_DIM)], "ssd_conv_bwd")
        GW["ssd_conv_w"][l] = dsw[:4]
        GS["ssd_conv_b"][l] = dsb[0]
        cb = (HEADS * HP) // SC_DIM
        dscb, dscc, dsch, dscw = _rows(_k_sconv_bwd, T, tm,
                                       [_cur(zin, SC_DIM, Z_SCB // SC_DIM), _cur(zin, SC_DIM, Z_SCC // SC_DIM),
                                        _cur(zin, SC_DIM, Z_SCH // SC_DIM), _cur(dycat, SC_DIM, cb),
                                        _halo(zin, "prev", SC_DIM, Z_SCC // SC_DIM), _halo(zin, "prev", SC_DIM, Z_SCH // SC_DIM),
                                        _halo(zin, "next", SC_DIM, Z_SCB // SC_DIM), _halo(dycat, "next", SC_DIM, cb)],
                                       [_cst(lw["sc_w"])], [_out(SC_DIM, BF16)] * 3, [_acc(HALO, SC_DIM)], "short_conv_bwd")
        GW["sc_conv_w"][l] = dscw[:3]
        dq, dk, dv = _flash_bwd(s["qr"], s["kr"], s["kvpad"], s["o"], dycat, T)
        dqpad, dkvpad, dkr = _rows(_k_rope_bwd, T, tm, [_cur(dq), _cur(dk), _cur(dv), _cur(cosf), _cur(sina), _cur(sinb)], [],
                                   [_out(HEADS * HP, BF16), _out(2 * HEADS * HP, BF16), _out(LANE, BF16)], [], "mla_rope_bwd")
        dqlat = _mm(dqpad, lw["w_q"], "nt", F32, "mm_q_dx")
        GW["mla_w_q_up"][l] = _unpad_heads(_mm(s["qlat"], dqpad, "tn", BF16, "mm_q_dw"), NOPE + ROPE)
        dkvlat = _mm(dkvpad, lw["w_kv"], "nt", F32, "mm_kv_dx")
        GW["mla_w_kv_up"][l] = _unpad_kv(_mm(s["kvlat"], dkvpad, "tn", BF16, "mm_kv_dw"))
        dcq, dckv, dqn, dkvn = _rows(_vjp_wrap(_f_mla_pre, 2, 2), T, tm,
                                     [_cur(zin, Q_LORA, 0), _cur(zin, KV_LORA, Z_CKV // KV_LORA), _cur(dqlat), _cur(dkvlat)],
                                     [_cst(s["qn"]), _cst(s["kvn"])], [_out(Q_LORA, BF16), _out(KV_LORA, BF16)],
                                     [_acc(1, Q_LORA), _acc(1, KV_LORA)], "mla_pre_bwd")
        GS["mla_q_norm"][l], GS["mla_kv_norm"][l] = dqn[0], dkvn[0]
        dzin = jnp.concatenate([dcq, dckv, dkr, dscb, dscc, dsch, dz, dxraw, ddtraw.astype(BF16), jnp.zeros((T, ZIN - Z_DT - LANE), BF16)], axis=1)
        dh1 = _mm(dzin, lw["w_in"], "nt", F32, "mm_in_dx")
        GW["w_in"][l] = _unpad_cols_in(_mm(s["h1"], dzin, "tn", BF16, "mm_in_dw"))
        gx, dgp = _rows(_vjp_wrap(_f_premix, 1, 1, add_first=True), T, tm, [_cur(s["x"]), _cur(dh1), _cur(gx0)], [_cst(s["g_pre"])],
                        [_out(D_MODEL, F32)], [_acc(1, D_MODEL)], "pre_mix_bwd")
        GS["norm_mix_pre"][l] = dgp[0]
    GS = {k: jnp.stack(v) for k, v in GS.items()}
    return loss_part[0, 0], gx, GW, GS


WEIGHTS = ("norm_mix_pre", "norm_mix_post", "norm_ffn_pre", "norm_ffn_post", "w_in", "mla_q_norm", "mla_w_q_up", "mla_kv_norm",
           "mla_w_kv_up", "sc_conv_w", "ssd_conv_w", "ssd_conv_b", "ssd_dt_bias", "ssd_a_log", "ssd_d", "ssd_norm", "w_out",
           "ffn_w_up", "ffn_conv_w", "ffn_conv_b", "ffn_w_down")
SHARDED = (("w_in", 2), ("mla_w_q_up", 2), ("mla_w_kv_up", 2), ("sc_conv_w", 2), ("ssd_conv_w", 2), ("w_out", 1),
           ("ffn_w_up", 2), ("ffn_conv_w", 2), ("ffn_w_down", 1))
SMALL = tuple(n for n in WEIGHTS if n not in dict(SHARDED))
N_CHIPS = 4
N_DEV = 8
ROW_ALIGN = 256
SLAB_ALIGN = 16
MAIN = ("ffn_w_down", "w_out", "w_in", "mla_w_q_up", "mla_w_kv_up", "sc_conv_w", "ssd_conv_w")
WIDE = ("ffn_w_up", "ffn_conv_w")


def _slab_rows(shape, width):
    if len(shape) == 2 and shape[1] == width and shape[0] % SLAB_ALIGN == 0:
        return shape[0]
    return -(-math.prod(shape) // (width * SLAB_ALIGN)) * SLAB_ALIGN


def _slab(piece, width, dtype):
    rows = _slab_rows(piece.shape, width)
    if piece.shape == (rows, width):
        return piece.astype(dtype)
    flat = piece.astype(dtype).reshape(-1)
    return jnp.pad(flat, (0, rows * width - flat.shape[0])).reshape(rows, width)


def _unslab(slab, shape):
    if slab.shape == tuple(shape):
        return slab
    return slab.reshape(-1)[:math.prod(shape)].reshape(shape)


def _layout(shapes):
    out = {}
    for buf, names in (("main", MAIN), ("wide", WIDE)):
        width = PACK_COLS if buf == "main" else shapes["ffn_w_up"][-1]
        ents, off = [], 0
        for n in names:
            shp = tuple(shapes[n])
            todo = [(None, False, shp), (None, True, shp)] if n.endswith("conv_w") else [(l, False, shp[1:]) for l in range(shp[0])]
            for l, lo, ps in todo:
                r = _slab_rows(ps, width)
                ents.append((n, l, lo, ps, off, r))
                off += r
        out[buf] = (width, -(-off // ROW_ALIGN) * ROW_ALIGN, ents)
    return out


def _pack(layout, piece, dtype):
    width, rows, ents = layout
    slabs = []
    for n, l, lo, ps, off, r in ents:
        p = piece(n, l, lo)
        slabs.append(jnp.zeros((r, width), dtype) if p is None else _slab(p, width, dtype))
    used = ents[-1][4] + ents[-1][5]
    if rows > used:
        slabs.append(jnp.zeros((rows - used, width), dtype))
    return jnp.concatenate(slabs, axis=0)


ANY = pl.BlockSpec(memory_space=pl.ANY)


def _pos():
    return lax.axis_index("x"), lax.axis_index("y"), lax.axis_index("c")


def _other_chips(x, y):
    return ((1 - x, y), (x, 1 - y), (1 - x, 1 - y))


def _remote(src, dst, ssem, rsem, dev):
    return pltpu.make_async_remote_copy(src_ref=src, dst_ref=dst, send_sem=ssem, recv_sem=rsem, device_id=dev, device_id_type=MESH)


AG_CHUNKS = 2


def _all_gather_weights(wpk):
    R, C = wpk.shape
    H = R // 2
    CH = H // AG_CHUNKS
    n = 3 * AG_CHUNKS

    def body(w_ref, out_ref, isend, irecv, dsend, drecv):
        x, y, c = _pos()
        k = 2 * x + y
        sib = (x, y, 1 - c)
        chips = _other_chips(x, y)

        def rows(kk, half, ch):
            return out_ref.at[kk, pl.ds(half * H + ch * CH, CH), :]

        first = []
        for p, (cx, cy) in enumerate(chips):
            for ch in range(AG_CHUNKS):
                s = p * AG_CHUNKS + ch
                cp = _remote(w_ref.at[pl.ds(c * H + ch * CH, CH), :], rows(k, c, ch), isend.at[s], irecv.at[s], (cx, cy, c))
                cp.start()
                first.append(cp)
        passed = []
        for p, (cx, cy) in enumerate(chips):
            for ch in range(AG_CHUNKS):
                s = p * AG_CHUNKS + ch
                land = rows(2 * cx + cy, c, ch)
                _remote(land, land, isend.at[s], irecv.at[s], (cx, cy, c)).wait_recv()
                fw = _remote(land, land, dsend.at[s], drecv.at[s], sib)
                fw.start()
                passed.append(fw)
        for p, (cx, cy) in enumerate(chips):
            for ch in range(AG_CHUNKS):
                s = p * AG_CHUNKS + ch
                land = rows(2 * cx + cy, 1 - c, ch)
                _remote(land, land, dsend.at[s], drecv.at[s], sib).wait_recv()
        for cp in first + passed:
            cp.wait_send()

    got = pl.pallas_call(
        body, name="all_gather_weights", in_specs=[ANY], out_specs=ANY,
        out_shape=jax.ShapeDtypeStruct((N_CHIPS, R, C), wpk.dtype),
        scratch_shapes=[pltpu.SemaphoreType.DMA((n,))] * 4,
    )(wpk)
    return lax.dynamic_update_slice(got, wpk[None], (2 * lax.axis_index("x") + lax.axis_index("y"), 0, 0))


def _rs_pair_exchange(g):
    _, R, C = g.shape
    H = R // 2

    def body(g_ref, got_ref, ssem, rsem):
        x, y, c = _pos()
        sib = (x, y, 1 - c)
        cps = []
        for kk in range(N_CHIPS):
            cp = _remote(g_ref.at[kk, pl.ds((1 - c) * H, H), :], got_ref.at[kk], ssem.at[kk], rsem.at[kk], sib)
            cp.start()
            cps.append(cp)
        for cp in cps:
            cp.wait()

    return pl.pallas_call(
        body, name="rs_pair_exchange", in_specs=[ANY], out_specs=ANY,
        out_shape=jax.ShapeDtypeStruct((N_CHIPS, H, C), g.dtype),
        scratch_shapes=[pltpu.SemaphoreType.DMA((N_CHIPS,))] * 2,
    )(g)


def _rs_chip_exchange(p):
    _, H, C = p.shape

    def body(p_ref, out_ref, ssem, rsem):
        x, y, c = _pos()
        k = 2 * x + y
        chips = _other_chips(x, y)
        cps = []
        for s, (cx, cy) in enumerate(chips):
            cp = _remote(p_ref.at[2 * cx + cy], out_ref.at[k], ssem.at[s], rsem.at[s], (cx, cy, c))
            cp.start()
            cps.append(cp)
        for s, (cx, cy) in enumerate(chips):
            land = out_ref.at[2 * cx + cy]
            _remote(land, land, ssem.at[s], rsem.at[s], (cx, cy, c)).wait_recv()
        for cp in cps:
            cp.wait_send()

    k = 2 * lax.axis_index("x") + lax.axis_index("y")
    got = pl.pallas_call(
        body, name="rs_chip_exchange", in_specs=[ANY], out_specs=ANY,
        out_shape=jax.ShapeDtypeStruct(p.shape, p.dtype),
        scratch_shapes=[pltpu.SemaphoreType.DMA((3,)), pltpu.SemaphoreType.DMA((3,))],
    )(p)
    return lax.dynamic_update_slice(got, lax.dynamic_slice_in_dim(p, k, 1, axis=0), (k, 0, 0))


def _rs_pair_share(f):
    H, C = f.shape

    def body(f_ref, out_ref, ssem, rsem):
        x, y, c = _pos()
        cp = _remote(f_ref, out_ref.at[c], ssem, rsem, (x, y, 1 - c))
        cp.start()
        land = out_ref.at[1 - c]
        _remote(land, land, ssem, rsem, (x, y, 1 - c)).wait_recv()
        cp.wait_send()

    got = pl.pallas_call(
        body, name="rs_pair_share", in_specs=[ANY], out_specs=ANY,
        out_shape=jax.ShapeDtypeStruct((2, H, C), f.dtype),
        scratch_shapes=[pltpu.SemaphoreType.DMA, pltpu.SemaphoreType.DMA],
    )(f)
    return lax.dynamic_update_slice(got, f[None], (lax.axis_index("c"), 0, 0))


def _all_reduce_small(s):
    r, C = s.shape

    def body(s_ref, o_ref, buf, ssem, rsem):
        x, y, c = _pos()
        me = 4 * x + 2 * y + c
        buf[me] = s_ref[...]
        cps = []
        for m in range(1, N_DEV):
            mx, my, mc = (m >> 2) & 1, (m >> 1) & 1, m & 1
            peer = (x ^ mx, y ^ my, c ^ mc)
            cp = _remote(s_ref, buf.at[me], ssem.at[m - 1], rsem.at[m - 1], peer)
            cp.start()
            cps.append(cp)
        for m in range(1, N_DEV):
            mx, my, mc = (m >> 2) & 1, (m >> 1) & 1, m & 1
            src = 4 * (x ^ mx) + 2 * (y ^ my) + (c ^ mc)
            _remote(s_ref, buf.at[src], ssem.at[m - 1], rsem.at[m - 1], (x ^ mx, y ^ my, c ^ mc)).wait_recv()
        for cp in cps:
            cp.wait_send()
        acc = buf[0]
        for j in range(1, N_DEV):
            acc = acc + buf[j]
        o_ref[...] = acc

    return pl.pallas_call(
        body, name="all_reduce_small", in_specs=[pl.BlockSpec(memory_space=pltpu.VMEM)],
        out_specs=pl.BlockSpec(memory_space=pltpu.VMEM), out_shape=jax.ShapeDtypeStruct((r, C), F32),
        scratch_shapes=[pltpu.VMEM((N_DEV, r, C), F32), pltpu.SemaphoreType.DMA((N_DEV - 1,)), pltpu.SemaphoreType.DMA((N_DEV - 1,))],
    )(s)


def _rtile(n, pref):
    if n <= pref:
        return n
    t = (pref // 16) * 16
    while t >= 16:
        if n % t == 0:
            return t
        t -= 16
    raise ValueError(f"no row tile for {n}")


def _reduce_scatter_grads(gpk):
    _, R, C = gpk.shape
    H = R // 2
    got = _rs_pair_exchange(gpk)
    own = lax.dynamic_index_in_dim(gpk.reshape(N_CHIPS, 2, H, C), lax.axis_index("c"), axis=1, keepdims=False)
    tm = _rtile(N_CHIPS * H, 512)
    (part,) = _rows(lambda i, n, a, b: (a.astype(F32) + b.astype(F32),), N_CHIPS * H, tm,
                    [_cur(own.reshape(N_CHIPS * H, C)), _cur(got.reshape(N_CHIPS * H, C))], [], [_out(C, BF16)], [], "rs_pair_add")
    parts = _rs_chip_exchange(part.reshape(N_CHIPS, H, C)).reshape(N_CHIPS * H, C)
    tm = _rtile(H, 1024)
    hb = H // tm

    def add4(i, n, a, b, c, d):
        return (((a.astype(F32) + b.astype(F32)) + c.astype(F32)) + d.astype(F32),)

    (red,) = _rows(add4, H, tm, [(parts, C, functools.partial(_const, v=0), j * hb) for j in range(N_CHIPS)], [], [_out(C, F32)], [],
                   "rs_chip_add")
    return _rs_pair_share(red).reshape(R, C)


def _adam(w, g, m, v, name, g_row=0):
    shp = w.shape
    two = lambda a: a.reshape(-1, shp[-1])
    rows = math.prod(shp[:-1])
    tm = _rtile(rows, 256)
    assert g_row % tm == 0
    g_in = (two(g), shp[-1], functools.partial(_const, v=0), g_row // tm)
    res = _rows(_k_adam, rows, tm, [_cur(two(w)), g_in, _cur(two(m)), _cur(two(v))], [], [_out(shp[-1], F32)] * 4, [], name)
    return tuple(r.reshape(shp) for r in res)


def _pack_flat(parts, rows):
    flat = jnp.concatenate([p.astype(F32).reshape(-1) for p in parts])
    return jnp.pad(flat, (0, rows * PACK_COLS - flat.shape[0])).reshape(rows, PACK_COLS)


def _unpack_flat(buf, shapes):
    flat, out, off = buf.reshape(-1), [], 0
    for shp in shapes:
        n = math.prod(shp)
        out.append(flat[off:off + n].reshape(shp))
        off += n
    return out


def kernel(x, positions, norm_mix_pre, norm_mix_post, norm_ffn_pre, norm_ffn_post, w_in, mla_q_norm, mla_w_q_up, mla_kv_norm, mla_w_kv_up, sc_conv_w, ssd_conv_w, ssd_conv_b, ssd_dt_bias, ssd_a_log, ssd_d, ssd_norm, w_out, ffn_w_up, ffn_conv_w, ffn_conv_b, ffn_w_down, loss_target, m_norm_mix_pre, m_norm_mix_post, m_norm_ffn_pre, m_norm_ffn_post, m_w_in, m_mla_q_norm, m_mla_w_q_up, m_mla_kv_norm, m_mla_w_kv_up, m_sc_conv_w, m_ssd_conv_w, m_ssd_conv_b, m_ssd_dt_bias, m_ssd_a_log, m_ssd_d, m_ssd_norm, m_w_out, m_ffn_w_up, m_ffn_conv_w, m_ffn_conv_b, m_ffn_w_down, v_norm_mix_pre, v_norm_mix_post, v_norm_ffn_pre, v_norm_ffn_post, v_w_in, v_mla_q_norm, v_mla_w_q_up, v_mla_kv_norm, v_mla_w_kv_up, v_sc_conv_w, v_ssd_conv_w, v_ssd_conv_b, v_ssd_dt_bias, v_ssd_a_log, v_ssd_d, v_ssd_norm, v_w_out, v_ffn_w_up, v_ffn_conv_w, v_ffn_conv_b, v_ffn_w_down):
    a = dict(locals())
    axis = dict(SHARDED)
    layout = _layout({n: a[n].shape for n in axis})

    def weight_piece(n, l, lo):
        w = a[n] if l is None else a[n][l]
        return w - w.astype(BF16).astype(F32) if lo else w

    W, taps = {n: [None] * DEPTH for n in axis}, {}
    for buf, (width, rows, ents) in layout.items():
        gathered = _all_gather_weights(_pack(layout[buf], weight_piece, BF16))
        for n, l, lo, ps, off, r in ents:
            parts = [_unslab(gathered[k, off:off + r], ps) for k in range(N_CHIPS)]
            if l is None:
                taps[n, lo] = jnp.concatenate(parts, axis=axis[n]).astype(F32)
            else:
                W[n][l] = jnp.concatenate(parts, axis=axis[n] - 1)
    for n, lo in taps:
        if not lo:
            W[n] = taps[n, False] + taps[n, True]
    S = {n: a[n] for n in SMALL}

    loss_part, gx, GW, GS = _local_step(a["x"][0], a["positions"][0], a["loss_target"][0], W, S)

    def chip_part(g, ax, k):
        if isinstance(g, tuple):
            g, k = g[k // 2], k % 2
            size = g.shape[ax - 1] // 2
        else:
            size = g.shape[ax - 1] // N_CHIPS
        return lax.slice_in_dim(g, k * size, (k + 1) * size, axis=ax - 1)

    def grad_piece(k):
        def piece(n, l, lo):
            if lo:
                return None
            if l is None:
                return jnp.stack([chip_part(GW[n][ll], axis[n], k) for ll in range(DEPTH)])
            return chip_part(GW[n][l], axis[n], k)
        return piece

    grads, delta, new_m, new_v = {}, {}, {}, {}
    for buf, (width, rows, ents) in layout.items():
        red = _reduce_scatter_grads(jnp.stack([_pack(layout[buf], grad_piece(k), BF16) for k in range(N_CHIPS)]))
        for n in (MAIN if buf == "main" else WIDE):
            mine = [e for e in ents if e[0] == n and not e[2]]
            if a[n].shape[-1] == width and all(e[3] == (e[5], width) for e in mine):
                g, g_row = red, mine[0][4]
            elif mine[0][1] is None:
                g, g_row = _unslab(red[mine[0][4]:mine[0][4] + mine[0][5]], mine[0][3]), 0
            else:
                g, g_row = jnp.stack([_unslab(red[e[4]:e[4] + e[5]], e[3]) for e in mine]), 0
            grads[n], delta[n], new_m[n], new_v[n] = _adam(a[n], g, a["m_" + n], a["v_" + n], "adamw_" + n, g_row)

    small_shapes = [a[n].shape for n in SMALL]
    rs = -(-(sum(math.prod(s) for s in small_shapes) + 1) // (PACK_COLS * SLAB_ALIGN)) * SLAB_ALIGN
    red = _all_reduce_small(_pack_flat([GS[n] for n in SMALL] + [loss_part.reshape(1)], rs))
    loss = _unpack_flat(red, small_shapes + [(1,)])[-1][0]
    pk = lambda pre: _pack_flat([a[pre + n] for n in SMALL], rs)
    for dst, buf in zip((grads, delta, new_m, new_v), _adam(pk(""), red, pk("m_"), pk("v_"), "adamw_small")):
        dst.update(zip(SMALL, _unpack_flat(buf, small_shapes)))

    return (loss, gx[None], *[grads[n] for n in WEIGHTS], *[delta[n] for n in WEIGHTS], *[new_m[n] for n in WEIGHTS],
            *[new_v[n] for n in WEIGHTS])
```

```python
import functools
import math

import jax
import jax.numpy as jnp
from jax import lax
from jax.experimental import pallas as pl
from jax.experimental.pallas import tpu as pltpu

F32 = jnp.float32
BF16 = jnp.bfloat16
MXU_DTYPE = jnp.bfloat16
HIGHEST = lax.Precision.HIGHEST
MESH = pl.DeviceIdType.MESH

D_MODEL = 1024
DEPTH = 4
HEADS = 8
Q_LORA = 256
KV_LORA = 128
NOPE = 64
ROPE = 32
VDIM = 64
ROPE_THETA = 10000.0
SC_DIM = 256
SSD_HEADS = 4
SSD_HEAD_DIM = 64
SSD_STATE = 128
SSD_DIM = 256
SSD_CONV_DIM = 768
SSD_CHUNK = 128
FFN_DIM = 2816
NORM_EPS = 1e-6
QK_SCALE = (NOPE + ROPE) ** -0.5
LANE = 128
HP = 128
FLASH_HEADS = 2

ZIN = 2560
Z_CQ, Z_CKV, Z_KR, Z_SCB, Z_SCC, Z_SCH, Z_SSZ, Z_XBC, Z_DT = 0, 256, 384, 512, 768, 1024, 1280, 1536, 2304
KR_LANE = 64
YCAT = HEADS * HP + SC_DIM + SSD_DIM
FFN_TILE = 256

ADAM_LR, ADAM_B1, ADAM_B2, ADAM_EPS, ADAM_WD, ADAM_STEP = 0.001, 0.9, 0.999, 1e-08, 0.01, 10

PACK_COLS = 1024


def _tile(n, pref):
    if n <= pref:
        return n
    t = (pref // LANE) * LANE
    while t >= LANE:
        if n % t == 0:
            return t
        t -= LANE
    raise ValueError(f"no tile for {n}")


MM_TM, MM_TN, MM_TK = 1024, 1408, 1536


def _mm(a, b, mode, out_dtype, name, tm=None, tn=MM_TN, tkmax=MM_TK):
    pair = isinstance(a, tuple)
    a_list = list(a) if pair else [a]
    if mode == "nn":
        (M, Ka), (_, N) = a_list[0].shape, b.shape
    elif mode == "nt":
        (M, Ka), (N, _) = a_list[0].shape, b.shape
    else:
        (Ka, M), (_, N) = a_list[0].shape, b.shape
    tm = (MM_TN if mode == "tn" else MM_TM) if tm is None else tm
    tm, tn, tk = _tile(M, tm), _tile(N, tn), _tile(Ka, tkmax)
    nka = Ka // tk
    nk = nka * len(a_list)
    if mode == "nn":
        a_specs = [pl.BlockSpec((tm, tk), lambda i, j, k: (i, jnp.minimum(k, nka - 1))),
                   pl.BlockSpec((tm, tk), lambda i, j, k: (i, jnp.maximum(k - nka, 0)))][:len(a_list)]
        b_spec = pl.BlockSpec((tk, tn), lambda i, j, k: (k, j))
        dims = NN
    elif mode == "nt":
        a_specs = [pl.BlockSpec((tm, tk), lambda i, j, k: (i, jnp.minimum(k, nka - 1))),
                   pl.BlockSpec((tm, tk), lambda i, j, k: (i, jnp.maximum(k - nka, 0)))][:len(a_list)]
        b_spec = pl.BlockSpec((tn, tk), lambda i, j, k: (j, k))
        dims = NT
    else:
        a_specs = [pl.BlockSpec((tk, tm), lambda i, j, k: (k, i))]
        b_spec = pl.BlockSpec((tk, tn), lambda i, j, k: (k, j))
        dims = TN
    na = len(a_list)

    def body(*refs):
        a_refs, b_ref, o_ref = refs[:na], refs[na], refs[na + 1]
        k = pl.program_id(2)

        def prod(a_ref):
            return lax.dot_general(a_ref[...].astype(MXU_DTYPE), b_ref[...].astype(MXU_DTYPE), dims, preferred_element_type=F32)

        if nk == 1:
            o_ref[...] = prod(a_refs[0]).astype(o_ref.dtype)
            return
        acc_ref = refs[na + 2]

        @pl.when(k == 0)
        def _():
            acc_ref[...] = prod(a_refs[0])

        @pl.when((k > 0) & (k < nka))
        def _():
            acc_ref[...] += prod(a_refs[0])

        if pair:
            @pl.when(k >= nka)
            def _():
                acc_ref[...] += prod(a_refs[1])

        @pl.when(k == nk - 1)
        def _():
            o_ref[...] = acc_ref[...].astype(o_ref.dtype)

    return pl.pallas_call(
        body, name=name, grid=(M // tm, N // tn, nk),
        in_specs=a_specs + [b_spec], out_specs=pl.BlockSpec((tm, tn), lambda i, j, k: (i, j)),
        out_shape=jax.ShapeDtypeStruct((M, N), out_dtype),
        scratch_shapes=[pltpu.VMEM((tm, tn), F32)] if nk > 1 else [],
        compiler_params=pltpu.CompilerParams(dimension_semantics=("parallel", "parallel", "arbitrary")),
    )(*a_list, b)


HALO = 8


def _const(j, v):
    return v


def _rows(fn, T, tm, ins, consts, outs, accs, name, ncol=1):
    n = T // tm
    hb = tm // HALO
    last = T // HALO - 1
    in_specs, args = [], []
    for arr, bc, cb, kind in ins:
        if isinstance(kind, int):
            in_specs.append(pl.BlockSpec((tm, bc), lambda j, i, cb=cb, off=kind: (i + off, cb(j))))
        elif kind == "cur":
            in_specs.append(pl.BlockSpec((tm, bc), lambda j, i, cb=cb: (i, cb(j))))
        elif kind == "prev":
            in_specs.append(pl.BlockSpec((HALO, bc), lambda j, i, cb=cb: (jnp.maximum(i * hb - 1, 0), cb(j))))
        else:
            in_specs.append(pl.BlockSpec((HALO, bc), lambda j, i, cb=cb: (jnp.minimum((i + 1) * hb, last), cb(j))))
        args.append(arr)
    for arr, bc, cb in consts:
        in_specs.append(pl.BlockSpec((arr.shape[0], bc), lambda j, i, cb=cb: (0, cb(j))))
        args.append(arr)
    out_specs, out_shape = [], []
    for tc, dt, bc, cb in outs:
        out_specs.append(pl.BlockSpec((tm, bc), lambda j, i, cb=cb: (i, cb(j))))
        out_shape.append(jax.ShapeDtypeStruct((T, tc), dt))
    for r, tc, bc, cb in accs:
        out_specs.append(pl.BlockSpec((r, bc), lambda j, i, cb=cb: (0, cb(j))))
        out_shape.append(jax.ShapeDtypeStruct((r, tc), F32))
    nin, nout, nacc = len(args), len(outs), len(accs)

    def body(*refs):
        i = pl.program_id(1)
        res = fn(i, n, *[r[...] for r in refs[:nin]])
        for r, v in zip(refs[nin:nin + nout], res[:nout]):
            r[...] = v.astype(r.dtype)
        if nacc:
            acc_refs = refs[nin + nout:nin + nout + nacc]

            @pl.when(i == 0)
            def _():
                for r in acc_refs:
                    r[...] = jnp.zeros_like(r)

            for r, v in zip(acc_refs, res[nout:]):
                r[...] += v.astype(F32)

    res = pl.pallas_call(
        body, name=name, grid=(ncol, n), in_specs=in_specs, out_specs=out_specs, out_shape=out_shape,
        compiler_params=pltpu.CompilerParams(dimension_semantics=("arbitrary", "arbitrary")),
    )(*args)
    return res


def _cur(arr, bc=None, blk=0):
    bc = arr.shape[1] if bc is None else bc
    return (arr, bc, functools.partial(_const, v=blk), "cur")


def _halo(arr, kind, bc=None, blk=0):
    bc = arr.shape[1] if bc is None else bc
    return (arr, bc, functools.partial(_const, v=blk), kind)


def _cst(arr):
    return (arr, arr.shape[1], functools.partial(_const, v=0))


def _out(cols, dt):
    return (cols, dt, cols, functools.partial(_const, v=0))


def _acc(rows, cols):
    return (rows, cols, cols, functools.partial(_const, v=0))


def _rms(x, w):
    return x * lax.rsqrt(jnp.mean(x * x, axis=-1, keepdims=True) + NORM_EPS) * w


def _sigmoid(x):
    return 0.5 * jnp.tanh(0.5 * x) + 0.5


def _silu(x):
    return x * _sigmoid(x)


def _dsilu(x):
    s = _sigmoid(x)
    return s * (1.0 + x * (1.0 - s))


def _softplus(x):
    return jnp.maximum(x, 0.0) + jnp.log1p(jnp.exp(-jnp.abs(x)))


def _shift(a, k):
    return pltpu.roll(a, k % a.shape[0], 0)


def _lroll(a, k):
    return pltpu.roll(a, k % a.shape[1], 1)


def _vjp_wrap(f, nrow, nconst, add_first=False):
    def g(i, n, *vals):
        rows, consts, mid = vals[:nrow], vals[len(vals) - nconst:], vals[nrow:len(vals) - nconst]
        cots = mid[:-1] if add_first else mid
        outs, pull = jax.vjp(f, *rows, *consts)
        grads = list(pull(tuple(c.astype(o.dtype) for c, o in zip(cots, outs))))
        if add_first:
            grads[0] = grads[0] + mid[-1]
        return tuple(grads)
    return g


def _rows_vjp(f, T, tm, rows, consts, cots, out_dtypes, name):
    return _rows(_vjp_wrap(f, len(rows), len(consts)), T, tm, [_cur(r) for r in rows] + [_cur(c) for c in cots],
                 [_cst(c) for c in consts], [_out(r.shape[1], dt) for r, dt in zip(rows, out_dtypes)],
                 [_acc(1, c.shape[1]) for c in consts], name)


def _f_premix(x, g):
    return (_rms(x, g),)


def _f_mla_pre(cq, ckv, qn, kvn):
    return _rms(cq, qn), _rms(ckv, kvn)


def _f_ssd_gate(y, z, nw):
    return (_rms(y * _silu(z), nw),)


def _f_post_mix(x, mixed, gpost, gffn):
    x1 = x + _rms(mixed, gpost)
    return x1, _rms(x1, gffn)


def _f_post_ffn(x1, d, gpost):
    return (x1 + _rms(d, gpost),)


def _rope_fwd(v, cosf, sina, sinb):
    return v * cosf + _lroll(v, -16) * sina + _lroll(v, 16) * sinb


def _rope_bwd(g, cosf, sina, sinb):
    return g * cosf + _lroll(g * sina, 16) + _lroll(g * sinb, -16)


def _k_rope_fwd(i, n, qpad, kvpad, kr, cosf, sina, sinb):
    qs, ks = [], []
    krr = _rope_fwd(kr, cosf, sina, sinb)
    for h in range(HEADS):
        sl = slice(h * HP, (h + 1) * HP)
        qs.append(_rope_fwd(qpad[:, sl], cosf, sina, sinb))
        ks.append(kvpad[:, sl].astype(F32) + krr)
    return jnp.concatenate(qs, axis=1), jnp.concatenate(ks, axis=1)


def _k_rope_bwd(i, n, dq, dk, dv, cosf, sina, sinb):
    lane = lax.broadcasted_iota(jnp.int32, (1, HP), 1)
    rmask = ((lane >= KR_LANE) & (lane < KR_LANE + ROPE)).astype(F32)
    dqs, dks = [], []
    dkr = jnp.zeros((dq.shape[0], HP), F32)
    for h in range(HEADS):
        sl = slice(h * HP, (h + 1) * HP)
        dqs.append(_rope_bwd(dq[:, sl], cosf, sina, sinb))
        dkh = dk[:, sl]
        dkr = dkr + dkh * rmask
        dks.append(dkh * (1.0 - rmask))
    dkr = _rope_bwd(dkr, cosf, sina, sinb) * rmask
    return jnp.concatenate(dqs, axis=1), jnp.concatenate(dks + [dv], axis=1), dkr


def _k_sconv_fwd(i, n, b, c, h, cp, hp, w):
    m = b.shape[0]
    up = jnp.where(i > 0, cp * hp, 0.0)
    ue = jnp.concatenate([up, c * h], axis=0)
    conv = w[2:3] * ue + w[1:2] * _shift(ue, 1) + w[0:1] * _shift(ue, 2)
    return (b * conv[HALO:],)


def _k_sconv_bwd(i, n, b, c, h, dy, cp, hp, bn, dyn, w):
    m = b.shape[0]
    up = jnp.where(i > 0, cp * hp, 0.0)
    ue = jnp.concatenate([up, c * h], axis=0)
    u1, u2 = _shift(ue, 1), _shift(ue, 2)
    conv = (w[2:3] * ue + w[1:2] * u1 + w[0:1] * u2)[HALO:]
    dc_cur = dy * b
    dce = jnp.concatenate([dc_cur, jnp.where(i < n - 1, dyn * bn, 0.0)], axis=0)
    du = (w[2:3] * dce + w[1:2] * _shift(dce, -1) + w[0:1] * _shift(dce, -2))[:m]
    dw = jnp.concatenate([
        jnp.sum(dc_cur * u2[HALO:], axis=0, keepdims=True),
        jnp.sum(dc_cur * u1[HALO:], axis=0, keepdims=True),
        jnp.sum(dc_cur * ue[HALO:], axis=0, keepdims=True),
        jnp.zeros((HALO - 3, b.shape[1]), F32)], axis=0)
    return dy * conv, du * h, du * c, dw


def _conv4(ue, w):
    return w[3:4] * ue + w[2:3] * _shift(ue, 1) + w[1:2] * _shift(ue, 2) + w[0:1] * _shift(ue, 3)


def _k_ssdconv_fwd(i, n, u, up, w, bias):
    ue = jnp.concatenate([jnp.where(i > 0, up, 0.0), u], axis=0)
    return (_silu(_conv4(ue, w)[HALO:] + bias),)


def _k_ssdconv_bwd(i, n, u, dout, up, un, doutn, w, bias):
    m = u.shape[0]
    ue = jnp.concatenate([jnp.where(i > 0, up, 0.0), u, un], axis=0)
    u1, u2, u3 = _shift(ue, 1), _shift(ue, 2), _shift(ue, 3)
    pre = (w[3:4] * ue + w[2:3] * u1 + w[1:2] * u2 + w[0:1] * u3)[HALO:] + bias
    doe = jnp.concatenate([dout, jnp.where(i < n - 1, doutn, 0.0)], axis=0)
    dpre = doe * _dsilu(pre)
    du = (w[3:4] * dpre + w[2:3] * _shift(dpre, -1) + w[1:2] * _shift(dpre, -2) + w[0:1] * _shift(dpre, -3))[:m]
    dp = dpre[:m]
    cur = slice(HALO, HALO + m)
    dw = jnp.concatenate([
        jnp.sum(dp * u3[cur], axis=0, keepdims=True),
        jnp.sum(dp * u2[cur], axis=0, keepdims=True),
        jnp.sum(dp * u1[cur], axis=0, keepdims=True),
        jnp.sum(dp * ue[cur], axis=0, keepdims=True),
        jnp.zeros((HALO - 4, u.shape[1]), F32)], axis=0)
    db = jnp.sum(dp, axis=0, keepdims=True)
    return du, dw, db


def _conv3(ue, w):
    return w[2:3] * ue + w[1:2] * _shift(ue, 1) + w[0:1] * _shift(ue, 2)


def _k_ffnact_fwd(i, n, ug, uu, ugp, uup, wg, wu, bg, bu):
    gate = _conv3(jnp.concatenate([jnp.where(i > 0, ugp, 0.0), ug], axis=0), wg)[HALO:] + bg
    upv = _conv3(jnp.concatenate([jnp.where(i > 0, uup, 0.0), uu], axis=0), wu)[HALO:] + bu
    return (_silu(gate) * upv,)


def _k_ffnact_bwd(i, n, ug, uu, dact, ugp, uup, ugn, uun, dactn, wg, wu, bg, bu):
    m = ug.shape[0]
    cur = slice(HALO, HALO + m)

    def taps(p, c, nx):
        e = jnp.concatenate([jnp.where(i > 0, p, 0.0), c, nx], axis=0)
        return e, _shift(e, 1), _shift(e, 2)

    def back(d, w):
        return (w[2:3] * d + w[1:2] * _shift(d, -1) + w[0:1] * _shift(d, -2))[:m]

    def wgrad(d, t):
        return jnp.concatenate([jnp.sum(d[:m] * t[2][cur], axis=0, keepdims=True), jnp.sum(d[:m] * t[1][cur], axis=0, keepdims=True),
                                jnp.sum(d[:m] * t[0][cur], axis=0, keepdims=True), jnp.zeros((HALO - 3, d.shape[1]), F32)], axis=0)

    tg, tu = taps(ugp, ug, ugn), taps(uup, uu, uun)
    gate = (wg[2:3] * tg[0] + wg[1:2] * tg[1] + wg[0:1] * tg[2])[HALO:] + bg
    upv = (wu[2:3] * tu[0] + wu[1:2] * tu[1] + wu[0:1] * tu[2])[HALO:] + bu
    dae = jnp.concatenate([dact, jnp.where(i < n - 1, dactn, 0.0)], axis=0)
    sg = _sigmoid(gate)
    dg = dae * upv * (sg * (1.0 + gate * (1.0 - sg)))
    dup = dae * (gate * sg)
    return (back(dg, wg), back(dup, wu), wgrad(dg, tg), wgrad(dup, tu),
            jnp.sum(dg[:m], axis=0, keepdims=True), jnp.sum(dup[:m], axis=0, keepdims=True))


def _k_loss(i, n, y, tgt):
    e = y - tgt
    part = 0.5 * jnp.sum(jnp.sum(e * e, axis=1, keepdims=True) / D_MODEL, axis=0, keepdims=True)
    return e * (1.0 / D_MODEL), jnp.broadcast_to(part, (1, LANE))


def _k_adam(i, n, w, g, m, v):
    m = ADAM_B1 * m + (1.0 - ADAM_B1) * g
    v = ADAM_B2 * v + (1.0 - ADAM_B2) * (g * g)
    m_hat = m / (1.0 - ADAM_B1 ** ADAM_STEP)
    v_hat = v / (1.0 - ADAM_B2 ** ADAM_STEP)
    delta = -ADAM_LR * (m_hat / (jnp.sqrt(v_hat) + ADAM_EPS) + ADAM_WD * w)
    return g, delta, m, v


def _dotf(a, b, dims):
    return lax.dot_general(a.astype(MXU_DTYPE), b.astype(MXU_DTYPE), dims, preferred_element_type=F32)


NN = (((1,), (0,)), ((), ()))
NT = (((1,), (1,)), ((), ()))
TN = (((0,), (0,)), ((), ()))


def _ssd_chunk(x0, x1, x2, x3, b0, b1, c0, c1, dtraw, p0, p1, p2, p3, dtb, alog, dsk):
    xs, bs, cs_, ps = (x0, x1, x2, x3), (b0, b1), (c0, c1), (p0, p1, p2, p3)
    L = dtraw.shape[0]
    dt = _softplus(dtraw + dtb)
    adt = dt * (-jnp.exp(alog))
    row = lax.broadcasted_iota(jnp.int32, (L, L), 0)
    col = lax.broadcasted_iota(jnp.int32, (L, L), 1)
    tril = row >= col
    cum = jnp.dot(tril.astype(F32), adt, precision=HIGHEST, preferred_element_type=F32)
    cum_t = cum.T
    lane = lax.broadcasted_iota(jnp.int32, (1, LANE), 1)
    sub = lax.broadcasted_iota(jnp.int32, (LANE, 1), 0)
    lastcol = (lax.broadcasted_iota(jnp.int32, (1, L), 1) == L - 1).astype(F32)
    ys, news = [], []
    for h in range(SSD_HEADS):
        g = h // (SSD_HEADS // 2)
        oh = (lane == h).astype(F32)
        dth = jnp.sum(dt * oh, axis=1, keepdims=True)
        csh = jnp.sum(cum * oh, axis=1, keepdims=True)
        csr = jnp.sum(cum_t * (sub == h).astype(F32), axis=0, keepdims=True)
        cl = jnp.sum(csr * lastcol, axis=1, keepdims=True)
        dskh = jnp.sum(dsk * oh, axis=1, keepdims=True)
        x, bm, cm, prev = xs[h], bs[g], cs_[g], ps[h]
        xdt = x * dth
        decay = jnp.exp(jnp.where(tril, csh - csr, -jnp.inf))
        scores = _dotf(cm, bm, NT) * decay
        y_diag = _dotf(scores, xdt, NN)
        bd = bm * jnp.exp(cl - csh)
        cst = _dotf(xdt, bd, TN)
        news.append(prev * jnp.exp(cl) + cst)
        y_off = _dotf(cm, prev, NT) * jnp.exp(csh)
        ys.append(y_diag + y_off + x * dskh)
    return (*ys, *news)


def _ssd_operands(x_ref, dt_ref, par_ref, prev):
    xs = [x_ref[:, h * SSD_HEAD_DIM:(h + 1) * SSD_HEAD_DIM] for h in range(SSD_HEADS)]
    bs = [x_ref[:, SSD_DIM + g * SSD_STATE:SSD_DIM + (g + 1) * SSD_STATE] for g in range(2)]
    cs_ = [x_ref[:, SSD_DIM + 2 * SSD_STATE + g * SSD_STATE:SSD_DIM + 2 * SSD_STATE + (g + 1) * SSD_STATE] for g in range(2)]
    return (*xs, *bs, *cs_, dt_ref[...], *prev, par_ref[0:1, :], par_ref[1:2, :], par_ref[2:3, :])


def _ssd_fwd(xbc, dtraw, par, T):
    L = SSD_CHUNK
    nc = T // L
    P = SSD_HEAD_DIM

    def body(x_ref, dt_ref, par_ref, y_ref, st_ref, state):
        @pl.when(pl.program_id(0) == 0)
        def _():
            state[...] = jnp.zeros_like(state)

        st_ref[0] = state[...]
        prev = [state[h * P:(h + 1) * P, :] for h in range(SSD_HEADS)]
        res = _ssd_chunk(*_ssd_operands(x_ref, dt_ref, par_ref, prev))
        for h in range(SSD_HEADS):
            y_ref[:, h * P:(h + 1) * P] = res[h]
            state[h * P:(h + 1) * P, :] = res[SSD_HEADS + h]

    return pl.pallas_call(
        body, name="ssd_scan_fwd", grid=(nc,),
        in_specs=[pl.BlockSpec((L, SSD_CONV_DIM), lambda c: (c, 0)), pl.BlockSpec((L, LANE), lambda c: (c, 0)),
                  pl.BlockSpec((8, LANE), lambda c: (0, 0))],
        out_specs=[pl.BlockSpec((L, SSD_DIM), lambda c: (c, 0)), pl.BlockSpec((1, SSD_DIM, SSD_STATE), lambda c: (c, 0, 0))],
        out_shape=[jax.ShapeDtypeStruct((T, SSD_DIM), F32), jax.ShapeDtypeStruct((nc, SSD_DIM, SSD_STATE), F32)],
        scratch_shapes=[pltpu.VMEM((SSD_DIM, SSD_STATE), F32)],
        compiler_params=pltpu.CompilerParams(dimension_semantics=("arbitrary",)),
    )(xbc, dtraw, par)


def _ssd_bwd(xbc, dtraw, par, states, dy, T):
    L = SSD_CHUNK
    nc = T // L
    P = SSD_HEAD_DIM

    def body(x_ref, dt_ref, par_ref, st_ref, dy_ref, dx_ref, ddt_ref, dpar_ref, dstate):
        @pl.when(pl.program_id(0) == 0)
        def _():
            dstate[...] = jnp.zeros_like(dstate)
            dpar_ref[...] = jnp.zeros_like(dpar_ref)

        prev = [st_ref[0, h * P:(h + 1) * P, :] for h in range(SSD_HEADS)]
        prim = _ssd_operands(x_ref, dt_ref, par_ref, prev)
        _, pull = jax.vjp(_ssd_chunk, *prim)
        cots = tuple(dy_ref[:, h * P:(h + 1) * P] for h in range(SSD_HEADS)) + tuple(
            dstate[h * P:(h + 1) * P, :] for h in range(SSD_HEADS))
        g = pull(cots)
        for h in range(SSD_HEADS):
            dx_ref[:, h * P:(h + 1) * P] = g[h]
            dstate[h * P:(h + 1) * P, :] = g[9 + h]
        for k in range(2):
            dx_ref[:, SSD_DIM + k * SSD_STATE:SSD_DIM + (k + 1) * SSD_STATE] = g[4 + k]
            dx_ref[:, SSD_DIM + 2 * SSD_STATE + k * SSD_STATE:SSD_DIM + 2 * SSD_STATE + (k + 1) * SSD_STATE] = g[6 + k]
        ddt_ref[...] = g[8]
        for r in range(3):
            dpar_ref[r:r + 1, :] += g[13 + r]

    rev = lambda c: (nc - 1 - c, 0)
    return pl.pallas_call(
        body, name="ssd_scan_bwd", grid=(nc,),
        in_specs=[pl.BlockSpec((L, SSD_CONV_DIM), rev), pl.BlockSpec((L, LANE), rev), pl.BlockSpec((8, LANE), lambda c: (0, 0)),
                  pl.BlockSpec((1, SSD_DIM, SSD_STATE), lambda c: (nc - 1 - c, 0, 0)), pl.BlockSpec((L, SSD_DIM), rev)],
        out_specs=[pl.BlockSpec((L, SSD_CONV_DIM), rev), pl.BlockSpec((L, LANE), rev), pl.BlockSpec((8, LANE), lambda c: (0, 0))],
        out_shape=[jax.ShapeDtypeStruct((T, SSD_CONV_DIM), F32), jax.ShapeDtypeStruct((T, LANE), F32),
                   jax.ShapeDtypeStruct((8, LANE), F32)],
        scratch_shapes=[pltpu.VMEM((SSD_DIM, SSD_STATE), F32)],
        compiler_params=pltpu.CompilerParams(dimension_semantics=("arbitrary",)),
    )(xbc, dtraw, par, states, dy)


def _causal_pairs(nq, by_query):
    if by_query:
        pairs = [(i, j) for i in range(nq) for j in range(i + 1)]
    else:
        pairs = [(i, j) for j in range(nq) for i in range(j, nq)]
    return jnp.asarray([p[0] for p in pairs], jnp.int32), jnp.asarray([p[1] for p in pairs], jnp.int32)


def _flash_fwd(q, k, kv, T):
    tq = tk = min(512, T)
    nq = T // tq
    G = FLASH_HEADS
    rep = tk // HP

    def body(qi_ref, kj_ref, q_ref, k_ref, v_ref, o_ref, m_ref, l_ref, acc_ref):
        t = pl.program_id(1)
        i, j = qi_ref[t], kj_ref[t]

        @pl.when(j == 0)
        def _():
            m_ref[...] = jnp.full_like(m_ref, -jnp.inf)
            l_ref[...] = jnp.zeros_like(l_ref)
            acc_ref[...] = jnp.zeros_like(acc_ref)

        def step(diagonal):
            for g in range(G):
                sl = slice(g * HP, (g + 1) * HP)
                s = _dotf(q_ref[:, sl], k_ref[:, sl], NT) * QK_SCALE
                if diagonal:
                    rows = lax.broadcasted_iota(jnp.int32, (tq, tk), 0)
                    cols = lax.broadcasted_iota(jnp.int32, (tq, tk), 1)
                    s = jnp.where(rows >= cols, s, -jnp.inf)
                m_old = m_ref[:, sl]
                m_new = jnp.maximum(m_old, jnp.max(s, axis=1, keepdims=True))
                p = jnp.exp(s - jnp.tile(m_new, (1, rep)))
                alpha = jnp.exp(m_old - m_new)
                l_ref[:, sl] = alpha * l_ref[:, sl] + jnp.sum(p, axis=1, keepdims=True)
                acc_ref[:, sl] = alpha * acc_ref[:, sl] + _dotf(p, v_ref[:, sl], NN)
                m_ref[:, sl] = m_new

        @pl.when(j < i)
        def _():
            step(False)

        @pl.when(j == i)
        def _():
            step(True)
            lane = lax.broadcasted_iota(jnp.int32, (tq, HP), 1)
            for g in range(G):
                sl = slice(g * HP, (g + 1) * HP)
                l = l_ref[:, sl]
                o_ref[:, sl] = jnp.where(lane < VDIM, acc_ref[:, sl] / l, m_ref[:, sl] + jnp.log(l))

    W = G * HP
    qi, kj = _causal_pairs(nq, by_query=True)
    return pl.pallas_call(
        body, name="mla_flash_fwd",
        grid_spec=pltpu.PrefetchScalarGridSpec(
            num_scalar_prefetch=2, grid=(HEADS // G, qi.shape[0]),
            in_specs=[pl.BlockSpec((tq, W), lambda h, t, qi, kj: (qi[t], h)),
                      pl.BlockSpec((tk, W), lambda h, t, qi, kj: (kj[t], h)),
                      pl.BlockSpec((tk, W), lambda h, t, qi, kj: (kj[t], HEADS // G + h))],
            out_specs=pl.BlockSpec((tq, W), lambda h, t, qi, kj: (qi[t], h)),
            scratch_shapes=[pltpu.VMEM((tq, W), F32), pltpu.VMEM((tq, W), F32), pltpu.VMEM((tq, W), F32)]),
        out_shape=jax.ShapeDtypeStruct((T, HEADS * HP), F32),
        compiler_params=pltpu.CompilerParams(dimension_semantics=("parallel", "arbitrary")),
    )(qi, kj, q, k, kv)


def _flash_bwd(q, k, kv, o, dycat, T):
    tq = tk = min(512, T)
    nq = T // tq
    G = FLASH_HEADS

    def body(qi_ref, kj_ref, q_ref, k_ref, v_ref, o_ref, do_ref, dq_ref, dk_ref, dv_ref):
        t = pl.program_id(1)
        i, j = qi_ref[t], kj_ref[t]

        @pl.when(t == 0)
        def _():
            dq_ref[...] = jnp.zeros_like(dq_ref)

        @pl.when(i == j)
        def _():
            dk_ref[...] = jnp.zeros_like(dk_ref)
            dv_ref[...] = jnp.zeros_like(dv_ref)

        def step(diagonal):
            r0 = pl.multiple_of(i * tq, tq)
            for g in range(G):
                sl = slice(g * HP, (g + 1) * HP)
                qv, kv, vv, ov, dov = q_ref[:, sl], k_ref[:, sl], v_ref[:, sl], o_ref[:, sl], do_ref[:, sl]
                s = _dotf(qv, kv, NT) * QK_SCALE
                p = jnp.exp(s - ov[:, VDIM:VDIM + 1])
                if diagonal:
                    rows = lax.broadcasted_iota(jnp.int32, (tq, tk), 0)
                    cols = lax.broadcasted_iota(jnp.int32, (tq, tk), 1)
                    p = jnp.where(rows >= cols, p, 0.0)
                dsum = jnp.sum(dov * ov, axis=1, keepdims=True)
                dv_ref[:, sl] += _dotf(p, dov, TN)
                dp = _dotf(dov, vv, NT)
                ds = p * (dp - dsum) * QK_SCALE
                dk_ref[:, sl] += _dotf(ds, qv, TN)
                dq_ref[pl.ds(r0, tq), sl] += _dotf(ds, kv, NN)

        @pl.when(i > j)
        def _():
            step(False)

        @pl.when(i == j)
        def _():
            step(True)

    W = G * HP
    qi, kj = _causal_pairs(nq, by_query=False)
    qmap = lambda h, t, qi, kj: (qi[t], h)
    kmap = lambda h, t, qi, kj: (kj[t], h)
    vmap = lambda h, t, qi, kj: (kj[t], HEADS // G + h)
    return pl.pallas_call(
        body, name="mla_flash_bwd",
        grid_spec=pltpu.PrefetchScalarGridSpec(
            num_scalar_prefetch=2, grid=(HEADS // G, qi.shape[0]),
            in_specs=[pl.BlockSpec((tq, W), qmap), pl.BlockSpec((tk, W), kmap), pl.BlockSpec((tk, W), vmap),
                      pl.BlockSpec((tq, W), qmap), pl.BlockSpec((tq, W), qmap)],
            out_specs=[pl.BlockSpec((T, W), lambda h, t, qi, kj: (0, h)), pl.BlockSpec((tk, W), kmap), pl.BlockSpec((tk, W), kmap)]),
        out_shape=[jax.ShapeDtypeStruct((T, HEADS * HP), F32)] * 3,
        compiler_params=pltpu.CompilerParams(dimension_semantics=("parallel", "arbitrary")),
    )(qi, kj, q, k, kv, o, dycat)


_IN_SRC = (0, 256, 384, 416, 672, 928, 1184, 1440, 2208, 2212)
_IN_DST = (Z_CQ, Z_CKV, Z_KR + KR_LANE, Z_SCB, Z_SCC, Z_SCH, Z_SSZ, Z_XBC, Z_DT)


def _pad_cols_in(w):
    parts, at = [], 0
    for s0, s1, d0 in zip(_IN_SRC[:-1], _IN_SRC[1:], _IN_DST):
        if d0 > at:
            parts.append(jnp.zeros(w.shape[:-1] + (d0 - at,), w.dtype))
        parts.append(w[..., s0:s1])
        at = d0 + (s1 - s0)
    parts.append(jnp.zeros(w.shape[:-1] + (ZIN - at,), w.dtype))
    return jnp.concatenate(parts, axis=-1)


def _unpad_cols_in(w):
    return jnp.concatenate([w[..., d0:d0 + (s1 - s0)] for s0, s1, d0 in zip(_IN_SRC[:-1], _IN_SRC[1:], _IN_DST)], axis=-1)


def _pad_heads(w, width):
    w = w.reshape(w.shape[:-1] + (HEADS, width))
    w = jnp.pad(w, [(0, 0)] * (w.ndim - 1) + [(0, HP - width)])
    return w.reshape(w.shape[:-2] + (HEADS * HP,))


def _unpad_heads(w, width):
    w = w.reshape(w.shape[:-1] + (HEADS, HP))[..., :width]
    return w.reshape(w.shape[:-2] + (HEADS * width,))


def _pad_kv(w):
    w = w.reshape(w.shape[:-1] + (HEADS, NOPE + VDIM))
    return jnp.concatenate([_pad_heads(w[..., :NOPE].reshape(w.shape[:-2] + (HEADS * NOPE,)), NOPE),
                            _pad_heads(w[..., NOPE:].reshape(w.shape[:-2] + (HEADS * VDIM,)), VDIM)], axis=-1)


def _unpad_kv(w):
    k = _unpad_heads(w[..., :HEADS * HP], NOPE).reshape(w.shape[:-1] + (HEADS, NOPE))
    v = _unpad_heads(w[..., HEADS * HP:], VDIM).reshape(w.shape[:-1] + (HEADS, VDIM))
    return jnp.concatenate([k, v], axis=-1).reshape(w.shape[:-1] + (HEADS * (NOPE + VDIM),))


def _pad_out_rows(w):
    att = jnp.swapaxes(_pad_heads(jnp.swapaxes(w[:HEADS * VDIM], 0, 1), VDIM), 0, 1)
    return jnp.concatenate([att, w[HEADS * VDIM:]], axis=0)


def _unpad_out_rows(w):
    att = jnp.swapaxes(_unpad_heads(jnp.swapaxes(w[:HEADS * HP], 0, 1), VDIM), 0, 1)
    return jnp.concatenate([att, w[HEADS * HP:]], axis=0)


def _row8(*vecs):
    c = vecs[0].shape[-1]
    return jnp.concatenate([v.reshape(1, c).astype(F32) for v in vecs] + [jnp.zeros((8 - len(vecs), c), F32)], axis=0)


def _lanes(v):
    return jnp.pad(v.astype(F32), (0, LANE - v.shape[0])).reshape(1, LANE)


def _rope_tables(positions):
    inv_freq = 1.0 / (ROPE_THETA ** (jnp.arange(0, ROPE, 2, dtype=F32) / ROPE))
    ang = positions.astype(F32)[:, None] * inv_freq
    cos, sin = jnp.cos(ang), jnp.sin(ang)
    T = positions.shape[0]
    half = ROPE // 2
    one = jnp.ones((T, KR_LANE), F32)
    zero = jnp.zeros((T, KR_LANE), F32)
    tail1 = jnp.ones((T, HP - KR_LANE - ROPE), F32)
    tail0 = jnp.zeros((T, HP - KR_LANE - ROPE), F32)
    z16 = jnp.zeros((T, half), F32)
    cosf = jnp.concatenate([one, cos, cos, tail1], axis=1)
    sina = jnp.concatenate([zero, -sin, z16, tail0], axis=1)
    sinb = jnp.concatenate([zero, z16, sin, tail0], axis=1)
    return cosf, sina, sinb


def _layer_weights(W, l):
    c = lambda a: a.astype(MXU_DTYPE)
    return dict(
        w_in=c(_pad_cols_in(W["w_in"][l])),
        w_q=c(_pad_heads(W["mla_w_q_up"][l], NOPE + ROPE)),
        w_kv=c(_pad_kv(W["mla_w_kv_up"][l])),
        w_out=c(_pad_out_rows(W["w_out"][l])),
        w_up=c(W["ffn_w_up"][l]),
        w_down=c(W["ffn_w_down"][l]),
        sc_w=_row8(*W["sc_conv_w"][l].astype(F32)),
        ssd_w=_row8(*W["ssd_conv_w"][l].astype(F32)),
        ffn_w=_row8(*W["ffn_conv_w"][l].astype(F32)),
    )


def _local_step(x, positions, target, W, S):
    T = x.shape[0]
    tm = min(256, T)
    cosf, sina, sinb = _rope_tables(positions)
    saved = []
    xl = x
    for l in range(DEPTH):
        lw = _layer_weights(W, l)
        g_pre = S["norm_mix_pre"][l].reshape(1, -1)
        g_post = S["norm_mix_post"][l].reshape(1, -1)
        g_fpre = S["norm_ffn_pre"][l].reshape(1, -1)
        g_fpost = S["norm_ffn_post"][l].reshape(1, -1)
        qn = S["mla_q_norm"][l].reshape(1, -1)
        kvn = S["mla_kv_norm"][l].reshape(1, -1)
        ssd_b = S["ssd_conv_b"][l].reshape(1, -1)
        ssd_par = _row8(jnp.pad(S["ssd_dt_bias"][l], (0, LANE - SSD_HEADS)), jnp.pad(S["ssd_a_log"][l], (0, LANE - SSD_HEADS)),
                        jnp.pad(S["ssd_d"][l], (0, LANE - SSD_HEADS)))
        ssd_nw = S["ssd_norm"][l].reshape(1, -1)
        ffn_b = S["ffn_conv_b"][l].reshape(1, -1)

        (h1,) = _rows(lambda i, n, *v: _f_premix(*v), T, tm, [_cur(xl)], [_cst(g_pre)], [_out(D_MODEL, BF16)], [], "pre_mix_norm")
        zin = _mm(h1, lw["w_in"], "nn", F32, "mm_in")
        qlat, kvlat = _rows(lambda i, n, *v: _f_mla_pre(*v), T, tm, [_cur(zin, Q_LORA, 0), _cur(zin, KV_LORA, Z_CKV // KV_LORA)],
                            [_cst(qn), _cst(kvn)], [_out(Q_LORA, BF16), _out(KV_LORA, BF16)], [], "mla_pre_norm")
        qpad = _mm(qlat, lw["w_q"], "nn", F32, "mm_q_up")
        kvpad = _mm(kvlat, lw["w_kv"], "nn", BF16, "mm_kv_up")
        qr, kr = _rows(_k_rope_fwd, T, tm, [_cur(qpad), _cur(kvpad, HEADS * HP, 0), _cur(zin, LANE, Z_KR // LANE),
                                            _cur(cosf), _cur(sina), _cur(sinb)], [],
                       [_out(HEADS * HP, BF16), _out(HEADS * HP, BF16)], [], "mla_rope")
        o = _flash_fwd(qr, kr, kvpad, T)
        (yconv,) = _rows(_k_sconv_fwd, T, tm, [_cur(zin, SC_DIM, Z_SCB // SC_DIM), _cur(zin, SC_DIM, Z_SCC // SC_DIM),
                                               _cur(zin, SC_DIM, Z_SCH // SC_DIM), _halo(zin, "prev", SC_DIM, Z_SCC // SC_DIM),
                                               _halo(zin, "prev", SC_DIM, Z_SCH // SC_DIM)], [_cst(lw["sc_w"])],
                         [_out(SC_DIM, F32)], [], "short_conv_fwd")
        (xbc,) = _rows(_k_ssdconv_fwd, T, tm, [_cur(zin, SSD_CONV_DIM, Z_XBC // SSD_CONV_DIM),
                                               _halo(zin, "prev", SSD_CONV_DIM, Z_XBC // SSD_CONV_DIM)],
                       [_cst(lw["ssd_w"]), _cst(ssd_b)], [_out(SSD_CONV_DIM, F32)], [], "ssd_conv_fwd")
        dtraw = zin[:, Z_DT:Z_DT + LANE]
        yscan, states = _ssd_fwd(xbc, dtraw, ssd_par, T)
        (yssd,) = _rows(lambda i, n, *v: _f_ssd_gate(*v), T, tm, [_cur(yscan), _cur(zin, SSD_DIM, Z_SSZ // SSD_DIM)], [_cst(ssd_nw)],
                        [_out(SSD_DIM, F32)], [], "ssd_gate_fwd")
        ycat = jnp.concatenate([o.astype(BF16), yconv.astype(BF16), yssd.astype(BF16)], axis=1)
        mixed = _mm(ycat, lw["w_out"], "nn", F32, "mm_out")
        x1, h2 = _rows(lambda i, n, *v: _f_post_mix(*v), T, tm, [_cur(xl), _cur(mixed)], [_cst(g_post), _cst(g_fpre)],
                       [_out(D_MODEL, F32), _out(D_MODEL, BF16)], [], "post_mix_fwd")
        upre = _mm(h2, lw["w_up"], "nn", F32, "mm_up")
        nt = FFN_DIM // FFN_TILE
        gcol, ucol = (lambda j: j), (lambda j: j + nt)
        (act,) = _rows(_k_ffnact_fwd, T, tm,
                       [(upre, FFN_TILE, gcol, "cur"), (upre, FFN_TILE, ucol, "cur"), (upre, FFN_TILE, gcol, "prev"),
                        (upre, FFN_TILE, ucol, "prev")],
                       [(lw["ffn_w"], FFN_TILE, gcol), (lw["ffn_w"], FFN_TILE, ucol), (ffn_b, FFN_TILE, gcol), (ffn_b, FFN_TILE, ucol)],
                       [(FFN_DIM, BF16, FFN_TILE, gcol)], [], "ffn_act_fwd", ncol=nt)
        dn = _mm(act, lw["w_down"], "nn", F32, "mm_down")
        (x2,) = _rows(lambda i, n, *v: _f_post_ffn(*v), T, tm, [_cur(x1), _cur(dn)], [_cst(g_fpost)], [_out(D_MODEL, F32)], [], "post_ffn_fwd")
        saved.append(dict(lw=lw, x=xl, h1=h1, zin=zin, qlat=qlat, kvlat=kvlat, qr=qr, kr=kr, kvpad=kvpad, o=o, xbc=xbc, dtraw=dtraw,
                          yscan=yscan, states=states, ycat=ycat, mixed=mixed, x1=x1, h2=h2, upre=upre, act=act, dn=dn,
                          g_pre=g_pre, g_post=g_post, g_fpre=g_fpre, g_fpost=g_fpost, qn=qn, kvn=kvn, ssd_b=ssd_b,
                          ssd_par=ssd_par, ssd_nw=ssd_nw, ffn_b=ffn_b))
        xl = x2

    gx, loss_part = _rows(_k_loss, T, tm, [_cur(xl), _cur(target)], [], [_out(D_MODEL, F32)], [_acc(1, LANE)], "loss_head")

    GW = {k: [None] * DEPTH for k in ("w_in", "mla_w_q_up", "mla_w_kv_up", "sc_conv_w", "ssd_conv_w", "w_out", "ffn_w_up",
                                      "ffn_conv_w", "ffn_w_down")}
    GS = {k: [None] * DEPTH for k in ("norm_mix_pre", "norm_mix_post", "norm_ffn_pre", "norm_ffn_post", "mla_q_norm", "mla_kv_norm",
                                      "ssd_conv_b", "ssd_dt_bias", "ssd_a_log", "ssd_d", "ssd_norm", "ffn_conv_b")}
    nt = FFN_DIM // FFN_TILE
    gcol, ucol = (lambda j: j), (lambda j: j + nt)
    for l in reversed(range(DEPTH)):
        s = saved[l]
        lw = s["lw"]
        gx1, ddn, dgf = _rows_vjp(_f_post_ffn, T, tm, [s["x1"], s["dn"]], [s["g_fpost"]], [gx], [F32, BF16], "post_ffn_bwd")
        GS["norm_ffn_post"][l] = dgf[0]
        dact = _mm(ddn, lw["w_down"], "nt", F32, "mm_down_dx")
        GW["ffn_w_down"][l] = _mm(s["act"], ddn, "tn", BF16, "mm_down_dw")
        up = s["upre"]
        dug, duu, dwg, dwu, dbg, dbu = _rows(
            _k_ffnact_bwd, T, tm,
            [(up, FFN_TILE, gcol, "cur"), (up, FFN_TILE, ucol, "cur"), (dact, FFN_TILE, gcol, "cur"), (up, FFN_TILE, gcol, "prev"),
             (up, FFN_TILE, ucol, "prev"), (up, FFN_TILE, gcol, "next"), (up, FFN_TILE, ucol, "next"), (dact, FFN_TILE, gcol, "next")],
            [(lw["ffn_w"], FFN_TILE, gcol), (lw["ffn_w"], FFN_TILE, ucol), (s["ffn_b"], FFN_TILE, gcol), (s["ffn_b"], FFN_TILE, ucol)],
            [(FFN_DIM, BF16, FFN_TILE, gcol)] * 2,
            [(HALO, FFN_DIM, FFN_TILE, gcol)] * 2 + [(1, FFN_DIM, FFN_TILE, gcol)] * 2, "ffn_act_bwd", ncol=nt)
        GW["ffn_conv_w"][l] = jnp.concatenate([dwg[:3], dwu[:3]], axis=1)
        GS["ffn_conv_b"][l] = jnp.concatenate([dbg[0], dbu[0]])
        dh2 = _mm((dug, duu), lw["w_up"], "nt", F32, "mm_up_dx")
        GW["ffn_w_up"][l] = (_mm(s["h2"], dug, "tn", BF16, "mm_up_dw_gate"), _mm(s["h2"], duu, "tn", BF16, "mm_up_dw_up"))
        gx0, dmixed, dgp, dgf = _rows_vjp(_f_post_mix, T, tm, [s["x"], s["mixed"]], [s["g_post"], s["g_fpre"]], [gx1, dh2],
                                          [F32, BF16], "post_mix_bwd")
        GS["norm_mix_post"][l], GS["norm_ffn_pre"][l] = dgp[0], dgf[0]
        dycat = _mm(dmixed, lw["w_out"], "nt", F32, "mm_out_dx")
        GW["w_out"][l] = _unpad_out_rows(_mm(s["ycat"], dmixed, "tn", BF16, "mm_out_dw"))
        zin = s["zin"]
        dyscan, dz, dnw = _rows(_vjp_wrap(_f_ssd_gate, 2, 1), T, tm,
                                [_cur(s["yscan"]), _cur(zin, SSD_DIM, Z_SSZ // SSD_DIM), _cur(dycat, SSD_DIM, (HEADS * HP + SC_DIM) // SSD_DIM)],
                                [_cst(s["ssd_nw"])], [_out(SSD_DIM, F32), _out(SSD_DIM, BF16)], [_acc(1, SSD_DIM)], "ssd_gate_bwd")
        GS["ssd_norm"][l] = dnw[0]
        dxbc, ddtraw, dpar = _ssd_bwd(s["xbc"], s["dtraw"], s["ssd_par"], s["states"], dyscan, T)
        GS["ssd_dt_bias"][l], GS["ssd_a_log"][l], GS["ssd_d"][l] = dpar[0, :SSD_HEADS], dpar[1, :SSD_HEADS], dpar[2, :SSD_HEADS]
        xb = Z_XBC // SSD_CONV_DIM
        dxraw, dsw, dsb = _rows(_k_ssdconv_bwd, T, tm,
                                [_cur(zin, SSD_CONV_DIM, xb), _cur(dxbc), _halo(zin, "prev", SSD_CONV_DIM, xb),
                                 _halo(zin, "next", SSD_CONV_DIM, xb), _halo(dxbc, "next")],
                                [_cst(lw["ssd_w"]), _cst(s["ssd_b"])], [_out(SSD_CONV_DIM, BF16)],
                                [_acc(HALO, SSD_CONV_DIM), _acc(1, SSD_CONV_DIM)], "ssd_conv_bwd")
        GW["ssd_conv_w"][l] = dsw[:4]
        GS["ssd_conv_b"][l] = dsb[0]
        cb = (HEADS * HP) // SC_DIM
        dscb, dscc, dsch, dscw = _rows(_k_sconv_bwd, T, tm,
                                       [_cur(zin, SC_DIM, Z_SCB // SC_DIM), _cur(zin, SC_DIM, Z_SCC // SC_DIM),
                                        _cur(zin, SC_DIM, Z_SCH // SC_DIM), _cur(dycat, SC_DIM, cb),
                                        _halo(zin, "prev", SC_DIM, Z_SCC // SC_DIM), _halo(zin, "prev", SC_DIM, Z_SCH // SC_DIM),
                                        _halo(zin, "next", SC_DIM, Z_SCB // SC_DIM), _halo(dycat, "next", SC_DIM, cb)],
                                       [_cst(lw["sc_w"])], [_out(SC_DIM, BF16)] * 3, [_acc(HALO, SC_DIM)], "short_conv_bwd")
        GW["sc_conv_w"][l] = dscw[:3]
        dq, dk, dv = _flash_bwd(s["qr"], s["kr"], s["kvpad"], s["o"], dycat, T)
        dqpad, dkvpad, dkr = _rows(_k_rope_bwd, T, tm, [_cur(dq), _cur(dk), _cur(dv), _cur(cosf), _cur(sina), _cur(sinb)], [],
                                   [_out(HEADS * HP, BF16), _out(2 * HEADS * HP, BF16), _out(LANE, BF16)], [], "mla_rope_bwd")
        dqlat = _mm(dqpad, lw["w_q"], "nt", F32, "mm_q_dx")
        GW["mla_w_q_up"][l] = _unpad_heads(_mm(s["qlat"], dqpad, "tn", BF16, "mm_q_dw"), NOPE + ROPE)
        dkvlat = _mm(dkvpad, lw["w_kv"], "nt", F32, "mm_kv_dx")
        GW["mla_w_kv_up"][l] = _unpad_kv(_mm(s["kvlat"], dkvpad, "tn", BF16, "mm_kv_dw"))
        dcq, dckv, dqn, dkvn = _rows(_vjp_wrap(_f_mla_pre, 2, 2), T, tm,
                                     [_cur(zin, Q_LORA, 0), _cur(zin, KV_LORA, Z_CKV // KV_LORA), _cur(dqlat), _cur(dkvlat)],
                                     [_cst(s["qn"]), _cst(s["kvn"])], [_out(Q_LORA, BF16), _out(KV_LORA, BF16)],
                                     [_acc(1, Q_LORA), _acc(1, KV_LORA)], "mla_pre_bwd")
        GS["mla_q_norm"][l], GS["mla_kv_norm"][l] = dqn[0], dkvn[0]
        dzin = jnp.concatenate([dcq, dckv, dkr, dscb, dscc, dsch, dz, dxraw, ddtraw.astype(BF16), jnp.zeros((T, ZIN - Z_DT - LANE), BF16)], axis=1)
        dh1 = _mm(dzin, lw["w_in"], "nt", F32, "mm_in_dx")
        GW["w_in"][l] = _unpad_cols_in(_mm(s["h1"], dzin, "tn", BF16, "mm_in_dw"))
        gx, dgp = _rows(_vjp_wrap(_f_premix, 1, 1, add_first=True), T, tm, [_cur(s["x"]), _cur(dh1), _cur(gx0)], [_cst(s["g_pre"])],
                        [_out(D_MODEL, F32)], [_acc(1, D_MODEL)], "pre_mix_bwd")
        GS["norm_mix_pre"][l] = dgp[0]
    GS = {k: jnp.stack(v) for k, v in GS.items()}
    return loss_part[0, 0], gx, GW, GS


WEIGHTS = ("norm_mix_pre", "norm_mix_post", "norm_ffn_pre", "norm_ffn_post", "w_in", "mla_q_norm", "mla_w_q_up", "mla_kv_norm",
           "mla_w_kv_up", "sc_conv_w", "ssd_conv_w", "ssd_conv_b", "ssd_dt_bias", "ssd_a_log", "ssd_d", "ssd_norm", "w_out",
           "ffn_w_up", "ffn_conv_w", "ffn_conv_b", "ffn_w_down")
SHARDED = (("w_in", 2), ("mla_w_q_up", 2), ("mla_w_kv_up", 2), ("sc_conv_w", 2), ("ssd_conv_w", 2), ("w_out", 1),
           ("ffn_w_up", 2), ("ffn_conv_w", 2), ("ffn_w_down", 1))
SMALL = tuple(n for n in WEIGHTS if n not in dict(SHARDED))
N_CHIPS = 4
N_DEV = 8
ROW_ALIGN = 256
SLAB_ALIGN = 16
MAIN = ("ffn_w_down", "w_out", "w_in", "mla_w_q_up", "mla_w_kv_up", "sc_conv_w", "ssd_conv_w")
WIDE = ("ffn_w_up", "ffn_conv_w")


def _slab_rows(shape, width):
    if len(shape) == 2 and shape[1] == width and shape[0] % SLAB_ALIGN == 0:
        return shape[0]
    return -(-math.prod(shape) // (width * SLAB_ALIGN)) * SLAB_ALIGN


def _slab(piece, width, dtype):
    rows = _slab_rows(piece.shape, width)
    if piece.shape == (rows, width):
        return piece.astype(dtype)
    flat = piece.astype(dtype).reshape(-1)
    return jnp.pad(flat, (0, rows * width - flat.shape[0])).reshape(rows, width)


def _unslab(slab, shape):
    if slab.shape == tuple(shape):
        return slab
    return slab.reshape(-1)[:math.prod(shape)].reshape(shape)


def _layout(shapes):
    out = {}
    for buf, names in (("main", MAIN), ("wide", WIDE)):
        width = PACK_COLS if buf == "main" else shapes["ffn_w_up"][-1]
        ents, off = [], 0
        for n in names:
            shp = tuple(shapes[n])
            todo = [(None, False, shp), (None, True, shp)] if n.endswith("conv_w") else [(l, False, shp[1:]) for l in range(shp[0])]
            for l, lo, ps in todo:
                r = _slab_rows(ps, width)
                ents.append((n, l, lo, ps, off, r))
                off += r
        out[buf] = (width, -(-off // ROW_ALIGN) * ROW_ALIGN, ents)
    return out


def _pack(layout, piece, dtype):
    width, rows, ents = layout
    slabs = []
    for n, l, lo, ps, off, r in ents:
        p = piece(n, l, lo)
        slabs.append(jnp.zeros((r, width), dtype) if p is None else _slab(p, width, dtype))
    used = ents[-1][4] + ents[-1][5]
    if rows > used:
        slabs.append(jnp.zeros((rows - used, width), dtype))
    return jnp.concatenate(slabs, axis=0)


ANY = pl.BlockSpec(memory_space=pl.ANY)


def _pos():
    return lax.axis_index("x"), lax.axis_index("y"), lax.axis_index("c")


def _other_chips(x, y):
    return ((1 - x, y), (x, 1 - y), (1 - x, 1 - y))


def _remote(src, dst, ssem, rsem, dev):
    return pltpu.make_async_remote_copy(src_ref=src, dst_ref=dst, send_sem=ssem, recv_sem=rsem, device_id=dev, device_id_type=MESH)


AG_CHUNKS = 2


def _all_gather_weights(wpk):
    R, C = wpk.shape
    H = R // 2
    CH = H // AG_CHUNKS
    n = 3 * AG_CHUNKS

    def body(w_ref, out_ref, isend, irecv, dsend, drecv):
        x, y, c = _pos()
        k = 2 * x + y
        sib = (x, y, 1 - c)
        chips = _other_chips(x, y)

        def rows(kk, half, ch):
            return out_ref.at[kk, pl.ds(half * H + ch * CH, CH), :]

        first = []
        for p, (cx, cy) in enumerate(chips):
            for ch in range(AG_CHUNKS):
                s = p * AG_CHUNKS + ch
                cp = _remote(w_ref.at[pl.ds(c * H + ch * CH, CH), :], rows(k, c, ch), isend.at[s], irecv.at[s], (cx, cy, c))
                cp.start()
                first.append(cp)
        passed = []
        for p, (cx, cy) in enumerate(chips):
            for ch in range(AG_CHUNKS):
                s = p * AG_CHUNKS + ch
                land = rows(2 * cx + cy, c, ch)
                _remote(land, land, isend.at[s], irecv.at[s], (cx, cy, c)).wait_recv()
                fw = _remote(land, land, dsend.at[s], drecv.at[s], sib)
                fw.start()
                passed.append(fw)
        for p, (cx, cy) in enumerate(chips):
            for ch in range(AG_CHUNKS):
                s = p * AG_CHUNKS + ch
                land = rows(2 * cx + cy, 1 - c, ch)
                _remote(land, land, dsend.at[s], drecv.at[s], sib).wait_recv()
        for cp in first + passed:
            cp.wait_send()

    got = pl.pallas_call(
        body, name="all_gather_weights", in_specs=[ANY], out_specs=ANY,
        out_shape=jax.ShapeDtypeStruct((N_CHIPS, R, C), wpk.dtype),
        scratch_shapes=[pltpu.SemaphoreType.DMA((n,))] * 4,
    )(wpk)
    return lax.dynamic_update_slice(got, wpk[None], (2 * lax.axis_index("x") + lax.axis_index("y"), 0, 0))


def _rs_pair_exchange(g):
    _, R, C = g.shape
    H = R // 2

    def body(g_ref, got_ref, ssem, rsem):
        x, y, c = _pos()
        sib = (x, y, 1 - c)
        cps = []
        for kk in range(N_CHIPS):
            cp = _remote(g_ref.at[kk, pl.ds((1 - c) * H, H), :], got_ref.at[kk], ssem.at[kk], rsem.at[kk], sib)
            cp.start()
            cps.append(cp)
        for cp in cps:
            cp.wait()

    return pl.pallas_call(
        body, name="rs_pair_exchange", in_specs=[ANY], out_specs=ANY,
        out_shape=jax.ShapeDtypeStruct((N_CHIPS, H, C), g.dtype),
        scratch_shapes=[pltpu.SemaphoreType.DMA((N_CHIPS,))] * 2,
    )(g)


def _rs_chip_exchange(p):
    _, H, C = p.shape

    def body(p_ref, out_ref, ssem, rsem):
        x, y, c = _pos()
        k = 2 * x + y
        chips = _other_chips(x, y)
        cps = []
        for s, (cx, cy) in enumerate(chips):
            cp = _remote(p_ref.at[2 * cx + cy], out_ref.at[k], ssem.at[s], rsem.at[s], (cx, cy, c))
            cp.start()
            cps.append(cp)
        for s, (cx, cy) in enumerate(chips):
            land = out_ref.at[2 * cx + cy]
            _remote(land, land, ssem.at[s], rsem.at[s], (cx, cy, c)).wait_recv()
        for cp in cps:
            cp.wait_send()

    k = 2 * lax.axis_index("x") + lax.axis_index("y")
    got = pl.pallas_call(
        body, name="rs_chip_exchange", in_specs=[ANY], out_specs=ANY,
        out_shape=jax.ShapeDtypeStruct(p.shape, p.dtype),
        scratch_shapes=[pltpu.SemaphoreType.DMA((3,)), pltpu.SemaphoreType.DMA((3,))],
    )(p)
    return lax.dynamic_update_slice(got, lax.dynamic_slice_in_dim(p, k, 1, axis=0), (k, 0, 0))


def _rs_pair_share(f):
    H, C = f.shape

    def body(f_ref, out_ref, ssem, rsem):
        x, y, c = _pos()
        cp = _remote(f_ref, out_ref.at[c], ssem, rsem, (x, y, 1 - c))
        cp.start()
        land = out_ref.at[1 - c]
        _remote(land, land, ssem, rsem, (x, y, 1 - c)).wait_recv()
        cp.wait_send()

    got = pl.pallas_call(
        body, name="rs_pair_share", in_specs=[ANY], out_specs=ANY,
        out_shape=jax.ShapeDtypeStruct((2, H, C), f.dtype),
        scratch_shapes=[pltpu.SemaphoreType.DMA, pltpu.SemaphoreType.DMA],
    )(f)
    return lax.dynamic_update_slice(got, f[None], (lax.axis_index("c"), 0, 0))


def _all_reduce_small(s):
    r, C = s.shape

    def body(s_ref, o_ref, buf, ssem, rsem):
        x, y, c = _pos()
        me = 4 * x + 2 * y + c
        buf[me] = s_ref[...]
        cps = []
        for m in range(1, N_DEV):
            mx, my, mc = (m >> 2) & 1, (m >> 1) & 1, m & 1
            peer = (x ^ mx, y ^ my, c ^ mc)
            cp = _remote(s_ref, buf.at[me], ssem.at[m - 1], rsem.at[m - 1], peer)
            cp.start()
            cps.append(cp)
        for m in range(1, N_DEV):
            mx, my, mc = (m >> 2) & 1, (m >> 1) & 1, m & 1
            src = 4 * (x ^ mx) + 2 * (y ^ my) + (c ^ mc)
            _remote(s_ref, buf.at[src], ssem.at[m - 1], rsem.at[m - 1], (x ^ mx, y ^ my, c ^ mc)).wait_recv()
        for cp in cps:
            cp.wait_send()
        acc = buf[0]
        for j in range(1, N_DEV):
            acc = acc + buf[j]
        o_ref[...] = acc

    return pl.pallas_call(
        body, name="all_reduce_small", in_specs=[pl.BlockSpec(memory_space=pltpu.VMEM)],
        out_specs=pl.BlockSpec(memory_space=pltpu.VMEM), out_shape=jax.ShapeDtypeStruct((r, C), F32),
        scratch_shapes=[pltpu.VMEM((N_DEV, r, C), F32), pltpu.SemaphoreType.DMA((N_DEV - 1,)), pltpu.SemaphoreType.DMA((N_DEV - 1,))],
    )(s)


def _rtile(n, pref):
    if n <= pref:
        return n
    t = (pref // 16) * 16
    while t >= 16:
        if n % t == 0:
            return t
        t -= 16
    raise ValueError(f"no row tile for {n}")


def _reduce_scatter_grads(gpk):
    _, R, C = gpk.shape
    H = R // 2
    got = _rs_pair_exchange(gpk)
    own = lax.dynamic_index_in_dim(gpk.reshape(N_CHIPS, 2, H, C), lax.axis_index("c"), axis=1, keepdims=False)
    tm = _rtile(N_CHIPS * H, 512)
    (part,) = _rows(lambda i, n, a, b: (a.astype(F32) + b.astype(F32),), N_CHIPS * H, tm,
                    [_cur(own.reshape(N_CHIPS * H, C)), _cur(got.reshape(N_CHIPS * H, C))], [], [_out(C, BF16)], [], "rs_pair_add")
    parts = _rs_chip_exchange(part.reshape(N_CHIPS, H, C)).reshape(N_CHIPS * H, C)
    tm = _rtile(H, 1024)
    hb = H // tm

    def add4(i, n, a, b, c, d):
        return (((a.astype(F32) + b.astype(F32)) + c.astype(F32)) + d.astype(F32),)

    (red,) = _rows(add4, H, tm, [(parts, C, functools.partial(_const, v=0), j * hb) for j in range(N_CHIPS)], [], [_out(C, F32)], [],
                   "rs_chip_add")
    return _rs_pair_share(red).reshape(R, C)


def _adam(w, g, m, v, name, g_row=0):
    shp = w.shape
    two = lambda a: a.reshape(-1, shp[-1])
    rows = math.prod(shp[:-1])
    tm = _rtile(rows, 256)
    assert g_row % tm == 0
    g_in = (two(g), shp[-1], functools.partial(_const, v=0), g_row // tm)
    res = _rows(_k_adam, rows, tm, [_cur(two(w)), g_in, _cur(two(m)), _cur(two(v))], [], [_out(shp[-1], F32)] * 4, [], name)
    return tuple(r.reshape(shp) for r in res)


def _pack_flat(parts, rows):
    flat = jnp.concatenate([p.astype(F32).reshape(-1) for p in parts])
    return jnp.pad(flat, (0, rows * PACK_COLS - flat.shape[0])).reshape(rows, PACK_COLS)


def _unpack_flat(buf, shapes):
    flat, out, off = buf.reshape(-1), [], 0
    for shp in shapes:
        n = math.prod(shp)
        out.append(flat[off:off + n].reshape(shp))
        off += n
    return out


def kernel(x, positions, norm_mix_pre, norm_mix_post, norm_ffn_pre, norm_ffn_post, w_in, mla_q_norm, mla_w_q_up, mla_kv_norm, mla_w_kv_up, sc_conv_w, ssd_conv_w, ssd_conv_b, ssd_dt_bias, ssd_a_log, ssd_d, ssd_norm, w_out, ffn_w_up, ffn_conv_w, ffn_conv_b, ffn_w_down, loss_target, m_norm_mix_pre, m_norm_mix_post, m_norm_ffn_pre, m_norm_ffn_post, m_w_in, m_mla_q_norm, m_mla_w_q_up, m_mla_kv_norm, m_mla_w_kv_up, m_sc_conv_w, m_ssd_conv_w, m_ssd_conv_b, m_ssd_dt_bias, m_ssd_a_log, m_ssd_d, m_ssd_norm, m_w_out, m_ffn_w_up, m_ffn_conv_w, m_ffn_conv_b, m_ffn_w_down, v_norm_mix_pre, v_norm_mix_post, v_norm_ffn_pre, v_norm_ffn_post, v_w_in, v_mla_q_norm, v_mla_w_q_up, v_mla_kv_norm, v_mla_w_kv_up, v_sc_conv_w, v_ssd_conv_w, v_ssd_conv_b, v_ssd_dt_bias, v_ssd_a_log, v_ssd_d, v_ssd_norm, v_w_out, v_ffn_w_up, v_ffn_conv_w, v_ffn_conv_b, v_ffn_w_down):
    a = dict(locals())
    axis = dict(SHARDED)
    layout = _layout({n: a[n].shape for n in axis})

    def weight_piece(n, l, lo):
        w = a[n] if l is None else a[n][l]
        return w - w.astype(BF16).astype(F32) if lo else w

    W, taps = {n: [None] * DEPTH for n in axis}, {}
    for buf, (width, rows, ents) in layout.items():
        gathered = _all_gather_weights(_pack(layout[buf], weight_piece, BF16))
        for n, l, lo, ps, off, r in ents:
            parts = [_unslab(gathered[k, off:off + r], ps) for k in range(N_CHIPS)]
            if l is None:
                taps[n, lo] = jnp.concatenate(parts, axis=axis[n]).astype(F32)
            else:
                W[n][l] = jnp.concatenate(parts, axis=axis[n] - 1)
    for n, lo in taps:
        if not lo:
            W[n] = taps[n, False] + taps[n, True]
    S = {n: a[n] for n in SMALL}

    loss_part, gx, GW, GS = _local_step(a["x"][0], a["positions"][0], a["loss_target"][0], W, S)

    def chip_part(g, ax, k):
        if isinstance(g, tuple):
            g, k = g[k // 2], k % 2
            size = g.shape[ax - 1] // 2
        else:
            size = g.shape[ax - 1] // N_CHIPS
        return lax.slice_in_dim(g, k * size, (k + 1) * size, axis=ax - 1)

    def grad_piece(k):
        def piece(n, l, lo):
            if lo:
                return None
            if l is None:
                return jnp.stack([chip_part(GW[n][ll], axis[n], k) for ll in range(DEPTH)])
            return chip_part(GW[n][l], axis[n], k)
        return piece

    grads, delta, new_m, new_v = {}, {}, {}, {}
    for buf, (width, rows, ents) in layout.items():
        red = _reduce_scatter_grads(jnp.stack([_pack(layout[buf], grad_piece(k), BF16) for k in range(N_CHIPS)]))
        for n in (MAIN if buf == "main" else WIDE):
            mine = [e for e in ents if e[0] == n and not e[2]]
            if a[n].shape[-1] == width and all(e[3] == (e[5], width) for e in mine):
                g, g_row = red, mine[0][4]
            elif mine[0][1] is None:
                g, g_row = _unslab(red[mine[0][4]:mine[0][4] + mine[0][5]], mine[0][3]), 0
            else:
                g, g_row = jnp.stack([_unslab(red[e[4]:e[4] + e[5]], e[3]) for e in mine]), 0
            grads[n], delta[n], new_m[n], new_v[n] = _adam(a[n], g, a["m_" + n], a["v_" + n], "adamw_" + n, g_row)

    small_shapes = [a[n].shape for n in SMALL]
    rs = -(-(sum(math.prod(s) for s in small_shapes) + 1) // (PACK_COLS * SLAB_ALIGN)) * SLAB_ALIGN
    red = _all_reduce_small(_pack_flat([GS[n] for n in SMALL] + [loss_part.reshape(1)], rs))
    loss = _unpack_flat(red, small_shapes + [(1,)])[-1][0]
    pk = lambda pre: _pack_flat([a[pre + n] for n in SMALL], rs)
    for dst, buf in zip((grads, delta, new_m, new_v), _adam(pk(""), red, pk("m_"), pk("v_"), "adamw_small")):
        dst.update(zip(SMALL, _unpack_flat(buf, small_shapes)))

    return (loss, gx[None], *[grads[n] for n in WEIGHTS], *[delta[n] for n in WEIGHTS], *[new_m[n] for n in WEIGHTS],
            *[new_v[n] for n in WEIGHTS])
```

```python
import functools
import math

import jax
import jax.numpy as jnp
from jax import lax
from jax.experimental import pallas as pl
from jax.experimental.pallas import tpu as pltpu

F32 = jnp.float32
BF16 = jnp.bfloat16
MXU_DTYPE = jnp.bfloat16
HIGHEST = lax.Precision.HIGHEST
MESH = pl.DeviceIdType.MESH

D_MODEL = 1024
DEPTH = 4
HEADS = 8
Q_LORA = 256
KV_LORA = 128
NOPE = 64
ROPE = 32
VDIM = 64
ROPE_THETA = 10000.0
SC_DIM = 256
SSD_HEADS = 4
SSD_HEAD_DIM = 64
SSD_STATE = 128
SSD_DIM = 256
SSD_CONV_DIM = 768
SSD_CHUNK = 128
FFN_DIM = 2816
NORM_EPS = 1e-6
QK_SCALE = (NOPE + ROPE) ** -0.5
LANE = 128
HP = 128
FLASH_HEADS = 2

ZIN = 2560
Z_CQ, Z_CKV, Z_KR, Z_SCB, Z_SCC, Z_SCH, Z_SSZ, Z_XBC, Z_DT = 0, 256, 384, 512, 768, 1024, 1280, 1536, 2304
KR_LANE = 64
YCAT = HEADS * HP + SC_DIM + SSD_DIM
FFN_TILE = 256
FFN_ROWS = 1024

ADAM_LR, ADAM_B1, ADAM_B2, ADAM_EPS, ADAM_WD, ADAM_STEP = 0.001, 0.9, 0.999, 1e-08, 0.01, 10

PACK_COLS = 1024


def _tile(n, pref):
    if n <= pref:
        return n
    t = (pref // LANE) * LANE
    while t >= LANE:
        if n % t == 0:
            return t
        t -= LANE
    raise ValueError(f"no tile for {n}")


MM_TM, MM_TN, MM_TK = 1024, 1408, 1536


def _mm(a, b, mode, out_dtype, name, tm=None, tn=MM_TN, tkmax=MM_TK):
    pair = isinstance(a, tuple)
    a_list = list(a) if pair else [a]
    layer = None
    if isinstance(b, tuple):
        b, layer = b
    bshape = b.shape[-2:]
    if mode == "nn":
        (M, Ka), (_, N) = a_list[0].shape, bshape
    elif mode == "nt":
        (M, Ka), (N, _) = a_list[0].shape, bshape
    else:
        (Ka, M), (_, N) = a_list[0].shape, bshape
    tm = (MM_TN if mode == "tn" else MM_TM) if tm is None else tm
    tm, tn, tk = _tile(M, tm), _tile(N, tn), _tile(Ka, tkmax)
    nka = Ka // tk
    nk = nka * len(a_list)

    def bspec(shape, index):
        if layer is None:
            return pl.BlockSpec(shape, index)
        return pl.BlockSpec((None,) + shape, lambda i, j, k: (layer,) + index(i, j, k))

    if mode == "nn":
        a_specs = [pl.BlockSpec((tm, tk), lambda i, j, k: (i, jnp.minimum(k, nka - 1))),
                   pl.BlockSpec((tm, tk), lambda i, j, k: (i, jnp.maximum(k - nka, 0)))][:len(a_list)]
        b_spec = bspec((tk, tn), lambda i, j, k: (k, j))
        dims = NN
    elif mode == "nt":
        a_specs = [pl.BlockSpec((tm, tk), lambda i, j, k: (i, jnp.minimum(k, nka - 1))),
                   pl.BlockSpec((tm, tk), lambda i, j, k: (i, jnp.maximum(k - nka, 0)))][:len(a_list)]
        b_spec = bspec((tn, tk), lambda i, j, k: (j, k))
        dims = NT
    else:
        a_specs = [pl.BlockSpec((tk, tm), lambda i, j, k: (k, i))]
        b_spec = pl.BlockSpec((tk, tn), lambda i, j, k: (k, j))
        dims = TN
    na = len(a_list)

    def body(*refs):
        a_refs, b_ref, o_ref = refs[:na], refs[na], refs[na + 1]
        k = pl.program_id(2)

        def prod(a_ref):
            return lax.dot_general(a_ref[...].astype(MXU_DTYPE), b_ref[...].astype(MXU_DTYPE), dims, preferred_element_type=F32)

        if nk == 1:
            o_ref[...] = prod(a_refs[0]).astype(o_ref.dtype)
            return
        acc_ref = refs[na + 2]

        @pl.when(k == 0)
        def _():
            acc_ref[...] = prod(a_refs[0])

        @pl.when((k > 0) & (k < nka))
        def _():
            acc_ref[...] += prod(a_refs[0])

        if pair:
            @pl.when(k >= nka)
            def _():
                acc_ref[...] += prod(a_refs[1])

        @pl.when(k == nk - 1)
        def _():
            o_ref[...] = acc_ref[...].astype(o_ref.dtype)

    return pl.pallas_call(
        body, name=name, grid=(M // tm, N // tn, nk),
        in_specs=a_specs + [b_spec], out_specs=pl.BlockSpec((tm, tn), lambda i, j, k: (i, j)),
        out_shape=jax.ShapeDtypeStruct((M, N), out_dtype),
        scratch_shapes=[pltpu.VMEM((tm, tn), F32)] if nk > 1 else [],
        compiler_params=pltpu.CompilerParams(dimension_semantics=("parallel", "parallel", "arbitrary")),
    )(*a_list, b)


HALO = 8


def _const(j, v):
    return v


def _rows(fn, T, tm, ins, consts, outs, accs, name, ncol=1):
    n = T // tm
    hb = tm // HALO
    last = T // HALO - 1
    in_specs, args = [], []
    for arr, bc, cb, kind in ins:
        if isinstance(kind, int):
            in_specs.append(pl.BlockSpec((tm, bc), lambda j, i, cb=cb, off=kind: (i + off, cb(j))))
        elif kind == "cur":
            in_specs.append(pl.BlockSpec((tm, bc), lambda j, i, cb=cb: (i, cb(j))))
        elif kind == "prev":
            in_specs.append(pl.BlockSpec((HALO, bc), lambda j, i, cb=cb: (jnp.maximum(i * hb - 1, 0), cb(j))))
        else:
            in_specs.append(pl.BlockSpec((HALO, bc), lambda j, i, cb=cb: (jnp.minimum((i + 1) * hb, last), cb(j))))
        args.append(arr)
    for arr, bc, cb in consts:
        in_specs.append(pl.BlockSpec((arr.shape[0], bc), lambda j, i, cb=cb: (0, cb(j))))
        args.append(arr)
    out_specs, out_shape = [], []
    for tc, dt, bc, cb in outs:
        out_specs.append(pl.BlockSpec((tm, bc), lambda j, i, cb=cb: (i, cb(j))))
        out_shape.append(jax.ShapeDtypeStruct((T, tc), dt))
    for r, tc, bc, cb in accs:
        out_specs.append(pl.BlockSpec((r, bc), lambda j, i, cb=cb: (0, cb(j))))
        out_shape.append(jax.ShapeDtypeStruct((r, tc), F32))
    nin, nout, nacc = len(args), len(outs), len(accs)

    def body(*refs):
        i = pl.program_id(1)
        res = fn(i, n, *[r[...] for r in refs[:nin]])
        for r, v in zip(refs[nin:nin + nout], res[:nout]):
            r[...] = v.astype(r.dtype)
        if nacc:
            acc_refs = refs[nin + nout:nin + nout + nacc]

            @pl.when(i == 0)
            def _():
                for r in acc_refs:
                    r[...] = jnp.zeros_like(r)

            for r, v in zip(acc_refs, res[nout:]):
                r[...] += v.astype(F32)

    res = pl.pallas_call(
        body, name=name, grid=(ncol, n), in_specs=in_specs, out_specs=out_specs, out_shape=out_shape,
        compiler_params=pltpu.CompilerParams(dimension_semantics=("arbitrary", "arbitrary")),
    )(*args)
    return res


def _cur(arr, bc=None, blk=0):
    bc = arr.shape[1] if bc is None else bc
    return (arr, bc, functools.partial(_const, v=blk), "cur")


def _halo(arr, kind, bc=None, blk=0):
    bc = arr.shape[1] if bc is None else bc
    return (arr, bc, functools.partial(_const, v=blk), kind)


def _cst(arr):
    return (arr, arr.shape[1], functools.partial(_const, v=0))


def _out(cols, dt):
    return (cols, dt, cols, functools.partial(_const, v=0))


def _acc(rows, cols):
    return (rows, cols, cols, functools.partial(_const, v=0))


def _rms(x, w):
    return x * lax.rsqrt(jnp.mean(x * x, axis=-1, keepdims=True) + NORM_EPS) * w


def _sigmoid(x):
    return 0.5 * jnp.tanh(0.5 * x) + 0.5


def _silu(x):
    return x * _sigmoid(x)


def _dsilu(x):
    s = _sigmoid(x)
    return s * (1.0 + x * (1.0 - s))


def _softplus(x):
    return jnp.maximum(x, 0.0) + jnp.log1p(jnp.exp(-jnp.abs(x)))


def _shift(a, k):
    return pltpu.roll(a, k % a.shape[0], 0)


def _lroll(a, k):
    return pltpu.roll(a, k % a.shape[1], 1)


def _vjp_wrap(f, nrow, nconst, add_first=False):
    def g(i, n, *vals):
        rows, consts, mid = vals[:nrow], vals[len(vals) - nconst:], vals[nrow:len(vals) - nconst]
        cots = mid[:-1] if add_first else mid
        outs, pull = jax.vjp(f, *rows, *consts)
        grads = list(pull(tuple(c.astype(o.dtype) for c, o in zip(cots, outs))))
        if add_first:
            grads[0] = grads[0] + mid[-1]
        return tuple(grads)
    return g


def _rows_vjp(f, T, tm, rows, consts, cots, out_dtypes, name):
    return _rows(_vjp_wrap(f, len(rows), len(consts)), T, tm, [_cur(r) for r in rows] + [_cur(c) for c in cots],
                 [_cst(c) for c in consts], [_out(r.shape[1], dt) for r, dt in zip(rows, out_dtypes)],
                 [_acc(1, c.shape[1]) for c in consts], name)


def _f_premix(x, g):
    return (_rms(x, g),)


def _f_mla_pre(cq, ckv, qn, kvn):
    return _rms(cq, qn), _rms(ckv, kvn)


def _f_ssd_gate(y, z, nw):
    return (_rms(y * _silu(z), nw),)


def _f_post_mix(x, mixed, gpost, gffn):
    x1 = x + _rms(mixed, gpost)
    return x1, _rms(x1, gffn)


def _f_post_ffn(x1, d, gpost):
    return (x1 + _rms(d, gpost),)


def _rope_fwd(v, cosf, sina, sinb):
    return v * cosf + _lroll(v, -16) * sina + _lroll(v, 16) * sinb


def _rope_bwd(g, cosf, sina, sinb):
    return g * cosf + _lroll(g * sina, 16) + _lroll(g * sinb, -16)


def _k_rope_fwd(i, n, qpad, kvpad, kr, cosf, sina, sinb):
    qs, ks = [], []
    krr = _rope_fwd(kr, cosf, sina, sinb)
    for h in range(HEADS):
        sl = slice(h * HP, (h + 1) * HP)
        qs.append(_rope_fwd(qpad[:, sl], cosf, sina, sinb))
        ks.append(kvpad[:, sl].astype(F32) + krr)
    return jnp.concatenate(qs, axis=1), jnp.concatenate(ks, axis=1)


def _k_rope_bwd(i, n, dq, dk, dv, cosf, sina, sinb):
    lane = lax.broadcasted_iota(jnp.int32, (1, HP), 1)
    rmask = ((lane >= KR_LANE) & (lane < KR_LANE + ROPE)).astype(F32)
    dqs, dks = [], []
    dkr = jnp.zeros((dq.shape[0], HP), F32)
    for h in range(HEADS):
        sl = slice(h * HP, (h + 1) * HP)
        dqs.append(_rope_bwd(dq[:, sl], cosf, sina, sinb))
        dkh = dk[:, sl]
        dkr = dkr + dkh * rmask
        dks.append(dkh * (1.0 - rmask))
    dkr = _rope_bwd(dkr, cosf, sina, sinb) * rmask
    return jnp.concatenate(dqs, axis=1), jnp.concatenate(dks + [dv], axis=1), dkr


def _k_sconv_fwd(i, n, b, c, h, cp, hp, w):
    m = b.shape[0]
    up = jnp.where(i > 0, cp * hp, 0.0)
    ue = jnp.concatenate([up, c * h], axis=0)
    conv = w[2:3] * ue + w[1:2] * _shift(ue, 1) + w[0:1] * _shift(ue, 2)
    return (b * conv[HALO:],)


def _k_sconv_bwd(i, n, b, c, h, dy, cp, hp, bn, dyn, w):
    m = b.shape[0]
    up = jnp.where(i > 0, cp * hp, 0.0)
    ue = jnp.concatenate([up, c * h], axis=0)
    u1, u2 = _shift(ue, 1), _shift(ue, 2)
    conv = (w[2:3] * ue + w[1:2] * u1 + w[0:1] * u2)[HALO:]
    dc_cur = dy * b
    dce = jnp.concatenate([dc_cur, jnp.where(i < n - 1, dyn * bn, 0.0)], axis=0)
    du = (w[2:3] * dce + w[1:2] * _shift(dce, -1) + w[0:1] * _shift(dce, -2))[:m]
    dw = jnp.concatenate([
        jnp.sum(dc_cur * u2[HALO:], axis=0, keepdims=True),
        jnp.sum(dc_cur * u1[HALO:], axis=0, keepdims=True),
        jnp.sum(dc_cur * ue[HALO:], axis=0, keepdims=True),
        jnp.zeros((HALO - 3, b.shape[1]), F32)], axis=0)
    return dy * conv, du * h, du * c, dw


def _conv4(ue, w):
    return w[3:4] * ue + w[2:3] * _shift(ue, 1) + w[1:2] * _shift(ue, 2) + w[0:1] * _shift(ue, 3)


def _k_ssdconv_fwd(i, n, u, up, w, bias):
    ue = jnp.concatenate([jnp.where(i > 0, up, 0.0), u], axis=0)
    return (_silu(_conv4(ue, w)[HALO:] + bias),)


def _k_ssdconv_bwd(i, n, u, dout, up, un, doutn, w, bias):
    m = u.shape[0]
    ue = jnp.concatenate([jnp.where(i > 0, up, 0.0), u, un], axis=0)
    u1, u2, u3 = _shift(ue, 1), _shift(ue, 2), _shift(ue, 3)
    pre = (w[3:4] * ue + w[2:3] * u1 + w[1:2] * u2 + w[0:1] * u3)[HALO:] + bias
    doe = jnp.concatenate([dout, jnp.where(i < n - 1, doutn, 0.0)], axis=0)
    dpre = doe * _dsilu(pre)
    du = (w[3:4] * dpre + w[2:3] * _shift(dpre, -1) + w[1:2] * _shift(dpre, -2) + w[0:1] * _shift(dpre, -3))[:m]
    dp = dpre[:m]
    cur = slice(HALO, HALO + m)
    dw = jnp.concatenate([
        jnp.sum(dp * u3[cur], axis=0, keepdims=True),
        jnp.sum(dp * u2[cur], axis=0, keepdims=True),
        jnp.sum(dp * u1[cur], axis=0, keepdims=True),
        jnp.sum(dp * ue[cur], axis=0, keepdims=True),
        jnp.zeros((HALO - 4, u.shape[1]), F32)], axis=0)
    db = jnp.sum(dp, axis=0, keepdims=True)
    return du, dw, db


def _conv3(ue, w):
    return w[2:3] * ue + w[1:2] * _shift(ue, 1) + w[0:1] * _shift(ue, 2)


def _k_ffnact_fwd(i, n, ug, uu, ugp, uup, wg, wu, bg, bu):
    gate = _conv3(jnp.concatenate([jnp.where(i > 0, ugp, 0.0), ug], axis=0), wg)[HALO:] + bg
    upv = _conv3(jnp.concatenate([jnp.where(i > 0, uup, 0.0), uu], axis=0), wu)[HALO:] + bu
    return (_silu(gate) * upv,)


def _k_ffnact_bwd(i, n, ug, uu, dact, ugp, uup, ugn, uun, dactn, wg, wu, bg, bu):
    m = ug.shape[0]
    cur = slice(HALO, HALO + m)

    def taps(p, c, nx):
        e = jnp.concatenate([jnp.where(i > 0, p, 0.0), c, nx], axis=0)
        return e, _shift(e, 1), _shift(e, 2)

    def back(d, w):
        return (w[2:3] * d + w[1:2] * _shift(d, -1) + w[0:1] * _shift(d, -2))[:m]

    def wgrad(d, t):
        return jnp.concatenate([jnp.sum(d[:m] * t[2][cur], axis=0, keepdims=True), jnp.sum(d[:m] * t[1][cur], axis=0, keepdims=True),
                                jnp.sum(d[:m] * t[0][cur], axis=0, keepdims=True), jnp.zeros((HALO - 3, d.shape[1]), F32)], axis=0)

    tg, tu = taps(ugp, ug, ugn), taps(uup, uu, uun)
    gate = (wg[2:3] * tg[0] + wg[1:2] * tg[1] + wg[0:1] * tg[2])[HALO:] + bg
    upv = (wu[2:3] * tu[0] + wu[1:2] * tu[1] + wu[0:1] * tu[2])[HALO:] + bu
    dae = jnp.concatenate([dact, jnp.where(i < n - 1, dactn, 0.0)], axis=0)
    sg = _sigmoid(gate)
    dg = dae * upv * (sg * (1.0 + gate * (1.0 - sg)))
    dup = dae * (gate * sg)
    return (back(dg, wg), back(dup, wu), wgrad(dg, tg), wgrad(dup, tu),
            jnp.sum(dg[:m], axis=0, keepdims=True), jnp.sum(dup[:m], axis=0, keepdims=True))


def _k_loss(i, n, y, tgt):
    e = y - tgt
    part = 0.5 * jnp.sum(jnp.sum(e * e, axis=1, keepdims=True) / D_MODEL, axis=0, keepdims=True)
    return e * (1.0 / D_MODEL), jnp.broadcast_to(part, (1, LANE))


def _k_adam(i, n, w, g, m, v):
    m = ADAM_B1 * m + (1.0 - ADAM_B1) * g
    v = ADAM_B2 * v + (1.0 - ADAM_B2) * (g * g)
    m_hat = m / (1.0 - ADAM_B1 ** ADAM_STEP)
    v_hat = v / (1.0 - ADAM_B2 ** ADAM_STEP)
    delta = -ADAM_LR * (m_hat / (jnp.sqrt(v_hat) + ADAM_EPS) + ADAM_WD * w)
    return g, delta, m, v


def _dotf(a, b, dims):
    return lax.dot_general(a.astype(MXU_DTYPE), b.astype(MXU_DTYPE), dims, preferred_element_type=F32)


NN = (((1,), (0,)), ((), ()))
NT = (((1,), (1,)), ((), ()))
TN = (((0,), (0,)), ((), ()))


def _ssd_chunk(x0, x1, x2, x3, b0, b1, c0, c1, dtraw, p0, p1, p2, p3, dtb, alog, dsk):
    xs, bs, cs_, ps = (x0, x1, x2, x3), (b0, b1), (c0, c1), (p0, p1, p2, p3)
    L = dtraw.shape[0]
    dt = _softplus(dtraw + dtb)
    adt = dt * (-jnp.exp(alog))
    row = lax.broadcasted_iota(jnp.int32, (L, L), 0)
    col = lax.broadcasted_iota(jnp.int32, (L, L), 1)
    tril = row >= col
    cum = jnp.dot(tril.astype(F32), adt, precision=HIGHEST, preferred_element_type=F32)
    cum_t = cum.T
    lane = lax.broadcasted_iota(jnp.int32, (1, LANE), 1)
    sub = lax.broadcasted_iota(jnp.int32, (LANE, 1), 0)
    lastcol = (lax.broadcasted_iota(jnp.int32, (1, L), 1) == L - 1).astype(F32)
    ys, news = [], []
    for h in range(SSD_HEADS):
        g = h // (SSD_HEADS // 2)
        oh = (lane == h).astype(F32)
        dth = jnp.sum(dt * oh, axis=1, keepdims=True)
        csh = jnp.sum(cum * oh, axis=1, keepdims=True)
        csr = jnp.sum(cum_t * (sub == h).astype(F32), axis=0, keepdims=True)
        cl = jnp.sum(csr * lastcol, axis=1, keepdims=True)
        dskh = jnp.sum(dsk * oh, axis=1, keepdims=True)
        x, bm, cm, prev = xs[h], bs[g], cs_[g], ps[h]
        xdt = x * dth
        decay = jnp.exp(jnp.where(tril, csh - csr, -jnp.inf))
        scores = _dotf(cm, bm, NT) * decay
        y_diag = _dotf(scores, xdt, NN)
        bd = bm * jnp.exp(cl - csh)
        cst = _dotf(xdt, bd, TN)
        news.append(prev * jnp.exp(cl) + cst)
        y_off = _dotf(cm, prev, NT) * jnp.exp(csh)
        ys.append(y_diag + y_off + x * dskh)
    return (*ys, *news)


def _ssd_operands(x_ref, dt_ref, par_ref, prev):
    xs = [x_ref[:, h * SSD_HEAD_DIM:(h + 1) * SSD_HEAD_DIM] for h in range(SSD_HEADS)]
    bs = [x_ref[:, SSD_DIM + g * SSD_STATE:SSD_DIM + (g + 1) * SSD_STATE] for g in range(2)]
    cs_ = [x_ref[:, SSD_DIM + 2 * SSD_STATE + g * SSD_STATE:SSD_DIM + 2 * SSD_STATE + (g + 1) * SSD_STATE] for g in range(2)]
    return (*xs, *bs, *cs_, dt_ref[...], *prev, par_ref[0:1, :], par_ref[1:2, :], par_ref[2:3, :])


def _ssd_fwd(xbc, dtraw, par, T):
    L = SSD_CHUNK
    nc = T // L
    P = SSD_HEAD_DIM

    def body(x_ref, dt_ref, par_ref, y_ref, st_ref, state):
        @pl.when(pl.program_id(0) == 0)
        def _():
            state[...] = jnp.zeros_like(state)

        st_ref[0] = state[...]
        prev = [state[h * P:(h + 1) * P, :] for h in range(SSD_HEADS)]
        res = _ssd_chunk(*_ssd_operands(x_ref, dt_ref, par_ref, prev))
        for h in range(SSD_HEADS):
            y_ref[:, h * P:(h + 1) * P] = res[h]
            state[h * P:(h + 1) * P, :] = res[SSD_HEADS + h]

    return pl.pallas_call(
        body, name="ssd_scan_fwd", grid=(nc,),
        in_specs=[pl.BlockSpec((L, SSD_CONV_DIM), lambda c: (c, 0)), pl.BlockSpec((L, LANE), lambda c: (c, 0)),
                  pl.BlockSpec((8, LANE), lambda c: (0, 0))],
        out_specs=[pl.BlockSpec((L, SSD_DIM), lambda c: (c, 0)), pl.BlockSpec((1, SSD_DIM, SSD_STATE), lambda c: (c, 0, 0))],
        out_shape=[jax.ShapeDtypeStruct((T, SSD_DIM), F32), jax.ShapeDtypeStruct((nc, SSD_DIM, SSD_STATE), F32)],
        scratch_shapes=[pltpu.VMEM((SSD_DIM, SSD_STATE), F32)],
        compiler_params=pltpu.CompilerParams(dimension_semantics=("arbitrary",)),
    )(xbc, dtraw, par)


def _ssd_bwd(xbc, dtraw, par, states, dy, T):
    L = SSD_CHUNK
    nc = T // L
    P = SSD_HEAD_DIM

    def body(x_ref, dt_ref, par_ref, st_ref, dy_ref, dx_ref, ddt_ref, dpar_ref, dstate):
        @pl.when(pl.program_id(0) == 0)
        def _():
            dstate[...] = jnp.zeros_like(dstate)
            dpar_ref[...] = jnp.zeros_like(dpar_ref)

        prev = [st_ref[0, h * P:(h + 1) * P, :] for h in range(SSD_HEADS)]
        prim = _ssd_operands(x_ref, dt_ref, par_ref, prev)
        _, pull = jax.vjp(_ssd_chunk, *prim)
        cots = tuple(dy_ref[:, h * P:(h + 1) * P] for h in range(SSD_HEADS)) + tuple(
            dstate[h * P:(h + 1) * P, :] for h in range(SSD_HEADS))
        g = pull(cots)
        for h in range(SSD_HEADS):
            dx_ref[:, h * P:(h + 1) * P] = g[h]
            dstate[h * P:(h + 1) * P, :] = g[9 + h]
        for k in range(2):
            dx_ref[:, SSD_DIM + k * SSD_STATE:SSD_DIM + (k + 1) * SSD_STATE] = g[4 + k]
            dx_ref[:, SSD_DIM + 2 * SSD_STATE + k * SSD_STATE:SSD_DIM + 2 * SSD_STATE + (k + 1) * SSD_STATE] = g[6 + k]
        ddt_ref[...] = g[8]
        for r in range(3):
            dpar_ref[r:r + 1, :] += g[13 + r]

    rev = lambda c: (nc - 1 - c, 0)
    return pl.pallas_call(
        body, name="ssd_scan_bwd", grid=(nc,),
        in_specs=[pl.BlockSpec((L, SSD_CONV_DIM), rev), pl.BlockSpec((L, LANE), rev), pl.BlockSpec((8, LANE), lambda c: (0, 0)),
                  pl.BlockSpec((1, SSD_DIM, SSD_STATE), lambda c: (nc - 1 - c, 0, 0)), pl.BlockSpec((L, SSD_DIM), rev)],
        out_specs=[pl.BlockSpec((L, SSD_CONV_DIM), rev), pl.BlockSpec((L, LANE), rev), pl.BlockSpec((8, LANE), lambda c: (0, 0))],
        out_shape=[jax.ShapeDtypeStruct((T, SSD_CONV_DIM), F32), jax.ShapeDtypeStruct((T, LANE), F32),
                   jax.ShapeDtypeStruct((8, LANE), F32)],
        scratch_shapes=[pltpu.VMEM((SSD_DIM, SSD_STATE), F32)],
        compiler_params=pltpu.CompilerParams(dimension_semantics=("arbitrary",)),
    )(xbc, dtraw, par, states, dy)


def _causal_pairs(nq, by_query):
    if by_query:
        pairs = [(i, j) for i in range(nq) for j in range(i + 1)]
    else:
        pairs = [(i, j) for j in range(nq) for i in range(j, nq)]
    return jnp.asarray([p[0] for p in pairs], jnp.int32), jnp.asarray([p[1] for p in pairs], jnp.int32)


def _flash_fwd(q, k, kv, T):
    tq = tk = min(512, T)
    nq = T // tq
    G = FLASH_HEADS
    rep = tk // HP

    def body(qi_ref, kj_ref, q_ref, k_ref, v_ref, o_ref, m_ref, l_ref, acc_ref):
        t = pl.program_id(1)
        i, j = qi_ref[t], kj_ref[t]

        @pl.when(j == 0)
        def _():
            m_ref[...] = jnp.full_like(m_ref, -jnp.inf)
            l_ref[...] = jnp.zeros_like(l_ref)
            acc_ref[...] = jnp.zeros_like(acc_ref)

        def step(diagonal):
            for g in range(G):
                sl = slice(g * HP, (g + 1) * HP)
                s = _dotf(q_ref[:, sl], k_ref[:, sl], NT) * QK_SCALE
                if diagonal:
                    rows = lax.broadcasted_iota(jnp.int32, (tq, tk), 0)
                    cols = lax.broadcasted_iota(jnp.int32, (tq, tk), 1)
                    s = jnp.where(rows >= cols, s, -jnp.inf)
                m_old = m_ref[:, sl]
                m_new = jnp.maximum(m_old, jnp.max(s, axis=1, keepdims=True))
                p = jnp.exp(s - jnp.tile(m_new, (1, rep)))
                alpha = jnp.exp(m_old - m_new)
                l_ref[:, sl] = alpha * l_ref[:, sl] + jnp.sum(p, axis=1, keepdims=True)
                acc_ref[:, sl] = alpha * acc_ref[:, sl] + _dotf(p, v_ref[:, sl], NN)
                m_ref[:, sl] = m_new

        @pl.when(j < i)
        def _():
            step(False)

        @pl.when(j == i)
        def _():
            step(True)
            lane = lax.broadcasted_iota(jnp.int32, (tq, HP), 1)
            for g in range(G):
                sl = slice(g * HP, (g + 1) * HP)
                l = l_ref[:, sl]
                o_ref[:, sl] = jnp.where(lane < VDIM, acc_ref[:, sl] / l, m_ref[:, sl] + jnp.log(l))

    W = G * HP
    qi, kj = _causal_pairs(nq, by_query=True)
    return pl.pallas_call(
        body, name="mla_flash_fwd",
        grid_spec=pltpu.PrefetchScalarGridSpec(
            num_scalar_prefetch=2, grid=(HEADS // G, qi.shape[0]),
            in_specs=[pl.BlockSpec((tq, W), lambda h, t, qi, kj: (qi[t], h)),
                      pl.BlockSpec((tk, W), lambda h, t, qi, kj: (kj[t], h)),
                      pl.BlockSpec((tk, W), lambda h, t, qi, kj: (kj[t], HEADS // G + h))],
            out_specs=pl.BlockSpec((tq, W), lambda h, t, qi, kj: (qi[t], h)),
            scratch_shapes=[pltpu.VMEM((tq, W), F32), pltpu.VMEM((tq, W), F32), pltpu.VMEM((tq, W), F32)]),
        out_shape=jax.ShapeDtypeStruct((T, HEADS * HP), F32),
        compiler_params=pltpu.CompilerParams(dimension_semantics=("parallel", "arbitrary")),
    )(qi, kj, q, k, kv)


def _flash_bwd(q, k, kv, o, dycat, T):
    tq = tk = min(512, T)
    nq = T // tq
    G = FLASH_HEADS

    def body(qi_ref, kj_ref, q_ref, k_ref, v_ref, o_ref, do_ref, dq_ref, dk_ref, dv_ref):
        t = pl.program_id(1)
        i, j = qi_ref[t], kj_ref[t]

        @pl.when(t == 0)
        def _():
            dq_ref[...] = jnp.zeros_like(dq_ref)

        @pl.when(i == j)
        def _():
            dk_ref[...] = jnp.zeros_like(dk_ref)
            dv_ref[...] = jnp.zeros_like(dv_ref)

        def step(diagonal):
            r0 = pl.multiple_of(i * tq, tq)
            for g in range(G):
                sl = slice(g * HP, (g + 1) * HP)
                qv, kv, vv, ov, dov = q_ref[:, sl], k_ref[:, sl], v_ref[:, sl], o_ref[:, sl], do_ref[:, sl]
                s = _dotf(qv, kv, NT) * QK_SCALE
                p = jnp.exp(s - ov[:, VDIM:VDIM + 1])
                if diagonal:
                    rows = lax.broadcasted_iota(jnp.int32, (tq, tk), 0)
                    cols = lax.broadcasted_iota(jnp.int32, (tq, tk), 1)
                    p = jnp.where(rows >= cols, p, 0.0)
                dsum = jnp.sum(dov * ov, axis=1, keepdims=True)
                dv_ref[:, sl] += _dotf(p, dov, TN)
                dp = _dotf(dov, vv, NT)
                ds = p * (dp - dsum) * QK_SCALE
                dk_ref[:, sl] += _dotf(ds, qv, TN)
                dq_ref[pl.ds(r0, tq), sl] += _dotf(ds, kv, NN)

        @pl.when(i > j)
        def _():
            step(False)

        @pl.when(i == j)
        def _():
            step(True)

    W = G * HP
    qi, kj = _causal_pairs(nq, by_query=False)
    qmap = lambda h, t, qi, kj: (qi[t], h)
    kmap = lambda h, t, qi, kj: (kj[t], h)
    vmap = lambda h, t, qi, kj: (kj[t], HEADS // G + h)
    return pl.pallas_call(
        body, name="mla_flash_bwd",
        grid_spec=pltpu.PrefetchScalarGridSpec(
            num_scalar_prefetch=2, grid=(HEADS // G, qi.shape[0]),
            in_specs=[pl.BlockSpec((tq, W), qmap), pl.BlockSpec((tk, W), kmap), pl.BlockSpec((tk, W), vmap),
                      pl.BlockSpec((tq, W), qmap), pl.BlockSpec((tq, W), qmap)],
            out_specs=[pl.BlockSpec((T, W), lambda h, t, qi, kj: (0, h)), pl.BlockSpec((tk, W), kmap), pl.BlockSpec((tk, W), kmap)]),
        out_shape=[jax.ShapeDtypeStruct((T, HEADS * HP), F32)] * 3,
        compiler_params=pltpu.CompilerParams(dimension_semantics=("parallel", "arbitrary")),
    )(qi, kj, q, k, kv, o, dycat)


_IN_SRC = (0, 256, 384, 416, 672, 928, 1184, 1440, 2208, 2212)
_IN_DST = (Z_CQ, Z_CKV, Z_KR + KR_LANE, Z_SCB, Z_SCC, Z_SCH, Z_SSZ, Z_XBC, Z_DT)


def _pad_cols_in(w):
    parts, at = [], 0
    for s0, s1, d0 in zip(_IN_SRC[:-1], _IN_SRC[1:], _IN_DST):
        if d0 > at:
            parts.append(jnp.zeros(w.shape[:-1] + (d0 - at,), w.dtype))
        parts.append(w[..., s0:s1])
        at = d0 + (s1 - s0)
    parts.append(jnp.zeros(w.shape[:-1] + (ZIN - at,), w.dtype))
    return jnp.concatenate(parts, axis=-1)


def _unpad_cols_in(w):
    return jnp.concatenate([w[..., d0:d0 + (s1 - s0)] for s0, s1, d0 in zip(_IN_SRC[:-1], _IN_SRC[1:], _IN_DST)], axis=-1)


def _pad_heads(w, width):
    w = w.reshape(w.shape[:-1] + (HEADS, width))
    w = jnp.pad(w, [(0, 0)] * (w.ndim - 1) + [(0, HP - width)])
    return w.reshape(w.shape[:-2] + (HEADS * HP,))


def _unpad_heads(w, width):
    w = w.reshape(w.shape[:-1] + (HEADS, HP))[..., :width]
    return w.reshape(w.shape[:-2] + (HEADS * width,))


def _pad_kv(w):
    w = w.reshape(w.shape[:-1] + (HEADS, NOPE + VDIM))
    return jnp.concatenate([_pad_heads(w[..., :NOPE].reshape(w.shape[:-2] + (HEADS * NOPE,)), NOPE),
                            _pad_heads(w[..., NOPE:].reshape(w.shape[:-2] + (HEADS * VDIM,)), VDIM)], axis=-1)


def _unpad_kv(w):
    k = _unpad_heads(w[..., :HEADS * HP], NOPE).reshape(w.shape[:-1] + (HEADS, NOPE))
    v = _unpad_heads(w[..., HEADS * HP:], VDIM).reshape(w.shape[:-1] + (HEADS, VDIM))
    return jnp.concatenate([k, v], axis=-1).reshape(w.shape[:-1] + (HEADS * (NOPE + VDIM),))


def _pad_out_rows(w):
    lead, d = w.shape[:-2], w.shape[-1]
    att = w[..., :HEADS * VDIM, :].reshape(lead + (HEADS, VDIM, d))
    att = jnp.pad(att, [(0, 0)] * (att.ndim - 2) + [(0, HP - VDIM), (0, 0)]).reshape(lead + (HEADS * HP, d))
    return jnp.concatenate([att, w[..., HEADS * VDIM:, :]], axis=-2)


def _unpad_out_rows(w):
    lead, d = w.shape[:-2], w.shape[-1]
    att = w[..., :HEADS * HP, :].reshape(lead + (HEADS, HP, d))[..., :VDIM, :].reshape(lead + (HEADS * VDIM, d))
    return jnp.concatenate([att, w[..., HEADS * HP:, :]], axis=-2)


def _rows8(w):
    return jnp.pad(w.astype(F32), [(0, 0)] * (w.ndim - 2) + [(0, 8 - w.shape[-2]), (0, 0)])


def _row8(*vecs):
    c = vecs[0].shape[-1]
    return jnp.concatenate([v.reshape(1, c).astype(F32) for v in vecs] + [jnp.zeros((8 - len(vecs), c), F32)], axis=0)


def _lanes(v):
    return jnp.pad(v.astype(F32), (0, LANE - v.shape[0])).reshape(1, LANE)


def _rope_tables(positions):
    inv_freq = 1.0 / (ROPE_THETA ** (jnp.arange(0, ROPE, 2, dtype=F32) / ROPE))
    ang = positions.astype(F32)[:, None] * inv_freq
    cos, sin = jnp.cos(ang), jnp.sin(ang)
    T = positions.shape[0]
    half = ROPE // 2
    one = jnp.ones((T, KR_LANE), F32)
    zero = jnp.zeros((T, KR_LANE), F32)
    tail1 = jnp.ones((T, HP - KR_LANE - ROPE), F32)
    tail0 = jnp.zeros((T, HP - KR_LANE - ROPE), F32)
    z16 = jnp.zeros((T, half), F32)
    cosf = jnp.concatenate([one, cos, cos, tail1], axis=1)
    sina = jnp.concatenate([zero, -sin, z16, tail0], axis=1)
    sinb = jnp.concatenate([zero, z16, sin, tail0], axis=1)
    return cosf, sina, sinb


def _kernel_weights(W):
    c = lambda a: a.astype(MXU_DTYPE)
    return dict(
        w_in=c(_pad_cols_in(W["w_in"])),
        w_q=c(_pad_heads(W["mla_w_q_up"], NOPE + ROPE)),
        w_kv=c(_pad_kv(W["mla_w_kv_up"])),
        w_out=c(_pad_out_rows(W["w_out"])),
        w_up=c(W["ffn_w_up"]),
        w_down=c(W["ffn_w_down"]),
        sc_w=_rows8(W["sc_conv_w"]),
        ssd_w=_rows8(W["ssd_conv_w"]),
        ffn_w=_rows8(W["ffn_conv_w"]),
    )


def _layer_weights(KW, l):
    return {k: (v[l] if k in ("sc_w", "ssd_w", "ffn_w") else (v, l)) for k, v in KW.items()}


def _local_step(x, positions, target, W, S):
    T = x.shape[0]
    tm = min(256, T)
    tm_ffn = min(FFN_ROWS, T)
    cosf, sina, sinb = _rope_tables(positions)
    KW = _kernel_weights(W)
    saved = []
    xl = x
    for l in range(DEPTH):
        lw = _layer_weights(KW, l)
        g_pre = S["norm_mix_pre"][l].reshape(1, -1)
        g_post = S["norm_mix_post"][l].reshape(1, -1)
        g_fpre = S["norm_ffn_pre"][l].reshape(1, -1)
        g_fpost = S["norm_ffn_post"][l].reshape(1, -1)
        qn = S["mla_q_norm"][l].reshape(1, -1)
        kvn = S["mla_kv_norm"][l].reshape(1, -1)
        ssd_b = S["ssd_conv_b"][l].reshape(1, -1)
        ssd_par = _row8(jnp.pad(S["ssd_dt_bias"][l], (0, LANE - SSD_HEADS)), jnp.pad(S["ssd_a_log"][l], (0, LANE - SSD_HEADS)),
                        jnp.pad(S["ssd_d"][l], (0, LANE - SSD_HEADS)))
        ssd_nw = S["ssd_norm"][l].reshape(1, -1)
        ffn_b = S["ffn_conv_b"][l].reshape(1, -1)

        (h1,) = _rows(lambda i, n, *v: _f_premix(*v), T, tm, [_cur(xl)], [_cst(g_pre)], [_out(D_MODEL, BF16)], [], "pre_mix_norm")
        zin = _mm(h1, lw["w_in"], "nn", F32, "mm_in")
        qlat, kvlat = _rows(lambda i, n, *v: _f_mla_pre(*v), T, tm, [_cur(zin, Q_LORA, 0), _cur(zin, KV_LORA, Z_CKV // KV_LORA)],
                            [_cst(qn), _cst(kvn)], [_out(Q_LORA, BF16), _out(KV_LORA, BF16)], [], "mla_pre_norm")
        qpad = _mm(qlat, lw["w_q"], "nn", F32, "mm_q_up")
        kvpad = _mm(kvlat, lw["w_kv"], "nn", BF16, "mm_kv_up")
        qr, kr = _rows(_k_rope_fwd, T, tm, [_cur(qpad), _cur(kvpad, HEADS * HP, 0), _cur(zin, LANE, Z_KR // LANE),
                                            _cur(cosf), _cur(sina), _cur(sinb)], [],
                       [_out(HEADS * HP, BF16), _out(HEADS * HP, BF16)], [], "mla_rope")
        o = _flash_fwd(qr, kr, kvpad, T)
        (yconv,) = _rows(_k_sconv_fwd, T, tm, [_cur(zin, SC_DIM, Z_SCB // SC_DIM), _cur(zin, SC_DIM, Z_SCC // SC_DIM),
                                               _cur(zin, SC_DIM, Z_SCH // SC_DIM), _halo(zin, "prev", SC_DIM, Z_SCC // SC_DIM),
                                               _halo(zin, "prev", SC_DIM, Z_SCH // SC_DIM)], [_cst(lw["sc_w"])],
                         [_out(SC_DIM, F32)], [], "short_conv_fwd")
        (xbc,) = _rows(_k_ssdconv_fwd, T, tm, [_cur(zin, SSD_CONV_DIM, Z_XBC // SSD_CONV_DIM),
                                               _halo(zin, "prev", SSD_CONV_DIM, Z_XBC // SSD_CONV_DIM)],
                       [_cst(lw["ssd_w"]), _cst(ssd_b)], [_out(SSD_CONV_DIM, F32)], [], "ssd_conv_fwd")
        dtraw = zin[:, Z_DT:Z_DT + LANE]
        yscan, states = _ssd_fwd(xbc, dtraw, ssd_par, T)
        (yssd,) = _rows(lambda i, n, *v: _f_ssd_gate(*v), T, tm, [_cur(yscan), _cur(zin, SSD_DIM, Z_SSZ // SSD_DIM)], [_cst(ssd_nw)],
                        [_out(SSD_DIM, F32)], [], "ssd_gate_fwd")
        ycat = jnp.concatenate([o.astype(BF16), yconv.astype(BF16), yssd.astype(BF16)], axis=1)
        mixed = _mm(ycat, lw["w_out"], "nn", F32, "mm_out")
        x1, h2 = _rows(lambda i, n, *v: _f_post_mix(*v), T, tm, [_cur(xl), _cur(mixed)], [_cst(g_post), _cst(g_fpre)],
                       [_out(D_MODEL, F32), _out(D_MODEL, BF16)], [], "post_mix_fwd")
        upre = _mm(h2, lw["w_up"], "nn", F32, "mm_up")
        nt = FFN_DIM // FFN_TILE
        gcol, ucol = (lambda j: j), (lambda j: j + nt)
        (act,) = _rows(_k_ffnact_fwd, T, tm_ffn,
                       [(upre, FFN_TILE, gcol, "cur"), (upre, FFN_TILE, ucol, "cur"), (upre, FFN_TILE, gcol, "prev"),
                        (upre, FFN_TILE, ucol, "prev")],
                       [(lw["ffn_w"], FFN_TILE, gcol), (lw["ffn_w"], FFN_TILE, ucol), (ffn_b, FFN_TILE, gcol), (ffn_b, FFN_TILE, ucol)],
                       [(FFN_DIM, BF16, FFN_TILE, gcol)], [], "ffn_act_fwd", ncol=nt)
        dn = _mm(act, lw["w_down"], "nn", F32, "mm_down")
        (x2,) = _rows(lambda i, n, *v: _f_post_ffn(*v), T, tm, [_cur(x1), _cur(dn)], [_cst(g_fpost)], [_out(D_MODEL, F32)], [], "post_ffn_fwd")
        saved.append(dict(lw=lw, x=xl, h1=h1, zin=zin, qlat=qlat, kvlat=kvlat, qr=qr, kr=kr, kvpad=kvpad, o=o, xbc=xbc, dtraw=dtraw,
                          yscan=yscan, states=states, ycat=ycat, mixed=mixed, x1=x1, h2=h2, upre=upre, act=act, dn=dn,
                          g_pre=g_pre, g_post=g_post, g_fpre=g_fpre, g_fpost=g_fpost, qn=qn, kvn=kvn, ssd_b=ssd_b,
                          ssd_par=ssd_par, ssd_nw=ssd_nw, ffn_b=ffn_b))
        xl = x2

    gx, loss_part = _rows(_k_loss, T, tm, [_cur(xl), _cur(target)], [], [_out(D_MODEL, F32)], [_acc(1, LANE)], "loss_head")

    GW = {k: [None] * DEPTH for k in ("w_in", "mla_w_q_up", "mla_w_kv_up", "sc_conv_w", "ssd_conv_w", "w_out", "ffn_w_up",
                                      "ffn_conv_w", "ffn_w_down")}
    GS = {k: [None] * DEPTH for k in ("norm_mix_pre", "norm_mix_post", "norm_ffn_pre", "norm_ffn_post", "mla_q_norm", "mla_kv_norm",
                                      "ssd_conv_b", "ssd_dt_bias", "ssd_a_log", "ssd_d", "ssd_norm", "ffn_conv_b")}
    nt = FFN_DIM // FFN_TILE
    gcol, ucol = (lambda j: j), (lambda j: j + nt)
    for l in reversed(range(DEPTH)):
        s = saved[l]
        lw = s["lw"]
        gx1, ddn, dgf = _rows_vjp(_f_post_ffn, T, tm, [s["x1"], s["dn"]], [s["g_fpost"]], [gx], [F32, BF16], "post_ffn_bwd")
        GS["norm_ffn_post"][l] = dgf[0]
        dact = _mm(ddn, lw["w_down"], "nt", F32, "mm_down_dx")
        GW["ffn_w_down"][l] = _mm(s["act"], ddn, "tn", BF16, "mm_down_dw")
        up = s["upre"]
        dug, duu, dwg, dwu, dbg, dbu = _rows(
            _k_ffnact_bwd, T, tm_ffn,
            [(up, FFN_TILE, gcol, "cur"), (up, FFN_TILE, ucol, "cur"), (dact, FFN_TILE, gcol, "cur"), (up, FFN_TILE, gcol, "prev"),
             (up, FFN_TILE, ucol, "prev"), (up, FFN_TILE, gcol, "next"), (up, FFN_TILE, ucol, "next"), (dact, FFN_TILE, gcol, "next")],
            [(lw["ffn_w"], FFN_TILE, gcol), (lw["ffn_w"], FFN_TILE, ucol), (s["ffn_b"], FFN_TILE, gcol), (s["ffn_b"], FFN_TILE, ucol)],
            [(FFN_DIM, BF16, FFN_TILE, gcol)] * 2,
            [(HALO, FFN_DIM, FFN_TILE, gcol)] * 2 + [(1, FFN_DIM, FFN_TILE, gcol)] * 2, "ffn_act_bwd", ncol=nt)
        GW["ffn_conv_w"][l] = jnp.concatenate([dwg[:3], dwu[:3]], axis=1)
        GS["ffn_conv_b"][l] = jnp.concatenate([dbg[0], dbu[0]])
        dh2 = _mm((dug, duu), lw["w_up"], "nt", F32, "mm_up_dx")
        GW["ffn_w_up"][l] = (_mm(s["h2"], dug, "tn", BF16, "mm_up_dw_gate"), _mm(s["h2"], duu, "tn", BF16, "mm_up_dw_up"))
        gx0, dmixed, dgp, dgf = _rows_vjp(_f_post_mix, T, tm, [s["x"], s["mixed"]], [s["g_post"], s["g_fpre"]], [gx1, dh2],
                                          [F32, BF16], "post_mix_bwd")
        GS["norm_mix_post"][l], GS["norm_ffn_pre"][l] = dgp[0], dgf[0]
        dycat = _mm(dmixed, lw["w_out"], "nt", F32, "mm_out_dx")
        GW["w_out"][l] = _unpad_out_rows(_mm(s["ycat"], dmixed, "tn", BF16, "mm_out_dw"))
        zin = s["zin"]
        dyscan, dz, dnw = _rows(_vjp_wrap(_f_ssd_gate, 2, 1), T, tm,
                                [_cur(s["yscan"]), _cur(zin, SSD_DIM, Z_SSZ // SSD_DIM), _cur(dycat, SSD_DIM, (HEADS * HP + SC_DIM) // SSD_DIM)],
                                [_cst(s["ssd_nw"])], [_out(SSD_DIM, F32), _out(SSD_DIM, BF16)], [_acc(1, SSD_DIM)], "ssd_gate_bwd")
        GS["ssd_norm"][l] = dnw[0]
        dxbc, ddtraw, dpar = _ssd_bwd(s["xbc"], s["dtraw"], s["ssd_par"], s["states"], dyscan, T)
        GS["ssd_dt_bias"][l], GS["ssd_a_log"][l], GS["ssd_d"][l] = dpar[0, :SSD_HEADS], dpar[1, :SSD_HEADS], dpar[2, :SSD_HEADS]
        xb = Z_XBC // SSD_CONV_DIM
        dxraw, dsw, dsb = _rows(_k_ssdconv_bwd, T, tm,
                                [_cur(zin, SSD_CONV_DIM, xb), _cur(dxbc), _halo(zin, "prev", SSD_CONV_DIM, xb),
                                 _halo(zin, "next", SSD_CONV_DIM, xb), _halo(dxbc, "next")],
                                [_cst(lw["ssd_w"]), _cst(s["ssd_b"])], [_out(SSD_CONV_DIM, BF16)],
                                [_acc(HALO, SSD_CONV_DIM), _acc(1, SSD_CONV_DIM)], "ssd_conv_bwd")
        GW["ssd_conv_w"][l] = dsw[:4]
        GS["ssd_conv_b"][l] = dsb[0]
        cb = (HEADS * HP) // SC_DIM
        dscb, dscc, dsch, dscw = _rows(_k_sconv_bwd, T, tm,
                                       [_cur(zin, SC_DIM, Z_SCB // SC_DIM), _cur(zin, SC_DIM, Z_SCC // SC_DIM),
                                        _cur(zin, SC_DIM, Z_SCH // SC_DIM), _cur(dycat, SC_DIM, cb),
                                        _halo(zin, "prev", SC_DIM, Z_SCC // SC_DIM), _halo(zin, "prev", SC_DIM, Z_SCH // SC_DIM),
                                        _halo(zin, "next", SC_DIM, Z_SCB // SC_DIM), _halo(dycat, "next", SC_DIM, cb)],
                                       [_cst(lw["sc_w"])], [_out(SC_DIM, BF16)] * 3, [_acc(HALO, SC_DIM)], "short_conv_bwd")
        GW["sc_conv_w"][l] = dscw[:3]
        dq, dk, dv = _flash_bwd(s["qr"], s["kr"], s["kvpad"], s["o"], dycat, T)
        dqpad, dkvpad, dkr = _rows(_k_rope_bwd, T, tm, [_cur(dq), _cur(dk), _cur(dv), _cur(cosf), _cur(sina), _cur(sinb)], [],
                                   [_out(HEADS * HP, BF16), _out(2 * HEADS * HP, BF16), _out(LANE, BF16)], [], "mla_rope_bwd")
        dqlat = _mm(dqpad, lw["w_q"], "nt", F32, "mm_q_dx")
        GW["mla_w_q_up"][l] = _unpad_heads(_mm(s["qlat"], dqpad, "tn", BF16, "mm_q_dw"), NOPE + ROPE)
        dkvlat = _mm(dkvpad, lw["w_kv"], "nt", F32, "mm_kv_dx")
        GW["mla_w_kv_up"][l] = _unpad_kv(_mm(s["kvlat"], dkvpad, "tn", BF16, "mm_kv_dw"))
        dcq, dckv, dqn, dkvn = _rows(_vjp_wrap(_f_mla_pre, 2, 2), T, tm,
                                     [_cur(zin, Q_LORA, 0), _cur(zin, KV_LORA, Z_CKV // KV_LORA), _cur(dqlat), _cur(dkvlat)],
                                     [_cst(s["qn"]), _cst(s["kvn"])], [_out(Q_LORA, BF16), _out(KV_LORA, BF16)],
                                     [_acc(1, Q_LORA), _acc(1, KV_LORA)], "mla_pre_bwd")
        GS["mla_q_norm"][l], GS["mla_kv_norm"][l] = dqn[0], dkvn[0]
        dzin = jnp.concatenate([dcq, dckv, dkr, dscb, dscc, dsch, dz, dxraw, ddtraw.astype(BF16), jnp.zeros((T, ZIN - Z_DT - LANE), BF16)], axis=1)
        dh1 = _mm(dzin, lw["w_in"], "nt", F32, "mm_in_dx")
        GW["w_in"][l] = _unpad_cols_in(_mm(s["h1"], dzin, "tn", BF16, "mm_in_dw"))
        gx, dgp = _rows(_vjp_wrap(_f_premix, 1, 1, add_first=True), T, tm, [_cur(s["x"]), _cur(dh1), _cur(gx0)], [_cst(s["g_pre"])],
                        [_out(D_MODEL, F32)], [_acc(1, D_MODEL)], "pre_mix_bwd")
        GS["norm_mix_pre"][l] = dgp[0]
    GS = {k: jnp.stack(v) for k, v in GS.items()}
    return loss_part[0, 0], gx, GW, GS


WEIGHTS = ("norm_mix_pre", "norm_mix_post", "norm_ffn_pre", "norm_ffn_post", "w_in", "mla_q_norm", "mla_w_q_up", "mla_kv_norm",
           "mla_w_kv_up", "sc_conv_w", "ssd_conv_w", "ssd_conv_b", "ssd_dt_bias", "ssd_a_log", "ssd_d", "ssd_norm", "w_out",
           "ffn_w_up", "ffn_conv_w", "ffn_conv_b", "ffn_w_down")
SHARDED = (("w_in", 2), ("mla_w_q_up", 2), ("mla_w_kv_up", 2), ("sc_conv_w", 2), ("ssd_conv_w", 2), ("w_out", 1),
           ("ffn_w_up", 2), ("ffn_conv_w", 2), ("ffn_w_down", 1))
SMALL = tuple(n for n in WEIGHTS if n not in dict(SHARDED))
N_CHIPS = 4
N_DEV = 8
ROW_ALIGN = 256
SLAB_ALIGN = 16
MAIN = ("ffn_w_down", "w_out", "w_in", "mla_w_q_up", "mla_w_kv_up", "sc_conv_w", "ssd_conv_w")
WIDE = ("ffn_w_up", "ffn_conv_w")


def _is_rows(shape, width):
    return shape[-1] == width and math.prod(shape[:-1]) % SLAB_ALIGN == 0


def _slab_rows(shape, width):
    if _is_rows(shape, width):
        return math.prod(shape[:-1])
    return -(-math.prod(shape) // (width * SLAB_ALIGN)) * SLAB_ALIGN


def _slab(piece, width, dtype, lead=0):
    ld, shape = piece.shape[:lead], piece.shape[lead:]
    rows = _slab_rows(shape, width)
    if _is_rows(shape, width):
        return piece.astype(dtype).reshape(ld + (rows, width))
    flat = piece.astype(dtype).reshape(ld + (-1,))
    return jnp.pad(flat, [(0, 0)] * lead + [(0, rows * width - flat.shape[-1])]).reshape(ld + (rows, width))


def _unslab(slab, shape, lead=0):
    ld = slab.shape[:lead]
    if _is_rows(shape, slab.shape[-1]):
        return slab.reshape(ld + tuple(shape))
    return slab.reshape(ld + (-1,))[..., :math.prod(shape)].reshape(ld + tuple(shape))


def _layout(shapes, per_layer):
    out = {}
    for buf, names in (("main", MAIN), ("wide", WIDE)):
        width = PACK_COLS if buf == "main" else shapes["ffn_w_up"][-1]
        ents, off = [], 0
        for n in names:
            shp = tuple(shapes[n])
            if n.endswith("conv_w"):
                todo = [(None, False, shp), (None, True, shp)]
            elif per_layer:
                todo = [(l, False, shp[1:]) for l in range(shp[0])]
            else:
                todo = [(None, False, shp)]
            for l, lo, ps in todo:
                r = _slab_rows(ps, width)
                ents.append((n, l, lo, ps, off, r))
                off += r
        out[buf] = (width, -(-off // ROW_ALIGN) * ROW_ALIGN, ents)
    return out


def _pack(layout, piece, dtype, lead=0):
    width, rows, ents = layout
    slabs, ld = [], None
    for n, l, lo, ps, off, r in ents:
        p = piece(n, l, lo)
        slabs.append(None if p is None else _slab(p, width, dtype, lead))
        ld = ld if p is None else p.shape[:lead]
    used = ents[-1][4] + ents[-1][5]
    slabs = [jnp.zeros(ld + (e[5], width), dtype) if s is None else s for s, e in zip(slabs, ents)]
    if rows > used:
        slabs.append(jnp.zeros(ld + (rows - used, width), dtype))
    return jnp.concatenate(slabs, axis=lead)


ANY = pl.BlockSpec(memory_space=pl.ANY)


def _pos():
    return lax.axis_index("x"), lax.axis_index("y"), lax.axis_index("c")


def _other_chips(x, y):
    return ((1 - x, y), (x, 1 - y), (1 - x, 1 - y))


def _remote(src, dst, ssem, rsem, dev):
    return pltpu.make_async_remote_copy(src_ref=src, dst_ref=dst, send_sem=ssem, recv_sem=rsem, device_id=dev, device_id_type=MESH)


AG_CHUNKS = 2


def _all_gather_weights(wpk):
    R, C = wpk.shape
    H = R // 2
    CH = H // AG_CHUNKS
    n = 3 * AG_CHUNKS

    def body(w_ref, out_ref, isend, irecv, dsend, drecv):
        x, y, c = _pos()
        k = 2 * x + y
        sib = (x, y, 1 - c)
        chips = _other_chips(x, y)

        def rows(kk, half, ch):
            return out_ref.at[kk, pl.ds(half * H + ch * CH, CH), :]

        first = []
        for p, (cx, cy) in enumerate(chips):
            for ch in range(AG_CHUNKS):
                s = p * AG_CHUNKS + ch
                cp = _remote(w_ref.at[pl.ds(c * H + ch * CH, CH), :], rows(k, c, ch), isend.at[s], irecv.at[s], (cx, cy, c))
                cp.start()
                first.append(cp)
        passed = []
        for p, (cx, cy) in enumerate(chips):
            for ch in range(AG_CHUNKS):
                s = p * AG_CHUNKS + ch
                land = rows(2 * cx + cy, c, ch)
                _remote(land, land, isend.at[s], irecv.at[s], (cx, cy, c)).wait_recv()
                fw = _remote(land, land, dsend.at[s], drecv.at[s], sib)
                fw.start()
                passed.append(fw)
        for p, (cx, cy) in enumerate(chips):
            for ch in range(AG_CHUNKS):
                s = p * AG_CHUNKS + ch
                land = rows(2 * cx + cy, 1 - c, ch)
                _remote(land, land, dsend.at[s], drecv.at[s], sib).wait_recv()
        for cp in first + passed:
            cp.wait_send()

    got = pl.pallas_call(
        body, name="all_gather_weights", in_specs=[ANY], out_specs=ANY,
        out_shape=jax.ShapeDtypeStruct((N_CHIPS, R, C), wpk.dtype),
        scratch_shapes=[pltpu.SemaphoreType.DMA((n,))] * 4,
    )(wpk)
    return lax.dynamic_update_slice(got, wpk[None], (2 * lax.axis_index("x") + lax.axis_index("y"), 0, 0))


def _rs_pair_exchange(g):
    _, R, C = g.shape
    H = R // 2

    def body(g_ref, got_ref, ssem, rsem):
        x, y, c = _pos()
        sib = (x, y, 1 - c)
        cps = []
        for kk in range(N_CHIPS):
            cp = _remote(g_ref.at[kk, pl.ds((1 - c) * H, H), :], got_ref.at[kk], ssem.at[kk], rsem.at[kk], sib)
            cp.start()
            cps.append(cp)
        for cp in cps:
            cp.wait()

    return pl.pallas_call(
        body, name="rs_pair_exchange", in_specs=[ANY], out_specs=ANY,
        out_shape=jax.ShapeDtypeStruct((N_CHIPS, H, C), g.dtype),
        scratch_shapes=[pltpu.SemaphoreType.DMA((N_CHIPS,))] * 2,
    )(g)


def _rs_chip_exchange(p):
    _, H, C = p.shape

    def body(p_ref, out_ref, ssem, rsem):
        x, y, c = _pos()
        k = 2 * x + y
        chips = _other_chips(x, y)
        cps = []
        for s, (cx, cy) in enumerate(chips):
            cp = _remote(p_ref.at[2 * cx + cy], out_ref.at[k], ssem.at[s], rsem.at[s], (cx, cy, c))
            cp.start()
            cps.append(cp)
        for s, (cx, cy) in enumerate(chips):
            land = out_ref.at[2 * cx + cy]
            _remote(land, land, ssem.at[s], rsem.at[s], (cx, cy, c)).wait_recv()
        for cp in cps:
            cp.wait_send()

    k = 2 * lax.axis_index("x") + lax.axis_index("y")
    got = pl.pallas_call(
        body, name="rs_chip_exchange", in_specs=[ANY], out_specs=ANY,
        out_shape=jax.ShapeDtypeStruct(p.shape, p.dtype),
        scratch_shapes=[pltpu.SemaphoreType.DMA((3,)), pltpu.SemaphoreType.DMA((3,))],
    )(p)
    return lax.dynamic_update_slice(got, lax.dynamic_slice_in_dim(p, k, 1, axis=0), (k, 0, 0))


def _rs_pair_share(f):
    H, C = f.shape

    def body(f_ref, out_ref, ssem, rsem):
        x, y, c = _pos()
        cp = _remote(f_ref, out_ref.at[c], ssem, rsem, (x, y, 1 - c))
        cp.start()
        land = out_ref.at[1 - c]
        _remote(land, land, ssem, rsem, (x, y, 1 - c)).wait_recv()
        cp.wait_send()

    got = pl.pallas_call(
        body, name="rs_pair_share", in_specs=[ANY], out_specs=ANY,
        out_shape=jax.ShapeDtypeStruct((2, H, C), f.dtype),
        scratch_shapes=[pltpu.SemaphoreType.DMA, pltpu.SemaphoreType.DMA],
    )(f)
    return lax.dynamic_update_slice(got, f[None], (lax.axis_index("c"), 0, 0))


def _all_reduce_small(s):
    r, C = s.shape

    def body(s_ref, o_ref, buf, ssem, rsem):
        x, y, c = _pos()
        me = 4 * x + 2 * y + c
        buf[me] = s_ref[...]
        cps = []
        for m in range(1, N_DEV):
            mx, my, mc = (m >> 2) & 1, (m >> 1) & 1, m & 1
            peer = (x ^ mx, y ^ my, c ^ mc)
            cp = _remote(s_ref, buf.at[me], ssem.at[m - 1], rsem.at[m - 1], peer)
            cp.start()
            cps.append(cp)
        for m in range(1, N_DEV):
            mx, my, mc = (m >> 2) & 1, (m >> 1) & 1, m & 1
            src = 4 * (x ^ mx) + 2 * (y ^ my) + (c ^ mc)
            _remote(s_ref, buf.at[src], ssem.at[m - 1], rsem.at[m - 1], (x ^ mx, y ^ my, c ^ mc)).wait_recv()
        for cp in cps:
            cp.wait_send()
        acc = buf[0]
        for j in range(1, N_DEV):
            acc = acc + buf[j]
        o_ref[...] = acc

    return pl.pallas_call(
        body, name="all_reduce_small", in_specs=[pl.BlockSpec(memory_space=pltpu.VMEM)],
        out_specs=pl.BlockSpec(memory_space=pltpu.VMEM), out_shape=jax.ShapeDtypeStruct((r, C), F32),
        scratch_shapes=[pltpu.VMEM((N_DEV, r, C), F32), pltpu.SemaphoreType.DMA((N_DEV - 1,)), pltpu.SemaphoreType.DMA((N_DEV - 1,))],
    )(s)


def _rtile(n, pref):
    if n <= pref:
        return n
    t = (pref // 16) * 16
    while t >= 16:
        if n % t == 0:
            return t
        t -= 16
    raise ValueError(f"no row tile for {n}")


def _reduce_scatter_grads(gpk):
    _, R, C = gpk.shape
    H = R // 2
    got = _rs_pair_exchange(gpk)
    own = lax.dynamic_index_in_dim(gpk.reshape(N_CHIPS, 2, H, C), lax.axis_index("c"), axis=1, keepdims=False)
    tm = _rtile(N_CHIPS * H, 512)
    (part,) = _rows(lambda i, n, a, b: (a.astype(F32) + b.astype(F32),), N_CHIPS * H, tm,
                    [_cur(own.reshape(N_CHIPS * H, C)), _cur(got.reshape(N_CHIPS * H, C))], [], [_out(C, BF16)], [], "rs_pair_add")
    parts = _rs_chip_exchange(part.reshape(N_CHIPS, H, C)).reshape(N_CHIPS * H, C)
    tm = _rtile(H, 1024)
    hb = H // tm

    def add4(i, n, a, b, c, d):
        return (((a.astype(F32) + b.astype(F32)) + c.astype(F32)) + d.astype(F32),)

    (red,) = _rows(add4, H, tm, [(parts, C, functools.partial(_const, v=0), j * hb) for j in range(N_CHIPS)], [], [_out(C, F32)], [],
                   "rs_chip_add")
    return _rs_pair_share(red).reshape(R, C)


def _adam(w, g, m, v, name, g_row=0):
    shp = w.shape
    two = lambda a: a.reshape(-1, shp[-1])
    rows = math.prod(shp[:-1])
    tm = _rtile(rows, 256)
    assert g_row % tm == 0
    g_in = (two(g), shp[-1], functools.partial(_const, v=0), g_row // tm)
    res = _rows(_k_adam, rows, tm, [_cur(two(w)), g_in, _cur(two(m)), _cur(two(v))], [], [_out(shp[-1], F32)] * 4, [], name)
    return tuple(r.reshape(shp) for r in res)


def _pack_flat(parts, rows):
    flat = jnp.concatenate([p.astype(F32).reshape(-1) for p in parts])
    return jnp.pad(flat, (0, rows * PACK_COLS - flat.shape[0])).reshape(rows, PACK_COLS)


def _unpack_flat(buf, shapes):
    flat, out, off = buf.reshape(-1), [], 0
    for shp in shapes:
        n = math.prod(shp)
        out.append(flat[off:off + n].reshape(shp))
        off += n
    return out


def kernel(x, positions, norm_mix_pre, norm_mix_post, norm_ffn_pre, norm_ffn_post, w_in, mla_q_norm, mla_w_q_up, mla_kv_norm, mla_w_kv_up, sc_conv_w, ssd_conv_w, ssd_conv_b, ssd_dt_bias, ssd_a_log, ssd_d, ssd_norm, w_out, ffn_w_up, ffn_conv_w, ffn_conv_b, ffn_w_down, loss_target, m_norm_mix_pre, m_norm_mix_post, m_norm_ffn_pre, m_norm_ffn_post, m_w_in, m_mla_q_norm, m_mla_w_q_up, m_mla_kv_norm, m_mla_w_kv_up, m_sc_conv_w, m_ssd_conv_w, m_ssd_conv_b, m_ssd_dt_bias, m_ssd_a_log, m_ssd_d, m_ssd_norm, m_w_out, m_ffn_w_up, m_ffn_conv_w, m_ffn_conv_b, m_ffn_w_down, v_norm_mix_pre, v_norm_mix_post, v_norm_ffn_pre, v_norm_ffn_post, v_w_in, v_mla_q_norm, v_mla_w_q_up, v_mla_kv_norm, v_mla_w_kv_up, v_sc_conv_w, v_ssd_conv_w, v_ssd_conv_b, v_ssd_dt_bias, v_ssd_a_log, v_ssd_d, v_ssd_norm, v_w_out, v_ffn_w_up, v_ffn_conv_w, v_ffn_conv_b, v_ffn_w_down):
    a = dict(locals())
    axis = dict(SHARDED)
    shard_shapes = {n: a[n].shape for n in axis}

    def weight_piece(n, l, lo):
        return a[n] - a[n].astype(BF16).astype(F32) if lo else a[n]

    W, resid = {}, {}
    for buf, (width, rows, ents) in _layout(shard_shapes, per_layer=False).items():
        gathered = _all_gather_weights(_pack((width, rows, ents), weight_piece, BF16))
        for n, l, lo, ps, off, r in ents:
            parts = _unslab(gathered[:, off:off + r], ps, lead=1)
            full = jnp.moveaxis(parts, 0, axis[n])
            full = full.reshape(full.shape[:axis[n]] + (-1,) + full.shape[axis[n] + 2:])
            (resid if lo else W)[n] = full
    for n in resid:
        W[n] = W[n].astype(F32) + resid[n].astype(F32)
    S = {n: a[n] for n in SMALL}

    loss_part, gx, GW, GS = _local_step(a["x"][0], a["positions"][0], a["loss_target"][0], W, S)

    def by_chip(g, ax, parts=N_CHIPS):
        g = g.reshape(g.shape[:ax] + (parts, g.shape[ax] // parts) + g.shape[ax + 1:])
        return jnp.moveaxis(g, ax, 0)

    def grad_piece(n, l, lo):
        if lo:
            return None
        if l is None:
            return by_chip(jnp.stack(GW[n]), axis[n])
        g = GW[n][l]
        if isinstance(g, tuple):
            return jnp.concatenate([by_chip(h, axis[n] - 1, N_CHIPS // 2) for h in g])
        return by_chip(g, axis[n] - 1)

    grads, delta, new_m, new_v = {}, {}, {}, {}
    for buf, (width, rows, ents) in _layout(shard_shapes, per_layer=True).items():
        red = _reduce_scatter_grads(_pack((width, rows, ents), grad_piece, BF16, lead=1))
        for n in (MAIN if buf == "main" else WIDE):
            mine = [e for e in ents if e[0] == n and not e[2]]
            if a[n].shape[-1] == width and all(e[3] == (e[5], width) for e in mine):
                g, g_row = red, mine[0][4]
            elif mine[0][1] is None:
                g, g_row = _unslab(red[mine[0][4]:mine[0][4] + mine[0][5]], mine[0][3]), 0
            else:
                g, g_row = jnp.stack([_unslab(red[e[4]:e[4] + e[5]], e[3]) for e in mine]), 0
            grads[n], delta[n], new_m[n], new_v[n] = _adam(a[n], g, a["m_" + n], a["v_" + n], "adamw_" + n, g_row)

    small_shapes = [a[n].shape for n in SMALL]
    rs = -(-(sum(math.prod(s) for s in small_shapes) + 1) // (PACK_COLS * SLAB_ALIGN)) * SLAB_ALIGN
    red = _all_reduce_small(_pack_flat([GS[n] for n in SMALL] + [loss_part.reshape(1)], rs))
    loss = _unpack_flat(red, small_shapes + [(1,)])[-1][0]
    pk = lambda pre: _pack_flat([a[pre + n] for n in SMALL], rs)
    for dst, buf in zip((grads, delta, new_m, new_v), _adam(pk(""), red, pk("m_"), pk("v_"), "adamw_small")):
        dst.update(zip(SMALL, _unpack_flat(buf, small_shapes)))

    return (loss, gx[None], *[grads[n] for n in WEIGHTS], *[delta[n] for n in WEIGHTS], *[new_m[n] for n in WEIGHTS],
            *[new_v[n] for n in WEIGHTS])
```

```python
import functools
import math

import jax
import jax.numpy as jnp
from jax import lax
from jax.experimental import pallas as pl
from jax.experimental.pallas import tpu as pltpu

F32 = jnp.float32
BF16 = jnp.bfloat16
MXU_DTYPE = jnp.bfloat16
HIGHEST = lax.Precision.HIGHEST
MESH = pl.DeviceIdType.MESH

D_MODEL = 1024
DEPTH = 4
HEADS = 8
Q_LORA = 256
KV_LORA = 128
NOPE = 64
ROPE = 32
VDIM = 64
ROPE_THETA = 10000.0
SC_DIM = 256
SSD_HEADS = 4
SSD_HEAD_DIM = 64
SSD_STATE = 128
SSD_DIM = 256
SSD_CONV_DIM = 768
SSD_CHUNK = 128
FFN_DIM = 2816
NORM_EPS = 1e-6
QK_SCALE = (NOPE + ROPE) ** -0.5
LANE = 128
HP = 128
FLASH_HEADS = 2

ZIN = 2560
Z_CQ, Z_CKV, Z_KR, Z_SCB, Z_SCC, Z_SCH, Z_SSZ, Z_XBC, Z_DT = 0, 256, 384, 512, 768, 1024, 1280, 1536, 2304
KR_LANE = 64
YCAT = HEADS * HP + SC_DIM + SSD_DIM
FFN_TILE = 256
FFN_ROWS = 1024

ADAM_LR, ADAM_B1, ADAM_B2, ADAM_EPS, ADAM_WD, ADAM_STEP = 0.001, 0.9, 0.999, 1e-08, 0.01, 10

PACK_COLS = 1024


def _tile(n, pref):
    if n <= pref:
        return n
    t = (pref // LANE) * LANE
    while t >= LANE:
        if n % t == 0:
            return t
        t -= LANE
    raise ValueError(f"no tile for {n}")


MM_TM, MM_TN, MM_TK = 1024, 1408, 1536


def _mm(a, b, mode, out_dtype, name, tm=None, tn=MM_TN, tkmax=MM_TK):
    pair = isinstance(a, tuple)
    a_list = list(a) if pair else [a]
    layer = None
    if isinstance(b, tuple):
        b, layer = b
    bshape = b.shape[-2:]
    if mode == "nn":
        (M, Ka), (_, N) = a_list[0].shape, bshape
    elif mode == "nt":
        (M, Ka), (N, _) = a_list[0].shape, bshape
    else:
        (Ka, M), (_, N) = a_list[0].shape, bshape
    tm = (MM_TN if mode == "tn" else MM_TM) if tm is None else tm
    tm, tn, tk = _tile(M, tm), _tile(N, tn), _tile(Ka, tkmax)
    nka = Ka // tk
    nk = nka * len(a_list)

    def bspec(shape, index):
        if layer is None:
            return pl.BlockSpec(shape, index)
        return pl.BlockSpec((None,) + shape, lambda i, j, k: (layer,) + index(i, j, k))

    if mode == "nn":
        a_specs = [pl.BlockSpec((tm, tk), lambda i, j, k: (i, jnp.minimum(k, nka - 1))),
                   pl.BlockSpec((tm, tk), lambda i, j, k: (i, jnp.maximum(k - nka, 0)))][:len(a_list)]
        b_spec = bspec((tk, tn), lambda i, j, k: (k, j))
        dims = NN
    elif mode == "nt":
        a_specs = [pl.BlockSpec((tm, tk), lambda i, j, k: (i, jnp.minimum(k, nka - 1))),
                   pl.BlockSpec((tm, tk), lambda i, j, k: (i, jnp.maximum(k - nka, 0)))][:len(a_list)]
        b_spec = bspec((tn, tk), lambda i, j, k: (j, k))
        dims = NT
    else:
        a_specs = [pl.BlockSpec((tk, tm), lambda i, j, k: (k, i))]
        b_spec = pl.BlockSpec((tk, tn), lambda i, j, k: (k, j))
        dims = TN
    na = len(a_list)

    def body(*refs):
        a_refs, b_ref, o_ref = refs[:na], refs[na], refs[na + 1]
        k = pl.program_id(2)

        def prod(a_ref):
            return lax.dot_general(a_ref[...].astype(MXU_DTYPE), b_ref[...].astype(MXU_DTYPE), dims, preferred_element_type=F32)

        if nk == 1:
            o_ref[...] = prod(a_refs[0]).astype(o_ref.dtype)
            return
        acc_ref = refs[na + 2]

        @pl.when(k == 0)
        def _():
            acc_ref[...] = prod(a_refs[0])

        @pl.when((k > 0) & (k < nka))
        def _():
            acc_ref[...] += prod(a_refs[0])

        if pair:
            @pl.when(k >= nka)
            def _():
                acc_ref[...] += prod(a_refs[1])

        @pl.when(k == nk - 1)
        def _():
            o_ref[...] = acc_ref[...].astype(o_ref.dtype)

    return pl.pallas_call(
        body, name=name, grid=(M // tm, N // tn, nk),
        in_specs=a_specs + [b_spec], out_specs=pl.BlockSpec((tm, tn), lambda i, j, k: (i, j)),
        out_shape=jax.ShapeDtypeStruct((M, N), out_dtype),
        scratch_shapes=[pltpu.VMEM((tm, tn), F32)] if nk > 1 else [],
        compiler_params=pltpu.CompilerParams(dimension_semantics=("parallel", "parallel", "arbitrary")),
    )(*a_list, b)


HALO = 8


def _const(j, v):
    return v


def _rows(fn, T, tm, ins, consts, outs, accs, name, ncol=1):
    n = T // tm
    hb = tm // HALO
    last = T // HALO - 1
    in_specs, args = [], []
    for arr, bc, cb, kind in ins:
        if isinstance(kind, int):
            in_specs.append(pl.BlockSpec((tm, bc), lambda j, i, cb=cb, off=kind: (i + off, cb(j))))
        elif kind == "cur":
            in_specs.append(pl.BlockSpec((tm, bc), lambda j, i, cb=cb: (i, cb(j))))
        elif kind == "prev":
            in_specs.append(pl.BlockSpec((HALO, bc), lambda j, i, cb=cb: (jnp.maximum(i * hb - 1, 0), cb(j))))
        else:
            in_specs.append(pl.BlockSpec((HALO, bc), lambda j, i, cb=cb: (jnp.minimum((i + 1) * hb, last), cb(j))))
        args.append(arr)
    for arr, bc, cb in consts:
        in_specs.append(pl.BlockSpec((arr.shape[0], bc), lambda j, i, cb=cb: (0, cb(j))))
        args.append(arr)
    out_specs, out_shape = [], []
    for tc, dt, bc, cb in outs:
        out_specs.append(pl.BlockSpec((tm, bc), lambda j, i, cb=cb: (i, cb(j))))
        out_shape.append(jax.ShapeDtypeStruct((T, tc), dt))
    for r, tc, bc, cb in accs:
        out_specs.append(pl.BlockSpec((r, bc), lambda j, i, cb=cb: (0, cb(j))))
        out_shape.append(jax.ShapeDtypeStruct((r, tc), F32))
    nin, nout, nacc = len(args), len(outs), len(accs)

    def body(*refs):
        i = pl.program_id(1)
        res = fn(i, n, *[r[...] for r in refs[:nin]])
        for r, v in zip(refs[nin:nin + nout], res[:nout]):
            r[...] = v.astype(r.dtype)
        if nacc:
            acc_refs = refs[nin + nout:nin + nout + nacc]

            @pl.when(i == 0)
            def _():
                for r in acc_refs:
                    r[...] = jnp.zeros_like(r)

            for r, v in zip(acc_refs, res[nout:]):
                r[...] += v.astype(F32)

    res = pl.pallas_call(
        body, name=name, grid=(ncol, n), in_specs=in_specs, out_specs=out_specs, out_shape=out_shape,
        compiler_params=pltpu.CompilerParams(dimension_semantics=("arbitrary", "arbitrary")),
    )(*args)
    return res


def _cur(arr, bc=None, blk=0):
    bc = arr.shape[1] if bc is None else bc
    return (arr, bc, functools.partial(_const, v=blk), "cur")


def _halo(arr, kind, bc=None, blk=0):
    bc = arr.shape[1] if bc is None else bc
    return (arr, bc, functools.partial(_const, v=blk), kind)


def _cst(arr):
    return (arr, arr.shape[1], functools.partial(_const, v=0))


def _out(cols, dt):
    return (cols, dt, cols, functools.partial(_const, v=0))


def _acc(rows, cols):
    return (rows, cols, cols, functools.partial(_const, v=0))


def _rms(x, w):
    return x * lax.rsqrt(jnp.mean(x * x, axis=-1, keepdims=True) + NORM_EPS) * w


def _sigmoid(x):
    return 0.5 * jnp.tanh(0.5 * x) + 0.5


def _silu(x):
    return x * _sigmoid(x)


def _dsilu(x):
    s = _sigmoid(x)
    return s * (1.0 + x * (1.0 - s))


def _softplus(x):
    return jnp.maximum(x, 0.0) + jnp.log1p(jnp.exp(-jnp.abs(x)))


def _shift(a, k):
    return pltpu.roll(a, k % a.shape[0], 0)


def _lroll(a, k):
    return pltpu.roll(a, k % a.shape[1], 1)


def _vjp_wrap(f, nrow, nconst, add_first=False):
    def g(i, n, *vals):
        rows, consts, mid = vals[:nrow], vals[len(vals) - nconst:], vals[nrow:len(vals) - nconst]
        cots = mid[:-1] if add_first else mid
        outs, pull = jax.vjp(f, *rows, *consts)
        grads = list(pull(tuple(c.astype(o.dtype) for c, o in zip(cots, outs))))
        if add_first:
            grads[0] = grads[0] + mid[-1]
        return tuple(grads)
    return g


def _rows_vjp(f, T, tm, rows, consts, cots, out_dtypes, name):
    return _rows(_vjp_wrap(f, len(rows), len(consts)), T, tm, [_cur(r) for r in rows] + [_cur(c) for c in cots],
                 [_cst(c) for c in consts], [_out(r.shape[1], dt) for r, dt in zip(rows, out_dtypes)],
                 [_acc(1, c.shape[1]) for c in consts], name)


def _f_premix(x, g):
    return (_rms(x, g),)


def _f_mla_pre(cq, ckv, qn, kvn):
    return _rms(cq, qn), _rms(ckv, kvn)


def _f_ssd_gate(y, z, nw):
    return (_rms(y * _silu(z), nw),)


def _f_post_mix(x, mixed, gpost, gffn):
    x1 = x + _rms(mixed, gpost)
    return x1, _rms(x1, gffn)


def _f_post_ffn(x1, d, gpost):
    return (x1 + _rms(d, gpost),)


def _rope_fwd(v, cosf, sina, sinb):
    return v * cosf + _lroll(v, -16) * sina + _lroll(v, 16) * sinb


def _rope_bwd(g, cosf, sina, sinb):
    return g * cosf + _lroll(g * sina, 16) + _lroll(g * sinb, -16)


def _k_rope_fwd(i, n, qpad, kvpad, kr, cosf, sina, sinb):
    qs, ks = [], []
    krr = _rope_fwd(kr, cosf, sina, sinb)
    for h in range(HEADS):
        sl = slice(h * HP, (h + 1) * HP)
        qs.append(_rope_fwd(qpad[:, sl], cosf, sina, sinb))
        ks.append(kvpad[:, sl].astype(F32) + krr)
    return jnp.concatenate(qs, axis=1), jnp.concatenate(ks, axis=1)


def _k_rope_bwd(i, n, dq, dk, dv, cosf, sina, sinb):
    lane = lax.broadcasted_iota(jnp.int32, (1, HP), 1)
    rmask = ((lane >= KR_LANE) & (lane < KR_LANE + ROPE)).astype(F32)
    dqs, dks = [], []
    dkr = jnp.zeros((dq.shape[0], HP), F32)
    for h in range(HEADS):
        sl = slice(h * HP, (h + 1) * HP)
        dqs.append(_rope_bwd(dq[:, sl], cosf, sina, sinb))
        dkh = dk[:, sl]
        dkr = dkr + dkh * rmask
        dks.append(dkh * (1.0 - rmask))
    dkr = _rope_bwd(dkr, cosf, sina, sinb) * rmask
    return jnp.concatenate(dqs, axis=1), jnp.concatenate(dks + [dv], axis=1), dkr


def _k_sconv_fwd(i, n, b, c, h, cp, hp, w):
    m = b.shape[0]
    up = jnp.where(i > 0, cp * hp, 0.0)
    ue = jnp.concatenate([up, c * h], axis=0)
    conv = w[2:3] * ue + w[1:2] * _shift(ue, 1) + w[0:1] * _shift(ue, 2)
    return (b * conv[HALO:],)


def _k_sconv_bwd(i, n, b, c, h, dy, cp, hp, bn, dyn, w):
    m = b.shape[0]
    up = jnp.where(i > 0, cp * hp, 0.0)
    ue = jnp.concatenate([up, c * h], axis=0)
    u1, u2 = _shift(ue, 1), _shift(ue, 2)
    conv = (w[2:3] * ue + w[1:2] * u1 + w[0:1] * u2)[HALO:]
    dc_cur = dy * b
    dce = jnp.concatenate([dc_cur, jnp.where(i < n - 1, dyn * bn, 0.0)], axis=0)
    du = (w[2:3] * dce + w[1:2] * _shift(dce, -1) + w[0:1] * _shift(dce, -2))[:m]
    dw = jnp.concatenate([
        jnp.sum(dc_cur * u2[HALO:], axis=0, keepdims=True),
        jnp.sum(dc_cur * u1[HALO:], axis=0, keepdims=True),
        jnp.sum(dc_cur * ue[HALO:], axis=0, keepdims=True),
        jnp.zeros((HALO - 3, b.shape[1]), F32)], axis=0)
    return dy * conv, du * h, du * c, dw


def _conv4(ue, w):
    return w[3:4] * ue + w[2:3] * _shift(ue, 1) + w[1:2] * _shift(ue, 2) + w[0:1] * _shift(ue, 3)


def _k_ssdconv_fwd(i, n, u, up, w, bias):
    ue = jnp.concatenate([jnp.where(i > 0, up, 0.0), u], axis=0)
    return (_silu(_conv4(ue, w)[HALO:] + bias),)


def _k_ssdconv_bwd(i, n, u, dout, up, un, doutn, w, bias):
    m = u.shape[0]
    ue = jnp.concatenate([jnp.where(i > 0, up, 0.0), u, un], axis=0)
    u1, u2, u3 = _shift(ue, 1), _shift(ue, 2), _shift(ue, 3)
    pre = (w[3:4] * ue + w[2:3] * u1 + w[1:2] * u2 + w[0:1] * u3)[HALO:] + bias
    doe = jnp.concatenate([dout, jnp.where(i < n - 1, doutn, 0.0)], axis=0)
    dpre = doe * _dsilu(pre)
    du = (w[3:4] * dpre + w[2:3] * _shift(dpre, -1) + w[1:2] * _shift(dpre, -2) + w[0:1] * _shift(dpre, -3))[:m]
    dp = dpre[:m]
    cur = slice(HALO, HALO + m)
    dw = jnp.concatenate([
        jnp.sum(dp * u3[cur], axis=0, keepdims=True),
        jnp.sum(dp * u2[cur], axis=0, keepdims=True),
        jnp.sum(dp * u1[cur], axis=0, keepdims=True),
        jnp.sum(dp * ue[cur], axis=0, keepdims=True),
        jnp.zeros((HALO - 4, u.shape[1]), F32)], axis=0)
    db = jnp.sum(dp, axis=0, keepdims=True)
    return du, dw, db


def _conv3(ue, w):
    return w[2:3] * ue + w[1:2] * _shift(ue, 1) + w[0:1] * _shift(ue, 2)


def _k_ffnact_fwd(i, n, ug, uu, ugp, uup, wg, wu, bg, bu):
    gate = _conv3(jnp.concatenate([jnp.where(i > 0, ugp, 0.0), ug], axis=0), wg)[HALO:] + bg
    upv = _conv3(jnp.concatenate([jnp.where(i > 0, uup, 0.0), uu], axis=0), wu)[HALO:] + bu
    return (_silu(gate) * upv,)


def _k_ffnact_bwd(i, n, ug, uu, dact, ugp, uup, ugn, uun, dactn, wg, wu, bg, bu):
    m = ug.shape[0]
    cur = slice(HALO, HALO + m)

    def taps(p, c, nx):
        e = jnp.concatenate([jnp.where(i > 0, p, 0.0), c, nx], axis=0)
        return e, _shift(e, 1), _shift(e, 2)

    def back(d, w):
        return (w[2:3] * d + w[1:2] * _shift(d, -1) + w[0:1] * _shift(d, -2))[:m]

    def wgrad(d, t):
        return jnp.concatenate([jnp.sum(d[:m] * t[2][cur], axis=0, keepdims=True), jnp.sum(d[:m] * t[1][cur], axis=0, keepdims=True),
                                jnp.sum(d[:m] * t[0][cur], axis=0, keepdims=True), jnp.zeros((HALO - 3, d.shape[1]), F32)], axis=0)

    tg, tu = taps(ugp, ug, ugn), taps(uup, uu, uun)
    gate = (wg[2:3] * tg[0] + wg[1:2] * tg[1] + wg[0:1] * tg[2])[HALO:] + bg
    upv = (wu[2:3] * tu[0] + wu[1:2] * tu[1] + wu[0:1] * tu[2])[HALO:] + bu
    dae = jnp.concatenate([dact, jnp.where(i < n - 1, dactn, 0.0)], axis=0)
    sg = _sigmoid(gate)
    dg = dae * upv * (sg * (1.0 + gate * (1.0 - sg)))
    dup = dae * (gate * sg)
    return (back(dg, wg), back(dup, wu), wgrad(dg, tg), wgrad(dup, tu),
            jnp.sum(dg[:m], axis=0, keepdims=True), jnp.sum(dup[:m], axis=0, keepdims=True))


def _k_loss(i, n, y, tgt):
    e = y - tgt
    part = 0.5 * jnp.sum(jnp.sum(e * e, axis=1, keepdims=True) / D_MODEL, axis=0, keepdims=True)
    return e * (1.0 / D_MODEL), jnp.broadcast_to(part, (1, LANE))


def _k_adam(i, n, w, g, m, v):
    m = ADAM_B1 * m + (1.0 - ADAM_B1) * g
    v = ADAM_B2 * v + (1.0 - ADAM_B2) * (g * g)
    m_hat = m / (1.0 - ADAM_B1 ** ADAM_STEP)
    v_hat = v / (1.0 - ADAM_B2 ** ADAM_STEP)
    delta = -ADAM_LR * (m_hat / (jnp.sqrt(v_hat) + ADAM_EPS) + ADAM_WD * w)
    return g, delta, m, v


def _dotf(a, b, dims):
    return lax.dot_general(a.astype(MXU_DTYPE), b.astype(MXU_DTYPE), dims, preferred_element_type=F32)


NN = (((1,), (0,)), ((), ()))
NT = (((1,), (1,)), ((), ()))
TN = (((0,), (0,)), ((), ()))


def _ssd_chunk(x0, x1, x2, x3, b0, b1, c0, c1, dtraw, p0, p1, p2, p3, dtb, alog, dsk):
    xs, bs, cs_, ps = (x0, x1, x2, x3), (b0, b1), (c0, c1), (p0, p1, p2, p3)
    L = dtraw.shape[0]
    dt = _softplus(dtraw + dtb)
    adt = dt * (-jnp.exp(alog))
    row = lax.broadcasted_iota(jnp.int32, (L, L), 0)
    col = lax.broadcasted_iota(jnp.int32, (L, L), 1)
    tril = row >= col
    cum = jnp.dot(tril.astype(F32), adt, precision=HIGHEST, preferred_element_type=F32)
    cum_t = cum.T
    lane = lax.broadcasted_iota(jnp.int32, (1, LANE), 1)
    sub = lax.broadcasted_iota(jnp.int32, (LANE, 1), 0)
    lastcol = (lax.broadcasted_iota(jnp.int32, (1, L), 1) == L - 1).astype(F32)
    ys, news = [], []
    for h in range(SSD_HEADS):
        g = h // (SSD_HEADS // 2)
        oh = (lane == h).astype(F32)
        dth = jnp.sum(dt * oh, axis=1, keepdims=True)
        csh = jnp.sum(cum * oh, axis=1, keepdims=True)
        csr = jnp.sum(cum_t * (sub == h).astype(F32), axis=0, keepdims=True)
        cl = jnp.sum(csr * lastcol, axis=1, keepdims=True)
        dskh = jnp.sum(dsk * oh, axis=1, keepdims=True)
        x, bm, cm, prev = xs[h], bs[g], cs_[g], ps[h]
        xdt = x * dth
        decay = jnp.exp(jnp.where(tril, csh - csr, -jnp.inf))
        scores = _dotf(cm, bm, NT) * decay
        y_diag = _dotf(scores, xdt, NN)
        bd = bm * jnp.exp(cl - csh)
        cst = _dotf(xdt, bd, TN)
        news.append(prev * jnp.exp(cl) + cst)
        y_off = _dotf(cm, prev, NT) * jnp.exp(csh)
        ys.append(y_diag + y_off + x * dskh)
    return (*ys, *news)


def _ssd_operands(x_ref, dt_ref, par_ref, prev):
    xs = [x_ref[:, h * SSD_HEAD_DIM:(h + 1) * SSD_HEAD_DIM] for h in range(SSD_HEADS)]
    bs = [x_ref[:, SSD_DIM + g * SSD_STATE:SSD_DIM + (g + 1) * SSD_STATE] for g in range(2)]
    cs_ = [x_ref[:, SSD_DIM + 2 * SSD_STATE + g * SSD_STATE:SSD_DIM + 2 * SSD_STATE + (g + 1) * SSD_STATE] for g in range(2)]
    return (*xs, *bs, *cs_, dt_ref[...], *prev, par_ref[0:1, :], par_ref[1:2, :], par_ref[2:3, :])


def _ssd_fwd(xbc, dtraw, par, T):
    L = SSD_CHUNK
    nc = T // L
    P = SSD_HEAD_DIM

    def body(x_ref, dt_ref, par_ref, y_ref, st_ref, state):
        @pl.when(pl.program_id(0) == 0)
        def _():
            state[...] = jnp.zeros_like(state)

        st_ref[0] = state[...]
        prev = [state[h * P:(h + 1) * P, :] for h in range(SSD_HEADS)]
        res = _ssd_chunk(*_ssd_operands(x_ref, dt_ref, par_ref, prev))
        for h in range(SSD_HEADS):
            y_ref[:, h * P:(h + 1) * P] = res[h]
            state[h * P:(h + 1) * P, :] = res[SSD_HEADS + h]

    return pl.pallas_call(
        body, name="ssd_scan_fwd", grid=(nc,),
        in_specs=[pl.BlockSpec((L, SSD_CONV_DIM), lambda c: (c, 0)), pl.BlockSpec((L, LANE), lambda c: (c, 0)),
                  pl.BlockSpec((8, LANE), lambda c: (0, 0))],
        out_specs=[pl.BlockSpec((L, SSD_DIM), lambda c: (c, 0)), pl.BlockSpec((1, SSD_DIM, SSD_STATE), lambda c: (c, 0, 0))],
        out_shape=[jax.ShapeDtypeStruct((T, SSD_DIM), F32), jax.ShapeDtypeStruct((nc, SSD_DIM, SSD_STATE), F32)],
        scratch_shapes=[pltpu.VMEM((SSD_DIM, SSD_STATE), F32)],
        compiler_params=pltpu.CompilerParams(dimension_semantics=("arbitrary",)),
    )(xbc, dtraw, par)


def _ssd_bwd(xbc, dtraw, par, states, dy, T):
    L = SSD_CHUNK
    nc = T // L
    P = SSD_HEAD_DIM

    def body(x_ref, dt_ref, par_ref, st_ref, dy_ref, dx_ref, ddt_ref, dpar_ref, dstate):
        @pl.when(pl.program_id(0) == 0)
        def _():
            dstate[...] = jnp.zeros_like(dstate)
            dpar_ref[...] = jnp.zeros_like(dpar_ref)

        prev = [st_ref[0, h * P:(h + 1) * P, :] for h in range(SSD_HEADS)]
        prim = _ssd_operands(x_ref, dt_ref, par_ref, prev)
        _, pull = jax.vjp(_ssd_chunk, *prim)
        cots = tuple(dy_ref[:, h * P:(h + 1) * P] for h in range(SSD_HEADS)) + tuple(
            dstate[h * P:(h + 1) * P, :] for h in range(SSD_HEADS))
        g = pull(cots)
        for h in range(SSD_HEADS):
            dx_ref[:, h * P:(h + 1) * P] = g[h]
            dstate[h * P:(h + 1) * P, :] = g[9 + h]
        for k in range(2):
            dx_ref[:, SSD_DIM + k * SSD_STATE:SSD_DIM + (k + 1) * SSD_STATE] = g[4 + k]
            dx_ref[:, SSD_DIM + 2 * SSD_STATE + k * SSD_STATE:SSD_DIM + 2 * SSD_STATE + (k + 1) * SSD_STATE] = g[6 + k]
        ddt_ref[...] = g[8]
        for r in range(3):
            dpar_ref[r:r + 1, :] += g[13 + r]

    rev = lambda c: (nc - 1 - c, 0)
    return pl.pallas_call(
        body, name="ssd_scan_bwd", grid=(nc,),
        in_specs=[pl.BlockSpec((L, SSD_CONV_DIM), rev), pl.BlockSpec((L, LANE), rev), pl.BlockSpec((8, LANE), lambda c: (0, 0)),
                  pl.BlockSpec((1, SSD_DIM, SSD_STATE), lambda c: (nc - 1 - c, 0, 0)), pl.BlockSpec((L, SSD_DIM), rev)],
        out_specs=[pl.BlockSpec((L, SSD_CONV_DIM), rev), pl.BlockSpec((L, LANE), rev), pl.BlockSpec((8, LANE), lambda c: (0, 0))],
        out_shape=[jax.ShapeDtypeStruct((T, SSD_CONV_DIM), F32), jax.ShapeDtypeStruct((T, LANE), F32),
                   jax.ShapeDtypeStruct((8, LANE), F32)],
        scratch_shapes=[pltpu.VMEM((SSD_DIM, SSD_STATE), F32)],
        compiler_params=pltpu.CompilerParams(dimension_semantics=("arbitrary",)),
    )(xbc, dtraw, par, states, dy)


def _causal_pairs(nq, by_query):
    if by_query:
        pairs = [(i, j) for i in range(nq) for j in range(i + 1)]
    else:
        pairs = [(i, j) for j in range(nq) for i in range(j, nq)]
    return jnp.asarray([p[0] for p in pairs], jnp.int32), jnp.asarray([p[1] for p in pairs], jnp.int32)


def _flash_fwd(q, k, kv, T):
    tq = tk = min(512, T)
    nq = T // tq
    G = FLASH_HEADS
    rep = tk // HP

    def body(qi_ref, kj_ref, q_ref, k_ref, v_ref, o_ref, m_ref, l_ref, acc_ref):
        t = pl.program_id(1)
        i, j = qi_ref[t], kj_ref[t]

        @pl.when(j == 0)
        def _():
            m_ref[...] = jnp.full_like(m_ref, -jnp.inf)
            l_ref[...] = jnp.zeros_like(l_ref)
            acc_ref[...] = jnp.zeros_like(acc_ref)

        def step(diagonal):
            for g in range(G):
                sl = slice(g * HP, (g + 1) * HP)
                s = _dotf(q_ref[:, sl], k_ref[:, sl], NT) * QK_SCALE
                if diagonal:
                    rows = lax.broadcasted_iota(jnp.int32, (tq, tk), 0)
                    cols = lax.broadcasted_iota(jnp.int32, (tq, tk), 1)
                    s = jnp.where(rows >= cols, s, -jnp.inf)
                m_old = m_ref[:, sl]
                m_new = jnp.maximum(m_old, jnp.max(s, axis=1, keepdims=True))
                p = jnp.exp(s - jnp.tile(m_new, (1, rep)))
                alpha = jnp.exp(m_old - m_new)
                l_ref[:, sl] = alpha * l_ref[:, sl] + jnp.sum(p, axis=1, keepdims=True)
                acc_ref[:, sl] = alpha * acc_ref[:, sl] + _dotf(p, v_ref[:, sl], NN)
                m_ref[:, sl] = m_new

        @pl.when(j < i)
        def _():
            step(False)

        @pl.when(j == i)
        def _():
            step(True)
            lane = lax.broadcasted_iota(jnp.int32, (tq, HP), 1)
            for g in range(G):
                sl = slice(g * HP, (g + 1) * HP)
                l = l_ref[:, sl]
                o_ref[:, sl] = jnp.where(lane < VDIM, acc_ref[:, sl] / l, m_ref[:, sl] + jnp.log(l))

    W = G * HP
    qi, kj = _causal_pairs(nq, by_query=True)
    return pl.pallas_call(
        body, name="mla_flash_fwd",
        grid_spec=pltpu.PrefetchScalarGridSpec(
            num_scalar_prefetch=2, grid=(HEADS // G, qi.shape[0]),
            in_specs=[pl.BlockSpec((tq, W), lambda h, t, qi, kj: (qi[t], h)),
                      pl.BlockSpec((tk, W), lambda h, t, qi, kj: (kj[t], h)),
                      pl.BlockSpec((tk, W), lambda h, t, qi, kj: (kj[t], HEADS // G + h))],
            out_specs=pl.BlockSpec((tq, W), lambda h, t, qi, kj: (qi[t], h)),
            scratch_shapes=[pltpu.VMEM((tq, W), F32), pltpu.VMEM((tq, W), F32), pltpu.VMEM((tq, W), F32)]),
        out_shape=jax.ShapeDtypeStruct((T, HEADS * HP), F32),
        compiler_params=pltpu.CompilerParams(dimension_semantics=("parallel", "arbitrary")),
    )(qi, kj, q, k, kv)


def _flash_bwd(q, k, kv, o, dycat, T):
    tq = tk = min(512, T)
    nq = T // tq
    G = FLASH_HEADS

    def body(qi_ref, kj_ref, q_ref, k_ref, v_ref, o_ref, do_ref, dq_ref, dk_ref, dv_ref):
        t = pl.program_id(1)
        i, j = qi_ref[t], kj_ref[t]

        @pl.when(t == 0)
        def _():
            dq_ref[...] = jnp.zeros_like(dq_ref)

        @pl.when(i == j)
        def _():
            dk_ref[...] = jnp.zeros_like(dk_ref)
            dv_ref[...] = jnp.zeros_like(dv_ref)

        def step(diagonal):
            r0 = pl.multiple_of(i * tq, tq)
            for g in range(G):
                sl = slice(g * HP, (g + 1) * HP)
                qv, kv, vv, ov, dov = q_ref[:, sl], k_ref[:, sl], v_ref[:, sl], o_ref[:, sl], do_ref[:, sl]
                s = _dotf(qv, kv, NT) * QK_SCALE
                p = jnp.exp(s - ov[:, VDIM:VDIM + 1])
                if diagonal:
                    rows = lax.broadcasted_iota(jnp.int32, (tq, tk), 0)
                    cols = lax.broadcasted_iota(jnp.int32, (tq, tk), 1)
                    p = jnp.where(rows >= cols, p, 0.0)
                dsum = jnp.sum(dov * ov, axis=1, keepdims=True)
                dv_ref[:, sl] += _dotf(p, dov, TN)
                dp = _dotf(dov, vv, NT)
                ds = p * (dp - dsum) * QK_SCALE
                dk_ref[:, sl] += _dotf(ds, qv, TN)
                dq_ref[pl.ds(r0, tq), sl] += _dotf(ds, kv, NN)

        @pl.when(i > j)
        def _():
            step(False)

        @pl.when(i == j)
        def _():
            step(True)

    W = G * HP
    qi, kj = _causal_pairs(nq, by_query=False)
    qmap = lambda h, t, qi, kj: (qi[t], h)
    kmap = lambda h, t, qi, kj: (kj[t], h)
    vmap = lambda h, t, qi, kj: (kj[t], HEADS // G + h)
    return pl.pallas_call(
        body, name="mla_flash_bwd",
        grid_spec=pltpu.PrefetchScalarGridSpec(
            num_scalar_prefetch=2, grid=(HEADS // G, qi.shape[0]),
            in_specs=[pl.BlockSpec((tq, W), qmap), pl.BlockSpec((tk, W), kmap), pl.BlockSpec((tk, W), vmap),
                      pl.BlockSpec((tq, W), qmap), pl.BlockSpec((tq, W), qmap)],
            out_specs=[pl.BlockSpec((T, W), lambda h, t, qi, kj: (0, h)), pl.BlockSpec((tk, W), kmap), pl.BlockSpec((tk, W), kmap)]),
        out_shape=[jax.ShapeDtypeStruct((T, HEADS * HP), F32)] * 3,
        compiler_params=pltpu.CompilerParams(dimension_semantics=("parallel", "arbitrary")),
    )(qi, kj, q, k, kv, o, dycat)


_IN_SRC = (0, 256, 384, 416, 672, 928, 1184, 1440, 2208, 2212)
_IN_DST = (Z_CQ, Z_CKV, Z_KR + KR_LANE, Z_SCB, Z_SCC, Z_SCH, Z_SSZ, Z_XBC, Z_DT)


def _pad_rows_in(w):
    ax = w.ndim - 2

    def zeros(n):
        return jnp.zeros(w.shape[:ax] + (n,) + w.shape[ax + 1:], w.dtype)

    def whole_tiles(p):
        n = p.shape[ax]
        return p if n % SLAB_ALIGN == 0 else jnp.pad(p, [(0, 0)] * ax + [(0, -n % SLAB_ALIGN), (0, 0)])

    parts, at = [], 0
    for s0, s1, d0 in zip(_IN_SRC[:-1], _IN_SRC[1:], _IN_DST):
        if d0 > at:
            parts.append(zeros(d0 - at))
        parts.append(whole_tiles(lax.slice_in_dim(w, s0, s1, axis=ax)))
        at = d0 + parts[-1].shape[ax]
    parts.append(zeros(ZIN - at))
    return jnp.concatenate(parts, axis=ax)


def _unpad_rows_in(w):
    ax = w.ndim - 2
    groups = list(zip(_IN_SRC[:-1], _IN_SRC[1:], _IN_DST))
    parts = [lax.slice_in_dim(w, d0, d0 + -(-(s1 - s0) // SLAB_ALIGN) * SLAB_ALIGN, axis=ax) for s0, s1, d0 in groups]
    return lax.slice_in_dim(jnp.concatenate(parts, axis=ax), 0, _IN_SRC[-1], axis=ax)


def _pad_heads(w, width):
    w = w.reshape(w.shape[:-1] + (HEADS, width))
    w = jnp.pad(w, [(0, 0)] * (w.ndim - 1) + [(0, HP - width)])
    return w.reshape(w.shape[:-2] + (HEADS * HP,))


def _unpad_heads(w, width):
    w = w.reshape(w.shape[:-1] + (HEADS, HP))[..., :width]
    return w.reshape(w.shape[:-2] + (HEADS * width,))


def _pad_kv(w):
    w = w.reshape(w.shape[:-1] + (HEADS, NOPE + VDIM))
    return jnp.concatenate([_pad_heads(w[..., :NOPE].reshape(w.shape[:-2] + (HEADS * NOPE,)), NOPE),
                            _pad_heads(w[..., NOPE:].reshape(w.shape[:-2] + (HEADS * VDIM,)), VDIM)], axis=-1)


def _unpad_kv(w):
    k = _unpad_heads(w[..., :HEADS * HP], NOPE).reshape(w.shape[:-1] + (HEADS, NOPE))
    v = _unpad_heads(w[..., HEADS * HP:], VDIM).reshape(w.shape[:-1] + (HEADS, VDIM))
    return jnp.concatenate([k, v], axis=-1).reshape(w.shape[:-1] + (HEADS * (NOPE + VDIM),))


def _pad_out_rows(w):
    lead, d = w.shape[:-2], w.shape[-1]
    att = w[..., :HEADS * VDIM, :].reshape(lead + (HEADS, VDIM, d))
    att = jnp.pad(att, [(0, 0)] * (att.ndim - 2) + [(0, HP - VDIM), (0, 0)]).reshape(lead + (HEADS * HP, d))
    return jnp.concatenate([att, w[..., HEADS * VDIM:, :]], axis=-2)


def _unpad_out_rows(w):
    lead, d = w.shape[:-2], w.shape[-1]
    att = w[..., :HEADS * HP, :].reshape(lead + (HEADS, HP, d))[..., :VDIM, :].reshape(lead + (HEADS * VDIM, d))
    return jnp.concatenate([att, w[..., HEADS * HP:, :]], axis=-2)


def _rows8(w):
    return jnp.pad(w.astype(F32), [(0, 0)] * (w.ndim - 2) + [(0, 8 - w.shape[-2]), (0, 0)])


def _row8(*vecs):
    c = vecs[0].shape[-1]
    return jnp.concatenate([v.reshape(1, c).astype(F32) for v in vecs] + [jnp.zeros((8 - len(vecs), c), F32)], axis=0)


def _lanes(v):
    return jnp.pad(v.astype(F32), (0, LANE - v.shape[0])).reshape(1, LANE)


def _rope_tables(positions):
    inv_freq = 1.0 / (ROPE_THETA ** (jnp.arange(0, ROPE, 2, dtype=F32) / ROPE))
    ang = positions.astype(F32)[:, None] * inv_freq
    cos, sin = jnp.cos(ang), jnp.sin(ang)
    T = positions.shape[0]
    half = ROPE // 2
    one = jnp.ones((T, KR_LANE), F32)
    zero = jnp.zeros((T, KR_LANE), F32)
    tail1 = jnp.ones((T, HP - KR_LANE - ROPE), F32)
    tail0 = jnp.zeros((T, HP - KR_LANE - ROPE), F32)
    z16 = jnp.zeros((T, half), F32)
    cosf = jnp.concatenate([one, cos, cos, tail1], axis=1)
    sina = jnp.concatenate([zero, -sin, z16, tail0], axis=1)
    sinb = jnp.concatenate([zero, z16, sin, tail0], axis=1)
    return cosf, sina, sinb


def _kernel_weights(W):
    c = lambda a: a.astype(MXU_DTYPE)
    return dict(
        w_in=c(_pad_rows_in(W["w_in"])),
        w_q=c(_pad_heads(W["mla_w_q_up"], NOPE + ROPE)),
        w_kv=c(_pad_kv(W["mla_w_kv_up"])),
        w_out=c(_pad_out_rows(W["w_out"])),
        w_up=c(W["ffn_w_up"]),
        w_down=c(W["ffn_w_down"]),
        sc_w=_rows8(W["sc_conv_w"]),
        ssd_w=_rows8(W["ssd_conv_w"]),
        ffn_w=_rows8(W["ffn_conv_w"]),
    )


def _layer_weights(KW, l):
    return {k: (v[l] if k in ("sc_w", "ssd_w", "ffn_w") else (v, l)) for k, v in KW.items()}


def _local_step(x, positions, target, W, S):
    T = x.shape[0]
    tm = min(256, T)
    tm_ffn = min(FFN_ROWS, T)
    cosf, sina, sinb = _rope_tables(positions)
    KW = _kernel_weights(W)
    saved = []
    xl = x
    for l in range(DEPTH):
        lw = _layer_weights(KW, l)
        g_pre = S["norm_mix_pre"][l].reshape(1, -1)
        g_post = S["norm_mix_post"][l].reshape(1, -1)
        g_fpre = S["norm_ffn_pre"][l].reshape(1, -1)
        g_fpost = S["norm_ffn_post"][l].reshape(1, -1)
        qn = S["mla_q_norm"][l].reshape(1, -1)
        kvn = S["mla_kv_norm"][l].reshape(1, -1)
        ssd_b = S["ssd_conv_b"][l].reshape(1, -1)
        ssd_par = _row8(jnp.pad(S["ssd_dt_bias"][l], (0, LANE - SSD_HEADS)), jnp.pad(S["ssd_a_log"][l], (0, LANE - SSD_HEADS)),
                        jnp.pad(S["ssd_d"][l], (0, LANE - SSD_HEADS)))
        ssd_nw = S["ssd_norm"][l].reshape(1, -1)
        ffn_b = S["ffn_conv_b"][l].reshape(1, -1)

        (h1,) = _rows(lambda i, n, *v: _f_premix(*v), T, tm, [_cur(xl)], [_cst(g_pre)], [_out(D_MODEL, BF16)], [], "pre_mix_norm")
        zin = _mm(h1, lw["w_in"], "nt", F32, "mm_in")
        qlat, kvlat = _rows(lambda i, n, *v: _f_mla_pre(*v), T, tm, [_cur(zin, Q_LORA, 0), _cur(zin, KV_LORA, Z_CKV // KV_LORA)],
                            [_cst(qn), _cst(kvn)], [_out(Q_LORA, BF16), _out(KV_LORA, BF16)], [], "mla_pre_norm")
        qpad = _mm(qlat, lw["w_q"], "nn", F32, "mm_q_up")
        kvpad = _mm(kvlat, lw["w_kv"], "nn", BF16, "mm_kv_up")
        qr, kr = _rows(_k_rope_fwd, T, tm, [_cur(qpad), _cur(kvpad, HEADS * HP, 0), _cur(zin, LANE, Z_KR // LANE),
                                            _cur(cosf), _cur(sina), _cur(sinb)], [],
                       [_out(HEADS * HP, BF16), _out(HEADS * HP, BF16)], [], "mla_rope")
        o = _flash_fwd(qr, kr, kvpad, T)
        (yconv,) = _rows(_k_sconv_fwd, T, tm, [_cur(zin, SC_DIM, Z_SCB // SC_DIM), _cur(zin, SC_DIM, Z_SCC // SC_DIM),
                                               _cur(zin, SC_DIM, Z_SCH // SC_DIM), _halo(zin, "prev", SC_DIM, Z_SCC // SC_DIM),
                                               _halo(zin, "prev", SC_DIM, Z_SCH // SC_DIM)], [_cst(lw["sc_w"])],
                         [_out(SC_DIM, F32)], [], "short_conv_fwd")
        (xbc,) = _rows(_k_ssdconv_fwd, T, tm, [_cur(zin, SSD_CONV_DIM, Z_XBC // SSD_CONV_DIM),
                                               _halo(zin, "prev", SSD_CONV_DIM, Z_XBC // SSD_CONV_DIM)],
                       [_cst(lw["ssd_w"]), _cst(ssd_b)], [_out(SSD_CONV_DIM, F32)], [], "ssd_conv_fwd")
        dtraw = zin[:, Z_DT:Z_DT + LANE]
        yscan, states = _ssd_fwd(xbc, dtraw, ssd_par, T)
        (yssd,) = _rows(lambda i, n, *v: _f_ssd_gate(*v), T, tm, [_cur(yscan), _cur(zin, SSD_DIM, Z_SSZ // SSD_DIM)], [_cst(ssd_nw)],
                        [_out(SSD_DIM, F32)], [], "ssd_gate_fwd")
        ycat = jnp.concatenate([o.astype(BF16), yconv.astype(BF16), yssd.astype(BF16)], axis=1)
        mixed = _mm(ycat, lw["w_out"], "nn", F32, "mm_out")
        x1, h2 = _rows(lambda i, n, *v: _f_post_mix(*v), T, tm, [_cur(xl), _cur(mixed)], [_cst(g_post), _cst(g_fpre)],
                       [_out(D_MODEL, F32), _out(D_MODEL, BF16)], [], "post_mix_fwd")
        upre = _mm(h2, lw["w_up"], "nn", F32, "mm_up")
        nt = FFN_DIM // FFN_TILE
        gcol, ucol = (lambda j: j), (lambda j: j + nt)
        (act,) = _rows(_k_ffnact_fwd, T, tm_ffn,
                       [(upre, FFN_TILE, gcol, "cur"), (upre, FFN_TILE, ucol, "cur"), (upre, FFN_TILE, gcol, "prev"),
                        (upre, FFN_TILE, ucol, "prev")],
                       [(lw["ffn_w"], FFN_TILE, gcol), (lw["ffn_w"], FFN_TILE, ucol), (ffn_b, FFN_TILE, gcol), (ffn_b, FFN_TILE, ucol)],
                       [(FFN_DIM, BF16, FFN_TILE, gcol)], [], "ffn_act_fwd", ncol=nt)
        dn = _mm(act, lw["w_down"], "nn", F32, "mm_down")
        (x2,) = _rows(lambda i, n, *v: _f_post_ffn(*v), T, tm, [_cur(x1), _cur(dn)], [_cst(g_fpost)], [_out(D_MODEL, F32)], [], "post_ffn_fwd")
        saved.append(dict(lw=lw, x=xl, h1=h1, zin=zin, qlat=qlat, kvlat=kvlat, qr=qr, kr=kr, kvpad=kvpad, o=o, xbc=xbc, dtraw=dtraw,
                          yscan=yscan, states=states, ycat=ycat, mixed=mixed, x1=x1, h2=h2, upre=upre, act=act, dn=dn,
                          g_pre=g_pre, g_post=g_post, g_fpre=g_fpre, g_fpost=g_fpost, qn=qn, kvn=kvn, ssd_b=ssd_b,
                          ssd_par=ssd_par, ssd_nw=ssd_nw, ffn_b=ffn_b))
        xl = x2

    gx, loss_part = _rows(_k_loss, T, tm, [_cur(xl), _cur(target)], [], [_out(D_MODEL, F32)], [_acc(1, LANE)], "loss_head")

    GW = {k: [None] * DEPTH for k in ("w_in", "mla_w_q_up", "mla_w_kv_up", "sc_conv_w", "ssd_conv_w", "w_out", "ffn_w_up",
                                      "ffn_conv_w", "ffn_w_down")}
    GS = {k: [None] * DEPTH for k in ("norm_mix_pre", "norm_mix_post", "norm_ffn_pre", "norm_ffn_post", "mla_q_norm", "mla_kv_norm",
                                      "ssd_conv_b", "ssd_dt_bias", "ssd_a_log", "ssd_d", "ssd_norm", "ffn_conv_b")}
    nt = FFN_DIM // FFN_TILE
    gcol, ucol = (lambda j: j), (lambda j: j + nt)
    for l in reversed(range(DEPTH)):
        s = saved[l]
        lw = s["lw"]
        gx1, ddn, dgf = _rows_vjp(_f_post_ffn, T, tm, [s["x1"], s["dn"]], [s["g_fpost"]], [gx], [F32, BF16], "post_ffn_bwd")
        GS["norm_ffn_post"][l] = dgf[0]
        dact = _mm(ddn, lw["w_down"], "nt", F32, "mm_down_dx")
        GW["ffn_w_down"][l] = _mm(s["act"], ddn, "tn", BF16, "mm_down_dw")
        up = s["upre"]
        dug, duu, dwg, dwu, dbg, dbu = _rows(
            _k_ffnact_bwd, T, tm_ffn,
            [(up, FFN_TILE, gcol, "cur"), (up, FFN_TILE, ucol, "cur"), (dact, FFN_TILE, gcol, "cur"), (up, FFN_TILE, gcol, "prev"),
             (up, FFN_TILE, ucol, "prev"), (up, FFN_TILE, gcol, "next"), (up, FFN_TILE, ucol, "next"), (dact, FFN_TILE, gcol, "next")],
            [(lw["ffn_w"], FFN_TILE, gcol), (lw["ffn_w"], FFN_TILE, ucol), (s["ffn_b"], FFN_TILE, gcol), (s["ffn_b"], FFN_TILE, ucol)],
            [(FFN_DIM, BF16, FFN_TILE, gcol)] * 2,
            [(HALO, FFN_DIM, FFN_TILE, gcol)] * 2 + [(1, FFN_DIM, FFN_TILE, gcol)] * 2, "ffn_act_bwd", ncol=nt)
        GW["ffn_conv_w"][l] = jnp.concatenate([dwg[:3], dwu[:3]], axis=1)
        GS["ffn_conv_b"][l] = jnp.concatenate([dbg[0], dbu[0]])
        dh2 = _mm((dug, duu), lw["w_up"], "nt", F32, "mm_up_dx")
        GW["ffn_w_up"][l] = (_mm(s["h2"], dug, "tn", BF16, "mm_up_dw_gate"), _mm(s["h2"], duu, "tn", BF16, "mm_up_dw_up"))
        gx0, dmixed, dgp, dgf = _rows_vjp(_f_post_mix, T, tm, [s["x"], s["mixed"]], [s["g_post"], s["g_fpre"]], [gx1, dh2],
                                          [F32, BF16], "post_mix_bwd")
        GS["norm_mix_post"][l], GS["norm_ffn_pre"][l] = dgp[0], dgf[0]
        dycat = _mm(dmixed, lw["w_out"], "nt", F32, "mm_out_dx")
        GW["w_out"][l] = _unpad_out_rows(_mm(s["ycat"], dmixed, "tn", BF16, "mm_out_dw"))
        zin = s["zin"]
        dyscan, dz, dnw = _rows(_vjp_wrap(_f_ssd_gate, 2, 1), T, tm,
                                [_cur(s["yscan"]), _cur(zin, SSD_DIM, Z_SSZ // SSD_DIM), _cur(dycat, SSD_DIM, (HEADS * HP + SC_DIM) // SSD_DIM)],
                                [_cst(s["ssd_nw"])], [_out(SSD_DIM, F32), _out(SSD_DIM, BF16)], [_acc(1, SSD_DIM)], "ssd_gate_bwd")
        GS["ssd_norm"][l] = dnw[0]
        dxbc, ddtraw, dpar = _ssd_bwd(s["xbc"], s["dtraw"], s["ssd_par"], s["states"], dyscan, T)
        GS["ssd_dt_bias"][l], GS["ssd_a_log"][l], GS["ssd_d"][l] = dpar[0, :SSD_HEADS], dpar[1, :SSD_HEADS], dpar[2, :SSD_HEADS]
        xb = Z_XBC // SSD_CONV_DIM
        dxraw, dsw, dsb = _rows(_k_ssdconv_bwd, T, tm,
                                [_cur(zin, SSD_CONV_DIM, xb), _cur(dxbc), _halo(zin, "prev", SSD_CONV_DIM, xb),
                                 _halo(zin, "next", SSD_CONV_DIM, xb), _halo(dxbc, "next")],
                                [_cst(lw["ssd_w"]), _cst(s["ssd_b"])], [_out(SSD_CONV_DIM, BF16)],
                                [_acc(HALO, SSD_CONV_DIM), _acc(1, SSD_CONV_DIM)], "ssd_conv_bwd")
        GW["ssd_conv_w"][l] = dsw[:4]
        GS["ssd_conv_b"][l] = dsb[0]
        cb = (HEADS * HP) // SC_DIM
        dscb, dscc, dsch, dscw = _rows(_k_sconv_bwd, T, tm,
                                       [_cur(zin, SC_DIM, Z_SCB // SC_DIM), _cur(zin, SC_DIM, Z_SCC // SC_DIM),
                                        _cur(zin, SC_DIM, Z_SCH // SC_DIM), _cur(dycat, SC_DIM, cb),
                                        _halo(zin, "prev", SC_DIM, Z_SCC // SC_DIM), _halo(zin, "prev", SC_DIM, Z_SCH // SC_DIM),
                                        _halo(zin, "next", SC_DIM, Z_SCB // SC_DIM), _halo(dycat, "next", SC_DIM, cb)],
                                       [_cst(lw["sc_w"])], [_out(SC_DIM, BF16)] * 3, [_acc(HALO, SC_DIM)], "short_conv_bwd")
        GW["sc_conv_w"][l] = dscw[:3]
        dq, dk, dv = _flash_bwd(s["qr"], s["kr"], s["kvpad"], s["o"], dycat, T)
        dqpad, dkvpad, dkr = _rows(_k_rope_bwd, T, tm, [_cur(dq), _cur(dk), _cur(dv), _cur(cosf), _cur(sina), _cur(sinb)], [],
                                   [_out(HEADS * HP, BF16), _out(2 * HEADS * HP, BF16), _out(LANE, BF16)], [], "mla_rope_bwd")
        dqlat = _mm(dqpad, lw["w_q"], "nt", F32, "mm_q_dx")
        GW["mla_w_q_up"][l] = _unpad_heads(_mm(s["qlat"], dqpad, "tn", BF16, "mm_q_dw"), NOPE + ROPE)
        dkvlat = _mm(dkvpad, lw["w_kv"], "nt", F32, "mm_kv_dx")
        GW["mla_w_kv_up"][l] = _unpad_kv(_mm(s["kvlat"], dkvpad, "tn", BF16, "mm_kv_dw"))
        dcq, dckv, dqn, dkvn = _rows(_vjp_wrap(_f_mla_pre, 2, 2), T, tm,
                                     [_cur(zin, Q_LORA, 0), _cur(zin, KV_LORA, Z_CKV // KV_LORA), _cur(dqlat), _cur(dkvlat)],
                                     [_cst(s["qn"]), _cst(s["kvn"])], [_out(Q_LORA, BF16), _out(KV_LORA, BF16)],
                                     [_acc(1, Q_LORA), _acc(1, KV_LORA)], "mla_pre_bwd")
        GS["mla_q_norm"][l], GS["mla_kv_norm"][l] = dqn[0], dkvn[0]
        dzin = jnp.concatenate([dcq, dckv, dkr, dscb, dscc, dsch, dz, dxraw, ddtraw.astype(BF16), jnp.zeros((T, ZIN - Z_DT - LANE), BF16)], axis=1)
        dh1 = _mm(dzin, lw["w_in"], "nn", F32, "mm_in_dx")
        GW["w_in"][l] = _unpad_rows_in(_mm(dzin, s["h1"], "tn", BF16, "mm_in_dw"))
        gx, dgp = _rows(_vjp_wrap(_f_premix, 1, 1, add_first=True), T, tm, [_cur(s["x"]), _cur(dh1), _cur(gx0)], [_cst(s["g_pre"])],
                        [_out(D_MODEL, F32)], [_acc(1, D_MODEL)], "pre_mix_bwd")
        GS["norm_mix_pre"][l] = dgp[0]
    GS = {k: jnp.stack(v) for k, v in GS.items()}
    return loss_part[0, 0], gx, GW, GS


WEIGHTS = ("norm_mix_pre", "norm_mix_post", "norm_ffn_pre", "norm_ffn_post", "w_in", "mla_q_norm", "mla_w_q_up", "mla_kv_norm",
           "mla_w_kv_up", "sc_conv_w", "ssd_conv_w", "ssd_conv_b", "ssd_dt_bias", "ssd_a_log", "ssd_d", "ssd_norm", "w_out",
           "ffn_w_up", "ffn_conv_w", "ffn_conv_b", "ffn_w_down")
SHARDED = (("w_in", 2), ("mla_w_q_up", 2), ("mla_w_kv_up", 2), ("sc_conv_w", 2), ("ssd_conv_w", 2), ("w_out", 1),
           ("ffn_w_up", 2), ("ffn_conv_w", 2), ("ffn_w_down", 1))
SMALL = tuple(n for n in WEIGHTS if n not in dict(SHARDED))
N_CHIPS = 4
N_DEV = 8
ROW_ALIGN = 256
SLAB_ALIGN = 16
MAIN = ("ffn_w_down", "w_out", "w_in", "mla_w_q_up", "mla_w_kv_up", "sc_conv_w", "ssd_conv_w")
WIDE = ("ffn_w_up", "ffn_conv_w")
TRANSPOSED = ("w_in",)


def _is_rows(shape, width):
    return shape[-1] == width and math.prod(shape[:-1]) % SLAB_ALIGN == 0


def _is_short(shape, width):
    return len(shape) == 2 and shape[1] == width and not _is_rows(shape, width)


def _slab_rows(shape, width):
    if _is_rows(shape, width):
        return math.prod(shape[:-1])
    if _is_short(shape, width):
        return -(-shape[0] // SLAB_ALIGN) * SLAB_ALIGN
    return -(-math.prod(shape) // (width * SLAB_ALIGN)) * SLAB_ALIGN


def _slab(piece, width, dtype, lead=0):
    ld, shape = piece.shape[:lead], piece.shape[lead:]
    rows = _slab_rows(shape, width)
    if _is_rows(shape, width):
        return piece.astype(dtype).reshape(ld + (rows, width))
    if _is_short(shape, width):
        return jnp.pad(piece.astype(dtype), [(0, 0)] * lead + [(0, rows - shape[0]), (0, 0)])
    flat = piece.astype(dtype).reshape(ld + (-1,))
    return jnp.pad(flat, [(0, 0)] * lead + [(0, rows * width - flat.shape[-1])]).reshape(ld + (rows, width))


def _unslab(slab, shape, lead=0):
    ld = slab.shape[:lead]
    if _is_rows(shape, slab.shape[-1]):
        return slab.reshape(ld + tuple(shape))
    if _is_short(shape, slab.shape[-1]):
        return slab[..., :shape[0], :]
    return slab.reshape(ld + (-1,))[..., :math.prod(shape)].reshape(ld + tuple(shape))


def _layout(shapes, per_layer):
    out = {}
    for buf, names in (("main", MAIN), ("wide", WIDE)):
        width = PACK_COLS if buf == "main" else shapes["ffn_w_up"][-1]
        ents, off = [], 0
        for n in names:
            shp = tuple(shapes[n])
            if n.endswith("conv_w"):
                todo = [(None, False, shp), (None, True, shp)]
            elif per_layer or _is_short(shp[1:], width):
                todo = [(l, False, shp[1:]) for l in range(shp[0])]
            else:
                todo = [(None, False, shp)]
            for l, lo, ps in todo:
                r = _slab_rows(ps, width)
                ents.append((n, l, lo, ps, off, r))
                off += r
        out[buf] = (width, -(-off // ROW_ALIGN) * ROW_ALIGN, ents)
    return out


def _pack(layout, piece, dtype, lead=0):
    width, rows, ents = layout
    slabs, ld = [], None
    for n, l, lo, ps, off, r in ents:
        p = piece(n, l, lo)
        slabs.append(None if p is None else _slab(p, width, dtype, lead))
        ld = ld if p is None else p.shape[:lead]
    used = ents[-1][4] + ents[-1][5]
    slabs = [jnp.zeros(ld + (e[5], width), dtype) if s is None else s for s, e in zip(slabs, ents)]
    if rows > used:
        slabs.append(jnp.zeros(ld + (rows - used, width), dtype))
    return jnp.concatenate(slabs, axis=lead)


ANY = pl.BlockSpec(memory_space=pl.ANY)


def _pos():
    return lax.axis_index("x"), lax.axis_index("y"), lax.axis_index("c")


def _other_chips(x, y):
    return ((1 - x, y), (x, 1 - y), (1 - x, 1 - y))


def _remote(src, dst, ssem, rsem, dev):
    return pltpu.make_async_remote_copy(src_ref=src, dst_ref=dst, send_sem=ssem, recv_sem=rsem, device_id=dev, device_id_type=MESH)


AG_CHUNKS = 2


def _all_gather_weights(wpk):
    R, C = wpk.shape
    H = R // 2
    CH = H // AG_CHUNKS
    n = 3 * AG_CHUNKS

    def body(w_ref, out_ref, isend, irecv, dsend, drecv):
        x, y, c = _pos()
        k = 2 * x + y
        sib = (x, y, 1 - c)
        chips = _other_chips(x, y)

        def rows(kk, half, ch):
            return out_ref.at[kk, pl.ds(half * H + ch * CH, CH), :]

        first = []
        for p, (cx, cy) in enumerate(chips):
            for ch in range(AG_CHUNKS):
                s = p * AG_CHUNKS + ch
                cp = _remote(w_ref.at[pl.ds(c * H + ch * CH, CH), :], rows(k, c, ch), isend.at[s], irecv.at[s], (cx, cy, c))
                cp.start()
                first.append(cp)
        passed = []
        for p, (cx, cy) in enumerate(chips):
            for ch in range(AG_CHUNKS):
                s = p * AG_CHUNKS + ch
                land = rows(2 * cx + cy, c, ch)
                _remote(land, land, isend.at[s], irecv.at[s], (cx, cy, c)).wait_recv()
                fw = _remote(land, land, dsend.at[s], drecv.at[s], sib)
                fw.start()
                passed.append(fw)
        for p, (cx, cy) in enumerate(chips):
            for ch in range(AG_CHUNKS):
                s = p * AG_CHUNKS + ch
                land = rows(2 * cx + cy, 1 - c, ch)
                _remote(land, land, dsend.at[s], drecv.at[s], sib).wait_recv()
        for cp in first + passed:
            cp.wait_send()

    got = pl.pallas_call(
        body, name="all_gather_weights", in_specs=[ANY], out_specs=ANY,
        out_shape=jax.ShapeDtypeStruct((N_CHIPS, R, C), wpk.dtype),
        scratch_shapes=[pltpu.SemaphoreType.DMA((n,))] * 4,
    )(wpk)
    return lax.dynamic_update_slice(got, wpk[None], (2 * lax.axis_index("x") + lax.axis_index("y"), 0, 0))


def _rs_pair_exchange(g):
    _, R, C = g.shape
    H = R // 2

    def body(g_ref, got_ref, ssem, rsem):
        x, y, c = _pos()
        sib = (x, y, 1 - c)
        cps = []
        for kk in range(N_CHIPS):
            cp = _remote(g_ref.at[kk, pl.ds((1 - c) * H, H), :], got_ref.at[kk], ssem.at[kk], rsem.at[kk], sib)
            cp.start()
            cps.append(cp)
        for cp in cps:
            cp.wait()

    return pl.pallas_call(
        body, name="rs_pair_exchange", in_specs=[ANY], out_specs=ANY,
        out_shape=jax.ShapeDtypeStruct((N_CHIPS, H, C), g.dtype),
        scratch_shapes=[pltpu.SemaphoreType.DMA((N_CHIPS,))] * 2,
    )(g)


def _rs_chip_exchange(p):
    _, H, C = p.shape

    def body(p_ref, out_ref, ssem, rsem):
        x, y, c = _pos()
        k = 2 * x + y
        chips = _other_chips(x, y)
        cps = []
        for s, (cx, cy) in enumerate(chips):
            cp = _remote(p_ref.at[2 * cx + cy], out_ref.at[k], ssem.at[s], rsem.at[s], (cx, cy, c))
            cp.start()
            cps.append(cp)
        for s, (cx, cy) in enumerate(chips):
            land = out_ref.at[2 * cx + cy]
            _remote(land, land, ssem.at[s], rsem.at[s], (cx, cy, c)).wait_recv()
        for cp in cps:
            cp.wait_send()

    k = 2 * lax.axis_index("x") + lax.axis_index("y")
    got = pl.pallas_call(
        body, name="rs_chip_exchange", in_specs=[ANY], out_specs=ANY,
        out_shape=jax.ShapeDtypeStruct(p.shape, p.dtype),
        scratch_shapes=[pltpu.SemaphoreType.DMA((3,)), pltpu.SemaphoreType.DMA((3,))],
    )(p)
    return lax.dynamic_update_slice(got, lax.dynamic_slice_in_dim(p, k, 1, axis=0), (k, 0, 0))


def _rs_pair_share(f):
    H, C = f.shape

    def body(f_ref, out_ref, ssem, rsem):
        x, y, c = _pos()
        cp = _remote(f_ref, out_ref.at[c], ssem, rsem, (x, y, 1 - c))
        cp.start()
        land = out_ref.at[1 - c]
        _remote(land, land, ssem, rsem, (x, y, 1 - c)).wait_recv()
        cp.wait_send()

    got = pl.pallas_call(
        body, name="rs_pair_share", in_specs=[ANY], out_specs=ANY,
        out_shape=jax.ShapeDtypeStruct((2, H, C), f.dtype),
        scratch_shapes=[pltpu.SemaphoreType.DMA, pltpu.SemaphoreType.DMA],
    )(f)
    return lax.dynamic_update_slice(got, f[None], (lax.axis_index("c"), 0, 0))


def _all_reduce_small(s):
    r, C = s.shape

    def body(s_ref, o_ref, buf, ssem, rsem):
        x, y, c = _pos()
        me = 4 * x + 2 * y + c
        buf[me] = s_ref[...]
        cps = []
        for m in range(1, N_DEV):
            mx, my, mc = (m >> 2) & 1, (m >> 1) & 1, m & 1
            peer = (x ^ mx, y ^ my, c ^ mc)
            cp = _remote(s_ref, buf.at[me], ssem.at[m - 1], rsem.at[m - 1], peer)
            cp.start()
            cps.append(cp)
        for m in range(1, N_DEV):
            mx, my, mc = (m >> 2) & 1, (m >> 1) & 1, m & 1
            src = 4 * (x ^ mx) + 2 * (y ^ my) + (c ^ mc)
            _remote(s_ref, buf.at[src], ssem.at[m - 1], rsem.at[m - 1], (x ^ mx, y ^ my, c ^ mc)).wait_recv()
        for cp in cps:
            cp.wait_send()
        acc = buf[0]
        for j in range(1, N_DEV):
            acc = acc + buf[j]
        o_ref[...] = acc

    return pl.pallas_call(
        body, name="all_reduce_small", in_specs=[pl.BlockSpec(memory_space=pltpu.VMEM)],
        out_specs=pl.BlockSpec(memory_space=pltpu.VMEM), out_shape=jax.ShapeDtypeStruct((r, C), F32),
        scratch_shapes=[pltpu.VMEM((N_DEV, r, C), F32), pltpu.SemaphoreType.DMA((N_DEV - 1,)), pltpu.SemaphoreType.DMA((N_DEV - 1,))],
    )(s)


def _rtile(n, pref):
    if n <= pref:
        return n
    t = (pref // 16) * 16
    while t >= 16:
        if n % t == 0:
            return t
        t -= 16
    raise ValueError(f"no row tile for {n}")


def _reduce_scatter_grads(gpk):
    _, R, C = gpk.shape
    H = R // 2
    got = _rs_pair_exchange(gpk)
    own = lax.dynamic_index_in_dim(gpk.reshape(N_CHIPS, 2, H, C), lax.axis_index("c"), axis=1, keepdims=False)
    tm = _rtile(N_CHIPS * H, 512)
    (part,) = _rows(lambda i, n, a, b: (a.astype(F32) + b.astype(F32),), N_CHIPS * H, tm,
                    [_cur(own.reshape(N_CHIPS * H, C)), _cur(got.reshape(N_CHIPS * H, C))], [], [_out(C, BF16)], [], "rs_pair_add")
    parts = _rs_chip_exchange(part.reshape(N_CHIPS, H, C)).reshape(N_CHIPS * H, C)
    tm = _rtile(H, 1024)
    hb = H // tm

    def add4(i, n, a, b, c, d):
        return (((a.astype(F32) + b.astype(F32)) + c.astype(F32)) + d.astype(F32),)

    (red,) = _rows(add4, H, tm, [(parts, C, functools.partial(_const, v=0), j * hb) for j in range(N_CHIPS)], [], [_out(C, F32)], [],
                   "rs_chip_add")
    return _rs_pair_share(red).reshape(R, C)


def _adam(w, g, m, v, name, g_row=0):
    shp = w.shape
    two = lambda a: a.reshape(-1, shp[-1])
    rows = math.prod(shp[:-1])
    tm = _rtile(rows, 256)
    assert g_row % tm == 0
    g_in = (two(g), shp[-1], functools.partial(_const, v=0), g_row // tm)
    res = _rows(_k_adam, rows, tm, [_cur(two(w)), g_in, _cur(two(m)), _cur(two(v))], [], [_out(shp[-1], F32)] * 4, [], name)
    return tuple(r.reshape(shp) for r in res)


def _pack_flat(parts, rows):
    flat = jnp.concatenate([p.astype(F32).reshape(-1) for p in parts])
    return jnp.pad(flat, (0, rows * PACK_COLS - flat.shape[0])).reshape(rows, PACK_COLS)


def _unpack_flat(buf, shapes):
    flat, out, off = buf.reshape(-1), [], 0
    for shp in shapes:
        n = math.prod(shp)
        out.append(flat[off:off + n].reshape(shp))
        off += n
    return out


def kernel(x, positions, norm_mix_pre, norm_mix_post, norm_ffn_pre, norm_ffn_post, w_in, mla_q_norm, mla_w_q_up, mla_kv_norm, mla_w_kv_up, sc_conv_w, ssd_conv_w, ssd_conv_b, ssd_dt_bias, ssd_a_log, ssd_d, ssd_norm, w_out, ffn_w_up, ffn_conv_w, ffn_conv_b, ffn_w_down, loss_target, m_norm_mix_pre, m_norm_mix_post, m_norm_ffn_pre, m_norm_ffn_post, m_w_in, m_mla_q_norm, m_mla_w_q_up, m_mla_kv_norm, m_mla_w_kv_up, m_sc_conv_w, m_ssd_conv_w, m_ssd_conv_b, m_ssd_dt_bias, m_ssd_a_log, m_ssd_d, m_ssd_norm, m_w_out, m_ffn_w_up, m_ffn_conv_w, m_ffn_conv_b, m_ffn_w_down, v_norm_mix_pre, v_norm_mix_post, v_norm_ffn_pre, v_norm_ffn_post, v_w_in, v_mla_q_norm, v_mla_w_q_up, v_mla_kv_norm, v_mla_w_kv_up, v_sc_conv_w, v_ssd_conv_w, v_ssd_conv_b, v_ssd_dt_bias, v_ssd_a_log, v_ssd_d, v_ssd_norm, v_w_out, v_ffn_w_up, v_ffn_conv_w, v_ffn_conv_b, v_ffn_w_down):
    a = dict(locals())
    axis = {n: (1 if n in TRANSPOSED else ax) for n, ax in SHARDED}
    packed = lambda n, w: jnp.swapaxes(w, -1, -2) if n in TRANSPOSED else w
    shard_shapes = {n: packed(n, a[n]).shape for n in axis}

    def weight_piece(n, l, lo):
        w = packed(n, a[n] if l is None else a[n][l])
        return w - w.astype(BF16).astype(F32) if lo else w

    W, resid, layers = {}, {}, {}
    for buf, (width, rows, ents) in _layout(shard_shapes, per_layer=False).items():
        gathered = _all_gather_weights(_pack((width, rows, ents), weight_piece, BF16))
        for n, l, lo, ps, off, r in ents:
            parts = _unslab(gathered[:, off:off + r], ps, lead=1)
            ax = axis[n] if l is None else axis[n] - 1
            full = jnp.moveaxis(parts, 0, ax)
            full = full.reshape(full.shape[:ax] + (-1,) + full.shape[ax + 2:])
            if l is None:
                (resid if lo else W)[n] = full
            else:
                layers.setdefault(n, []).append(full)
    W.update({n: jnp.stack(v) for n, v in layers.items()})
    for n in resid:
        W[n] = W[n].astype(F32) + resid[n].astype(F32)
    S = {n: a[n] for n in SMALL}

    loss_part, gx, GW, GS = _local_step(a["x"][0], a["positions"][0], a["loss_target"][0], W, S)

    def by_chip(g, ax, parts=N_CHIPS):
        g = g.reshape(g.shape[:ax] + (parts, g.shape[ax] // parts) + g.shape[ax + 1:])
        return jnp.moveaxis(g, ax, 0)

    def grad_piece(n, l, lo):
        if lo:
            return None
        if l is None:
            return by_chip(jnp.stack(GW[n]), axis[n])
        g = GW[n][l]
        if isinstance(g, tuple):
            return jnp.concatenate([by_chip(h, axis[n] - 1, N_CHIPS // 2) for h in g])
        return by_chip(g, axis[n] - 1)

    grads, delta, new_m, new_v = {}, {}, {}, {}
    for buf, (width, rows, ents) in _layout(shard_shapes, per_layer=True).items():
        red = _reduce_scatter_grads(_pack((width, rows, ents), grad_piece, BF16, lead=1))
        for n in (MAIN if buf == "main" else WIDE):
            mine = [e for e in ents if e[0] == n and not e[2]]
            if a[n].shape[-1] == width and all(e[3] == (e[5], width) for e in mine):
                g, g_row = red, mine[0][4]
            elif mine[0][1] is None:
                g, g_row = _unslab(red[mine[0][4]:mine[0][4] + mine[0][5]], mine[0][3]), 0
            else:
                g, g_row = jnp.stack([packed(n, _unslab(red[e[4]:e[4] + e[5]], e[3])) for e in mine]), 0
            grads[n], delta[n], new_m[n], new_v[n] = _adam(a[n], g, a["m_" + n], a["v_" + n], "adamw_" + n, g_row)

    small_shapes = [a[n].shape for n in SMALL]
    rs = -(-(sum(math.prod(s) for s in small_shapes) + 1) // (PACK_COLS * SLAB_ALIGN)) * SLAB_ALIGN
    red = _all_reduce_small(_pack_flat([GS[n] for n in SMALL] + [loss_part.reshape(1)], rs))
    loss = _unpack_flat(red, small_shapes + [(1,)])[-1][0]
    pk = lambda pre: _pack_flat([a[pre + n] for n in SMALL], rs)
    for dst, buf in zip((grads, delta, new_m, new_v), _adam(pk(""), red, pk("m_"), pk("v_"), "adamw_small")):
        dst.update(zip(SMALL, _unpack_flat(buf, small_shapes)))

    return (loss, gx[None], *[grads[n] for n in WEIGHTS], *[delta[n] for n in WEIGHTS], *[new_m[n] for n in WEIGHTS],
            *[new_v[n] for n in WEIGHTS])
```

```python
import functools
import math

import jax
import jax.numpy as jnp
from jax import lax
from jax.experimental import pallas as pl
from jax.experimental.pallas import tpu as pltpu

F32 = jnp.float32
BF16 = jnp.bfloat16
MXU_DTYPE = jnp.bfloat16
HIGHEST = lax.Precision.HIGHEST
MESH = pl.DeviceIdType.MESH

D_MODEL = 1024
DEPTH = 4
HEADS = 8
Q_LORA = 256
KV_LORA = 128
NOPE = 64
ROPE = 32
VDIM = 64
ROPE_THETA = 10000.0
SC_DIM = 256
SSD_HEADS = 4
SSD_HEAD_DIM = 64
SSD_STATE = 128
SSD_DIM = 256
SSD_CONV_DIM = 768
SSD_CHUNK = 128
FFN_DIM = 2816
NORM_EPS = 1e-6
QK_SCALE = (NOPE + ROPE) ** -0.5
LANE = 128
HP = 128
FLASH_HEADS = 2

ZIN = 2560
Z_CQ, Z_CKV, Z_KR, Z_SCB, Z_SCC, Z_SCH, Z_SSZ, Z_XBC, Z_DT = 0, 256, 384, 512, 768, 1024, 1280, 1536, 2304
KR_LANE = 64
YCAT = HEADS * HP + SC_DIM + SSD_DIM
FFN_TILE = 256
FFN_ROWS = 1024

ADAM_LR, ADAM_B1, ADAM_B2, ADAM_EPS, ADAM_WD, ADAM_STEP = 0.001, 0.9, 0.999, 1e-08, 0.01, 10

PACK_COLS = 1024


def _tile(n, pref):
    if n <= pref:
        return n
    t = (pref // LANE) * LANE
    while t >= LANE:
        if n % t == 0:
            return t
        t -= LANE
    raise ValueError(f"no tile for {n}")


MM_TM, MM_TN, MM_TK = 1024, 1408, 1536


def _mm(a, b, mode, out_dtype, name, tm=None, tn=MM_TN, tkmax=MM_TK):
    pair = isinstance(a, tuple)
    a_list = list(a) if pair else [a]
    layer = None
    if isinstance(b, tuple):
        b, layer = b
    bshape = b.shape[-2:]
    if mode == "nn":
        (M, Ka), (_, N) = a_list[0].shape, bshape
    elif mode == "nt":
        (M, Ka), (N, _) = a_list[0].shape, bshape
    else:
        (Ka, M), (_, N) = a_list[0].shape, bshape
    tm = (MM_TN if mode == "tn" else MM_TM) if tm is None else tm
    tm, tn, tk = _tile(M, tm), _tile(N, tn), _tile(Ka, tkmax)
    nka = Ka // tk
    nk = nka * len(a_list)

    def bspec(shape, index):
        if layer is None:
            return pl.BlockSpec(shape, index)
        return pl.BlockSpec((None,) + shape, lambda i, j, k: (layer,) + index(i, j, k))

    if mode == "nn":
        a_specs = [pl.BlockSpec((tm, tk), lambda i, j, k: (i, jnp.minimum(k, nka - 1))),
                   pl.BlockSpec((tm, tk), lambda i, j, k: (i, jnp.maximum(k - nka, 0)))][:len(a_list)]
        b_spec = bspec((tk, tn), lambda i, j, k: (k, j))
        dims = NN
    elif mode == "nt":
        a_specs = [pl.BlockSpec((tm, tk), lambda i, j, k: (i, jnp.minimum(k, nka - 1))),
                   pl.BlockSpec((tm, tk), lambda i, j, k: (i, jnp.maximum(k - nka, 0)))][:len(a_list)]
        b_spec = bspec((tn, tk), lambda i, j, k: (j, k))
        dims = NT
    else:
        a_specs = [pl.BlockSpec((tk, tm), lambda i, j, k: (k, i))]
        b_spec = pl.BlockSpec((tk, tn), lambda i, j, k: (k, j))
        dims = TN
    na = len(a_list)

    def body(*refs):
        a_refs, b_ref, o_ref = refs[:na], refs[na], refs[na + 1]
        k = pl.program_id(2)

        def prod(a_ref):
            return lax.dot_general(a_ref[...].astype(MXU_DTYPE), b_ref[...].astype(MXU_DTYPE), dims, preferred_element_type=F32)

        if nk == 1:
            o_ref[...] = prod(a_refs[0]).astype(o_ref.dtype)
            return
        acc_ref = refs[na + 2]

        @pl.when(k == 0)
        def _():
            acc_ref[...] = prod(a_refs[0])

        @pl.when((k > 0) & (k < nka))
        def _():
            acc_ref[...] += prod(a_refs[0])

        if pair:
            @pl.when(k >= nka)
            def _():
                acc_ref[...] += prod(a_refs[1])

        @pl.when(k == nk - 1)
        def _():
            o_ref[...] = acc_ref[...].astype(o_ref.dtype)

    return pl.pallas_call(
        body, name=name, grid=(M // tm, N // tn, nk),
        in_specs=a_specs + [b_spec], out_specs=pl.BlockSpec((tm, tn), lambda i, j, k: (i, j)),
        out_shape=jax.ShapeDtypeStruct((M, N), out_dtype),
        scratch_shapes=[pltpu.VMEM((tm, tn), F32)] if nk > 1 else [],
        compiler_params=pltpu.CompilerParams(dimension_semantics=("parallel", "parallel", "arbitrary")),
    )(*a_list, b)


HALO = 8


def _const(j, v):
    return v


def _rows(fn, T, tm, ins, consts, outs, accs, name, ncol=1):
    n = T // tm
    hb = tm // HALO
    last = T // HALO - 1
    in_specs, args = [], []
    for arr, bc, cb, kind in ins:
        if isinstance(kind, int):
            in_specs.append(pl.BlockSpec((tm, bc), lambda j, i, cb=cb, off=kind: (i + off, cb(j))))
        elif kind == "cur":
            in_specs.append(pl.BlockSpec((tm, bc), lambda j, i, cb=cb: (i, cb(j))))
        elif kind == "prev":
            in_specs.append(pl.BlockSpec((HALO, bc), lambda j, i, cb=cb: (jnp.maximum(i * hb - 1, 0), cb(j))))
        else:
            in_specs.append(pl.BlockSpec((HALO, bc), lambda j, i, cb=cb: (jnp.minimum((i + 1) * hb, last), cb(j))))
        args.append(arr)
    for arr, bc, cb in consts:
        in_specs.append(pl.BlockSpec((arr.shape[0], bc), lambda j, i, cb=cb: (0, cb(j))))
        args.append(arr)
    out_specs, out_shape = [], []
    for tc, dt, bc, cb in outs:
        out_specs.append(pl.BlockSpec((tm, bc), lambda j, i, cb=cb: (i, cb(j))))
        out_shape.append(jax.ShapeDtypeStruct((T, tc), dt))
    for r, tc, bc, cb in accs:
        out_specs.append(pl.BlockSpec((r, bc), lambda j, i, cb=cb: (0, cb(j))))
        out_shape.append(jax.ShapeDtypeStruct((r, tc), F32))
    nin, nout, nacc = len(args), len(outs), len(accs)

    def body(*refs):
        i = pl.program_id(1)
        res = fn(i, n, *[r[...] for r in refs[:nin]])
        for r, v in zip(refs[nin:nin + nout], res[:nout]):
            r[...] = v.astype(r.dtype)
        if nacc:
            acc_refs = refs[nin + nout:nin + nout + nacc]

            @pl.when(i == 0)
            def _():
                for r in acc_refs:
                    r[...] = jnp.zeros_like(r)

            for r, v in zip(acc_refs, res[nout:]):
                r[...] += v.astype(F32)

    res = pl.pallas_call(
        body, name=name, grid=(ncol, n), in_specs=in_specs, out_specs=out_specs, out_shape=out_shape,
        compiler_params=pltpu.CompilerParams(dimension_semantics=("arbitrary", "arbitrary")),
    )(*args)
    return res


def _cur(arr, bc=None, blk=0):
    bc = arr.shape[1] if bc is None else bc
    return (arr, bc, functools.partial(_const, v=blk), "cur")


def _halo(arr, kind, bc=None, blk=0):
    bc = arr.shape[1] if bc is None else bc
    return (arr, bc, functools.partial(_const, v=blk), kind)


def _cst(arr):
    return (arr, arr.shape[1], functools.partial(_const, v=0))


def _out(cols, dt):
    return (cols, dt, cols, functools.partial(_const, v=0))


def _acc(rows, cols):
    return (rows, cols, cols, functools.partial(_const, v=0))


def _rms(x, w):
    return x * lax.rsqrt(jnp.mean(x * x, axis=-1, keepdims=True) + NORM_EPS) * w


def _sigmoid(x):
    return 0.5 * jnp.tanh(0.5 * x) + 0.5


def _silu(x):
    return x * _sigmoid(x)


def _dsilu(x):
    s = _sigmoid(x)
    return s * (1.0 + x * (1.0 - s))


def _softplus(x):
    return jnp.maximum(x, 0.0) + jnp.log1p(jnp.exp(-jnp.abs(x)))


def _shift(a, k):
    return pltpu.roll(a, k % a.shape[0], 0)


def _lroll(a, k):
    return pltpu.roll(a, k % a.shape[1], 1)


def _vjp_wrap(f, nrow, nconst, add_first=False):
    def g(i, n, *vals):
        rows, consts, mid = vals[:nrow], vals[len(vals) - nconst:], vals[nrow:len(vals) - nconst]
        cots = mid[:-1] if add_first else mid
        outs, pull = jax.vjp(f, *rows, *consts)
        grads = list(pull(tuple(c.astype(o.dtype) for c, o in zip(cots, outs))))
        if add_first:
            grads[0] = grads[0] + mid[-1]
        return tuple(grads)
    return g


def _rows_vjp(f, T, tm, rows, consts, cots, out_dtypes, name):
    return _rows(_vjp_wrap(f, len(rows), len(consts)), T, tm, [_cur(r) for r in rows] + [_cur(c) for c in cots],
                 [_cst(c) for c in consts], [_out(r.shape[1], dt) for r, dt in zip(rows, out_dtypes)],
                 [_acc(1, c.shape[1]) for c in consts], name)


def _f_premix(x, g):
    return (_rms(x, g),)


def _f_mla_pre(cq, ckv, qn, kvn):
    return _rms(cq, qn), _rms(ckv, kvn)


def _f_ssd_gate(y, z, nw):
    return (_rms(y * _silu(z), nw),)


def _f_post_mix(x, mixed, gpost, gffn):
    x1 = x + _rms(mixed, gpost)
    return x1, _rms(x1, gffn)


def _f_post_ffn(x1, d, gpost):
    return (x1 + _rms(d, gpost),)


def _rope_fwd(v, cosf, sina, sinb):
    return v * cosf + _lroll(v, -16) * sina + _lroll(v, 16) * sinb


def _rope_bwd(g, cosf, sina, sinb):
    return g * cosf + _lroll(g * sina, 16) + _lroll(g * sinb, -16)


def _k_rope_fwd(i, n, qpad, kvpad, kr, cosf, sina, sinb):
    qs, ks = [], []
    krr = _rope_fwd(kr, cosf, sina, sinb)
    for h in range(HEADS):
        sl = slice(h * HP, (h + 1) * HP)
        qs.append(_rope_fwd(qpad[:, sl], cosf, sina, sinb))
        ks.append(kvpad[:, sl].astype(F32) + krr)
    return jnp.concatenate(qs, axis=1), jnp.concatenate(ks, axis=1)


def _k_rope_bwd(i, n, dq, dk, dv, cosf, sina, sinb):
    lane = lax.broadcasted_iota(jnp.int32, (1, HP), 1)
    rmask = ((lane >= KR_LANE) & (lane < KR_LANE + ROPE)).astype(F32)
    dqs, dks = [], []
    dkr = jnp.zeros((dq.shape[0], HP), F32)
    for h in range(HEADS):
        sl = slice(h * HP, (h + 1) * HP)
        dqs.append(_rope_bwd(dq[:, sl], cosf, sina, sinb))
        dkh = dk[:, sl]
        dkr = dkr + dkh * rmask
        dks.append(dkh * (1.0 - rmask))
    dkr = _rope_bwd(dkr, cosf, sina, sinb) * rmask
    return jnp.concatenate(dqs, axis=1), jnp.concatenate(dks + [dv], axis=1), dkr


def _k_sconv_fwd(i, n, b, c, h, cp, hp, w):
    m = b.shape[0]
    up = jnp.where(i > 0, cp * hp, 0.0)
    ue = jnp.concatenate([up, c * h], axis=0)
    conv = w[2:3] * ue + w[1:2] * _shift(ue, 1) + w[0:1] * _shift(ue, 2)
    return (b * conv[HALO:],)


def _k_sconv_bwd(i, n, b, c, h, dy, cp, hp, bn, dyn, w):
    m = b.shape[0]
    up = jnp.where(i > 0, cp * hp, 0.0)
    ue = jnp.concatenate([up, c * h], axis=0)
    u1, u2 = _shift(ue, 1), _shift(ue, 2)
    conv = (w[2:3] * ue + w[1:2] * u1 + w[0:1] * u2)[HALO:]
    dc_cur = dy * b
    dce = jnp.concatenate([dc_cur, jnp.where(i < n - 1, dyn * bn, 0.0)], axis=0)
    du = (w[2:3] * dce + w[1:2] * _shift(dce, -1) + w[0:1] * _shift(dce, -2))[:m]
    dw = jnp.concatenate([
        jnp.sum(dc_cur * u2[HALO:], axis=0, keepdims=True),
        jnp.sum(dc_cur * u1[HALO:], axis=0, keepdims=True),
        jnp.sum(dc_cur * ue[HALO:], axis=0, keepdims=True),
        jnp.zeros((HALO - 3, b.shape[1]), F32)], axis=0)
    return dy * conv, du * h, du * c, dw


def _conv4(ue, w):
    return w[3:4] * ue + w[2:3] * _shift(ue, 1) + w[1:2] * _shift(ue, 2) + w[0:1] * _shift(ue, 3)


def _k_ssdconv_fwd(i, n, u, up, w, bias):
    ue = jnp.concatenate([jnp.where(i > 0, up, 0.0), u], axis=0)
    return (_silu(_conv4(ue, w)[HALO:] + bias),)


def _k_ssdconv_bwd(i, n, u, dout, up, un, doutn, w, bias):
    m = u.shape[0]
    ue = jnp.concatenate([jnp.where(i > 0, up, 0.0), u, un], axis=0)
    u1, u2, u3 = _shift(ue, 1), _shift(ue, 2), _shift(ue, 3)
    pre = (w[3:4] * ue + w[2:3] * u1 + w[1:2] * u2 + w[0:1] * u3)[HALO:] + bias
    doe = jnp.concatenate([dout, jnp.where(i < n - 1, doutn, 0.0)], axis=0)
    dpre = doe * _dsilu(pre)
    du = (w[3:4] * dpre + w[2:3] * _shift(dpre, -1) + w[1:2] * _shift(dpre, -2) + w[0:1] * _shift(dpre, -3))[:m]
    dp = dpre[:m]
    cur = slice(HALO, HALO + m)
    dw = jnp.concatenate([
        jnp.sum(dp * u3[cur], axis=0, keepdims=True),
        jnp.sum(dp * u2[cur], axis=0, keepdims=True),
        jnp.sum(dp * u1[cur], axis=0, keepdims=True),
        jnp.sum(dp * ue[cur], axis=0, keepdims=True),
        jnp.zeros((HALO - 4, u.shape[1]), F32)], axis=0)
    db = jnp.sum(dp, axis=0, keepdims=True)
    return du, dw, db


def _conv3(ue, w):
    return w[2:3] * ue + w[1:2] * _shift(ue, 1) + w[0:1] * _shift(ue, 2)


def _k_ffnact_fwd(i, n, ug, uu, ugp, uup, wg, wu, bg, bu):
    gate = _conv3(jnp.concatenate([jnp.where(i > 0, ugp, 0.0), ug], axis=0), wg)[HALO:] + bg
    upv = _conv3(jnp.concatenate([jnp.where(i > 0, uup, 0.0), uu], axis=0), wu)[HALO:] + bu
    return (_silu(gate) * upv,)


def _k_ffnact_bwd(i, n, ug, uu, dact, ugp, uup, ugn, uun, dactn, wg, wu, bg, bu):
    m = ug.shape[0]
    cur = slice(HALO, HALO + m)

    def taps(p, c, nx):
        e = jnp.concatenate([jnp.where(i > 0, p, 0.0), c, nx], axis=0)
        return e, _shift(e, 1), _shift(e, 2)

    def back(d, w):
        return (w[2:3] * d + w[1:2] * _shift(d, -1) + w[0:1] * _shift(d, -2))[:m]

    def wgrad(d, t):
        return jnp.concatenate([jnp.sum(d[:m] * t[2][cur], axis=0, keepdims=True), jnp.sum(d[:m] * t[1][cur], axis=0, keepdims=True),
                                jnp.sum(d[:m] * t[0][cur], axis=0, keepdims=True), jnp.zeros((HALO - 3, d.shape[1]), F32)], axis=0)

    tg, tu = taps(ugp, ug, ugn), taps(uup, uu, uun)
    gate = (wg[2:3] * tg[0] + wg[1:2] * tg[1] + wg[0:1] * tg[2])[HALO:] + bg
    upv = (wu[2:3] * tu[0] + wu[1:2] * tu[1] + wu[0:1] * tu[2])[HALO:] + bu
    dae = jnp.concatenate([dact, jnp.where(i < n - 1, dactn, 0.0)], axis=0)
    sg = _sigmoid(gate)
    dg = dae * upv * (sg * (1.0 + gate * (1.0 - sg)))
    dup = dae * (gate * sg)
    return (back(dg, wg), back(dup, wu), wgrad(dg, tg), wgrad(dup, tu),
            jnp.sum(dg[:m], axis=0, keepdims=True), jnp.sum(dup[:m], axis=0, keepdims=True))


def _k_loss(i, n, y, tgt):
    e = y - tgt
    part = 0.5 * jnp.sum(jnp.sum(e * e, axis=1, keepdims=True) / D_MODEL, axis=0, keepdims=True)
    return e * (1.0 / D_MODEL), jnp.broadcast_to(part, (1, LANE))


def _k_adam(i, n, w, g, m, v):
    m = ADAM_B1 * m + (1.0 - ADAM_B1) * g
    v = ADAM_B2 * v + (1.0 - ADAM_B2) * (g * g)
    m_hat = m / (1.0 - ADAM_B1 ** ADAM_STEP)
    v_hat = v / (1.0 - ADAM_B2 ** ADAM_STEP)
    delta = -ADAM_LR * (m_hat / (jnp.sqrt(v_hat) + ADAM_EPS) + ADAM_WD * w)
    return g, delta, m, v


def _dotf(a, b, dims):
    return lax.dot_general(a.astype(MXU_DTYPE), b.astype(MXU_DTYPE), dims, preferred_element_type=F32)


NN = (((1,), (0,)), ((), ()))
NT = (((1,), (1,)), ((), ()))
TN = (((0,), (0,)), ((), ()))


def _ssd_chunk(x0, x1, x2, x3, b0, b1, c0, c1, dtraw, p0, p1, p2, p3, dtb, alog, dsk):
    xs, bs, cs_, ps = (x0, x1, x2, x3), (b0, b1), (c0, c1), (p0, p1, p2, p3)
    L = dtraw.shape[0]
    dt = _softplus(dtraw + dtb)
    adt = dt * (-jnp.exp(alog))
    row = lax.broadcasted_iota(jnp.int32, (L, L), 0)
    col = lax.broadcasted_iota(jnp.int32, (L, L), 1)
    tril = row >= col
    cum = jnp.dot(tril.astype(F32), adt, precision=HIGHEST, preferred_element_type=F32)
    cum_t = cum.T
    lane = lax.broadcasted_iota(jnp.int32, (1, LANE), 1)
    sub = lax.broadcasted_iota(jnp.int32, (LANE, 1), 0)
    lastcol = (lax.broadcasted_iota(jnp.int32, (1, L), 1) == L - 1).astype(F32)
    ys, news = [], []
    for h in range(SSD_HEADS):
        g = h // (SSD_HEADS // 2)
        oh = (lane == h).astype(F32)
        dth = jnp.sum(dt * oh, axis=1, keepdims=True)
        csh = jnp.sum(cum * oh, axis=1, keepdims=True)
        csr = jnp.sum(cum_t * (sub == h).astype(F32), axis=0, keepdims=True)
        cl = jnp.sum(csr * lastcol, axis=1, keepdims=True)
        dskh = jnp.sum(dsk * oh, axis=1, keepdims=True)
        x, bm, cm, prev = xs[h], bs[g], cs_[g], ps[h]
        xdt = x * dth
        decay = jnp.exp(jnp.where(tril, csh - csr, -jnp.inf))
        scores = _dotf(cm, bm, NT) * decay
        y_diag = _dotf(scores, xdt, NN)
        bd = bm * jnp.exp(cl - csh)
        cst = _dotf(xdt, bd, TN)
        news.append(prev * jnp.exp(cl) + cst)
        y_off = _dotf(cm, prev, NT) * jnp.exp(csh)
        ys.append(y_diag + y_off + x * dskh)
    return (*ys, *news)


def _ssd_operands(x_ref, dt_ref, par_ref, prev):
    xs = [x_ref[:, h * SSD_HEAD_DIM:(h + 1) * SSD_HEAD_DIM] for h in range(SSD_HEADS)]
    bs = [x_ref[:, SSD_DIM + g * SSD_STATE:SSD_DIM + (g + 1) * SSD_STATE] for g in range(2)]
    cs_ = [x_ref[:, SSD_DIM + 2 * SSD_STATE + g * SSD_STATE:SSD_DIM + 2 * SSD_STATE + (g + 1) * SSD_STATE] for g in range(2)]
    return (*xs, *bs, *cs_, dt_ref[...], *prev, par_ref[0:1, :], par_ref[1:2, :], par_ref[2:3, :])


def _ssd_fwd(xbc, dtraw, par, T):
    L = SSD_CHUNK
    nc = T // L
    P = SSD_HEAD_DIM

    def body(x_ref, dt_ref, par_ref, y_ref, st_ref, state):
        @pl.when(pl.program_id(0) == 0)
        def _():
            state[...] = jnp.zeros_like(state)

        st_ref[0] = state[...]
        prev = [state[h * P:(h + 1) * P, :] for h in range(SSD_HEADS)]
        res = _ssd_chunk(*_ssd_operands(x_ref, dt_ref, par_ref, prev))
        for h in range(SSD_HEADS):
            y_ref[:, h * P:(h + 1) * P] = res[h]
            state[h * P:(h + 1) * P, :] = res[SSD_HEADS + h]

    return pl.pallas_call(
        body, name="ssd_scan_fwd", grid=(nc,),
        in_specs=[pl.BlockSpec((L, SSD_CONV_DIM), lambda c: (c, 0)), pl.BlockSpec((L, LANE), lambda c: (c, 0)),
                  pl.BlockSpec((8, LANE), lambda c: (0, 0))],
        out_specs=[pl.BlockSpec((L, SSD_DIM), lambda c: (c, 0)), pl.BlockSpec((1, SSD_DIM, SSD_STATE), lambda c: (c, 0, 0))],
        out_shape=[jax.ShapeDtypeStruct((T, SSD_DIM), F32), jax.ShapeDtypeStruct((nc, SSD_DIM, SSD_STATE), F32)],
        scratch_shapes=[pltpu.VMEM((SSD_DIM, SSD_STATE), F32)],
        compiler_params=pltpu.CompilerParams(dimension_semantics=("arbitrary",)),
    )(xbc, dtraw, par)


def _ssd_bwd(xbc, dtraw, par, states, dy, T):
    L = SSD_CHUNK
    nc = T // L
    P = SSD_HEAD_DIM

    def body(x_ref, dt_ref, par_ref, st_ref, dy_ref, dx_ref, ddt_ref, dpar_ref, dstate):
        @pl.when(pl.program_id(0) == 0)
        def _():
            dstate[...] = jnp.zeros_like(dstate)
            dpar_ref[...] = jnp.zeros_like(dpar_ref)

        prev = [st_ref[0, h * P:(h + 1) * P, :] for h in range(SSD_HEADS)]
        prim = _ssd_operands(x_ref, dt_ref, par_ref, prev)
        _, pull = jax.vjp(_ssd_chunk, *prim)
        cots = tuple(dy_ref[:, h * P:(h + 1) * P] for h in range(SSD_HEADS)) + tuple(
            dstate[h * P:(h + 1) * P, :] for h in range(SSD_HEADS))
        g = pull(cots)
        for h in range(SSD_HEADS):
            dx_ref[:, h * P:(h + 1) * P] = g[h]
            dstate[h * P:(h + 1) * P, :] = g[9 + h]
        for k in range(2):
            dx_ref[:, SSD_DIM + k * SSD_STATE:SSD_DIM + (k + 1) * SSD_STATE] = g[4 + k]
            dx_ref[:, SSD_DIM + 2 * SSD_STATE + k * SSD_STATE:SSD_DIM + 2 * SSD_STATE + (k + 1) * SSD_STATE] = g[6 + k]
        ddt_ref[...] = g[8]
        for r in range(3):
            dpar_ref[r:r + 1, :] += g[13 + r]

    rev = lambda c: (nc - 1 - c, 0)
    return pl.pallas_call(
        body, name="ssd_scan_bwd", grid=(nc,),
        in_specs=[pl.BlockSpec((L, SSD_CONV_DIM), rev), pl.BlockSpec((L, LANE), rev), pl.BlockSpec((8, LANE), lambda c: (0, 0)),
                  pl.BlockSpec((1, SSD_DIM, SSD_STATE), lambda c: (nc - 1 - c, 0, 0)), pl.BlockSpec((L, SSD_DIM), rev)],
        out_specs=[pl.BlockSpec((L, SSD_CONV_DIM), rev), pl.BlockSpec((L, LANE), rev), pl.BlockSpec((8, LANE), lambda c: (0, 0))],
        out_shape=[jax.ShapeDtypeStruct((T, SSD_CONV_DIM), F32), jax.ShapeDtypeStruct((T, LANE), F32),
                   jax.ShapeDtypeStruct((8, LANE), F32)],
        scratch_shapes=[pltpu.VMEM((SSD_DIM, SSD_STATE), F32)],
        compiler_params=pltpu.CompilerParams(dimension_semantics=("arbitrary",)),
    )(xbc, dtraw, par, states, dy)


def _causal_pairs(nq, by_query):
    if by_query:
        pairs = [(i, j) for i in range(nq) for j in range(i + 1)]
    else:
        pairs = [(i, j) for j in range(nq) for i in range(j, nq)]
    return jnp.asarray([p[0] for p in pairs], jnp.int32), jnp.asarray([p[1] for p in pairs], jnp.int32)


def _flash_fwd(q, k, kv, T, carry=()):
    tq = tk = min(512, T)
    nq = T // tq
    G = FLASH_HEADS
    rep = tk // HP
    nc = len(carry)
    qi, kj = _causal_pairs(nq, by_query=True)
    nh, nt = HEADS // G, qi.shape[0]

    def body(qi_ref, kj_ref, q_ref, k_ref, v_ref, *rest):
        w_refs, o_ref, g_refs = rest[:nc], rest[nc], rest[nc + 1:2 * nc + 1]
        m_ref, l_ref, acc_ref = rest[2 * nc + 1:2 * nc + 4]
        h, t = pl.program_id(0), pl.program_id(1)
        i, j = qi_ref[t], kj_ref[t]
        if nc:
            plan = lambda: _ag_plan(w_refs, g_refs, rest[2 * nc + 4:])

            @pl.when((h == 0) & (t == 0))
            def _():
                for cp in plan()[0]:
                    cp.start()

        @pl.when(j == 0)
        def _():
            m_ref[...] = jnp.full_like(m_ref, -jnp.inf)
            l_ref[...] = jnp.zeros_like(l_ref)
            acc_ref[...] = jnp.zeros_like(acc_ref)

        def step(diagonal):
            for g in range(G):
                sl = slice(g * HP, (g + 1) * HP)
                s = _dotf(q_ref[:, sl], k_ref[:, sl], NT) * QK_SCALE
                if diagonal:
                    rows = lax.broadcasted_iota(jnp.int32, (tq, tk), 0)
                    cols = lax.broadcasted_iota(jnp.int32, (tq, tk), 1)
                    s = jnp.where(rows >= cols, s, -jnp.inf)
                m_old = m_ref[:, sl]
                m_new = jnp.maximum(m_old, jnp.max(s, axis=1, keepdims=True))
                p = jnp.exp(s - jnp.tile(m_new, (1, rep)))
                alpha = jnp.exp(m_old - m_new)
                l_ref[:, sl] = alpha * l_ref[:, sl] + jnp.sum(p, axis=1, keepdims=True)
                acc_ref[:, sl] = alpha * acc_ref[:, sl] + _dotf(p, v_ref[:, sl], NN)
                m_ref[:, sl] = m_new

        @pl.when(j < i)
        def _():
            step(False)

        @pl.when(j == i)
        def _():
            step(True)
            lane = lax.broadcasted_iota(jnp.int32, (tq, HP), 1)
            for g in range(G):
                sl = slice(g * HP, (g + 1) * HP)
                l = l_ref[:, sl]
                o_ref[:, sl] = jnp.where(lane < VDIM, acc_ref[:, sl] / l, m_ref[:, sl] + jnp.log(l))

        if nc:
            @pl.when((h == nh // 2) & (t == 0))
            def _():
                _, lands, forwards, _ = plan()
                for land, fw in zip(lands, forwards):
                    land.wait_recv()
                    fw.start()

            @pl.when((h == nh - 1) & (t == nt - 1))
            def _():
                sends, _, forwards, finals = plan()
                for cp in finals:
                    cp.wait_recv()
                for cp in sends + forwards:
                    cp.wait_send()

    W = G * HP
    res = pl.pallas_call(
        body, name="mla_flash_fwd",
        grid_spec=pltpu.PrefetchScalarGridSpec(
            num_scalar_prefetch=2, grid=(nh, nt),
            in_specs=[pl.BlockSpec((tq, W), lambda h, t, qi, kj: (qi[t], h)),
                      pl.BlockSpec((tk, W), lambda h, t, qi, kj: (kj[t], h)),
                      pl.BlockSpec((tk, W), lambda h, t, qi, kj: (kj[t], HEADS // G + h))] + [ANY] * nc,
            out_specs=[pl.BlockSpec((tq, W), lambda h, t, qi, kj: (qi[t], h))] + [ANY] * nc,
            scratch_shapes=[pltpu.VMEM((tq, W), F32), pltpu.VMEM((tq, W), F32), pltpu.VMEM((tq, W), F32)] + (_ag_sems(nc) if nc else [])),
        out_shape=[jax.ShapeDtypeStruct((T, HEADS * HP), F32)] + [jax.ShapeDtypeStruct((N_CHIPS,) + w.shape, w.dtype) for w in carry],
        compiler_params=pltpu.CompilerParams(dimension_semantics=("arbitrary", "arbitrary")),
    )(qi, kj, q, k, kv, *carry)
    return res[0] if not nc else (res[0], [_own_slot(g, w) for g, w in zip(res[1:], carry)])


def _flash_bwd(q, k, kv, o, dycat, T, carry=()):
    tq = tk = min(512, T)
    nq = T // tq
    G = FLASH_HEADS
    nc = len(carry)
    qi, kj = _causal_pairs(nq, by_query=False)
    nh, nt = HEADS // G, qi.shape[0]

    def body(qi_ref, kj_ref, q_ref, k_ref, v_ref, o_ref, do_ref, *rest):
        p_refs, (dq_ref, dk_ref, dv_ref), part_refs = rest[:nc], rest[nc:nc + 3], rest[nc + 3:2 * nc + 3]
        h, t = pl.program_id(0), pl.program_id(1)
        i, j = qi_ref[t], kj_ref[t]
        if nc:
            plan = lambda: _chip_plan(p_refs, part_refs, rest[2 * nc + 3:])

            @pl.when((h == 0) & (t == 0))
            def _():
                for cp in plan()[0]:
                    cp.start()

        @pl.when(t == 0)
        def _():
            dq_ref[...] = jnp.zeros_like(dq_ref)

        @pl.when(i == j)
        def _():
            dk_ref[...] = jnp.zeros_like(dk_ref)
            dv_ref[...] = jnp.zeros_like(dv_ref)

        def step(diagonal):
            r0 = pl.multiple_of(i * tq, tq)
            for g in range(G):
                sl = slice(g * HP, (g + 1) * HP)
                qv, kv, vv, ov, dov = q_ref[:, sl], k_ref[:, sl], v_ref[:, sl], o_ref[:, sl], do_ref[:, sl]
                s = _dotf(qv, kv, NT) * QK_SCALE
                p = jnp.exp(s - ov[:, VDIM:VDIM + 1])
                if diagonal:
                    rows = lax.broadcasted_iota(jnp.int32, (tq, tk), 0)
                    cols = lax.broadcasted_iota(jnp.int32, (tq, tk), 1)
                    p = jnp.where(rows >= cols, p, 0.0)
                dsum = jnp.sum(dov * ov, axis=1, keepdims=True)
                dv_ref[:, sl] += _dotf(p, dov, TN)
                dp = _dotf(dov, vv, NT)
                ds = p * (dp - dsum) * QK_SCALE
                dk_ref[:, sl] += _dotf(ds, qv, TN)
                dq_ref[pl.ds(r0, tq), sl] += _dotf(ds, kv, NN)

        @pl.when(i > j)
        def _():
            step(False)

        @pl.when(i == j)
        def _():
            step(True)

        if nc:
            @pl.when((h == nh - 1) & (t == nt - 1))
            def _():
                sends, lands = plan()
                for cp in lands:
                    cp.wait_recv()
                for cp in sends:
                    cp.wait_send()

    W = G * HP
    qmap = lambda h, t, qi, kj: (qi[t], h)
    kmap = lambda h, t, qi, kj: (kj[t], h)
    vmap = lambda h, t, qi, kj: (kj[t], HEADS // G + h)
    res = pl.pallas_call(
        body, name="mla_flash_bwd",
        grid_spec=pltpu.PrefetchScalarGridSpec(
            num_scalar_prefetch=2, grid=(nh, nt),
            in_specs=[pl.BlockSpec((tq, W), qmap), pl.BlockSpec((tk, W), kmap), pl.BlockSpec((tk, W), vmap),
                      pl.BlockSpec((tq, W), qmap), pl.BlockSpec((tq, W), qmap)] + [ANY] * nc,
            out_specs=[pl.BlockSpec((T, W), lambda h, t, qi, kj: (0, h)), pl.BlockSpec((tk, W), kmap), pl.BlockSpec((tk, W), kmap)]
            + [ANY] * nc,
            scratch_shapes=_chip_sems(nc) if nc else []),
        out_shape=[jax.ShapeDtypeStruct((T, HEADS * HP), F32)] * 3 + [jax.ShapeDtypeStruct(p.shape, p.dtype) for p in carry],
        compiler_params=pltpu.CompilerParams(dimension_semantics=("arbitrary", "arbitrary")),
    )(qi, kj, q, k, kv, o, dycat, *carry)
    return tuple(res[:3]) if not nc else (*res[:3], _chip_parts(res[3:], carry))


_IN_SRC = (0, 256, 384, 416, 672, 928, 1184, 1440, 2208, 2212)
_IN_DST = (Z_CQ, Z_CKV, Z_KR + KR_LANE, Z_SCB, Z_SCC, Z_SCH, Z_SSZ, Z_XBC, Z_DT)


def _pad_rows_in(w):
    ax = w.ndim - 2

    def zeros(n):
        return jnp.zeros(w.shape[:ax] + (n,) + w.shape[ax + 1:], w.dtype)

    def whole_tiles(p):
        n = p.shape[ax]
        return p if n % SLAB_ALIGN == 0 else jnp.pad(p, [(0, 0)] * ax + [(0, -n % SLAB_ALIGN), (0, 0)])

    parts, at = [], 0
    for s0, s1, d0 in zip(_IN_SRC[:-1], _IN_SRC[1:], _IN_DST):
        if d0 > at:
            parts.append(zeros(d0 - at))
        parts.append(whole_tiles(lax.slice_in_dim(w, s0, s1, axis=ax)))
        at = d0 + parts[-1].shape[ax]
    parts.append(zeros(ZIN - at))
    return jnp.concatenate(parts, axis=ax)


def _unpad_rows_in(w):
    ax = w.ndim - 2
    groups = list(zip(_IN_SRC[:-1], _IN_SRC[1:], _IN_DST))
    parts = [lax.slice_in_dim(w, d0, d0 + -(-(s1 - s0) // SLAB_ALIGN) * SLAB_ALIGN, axis=ax) for s0, s1, d0 in groups]
    return lax.slice_in_dim(jnp.concatenate(parts, axis=ax), 0, _IN_SRC[-1], axis=ax)


def _pad_heads(w, width):
    w = w.reshape(w.shape[:-1] + (HEADS, width))
    w = jnp.pad(w, [(0, 0)] * (w.ndim - 1) + [(0, HP - width)])
    return w.reshape(w.shape[:-2] + (HEADS * HP,))


def _unpad_heads(w, width):
    w = w.reshape(w.shape[:-1] + (HEADS, HP))[..., :width]
    return w.reshape(w.shape[:-2] + (HEADS * width,))


def _pad_kv(w):
    w = w.reshape(w.shape[:-1] + (HEADS, NOPE + VDIM))
    return jnp.concatenate([_pad_heads(w[..., :NOPE].reshape(w.shape[:-2] + (HEADS * NOPE,)), NOPE),
                            _pad_heads(w[..., NOPE:].reshape(w.shape[:-2] + (HEADS * VDIM,)), VDIM)], axis=-1)


def _unpad_kv(w):
    k = _unpad_heads(w[..., :HEADS * HP], NOPE).reshape(w.shape[:-1] + (HEADS, NOPE))
    v = _unpad_heads(w[..., HEADS * HP:], VDIM).reshape(w.shape[:-1] + (HEADS, VDIM))
    return jnp.concatenate([k, v], axis=-1).reshape(w.shape[:-1] + (HEADS * (NOPE + VDIM),))


def _pad_out_rows(w):
    lead, d = w.shape[:-2], w.shape[-1]
    att = w[..., :HEADS * VDIM, :].reshape(lead + (HEADS, VDIM, d))
    att = jnp.pad(att, [(0, 0)] * (att.ndim - 2) + [(0, HP - VDIM), (0, 0)]).reshape(lead + (HEADS * HP, d))
    return jnp.concatenate([att, w[..., HEADS * VDIM:, :]], axis=-2)


def _unpad_out_rows(w):
    lead, d = w.shape[:-2], w.shape[-1]
    att = w[..., :HEADS * HP, :].reshape(lead + (HEADS, HP, d))[..., :VDIM, :].reshape(lead + (HEADS * VDIM, d))
    return jnp.concatenate([att, w[..., HEADS * HP:, :]], axis=-2)


def _rows8(w):
    return jnp.pad(w.astype(F32), [(0, 0)] * (w.ndim - 2) + [(0, 8 - w.shape[-2]), (0, 0)])


def _row8(*vecs):
    c = vecs[0].shape[-1]
    return jnp.concatenate([v.reshape(1, c).astype(F32) for v in vecs] + [jnp.zeros((8 - len(vecs), c), F32)], axis=0)


def _lanes(v):
    return jnp.pad(v.astype(F32), (0, LANE - v.shape[0])).reshape(1, LANE)


def _rope_tables(positions):
    inv_freq = 1.0 / (ROPE_THETA ** (jnp.arange(0, ROPE, 2, dtype=F32) / ROPE))
    ang = positions.astype(F32)[:, None] * inv_freq
    cos, sin = jnp.cos(ang), jnp.sin(ang)
    T = positions.shape[0]
    half = ROPE // 2
    one = jnp.ones((T, KR_LANE), F32)
    zero = jnp.zeros((T, KR_LANE), F32)
    tail1 = jnp.ones((T, HP - KR_LANE - ROPE), F32)
    tail0 = jnp.zeros((T, HP - KR_LANE - ROPE), F32)
    z16 = jnp.zeros((T, half), F32)
    cosf = jnp.concatenate([one, cos, cos, tail1], axis=1)
    sina = jnp.concatenate([zero, -sin, z16, tail0], axis=1)
    sinb = jnp.concatenate([zero, z16, sin, tail0], axis=1)
    return cosf, sina, sinb


def _kernel_weights(W):
    c = lambda a: a.astype(MXU_DTYPE)
    return dict(
        w_in=c(_pad_rows_in(W["w_in"])),
        w_q=c(_pad_heads(W["mla_w_q_up"], NOPE + ROPE)),
        w_kv=c(_pad_kv(W["mla_w_kv_up"])),
        w_out=c(_pad_out_rows(W["w_out"])),
        w_up=c(W["ffn_w_up"]),
        w_down=c(W["ffn_w_down"]),
        sc_w=_rows8(W["sc_conv_w"]),
        ssd_w=_rows8(W["ssd_conv_w"]),
        ffn_w=_rows8(W["ffn_conv_w"]),
    )


def _layer_weights(KW, l):
    return {k: (v[l] if k in ("sc_w", "ssd_w", "ffn_w") else (v, l)) for k, v in KW.items()}


def _local_step(x, positions, target, W, S, ex=None):
    T = x.shape[0]
    tm = min(256, T)
    tm_ffn = min(FFN_ROWS, T)
    cosf, sina, sinb = _rope_tables(positions)
    if ex is None:
        KW = _kernel_weights(W)
    else:
        gathered = _all_gather_weights(ex.shard(0))
    saved = []
    xl = x
    for l in range(DEPTH):
        lw = _layer_weights(KW, l) if ex is None else _kernel_weights(ex.weights(gathered))
        g_pre = S["norm_mix_pre"][l].reshape(1, -1)
        g_post = S["norm_mix_post"][l].reshape(1, -1)
        g_fpre = S["norm_ffn_pre"][l].reshape(1, -1)
        g_fpost = S["norm_ffn_post"][l].reshape(1, -1)
        qn = S["mla_q_norm"][l].reshape(1, -1)
        kvn = S["mla_kv_norm"][l].reshape(1, -1)
        ssd_b = S["ssd_conv_b"][l].reshape(1, -1)
        ssd_par = _row8(jnp.pad(S["ssd_dt_bias"][l], (0, LANE - SSD_HEADS)), jnp.pad(S["ssd_a_log"][l], (0, LANE - SSD_HEADS)),
                        jnp.pad(S["ssd_d"][l], (0, LANE - SSD_HEADS)))
        ssd_nw = S["ssd_norm"][l].reshape(1, -1)
        ffn_b = S["ffn_conv_b"][l].reshape(1, -1)

        (h1,) = _rows(lambda i, n, *v: _f_premix(*v), T, tm, [_cur(xl)], [_cst(g_pre)], [_out(D_MODEL, BF16)], [], "pre_mix_norm")
        zin = _mm(h1, lw["w_in"], "nt", F32, "mm_in")
        qlat, kvlat = _rows(lambda i, n, *v: _f_mla_pre(*v), T, tm, [_cur(zin, Q_LORA, 0), _cur(zin, KV_LORA, Z_CKV // KV_LORA)],
                            [_cst(qn), _cst(kvn)], [_out(Q_LORA, BF16), _out(KV_LORA, BF16)], [], "mla_pre_norm")
        qpad = _mm(qlat, lw["w_q"], "nn", F32, "mm_q_up")
        kvpad = _mm(kvlat, lw["w_kv"], "nn", BF16, "mm_kv_up")
        qr, kr = _rows(_k_rope_fwd, T, tm, [_cur(qpad), _cur(kvpad, HEADS * HP, 0), _cur(zin, LANE, Z_KR // LANE),
                                            _cur(cosf), _cur(sina), _cur(sinb)], [],
                       [_out(HEADS * HP, BF16), _out(HEADS * HP, BF16)], [], "mla_rope")
        if ex is not None and l + 1 < DEPTH:
            o, gathered = _flash_fwd(qr, kr, kvpad, T, carry=ex.shard(l + 1))
        else:
            o = _flash_fwd(qr, kr, kvpad, T)
        (yconv,) = _rows(_k_sconv_fwd, T, tm, [_cur(zin, SC_DIM, Z_SCB // SC_DIM), _cur(zin, SC_DIM, Z_SCC // SC_DIM),
                                               _cur(zin, SC_DIM, Z_SCH // SC_DIM), _halo(zin, "prev", SC_DIM, Z_SCC // SC_DIM),
                                               _halo(zin, "prev", SC_DIM, Z_SCH // SC_DIM)], [_cst(lw["sc_w"])],
                         [_out(SC_DIM, F32)], [], "short_conv_fwd")
        (xbc,) = _rows(_k_ssdconv_fwd, T, tm, [_cur(zin, SSD_CONV_DIM, Z_XBC // SSD_CONV_DIM),
                                               _halo(zin, "prev", SSD_CONV_DIM, Z_XBC // SSD_CONV_DIM)],
                       [_cst(lw["ssd_w"]), _cst(ssd_b)], [_out(SSD_CONV_DIM, F32)], [], "ssd_conv_fwd")
        dtraw = zin[:, Z_DT:Z_DT + LANE]
        yscan, states = _ssd_fwd(xbc, dtraw, ssd_par, T)
        (yssd,) = _rows(lambda i, n, *v: _f_ssd_gate(*v), T, tm, [_cur(yscan), _cur(zin, SSD_DIM, Z_SSZ // SSD_DIM)], [_cst(ssd_nw)],
                        [_out(SSD_DIM, F32)], [], "ssd_gate_fwd")
        ycat = jnp.concatenate([o.astype(BF16), yconv.astype(BF16), yssd.astype(BF16)], axis=1)
        mixed = _mm(ycat, lw["w_out"], "nn", F32, "mm_out")
        x1, h2 = _rows(lambda i, n, *v: _f_post_mix(*v), T, tm, [_cur(xl), _cur(mixed)], [_cst(g_post), _cst(g_fpre)],
                       [_out(D_MODEL, F32), _out(D_MODEL, BF16)], [], "post_mix_fwd")
        upre = _mm(h2, lw["w_up"], "nn", F32, "mm_up")
        nt = FFN_DIM // FFN_TILE
        gcol, ucol = (lambda j: j), (lambda j: j + nt)
        (act,) = _rows(_k_ffnact_fwd, T, tm_ffn,
                       [(upre, FFN_TILE, gcol, "cur"), (upre, FFN_TILE, ucol, "cur"), (upre, FFN_TILE, gcol, "prev"),
                        (upre, FFN_TILE, ucol, "prev")],
                       [(lw["ffn_w"], FFN_TILE, gcol), (lw["ffn_w"], FFN_TILE, ucol), (ffn_b, FFN_TILE, gcol), (ffn_b, FFN_TILE, ucol)],
                       [(FFN_DIM, BF16, FFN_TILE, gcol)], [], "ffn_act_fwd", ncol=nt)
        dn = _mm(act, lw["w_down"], "nn", F32, "mm_down")
        (x2,) = _rows(lambda i, n, *v: _f_post_ffn(*v), T, tm, [_cur(x1), _cur(dn)], [_cst(g_fpost)], [_out(D_MODEL, F32)], [], "post_ffn_fwd")
        saved.append(dict(lw=lw, x=xl, h1=h1, zin=zin, qlat=qlat, kvlat=kvlat, qr=qr, kr=kr, kvpad=kvpad, o=o, xbc=xbc, dtraw=dtraw,
                          yscan=yscan, states=states, ycat=ycat, mixed=mixed, x1=x1, h2=h2, upre=upre, act=act, dn=dn,
                          g_pre=g_pre, g_post=g_post, g_fpre=g_fpre, g_fpost=g_fpost, qn=qn, kvn=kvn, ssd_b=ssd_b,
                          ssd_par=ssd_par, ssd_nw=ssd_nw, ffn_b=ffn_b))
        xl = x2

    gx, loss_part = _rows(_k_loss, T, tm, [_cur(xl), _cur(target)], [], [_out(D_MODEL, F32)], [_acc(1, LANE)], "loss_head")

    GW = {k: [None] * DEPTH for k in ("w_in", "mla_w_q_up", "mla_w_kv_up", "sc_conv_w", "ssd_conv_w", "w_out", "ffn_w_up",
                                      "ffn_conv_w", "ffn_w_down")}
    GS = {k: [None] * DEPTH for k in ("norm_mix_pre", "norm_mix_post", "norm_ffn_pre", "norm_ffn_post", "mla_q_norm", "mla_kv_norm",
                                      "ssd_conv_b", "ssd_dt_bias", "ssd_a_log", "ssd_d", "ssd_norm", "ffn_conv_b")}
    nt = FFN_DIM // FFN_TILE
    gcol, ucol = (lambda j: j), (lambda j: j + nt)
    pending = None
    for l in reversed(range(DEPTH)):
        s = saved[l]
        lw = s["lw"]
        gx1, ddn, dgf = _rows_vjp(_f_post_ffn, T, tm, [s["x1"], s["dn"]], [s["g_fpost"]], [gx], [F32, BF16], "post_ffn_bwd")
        GS["norm_ffn_post"][l] = dgf[0]
        dact = _mm(ddn, lw["w_down"], "nt", F32, "mm_down_dx")
        GW["ffn_w_down"][l] = _mm(s["act"], ddn, "tn", BF16, "mm_down_dw")
        up = s["upre"]
        dug, duu, dwg, dwu, dbg, dbu = _rows(
            _k_ffnact_bwd, T, tm_ffn,
            [(up, FFN_TILE, gcol, "cur"), (up, FFN_TILE, ucol, "cur"), (dact, FFN_TILE, gcol, "cur"), (up, FFN_TILE, gcol, "prev"),
             (up, FFN_TILE, ucol, "prev"), (up, FFN_TILE, gcol, "next"), (up, FFN_TILE, ucol, "next"), (dact, FFN_TILE, gcol, "next")],
            [(lw["ffn_w"], FFN_TILE, gcol), (lw["ffn_w"], FFN_TILE, ucol), (s["ffn_b"], FFN_TILE, gcol), (s["ffn_b"], FFN_TILE, ucol)],
            [(FFN_DIM, BF16, FFN_TILE, gcol)] * 2,
            [(HALO, FFN_DIM, FFN_TILE, gcol)] * 2 + [(1, FFN_DIM, FFN_TILE, gcol)] * 2, "ffn_act_bwd", ncol=nt)
        GW["ffn_conv_w"][l] = jnp.concatenate([dwg[:3], dwu[:3]], axis=1)
        GS["ffn_conv_b"][l] = jnp.concatenate([dbg[0], dbu[0]])
        dh2 = _mm((dug, duu), lw["w_up"], "nt", F32, "mm_up_dx")
        GW["ffn_w_up"][l] = (_mm(s["h2"], dug, "tn", BF16, "mm_up_dw_gate"), _mm(s["h2"], duu, "tn", BF16, "mm_up_dw_up"))
        gx0, dmixed, dgp, dgf = _rows_vjp(_f_post_mix, T, tm, [s["x"], s["mixed"]], [s["g_post"], s["g_fpre"]], [gx1, dh2],
                                          [F32, BF16], "post_mix_bwd")
        GS["norm_mix_post"][l], GS["norm_ffn_pre"][l] = dgp[0], dgf[0]
        dycat = _mm(dmixed, lw["w_out"], "nt", F32, "mm_out_dx")
        GW["w_out"][l] = _unpad_out_rows(_mm(s["ycat"], dmixed, "tn", BF16, "mm_out_dw"))
        zin = s["zin"]
        dyscan, dz, dnw = _rows(_vjp_wrap(_f_ssd_gate, 2, 1), T, tm,
                                [_cur(s["yscan"]), _cur(zin, SSD_DIM, Z_SSZ // SSD_DIM), _cur(dycat, SSD_DIM, (HEADS * HP + SC_DIM) // SSD_DIM)],
                                [_cst(s["ssd_nw"])], [_out(SSD_DIM, F32), _out(SSD_DIM, BF16)], [_acc(1, SSD_DIM)], "ssd_gate_bwd")
        GS["ssd_norm"][l] = dnw[0]
        dxbc, ddtraw, dpar = _ssd_bwd(s["xbc"], s["dtraw"], s["ssd_par"], s["states"], dyscan, T)
        GS["ssd_dt_bias"][l], GS["ssd_a_log"][l], GS["ssd_d"][l] = dpar[0, :SSD_HEADS], dpar[1, :SSD_HEADS], dpar[2, :SSD_HEADS]
        xb = Z_XBC // SSD_CONV_DIM
        dxraw, dsw, dsb = _rows(_k_ssdconv_bwd, T, tm,
                                [_cur(zin, SSD_CONV_DIM, xb), _cur(dxbc), _halo(zin, "prev", SSD_CONV_DIM, xb),
                                 _halo(zin, "next", SSD_CONV_DIM, xb), _halo(dxbc, "next")],
                                [_cst(lw["ssd_w"]), _cst(s["ssd_b"])], [_out(SSD_CONV_DIM, BF16)],
                                [_acc(HALO, SSD_CONV_DIM), _acc(1, SSD_CONV_DIM)], "ssd_conv_bwd")
        GW["ssd_conv_w"][l] = dsw[:4]
        GS["ssd_conv_b"][l] = dsb[0]
        cb = (HEADS * HP) // SC_DIM
        dscb, dscc, dsch, dscw = _rows(_k_sconv_bwd, T, tm,
                                       [_cur(zin, SC_DIM, Z_SCB // SC_DIM), _cur(zin, SC_DIM, Z_SCC // SC_DIM),
                                        _cur(zin, SC_DIM, Z_SCH // SC_DIM), _cur(dycat, SC_DIM, cb),
                                        _halo(zin, "prev", SC_DIM, Z_SCC // SC_DIM), _halo(zin, "prev", SC_DIM, Z_SCH // SC_DIM),
                                        _halo(zin, "next", SC_DIM, Z_SCB // SC_DIM), _halo(dycat, "next", SC_DIM, cb)],
                                       [_cst(lw["sc_w"])], [_out(SC_DIM, BF16)] * 3, [_acc(HALO, SC_DIM)], "short_conv_bwd")
        GW["sc_conv_w"][l] = dscw[:3]
        if pending is not None:
            dq, dk, dv, parts = _flash_bwd(s["qr"], s["kr"], s["kvpad"], s["o"], dycat, T, carry=pending)
            ex.collect(l + 1, parts)
        else:
            dq, dk, dv = _flash_bwd(s["qr"], s["kr"], s["kvpad"], s["o"], dycat, T)
        dqpad, dkvpad, dkr = _rows(_k_rope_bwd, T, tm, [_cur(dq), _cur(dk), _cur(dv), _cur(cosf), _cur(sina), _cur(sinb)], [],
                                   [_out(HEADS * HP, BF16), _out(2 * HEADS * HP, BF16), _out(LANE, BF16)], [], "mla_rope_bwd")
        dqlat = _mm(dqpad, lw["w_q"], "nt", F32, "mm_q_dx")
        GW["mla_w_q_up"][l] = _unpad_heads(_mm(s["qlat"], dqpad, "tn", BF16, "mm_q_dw"), NOPE + ROPE)
        dkvlat = _mm(dkvpad, lw["w_kv"], "nt", F32, "mm_kv_dx")
        GW["mla_w_kv_up"][l] = _unpad_kv(_mm(s["kvlat"], dkvpad, "tn", BF16, "mm_kv_dw"))
        dcq, dckv, dqn, dkvn = _rows(_vjp_wrap(_f_mla_pre, 2, 2), T, tm,
                                     [_cur(zin, Q_LORA, 0), _cur(zin, KV_LORA, Z_CKV // KV_LORA), _cur(dqlat), _cur(dkvlat)],
                                     [_cst(s["qn"]), _cst(s["kvn"])], [_out(Q_LORA, BF16), _out(KV_LORA, BF16)],
                                     [_acc(1, Q_LORA), _acc(1, KV_LORA)], "mla_pre_bwd")
        GS["mla_q_norm"][l], GS["mla_kv_norm"][l] = dqn[0], dkvn[0]
        dzin = jnp.concatenate([dcq, dckv, dkr, dscb, dscc, dsch, dz, dxraw, ddtraw.astype(BF16), jnp.zeros((T, ZIN - Z_DT - LANE), BF16)], axis=1)
        dh1 = _mm(dzin, lw["w_in"], "nn", F32, "mm_in_dx")
        GW["w_in"][l] = _unpad_rows_in(_mm(dzin, s["h1"], "tn", BF16, "mm_in_dw"))
        gx, dgp = _rows(_vjp_wrap(_f_premix, 1, 1, add_first=True), T, tm, [_cur(s["x"]), _cur(dh1), _cur(gx0)], [_cst(s["g_pre"])],
                        [_out(D_MODEL, F32)], [_acc(1, D_MODEL)], "pre_mix_bwd")
        GS["norm_mix_pre"][l] = dgp[0]
        if ex is not None:
            pending = ex.submit({k: v[l] for k, v in GW.items()})
    if ex is not None:
        ex.collect(0, _rs_chip_exchange(pending))
    GS = {k: jnp.stack(v) for k, v in GS.items()}
    return loss_part[0, 0], gx, GW, GS


WEIGHTS = ("norm_mix_pre", "norm_mix_post", "norm_ffn_pre", "norm_ffn_post", "w_in", "mla_q_norm", "mla_w_q_up", "mla_kv_norm",
           "mla_w_kv_up", "sc_conv_w", "ssd_conv_w", "ssd_conv_b", "ssd_dt_bias", "ssd_a_log", "ssd_d", "ssd_norm", "w_out",
           "ffn_w_up", "ffn_conv_w", "ffn_conv_b", "ffn_w_down")
SHARDED = (("w_in", 2), ("mla_w_q_up", 2), ("mla_w_kv_up", 2), ("sc_conv_w", 2), ("ssd_conv_w", 2), ("w_out", 1),
           ("ffn_w_up", 2), ("ffn_conv_w", 2), ("ffn_w_down", 1))
SMALL = tuple(n for n in WEIGHTS if n not in dict(SHARDED))
N_CHIPS = 4
N_DEV = 8
ROW_ALIGN = 64
SLAB_ALIGN = 16
MAIN = ("ffn_w_down", "w_out", "w_in", "mla_w_q_up", "mla_w_kv_up", "sc_conv_w", "ssd_conv_w")
WIDE = ("ffn_w_up", "ffn_conv_w")
TRANSPOSED = ("w_in",)


def _is_rows(shape, width):
    return shape[-1] == width and math.prod(shape[:-1]) % SLAB_ALIGN == 0


def _is_short(shape, width):
    return len(shape) == 2 and shape[1] == width and not _is_rows(shape, width)


def _slab_rows(shape, width):
    if _is_rows(shape, width):
        return math.prod(shape[:-1])
    if _is_short(shape, width):
        return -(-shape[0] // SLAB_ALIGN) * SLAB_ALIGN
    return -(-math.prod(shape) // (width * SLAB_ALIGN)) * SLAB_ALIGN


def _slab(piece, width, dtype, lead=0):
    ld, shape = piece.shape[:lead], piece.shape[lead:]
    rows = _slab_rows(shape, width)
    if _is_rows(shape, width):
        return piece.astype(dtype).reshape(ld + (rows, width))
    if _is_short(shape, width):
        return jnp.pad(piece.astype(dtype), [(0, 0)] * lead + [(0, rows - shape[0]), (0, 0)])
    flat = piece.astype(dtype).reshape(ld + (-1,))
    return jnp.pad(flat, [(0, 0)] * lead + [(0, rows * width - flat.shape[-1])]).reshape(ld + (rows, width))


def _unslab(slab, shape, lead=0):
    ld = slab.shape[:lead]
    if _is_rows(shape, slab.shape[-1]):
        return slab.reshape(ld + tuple(shape))
    if _is_short(shape, slab.shape[-1]):
        return slab[..., :shape[0], :]
    return slab.reshape(ld + (-1,))[..., :math.prod(shape)].reshape(ld + tuple(shape))


def _layout(shapes, per_layer):
    out = {}
    for buf, names in (("main", MAIN), ("wide", WIDE)):
        width = PACK_COLS if buf == "main" else shapes["ffn_w_up"][-1]
        ents, off = [], 0
        for n in names:
            shp = tuple(shapes[n])
            if n.endswith("conv_w"):
                todo = [(None, False, shp), (None, True, shp)]
            elif per_layer or _is_short(shp[1:], width):
                todo = [(l, False, shp[1:]) for l in range(shp[0])]
            else:
                todo = [(None, False, shp)]
            for l, lo, ps in todo:
                r = _slab_rows(ps, width)
                ents.append((n, l, lo, ps, off, r))
                off += r
        out[buf] = (width, -(-off // ROW_ALIGN) * ROW_ALIGN, ents)
    return out


def _pack(layout, piece, dtype, lead=0):
    width, rows, ents = layout
    slabs, ld = [], None
    for n, l, lo, ps, off, r in ents:
        p = piece(n, l, lo)
        slabs.append(None if p is None else _slab(p, width, dtype, lead))
        ld = ld if p is None else p.shape[:lead]
    used = ents[-1][4] + ents[-1][5]
    slabs = [jnp.zeros(ld + (e[5], width), dtype) if s is None else s for s, e in zip(slabs, ents)]
    if rows > used:
        slabs.append(jnp.zeros(ld + (rows - used, width), dtype))
    return jnp.concatenate(slabs, axis=lead)


ANY = pl.BlockSpec(memory_space=pl.ANY)


def _pos():
    return lax.axis_index("x"), lax.axis_index("y"), lax.axis_index("c")


def _other_chips(x, y):
    return ((1 - x, y), (x, 1 - y), (1 - x, 1 - y))


def _remote(src, dst, ssem, rsem, dev):
    return pltpu.make_async_remote_copy(src_ref=src, dst_ref=dst, send_sem=ssem, recv_sem=rsem, device_id=dev, device_id_type=MESH)


AG_CHUNKS = 2


def _chip_index():
    return 2 * lax.axis_index("x") + lax.axis_index("y")


def _ag_sems(nbuf):
    return [pltpu.SemaphoreType.DMA((nbuf * 3 * AG_CHUNKS,))] * 4


def _ag_plan(w_refs, out_refs, sems):
    isend, irecv, dsend, drecv = sems
    x, y, c = _pos()
    k = 2 * x + y
    sib = (x, y, 1 - c)
    sends, lands, forwards, finals = [], [], [], []
    s = 0
    for w_ref, out_ref in zip(w_refs, out_refs):
        H = w_ref.shape[0] // 2
        CH = H // AG_CHUNKS
        for cx, cy in _other_chips(x, y):
            for ch in range(AG_CHUNKS):
                mine = out_ref.at[k, pl.ds(c * H + ch * CH, CH), :]
                near = out_ref.at[2 * cx + cy, pl.ds(c * H + ch * CH, CH), :]
                far = out_ref.at[2 * cx + cy, pl.ds((1 - c) * H + ch * CH, CH), :]
                sends.append(_remote(w_ref.at[pl.ds(c * H + ch * CH, CH), :], mine, isend.at[s], irecv.at[s], (cx, cy, c)))
                lands.append(_remote(near, near, isend.at[s], irecv.at[s], (cx, cy, c)))
                forwards.append(_remote(near, near, dsend.at[s], drecv.at[s], sib))
                finals.append(_remote(far, far, dsend.at[s], drecv.at[s], sib))
                s += 1
    return sends, lands, forwards, finals


def _own_slot(got, own):
    return lax.dynamic_update_slice(got, own[None], (_chip_index(), 0, 0))


def _all_gather_weights(ws):
    nb = len(ws)

    def body(*refs):
        sends, lands, forwards, finals = _ag_plan(refs[:nb], refs[nb:2 * nb], refs[2 * nb:])
        for cp in sends:
            cp.start()
        for land, fw in zip(lands, forwards):
            land.wait_recv()
            fw.start()
        for cp in finals:
            cp.wait_recv()
        for cp in sends + forwards:
            cp.wait_send()

    got = pl.pallas_call(
        body, name="all_gather_weights", in_specs=[ANY] * nb, out_specs=[ANY] * nb,
        out_shape=[jax.ShapeDtypeStruct((N_CHIPS,) + w.shape, w.dtype) for w in ws], scratch_shapes=_ag_sems(nb),
    )(*ws)
    return [_own_slot(g, w) for g, w in zip(got, ws)]


def _rs_pair_exchange(gs):
    nb = len(gs)

    def body(*refs):
        g_refs, got_refs, (ssem, rsem) = refs[:nb], refs[nb:2 * nb], refs[2 * nb:]
        x, y, c = _pos()
        cps = []
        for b, (g_ref, got_ref) in enumerate(zip(g_refs, got_refs)):
            H = g_ref.shape[1] // 2
            for kk in range(N_CHIPS):
                s = b * N_CHIPS + kk
                cps.append(_remote(g_ref.at[kk, pl.ds((1 - c) * H, H), :], got_ref.at[kk], ssem.at[s], rsem.at[s], (x, y, 1 - c)))
        for cp in cps:
            cp.start()
        for cp in cps:
            cp.wait()

    return pl.pallas_call(
        body, name="rs_pair_exchange", in_specs=[ANY] * nb, out_specs=[ANY] * nb,
        out_shape=[jax.ShapeDtypeStruct((N_CHIPS, g.shape[1] // 2, g.shape[2]), g.dtype) for g in gs],
        scratch_shapes=[pltpu.SemaphoreType.DMA((nb * N_CHIPS,))] * 2,
    )(*gs)


def _chip_sems(nbuf):
    return [pltpu.SemaphoreType.DMA((nbuf * 3,))] * 2


def _chip_plan(p_refs, out_refs, sems):
    ssem, rsem = sems
    x, y, c = _pos()
    sends, lands = [], []
    s = 0
    for p_ref, out_ref in zip(p_refs, out_refs):
        for cx, cy in _other_chips(x, y):
            sends.append(_remote(p_ref.at[2 * cx + cy], out_ref.at[2 * x + y], ssem.at[s], rsem.at[s], (cx, cy, c)))
            land = out_ref.at[2 * cx + cy]
            lands.append(_remote(land, land, ssem.at[s], rsem.at[s], (cx, cy, c)))
            s += 1
    return sends, lands


def _chip_parts(got, ps):
    k = _chip_index()
    return [lax.dynamic_update_slice(g, lax.dynamic_slice_in_dim(p, k, 1, axis=0), (k, 0, 0)) for g, p in zip(got, ps)]


def _rs_chip_exchange(ps):
    nb = len(ps)

    def body(*refs):
        sends, lands = _chip_plan(refs[:nb], refs[nb:2 * nb], refs[2 * nb:])
        for cp in sends:
            cp.start()
        for cp in lands:
            cp.wait_recv()
        for cp in sends:
            cp.wait_send()

    got = pl.pallas_call(
        body, name="rs_chip_exchange", in_specs=[ANY] * nb, out_specs=[ANY] * nb,
        out_shape=[jax.ShapeDtypeStruct(p.shape, p.dtype) for p in ps], scratch_shapes=_chip_sems(nb),
    )(*ps)
    return _chip_parts(got, ps)


def _rs_pair_share(fs):
    nb = len(fs)

    def body(*refs):
        f_refs, out_refs, (ssem, rsem) = refs[:nb], refs[nb:2 * nb], refs[2 * nb:]
        x, y, c = _pos()
        sends, lands = [], []
        for b, (f_ref, out_ref) in enumerate(zip(f_refs, out_refs)):
            sends.append(_remote(f_ref, out_ref.at[c], ssem.at[b], rsem.at[b], (x, y, 1 - c)))
            land = out_ref.at[1 - c]
            lands.append(_remote(land, land, ssem.at[b], rsem.at[b], (x, y, 1 - c)))
        for cp in sends:
            cp.start()
        for cp in lands:
            cp.wait_recv()
        for cp in sends:
            cp.wait_send()

    got = pl.pallas_call(
        body, name="rs_pair_share", in_specs=[ANY] * nb, out_specs=[ANY] * nb,
        out_shape=[jax.ShapeDtypeStruct((2,) + f.shape, f.dtype) for f in fs],
        scratch_shapes=[pltpu.SemaphoreType.DMA((nb,))] * 2,
    )(*fs)
    return [lax.dynamic_update_slice(g, f[None], (lax.axis_index("c"), 0, 0)) for g, f in zip(got, fs)]


def _all_reduce_small(s):
    r, C = s.shape

    def body(s_ref, o_ref, buf, ssem, rsem):
        x, y, c = _pos()
        me = 4 * x + 2 * y + c
        buf[me] = s_ref[...]
        cps = []
        for m in range(1, N_DEV):
            mx, my, mc = (m >> 2) & 1, (m >> 1) & 1, m & 1
            peer = (x ^ mx, y ^ my, c ^ mc)
            cp = _remote(s_ref, buf.at[me], ssem.at[m - 1], rsem.at[m - 1], peer)
            cp.start()
            cps.append(cp)
        for m in range(1, N_DEV):
            mx, my, mc = (m >> 2) & 1, (m >> 1) & 1, m & 1
            src = 4 * (x ^ mx) + 2 * (y ^ my) + (c ^ mc)
            _remote(s_ref, buf.at[src], ssem.at[m - 1], rsem.at[m - 1], (x ^ mx, y ^ my, c ^ mc)).wait_recv()
        for cp in cps:
            cp.wait_send()
        acc = buf[0]
        for j in range(1, N_DEV):
            acc = acc + buf[j]
        o_ref[...] = acc

    return pl.pallas_call(
        body, name="all_reduce_small", in_specs=[pl.BlockSpec(memory_space=pltpu.VMEM)],
        out_specs=pl.BlockSpec(memory_space=pltpu.VMEM), out_shape=jax.ShapeDtypeStruct((r, C), F32),
        scratch_shapes=[pltpu.VMEM((N_DEV, r, C), F32), pltpu.SemaphoreType.DMA((N_DEV - 1,)), pltpu.SemaphoreType.DMA((N_DEV - 1,))],
    )(s)


def _rtile(n, pref):
    if n <= pref:
        return n
    t = (pref // 16) * 16
    while t >= 16:
        if n % t == 0:
            return t
        t -= 16
    raise ValueError(f"no row tile for {n}")


def _rs_pair_sums(gpks):
    gots = _rs_pair_exchange(gpks)
    out = []
    for gpk, got in zip(gpks, gots):
        _, R, C = gpk.shape
        H = R // 2
        own = lax.dynamic_index_in_dim(gpk.reshape(N_CHIPS, 2, H, C), lax.axis_index("c"), axis=1, keepdims=False)
        (part,) = _rows(lambda i, n, a, b: (a.astype(F32) + b.astype(F32),), N_CHIPS * H, _rtile(N_CHIPS * H, 512),
                        [_cur(own.reshape(N_CHIPS * H, C)), _cur(got.reshape(N_CHIPS * H, C))], [], [_out(C, BF16)], [], "rs_pair_add")
        out.append(part.reshape(N_CHIPS, H, C))
    return out


def _rs_chip_sums(parts):
    def add4(i, n, a, b, c, d):
        return (((a.astype(F32) + b.astype(F32)) + c.astype(F32)) + d.astype(F32),)

    out = []
    for p in parts:
        _, H, C = p.shape
        tm = _rtile(H, 1024)
        (red,) = _rows(add4, H, tm, [(p.reshape(N_CHIPS * H, C), C, functools.partial(_const, v=0), j * (H // tm)) for j in range(N_CHIPS)],
                       [], [_out(C, F32)], [], "rs_chip_add")
        out.append(red)
    return out


class _Exchange:
    def __init__(self, a):
        self.a = a
        self.axis = {n: (1 if n in TRANSPOSED else ax) for n, ax in SHARDED}
        shapes = {n: (1,) + tuple(self.packed(n, a[n]).shape[1:]) for n in self.axis}
        self.layout = _layout(shapes, per_layer=True)
        self.reduced = [None] * DEPTH

    @staticmethod
    def packed(n, w):
        return jnp.swapaxes(w, -1, -2) if n in TRANSPOSED else w

    def shard(self, l):
        def piece(n, li, lo):
            w = self.packed(n, self.a[n][l:l + 1] if li is None else self.a[n][l])
            return w - w.astype(BF16).astype(F32) if lo else w
        return [_pack(lay, piece, BF16) for lay in self.layout.values()]

    def weights(self, gathered):
        W, resid = {}, {}
        for (width, rows, ents), g in zip(self.layout.values(), gathered):
            for n, li, lo, ps, off, r in ents:
                parts = _unslab(g[:, off:off + r], ps, lead=1)
                ax = self.axis[n] + (1 if li is None else 0)
                full = jnp.moveaxis(parts, 0, ax - 1)
                full = full.reshape(full.shape[:ax - 1] + (-1,) + full.shape[ax + 1:])
                (resid if lo else W)[n] = full[0] if li is None else full
        for n in resid:
            W[n] = W[n].astype(F32) + resid[n].astype(F32)
        return W

    def submit(self, GW):
        def by_chip(g, ax, parts=N_CHIPS):
            g = g.reshape(g.shape[:ax] + (parts, g.shape[ax] // parts) + g.shape[ax + 1:])
            return jnp.moveaxis(g, ax, 0)

        def piece(n, li, lo):
            if lo:
                return None
            g = GW[n]
            if isinstance(g, tuple):
                return jnp.concatenate([by_chip(h, self.axis[n] - 1, N_CHIPS // 2) for h in g])
            return by_chip(g[None], self.axis[n]) if li is None else by_chip(g, self.axis[n] - 1)

        return _rs_pair_sums([_pack(lay, piece, BF16, lead=1) for lay in self.layout.values()])

    def collect(self, l, parts):
        self.reduced[l] = _rs_chip_sums(parts)

    def finish(self):
        both = _rs_pair_share([f for fs in self.reduced for f in fs])
        nb = len(self.layout)
        grads = {}
        for b, (width, rows, ents) in enumerate(self.layout.values()):
            for n, li, lo, ps, off, r in ents:
                if not lo:
                    per_layer = [self.packed(n, _unslab(both[l * nb + b].reshape(rows, width)[off:off + r], ps)) for l in range(DEPTH)]
                    grads[n] = jnp.concatenate(per_layer) if li is None else jnp.stack(per_layer)
        return grads


def _adam(w, g, m, v, name, g_row=0):
    shp = w.shape
    two = lambda a: a.reshape(-1, shp[-1])
    rows = math.prod(shp[:-1])
    tm = _rtile(rows, 256)
    assert g_row % tm == 0
    g_in = (two(g), shp[-1], functools.partial(_const, v=0), g_row // tm)
    res = _rows(_k_adam, rows, tm, [_cur(two(w)), g_in, _cur(two(m)), _cur(two(v))], [], [_out(shp[-1], F32)] * 4, [], name)
    return tuple(r.reshape(shp) for r in res)


def _pack_flat(parts, rows):
    flat = jnp.concatenate([p.astype(F32).reshape(-1) for p in parts])
    return jnp.pad(flat, (0, rows * PACK_COLS - flat.shape[0])).reshape(rows, PACK_COLS)


def _unpack_flat(buf, shapes):
    flat, out, off = buf.reshape(-1), [], 0
    for shp in shapes:
        n = math.prod(shp)
        out.append(flat[off:off + n].reshape(shp))
        off += n
    return out


def kernel(x, positions, norm_mix_pre, norm_mix_post, norm_ffn_pre, norm_ffn_post, w_in, mla_q_norm, mla_w_q_up, mla_kv_norm, mla_w_kv_up, sc_conv_w, ssd_conv_w, ssd_conv_b, ssd_dt_bias, ssd_a_log, ssd_d, ssd_norm, w_out, ffn_w_up, ffn_conv_w, ffn_conv_b, ffn_w_down, loss_target, m_norm_mix_pre, m_norm_mix_post, m_norm_ffn_pre, m_norm_ffn_post, m_w_in, m_mla_q_norm, m_mla_w_q_up, m_mla_kv_norm, m_mla_w_kv_up, m_sc_conv_w, m_ssd_conv_w, m_ssd_conv_b, m_ssd_dt_bias, m_ssd_a_log, m_ssd_d, m_ssd_norm, m_w_out, m_ffn_w_up, m_ffn_conv_w, m_ffn_conv_b, m_ffn_w_down, v_norm_mix_pre, v_norm_mix_post, v_norm_ffn_pre, v_norm_ffn_post, v_w_in, v_mla_q_norm, v_mla_w_q_up, v_mla_kv_norm, v_mla_w_kv_up, v_sc_conv_w, v_ssd_conv_w, v_ssd_conv_b, v_ssd_dt_bias, v_ssd_a_log, v_ssd_d, v_ssd_norm, v_w_out, v_ffn_w_up, v_ffn_conv_w, v_ffn_conv_b, v_ffn_w_down):
    a = dict(locals())
    ex = _Exchange(a)
    S = {n: a[n] for n in SMALL}
    loss_part, gx, _, GS = _local_step(a["x"][0], a["positions"][0], a["loss_target"][0], None, S, ex)

    grads, delta, new_m, new_v = {}, {}, {}, {}
    for n, g in ex.finish().items():
        grads[n], delta[n], new_m[n], new_v[n] = _adam(a[n], g, a["m_" + n], a["v_" + n], "adamw_" + n)

    small_shapes = [a[n].shape for n in SMALL]
    rs = -(-(sum(math.prod(s) for s in small_shapes) + 1) // (PACK_COLS * SLAB_ALIGN)) * SLAB_ALIGN
    red = _all_reduce_small(_pack_flat([GS[n] for n in SMALL] + [loss_part.reshape(1)], rs))
    loss = _unpack_flat(red, small_shapes + [(1,)])[-1][0]
    pk = lambda pre: _pack_flat([a[pre + n] for n in SMALL], rs)
    for dst, buf in zip((grads, delta, new_m, new_v), _adam(pk(""), red, pk("m_"), pk("v_"), "adamw_small")):
        dst.update(zip(SMALL, _unpack_flat(buf, small_shapes)))

    return (loss, gx[None], *[grads[n] for n in WEIGHTS], *[delta[n] for n in WEIGHTS], *[new_m[n] for n in WEIGHTS],
            *[new_v[n] for n in WEIGHTS])
```

```python
import functools
import math

import jax
import jax.numpy as jnp
from jax import lax
from jax.experimental import pallas as pl
from jax.experimental.pallas import tpu as pltpu

F32 = jnp.float32
BF16 = jnp.bfloat16
MXU_DTYPE = jnp.bfloat16
HIGHEST = lax.Precision.HIGHEST
MESH = pl.DeviceIdType.MESH

D_MODEL = 1024
DEPTH = 4
HEADS = 8
Q_LORA = 256
KV_LORA = 128
NOPE = 64
ROPE = 32
VDIM = 64
ROPE_THETA = 10000.0
SC_DIM = 256
SSD_HEADS = 4
SSD_HEAD_DIM = 64
SSD_STATE = 128
SSD_DIM = 256
SSD_CONV_DIM = 768
SSD_CHUNK = 128
FFN_DIM = 2816
NORM_EPS = 1e-6
QK_SCALE = (NOPE + ROPE) ** -0.5
LANE = 128
HP = 128
FLASH_HEADS = 2

ZIN = 2560
Z_CQ, Z_CKV, Z_KR, Z_SCB, Z_SCC, Z_SCH, Z_SSZ, Z_XBC, Z_DT = 0, 256, 384, 512, 768, 1024, 1280, 1536, 2304
KR_LANE = 64
YCAT = HEADS * HP + SC_DIM + SSD_DIM
FFN_TILE = 256
FFN_ROWS = 1024

ADAM_LR, ADAM_B1, ADAM_B2, ADAM_EPS, ADAM_WD, ADAM_STEP = 0.001, 0.9, 0.999, 1e-08, 0.01, 10

PACK_COLS = 1024


def _tile(n, pref):
    if n <= pref:
        return n
    t = (pref // LANE) * LANE
    while t >= LANE:
        if n % t == 0:
            return t
        t -= LANE
    raise ValueError(f"no tile for {n}")


MM_TM, MM_TN, MM_TK = 1024, 1408, 1536


def _mm(a, b, mode, out_dtype, name, tm=None, tn=MM_TN, tkmax=MM_TK):
    pair = isinstance(a, tuple)
    a_list = list(a) if pair else [a]
    layer = None
    if isinstance(b, tuple):
        b, layer = b
    bshape = b.shape[-2:]
    if mode == "nn":
        (M, Ka), (_, N) = a_list[0].shape, bshape
    elif mode == "nt":
        (M, Ka), (N, _) = a_list[0].shape, bshape
    else:
        (Ka, M), (_, N) = a_list[0].shape, bshape
    tm = (MM_TN if mode == "tn" else MM_TM) if tm is None else tm
    tm, tn, tk = _tile(M, tm), _tile(N, tn), _tile(Ka, tkmax)
    nka = Ka // tk
    nk = nka * len(a_list)

    def bspec(shape, index):
        if layer is None:
            return pl.BlockSpec(shape, index)
        return pl.BlockSpec((None,) + shape, lambda i, j, k: (layer,) + index(i, j, k))

    if mode == "nn":
        a_specs = [pl.BlockSpec((tm, tk), lambda i, j, k: (i, jnp.minimum(k, nka - 1))),
                   pl.BlockSpec((tm, tk), lambda i, j, k: (i, jnp.maximum(k - nka, 0)))][:len(a_list)]
        b_spec = bspec((tk, tn), lambda i, j, k: (k, j))
        dims = NN
    elif mode == "nt":
        a_specs = [pl.BlockSpec((tm, tk), lambda i, j, k: (i, jnp.minimum(k, nka - 1))),
                   pl.BlockSpec((tm, tk), lambda i, j, k: (i, jnp.maximum(k - nka, 0)))][:len(a_list)]
        b_spec = bspec((tn, tk), lambda i, j, k: (j, k))
        dims = NT
    else:
        a_specs = [pl.BlockSpec((tk, tm), lambda i, j, k: (k, i))]
        b_spec = pl.BlockSpec((tk, tn), lambda i, j, k: (k, j))
        dims = TN
    na = len(a_list)

    def body(*refs):
        a_refs, b_ref, o_ref = refs[:na], refs[na], refs[na + 1]
        k = pl.program_id(2)

        def prod(a_ref):
            return lax.dot_general(a_ref[...].astype(MXU_DTYPE), b_ref[...].astype(MXU_DTYPE), dims, preferred_element_type=F32)

        if nk == 1:
            o_ref[...] = prod(a_refs[0]).astype(o_ref.dtype)
            return
        acc_ref = refs[na + 2]

        @pl.when(k == 0)
        def _():
            acc_ref[...] = prod(a_refs[0])

        @pl.when((k > 0) & (k < nka))
        def _():
            acc_ref[...] += prod(a_refs[0])

        if pair:
            @pl.when(k >= nka)
            def _():
                acc_ref[...] += prod(a_refs[1])

        @pl.when(k == nk - 1)
        def _():
            o_ref[...] = acc_ref[...].astype(o_ref.dtype)

    return pl.pallas_call(
        body, name=name, grid=(M // tm, N // tn, nk),
        in_specs=a_specs + [b_spec], out_specs=pl.BlockSpec((tm, tn), lambda i, j, k: (i, j)),
        out_shape=jax.ShapeDtypeStruct((M, N), out_dtype),
        scratch_shapes=[pltpu.VMEM((tm, tn), F32)] if nk > 1 else [],
        compiler_params=pltpu.CompilerParams(dimension_semantics=("parallel", "parallel", "arbitrary")),
    )(*a_list, b)


HALO = 8


def _const(j, v):
    return v


def _rows(fn, T, tm, ins, consts, outs, accs, name, ncol=1):
    n = T // tm
    hb = tm // HALO
    last = T // HALO - 1
    in_specs, args = [], []
    for arr, bc, cb, kind in ins:
        if isinstance(kind, int):
            in_specs.append(pl.BlockSpec((tm, bc), lambda j, i, cb=cb, off=kind: (i + off, cb(j))))
        elif kind == "cur":
            in_specs.append(pl.BlockSpec((tm, bc), lambda j, i, cb=cb: (i, cb(j))))
        elif kind == "prev":
            in_specs.append(pl.BlockSpec((HALO, bc), lambda j, i, cb=cb: (jnp.maximum(i * hb - 1, 0), cb(j))))
        else:
            in_specs.append(pl.BlockSpec((HALO, bc), lambda j, i, cb=cb: (jnp.minimum((i + 1) * hb, last), cb(j))))
        args.append(arr)
    for arr, bc, cb in consts:
        in_specs.append(pl.BlockSpec((arr.shape[0], bc), lambda j, i, cb=cb: (0, cb(j))))
        args.append(arr)
    out_specs, out_shape = [], []
    for tc, dt, bc, cb in outs:
        out_specs.append(pl.BlockSpec((tm, bc), lambda j, i, cb=cb: (i, cb(j))))
        out_shape.append(jax.ShapeDtypeStruct((T, tc), dt))
    for r, tc, bc, cb in accs:
        out_specs.append(pl.BlockSpec((r, bc), lambda j, i, cb=cb: (0, cb(j))))
        out_shape.append(jax.ShapeDtypeStruct((r, tc), F32))
    nin, nout, nacc = len(args), len(outs), len(accs)

    def body(*refs):
        i = pl.program_id(1)
        res = fn(i, n, *[r[...] for r in refs[:nin]])
        for r, v in zip(refs[nin:nin + nout], res[:nout]):
            r[...] = v.astype(r.dtype)
        if nacc:
            acc_refs = refs[nin + nout:nin + nout + nacc]

            @pl.when(i == 0)
            def _():
                for r in acc_refs:
                    r[...] = jnp.zeros_like(r)

            for r, v in zip(acc_refs, res[nout:]):
                r[...] += v.astype(F32)

    res = pl.pallas_call(
        body, name=name, grid=(ncol, n), in_specs=in_specs, out_specs=out_specs, out_shape=out_shape,
        compiler_params=pltpu.CompilerParams(dimension_semantics=("arbitrary", "arbitrary")),
    )(*args)
    return res


def _cur(arr, bc=None, blk=0):
    bc = arr.shape[1] if bc is None else bc
    return (arr, bc, functools.partial(_const, v=blk), "cur")


def _halo(arr, kind, bc=None, blk=0):
    bc = arr.shape[1] if bc is None else bc
    return (arr, bc, functools.partial(_const, v=blk), kind)


def _cst(arr):
    return (arr, arr.shape[1], functools.partial(_const, v=0))


def _out(cols, dt):
    return (cols, dt, cols, functools.partial(_const, v=0))


def _acc(rows, cols):
    return (rows, cols, cols, functools.partial(_const, v=0))


def _rms(x, w):
    return x * lax.rsqrt(jnp.mean(x * x, axis=-1, keepdims=True) + NORM_EPS) * w


def _sigmoid(x):
    return 0.5 * jnp.tanh(0.5 * x) + 0.5


def _silu(x):
    return x * _sigmoid(x)


def _dsilu(x):
    s = _sigmoid(x)
    return s * (1.0 + x * (1.0 - s))


def _softplus(x):
    return jnp.maximum(x, 0.0) + jnp.log1p(jnp.exp(-jnp.abs(x)))


def _shift(a, k):
    return pltpu.roll(a, k % a.shape[0], 0)


def _lroll(a, k):
    return pltpu.roll(a, k % a.shape[1], 1)


def _vjp_wrap(f, nrow, nconst, add_first=False):
    def g(i, n, *vals):
        rows, consts, mid = vals[:nrow], vals[len(vals) - nconst:], vals[nrow:len(vals) - nconst]
        cots = mid[:-1] if add_first else mid
        outs, pull = jax.vjp(f, *rows, *consts)
        grads = list(pull(tuple(c.astype(o.dtype) for c, o in zip(cots, outs))))
        if add_first:
            grads[0] = grads[0] + mid[-1]
        return tuple(grads)
    return g


def _rows_vjp(f, T, tm, rows, consts, cots, out_dtypes, name):
    return _rows(_vjp_wrap(f, len(rows), len(consts)), T, tm, [_cur(r) for r in rows] + [_cur(c) for c in cots],
                 [_cst(c) for c in consts], [_out(r.shape[1], dt) for r, dt in zip(rows, out_dtypes)],
                 [_acc(1, c.shape[1]) for c in consts], name)


def _f_premix(x, g):
    return (_rms(x, g),)


def _f_mla_pre(cq, ckv, qn, kvn):
    return _rms(cq, qn), _rms(ckv, kvn)


def _f_ssd_gate(y, z, nw):
    return (_rms(y * _silu(z), nw),)


def _f_post_mix(x, mixed, gpost, gffn):
    x1 = x + _rms(mixed, gpost)
    return x1, _rms(x1, gffn)


def _f_post_ffn(x1, d, gpost):
    return (x1 + _rms(d, gpost),)


def _rope_fwd(v, cosf, sina, sinb):
    return v * cosf + _lroll(v, -16) * sina + _lroll(v, 16) * sinb


def _rope_bwd(g, cosf, sina, sinb):
    return g * cosf + _lroll(g * sina, 16) + _lroll(g * sinb, -16)


def _k_rope_fwd(i, n, qpad, kvpad, kr, cosf, sina, sinb):
    qs, ks = [], []
    krr = _rope_fwd(kr, cosf, sina, sinb)
    for h in range(HEADS):
        sl = slice(h * HP, (h + 1) * HP)
        qs.append(_rope_fwd(qpad[:, sl], cosf, sina, sinb))
        ks.append(kvpad[:, sl].astype(F32) + krr)
    return jnp.concatenate(qs, axis=1), jnp.concatenate(ks, axis=1)


def _k_rope_bwd(i, n, dq, dk, dv, cosf, sina, sinb):
    lane = lax.broadcasted_iota(jnp.int32, (1, HP), 1)
    rmask = ((lane >= KR_LANE) & (lane < KR_LANE + ROPE)).astype(F32)
    dqs, dks = [], []
    dkr = jnp.zeros((dq.shape[0], HP), F32)
    for h in range(HEADS):
        sl = slice(h * HP, (h + 1) * HP)
        dqs.append(_rope_bwd(dq[:, sl], cosf, sina, sinb))
        dkh = dk[:, sl]
        dkr = dkr + dkh * rmask
        dks.append(dkh * (1.0 - rmask))
    dkr = _rope_bwd(dkr, cosf, sina, sinb) * rmask
    return jnp.concatenate(dqs, axis=1), jnp.concatenate(dks + [dv], axis=1), dkr


def _k_sconv_fwd(i, n, b, c, h, cp, hp, w):
    m = b.shape[0]
    up = jnp.where(i > 0, cp * hp, 0.0)
    ue = jnp.concatenate([up, c * h], axis=0)
    conv = w[2:3] * ue + w[1:2] * _shift(ue, 1) + w[0:1] * _shift(ue, 2)
    return (b * conv[HALO:],)


def _k_sconv_bwd(i, n, b, c, h, dy, cp, hp, bn, dyn, w):
    m = b.shape[0]
    up = jnp.where(i > 0, cp * hp, 0.0)
    ue = jnp.concatenate([up, c * h], axis=0)
    u1, u2 = _shift(ue, 1), _shift(ue, 2)
    conv = (w[2:3] * ue + w[1:2] * u1 + w[0:1] * u2)[HALO:]
    dc_cur = dy * b
    dce = jnp.concatenate([dc_cur, jnp.where(i < n - 1, dyn * bn, 0.0)], axis=0)
    du = (w[2:3] * dce + w[1:2] * _shift(dce, -1) + w[0:1] * _shift(dce, -2))[:m]
    dw = jnp.concatenate([
        jnp.sum(dc_cur * u2[HALO:], axis=0, keepdims=True),
        jnp.sum(dc_cur * u1[HALO:], axis=0, keepdims=True),
        jnp.sum(dc_cur * ue[HALO:], axis=0, keepdims=True),
        jnp.zeros((HALO - 3, b.shape[1]), F32)], axis=0)
    return dy * conv, du * h, du * c, dw


def _conv4(ue, w):
    return w[3:4] * ue + w[2:3] * _shift(ue, 1) + w[1:2] * _shift(ue, 2) + w[0:1] * _shift(ue, 3)


def _k_ssdconv_fwd(i, n, u, up, w, bias):
    ue = jnp.concatenate([jnp.where(i > 0, up, 0.0), u], axis=0)
    return (_silu(_conv4(ue, w)[HALO:] + bias),)


def _k_ssdconv_bwd(i, n, u, dout, up, un, doutn, w, bias):
    m = u.shape[0]
    ue = jnp.concatenate([jnp.where(i > 0, up, 0.0), u, un], axis=0)
    u1, u2, u3 = _shift(ue, 1), _shift(ue, 2), _shift(ue, 3)
    pre = (w[3:4] * ue + w[2:3] * u1 + w[1:2] * u2 + w[0:1] * u3)[HALO:] + bias
    doe = jnp.concatenate([dout, jnp.where(i < n - 1, doutn, 0.0)], axis=0)
    dpre = doe * _dsilu(pre)
    du = (w[3:4] * dpre + w[2:3] * _shift(dpre, -1) + w[1:2] * _shift(dpre, -2) + w[0:1] * _shift(dpre, -3))[:m]
    dp = dpre[:m]
    cur = slice(HALO, HALO + m)
    dw = jnp.concatenate([
        jnp.sum(dp * u3[cur], axis=0, keepdims=True),
        jnp.sum(dp * u2[cur], axis=0, keepdims=True),
        jnp.sum(dp * u1[cur], axis=0, keepdims=True),
        jnp.sum(dp * ue[cur], axis=0, keepdims=True),
        jnp.zeros((HALO - 4, u.shape[1]), F32)], axis=0)
    db = jnp.sum(dp, axis=0, keepdims=True)
    return du, dw, db


def _conv3(ue, w):
    return w[2:3] * ue + w[1:2] * _shift(ue, 1) + w[0:1] * _shift(ue, 2)


def _k_ffnact_fwd(i, n, ug, uu, ugp, uup, wg, wu, bg, bu):
    gate = _conv3(jnp.concatenate([jnp.where(i > 0, ugp, 0.0), ug], axis=0), wg)[HALO:] + bg
    upv = _conv3(jnp.concatenate([jnp.where(i > 0, uup, 0.0), uu], axis=0), wu)[HALO:] + bu
    return (_silu(gate) * upv,)


def _k_ffnact_bwd(i, n, ug, uu, dact, ugp, uup, ugn, uun, dactn, wg, wu, bg, bu):
    m = ug.shape[0]
    cur = slice(HALO, HALO + m)

    def taps(p, c, nx):
        e = jnp.concatenate([jnp.where(i > 0, p, 0.0), c, nx], axis=0)
        return e, _shift(e, 1), _shift(e, 2)

    def back(d, w):
        return (w[2:3] * d + w[1:2] * _shift(d, -1) + w[0:1] * _shift(d, -2))[:m]

    def wgrad(d, t):
        return jnp.concatenate([jnp.sum(d[:m] * t[2][cur], axis=0, keepdims=True), jnp.sum(d[:m] * t[1][cur], axis=0, keepdims=True),
                                jnp.sum(d[:m] * t[0][cur], axis=0, keepdims=True), jnp.zeros((HALO - 3, d.shape[1]), F32)], axis=0)

    tg, tu = taps(ugp, ug, ugn), taps(uup, uu, uun)
    gate = (wg[2:3] * tg[0] + wg[1:2] * tg[1] + wg[0:1] * tg[2])[HALO:] + bg
    upv = (wu[2:3] * tu[0] + wu[1:2] * tu[1] + wu[0:1] * tu[2])[HALO:] + bu
    dae = jnp.concatenate([dact, jnp.where(i < n - 1, dactn, 0.0)], axis=0)
    sg = _sigmoid(gate)
    dg = dae * upv * (sg * (1.0 + gate * (1.0 - sg)))
    dup = dae * (gate * sg)
    return (back(dg, wg), back(dup, wu), wgrad(dg, tg), wgrad(dup, tu),
            jnp.sum(dg[:m], axis=0, keepdims=True), jnp.sum(dup[:m], axis=0, keepdims=True))


def _k_loss(i, n, y, tgt):
    e = y - tgt
    part = 0.5 * jnp.sum(jnp.sum(e * e, axis=1, keepdims=True) / D_MODEL, axis=0, keepdims=True)
    return e * (1.0 / D_MODEL), jnp.broadcast_to(part, (1, LANE))


def _k_adam(i, n, w, g, m, v):
    m = ADAM_B1 * m + (1.0 - ADAM_B1) * g
    v = ADAM_B2 * v + (1.0 - ADAM_B2) * (g * g)
    m_hat = m / (1.0 - ADAM_B1 ** ADAM_STEP)
    v_hat = v / (1.0 - ADAM_B2 ** ADAM_STEP)
    delta = -ADAM_LR * (m_hat / (jnp.sqrt(v_hat) + ADAM_EPS) + ADAM_WD * w)
    return g, delta, m, v


def _dotf(a, b, dims):
    return lax.dot_general(a.astype(MXU_DTYPE), b.astype(MXU_DTYPE), dims, preferred_element_type=F32)


NN = (((1,), (0,)), ((), ()))
NT = (((1,), (1,)), ((), ()))
TN = (((0,), (0,)), ((), ()))


def _ssd_chunk(x0, x1, x2, x3, b0, b1, c0, c1, dtraw, p0, p1, p2, p3, dtb, alog, dsk):
    xs, bs, cs_, ps = (x0, x1, x2, x3), (b0, b1), (c0, c1), (p0, p1, p2, p3)
    L = dtraw.shape[0]
    dt = _softplus(dtraw + dtb)
    adt = dt * (-jnp.exp(alog))
    row = lax.broadcasted_iota(jnp.int32, (L, L), 0)
    col = lax.broadcasted_iota(jnp.int32, (L, L), 1)
    tril = row >= col
    cum = jnp.dot(tril.astype(F32), adt, precision=HIGHEST, preferred_element_type=F32)
    cum_t = cum.T
    lane = lax.broadcasted_iota(jnp.int32, (1, LANE), 1)
    sub = lax.broadcasted_iota(jnp.int32, (LANE, 1), 0)
    lastcol = (lax.broadcasted_iota(jnp.int32, (1, L), 1) == L - 1).astype(F32)
    ys, news = [], []
    for h in range(SSD_HEADS):
        g = h // (SSD_HEADS // 2)
        oh = (lane == h).astype(F32)
        dth = jnp.sum(dt * oh, axis=1, keepdims=True)
        csh = jnp.sum(cum * oh, axis=1, keepdims=True)
        csr = jnp.sum(cum_t * (sub == h).astype(F32), axis=0, keepdims=True)
        cl = jnp.sum(csr * lastcol, axis=1, keepdims=True)
        dskh = jnp.sum(dsk * oh, axis=1, keepdims=True)
        x, bm, cm, prev = xs[h], bs[g], cs_[g], ps[h]
        xdt = x * dth
        decay = jnp.exp(jnp.where(tril, csh - csr, -jnp.inf))
        scores = _dotf(cm, bm, NT) * decay
        y_diag = _dotf(scores, xdt, NN)
        bd = bm * jnp.exp(cl - csh)
        cst = _dotf(xdt, bd, TN)
        news.append(prev * jnp.exp(cl) + cst)
        y_off = _dotf(cm, prev, NT) * jnp.exp(csh)
        ys.append(y_diag + y_off + x * dskh)
    return (*ys, *news)


def _ssd_operands(x_ref, dt_ref, par_ref, prev):
    xs = [x_ref[:, h * SSD_HEAD_DIM:(h + 1) * SSD_HEAD_DIM] for h in range(SSD_HEADS)]
    bs = [x_ref[:, SSD_DIM + g * SSD_STATE:SSD_DIM + (g + 1) * SSD_STATE] for g in range(2)]
    cs_ = [x_ref[:, SSD_DIM + 2 * SSD_STATE + g * SSD_STATE:SSD_DIM + 2 * SSD_STATE + (g + 1) * SSD_STATE] for g in range(2)]
    return (*xs, *bs, *cs_, dt_ref[...], *prev, par_ref[0:1, :], par_ref[1:2, :], par_ref[2:3, :])


def _ssd_fwd(xbc, dtraw, par, T):
    L = SSD_CHUNK
    nc = T // L
    P = SSD_HEAD_DIM

    def body(x_ref, dt_ref, par_ref, y_ref, st_ref, state):
        @pl.when(pl.program_id(0) == 0)
        def _():
            state[...] = jnp.zeros_like(state)

        st_ref[0] = state[...]
        prev = [state[h * P:(h + 1) * P, :] for h in range(SSD_HEADS)]
        res = _ssd_chunk(*_ssd_operands(x_ref, dt_ref, par_ref, prev))
        for h in range(SSD_HEADS):
            y_ref[:, h * P:(h + 1) * P] = res[h]
            state[h * P:(h + 1) * P, :] = res[SSD_HEADS + h]

    return pl.pallas_call(
        body, name="ssd_scan_fwd", grid=(nc,),
        in_specs=[pl.BlockSpec((L, SSD_CONV_DIM), lambda c: (c, 0)), pl.BlockSpec((L, LANE), lambda c: (c, 0)),
                  pl.BlockSpec((8, LANE), lambda c: (0, 0))],
        out_specs=[pl.BlockSpec((L, SSD_DIM), lambda c: (c, 0)), pl.BlockSpec((1, SSD_DIM, SSD_STATE), lambda c: (c, 0, 0))],
        out_shape=[jax.ShapeDtypeStruct((T, SSD_DIM), F32), jax.ShapeDtypeStruct((nc, SSD_DIM, SSD_STATE), F32)],
        scratch_shapes=[pltpu.VMEM((SSD_DIM, SSD_STATE), F32)],
        compiler_params=pltpu.CompilerParams(dimension_semantics=("arbitrary",)),
    )(xbc, dtraw, par)


def _ssd_bwd(xbc, dtraw, par, states, dy, T):
    L = SSD_CHUNK
    nc = T // L
    P = SSD_HEAD_DIM

    def body(x_ref, dt_ref, par_ref, st_ref, dy_ref, dx_ref, ddt_ref, dpar_ref, dstate):
        @pl.when(pl.program_id(0) == 0)
        def _():
            dstate[...] = jnp.zeros_like(dstate)
            dpar_ref[...] = jnp.zeros_like(dpar_ref)

        prev = [st_ref[0, h * P:(h + 1) * P, :] for h in range(SSD_HEADS)]
        prim = _ssd_operands(x_ref, dt_ref, par_ref, prev)
        _, pull = jax.vjp(_ssd_chunk, *prim)
        cots = tuple(dy_ref[:, h * P:(h + 1) * P] for h in range(SSD_HEADS)) + tuple(
            dstate[h * P:(h + 1) * P, :] for h in range(SSD_HEADS))
        g = pull(cots)
        for h in range(SSD_HEADS):
            dx_ref[:, h * P:(h + 1) * P] = g[h]
            dstate[h * P:(h + 1) * P, :] = g[9 + h]
        for k in range(2):
            dx_ref[:, SSD_DIM + k * SSD_STATE:SSD_DIM + (k + 1) * SSD_STATE] = g[4 + k]
            dx_ref[:, SSD_DIM + 2 * SSD_STATE + k * SSD_STATE:SSD_DIM + 2 * SSD_STATE + (k + 1) * SSD_STATE] = g[6 + k]
        ddt_ref[...] = g[8]
        for r in range(3):
            dpar_ref[r:r + 1, :] += g[13 + r]

    rev = lambda c: (nc - 1 - c, 0)
    return pl.pallas_call(
        body, name="ssd_scan_bwd", grid=(nc,),
        in_specs=[pl.BlockSpec((L, SSD_CONV_DIM), rev), pl.BlockSpec((L, LANE), rev), pl.BlockSpec((8, LANE), lambda c: (0, 0)),
                  pl.BlockSpec((1, SSD_DIM, SSD_STATE), lambda c: (nc - 1 - c, 0, 0)), pl.BlockSpec((L, SSD_DIM), rev)],
        out_specs=[pl.BlockSpec((L, SSD_CONV_DIM), rev), pl.BlockSpec((L, LANE), rev), pl.BlockSpec((8, LANE), lambda c: (0, 0))],
        out_shape=[jax.ShapeDtypeStruct((T, SSD_CONV_DIM), F32), jax.ShapeDtypeStruct((T, LANE), F32),
                   jax.ShapeDtypeStruct((8, LANE), F32)],
        scratch_shapes=[pltpu.VMEM((SSD_DIM, SSD_STATE), F32)],
        compiler_params=pltpu.CompilerParams(dimension_semantics=("arbitrary",)),
    )(xbc, dtraw, par, states, dy)


def _causal_pairs(nq, by_query):
    if by_query:
        pairs = [(i, j) for i in range(nq) for j in range(i + 1)]
    else:
        pairs = [(i, j) for j in range(nq) for i in range(j, nq)]
    return jnp.asarray([p[0] for p in pairs], jnp.int32), jnp.asarray([p[1] for p in pairs], jnp.int32)


def _flash_fwd(q, k, kv, T, carry=()):
    tq = tk = min(512, T)
    nq = T // tq
    G = FLASH_HEADS
    rep = tk // HP
    nc = len(carry)
    qi, kj = _causal_pairs(nq, by_query=True)
    nh, nt = HEADS // G, qi.shape[0]

    def body(qi_ref, kj_ref, q_ref, k_ref, v_ref, *rest):
        w_refs, o_ref, g_refs = rest[:nc], rest[nc], rest[nc + 1:2 * nc + 1]
        m_ref, l_ref, acc_ref = rest[2 * nc + 1:2 * nc + 4]
        h, t = pl.program_id(0), pl.program_id(1)
        i, j = qi_ref[t], kj_ref[t]
        if nc:
            plan = lambda: _ag_plan(w_refs, g_refs, rest[2 * nc + 4:])

            @pl.when((h == 0) & (t == 0))
            def _():
                for cp in plan()[0]:
                    cp.start()

        @pl.when(j == 0)
        def _():
            m_ref[...] = jnp.full_like(m_ref, -jnp.inf)
            l_ref[...] = jnp.zeros_like(l_ref)
            acc_ref[...] = jnp.zeros_like(acc_ref)

        def step(diagonal):
            for g in range(G):
                sl = slice(g * HP, (g + 1) * HP)
                s = _dotf(q_ref[:, sl], k_ref[:, sl], NT) * QK_SCALE
                if diagonal:
                    rows = lax.broadcasted_iota(jnp.int32, (tq, tk), 0)
                    cols = lax.broadcasted_iota(jnp.int32, (tq, tk), 1)
                    s = jnp.where(rows >= cols, s, -jnp.inf)
                m_old = m_ref[:, sl]
                m_new = jnp.maximum(m_old, jnp.max(s, axis=1, keepdims=True))
                p = jnp.exp(s - jnp.tile(m_new, (1, rep)))
                alpha = jnp.exp(m_old - m_new)
                l_ref[:, sl] = alpha * l_ref[:, sl] + jnp.sum(p, axis=1, keepdims=True)
                acc_ref[:, sl] = alpha * acc_ref[:, sl] + _dotf(p, v_ref[:, sl], NN)
                m_ref[:, sl] = m_new

        @pl.when(j < i)
        def _():
            step(False)

        @pl.when(j == i)
        def _():
            step(True)
            lane = lax.broadcasted_iota(jnp.int32, (tq, HP), 1)
            for g in range(G):
                sl = slice(g * HP, (g + 1) * HP)
                l = l_ref[:, sl]
                o_ref[:, sl] = jnp.where(lane < VDIM, acc_ref[:, sl] / l, m_ref[:, sl] + jnp.log(l))

        if nc:
            @pl.when((h == nh - 1) & (t == 0))
            def _():
                _, lands, forwards, _ = plan()
                for land, fw in zip(lands, forwards):
                    land.wait_recv()
                    fw.start()

            @pl.when((h == nh - 1) & (t == nt - 1))
            def _():
                sends, _, forwards, finals = plan()
                for cp in finals:
                    cp.wait_recv()
                for cp in sends + forwards:
                    cp.wait_send()

    W = G * HP
    res = pl.pallas_call(
        body, name="mla_flash_fwd",
        grid_spec=pltpu.PrefetchScalarGridSpec(
            num_scalar_prefetch=2, grid=(nh, nt),
            in_specs=[pl.BlockSpec((tq, W), lambda h, t, qi, kj: (qi[t], h)),
                      pl.BlockSpec((tk, W), lambda h, t, qi, kj: (kj[t], h)),
                      pl.BlockSpec((tk, W), lambda h, t, qi, kj: (kj[t], HEADS // G + h))] + [ANY] * nc,
            out_specs=[pl.BlockSpec((tq, W), lambda h, t, qi, kj: (qi[t], h))] + [ANY] * nc,
            scratch_shapes=[pltpu.VMEM((tq, W), F32), pltpu.VMEM((tq, W), F32), pltpu.VMEM((tq, W), F32)] + (_ag_sems(nc) if nc else [])),
        out_shape=[jax.ShapeDtypeStruct((T, HEADS * HP), F32)] + [jax.ShapeDtypeStruct((N_CHIPS,) + w.shape, w.dtype) for w in carry],
        compiler_params=pltpu.CompilerParams(dimension_semantics=("arbitrary", "arbitrary")),
    )(qi, kj, q, k, kv, *carry)
    return res[0] if not nc else (res[0], [_own_slot(g, w) for g, w in zip(res[1:], carry)])


def _flash_bwd(q, k, kv, o, dycat, T, carry=()):
    tq = tk = min(512, T)
    nq = T // tq
    G = FLASH_HEADS
    nc = len(carry)
    qi, kj = _causal_pairs(nq, by_query=False)
    nh, nt = HEADS // G, qi.shape[0]

    def body(qi_ref, kj_ref, q_ref, k_ref, v_ref, o_ref, do_ref, *rest):
        p_refs, (dq_ref, dk_ref, dv_ref), part_refs = rest[:nc], rest[nc:nc + 3], rest[nc + 3:2 * nc + 3]
        h, t = pl.program_id(0), pl.program_id(1)
        i, j = qi_ref[t], kj_ref[t]
        if nc:
            plan = lambda: _chip_plan(p_refs, part_refs, rest[2 * nc + 3:])

            @pl.when((h == 0) & (t == 0))
            def _():
                for cp in plan()[0]:
                    cp.start()

        @pl.when(t == 0)
        def _():
            dq_ref[...] = jnp.zeros_like(dq_ref)

        @pl.when(i == j)
        def _():
            dk_ref[...] = jnp.zeros_like(dk_ref)
            dv_ref[...] = jnp.zeros_like(dv_ref)

        def step(diagonal):
            r0 = pl.multiple_of(i * tq, tq)
            for g in range(G):
                sl = slice(g * HP, (g + 1) * HP)
                qv, kv, vv, ov, dov = q_ref[:, sl], k_ref[:, sl], v_ref[:, sl], o_ref[:, sl], do_ref[:, sl]
                s = _dotf(qv, kv, NT) * QK_SCALE
                p = jnp.exp(s - ov[:, VDIM:VDIM + 1])
                if diagonal:
                    rows = lax.broadcasted_iota(jnp.int32, (tq, tk), 0)
                    cols = lax.broadcasted_iota(jnp.int32, (tq, tk), 1)
                    p = jnp.where(rows >= cols, p, 0.0)
                dsum = jnp.sum(dov * ov, axis=1, keepdims=True)
                dv_ref[:, sl] += _dotf(p, dov, TN)
                dp = _dotf(dov, vv, NT)
                ds = p * (dp - dsum) * QK_SCALE
                dk_ref[:, sl] += _dotf(ds, qv, TN)
                dq_ref[pl.ds(r0, tq), sl] += _dotf(ds, kv, NN)

        @pl.when(i > j)
        def _():
            step(False)

        @pl.when(i == j)
        def _():
            step(True)

        if nc:
            @pl.when((h == nh - 1) & (t == nt - 1))
            def _():
                sends, lands = plan()
                for cp in lands:
                    cp.wait_recv()
                for cp in sends:
                    cp.wait_send()

    W = G * HP
    qmap = lambda h, t, qi, kj: (qi[t], h)
    kmap = lambda h, t, qi, kj: (kj[t], h)
    vmap = lambda h, t, qi, kj: (kj[t], HEADS // G + h)
    res = pl.pallas_call(
        body, name="mla_flash_bwd",
        grid_spec=pltpu.PrefetchScalarGridSpec(
            num_scalar_prefetch=2, grid=(nh, nt),
            in_specs=[pl.BlockSpec((tq, W), qmap), pl.BlockSpec((tk, W), kmap), pl.BlockSpec((tk, W), vmap),
                      pl.BlockSpec((tq, W), qmap), pl.BlockSpec((tq, W), qmap)] + [ANY] * nc,
            out_specs=[pl.BlockSpec((T, W), lambda h, t, qi, kj: (0, h)), pl.BlockSpec((tk, W), kmap), pl.BlockSpec((tk, W), kmap)]
            + [ANY] * nc,
            scratch_shapes=_chip_sems(nc) if nc else []),
        out_shape=[jax.ShapeDtypeStruct((T, HEADS * HP), F32)] * 3 + [jax.ShapeDtypeStruct(p.shape, p.dtype) for p in carry],
        compiler_params=pltpu.CompilerParams(dimension_semantics=("arbitrary", "arbitrary")),
    )(qi, kj, q, k, kv, o, dycat, *carry)
    return tuple(res[:3]) if not nc else (*res[:3], _chip_parts(res[3:], carry))


_IN_SRC = (0, 256, 384, 416, 672, 928, 1184, 1440, 2208, 2212)
_IN_DST = (Z_CQ, Z_CKV, Z_KR + KR_LANE, Z_SCB, Z_SCC, Z_SCH, Z_SSZ, Z_XBC, Z_DT)


def _pad_rows_in(w):
    ax = w.ndim - 2

    def zeros(n):
        return jnp.zeros(w.shape[:ax] + (n,) + w.shape[ax + 1:], w.dtype)

    def whole_tiles(p):
        n = p.shape[ax]
        return p if n % SLAB_ALIGN == 0 else jnp.pad(p, [(0, 0)] * ax + [(0, -n % SLAB_ALIGN), (0, 0)])

    parts, at = [], 0
    for s0, s1, d0 in zip(_IN_SRC[:-1], _IN_SRC[1:], _IN_DST):
        if d0 > at:
            parts.append(zeros(d0 - at))
        parts.append(whole_tiles(lax.slice_in_dim(w, s0, s1, axis=ax)))
        at = d0 + parts[-1].shape[ax]
    parts.append(zeros(ZIN - at))
    return jnp.concatenate(parts, axis=ax)


def _unpad_rows_in(w):
    ax = w.ndim - 2
    groups = list(zip(_IN_SRC[:-1], _IN_SRC[1:], _IN_DST))
    parts = [lax.slice_in_dim(w, d0, d0 + -(-(s1 - s0) // SLAB_ALIGN) * SLAB_ALIGN, axis=ax) for s0, s1, d0 in groups]
    return lax.slice_in_dim(jnp.concatenate(parts, axis=ax), 0, _IN_SRC[-1], axis=ax)


def _pad_heads(w, width):
    w = w.reshape(w.shape[:-1] + (HEADS, width))
    w = jnp.pad(w, [(0, 0)] * (w.ndim - 1) + [(0, HP - width)])
    return w.reshape(w.shape[:-2] + (HEADS * HP,))


def _unpad_heads(w, width):
    w = w.reshape(w.shape[:-1] + (HEADS, HP))[..., :width]
    return w.reshape(w.shape[:-2] + (HEADS * width,))


def _pad_kv(w):
    w = w.reshape(w.shape[:-1] + (HEADS, NOPE + VDIM))
    return jnp.concatenate([_pad_heads(w[..., :NOPE].reshape(w.shape[:-2] + (HEADS * NOPE,)), NOPE),
                            _pad_heads(w[..., NOPE:].reshape(w.shape[:-2] + (HEADS * VDIM,)), VDIM)], axis=-1)


def _unpad_kv(w):
    k = _unpad_heads(w[..., :HEADS * HP], NOPE).reshape(w.shape[:-1] + (HEADS, NOPE))
    v = _unpad_heads(w[..., HEADS * HP:], VDIM).reshape(w.shape[:-1] + (HEADS, VDIM))
    return jnp.concatenate([k, v], axis=-1).reshape(w.shape[:-1] + (HEADS * (NOPE + VDIM),))


def _pad_out_rows(w):
    lead, d = w.shape[:-2], w.shape[-1]
    att = w[..., :HEADS * VDIM, :].reshape(lead + (HEADS, VDIM, d))
    att = jnp.pad(att, [(0, 0)] * (att.ndim - 2) + [(0, HP - VDIM), (0, 0)]).reshape(lead + (HEADS * HP, d))
    return jnp.concatenate([att, w[..., HEADS * VDIM:, :]], axis=-2)


def _unpad_out_rows(w):
    lead, d = w.shape[:-2], w.shape[-1]
    att = w[..., :HEADS * HP, :].reshape(lead + (HEADS, HP, d))[..., :VDIM, :].reshape(lead + (HEADS * VDIM, d))
    return jnp.concatenate([att, w[..., HEADS * HP:, :]], axis=-2)


def _rows8(w):
    return jnp.pad(w.astype(F32), [(0, 0)] * (w.ndim - 2) + [(0, 8 - w.shape[-2]), (0, 0)])


def _row8(*vecs):
    c = vecs[0].shape[-1]
    return jnp.concatenate([v.reshape(1, c).astype(F32) for v in vecs] + [jnp.zeros((8 - len(vecs), c), F32)], axis=0)


def _lanes(v):
    return jnp.pad(v.astype(F32), (0, LANE - v.shape[0])).reshape(1, LANE)


def _rope_tables(positions):
    inv_freq = 1.0 / (ROPE_THETA ** (jnp.arange(0, ROPE, 2, dtype=F32) / ROPE))
    ang = positions.astype(F32)[:, None] * inv_freq
    cos, sin = jnp.cos(ang), jnp.sin(ang)
    T = positions.shape[0]
    half = ROPE // 2
    one = jnp.ones((T, KR_LANE), F32)
    zero = jnp.zeros((T, KR_LANE), F32)
    tail1 = jnp.ones((T, HP - KR_LANE - ROPE), F32)
    tail0 = jnp.zeros((T, HP - KR_LANE - ROPE), F32)
    z16 = jnp.zeros((T, half), F32)
    cosf = jnp.concatenate([one, cos, cos, tail1], axis=1)
    sina = jnp.concatenate([zero, -sin, z16, tail0], axis=1)
    sinb = jnp.concatenate([zero, z16, sin, tail0], axis=1)
    return cosf, sina, sinb


def _kernel_weights(W):
    c = lambda a: a.astype(MXU_DTYPE)
    forms = dict(
        w_in=("w_in", lambda w: c(_pad_rows_in(w))),
        w_q=("mla_w_q_up", lambda w: c(_pad_heads(w, NOPE + ROPE))),
        w_kv=("mla_w_kv_up", lambda w: c(_pad_kv(w))),
        w_out=("w_out", lambda w: c(_pad_out_rows(w))),
        w_up=("ffn_w_up", c),
        w_down=("ffn_w_down", c),
        sc_w=("sc_conv_w", _rows8),
        ssd_w=("ssd_conv_w", _rows8),
        ffn_w=("ffn_conv_w", _rows8),
    )
    return {k: f(W[n]) for k, (n, f) in forms.items() if n in W}


def _layer_weights(KW, l):
    return {k: (v[l] if k in ("sc_w", "ssd_w", "ffn_w") else (v, l)) for k, v in KW.items()}


def _local_step(x, positions, target, W, S, ex=None):
    T = x.shape[0]
    tm = min(256, T)
    tm_ffn = min(FFN_ROWS, T)
    cosf, sina, sinb = _rope_tables(positions)
    if ex is None:
        KW = _kernel_weights(W)
    else:
        early = _all_gather_weights(ex.shard(0, "early"))
    saved = []
    xl = x
    for l in range(DEPTH):
        lw = _layer_weights(KW, l) if ex is None else _kernel_weights(ex.weights(early, "early"))
        g_pre = S["norm_mix_pre"][l].reshape(1, -1)
        g_post = S["norm_mix_post"][l].reshape(1, -1)
        g_fpre = S["norm_ffn_pre"][l].reshape(1, -1)
        g_fpost = S["norm_ffn_post"][l].reshape(1, -1)
        qn = S["mla_q_norm"][l].reshape(1, -1)
        kvn = S["mla_kv_norm"][l].reshape(1, -1)
        ssd_b = S["ssd_conv_b"][l].reshape(1, -1)
        ssd_par = _row8(jnp.pad(S["ssd_dt_bias"][l], (0, LANE - SSD_HEADS)), jnp.pad(S["ssd_a_log"][l], (0, LANE - SSD_HEADS)),
                        jnp.pad(S["ssd_d"][l], (0, LANE - SSD_HEADS)))
        ssd_nw = S["ssd_norm"][l].reshape(1, -1)
        ffn_b = S["ffn_conv_b"][l].reshape(1, -1)

        (h1,) = _rows(lambda i, n, *v: _f_premix(*v), T, tm, [_cur(xl)], [_cst(g_pre)], [_out(D_MODEL, BF16)], [], "pre_mix_norm")
        zin = _mm(h1, lw["w_in"], "nt", F32, "mm_in")
        qlat, kvlat = _rows(lambda i, n, *v: _f_mla_pre(*v), T, tm, [_cur(zin, Q_LORA, 0), _cur(zin, KV_LORA, Z_CKV // KV_LORA)],
                            [_cst(qn), _cst(kvn)], [_out(Q_LORA, BF16), _out(KV_LORA, BF16)], [], "mla_pre_norm")
        qpad = _mm(qlat, lw["w_q"], "nn", F32, "mm_q_up")
        kvpad = _mm(kvlat, lw["w_kv"], "nn", BF16, "mm_kv_up")
        qr, kr = _rows(_k_rope_fwd, T, tm, [_cur(qpad), _cur(kvpad, HEADS * HP, 0), _cur(zin, LANE, Z_KR // LANE),
                                            _cur(cosf), _cur(sina), _cur(sinb)], [],
                       [_out(HEADS * HP, BF16), _out(HEADS * HP, BF16)], [], "mla_rope")
        if ex is None:
            o = _flash_fwd(qr, kr, kvpad, T)
        else:
            nlate = len(ex.layouts["late"])
            o, got = _flash_fwd(qr, kr, kvpad, T, carry=ex.shard(l, "late") + (ex.shard(l + 1, "early") if l + 1 < DEPTH else []))
            lw.update(_kernel_weights(ex.weights(got[:nlate], "late")))
            early = got[nlate:]
        (yconv,) = _rows(_k_sconv_fwd, T, tm, [_cur(zin, SC_DIM, Z_SCB // SC_DIM), _cur(zin, SC_DIM, Z_SCC // SC_DIM),
                                               _cur(zin, SC_DIM, Z_SCH // SC_DIM), _halo(zin, "prev", SC_DIM, Z_SCC // SC_DIM),
                                               _halo(zin, "prev", SC_DIM, Z_SCH // SC_DIM)], [_cst(lw["sc_w"])],
                         [_out(SC_DIM, F32)], [], "short_conv_fwd")
        (xbc,) = _rows(_k_ssdconv_fwd, T, tm, [_cur(zin, SSD_CONV_DIM, Z_XBC // SSD_CONV_DIM),
                                               _halo(zin, "prev", SSD_CONV_DIM, Z_XBC // SSD_CONV_DIM)],
                       [_cst(lw["ssd_w"]), _cst(ssd_b)], [_out(SSD_CONV_DIM, F32)], [], "ssd_conv_fwd")
        dtraw = zin[:, Z_DT:Z_DT + LANE]
        yscan, states = _ssd_fwd(xbc, dtraw, ssd_par, T)
        (yssd,) = _rows(lambda i, n, *v: _f_ssd_gate(*v), T, tm, [_cur(yscan), _cur(zin, SSD_DIM, Z_SSZ // SSD_DIM)], [_cst(ssd_nw)],
                        [_out(SSD_DIM, F32)], [], "ssd_gate_fwd")
        ycat = jnp.concatenate([o.astype(BF16), yconv.astype(BF16), yssd.astype(BF16)], axis=1)
        mixed = _mm(ycat, lw["w_out"], "nn", F32, "mm_out")
        x1, h2 = _rows(lambda i, n, *v: _f_post_mix(*v), T, tm, [_cur(xl), _cur(mixed)], [_cst(g_post), _cst(g_fpre)],
                       [_out(D_MODEL, F32), _out(D_MODEL, BF16)], [], "post_mix_fwd")
        upre = _mm(h2, lw["w_up"], "nn", F32, "mm_up")
        nt = FFN_DIM // FFN_TILE
        gcol, ucol = (lambda j: j), (lambda j: j + nt)
        (act,) = _rows(_k_ffnact_fwd, T, tm_ffn,
                       [(upre, FFN_TILE, gcol, "cur"), (upre, FFN_TILE, ucol, "cur"), (upre, FFN_TILE, gcol, "prev"),
                        (upre, FFN_TILE, ucol, "prev")],
                       [(lw["ffn_w"], FFN_TILE, gcol), (lw["ffn_w"], FFN_TILE, ucol), (ffn_b, FFN_TILE, gcol), (ffn_b, FFN_TILE, ucol)],
                       [(FFN_DIM, BF16, FFN_TILE, gcol)], [], "ffn_act_fwd", ncol=nt)
        dn = _mm(act, lw["w_down"], "nn", F32, "mm_down")
        (x2,) = _rows(lambda i, n, *v: _f_post_ffn(*v), T, tm, [_cur(x1), _cur(dn)], [_cst(g_fpost)], [_out(D_MODEL, F32)], [], "post_ffn_fwd")
        saved.append(dict(lw=lw, x=xl, h1=h1, zin=zin, qlat=qlat, kvlat=kvlat, qr=qr, kr=kr, kvpad=kvpad, o=o, xbc=xbc, dtraw=dtraw,
                          yscan=yscan, states=states, ycat=ycat, mixed=mixed, x1=x1, h2=h2, upre=upre, act=act, dn=dn,
                          g_pre=g_pre, g_post=g_post, g_fpre=g_fpre, g_fpost=g_fpost, qn=qn, kvn=kvn, ssd_b=ssd_b,
                          ssd_par=ssd_par, ssd_nw=ssd_nw, ffn_b=ffn_b))
        xl = x2

    gx, loss_part = _rows(_k_loss, T, tm, [_cur(xl), _cur(target)], [], [_out(D_MODEL, F32)], [_acc(1, LANE)], "loss_head")

    GW = {k: [None] * DEPTH for k in ("w_in", "mla_w_q_up", "mla_w_kv_up", "sc_conv_w", "ssd_conv_w", "w_out", "ffn_w_up",
                                      "ffn_conv_w", "ffn_w_down")}
    GS = {k: [None] * DEPTH for k in ("norm_mix_pre", "norm_mix_post", "norm_ffn_pre", "norm_ffn_post", "mla_q_norm", "mla_kv_norm",
                                      "ssd_conv_b", "ssd_dt_bias", "ssd_a_log", "ssd_d", "ssd_norm", "ffn_conv_b")}
    nt = FFN_DIM // FFN_TILE
    gcol, ucol = (lambda j: j), (lambda j: j + nt)
    pending = None
    for l in reversed(range(DEPTH)):
        s = saved[l]
        lw = s["lw"]
        gx1, ddn, dgf = _rows_vjp(_f_post_ffn, T, tm, [s["x1"], s["dn"]], [s["g_fpost"]], [gx], [F32, BF16], "post_ffn_bwd")
        GS["norm_ffn_post"][l] = dgf[0]
        dact = _mm(ddn, lw["w_down"], "nt", F32, "mm_down_dx")
        GW["ffn_w_down"][l] = _mm(s["act"], ddn, "tn", BF16, "mm_down_dw")
        up = s["upre"]
        dug, duu, dwg, dwu, dbg, dbu = _rows(
            _k_ffnact_bwd, T, tm_ffn,
            [(up, FFN_TILE, gcol, "cur"), (up, FFN_TILE, ucol, "cur"), (dact, FFN_TILE, gcol, "cur"), (up, FFN_TILE, gcol, "prev"),
             (up, FFN_TILE, ucol, "prev"), (up, FFN_TILE, gcol, "next"), (up, FFN_TILE, ucol, "next"), (dact, FFN_TILE, gcol, "next")],
            [(lw["ffn_w"], FFN_TILE, gcol), (lw["ffn_w"], FFN_TILE, ucol), (s["ffn_b"], FFN_TILE, gcol), (s["ffn_b"], FFN_TILE, ucol)],
            [(FFN_DIM, BF16, FFN_TILE, gcol)] * 2,
            [(HALO, FFN_DIM, FFN_TILE, gcol)] * 2 + [(1, FFN_DIM, FFN_TILE, gcol)] * 2, "ffn_act_bwd", ncol=nt)
        GW["ffn_conv_w"][l] = jnp.concatenate([dwg[:3], dwu[:3]], axis=1)
        GS["ffn_conv_b"][l] = jnp.concatenate([dbg[0], dbu[0]])
        dh2 = _mm((dug, duu), lw["w_up"], "nt", F32, "mm_up_dx")
        GW["ffn_w_up"][l] = (_mm(s["h2"], dug, "tn", BF16, "mm_up_dw_gate"), _mm(s["h2"], duu, "tn", BF16, "mm_up_dw_up"))
        gx0, dmixed, dgp, dgf = _rows_vjp(_f_post_mix, T, tm, [s["x"], s["mixed"]], [s["g_post"], s["g_fpre"]], [gx1, dh2],
                                          [F32, BF16], "post_mix_bwd")
        GS["norm_mix_post"][l], GS["norm_ffn_pre"][l] = dgp[0], dgf[0]
        dycat = _mm(dmixed, lw["w_out"], "nt", F32, "mm_out_dx")
        GW["w_out"][l] = _unpad_out_rows(_mm(s["ycat"], dmixed, "tn", BF16, "mm_out_dw"))
        zin = s["zin"]
        dyscan, dz, dnw = _rows(_vjp_wrap(_f_ssd_gate, 2, 1), T, tm,
                                [_cur(s["yscan"]), _cur(zin, SSD_DIM, Z_SSZ // SSD_DIM), _cur(dycat, SSD_DIM, (HEADS * HP + SC_DIM) // SSD_DIM)],
                                [_cst(s["ssd_nw"])], [_out(SSD_DIM, F32), _out(SSD_DIM, BF16)], [_acc(1, SSD_DIM)], "ssd_gate_bwd")
        GS["ssd_norm"][l] = dnw[0]
        dxbc, ddtraw, dpar = _ssd_bwd(s["xbc"], s["dtraw"], s["ssd_par"], s["states"], dyscan, T)
        GS["ssd_dt_bias"][l], GS["ssd_a_log"][l], GS["ssd_d"][l] = dpar[0, :SSD_HEADS], dpar[1, :SSD_HEADS], dpar[2, :SSD_HEADS]
        xb = Z_XBC // SSD_CONV_DIM
        dxraw, dsw, dsb = _rows(_k_ssdconv_bwd, T, tm,
                                [_cur(zin, SSD_CONV_DIM, xb), _cur(dxbc), _halo(zin, "prev", SSD_CONV_DIM, xb),
                                 _halo(zin, "next", SSD_CONV_DIM, xb), _halo(dxbc, "next")],
                                [_cst(lw["ssd_w"]), _cst(s["ssd_b"])], [_out(SSD_CONV_DIM, BF16)],
                                [_acc(HALO, SSD_CONV_DIM), _acc(1, SSD_CONV_DIM)], "ssd_conv_bwd")
        GW["ssd_conv_w"][l] = dsw[:4]
        GS["ssd_conv_b"][l] = dsb[0]
        cb = (HEADS * HP) // SC_DIM
        dscb, dscc, dsch, dscw = _rows(_k_sconv_bwd, T, tm,
                                       [_cur(zin, SC_DIM, Z_SCB // SC_DIM), _cur(zin, SC_DIM, Z_SCC // SC_DIM),
                                        _cur(zin, SC_DIM, Z_SCH // SC_DIM), _cur(dycat, SC_DIM, cb),
                                        _halo(zin, "prev", SC_DIM, Z_SCC // SC_DIM), _halo(zin, "prev", SC_DIM, Z_SCH // SC_DIM),
                                        _halo(zin, "next", SC_DIM, Z_SCB // SC_DIM), _halo(dycat, "next", SC_DIM, cb)],
                                       [_cst(lw["sc_w"])], [_out(SC_DIM, BF16)] * 3, [_acc(HALO, SC_DIM)], "short_conv_bwd")
        GW["sc_conv_w"][l] = dscw[:3]
        if ex is None:
            dq, dk, dv = _flash_bwd(s["qr"], s["kr"], s["kvpad"], s["o"], dycat, T)
        else:
            late = ex.submit({n: GW[n][l] for ns in LATE for n in ns}, "late")
            dq, dk, dv, parts = _flash_bwd(s["qr"], s["kr"], s["kvpad"], s["o"], dycat, T, carry=late + (pending or []))
            ex.collect(l, "late", parts[:len(late)])
            if pending:
                ex.collect(l + 1, "early", parts[len(late):])
        dqpad, dkvpad, dkr = _rows(_k_rope_bwd, T, tm, [_cur(dq), _cur(dk), _cur(dv), _cur(cosf), _cur(sina), _cur(sinb)], [],
                                   [_out(HEADS * HP, BF16), _out(2 * HEADS * HP, BF16), _out(LANE, BF16)], [], "mla_rope_bwd")
        dqlat = _mm(dqpad, lw["w_q"], "nt", F32, "mm_q_dx")
        GW["mla_w_q_up"][l] = _unpad_heads(_mm(s["qlat"], dqpad, "tn", BF16, "mm_q_dw"), NOPE + ROPE)
        dkvlat = _mm(dkvpad, lw["w_kv"], "nt", F32, "mm_kv_dx")
        GW["mla_w_kv_up"][l] = _unpad_kv(_mm(s["kvlat"], dkvpad, "tn", BF16, "mm_kv_dw"))
        dcq, dckv, dqn, dkvn = _rows(_vjp_wrap(_f_mla_pre, 2, 2), T, tm,
                                     [_cur(zin, Q_LORA, 0), _cur(zin, KV_LORA, Z_CKV // KV_LORA), _cur(dqlat), _cur(dkvlat)],
                                     [_cst(s["qn"]), _cst(s["kvn"])], [_out(Q_LORA, BF16), _out(KV_LORA, BF16)],
                                     [_acc(1, Q_LORA), _acc(1, KV_LORA)], "mla_pre_bwd")
        GS["mla_q_norm"][l], GS["mla_kv_norm"][l] = dqn[0], dkvn[0]
        dzin = jnp.concatenate([dcq, dckv, dkr, dscb, dscc, dsch, dz, dxraw, ddtraw.astype(BF16), jnp.zeros((T, ZIN - Z_DT - LANE), BF16)], axis=1)
        dh1 = _mm(dzin, lw["w_in"], "nn", F32, "mm_in_dx")
        GW["w_in"][l] = _unpad_rows_in(_mm(dzin, s["h1"], "tn", BF16, "mm_in_dw"))
        gx, dgp = _rows(_vjp_wrap(_f_premix, 1, 1, add_first=True), T, tm, [_cur(s["x"]), _cur(dh1), _cur(gx0)], [_cst(s["g_pre"])],
                        [_out(D_MODEL, F32)], [_acc(1, D_MODEL)], "pre_mix_bwd")
        GS["norm_mix_pre"][l] = dgp[0]
        if ex is not None:
            pending = ex.submit({n: GW[n][l] for ns in EARLY for n in ns}, "early")
    if ex is not None:
        ex.collect(0, "early", _rs_chip_exchange(pending))
    GS = {k: jnp.stack(v) for k, v in GS.items()}
    return loss_part[0, 0], gx, GW, GS


WEIGHTS = ("norm_mix_pre", "norm_mix_post", "norm_ffn_pre", "norm_ffn_post", "w_in", "mla_q_norm", "mla_w_q_up", "mla_kv_norm",
           "mla_w_kv_up", "sc_conv_w", "ssd_conv_w", "ssd_conv_b", "ssd_dt_bias", "ssd_a_log", "ssd_d", "ssd_norm", "w_out",
           "ffn_w_up", "ffn_conv_w", "ffn_conv_b", "ffn_w_down")
SHARDED = (("w_in", 2), ("mla_w_q_up", 2), ("mla_w_kv_up", 2), ("sc_conv_w", 2), ("ssd_conv_w", 2), ("w_out", 1),
           ("ffn_w_up", 2), ("ffn_conv_w", 2), ("ffn_w_down", 1))
SMALL = tuple(n for n in WEIGHTS if n not in dict(SHARDED))
N_CHIPS = 4
N_DEV = 8
ROW_ALIGN = 64
SLAB_ALIGN = 16
EARLY = (("w_in", "mla_w_q_up", "mla_w_kv_up", "sc_conv_w", "ssd_conv_w"),)
LATE = (("ffn_w_down", "w_out"), ("ffn_w_up", "ffn_conv_w"))
TRANSPOSED = ("w_in",)


def _is_rows(shape, width):
    return shape[-1] == width and math.prod(shape[:-1]) % SLAB_ALIGN == 0


def _is_short(shape, width):
    return len(shape) == 2 and shape[1] == width and not _is_rows(shape, width)


def _slab_rows(shape, width):
    if _is_rows(shape, width):
        return math.prod(shape[:-1])
    if _is_short(shape, width):
        return -(-shape[0] // SLAB_ALIGN) * SLAB_ALIGN
    return -(-math.prod(shape) // (width * SLAB_ALIGN)) * SLAB_ALIGN


def _slab(piece, width, dtype, lead=0):
    ld, shape = piece.shape[:lead], piece.shape[lead:]
    rows = _slab_rows(shape, width)
    if _is_rows(shape, width):
        return piece.astype(dtype).reshape(ld + (rows, width))
    if _is_short(shape, width):
        return jnp.pad(piece.astype(dtype), [(0, 0)] * lead + [(0, rows - shape[0]), (0, 0)])
    flat = piece.astype(dtype).reshape(ld + (-1,))
    return jnp.pad(flat, [(0, 0)] * lead + [(0, rows * width - flat.shape[-1])]).reshape(ld + (rows, width))


def _unslab(slab, shape, lead=0):
    ld = slab.shape[:lead]
    if _is_rows(shape, slab.shape[-1]):
        return slab.reshape(ld + tuple(shape))
    if _is_short(shape, slab.shape[-1]):
        return slab[..., :shape[0], :]
    return slab.reshape(ld + (-1,))[..., :math.prod(shape)].reshape(ld + tuple(shape))


def _layout(shapes, names, width):
    ents, off = [], 0
    for n in names:
        shp = tuple(shapes[n])
        todo = [(None, False, shp), (None, True, shp)] if n.endswith("conv_w") else [(l, False, shp[1:]) for l in range(shp[0])]
        for l, lo, ps in todo:
            r = _slab_rows(ps, width)
            ents.append((n, l, lo, ps, off, r))
            off += r
    return width, -(-off // ROW_ALIGN) * ROW_ALIGN, ents


def _pack(layout, piece, dtype, lead=0):
    width, rows, ents = layout
    slabs, ld = [], None
    for n, l, lo, ps, off, r in ents:
        p = piece(n, l, lo)
        slabs.append(None if p is None else _slab(p, width, dtype, lead))
        ld = ld if p is None else p.shape[:lead]
    used = ents[-1][4] + ents[-1][5]
    slabs = [jnp.zeros(ld + (e[5], width), dtype) if s is None else s for s, e in zip(slabs, ents)]
    if rows > used:
        slabs.append(jnp.zeros(ld + (rows - used, width), dtype))
    return jnp.concatenate(slabs, axis=lead)


ANY = pl.BlockSpec(memory_space=pl.ANY)


def _pos():
    return lax.axis_index("x"), lax.axis_index("y"), lax.axis_index("c")


def _other_chips(x, y):
    return ((1 - x, y), (x, 1 - y), (1 - x, 1 - y))


def _remote(src, dst, ssem, rsem, dev):
    return pltpu.make_async_remote_copy(src_ref=src, dst_ref=dst, send_sem=ssem, recv_sem=rsem, device_id=dev, device_id_type=MESH)


AG_CHUNKS = 2


def _chip_index():
    return 2 * lax.axis_index("x") + lax.axis_index("y")


def _ag_sems(nbuf):
    return [pltpu.SemaphoreType.DMA((nbuf * 3 * AG_CHUNKS,))] * 4


def _ag_plan(w_refs, out_refs, sems):
    isend, irecv, dsend, drecv = sems
    x, y, c = _pos()
    k = 2 * x + y
    sib = (x, y, 1 - c)
    sends, lands, forwards, finals = [], [], [], []
    s = 0
    for w_ref, out_ref in zip(w_refs, out_refs):
        H = w_ref.shape[0] // 2
        CH = H // AG_CHUNKS
        for cx, cy in _other_chips(x, y):
            for ch in range(AG_CHUNKS):
                mine = out_ref.at[k, pl.ds(c * H + ch * CH, CH), :]
                near = out_ref.at[2 * cx + cy, pl.ds(c * H + ch * CH, CH), :]
                far = out_ref.at[2 * cx + cy, pl.ds((1 - c) * H + ch * CH, CH), :]
                sends.append(_remote(w_ref.at[pl.ds(c * H + ch * CH, CH), :], mine, isend.at[s], irecv.at[s], (cx, cy, c)))
                lands.append(_remote(near, near, isend.at[s], irecv.at[s], (cx, cy, c)))
                forwards.append(_remote(near, near, dsend.at[s], drecv.at[s], sib))
                finals.append(_remote(far, far, dsend.at[s], drecv.at[s], sib))
                s += 1
    return sends, lands, forwards, finals


def _own_slot(got, own):
    return lax.dynamic_update_slice(got, own[None], (_chip_index(), 0, 0))


def _all_gather_weights(ws):
    nb = len(ws)

    def body(*refs):
        sends, lands, forwards, finals = _ag_plan(refs[:nb], refs[nb:2 * nb], refs[2 * nb:])
        for cp in sends:
            cp.start()
        for land, fw in zip(lands, forwards):
            land.wait_recv()
            fw.start()
        for cp in finals:
            cp.wait_recv()
        for cp in sends + forwards:
            cp.wait_send()

    got = pl.pallas_call(
        body, name="all_gather_weights", in_specs=[ANY] * nb, out_specs=[ANY] * nb,
        out_shape=[jax.ShapeDtypeStruct((N_CHIPS,) + w.shape, w.dtype) for w in ws], scratch_shapes=_ag_sems(nb),
    )(*ws)
    return [_own_slot(g, w) for g, w in zip(got, ws)]


def _rs_pair_exchange(gs):
    nb = len(gs)

    def body(*refs):
        g_refs, got_refs, (ssem, rsem) = refs[:nb], refs[nb:2 * nb], refs[2 * nb:]
        x, y, c = _pos()
        cps = []
        for b, (g_ref, got_ref) in enumerate(zip(g_refs, got_refs)):
            H = g_ref.shape[1] // 2
            for kk in range(N_CHIPS):
                s = b * N_CHIPS + kk
                cps.append(_remote(g_ref.at[kk, pl.ds((1 - c) * H, H), :], got_ref.at[kk], ssem.at[s], rsem.at[s], (x, y, 1 - c)))
        for cp in cps:
            cp.start()
        for cp in cps:
            cp.wait()

    return pl.pallas_call(
        body, name="rs_pair_exchange", in_specs=[ANY] * nb, out_specs=[ANY] * nb,
        out_shape=[jax.ShapeDtypeStruct((N_CHIPS, g.shape[1] // 2, g.shape[2]), g.dtype) for g in gs],
        scratch_shapes=[pltpu.SemaphoreType.DMA((nb * N_CHIPS,))] * 2,
    )(*gs)


def _chip_sems(nbuf):
    return [pltpu.SemaphoreType.DMA((nbuf * 3,))] * 2


def _chip_plan(p_refs, out_refs, sems):
    ssem, rsem = sems
    x, y, c = _pos()
    sends, lands = [], []
    s = 0
    for p_ref, out_ref in zip(p_refs, out_refs):
        for cx, cy in _other_chips(x, y):
            sends.append(_remote(p_ref.at[2 * cx + cy], out_ref.at[2 * x + y], ssem.at[s], rsem.at[s], (cx, cy, c)))
            land = out_ref.at[2 * cx + cy]
            lands.append(_remote(land, land, ssem.at[s], rsem.at[s], (cx, cy, c)))
            s += 1
    return sends, lands


def _chip_parts(got, ps):
    k = _chip_index()
    return [lax.dynamic_update_slice(g, lax.dynamic_slice_in_dim(p, k, 1, axis=0), (k, 0, 0)) for g, p in zip(got, ps)]


def _rs_chip_exchange(ps):
    nb = len(ps)

    def body(*refs):
        sends, lands = _chip_plan(refs[:nb], refs[nb:2 * nb], refs[2 * nb:])
        for cp in sends:
            cp.start()
        for cp in lands:
            cp.wait_recv()
        for cp in sends:
            cp.wait_send()

    got = pl.pallas_call(
        body, name="rs_chip_exchange", in_specs=[ANY] * nb, out_specs=[ANY] * nb,
        out_shape=[jax.ShapeDtypeStruct(p.shape, p.dtype) for p in ps], scratch_shapes=_chip_sems(nb),
    )(*ps)
    return _chip_parts(got, ps)


def _rs_pair_share(fs):
    nb = len(fs)

    def body(*refs):
        f_refs, out_refs, (ssem, rsem) = refs[:nb], refs[nb:2 * nb], refs[2 * nb:]
        x, y, c = _pos()
        sends, lands = [], []
        for b, (f_ref, out_ref) in enumerate(zip(f_refs, out_refs)):
            sends.append(_remote(f_ref, out_ref.at[c], ssem.at[b], rsem.at[b], (x, y, 1 - c)))
            land = out_ref.at[1 - c]
            lands.append(_remote(land, land, ssem.at[b], rsem.at[b], (x, y, 1 - c)))
        for cp in sends:
            cp.start()
        for cp in lands:
            cp.wait_recv()
        for cp in sends:
            cp.wait_send()

    got = pl.pallas_call(
        body, name="rs_pair_share", in_specs=[ANY] * nb, out_specs=[ANY] * nb,
        out_shape=[jax.ShapeDtypeStruct((2,) + f.shape, f.dtype) for f in fs],
        scratch_shapes=[pltpu.SemaphoreType.DMA((nb,))] * 2,
    )(*fs)
    return [lax.dynamic_update_slice(g, f[None], (lax.axis_index("c"), 0, 0)) for g, f in zip(got, fs)]


def _all_reduce_small(s):
    r, C = s.shape

    def body(s_ref, o_ref, buf, ssem, rsem):
        x, y, c = _pos()
        me = 4 * x + 2 * y + c
        buf[me] = s_ref[...]
        cps = []
        for m in range(1, N_DEV):
            mx, my, mc = (m >> 2) & 1, (m >> 1) & 1, m & 1
            peer = (x ^ mx, y ^ my, c ^ mc)
            cp = _remote(s_ref, buf.at[me], ssem.at[m - 1], rsem.at[m - 1], peer)
            cp.start()
            cps.append(cp)
        for m in range(1, N_DEV):
            mx, my, mc = (m >> 2) & 1, (m >> 1) & 1, m & 1
            src = 4 * (x ^ mx) + 2 * (y ^ my) + (c ^ mc)
            _remote(s_ref, buf.at[src], ssem.at[m - 1], rsem.at[m - 1], (x ^ mx, y ^ my, c ^ mc)).wait_recv()
        for cp in cps:
            cp.wait_send()
        acc = buf[0]
        for j in range(1, N_DEV):
            acc = acc + buf[j]
        o_ref[...] = acc

    return pl.pallas_call(
        body, name="all_reduce_small", in_specs=[pl.BlockSpec(memory_space=pltpu.VMEM)],
        out_specs=pl.BlockSpec(memory_space=pltpu.VMEM), out_shape=jax.ShapeDtypeStruct((r, C), F32),
        scratch_shapes=[pltpu.VMEM((N_DEV, r, C), F32), pltpu.SemaphoreType.DMA((N_DEV - 1,)), pltpu.SemaphoreType.DMA((N_DEV - 1,))],
    )(s)


def _rtile(n, pref):
    if n <= pref:
        return n
    t = (pref // 16) * 16
    while t >= 16:
        if n % t == 0:
            return t
        t -= 16
    raise ValueError(f"no row tile for {n}")


def _rs_pair_sums(gpks):
    gots = _rs_pair_exchange(gpks)
    out = []
    for gpk, got in zip(gpks, gots):
        _, R, C = gpk.shape
        H = R // 2
        own = lax.dynamic_index_in_dim(gpk.reshape(N_CHIPS, 2, H, C), lax.axis_index("c"), axis=1, keepdims=False)
        (part,) = _rows(lambda i, n, a, b: (a.astype(F32) + b.astype(F32),), N_CHIPS * H, _rtile(N_CHIPS * H, 512),
                        [_cur(own.reshape(N_CHIPS * H, C)), _cur(got.reshape(N_CHIPS * H, C))], [], [_out(C, BF16)], [], "rs_pair_add")
        out.append(part.reshape(N_CHIPS, H, C))
    return out


def _rs_chip_sums(parts):
    def add4(i, n, a, b, c, d):
        return (((a.astype(F32) + b.astype(F32)) + c.astype(F32)) + d.astype(F32),)

    out = []
    for p in parts:
        _, H, C = p.shape
        tm = _rtile(H, 1024)
        (red,) = _rows(add4, H, tm, [(p.reshape(N_CHIPS * H, C), C, functools.partial(_const, v=0), j * (H // tm)) for j in range(N_CHIPS)],
                       [], [_out(C, F32)], [], "rs_chip_add")
        out.append(red)
    return out


class _Exchange:
    def __init__(self, a):
        self.a = a
        self.axis = {n: (1 if n in TRANSPOSED else ax) for n, ax in SHARDED}
        shapes = {n: (1,) + tuple(self.packed(n, a[n]).shape[1:]) for n in self.axis}
        widths = lambda names: shapes[names[0]][-1] if names[0] == "ffn_w_up" else PACK_COLS
        self.layouts = {"early": [_layout(shapes, ns, widths(ns)) for ns in EARLY], "late": [_layout(shapes, ns, widths(ns)) for ns in LATE]}
        self.reduced = {}

    @staticmethod
    def packed(n, w):
        return jnp.swapaxes(w, -1, -2) if n in TRANSPOSED else w

    def shard(self, l, group):
        def piece(n, li, lo):
            w = self.packed(n, self.a[n][l:l + 1] if li is None else self.a[n][l])
            return w - w.astype(BF16).astype(F32) if lo else w
        return [_pack(lay, piece, BF16) for lay in self.layouts[group]]

    def weights(self, gathered, group):
        W, resid = {}, {}
        for (width, rows, ents), g in zip(self.layouts[group], gathered):
            for n, li, lo, ps, off, r in ents:
                parts = _unslab(g[:, off:off + r], ps, lead=1)
                ax = self.axis[n] + (1 if li is None else 0)
                full = jnp.moveaxis(parts, 0, ax - 1)
                full = full.reshape(full.shape[:ax - 1] + (-1,) + full.shape[ax + 1:])
                (resid if lo else W)[n] = full[0] if li is None else full
        for n in resid:
            W[n] = W[n].astype(F32) + resid[n].astype(F32)
        return W

    def submit(self, GW, group):
        def by_chip(g, ax, parts=N_CHIPS):
            g = g.reshape(g.shape[:ax] + (parts, g.shape[ax] // parts) + g.shape[ax + 1:])
            return jnp.moveaxis(g, ax, 0)

        def piece(n, li, lo):
            if lo:
                return None
            g = GW[n]
            if isinstance(g, tuple):
                return jnp.concatenate([by_chip(h, self.axis[n] - 1, N_CHIPS // 2) for h in g])
            return by_chip(g[None], self.axis[n]) if li is None else by_chip(g, self.axis[n] - 1)

        return _rs_pair_sums([_pack(lay, piece, BF16, lead=1) for lay in self.layouts[group]])

    def collect(self, l, group, parts):
        self.reduced[l, group] = _rs_chip_sums(parts)

    def finish(self):
        keys = [(l, g) for l in range(DEPTH) for g in self.layouts]
        flat = _rs_pair_share([f for key in keys for f in self.reduced[key]])
        both, at = {}, 0
        for key in keys:
            both[key] = flat[at:at + len(self.layouts[key[1]])]
            at += len(self.layouts[key[1]])
        grads = {}
        for group, lays in self.layouts.items():
            for b, (width, rows, ents) in enumerate(lays):
                for n, li, lo, ps, off, r in ents:
                    if not lo:
                        per_layer = [self.packed(n, _unslab(both[l, group][b].reshape(rows, width)[off:off + r], ps)) for l in range(DEPTH)]
                        grads[n] = jnp.concatenate(per_layer) if li is None else jnp.stack(per_layer)
        return grads


def _adam(w, g, m, v, name, g_row=0):
    shp = w.shape
    two = lambda a: a.reshape(-1, shp[-1])
    rows = math.prod(shp[:-1])
    tm = _rtile(rows, 256)
    assert g_row % tm == 0
    g_in = (two(g), shp[-1], functools.partial(_const, v=0), g_row // tm)
    res = _rows(_k_adam, rows, tm, [_cur(two(w)), g_in, _cur(two(m)), _cur(two(v))], [], [_out(shp[-1], F32)] * 4, [], name)
    return tuple(r.reshape(shp) for r in res)


def _pack_flat(parts, rows):
    flat = jnp.concatenate([p.astype(F32).reshape(-1) for p in parts])
    return jnp.pad(flat, (0, rows * PACK_COLS - flat.shape[0])).reshape(rows, PACK_COLS)


def _unpack_flat(buf, shapes):
    flat, out, off = buf.reshape(-1), [], 0
    for shp in shapes:
        n = math.prod(shp)
        out.append(flat[off:off + n].reshape(shp))
        off += n
    return out


def kernel(x, positions, norm_mix_pre, norm_mix_post, norm_ffn_pre, norm_ffn_post, w_in, mla_q_norm, mla_w_q_up, mla_kv_norm, mla_w_kv_up, sc_conv_w, ssd_conv_w, ssd_conv_b, ssd_dt_bias, ssd_a_log, ssd_d, ssd_norm, w_out, ffn_w_up, ffn_conv_w, ffn_conv_b, ffn_w_down, loss_target, m_norm_mix_pre, m_norm_mix_post, m_norm_ffn_pre, m_norm_ffn_post, m_w_in, m_mla_q_norm, m_mla_w_q_up, m_mla_kv_norm, m_mla_w_kv_up, m_sc_conv_w, m_ssd_conv_w, m_ssd_conv_b, m_ssd_dt_bias, m_ssd_a_log, m_ssd_d, m_ssd_norm, m_w_out, m_ffn_w_up, m_ffn_conv_w, m_ffn_conv_b, m_ffn_w_down, v_norm_mix_pre, v_norm_mix_post, v_norm_ffn_pre, v_norm_ffn_post, v_w_in, v_mla_q_norm, v_mla_w_q_up, v_mla_kv_norm, v_mla_w_kv_up, v_sc_conv_w, v_ssd_conv_w, v_ssd_conv_b, v_ssd_dt_bias, v_ssd_a_log, v_ssd_d, v_ssd_norm, v_w_out, v_ffn_w_up, v_ffn_conv_w, v_ffn_conv_b, v_ffn_w_down):
    a = dict(locals())
    ex = _Exchange(a)
    S = {n: a[n] for n in SMALL}
    loss_part, gx, _, GS = _local_step(a["x"][0], a["positions"][0], a["loss_target"][0], None, S, ex)

    grads, delta, new_m, new_v = {}, {}, {}, {}
    for n, g in ex.finish().items():
        grads[n], delta[n], new_m[n], new_v[n] = _adam(a[n], g, a["m_" + n], a["v_" + n], "adamw_" + n)

    small_shapes = [a[n].shape for n in SMALL]
    rs = -(-(sum(math.prod(s) for s in small_shapes) + 1) // (PACK_COLS * SLAB_ALIGN)) * SLAB_ALIGN
    red = _all_reduce_small(_pack_flat([GS[n] for n in SMALL] + [loss_part.reshape(1)], rs))
    loss = _unpack_flat(red, small_shapes + [(1,)])[-1][0]
    pk = lambda pre: _pack_flat([a[pre + n] for n in SMALL], rs)
    for dst, buf in zip((grads, delta, new_m, new_v), _adam(pk(""), red, pk("m_"), pk("v_"), "adamw_small")):
        dst.update(zip(SMALL, _unpack_flat(buf, small_shapes)))

    return (loss, gx[None], *[grads[n] for n in WEIGHTS], *[delta[n] for n in WEIGHTS], *[new_m[n] for n in WEIGHTS],
            *[new_v[n] for n in WEIGHTS])
```

```python
import functools
import math

import jax
import jax.numpy as jnp
from jax import lax
from jax.experimental import pallas as pl
from jax.experimental.pallas import tpu as pltpu

F32 = jnp.float32
BF16 = jnp.bfloat16
MXU_DTYPE = jnp.bfloat16
HIGHEST = lax.Precision.HIGHEST
MESH = pl.DeviceIdType.MESH

D_MODEL = 1024
DEPTH = 4
HEADS = 8
Q_LORA = 256
KV_LORA = 128
NOPE = 64
ROPE = 32
VDIM = 64
ROPE_THETA = 10000.0
SC_DIM = 256
SSD_HEADS = 4
SSD_HEAD_DIM = 64
SSD_STATE = 128
SSD_DIM = 256
SSD_CONV_DIM = 768
SSD_CHUNK = 128
FFN_DIM = 2816
NORM_EPS = 1e-6
QK_SCALE = (NOPE + ROPE) ** -0.5
LANE = 128
HP = 128
FLASH_HEADS = 2

ZIN = 2560
Z_CQ, Z_CKV, Z_KR, Z_SCB, Z_SCC, Z_SCH, Z_SSZ, Z_XBC, Z_DT = 0, 256, 384, 512, 768, 1024, 1280, 1536, 2304
KR_LANE = 64
YCAT = HEADS * HP + SC_DIM + SSD_DIM
FFN_TILE = 256
FFN_ROWS = 1024
ROW_BLOCK = 512

ADAM_LR, ADAM_B1, ADAM_B2, ADAM_EPS, ADAM_WD, ADAM_STEP = 0.001, 0.9, 0.999, 1e-08, 0.01, 10

PACK_COLS = 1024


def _tile(n, pref):
    if n <= pref:
        return n
    t = (pref // LANE) * LANE
    while t >= LANE:
        if n % t == 0:
            return t
        t -= LANE
    raise ValueError(f"no tile for {n}")


MM_TM, MM_TN, MM_TK = 1024, 1408, 1536


def _mm(a, b, mode, out_dtype, name, tm=None, tn=MM_TN, tkmax=MM_TK):
    pair = isinstance(a, tuple)
    a_list = list(a) if pair else [a]
    layer = None
    if isinstance(b, tuple):
        b, layer = b
    bshape = b.shape[-2:]
    if mode == "nn":
        (M, Ka), (_, N) = a_list[0].shape, bshape
    elif mode == "nt":
        (M, Ka), (N, _) = a_list[0].shape, bshape
    else:
        (Ka, M), (_, N) = a_list[0].shape, bshape
    tm = (MM_TN if mode == "tn" else MM_TM) if tm is None else tm
    tm, tn, tk = _tile(M, tm), _tile(N, tn), _tile(Ka, tkmax)
    nka = Ka // tk
    nk = nka * len(a_list)

    def bspec(shape, index):
        if layer is None:
            return pl.BlockSpec(shape, index)
        return pl.BlockSpec((None,) + shape, lambda i, j, k: (layer,) + index(i, j, k))

    if mode == "nn":
        a_specs = [pl.BlockSpec((tm, tk), lambda i, j, k: (i, jnp.minimum(k, nka - 1))),
                   pl.BlockSpec((tm, tk), lambda i, j, k: (i, jnp.maximum(k - nka, 0)))][:len(a_list)]
        b_spec = bspec((tk, tn), lambda i, j, k: (k, j))
        dims = NN
    elif mode == "nt":
        a_specs = [pl.BlockSpec((tm, tk), lambda i, j, k: (i, jnp.minimum(k, nka - 1))),
                   pl.BlockSpec((tm, tk), lambda i, j, k: (i, jnp.maximum(k - nka, 0)))][:len(a_list)]
        b_spec = bspec((tn, tk), lambda i, j, k: (j, k))
        dims = NT
    else:
        a_specs = [pl.BlockSpec((tk, tm), lambda i, j, k: (k, i))]
        b_spec = pl.BlockSpec((tk, tn), lambda i, j, k: (k, j))
        dims = TN
    na = len(a_list)

    def body(*refs):
        a_refs, b_ref, o_ref = refs[:na], refs[na], refs[na + 1]
        k = pl.program_id(2)

        def prod(a_ref):
            return lax.dot_general(a_ref[...].astype(MXU_DTYPE), b_ref[...].astype(MXU_DTYPE), dims, preferred_element_type=F32)

        if nk == 1:
            o_ref[...] = prod(a_refs[0]).astype(o_ref.dtype)
            return
        acc_ref = refs[na + 2]

        @pl.when(k == 0)
        def _():
            acc_ref[...] = prod(a_refs[0])

        @pl.when((k > 0) & (k < nka))
        def _():
            acc_ref[...] += prod(a_refs[0])

        if pair:
            @pl.when(k >= nka)
            def _():
                acc_ref[...] += prod(a_refs[1])

        @pl.when(k == nk - 1)
        def _():
            o_ref[...] = acc_ref[...].astype(o_ref.dtype)

    return pl.pallas_call(
        body, name=name, grid=(M // tm, N // tn, nk),
        in_specs=a_specs + [b_spec], out_specs=pl.BlockSpec((tm, tn), lambda i, j, k: (i, j)),
        out_shape=jax.ShapeDtypeStruct((M, N), out_dtype),
        scratch_shapes=[pltpu.VMEM((tm, tn), F32)] if nk > 1 else [],
        compiler_params=pltpu.CompilerParams(dimension_semantics=("parallel", "parallel", "arbitrary")),
    )(*a_list, b)


HALO = 8


def _const(j, v):
    return v


def _rows(fn, T, tm, ins, consts, outs, accs, name, ncol=1):
    n = T // tm
    hb = tm // HALO
    last = T // HALO - 1
    in_specs, args = [], []
    for arr, bc, cb, kind in ins:
        if isinstance(kind, int):
            in_specs.append(pl.BlockSpec((tm, bc), lambda j, i, cb=cb, off=kind: (i + off, cb(j))))
        elif kind == "cur":
            in_specs.append(pl.BlockSpec((tm, bc), lambda j, i, cb=cb: (i, cb(j))))
        elif kind == "prev":
            in_specs.append(pl.BlockSpec((HALO, bc), lambda j, i, cb=cb: (jnp.maximum(i * hb - 1, 0), cb(j))))
        else:
            in_specs.append(pl.BlockSpec((HALO, bc), lambda j, i, cb=cb: (jnp.minimum((i + 1) * hb, last), cb(j))))
        args.append(arr)
    for arr, bc, cb in consts:
        in_specs.append(pl.BlockSpec((arr.shape[0], bc), lambda j, i, cb=cb: (0, cb(j))))
        args.append(arr)
    out_specs, out_shape = [], []
    for tc, dt, bc, cb in outs:
        out_specs.append(pl.BlockSpec((tm, bc), lambda j, i, cb=cb: (i, cb(j))))
        out_shape.append(jax.ShapeDtypeStruct((T, tc), dt))
    for r, tc, bc, cb in accs:
        out_specs.append(pl.BlockSpec((r, bc), lambda j, i, cb=cb: (0, cb(j))))
        out_shape.append(jax.ShapeDtypeStruct((r, tc), F32))
    nin, nout, nacc = len(args), len(outs), len(accs)

    def body(*refs):
        i = pl.program_id(1)
        res = fn(i, n, *[r[...] for r in refs[:nin]])
        for r, v in zip(refs[nin:nin + nout], res[:nout]):
            r[...] = v.astype(r.dtype)
        if nacc:
            acc_refs = refs[nin + nout:nin + nout + nacc]

            @pl.when(i == 0)
            def _():
                for r in acc_refs:
                    r[...] = jnp.zeros_like(r)

            for r, v in zip(acc_refs, res[nout:]):
                r[...] += v.astype(F32)

    res = pl.pallas_call(
        body, name=name, grid=(ncol, n), in_specs=in_specs, out_specs=out_specs, out_shape=out_shape,
        compiler_params=pltpu.CompilerParams(dimension_semantics=("arbitrary", "arbitrary")),
    )(*args)
    return res


def _cur(arr, bc=None, blk=0):
    bc = arr.shape[1] if bc is None else bc
    return (arr, bc, functools.partial(_const, v=blk), "cur")


def _halo(arr, kind, bc=None, blk=0):
    bc = arr.shape[1] if bc is None else bc
    return (arr, bc, functools.partial(_const, v=blk), kind)


def _cst(arr):
    return (arr, arr.shape[1], functools.partial(_const, v=0))


def _out(cols, dt):
    return (cols, dt, cols, functools.partial(_const, v=0))


def _acc(rows, cols):
    return (rows, cols, cols, functools.partial(_const, v=0))


def _rms(x, w):
    return x * lax.rsqrt(jnp.mean(x * x, axis=-1, keepdims=True) + NORM_EPS) * w


def _sigmoid(x):
    return 0.5 * jnp.tanh(0.5 * x) + 0.5


def _silu(x):
    return x * _sigmoid(x)


def _dsilu(x):
    s = _sigmoid(x)
    return s * (1.0 + x * (1.0 - s))


def _softplus(x):
    return jnp.maximum(x, 0.0) + jnp.log1p(jnp.exp(-jnp.abs(x)))


def _shift(a, k):
    return pltpu.roll(a, k % a.shape[0], 0)


def _lroll(a, k):
    return pltpu.roll(a, k % a.shape[1], 1)


def _vjp_wrap(f, nrow, nconst, add_first=False):
    def g(i, n, *vals):
        rows, consts, mid = vals[:nrow], vals[len(vals) - nconst:], vals[nrow:len(vals) - nconst]
        cots = mid[:-1] if add_first else mid
        outs, pull = jax.vjp(f, *rows, *consts)
        grads = list(pull(tuple(c.astype(o.dtype) for c, o in zip(cots, outs))))
        if add_first:
            grads[0] = grads[0] + mid[-1]
        return tuple(grads)
    return g


def _rows_vjp(f, T, tm, rows, consts, cots, out_dtypes, name):
    return _rows(_vjp_wrap(f, len(rows), len(consts)), T, tm, [_cur(r) for r in rows] + [_cur(c) for c in cots],
                 [_cst(c) for c in consts], [_out(r.shape[1], dt) for r, dt in zip(rows, out_dtypes)],
                 [_acc(1, c.shape[1]) for c in consts], name)


def _f_premix(x, g):
    return (_rms(x, g),)


def _f_mla_pre(cq, ckv, qn, kvn):
    return _rms(cq, qn), _rms(ckv, kvn)


def _f_ssd_gate(y, z, nw):
    return (_rms(y * _silu(z), nw),)


def _f_post_mix(x, mixed, gpost, gffn):
    x1 = x + _rms(mixed, gpost)
    return x1, _rms(x1, gffn)


def _f_post_ffn(x1, d, gpost):
    return (x1 + _rms(d, gpost),)


def _rope_fwd(v, cosf, sina, sinb):
    return v * cosf + _lroll(v, -16) * sina + _lroll(v, 16) * sinb


def _rope_bwd(g, cosf, sina, sinb):
    return g * cosf + _lroll(g * sina, 16) + _lroll(g * sinb, -16)


def _k_rope_fwd(i, n, qpad, kvpad, kr, cosf, sina, sinb):
    qs, ks = [], []
    krr = _rope_fwd(kr, cosf, sina, sinb)
    for h in range(HEADS):
        sl = slice(h * HP, (h + 1) * HP)
        qs.append(_rope_fwd(qpad[:, sl], cosf, sina, sinb))
        ks.append(kvpad[:, sl].astype(F32) + krr)
    return jnp.concatenate(qs, axis=1), jnp.concatenate(ks, axis=1)


def _k_rope_bwd(i, n, dq, dk, dv, cosf, sina, sinb):
    lane = lax.broadcasted_iota(jnp.int32, (1, HP), 1)
    rmask = ((lane >= KR_LANE) & (lane < KR_LANE + ROPE)).astype(F32)
    dqs, dks = [], []
    dkr = jnp.zeros((dq.shape[0], HP), F32)
    for h in range(HEADS):
        sl = slice(h * HP, (h + 1) * HP)
        dqs.append(_rope_bwd(dq[:, sl], cosf, sina, sinb))
        dkh = dk[:, sl]
        dkr = dkr + dkh * rmask
        dks.append(dkh * (1.0 - rmask))
    dkr = _rope_bwd(dkr, cosf, sina, sinb) * rmask
    return jnp.concatenate(dqs, axis=1), jnp.concatenate(dks + [dv], axis=1), dkr


def _k_sconv_fwd(i, n, b, c, h, cp, hp, w):
    m = b.shape[0]
    up = jnp.where(i > 0, cp * hp, 0.0)
    ue = jnp.concatenate([up, c * h], axis=0)
    conv = w[2:3] * ue + w[1:2] * _shift(ue, 1) + w[0:1] * _shift(ue, 2)
    return (b * conv[HALO:],)


def _k_sconv_bwd(i, n, b, c, h, dy, cp, hp, bn, dyn, w):
    m = b.shape[0]
    up = jnp.where(i > 0, cp * hp, 0.0)
    ue = jnp.concatenate([up, c * h], axis=0)
    u1, u2 = _shift(ue, 1), _shift(ue, 2)
    conv = (w[2:3] * ue + w[1:2] * u1 + w[0:1] * u2)[HALO:]
    dc_cur = dy * b
    dce = jnp.concatenate([dc_cur, jnp.where(i < n - 1, dyn * bn, 0.0)], axis=0)
    du = (w[2:3] * dce + w[1:2] * _shift(dce, -1) + w[0:1] * _shift(dce, -2))[:m]
    dw = jnp.concatenate([
        jnp.sum(dc_cur * u2[HALO:], axis=0, keepdims=True),
        jnp.sum(dc_cur * u1[HALO:], axis=0, keepdims=True),
        jnp.sum(dc_cur * ue[HALO:], axis=0, keepdims=True),
        jnp.zeros((HALO - 3, b.shape[1]), F32)], axis=0)
    return dy * conv, du * h, du * c, dw


def _conv4(ue, w):
    return w[3:4] * ue + w[2:3] * _shift(ue, 1) + w[1:2] * _shift(ue, 2) + w[0:1] * _shift(ue, 3)


def _k_ssdconv_fwd(i, n, u, up, w, bias):
    ue = jnp.concatenate([jnp.where(i > 0, up, 0.0), u], axis=0)
    return (_silu(_conv4(ue, w)[HALO:] + bias),)


def _k_ssdconv_bwd(i, n, u, dout, up, un, doutn, w, bias):
    m = u.shape[0]
    ue = jnp.concatenate([jnp.where(i > 0, up, 0.0), u, un], axis=0)
    u1, u2, u3 = _shift(ue, 1), _shift(ue, 2), _shift(ue, 3)
    pre = (w[3:4] * ue + w[2:3] * u1 + w[1:2] * u2 + w[0:1] * u3)[HALO:] + bias
    doe = jnp.concatenate([dout, jnp.where(i < n - 1, doutn, 0.0)], axis=0)
    dpre = doe * _dsilu(pre)
    du = (w[3:4] * dpre + w[2:3] * _shift(dpre, -1) + w[1:2] * _shift(dpre, -2) + w[0:1] * _shift(dpre, -3))[:m]
    dp = dpre[:m]
    cur = slice(HALO, HALO + m)
    dw = jnp.concatenate([
        jnp.sum(dp * u3[cur], axis=0, keepdims=True),
        jnp.sum(dp * u2[cur], axis=0, keepdims=True),
        jnp.sum(dp * u1[cur], axis=0, keepdims=True),
        jnp.sum(dp * ue[cur], axis=0, keepdims=True),
        jnp.zeros((HALO - 4, u.shape[1]), F32)], axis=0)
    db = jnp.sum(dp, axis=0, keepdims=True)
    return du, dw, db


def _conv3(ue, w):
    return w[2:3] * ue + w[1:2] * _shift(ue, 1) + w[0:1] * _shift(ue, 2)


def _k_ffnact_fwd(i, n, ug, uu, ugp, uup, wg, wu, bg, bu):
    gate = _conv3(jnp.concatenate([jnp.where(i > 0, ugp, 0.0), ug], axis=0), wg)[HALO:] + bg
    upv = _conv3(jnp.concatenate([jnp.where(i > 0, uup, 0.0), uu], axis=0), wu)[HALO:] + bu
    return (_silu(gate) * upv,)


def _k_ffnact_bwd(i, n, ug, uu, dact, ugp, uup, ugn, uun, dactn, wg, wu, bg, bu):
    m = ug.shape[0]
    cur = slice(HALO, HALO + m)

    def taps(p, c, nx):
        e = jnp.concatenate([jnp.where(i > 0, p, 0.0), c, nx], axis=0)
        return e, _shift(e, 1), _shift(e, 2)

    def back(d, w):
        return (w[2:3] * d + w[1:2] * _shift(d, -1) + w[0:1] * _shift(d, -2))[:m]

    def wgrad(d, t):
        return jnp.concatenate([jnp.sum(d[:m] * t[2][cur], axis=0, keepdims=True), jnp.sum(d[:m] * t[1][cur], axis=0, keepdims=True),
                                jnp.sum(d[:m] * t[0][cur], axis=0, keepdims=True), jnp.zeros((HALO - 3, d.shape[1]), F32)], axis=0)

    tg, tu = taps(ugp, ug, ugn), taps(uup, uu, uun)
    gate = (wg[2:3] * tg[0] + wg[1:2] * tg[1] + wg[0:1] * tg[2])[HALO:] + bg
    upv = (wu[2:3] * tu[0] + wu[1:2] * tu[1] + wu[0:1] * tu[2])[HALO:] + bu
    dae = jnp.concatenate([dact, jnp.where(i < n - 1, dactn, 0.0)], axis=0)
    sg = _sigmoid(gate)
    dg = dae * upv * (sg * (1.0 + gate * (1.0 - sg)))
    dup = dae * (gate * sg)
    return (back(dg, wg), back(dup, wu), wgrad(dg, tg), wgrad(dup, tu),
            jnp.sum(dg[:m], axis=0, keepdims=True), jnp.sum(dup[:m], axis=0, keepdims=True))


def _k_loss(i, n, y, tgt):
    e = y - tgt
    part = 0.5 * jnp.sum(jnp.sum(e * e, axis=1, keepdims=True) / D_MODEL, axis=0, keepdims=True)
    return e * (1.0 / D_MODEL), jnp.broadcast_to(part, (1, LANE))


def _k_adam(i, n, w, g, m, v):
    m = ADAM_B1 * m + (1.0 - ADAM_B1) * g
    v = ADAM_B2 * v + (1.0 - ADAM_B2) * (g * g)
    m_hat = m / (1.0 - ADAM_B1 ** ADAM_STEP)
    v_hat = v / (1.0 - ADAM_B2 ** ADAM_STEP)
    delta = -ADAM_LR * (m_hat / (jnp.sqrt(v_hat) + ADAM_EPS) + ADAM_WD * w)
    return g, delta, m, v


def _dotf(a, b, dims):
    return lax.dot_general(a.astype(MXU_DTYPE), b.astype(MXU_DTYPE), dims, preferred_element_type=F32)


NN = (((1,), (0,)), ((), ()))
NT = (((1,), (1,)), ((), ()))
TN = (((0,), (0,)), ((), ()))


def _ssd_chunk(x0, x1, x2, x3, b0, b1, c0, c1, dtraw, p0, p1, p2, p3, dtb, alog, dsk):
    xs, bs, cs_, ps = (x0, x1, x2, x3), (b0, b1), (c0, c1), (p0, p1, p2, p3)
    L = dtraw.shape[0]
    dt = _softplus(dtraw + dtb)
    adt = dt * (-jnp.exp(alog))
    row = lax.broadcasted_iota(jnp.int32, (L, L), 0)
    col = lax.broadcasted_iota(jnp.int32, (L, L), 1)
    tril = row >= col
    cum = jnp.dot(tril.astype(F32), adt, precision=HIGHEST, preferred_element_type=F32)
    cum_t = cum.T
    lane = lax.broadcasted_iota(jnp.int32, (1, LANE), 1)
    sub = lax.broadcasted_iota(jnp.int32, (LANE, 1), 0)
    lastcol = (lax.broadcasted_iota(jnp.int32, (1, L), 1) == L - 1).astype(F32)
    ys, news = [], []
    for h in range(SSD_HEADS):
        g = h // (SSD_HEADS // 2)
        oh = (lane == h).astype(F32)
        dth = jnp.sum(dt * oh, axis=1, keepdims=True)
        csh = jnp.sum(cum * oh, axis=1, keepdims=True)
        csr = jnp.sum(cum_t * (sub == h).astype(F32), axis=0, keepdims=True)
        cl = jnp.sum(csr * lastcol, axis=1, keepdims=True)
        dskh = jnp.sum(dsk * oh, axis=1, keepdims=True)
        x, bm, cm, prev = xs[h], bs[g], cs_[g], ps[h]
        xdt = x * dth
        decay = jnp.exp(jnp.where(tril, csh - csr, -jnp.inf))
        scores = _dotf(cm, bm, NT) * decay
        y_diag = _dotf(scores, xdt, NN)
        bd = bm * jnp.exp(cl - csh)
        cst = _dotf(xdt, bd, TN)
        news.append(prev * jnp.exp(cl) + cst)
        y_off = _dotf(cm, prev, NT) * jnp.exp(csh)
        ys.append(y_diag + y_off + x * dskh)
    return (*ys, *news)


def _ssd_operands(x_ref, dt_ref, par_ref, prev):
    xs = [x_ref[:, h * SSD_HEAD_DIM:(h + 1) * SSD_HEAD_DIM] for h in range(SSD_HEADS)]
    bs = [x_ref[:, SSD_DIM + g * SSD_STATE:SSD_DIM + (g + 1) * SSD_STATE] for g in range(2)]
    cs_ = [x_ref[:, SSD_DIM + 2 * SSD_STATE + g * SSD_STATE:SSD_DIM + 2 * SSD_STATE + (g + 1) * SSD_STATE] for g in range(2)]
    return (*xs, *bs, *cs_, dt_ref[...], *prev, par_ref[0:1, :], par_ref[1:2, :], par_ref[2:3, :])


def _ssd_fwd(xbc, dtraw, par, T, dt_blk=0):
    L = SSD_CHUNK
    nc = T // L
    P = SSD_HEAD_DIM

    def body(x_ref, dt_ref, par_ref, y_ref, st_ref, state):
        @pl.when(pl.program_id(0) == 0)
        def _():
            state[...] = jnp.zeros_like(state)

        st_ref[0] = state[...]
        prev = [state[h * P:(h + 1) * P, :] for h in range(SSD_HEADS)]
        res = _ssd_chunk(*_ssd_operands(x_ref, dt_ref, par_ref, prev))
        for h in range(SSD_HEADS):
            y_ref[:, h * P:(h + 1) * P] = res[h]
            state[h * P:(h + 1) * P, :] = res[SSD_HEADS + h]

    return pl.pallas_call(
        body, name="ssd_scan_fwd", grid=(nc,),
        in_specs=[pl.BlockSpec((L, SSD_CONV_DIM), lambda c: (c, 0)), pl.BlockSpec((L, LANE), lambda c: (c, dt_blk)),
                  pl.BlockSpec((8, LANE), lambda c: (0, 0))],
        out_specs=[pl.BlockSpec((L, SSD_DIM), lambda c: (c, 0)), pl.BlockSpec((1, SSD_DIM, SSD_STATE), lambda c: (c, 0, 0))],
        out_shape=[jax.ShapeDtypeStruct((T, SSD_DIM), F32), jax.ShapeDtypeStruct((nc, SSD_DIM, SSD_STATE), F32)],
        scratch_shapes=[pltpu.VMEM((SSD_DIM, SSD_STATE), F32)],
        compiler_params=pltpu.CompilerParams(dimension_semantics=("arbitrary",)),
    )(xbc, dtraw, par)


def _ssd_bwd(xbc, dtraw, par, states, dy, T, dt_blk=0):
    L = SSD_CHUNK
    nc = T // L
    P = SSD_HEAD_DIM

    def body(x_ref, dt_ref, par_ref, st_ref, dy_ref, dx_ref, ddt_ref, dpar_ref, dstate):
        @pl.when(pl.program_id(0) == 0)
        def _():
            dstate[...] = jnp.zeros_like(dstate)
            dpar_ref[...] = jnp.zeros_like(dpar_ref)

        prev = [st_ref[0, h * P:(h + 1) * P, :] for h in range(SSD_HEADS)]
        prim = _ssd_operands(x_ref, dt_ref, par_ref, prev)
        _, pull = jax.vjp(_ssd_chunk, *prim)
        cots = tuple(dy_ref[:, h * P:(h + 1) * P] for h in range(SSD_HEADS)) + tuple(
            dstate[h * P:(h + 1) * P, :] for h in range(SSD_HEADS))
        g = pull(cots)
        for h in range(SSD_HEADS):
            dx_ref[:, h * P:(h + 1) * P] = g[h]
            dstate[h * P:(h + 1) * P, :] = g[9 + h]
        for k in range(2):
            dx_ref[:, SSD_DIM + k * SSD_STATE:SSD_DIM + (k + 1) * SSD_STATE] = g[4 + k]
            dx_ref[:, SSD_DIM + 2 * SSD_STATE + k * SSD_STATE:SSD_DIM + 2 * SSD_STATE + (k + 1) * SSD_STATE] = g[6 + k]
        ddt_ref[...] = g[8]
        for r in range(3):
            dpar_ref[r:r + 1, :] += g[13 + r]

    rev = lambda c: (nc - 1 - c, 0)
    return pl.pallas_call(
        body, name="ssd_scan_bwd", grid=(nc,),
        in_specs=[pl.BlockSpec((L, SSD_CONV_DIM), rev), pl.BlockSpec((L, LANE), lambda c: (nc - 1 - c, dt_blk)),
                  pl.BlockSpec((8, LANE), lambda c: (0, 0)),
                  pl.BlockSpec((1, SSD_DIM, SSD_STATE), lambda c: (nc - 1 - c, 0, 0)), pl.BlockSpec((L, SSD_DIM), rev)],
        out_specs=[pl.BlockSpec((L, SSD_CONV_DIM), rev), pl.BlockSpec((L, LANE), rev), pl.BlockSpec((8, LANE), lambda c: (0, 0))],
        out_shape=[jax.ShapeDtypeStruct((T, SSD_CONV_DIM), F32), jax.ShapeDtypeStruct((T, LANE), F32),
                   jax.ShapeDtypeStruct((8, LANE), F32)],
        scratch_shapes=[pltpu.VMEM((SSD_DIM, SSD_STATE), F32)],
        compiler_params=pltpu.CompilerParams(dimension_semantics=("arbitrary",)),
    )(xbc, dtraw, par, states, dy)


def _causal_pairs(nq, by_query):
    if by_query:
        pairs = [(i, j) for i in range(nq) for j in range(i + 1)]
    else:
        pairs = [(i, j) for j in range(nq) for i in range(j, nq)]
    return jnp.asarray([p[0] for p in pairs], jnp.int32), jnp.asarray([p[1] for p in pairs], jnp.int32)


def _flash_fwd(q, k, kv, T, carry=()):
    tq = tk = min(512, T)
    nq = T // tq
    G = FLASH_HEADS
    rep = tk // HP
    nc = len(carry)
    qi, kj = _causal_pairs(nq, by_query=True)
    nh, nt = HEADS // G, qi.shape[0]

    def body(qi_ref, kj_ref, q_ref, k_ref, v_ref, *rest):
        w_refs, o_ref, g_refs = rest[:nc], rest[nc], rest[nc + 1:2 * nc + 1]
        m_ref, l_ref, acc_ref = rest[2 * nc + 1:2 * nc + 4]
        h, t = pl.program_id(0), pl.program_id(1)
        i, j = qi_ref[t], kj_ref[t]
        if nc:
            plan = lambda: _ag_plan(w_refs, g_refs, rest[2 * nc + 4:])

            @pl.when((h == 0) & (t == 0))
            def _():
                for cp in plan()[0]:
                    cp.start()

        @pl.when(j == 0)
        def _():
            m_ref[...] = jnp.full_like(m_ref, -jnp.inf)
            l_ref[...] = jnp.zeros_like(l_ref)
            acc_ref[...] = jnp.zeros_like(acc_ref)

        def step(diagonal):
            for g in range(G):
                sl = slice(g * HP, (g + 1) * HP)
                s = _dotf(q_ref[:, sl], k_ref[:, sl], NT) * QK_SCALE
                if diagonal:
                    rows = lax.broadcasted_iota(jnp.int32, (tq, tk), 0)
                    cols = lax.broadcasted_iota(jnp.int32, (tq, tk), 1)
                    s = jnp.where(rows >= cols, s, -jnp.inf)
                m_old = m_ref[:, sl]
                m_new = jnp.maximum(m_old, jnp.max(s, axis=1, keepdims=True))
                p = jnp.exp(s - jnp.tile(m_new, (1, rep)))
                alpha = jnp.exp(m_old - m_new)
                l_ref[:, sl] = alpha * l_ref[:, sl] + jnp.sum(p, axis=1, keepdims=True)
                acc_ref[:, sl] = alpha * acc_ref[:, sl] + _dotf(p, v_ref[:, sl], NN)
                m_ref[:, sl] = m_new

        @pl.when(j < i)
        def _():
            step(False)

        @pl.when(j == i)
        def _():
            step(True)
            lane = lax.broadcasted_iota(jnp.int32, (tq, HP), 1)
            for g in range(G):
                sl = slice(g * HP, (g + 1) * HP)
                l = l_ref[:, sl]
                o_ref[:, sl] = jnp.where(lane < VDIM, acc_ref[:, sl] / l, m_ref[:, sl] + jnp.log(l))

        if nc:
            @pl.when((h == nh - 1) & (t == 0))
            def _():
                _, lands, forwards, _ = plan()
                for land, fw in zip(lands, forwards):
                    land.wait_recv()
                    fw.start()

            @pl.when((h == nh - 1) & (t == nt - 1))
            def _():
                sends, _, forwards, finals = plan()
                for cp in finals:
                    cp.wait_recv()
                for cp in sends + forwards:
                    cp.wait_send()

    W = G * HP
    res = pl.pallas_call(
        body, name="mla_flash_fwd",
        grid_spec=pltpu.PrefetchScalarGridSpec(
            num_scalar_prefetch=2, grid=(nh, nt),
            in_specs=[pl.BlockSpec((tq, W), lambda h, t, qi, kj: (qi[t], h)),
                      pl.BlockSpec((tk, W), lambda h, t, qi, kj: (kj[t], h)),
                      pl.BlockSpec((tk, W), lambda h, t, qi, kj: (kj[t], HEADS // G + h))] + [ANY] * nc,
            out_specs=[pl.BlockSpec((tq, W), lambda h, t, qi, kj: (qi[t], h))] + [ANY] * nc,
            scratch_shapes=[pltpu.VMEM((tq, W), F32), pltpu.VMEM((tq, W), F32), pltpu.VMEM((tq, W), F32)] + (_ag_sems(nc) if nc else [])),
        out_shape=[jax.ShapeDtypeStruct((T, HEADS * HP), F32)] + [jax.ShapeDtypeStruct((N_CHIPS,) + w.shape, w.dtype) for w in carry],
        compiler_params=pltpu.CompilerParams(dimension_semantics=("arbitrary", "arbitrary")),
    )(qi, kj, q, k, kv, *carry)
    return res[0] if not nc else (res[0], [_own_slot(g, w) for g, w in zip(res[1:], carry)])


def _flash_bwd(q, k, kv, o, dycat, T, carry=()):
    tq = tk = min(512, T)
    nq = T // tq
    G = FLASH_HEADS
    nc = len(carry)
    qi, kj = _causal_pairs(nq, by_query=False)
    nh, nt = HEADS // G, qi.shape[0]

    def body(qi_ref, kj_ref, q_ref, k_ref, v_ref, o_ref, do_ref, *rest):
        p_refs, (dq_ref, dk_ref, dv_ref), part_refs = rest[:nc], rest[nc:nc + 3], rest[nc + 3:2 * nc + 3]
        h, t = pl.program_id(0), pl.program_id(1)
        i, j = qi_ref[t], kj_ref[t]
        if nc:
            plan = lambda: _chip_plan(p_refs, part_refs, rest[2 * nc + 3:])

            @pl.when((h == 0) & (t == 0))
            def _():
                for cp in plan()[0]:
                    cp.start()

        @pl.when(t == 0)
        def _():
            dq_ref[...] = jnp.zeros_like(dq_ref)

        @pl.when(i == j)
        def _():
            dk_ref[...] = jnp.zeros_like(dk_ref)
            dv_ref[...] = jnp.zeros_like(dv_ref)

        def step(diagonal):
            r0 = pl.multiple_of(i * tq, tq)
            for g in range(G):
                sl = slice(g * HP, (g + 1) * HP)
                qv, kv, vv, ov, dov = q_ref[:, sl], k_ref[:, sl], v_ref[:, sl], o_ref[:, sl], do_ref[:, sl]
                s = _dotf(qv, kv, NT) * QK_SCALE
                p = jnp.exp(s - ov[:, VDIM:VDIM + 1])
                if diagonal:
                    rows = lax.broadcasted_iota(jnp.int32, (tq, tk), 0)
                    cols = lax.broadcasted_iota(jnp.int32, (tq, tk), 1)
                    p = jnp.where(rows >= cols, p, 0.0)
                dsum = jnp.sum(dov * ov, axis=1, keepdims=True)
                dv_ref[:, sl] += _dotf(p, dov, TN)
                dp = _dotf(dov, vv, NT)
                ds = p * (dp - dsum) * QK_SCALE
                dk_ref[:, sl] += _dotf(ds, qv, TN)
                dq_ref[pl.ds(r0, tq), sl] += _dotf(ds, kv, NN)

        @pl.when(i > j)
        def _():
            step(False)

        @pl.when(i == j)
        def _():
            step(True)

        if nc:
            @pl.when((h == nh - 1) & (t == nt - 1))
            def _():
                sends, lands = plan()
                for cp in lands:
                    cp.wait_recv()
                for cp in sends:
                    cp.wait_send()

    W = G * HP
    qmap = lambda h, t, qi, kj: (qi[t], h)
    kmap = lambda h, t, qi, kj: (kj[t], h)
    vmap = lambda h, t, qi, kj: (kj[t], HEADS // G + h)
    res = pl.pallas_call(
        body, name="mla_flash_bwd",
        grid_spec=pltpu.PrefetchScalarGridSpec(
            num_scalar_prefetch=2, grid=(nh, nt),
            in_specs=[pl.BlockSpec((tq, W), qmap), pl.BlockSpec((tk, W), kmap), pl.BlockSpec((tk, W), vmap),
                      pl.BlockSpec((tq, W), qmap), pl.BlockSpec((tq, W), qmap)] + [ANY] * nc,
            out_specs=[pl.BlockSpec((T, W), lambda h, t, qi, kj: (0, h)), pl.BlockSpec((tk, W), kmap), pl.BlockSpec((tk, W), kmap)]
            + [ANY] * nc,
            scratch_shapes=_chip_sems(nc) if nc else []),
        out_shape=[jax.ShapeDtypeStruct((T, HEADS * HP), F32)] * 3 + [jax.ShapeDtypeStruct(p.shape, p.dtype) for p in carry],
        compiler_params=pltpu.CompilerParams(dimension_semantics=("arbitrary", "arbitrary")),
    )(qi, kj, q, k, kv, o, dycat, *carry)
    return tuple(res[:3]) if not nc else (*res[:3], _chip_parts(res[3:], carry))


_IN_SRC = (0, 256, 384, 416, 672, 928, 1184, 1440, 2208, 2212)
_IN_DST = (Z_CQ, Z_CKV, Z_KR + KR_LANE, Z_SCB, Z_SCC, Z_SCH, Z_SSZ, Z_XBC, Z_DT)


def _pad_rows_in(w):
    ax = w.ndim - 2

    def zeros(n):
        return jnp.zeros(w.shape[:ax] + (n,) + w.shape[ax + 1:], w.dtype)

    def whole_tiles(p):
        n = p.shape[ax]
        return p if n % SLAB_ALIGN == 0 else jnp.pad(p, [(0, 0)] * ax + [(0, -n % SLAB_ALIGN), (0, 0)])

    parts, at = [], 0
    for s0, s1, d0 in zip(_IN_SRC[:-1], _IN_SRC[1:], _IN_DST):
        if d0 > at:
            parts.append(zeros(d0 - at))
        parts.append(whole_tiles(lax.slice_in_dim(w, s0, s1, axis=ax)))
        at = d0 + parts[-1].shape[ax]
    parts.append(zeros(ZIN - at))
    return jnp.concatenate(parts, axis=ax)


def _unpad_rows_in(w):
    ax = w.ndim - 2
    groups = list(zip(_IN_SRC[:-1], _IN_SRC[1:], _IN_DST))
    parts = [lax.slice_in_dim(w, d0, d0 + -(-(s1 - s0) // SLAB_ALIGN) * SLAB_ALIGN, axis=ax) for s0, s1, d0 in groups]
    return lax.slice_in_dim(jnp.concatenate(parts, axis=ax), 0, _IN_SRC[-1], axis=ax)


def _pad_heads(w, width):
    w = w.reshape(w.shape[:-1] + (HEADS, width))
    w = jnp.pad(w, [(0, 0)] * (w.ndim - 1) + [(0, HP - width)])
    return w.reshape(w.shape[:-2] + (HEADS * HP,))


def _unpad_heads(w, width):
    w = w.reshape(w.shape[:-1] + (HEADS, HP))[..., :width]
    return w.reshape(w.shape[:-2] + (HEADS * width,))


def _pad_kv(w):
    w = w.reshape(w.shape[:-1] + (HEADS, NOPE + VDIM))
    return jnp.concatenate([_pad_heads(w[..., :NOPE].reshape(w.shape[:-2] + (HEADS * NOPE,)), NOPE),
                            _pad_heads(w[..., NOPE:].reshape(w.shape[:-2] + (HEADS * VDIM,)), VDIM)], axis=-1)


def _unpad_kv(w):
    k = _unpad_heads(w[..., :HEADS * HP], NOPE).reshape(w.shape[:-1] + (HEADS, NOPE))
    v = _unpad_heads(w[..., HEADS * HP:], VDIM).reshape(w.shape[:-1] + (HEADS, VDIM))
    return jnp.concatenate([k, v], axis=-1).reshape(w.shape[:-1] + (HEADS * (NOPE + VDIM),))


def _pad_out_rows(w):
    lead, d = w.shape[:-2], w.shape[-1]
    att = w[..., :HEADS * VDIM, :].reshape(lead + (HEADS, VDIM, d))
    att = jnp.pad(att, [(0, 0)] * (att.ndim - 2) + [(0, HP - VDIM), (0, 0)]).reshape(lead + (HEADS * HP, d))
    return jnp.concatenate([att, w[..., HEADS * VDIM:, :]], axis=-2)


def _unpad_out_rows(w):
    lead, d = w.shape[:-2], w.shape[-1]
    att = w[..., :HEADS * HP, :].reshape(lead + (HEADS, HP, d))[..., :VDIM, :].reshape(lead + (HEADS * VDIM, d))
    return jnp.concatenate([att, w[..., HEADS * HP:, :]], axis=-2)


def _rows8(w):
    return jnp.pad(w.astype(F32), [(0, 0)] * (w.ndim - 2) + [(0, 8 - w.shape[-2]), (0, 0)])


def _row8(*vecs):
    c = vecs[0].shape[-1]
    return jnp.concatenate([v.reshape(1, c).astype(F32) for v in vecs] + [jnp.zeros((8 - len(vecs), c), F32)], axis=0)


def _lanes(v):
    return jnp.pad(v.astype(F32), (0, LANE - v.shape[0])).reshape(1, LANE)


def _rope_tables(positions):
    inv_freq = 1.0 / (ROPE_THETA ** (jnp.arange(0, ROPE, 2, dtype=F32) / ROPE))
    ang = positions.astype(F32)[:, None] * inv_freq
    cos, sin = jnp.cos(ang), jnp.sin(ang)
    T = positions.shape[0]
    half = ROPE // 2
    one = jnp.ones((T, KR_LANE), F32)
    zero = jnp.zeros((T, KR_LANE), F32)
    tail1 = jnp.ones((T, HP - KR_LANE - ROPE), F32)
    tail0 = jnp.zeros((T, HP - KR_LANE - ROPE), F32)
    z16 = jnp.zeros((T, half), F32)
    cosf = jnp.concatenate([one, cos, cos, tail1], axis=1)
    sina = jnp.concatenate([zero, -sin, z16, tail0], axis=1)
    sinb = jnp.concatenate([zero, z16, sin, tail0], axis=1)
    return cosf, sina, sinb


def _kernel_weights(W):
    c = lambda a: a.astype(MXU_DTYPE)
    forms = dict(
        w_in=("w_in", lambda w: c(_pad_rows_in(w))),
        w_q=("mla_w_q_up", lambda w: c(_pad_heads(w, NOPE + ROPE))),
        w_kv=("mla_w_kv_up", lambda w: c(_pad_kv(w))),
        w_out=("w_out", lambda w: c(_pad_out_rows(w))),
        w_up=("ffn_w_up", c),
        w_down=("ffn_w_down", c),
        sc_w=("sc_conv_w", _rows8),
        ssd_w=("ssd_conv_w", _rows8),
        ffn_w=("ffn_conv_w", _rows8),
    )
    return {k: f(W[n]) for k, (n, f) in forms.items() if n in W}


def _layer_weights(KW, l):
    return {k: (v[l] if k in ("sc_w", "ssd_w", "ffn_w") else (v, l)) for k, v in KW.items()}


def _local_step(x, positions, target, W, S, ex=None):
    T = x.shape[0]
    tm = min(ROW_BLOCK, T)
    tm_ffn = min(FFN_ROWS, T)
    cosf, sina, sinb = _rope_tables(positions)
    if ex is None:
        KW = _kernel_weights(W)
    else:
        early = _all_gather_weights(ex.shard(0, "early"))
    saved = []
    xl = x
    for l in range(DEPTH):
        lw = _layer_weights(KW, l) if ex is None else _kernel_weights(ex.weights(early, "early"))
        g_pre = S["norm_mix_pre"][l].reshape(1, -1)
        g_post = S["norm_mix_post"][l].reshape(1, -1)
        g_fpre = S["norm_ffn_pre"][l].reshape(1, -1)
        g_fpost = S["norm_ffn_post"][l].reshape(1, -1)
        qn = S["mla_q_norm"][l].reshape(1, -1)
        kvn = S["mla_kv_norm"][l].reshape(1, -1)
        ssd_b = S["ssd_conv_b"][l].reshape(1, -1)
        ssd_par = _row8(jnp.pad(S["ssd_dt_bias"][l], (0, LANE - SSD_HEADS)), jnp.pad(S["ssd_a_log"][l], (0, LANE - SSD_HEADS)),
                        jnp.pad(S["ssd_d"][l], (0, LANE - SSD_HEADS)))
        ssd_nw = S["ssd_norm"][l].reshape(1, -1)
        ffn_b = S["ffn_conv_b"][l].reshape(1, -1)

        (h1,) = _rows(lambda i, n, *v: _f_premix(*v), T, tm, [_cur(xl)], [_cst(g_pre)], [_out(D_MODEL, BF16)], [], "pre_mix_norm")
        zin = _mm(h1, lw["w_in"], "nt", F32, "mm_in")
        qlat, kvlat = _rows(lambda i, n, *v: _f_mla_pre(*v), T, tm, [_cur(zin, Q_LORA, 0), _cur(zin, KV_LORA, Z_CKV // KV_LORA)],
                            [_cst(qn), _cst(kvn)], [_out(Q_LORA, BF16), _out(KV_LORA, BF16)], [], "mla_pre_norm")
        qpad = _mm(qlat, lw["w_q"], "nn", F32, "mm_q_up")
        kvpad = _mm(kvlat, lw["w_kv"], "nn", BF16, "mm_kv_up")
        qr, kr = _rows(_k_rope_fwd, T, tm, [_cur(qpad), _cur(kvpad, HEADS * HP, 0), _cur(zin, LANE, Z_KR // LANE),
                                            _cur(cosf), _cur(sina), _cur(sinb)], [],
                       [_out(HEADS * HP, BF16), _out(HEADS * HP, BF16)], [], "mla_rope")
        if ex is None:
            o = _flash_fwd(qr, kr, kvpad, T)
        else:
            nlate = len(ex.layouts["late"])
            o, got = _flash_fwd(qr, kr, kvpad, T, carry=ex.shard(l, "late") + (ex.shard(l + 1, "early") if l + 1 < DEPTH else []))
            lw.update(_kernel_weights(ex.weights(got[:nlate], "late")))
            early = got[nlate:]
        (yconv,) = _rows(_k_sconv_fwd, T, tm, [_cur(zin, SC_DIM, Z_SCB // SC_DIM), _cur(zin, SC_DIM, Z_SCC // SC_DIM),
                                               _cur(zin, SC_DIM, Z_SCH // SC_DIM), _halo(zin, "prev", SC_DIM, Z_SCC // SC_DIM),
                                               _halo(zin, "prev", SC_DIM, Z_SCH // SC_DIM)], [_cst(lw["sc_w"])],
                         [_out(SC_DIM, F32)], [], "short_conv_fwd")
        (xbc,) = _rows(_k_ssdconv_fwd, T, tm, [_cur(zin, SSD_CONV_DIM, Z_XBC // SSD_CONV_DIM),
                                               _halo(zin, "prev", SSD_CONV_DIM, Z_XBC // SSD_CONV_DIM)],
                       [_cst(lw["ssd_w"]), _cst(ssd_b)], [_out(SSD_CONV_DIM, F32)], [], "ssd_conv_fwd")
        yscan, states = _ssd_fwd(xbc, zin, ssd_par, T, Z_DT // LANE)
        (yssd,) = _rows(lambda i, n, *v: _f_ssd_gate(*v), T, tm, [_cur(yscan), _cur(zin, SSD_DIM, Z_SSZ // SSD_DIM)], [_cst(ssd_nw)],
                        [_out(SSD_DIM, F32)], [], "ssd_gate_fwd")
        ycat = jnp.concatenate([o.astype(BF16), yconv.astype(BF16), yssd.astype(BF16)], axis=1)
        mixed = _mm(ycat, lw["w_out"], "nn", F32, "mm_out")
        x1, h2 = _rows(lambda i, n, *v: _f_post_mix(*v), T, tm, [_cur(xl), _cur(mixed)], [_cst(g_post), _cst(g_fpre)],
                       [_out(D_MODEL, F32), _out(D_MODEL, BF16)], [], "post_mix_fwd")
        upre = _mm(h2, lw["w_up"], "nn", F32, "mm_up")
        nt = FFN_DIM // FFN_TILE
        gcol, ucol = (lambda j: j), (lambda j: j + nt)
        (act,) = _rows(_k_ffnact_fwd, T, tm_ffn,
                       [(upre, FFN_TILE, gcol, "cur"), (upre, FFN_TILE, ucol, "cur"), (upre, FFN_TILE, gcol, "prev"),
                        (upre, FFN_TILE, ucol, "prev")],
                       [(lw["ffn_w"], FFN_TILE, gcol), (lw["ffn_w"], FFN_TILE, ucol), (ffn_b, FFN_TILE, gcol), (ffn_b, FFN_TILE, ucol)],
                       [(FFN_DIM, BF16, FFN_TILE, gcol)], [], "ffn_act_fwd", ncol=nt)
        dn = _mm(act, lw["w_down"], "nn", F32, "mm_down")
        (x2,) = _rows(lambda i, n, *v: _f_post_ffn(*v), T, tm, [_cur(x1), _cur(dn)], [_cst(g_fpost)], [_out(D_MODEL, F32)], [], "post_ffn_fwd")
        saved.append(dict(lw=lw, x=xl, h1=h1, zin=zin, qlat=qlat, kvlat=kvlat, qr=qr, kr=kr, kvpad=kvpad, o=o, xbc=xbc,
                          yscan=yscan, states=states, ycat=ycat, mixed=mixed, x1=x1, h2=h2, upre=upre, act=act, dn=dn,
                          g_pre=g_pre, g_post=g_post, g_fpre=g_fpre, g_fpost=g_fpost, qn=qn, kvn=kvn, ssd_b=ssd_b,
                          ssd_par=ssd_par, ssd_nw=ssd_nw, ffn_b=ffn_b))
        xl = x2

    gx, loss_part = _rows(_k_loss, T, tm, [_cur(xl), _cur(target)], [], [_out(D_MODEL, F32)], [_acc(1, LANE)], "loss_head")

    GW = {k: [None] * DEPTH for k in ("w_in", "mla_w_q_up", "mla_w_kv_up", "sc_conv_w", "ssd_conv_w", "w_out", "ffn_w_up",
                                      "ffn_conv_w", "ffn_w_down")}
    GS = {k: [None] * DEPTH for k in ("norm_mix_pre", "norm_mix_post", "norm_ffn_pre", "norm_ffn_post", "mla_q_norm", "mla_kv_norm",
                                      "ssd_conv_b", "ssd_dt_bias", "ssd_a_log", "ssd_d", "ssd_norm", "ffn_conv_b")}
    nt = FFN_DIM // FFN_TILE
    gcol, ucol = (lambda j: j), (lambda j: j + nt)
    pending = None
    for l in reversed(range(DEPTH)):
        s = saved[l]
        lw = s["lw"]
        gx1, ddn, dgf = _rows_vjp(_f_post_ffn, T, tm, [s["x1"], s["dn"]], [s["g_fpost"]], [gx], [F32, BF16], "post_ffn_bwd")
        GS["norm_ffn_post"][l] = dgf[0]
        dact = _mm(ddn, lw["w_down"], "nt", F32, "mm_down_dx")
        GW["ffn_w_down"][l] = _mm(s["act"], ddn, "tn", BF16, "mm_down_dw")
        up = s["upre"]
        dug, duu, dwg, dwu, dbg, dbu = _rows(
            _k_ffnact_bwd, T, tm_ffn,
            [(up, FFN_TILE, gcol, "cur"), (up, FFN_TILE, ucol, "cur"), (dact, FFN_TILE, gcol, "cur"), (up, FFN_TILE, gcol, "prev"),
             (up, FFN_TILE, ucol, "prev"), (up, FFN_TILE, gcol, "next"), (up, FFN_TILE, ucol, "next"), (dact, FFN_TILE, gcol, "next")],
            [(lw["ffn_w"], FFN_TILE, gcol), (lw["ffn_w"], FFN_TILE, ucol), (s["ffn_b"], FFN_TILE, gcol), (s["ffn_b"], FFN_TILE, ucol)],
            [(FFN_DIM, BF16, FFN_TILE, gcol)] * 2,
            [(HALO, FFN_DIM, FFN_TILE, gcol)] * 2 + [(1, FFN_DIM, FFN_TILE, gcol)] * 2, "ffn_act_bwd", ncol=nt)
        GW["ffn_conv_w"][l] = jnp.concatenate([dwg[:3], dwu[:3]], axis=1)
        GS["ffn_conv_b"][l] = jnp.concatenate([dbg[0], dbu[0]])
        dh2 = _mm((dug, duu), lw["w_up"], "nt", F32, "mm_up_dx")
        GW["ffn_w_up"][l] = (_mm(s["h2"], dug, "tn", BF16, "mm_up_dw_gate"), _mm(s["h2"], duu, "tn", BF16, "mm_up_dw_up"))
        gx0, dmixed, dgp, dgf = _rows_vjp(_f_post_mix, T, tm, [s["x"], s["mixed"]], [s["g_post"], s["g_fpre"]], [gx1, dh2],
                                          [F32, BF16], "post_mix_bwd")
        GS["norm_mix_post"][l], GS["norm_ffn_pre"][l] = dgp[0], dgf[0]
        dycat = _mm(dmixed, lw["w_out"], "nt", F32, "mm_out_dx")
        GW["w_out"][l] = _unpad_out_rows(_mm(s["ycat"], dmixed, "tn", BF16, "mm_out_dw"))
        zin = s["zin"]
        dyscan, dz, dnw = _rows(_vjp_wrap(_f_ssd_gate, 2, 1), T, tm,
                                [_cur(s["yscan"]), _cur(zin, SSD_DIM, Z_SSZ // SSD_DIM), _cur(dycat, SSD_DIM, (HEADS * HP + SC_DIM) // SSD_DIM)],
                                [_cst(s["ssd_nw"])], [_out(SSD_DIM, F32), _out(SSD_DIM, BF16)], [_acc(1, SSD_DIM)], "ssd_gate_bwd")
        GS["ssd_norm"][l] = dnw[0]
        dxbc, ddtraw, dpar = _ssd_bwd(s["xbc"], zin, s["ssd_par"], s["states"], dyscan, T, Z_DT // LANE)
        GS["ssd_dt_bias"][l], GS["ssd_a_log"][l], GS["ssd_d"][l] = dpar[0, :SSD_HEADS], dpar[1, :SSD_HEADS], dpar[2, :SSD_HEADS]
        xb = Z_XBC // SSD_CONV_DIM
        dxraw, dsw, dsb = _rows(_k_ssdconv_bwd, T, tm,
                                [_cur(zin, SSD_CONV_DIM, xb), _cur(dxbc), _halo(zin, "prev", SSD_CONV_DIM, xb),
                                 _halo(zin, "next", SSD_CONV_DIM, xb), _halo(dxbc, "next")],
                                [_cst(lw["ssd_w"]), _cst(s["ssd_b"])], [_out(SSD_CONV_DIM, BF16)],
                                [_acc(HALO, SSD_CONV_DIM), _acc(1, SSD_CONV_DIM)], "ssd_conv_bwd")
        GW["ssd_conv_w"][l] = dsw[:4]
        GS["ssd_conv_b"][l] = dsb[0]
        cb = (HEADS * HP) // SC_DIM
        dscb, dscc, dsch, dscw = _rows(_k_sconv_bwd, T, tm,
                                       [_cur(zin, SC_DIM, Z_SCB // SC_DIM), _cur(zin, SC_DIM, Z_SCC // SC_DIM),
                                        _cur(zin, SC_DIM, Z_SCH // SC_DIM), _cur(dycat, SC_DIM, cb),
                                        _halo(zin, "prev", SC_DIM, Z_SCC // SC_DIM), _halo(zin, "prev", SC_DIM, Z_SCH // SC_DIM),
                                        _halo(zin, "next", SC_DIM, Z_SCB // SC_DIM), _halo(dycat, "next", SC_DIM, cb)],
                                       [_cst(lw["sc_w"])], [_out(SC_DIM, BF16)] * 3, [_acc(HALO, SC_DIM)], "short_conv_bwd")
        GW["sc_conv_w"][l] = dscw[:3]
        if ex is None:
            dq, dk, dv = _flash_bwd(s["qr"], s["kr"], s["kvpad"], s["o"], dycat, T)
        else:
            late = ex.submit({n: GW[n][l] for ns in LATE for n in ns}, "late")
            dq, dk, dv, parts = _flash_bwd(s["qr"], s["kr"], s["kvpad"], s["o"], dycat, T, carry=late + (pending or []))
            ex.collect(l, "late", parts[:len(late)])
            if pending:
                ex.collect(l + 1, "early", parts[len(late):])
        dqpad, dkvpad, dkr = _rows(_k_rope_bwd, T, tm, [_cur(dq), _cur(dk), _cur(dv), _cur(cosf), _cur(sina), _cur(sinb)], [],
                                   [_out(HEADS * HP, BF16), _out(2 * HEADS * HP, BF16), _out(LANE, BF16)], [], "mla_rope_bwd")
        dqlat = _mm(dqpad, lw["w_q"], "nt", F32, "mm_q_dx")
        GW["mla_w_q_up"][l] = _unpad_heads(_mm(s["qlat"], dqpad, "tn", BF16, "mm_q_dw"), NOPE + ROPE)
        dkvlat = _mm(dkvpad, lw["w_kv"], "nt", F32, "mm_kv_dx")
        GW["mla_w_kv_up"][l] = _unpad_kv(_mm(s["kvlat"], dkvpad, "tn", BF16, "mm_kv_dw"))
        dcq, dckv, dqn, dkvn = _rows(_vjp_wrap(_f_mla_pre, 2, 2), T, tm,
                                     [_cur(zin, Q_LORA, 0), _cur(zin, KV_LORA, Z_CKV // KV_LORA), _cur(dqlat), _cur(dkvlat)],
                                     [_cst(s["qn"]), _cst(s["kvn"])], [_out(Q_LORA, BF16), _out(KV_LORA, BF16)],
                                     [_acc(1, Q_LORA), _acc(1, KV_LORA)], "mla_pre_bwd")
        GS["mla_q_norm"][l], GS["mla_kv_norm"][l] = dqn[0], dkvn[0]
        dzin = jnp.concatenate([dcq, dckv, dkr, dscb, dscc, dsch, dz, dxraw, ddtraw.astype(BF16), jnp.zeros((T, ZIN - Z_DT - LANE), BF16)], axis=1)
        dh1 = _mm(dzin, lw["w_in"], "nn", F32, "mm_in_dx")
        GW["w_in"][l] = _unpad_rows_in(_mm(dzin, s["h1"], "tn", BF16, "mm_in_dw"))
        gx, dgp = _rows(_vjp_wrap(_f_premix, 1, 1, add_first=True), T, tm, [_cur(s["x"]), _cur(dh1), _cur(gx0)], [_cst(s["g_pre"])],
                        [_out(D_MODEL, F32)], [_acc(1, D_MODEL)], "pre_mix_bwd")
        GS["norm_mix_pre"][l] = dgp[0]
        if ex is not None:
            pending = ex.submit({n: GW[n][l] for ns in EARLY for n in ns}, "early")
    if ex is not None:
        ex.collect(0, "early", _rs_chip_exchange(pending))
    GS = {k: jnp.stack(v) for k, v in GS.items()}
    return loss_part[0, 0], gx, GW, GS


WEIGHTS = ("norm_mix_pre", "norm_mix_post", "norm_ffn_pre", "norm_ffn_post", "w_in", "mla_q_norm", "mla_w_q_up", "mla_kv_norm",
           "mla_w_kv_up", "sc_conv_w", "ssd_conv_w", "ssd_conv_b", "ssd_dt_bias", "ssd_a_log", "ssd_d", "ssd_norm", "w_out",
           "ffn_w_up", "ffn_conv_w", "ffn_conv_b", "ffn_w_down")
SHARDED = (("w_in", 2), ("mla_w_q_up", 2), ("mla_w_kv_up", 2), ("sc_conv_w", 2), ("ssd_conv_w", 2), ("w_out", 1),
           ("ffn_w_up", 2), ("ffn_conv_w", 2), ("ffn_w_down", 1))
SMALL = tuple(n for n in WEIGHTS if n not in dict(SHARDED))
N_CHIPS = 4
N_DEV = 8
ROW_ALIGN = 64
SLAB_ALIGN = 16
EARLY = (("w_in", "mla_w_q_up", "mla_w_kv_up", "sc_conv_w", "ssd_conv_w"),)
LATE = (("ffn_w_down", "w_out"), ("ffn_w_up", "ffn_conv_w"))
TRANSPOSED = ("w_in",)


def _is_rows(shape, width):
    return shape[-1] == width and math.prod(shape[:-1]) % SLAB_ALIGN == 0


def _is_short(shape, width):
    return len(shape) == 2 and shape[1] == width and not _is_rows(shape, width)


def _slab_rows(shape, width):
    if _is_rows(shape, width):
        return math.prod(shape[:-1])
    if _is_short(shape, width):
        return -(-shape[0] // SLAB_ALIGN) * SLAB_ALIGN
    return -(-math.prod(shape) // (width * SLAB_ALIGN)) * SLAB_ALIGN


def _slab(piece, width, dtype, lead=0):
    ld, shape = piece.shape[:lead], piece.shape[lead:]
    rows = _slab_rows(shape, width)
    if _is_rows(shape, width):
        return piece.astype(dtype).reshape(ld + (rows, width))
    if _is_short(shape, width):
        return jnp.pad(piece.astype(dtype), [(0, 0)] * lead + [(0, rows - shape[0]), (0, 0)])
    flat = piece.astype(dtype).reshape(ld + (-1,))
    return jnp.pad(flat, [(0, 0)] * lead + [(0, rows * width - flat.shape[-1])]).reshape(ld + (rows, width))


def _unslab(slab, shape, lead=0):
    ld = slab.shape[:lead]
    if _is_rows(shape, slab.shape[-1]):
        return slab.reshape(ld + tuple(shape))
    if _is_short(shape, slab.shape[-1]):
        return slab[..., :shape[0], :]
    return slab.reshape(ld + (-1,))[..., :math.prod(shape)].reshape(ld + tuple(shape))


def _layout(shapes, names, width):
    ents, off = [], 0
    for n in names:
        shp = tuple(shapes[n])
        todo = [(None, False, shp), (None, True, shp)] if n.endswith("conv_w") else [(l, False, shp[1:]) for l in range(shp[0])]
        for l, lo, ps in todo:
            r = _slab_rows(ps, width)
            ents.append((n, l, lo, ps, off, r))
            off += r
    return width, -(-off // ROW_ALIGN) * ROW_ALIGN, ents


def _pack(layout, piece, dtype, lead=0):
    width, rows, ents = layout
    slabs, ld = [], None
    for n, l, lo, ps, off, r in ents:
        p = piece(n, l, lo)
        slabs.append(None if p is None else _slab(p, width, dtype, lead))
        ld = ld if p is None else p.shape[:lead]
    used = ents[-1][4] + ents[-1][5]
    slabs = [jnp.zeros(ld + (e[5], width), dtype) if s is None else s for s, e in zip(slabs, ents)]
    if rows > used:
        slabs.append(jnp.zeros(ld + (rows - used, width), dtype))
    return jnp.concatenate(slabs, axis=lead)


ANY = pl.BlockSpec(memory_space=pl.ANY)


def _pos():
    return lax.axis_index("x"), lax.axis_index("y"), lax.axis_index("c")


def _other_chips(x, y):
    return ((1 - x, y), (x, 1 - y), (1 - x, 1 - y))


def _remote(src, dst, ssem, rsem, dev):
    return pltpu.make_async_remote_copy(src_ref=src, dst_ref=dst, send_sem=ssem, recv_sem=rsem, device_id=dev, device_id_type=MESH)


AG_CHUNKS = 2


def _chip_index():
    return 2 * lax.axis_index("x") + lax.axis_index("y")


def _ag_sems(nbuf):
    return [pltpu.SemaphoreType.DMA((nbuf * 3 * AG_CHUNKS,))] * 4


def _ag_plan(w_refs, out_refs, sems):
    isend, irecv, dsend, drecv = sems
    x, y, c = _pos()
    k = 2 * x + y
    sib = (x, y, 1 - c)
    sends, lands, forwards, finals = [], [], [], []
    s = 0
    for w_ref, out_ref in zip(w_refs, out_refs):
        H = w_ref.shape[0] // 2
        CH = H // AG_CHUNKS
        for cx, cy in _other_chips(x, y):
            for ch in range(AG_CHUNKS):
                mine = out_ref.at[k, pl.ds(c * H + ch * CH, CH), :]
                near = out_ref.at[2 * cx + cy, pl.ds(c * H + ch * CH, CH), :]
                far = out_ref.at[2 * cx + cy, pl.ds((1 - c) * H + ch * CH, CH), :]
                sends.append(_remote(w_ref.at[pl.ds(c * H + ch * CH, CH), :], mine, isend.at[s], irecv.at[s], (cx, cy, c)))
                lands.append(_remote(near, near, isend.at[s], irecv.at[s], (cx, cy, c)))
                forwards.append(_remote(near, near, dsend.at[s], drecv.at[s], sib))
                finals.append(_remote(far, far, dsend.at[s], drecv.at[s], sib))
                s += 1
    return sends, lands, forwards, finals


def _own_slot(got, own):
    return lax.dynamic_update_slice(got, own[None], (_chip_index(), 0, 0))


def _all_gather_weights(ws):
    nb = len(ws)

    def body(*refs):
        sends, lands, forwards, finals = _ag_plan(refs[:nb], refs[nb:2 * nb], refs[2 * nb:])
        for cp in sends:
            cp.start()
        for land, fw in zip(lands, forwards):
            land.wait_recv()
            fw.start()
        for cp in finals:
            cp.wait_recv()
        for cp in sends + forwards:
            cp.wait_send()

    got = pl.pallas_call(
        body, name="all_gather_weights", in_specs=[ANY] * nb, out_specs=[ANY] * nb,
        out_shape=[jax.ShapeDtypeStruct((N_CHIPS,) + w.shape, w.dtype) for w in ws], scratch_shapes=_ag_sems(nb),
    )(*ws)
    return [_own_slot(g, w) for g, w in zip(got, ws)]


def _rs_pair_exchange(gs):
    nb = len(gs)

    def body(*refs):
        g_refs, got_refs, (ssem, rsem) = refs[:nb], refs[nb:2 * nb], refs[2 * nb:]
        x, y, c = _pos()
        cps = []
        for b, (g_ref, got_ref) in enumerate(zip(g_refs, got_refs)):
            H = g_ref.shape[1] // 2
            for kk in range(N_CHIPS):
                s = b * N_CHIPS + kk
                cps.append(_remote(g_ref.at[kk, pl.ds((1 - c) * H, H), :], got_ref.at[kk], ssem.at[s], rsem.at[s], (x, y, 1 - c)))
        for cp in cps:
            cp.start()
        for cp in cps:
            cp.wait()

    return pl.pallas_call(
        body, name="rs_pair_exchange", in_specs=[ANY] * nb, out_specs=[ANY] * nb,
        out_shape=[jax.ShapeDtypeStruct((N_CHIPS, g.shape[1] // 2, g.shape[2]), g.dtype) for g in gs],
        scratch_shapes=[pltpu.SemaphoreType.DMA((nb * N_CHIPS,))] * 2,
    )(*gs)


def _chip_sems(nbuf):
    return [pltpu.SemaphoreType.DMA((nbuf * 3,))] * 2


def _chip_plan(p_refs, out_refs, sems):
    ssem, rsem = sems
    x, y, c = _pos()
    sends, lands = [], []
    s = 0
    for p_ref, out_ref in zip(p_refs, out_refs):
        for cx, cy in _other_chips(x, y):
            sends.append(_remote(p_ref.at[2 * cx + cy], out_ref.at[2 * x + y], ssem.at[s], rsem.at[s], (cx, cy, c)))
            land = out_ref.at[2 * cx + cy]
            lands.append(_remote(land, land, ssem.at[s], rsem.at[s], (cx, cy, c)))
            s += 1
    return sends, lands


def _chip_parts(got, ps):
    k = _chip_index()
    return [lax.dynamic_update_slice(g, lax.dynamic_slice_in_dim(p, k, 1, axis=0), (k, 0, 0)) for g, p in zip(got, ps)]


def _rs_chip_exchange(ps):
    nb = len(ps)

    def body(*refs):
        sends, lands = _chip_plan(refs[:nb], refs[nb:2 * nb], refs[2 * nb:])
        for cp in sends:
            cp.start()
        for cp in lands:
            cp.wait_recv()
        for cp in sends:
            cp.wait_send()

    got = pl.pallas_call(
        body, name="rs_chip_exchange", in_specs=[ANY] * nb, out_specs=[ANY] * nb,
        out_shape=[jax.ShapeDtypeStruct(p.shape, p.dtype) for p in ps], scratch_shapes=_chip_sems(nb),
    )(*ps)
    return _chip_parts(got, ps)


def _rs_pair_share(fs):
    nb = len(fs)

    def body(*refs):
        f_refs, out_refs, (ssem, rsem) = refs[:nb], refs[nb:2 * nb], refs[2 * nb:]
        x, y, c = _pos()
        sends, lands = [], []
        for b, (f_ref, out_ref) in enumerate(zip(f_refs, out_refs)):
            sends.append(_remote(f_ref, out_ref.at[c], ssem.at[b], rsem.at[b], (x, y, 1 - c)))
            land = out_ref.at[1 - c]
            lands.append(_remote(land, land, ssem.at[b], rsem.at[b], (x, y, 1 - c)))
        for cp in sends:
            cp.start()
        for cp in lands:
            cp.wait_recv()
        for cp in sends:
            cp.wait_send()

    got = pl.pallas_call(
        body, name="rs_pair_share", in_specs=[ANY] * nb, out_specs=[ANY] * nb,
        out_shape=[jax.ShapeDtypeStruct((2,) + f.shape, f.dtype) for f in fs],
        scratch_shapes=[pltpu.SemaphoreType.DMA((nb,))] * 2,
    )(*fs)
    return [lax.dynamic_update_slice(g, f[None], (lax.axis_index("c"), 0, 0)) for g, f in zip(got, fs)]


def _all_reduce_small(s):
    r, C = s.shape

    def body(s_ref, o_ref, buf, ssem, rsem):
        x, y, c = _pos()
        me = 4 * x + 2 * y + c
        buf[me] = s_ref[...]
        cps = []
        for m in range(1, N_DEV):
            mx, my, mc = (m >> 2) & 1, (m >> 1) & 1, m & 1
            peer = (x ^ mx, y ^ my, c ^ mc)
            cp = _remote(s_ref, buf.at[me], ssem.at[m - 1], rsem.at[m - 1], peer)
            cp.start()
            cps.append(cp)
        for m in range(1, N_DEV):
            mx, my, mc = (m >> 2) & 1, (m >> 1) & 1, m & 1
            src = 4 * (x ^ mx) + 2 * (y ^ my) + (c ^ mc)
            _remote(s_ref, buf.at[src], ssem.at[m - 1], rsem.at[m - 1], (x ^ mx, y ^ my, c ^ mc)).wait_recv()
        for cp in cps:
            cp.wait_send()
        acc = buf[0]
        for j in range(1, N_DEV):
            acc = acc + buf[j]
        o_ref[...] = acc

    return pl.pallas_call(
        body, name="all_reduce_small", in_specs=[pl.BlockSpec(memory_space=pltpu.VMEM)],
        out_specs=pl.BlockSpec(memory_space=pltpu.VMEM), out_shape=jax.ShapeDtypeStruct((r, C), F32),
        scratch_shapes=[pltpu.VMEM((N_DEV, r, C), F32), pltpu.SemaphoreType.DMA((N_DEV - 1,)), pltpu.SemaphoreType.DMA((N_DEV - 1,))],
    )(s)


def _rtile(n, pref):
    if n <= pref:
        return n
    t = (pref // 16) * 16
    while t >= 16:
        if n % t == 0:
            return t
        t -= 16
    raise ValueError(f"no row tile for {n}")


def _rs_pair_sums(gpks):
    gots = _rs_pair_exchange(gpks)
    out = []
    for gpk, got in zip(gpks, gots):
        _, R, C = gpk.shape
        H = R // 2
        own = lax.dynamic_index_in_dim(gpk.reshape(N_CHIPS, 2, H, C), lax.axis_index("c"), axis=1, keepdims=False)
        (part,) = _rows(lambda i, n, a, b: (a.astype(F32) + b.astype(F32),), N_CHIPS * H, _rtile(N_CHIPS * H, 512),
                        [_cur(own.reshape(N_CHIPS * H, C)), _cur(got.reshape(N_CHIPS * H, C))], [], [_out(C, BF16)], [], "rs_pair_add")
        out.append(part.reshape(N_CHIPS, H, C))
    return out


def _rs_chip_sums(parts):
    def add4(i, n, a, b, c, d):
        return (((a.astype(F32) + b.astype(F32)) + c.astype(F32)) + d.astype(F32),)

    out = []
    for p in parts:
        _, H, C = p.shape
        tm = _rtile(H, 1024)
        (red,) = _rows(add4, H, tm, [(p.reshape(N_CHIPS * H, C), C, functools.partial(_const, v=0), j * (H // tm)) for j in range(N_CHIPS)],
                       [], [_out(C, F32)], [], "rs_chip_add")
        out.append(red)
    return out


class _Exchange:
    def __init__(self, a):
        self.a = a
        self.axis = {n: (1 if n in TRANSPOSED else ax) for n, ax in SHARDED}
        shapes = {n: (1,) + tuple(self.packed(n, a[n]).shape[1:]) for n in self.axis}
        widths = lambda names: shapes[names[0]][-1] if names[0] == "ffn_w_up" else PACK_COLS
        self.layouts = {"early": [_layout(shapes, ns, widths(ns)) for ns in EARLY], "late": [_layout(shapes, ns, widths(ns)) for ns in LATE]}
        self.reduced = {}

    @staticmethod
    def packed(n, w):
        return jnp.swapaxes(w, -1, -2) if n in TRANSPOSED else w

    def shard(self, l, group):
        def piece(n, li, lo):
            w = self.packed(n, self.a[n][l:l + 1] if li is None else self.a[n][l])
            return w - w.astype(BF16).astype(F32) if lo else w
        return [_pack(lay, piece, BF16) for lay in self.layouts[group]]

    def weights(self, gathered, group):
        W, resid = {}, {}
        for (width, rows, ents), g in zip(self.layouts[group], gathered):
            for n, li, lo, ps, off, r in ents:
                parts = _unslab(g[:, off:off + r], ps, lead=1)
                ax = self.axis[n] + (1 if li is None else 0)
                full = jnp.moveaxis(parts, 0, ax - 1)
                full = full.reshape(full.shape[:ax - 1] + (-1,) + full.shape[ax + 1:])
                (resid if lo else W)[n] = full[0] if li is None else full
        for n in resid:
            W[n] = W[n].astype(F32) + resid[n].astype(F32)
        return W

    def submit(self, GW, group):
        def by_chip(g, ax, parts=N_CHIPS):
            g = g.reshape(g.shape[:ax] + (parts, g.shape[ax] // parts) + g.shape[ax + 1:])
            return jnp.moveaxis(g, ax, 0)

        def piece(n, li, lo):
            if lo:
                return None
            g = GW[n]
            if isinstance(g, tuple):
                return jnp.concatenate([by_chip(h, self.axis[n] - 1, N_CHIPS // 2) for h in g])
            return by_chip(g[None], self.axis[n]) if li is None else by_chip(g, self.axis[n] - 1)

        return _rs_pair_sums([_pack(lay, piece, BF16, lead=1) for lay in self.layouts[group]])

    def collect(self, l, group, parts):
        self.reduced[l, group] = _rs_chip_sums(parts)

    def finish(self):
        keys = [(l, g) for l in range(DEPTH) for g in self.layouts]
        flat = _rs_pair_share([f for key in keys for f in self.reduced[key]])
        both, at = {}, 0
        for key in keys:
            both[key] = flat[at:at + len(self.layouts[key[1]])]
            at += len(self.layouts[key[1]])
        grads = {}
        for group, lays in self.layouts.items():
            for b, (width, rows, ents) in enumerate(lays):
                for n, li, lo, ps, off, r in ents:
                    if not lo:
                        per_layer = [self.packed(n, _unslab(both[l, group][b].reshape(rows, width)[off:off + r], ps)) for l in range(DEPTH)]
                        grads[n] = jnp.concatenate(per_layer) if li is None else jnp.stack(per_layer)
        return grads


def _adam(w, g, m, v, name, g_row=0):
    shp = w.shape
    two = lambda a: a.reshape(-1, shp[-1])
    rows = math.prod(shp[:-1])
    tm = _rtile(rows, 256)
    assert g_row % tm == 0
    g_in = (two(g), shp[-1], functools.partial(_const, v=0), g_row // tm)
    res = _rows(_k_adam, rows, tm, [_cur(two(w)), g_in, _cur(two(m)), _cur(two(v))], [], [_out(shp[-1], F32)] * 4, [], name)
    return tuple(r.reshape(shp) for r in res)


def _pack_flat(parts, rows):
    flat = jnp.concatenate([p.astype(F32).reshape(-1) for p in parts])
    return jnp.pad(flat, (0, rows * PACK_COLS - flat.shape[0])).reshape(rows, PACK_COLS)


def _unpack_flat(buf, shapes):
    flat, out, off = buf.reshape(-1), [], 0
    for shp in shapes:
        n = math.prod(shp)
        out.append(flat[off:off + n].reshape(shp))
        off += n
    return out


def kernel(x, positions, norm_mix_pre, norm_mix_post, norm_ffn_pre, norm_ffn_post, w_in, mla_q_norm, mla_w_q_up, mla_kv_norm, mla_w_kv_up, sc_conv_w, ssd_conv_w, ssd_conv_b, ssd_dt_bias, ssd_a_log, ssd_d, ssd_norm, w_out, ffn_w_up, ffn_conv_w, ffn_conv_b, ffn_w_down, loss_target, m_norm_mix_pre, m_norm_mix_post, m_norm_ffn_pre, m_norm_ffn_post, m_w_in, m_mla_q_norm, m_mla_w_q_up, m_mla_kv_norm, m_mla_w_kv_up, m_sc_conv_w, m_ssd_conv_w, m_ssd_conv_b, m_ssd_dt_bias, m_ssd_a_log, m_ssd_d, m_ssd_norm, m_w_out, m_ffn_w_up, m_ffn_conv_w, m_ffn_conv_b, m_ffn_w_down, v_norm_mix_pre, v_norm_mix_post, v_norm_ffn_pre, v_norm_ffn_post, v_w_in, v_mla_q_norm, v_mla_w_q_up, v_mla_kv_norm, v_mla_w_kv_up, v_sc_conv_w, v_ssd_conv_w, v_ssd_conv_b, v_ssd_dt_bias, v_ssd_a_log, v_ssd_d, v_ssd_norm, v_w_out, v_ffn_w_up, v_ffn_conv_w, v_ffn_conv_b, v_ffn_w_down):
    a = dict(locals())
    ex = _Exchange(a)
    S = {n: a[n] for n in SMALL}
    loss_part, gx, _, GS = _local_step(a["x"][0], a["positions"][0], a["loss_target"][0], None, S, ex)

    grads, delta, new_m, new_v = {}, {}, {}, {}
    for n, g in ex.finish().items():
        grads[n], delta[n], new_m[n], new_v[n] = _adam(a[n], g, a["m_" + n], a["v_" + n], "adamw_" + n)

    small_shapes = [a[n].shape for n in SMALL]
    rs = -(-(sum(math.prod(s) for s in small_shapes) + 1) // (PACK_COLS * SLAB_ALIGN)) * SLAB_ALIGN
    red = _all_reduce_small(_pack_flat([GS[n] for n in SMALL] + [loss_part.reshape(1)], rs))
    loss = _unpack_flat(red, small_shapes + [(1,)])[-1][0]
    pk = lambda pre: _pack_flat([a[pre + n] for n in SMALL], rs)
    for dst, buf in zip((grads, delta, new_m, new_v), _adam(pk(""), red, pk("m_"), pk("v_"), "adamw_small")):
        dst.update(zip(SMALL, _unpack_flat(buf, small_shapes)))

    return (loss, gx[None], *[grads[n] for n in WEIGHTS], *[delta[n] for n in WEIGHTS], *[new_m[n] for n in WEIGHTS],
            *[new_v[n] for n in WEIGHTS])
```

```python
import functools
import math

import jax
import jax.numpy as jnp
from jax import lax
from jax.experimental import pallas as pl
from jax.experimental.pallas import tpu as pltpu

F32 = jnp.float32
BF16 = jnp.bfloat16
MXU_DTYPE = jnp.bfloat16
HIGHEST = lax.Precision.HIGHEST
MESH = pl.DeviceIdType.MESH

D_MODEL = 1024
DEPTH = 4
HEADS = 8
Q_LORA = 256
KV_LORA = 128
NOPE = 64
ROPE = 32
VDIM = 64
ROPE_THETA = 10000.0
SC_DIM = 256
SSD_HEADS = 4
SSD_HEAD_DIM = 64
SSD_STATE = 128
SSD_DIM = 256
SSD_CONV_DIM = 768
SSD_CHUNK = 128
FFN_DIM = 2816
NORM_EPS = 1e-6
QK_SCALE = (NOPE + ROPE) ** -0.5
LANE = 128
HP = 128
FLASH_HEADS = 2

ZIN = 2560
Z_CQ, Z_CKV, Z_KR, Z_SCB, Z_SCC, Z_SCH, Z_SSZ, Z_XBC, Z_DT = 0, 256, 384, 512, 768, 1024, 1280, 1536, 2304
KR_LANE = 64
YCAT = HEADS * HP + SC_DIM + SSD_DIM
FFN_TILE = 256
FFN_ROWS = 1024
ROW_BLOCK = 512

ADAM_LR, ADAM_B1, ADAM_B2, ADAM_EPS, ADAM_WD, ADAM_STEP = 0.001, 0.9, 0.999, 1e-08, 0.01, 10

PACK_COLS = 1024


def _tile(n, pref):
    if n <= pref:
        return n
    t = (pref // LANE) * LANE
    while t >= LANE:
        if n % t == 0:
            return t
        t -= LANE
    raise ValueError(f"no tile for {n}")


MM_TM, MM_TN, MM_TK = 1024, 1408, 1536


def _mm(a, b, mode, out_dtype, name, tm=None, tn=MM_TN, tkmax=MM_TK):
    pair = isinstance(a, tuple)
    a_list = list(a) if pair else [a]
    layer = None
    if isinstance(b, tuple):
        b, layer = b
    bshape = b.shape[-2:]
    if mode == "nn":
        (M, Ka), (_, N) = a_list[0].shape, bshape
    elif mode == "nt":
        (M, Ka), (N, _) = a_list[0].shape, bshape
    else:
        (Ka, M), (_, N) = a_list[0].shape, bshape
    tm = (MM_TN if mode == "tn" else MM_TM) if tm is None else tm
    tm, tn, tk = _tile(M, tm), _tile(N, tn), _tile(Ka, tkmax)
    nka = Ka // tk
    nk = nka * len(a_list)

    def bspec(shape, index):
        if layer is None:
            return pl.BlockSpec(shape, index)
        return pl.BlockSpec((None,) + shape, lambda i, j, k: (layer,) + index(i, j, k))

    if mode == "nn":
        a_specs = [pl.BlockSpec((tm, tk), lambda i, j, k: (i, jnp.minimum(k, nka - 1))),
                   pl.BlockSpec((tm, tk), lambda i, j, k: (i, jnp.maximum(k - nka, 0)))][:len(a_list)]
        b_spec = bspec((tk, tn), lambda i, j, k: (k, j))
        dims = NN
    elif mode == "nt":
        a_specs = [pl.BlockSpec((tm, tk), lambda i, j, k: (i, jnp.minimum(k, nka - 1))),
                   pl.BlockSpec((tm, tk), lambda i, j, k: (i, jnp.maximum(k - nka, 0)))][:len(a_list)]
        b_spec = bspec((tn, tk), lambda i, j, k: (j, k))
        dims = NT
    else:
        a_specs = [pl.BlockSpec((tk, tm), lambda i, j, k: (k, i))]
        b_spec = pl.BlockSpec((tk, tn), lambda i, j, k: (k, j))
        dims = TN
    na = len(a_list)

    def body(*refs):
        a_refs, b_ref, o_ref = refs[:na], refs[na], refs[na + 1]
        k = pl.program_id(2)

        def prod(a_ref):
            return lax.dot_general(a_ref[...].astype(MXU_DTYPE), b_ref[...].astype(MXU_DTYPE), dims, preferred_element_type=F32)

        if nk == 1:
            o_ref[...] = prod(a_refs[0]).astype(o_ref.dtype)
            return
        acc_ref = refs[na + 2]

        @pl.when(k == 0)
        def _():
            acc_ref[...] = prod(a_refs[0])

        @pl.when((k > 0) & (k < nka))
        def _():
            acc_ref[...] += prod(a_refs[0])

        if pair:
            @pl.when(k >= nka)
            def _():
                acc_ref[...] += prod(a_refs[1])

        @pl.when(k == nk - 1)
        def _():
            o_ref[...] = acc_ref[...].astype(o_ref.dtype)

    return pl.pallas_call(
        body, name=name, grid=(M // tm, N // tn, nk),
        in_specs=a_specs + [b_spec], out_specs=pl.BlockSpec((tm, tn), lambda i, j, k: (i, j)),
        out_shape=jax.ShapeDtypeStruct((M, N), out_dtype),
        scratch_shapes=[pltpu.VMEM((tm, tn), F32)] if nk > 1 else [],
        compiler_params=pltpu.CompilerParams(dimension_semantics=("parallel", "parallel", "arbitrary")),
    )(*a_list, b)


HALO = 8


def _const(j, v):
    return v


def _rows(fn, T, tm, ins, consts, outs, accs, name, ncol=1):
    n = T // tm
    hb = tm // HALO
    last = T // HALO - 1
    in_specs, args = [], []
    for arr, bc, cb, kind in ins:
        if isinstance(kind, int):
            in_specs.append(pl.BlockSpec((tm, bc), lambda j, i, cb=cb, off=kind: (i + off, cb(j))))
        elif kind == "cur":
            in_specs.append(pl.BlockSpec((tm, bc), lambda j, i, cb=cb: (i, cb(j))))
        elif kind == "prev":
            in_specs.append(pl.BlockSpec((HALO, bc), lambda j, i, cb=cb: (jnp.maximum(i * hb - 1, 0), cb(j))))
        else:
            in_specs.append(pl.BlockSpec((HALO, bc), lambda j, i, cb=cb: (jnp.minimum((i + 1) * hb, last), cb(j))))
        args.append(arr)
    for arr, bc, cb in consts:
        in_specs.append(pl.BlockSpec((arr.shape[0], bc), lambda j, i, cb=cb: (0, cb(j))))
        args.append(arr)
    out_specs, out_shape = [], []
    for tc, dt, bc, cb in outs:
        out_specs.append(pl.BlockSpec((tm, bc), lambda j, i, cb=cb: (i, cb(j))))
        out_shape.append(jax.ShapeDtypeStruct((T, tc), dt))
    for r, tc, bc, cb in accs:
        out_specs.append(pl.BlockSpec((r, bc), lambda j, i, cb=cb: (0, cb(j))))
        out_shape.append(jax.ShapeDtypeStruct((r, tc), F32))
    nin, nout, nacc = len(args), len(outs), len(accs)

    def body(*refs):
        i = pl.program_id(1)
        res = fn(i, n, *[r[...] for r in refs[:nin]])
        for r, v in zip(refs[nin:nin + nout], res[:nout]):
            r[...] = v.astype(r.dtype)
        if nacc:
            acc_refs = refs[nin + nout:nin + nout + nacc]

            @pl.when(i == 0)
            def _():
                for r in acc_refs:
                    r[...] = jnp.zeros_like(r)

            for r, v in zip(acc_refs, res[nout:]):
                r[...] += v.astype(F32)

    res = pl.pallas_call(
        body, name=name, grid=(ncol, n), in_specs=in_specs, out_specs=out_specs, out_shape=out_shape,
        compiler_params=pltpu.CompilerParams(dimension_semantics=("arbitrary", "arbitrary")),
    )(*args)
    return res


def _cur(arr, bc=None, blk=0):
    bc = arr.shape[1] if bc is None else bc
    return (arr, bc, functools.partial(_const, v=blk), "cur")


def _halo(arr, kind, bc=None, blk=0):
    bc = arr.shape[1] if bc is None else bc
    return (arr, bc, functools.partial(_const, v=blk), kind)


def _cst(arr):
    return (arr, arr.shape[1], functools.partial(_const, v=0))


def _out(cols, dt):
    return (cols, dt, cols, functools.partial(_const, v=0))


def _acc(rows, cols):
    return (rows, cols, cols, functools.partial(_const, v=0))


def _rms(x, w):
    return x * lax.rsqrt(jnp.mean(x * x, axis=-1, keepdims=True) + NORM_EPS) * w


def _sigmoid(x):
    return 0.5 * jnp.tanh(0.5 * x) + 0.5


def _silu(x):
    return x * _sigmoid(x)


def _dsilu(x):
    s = _sigmoid(x)
    return s * (1.0 + x * (1.0 - s))


def _softplus(x):
    return jnp.maximum(x, 0.0) + jnp.log1p(jnp.exp(-jnp.abs(x)))


def _shift(a, k):
    return pltpu.roll(a, k % a.shape[0], 0)


def _lroll(a, k):
    return pltpu.roll(a, k % a.shape[1], 1)


def _vjp_wrap(f, nrow, nconst, add_first=False):
    def g(i, n, *vals):
        rows, consts, mid = vals[:nrow], vals[len(vals) - nconst:], vals[nrow:len(vals) - nconst]
        cots = mid[:-1] if add_first else mid
        outs, pull = jax.vjp(f, *rows, *consts)
        grads = list(pull(tuple(c.astype(o.dtype) for c, o in zip(cots, outs))))
        if add_first:
            grads[0] = grads[0] + mid[-1]
        return tuple(grads)
    return g


def _rows_vjp(f, T, tm, rows, consts, cots, out_dtypes, name):
    return _rows(_vjp_wrap(f, len(rows), len(consts)), T, tm, [_cur(r) for r in rows] + [_cur(c) for c in cots],
                 [_cst(c) for c in consts], [_out(r.shape[1], dt) for r, dt in zip(rows, out_dtypes)],
                 [_acc(1, c.shape[1]) for c in consts], name)


def _f_premix(x, g):
    return (_rms(x, g),)


def _f_mla_pre(cq, ckv, qn, kvn):
    return _rms(cq, qn), _rms(ckv, kvn)


def _f_ssd_gate(y, z, nw):
    return (_rms(y * _silu(z), nw),)


def _f_post_mix(x, mixed, gpost, gffn):
    x1 = x + _rms(mixed, gpost)
    return x1, _rms(x1, gffn)


def _f_post_ffn(x1, d, gpost):
    return (x1 + _rms(d, gpost),)


def _rope_fwd(v, cosf, sina, sinb):
    return v * cosf + _lroll(v, -16) * sina + _lroll(v, 16) * sinb


def _rope_bwd(g, cosf, sina, sinb):
    return g * cosf + _lroll(g * sina, 16) + _lroll(g * sinb, -16)


def _k_rope_fwd(i, n, qpad, kvpad, kr, cosf, sina, sinb):
    qs, ks = [], []
    krr = _rope_fwd(kr, cosf, sina, sinb)
    for h in range(HEADS):
        sl = slice(h * HP, (h + 1) * HP)
        qs.append(_rope_fwd(qpad[:, sl], cosf, sina, sinb))
        ks.append(kvpad[:, sl].astype(F32) + krr)
    return jnp.concatenate(qs, axis=1), jnp.concatenate(ks, axis=1)


def _k_rope_bwd(i, n, dq, dk, dv, cosf, sina, sinb):
    lane = lax.broadcasted_iota(jnp.int32, (1, HP), 1)
    rmask = ((lane >= KR_LANE) & (lane < KR_LANE + ROPE)).astype(F32)
    dqs, dks = [], []
    dkr = jnp.zeros((dq.shape[0], HP), F32)
    for h in range(HEADS):
        sl = slice(h * HP, (h + 1) * HP)
        dqs.append(_rope_bwd(dq[:, sl], cosf, sina, sinb))
        dkh = dk[:, sl]
        dkr = dkr + dkh * rmask
        dks.append(dkh * (1.0 - rmask))
    dkr = _rope_bwd(dkr, cosf, sina, sinb) * rmask
    return jnp.concatenate(dqs, axis=1), jnp.concatenate(dks + [dv], axis=1), dkr


def _k_sconv_fwd(i, n, b, c, h, cp, hp, w):
    m = b.shape[0]
    up = jnp.where(i > 0, cp * hp, 0.0)
    ue = jnp.concatenate([up, c * h], axis=0)
    conv = w[2:3] * ue + w[1:2] * _shift(ue, 1) + w[0:1] * _shift(ue, 2)
    return (b * conv[HALO:],)


def _k_sconv_bwd(i, n, b, c, h, dy, cp, hp, bn, dyn, w):
    m = b.shape[0]
    up = jnp.where(i > 0, cp * hp, 0.0)
    ue = jnp.concatenate([up, c * h], axis=0)
    u1, u2 = _shift(ue, 1), _shift(ue, 2)
    conv = (w[2:3] * ue + w[1:2] * u1 + w[0:1] * u2)[HALO:]
    dc_cur = dy * b
    dce = jnp.concatenate([dc_cur, jnp.where(i < n - 1, dyn * bn, 0.0)], axis=0)
    du = (w[2:3] * dce + w[1:2] * _shift(dce, -1) + w[0:1] * _shift(dce, -2))[:m]
    dw = jnp.concatenate([
        jnp.sum(dc_cur * u2[HALO:], axis=0, keepdims=True),
        jnp.sum(dc_cur * u1[HALO:], axis=0, keepdims=True),
        jnp.sum(dc_cur * ue[HALO:], axis=0, keepdims=True),
        jnp.zeros((HALO - 3, b.shape[1]), F32)], axis=0)
    return dy * conv, du * h, du * c, dw


def _conv4(ue, w):
    return w[3:4] * ue + w[2:3] * _shift(ue, 1) + w[1:2] * _shift(ue, 2) + w[0:1] * _shift(ue, 3)


def _k_ssdconv_fwd(i, n, u, up, w, bias):
    ue = jnp.concatenate([jnp.where(i > 0, up, 0.0), u], axis=0)
    return (_silu(_conv4(ue, w)[HALO:] + bias),)


def _k_ssdconv_bwd(i, n, u, dout, up, un, doutn, w, bias):
    m = u.shape[0]
    ue = jnp.concatenate([jnp.where(i > 0, up, 0.0), u, un], axis=0)
    u1, u2, u3 = _shift(ue, 1), _shift(ue, 2), _shift(ue, 3)
    pre = (w[3:4] * ue + w[2:3] * u1 + w[1:2] * u2 + w[0:1] * u3)[HALO:] + bias
    doe = jnp.concatenate([dout, jnp.where(i < n - 1, doutn, 0.0)], axis=0)
    dpre = doe * _dsilu(pre)
    du = (w[3:4] * dpre + w[2:3] * _shift(dpre, -1) + w[1:2] * _shift(dpre, -2) + w[0:1] * _shift(dpre, -3))[:m]
    dp = dpre[:m]
    cur = slice(HALO, HALO + m)
    dw = jnp.concatenate([
        jnp.sum(dp * u3[cur], axis=0, keepdims=True),
        jnp.sum(dp * u2[cur], axis=0, keepdims=True),
        jnp.sum(dp * u1[cur], axis=0, keepdims=True),
        jnp.sum(dp * ue[cur], axis=0, keepdims=True),
        jnp.zeros((HALO - 4, u.shape[1]), F32)], axis=0)
    db = jnp.sum(dp, axis=0, keepdims=True)
    return du, dw, db


def _conv3(ue, w):
    return w[2:3] * ue + w[1:2] * _shift(ue, 1) + w[0:1] * _shift(ue, 2)


def _k_ffnact_fwd(i, n, ug, uu, ugp, uup, wg, wu, bg, bu):
    gate = _conv3(jnp.concatenate([jnp.where(i > 0, ugp, 0.0), ug], axis=0), wg)[HALO:] + bg
    upv = _conv3(jnp.concatenate([jnp.where(i > 0, uup, 0.0), uu], axis=0), wu)[HALO:] + bu
    return (_silu(gate) * upv,)


def _k_ffnact_bwd(i, n, ug, uu, dact, ugp, uup, ugn, uun, dactn, wg, wu, bg, bu):
    m = ug.shape[0]
    cur = slice(HALO, HALO + m)

    def taps(p, c, nx):
        e = jnp.concatenate([jnp.where(i > 0, p, 0.0), c, nx], axis=0)
        return e, _shift(e, 1), _shift(e, 2)

    def back(d, w):
        return (w[2:3] * d + w[1:2] * _shift(d, -1) + w[0:1] * _shift(d, -2))[:m]

    def wgrad(d, t):
        return jnp.concatenate([jnp.sum(d[:m] * t[2][cur], axis=0, keepdims=True), jnp.sum(d[:m] * t[1][cur], axis=0, keepdims=True),
                                jnp.sum(d[:m] * t[0][cur], axis=0, keepdims=True), jnp.zeros((HALO - 3, d.shape[1]), F32)], axis=0)

    tg, tu = taps(ugp, ug, ugn), taps(uup, uu, uun)
    gate = (wg[2:3] * tg[0] + wg[1:2] * tg[1] + wg[0:1] * tg[2])[HALO:] + bg
    upv = (wu[2:3] * tu[0] + wu[1:2] * tu[1] + wu[0:1] * tu[2])[HALO:] + bu
    dae = jnp.concatenate([dact, jnp.where(i < n - 1, dactn, 0.0)], axis=0)
    sg = _sigmoid(gate)
    dg = dae * upv * (sg * (1.0 + gate * (1.0 - sg)))
    dup = dae * (gate * sg)
    return (back(dg, wg), back(dup, wu), wgrad(dg, tg), wgrad(dup, tu),
            jnp.sum(dg[:m], axis=0, keepdims=True), jnp.sum(dup[:m], axis=0, keepdims=True))


def _k_loss(i, n, y, tgt):
    e = y - tgt
    part = 0.5 * jnp.sum(jnp.sum(e * e, axis=1, keepdims=True) / D_MODEL, axis=0, keepdims=True)
    return e * (1.0 / D_MODEL), jnp.broadcast_to(part, (1, LANE))


def _k_adam(i, n, w, g, m, v):
    m = ADAM_B1 * m + (1.0 - ADAM_B1) * g
    v = ADAM_B2 * v + (1.0 - ADAM_B2) * (g * g)
    m_hat = m / (1.0 - ADAM_B1 ** ADAM_STEP)
    v_hat = v / (1.0 - ADAM_B2 ** ADAM_STEP)
    delta = -ADAM_LR * (m_hat / (jnp.sqrt(v_hat) + ADAM_EPS) + ADAM_WD * w)
    return g, delta, m, v


def _dotf(a, b, dims):
    return lax.dot_general(a.astype(MXU_DTYPE), b.astype(MXU_DTYPE), dims, preferred_element_type=F32)


NN = (((1,), (0,)), ((), ()))
NT = (((1,), (1,)), ((), ()))
TN = (((0,), (0,)), ((), ()))


def _ssd_chunk(x0, x1, x2, x3, b0, b1, c0, c1, dtraw, p0, p1, p2, p3, dtb, alog, dsk):
    xs, bs, cs_, ps = (x0, x1, x2, x3), (b0, b1), (c0, c1), (p0, p1, p2, p3)
    L = dtraw.shape[0]
    dt = _softplus(dtraw + dtb)
    adt = dt * (-jnp.exp(alog))
    row = lax.broadcasted_iota(jnp.int32, (L, L), 0)
    col = lax.broadcasted_iota(jnp.int32, (L, L), 1)
    tril = row >= col
    cum = jnp.dot(tril.astype(F32), adt, precision=HIGHEST, preferred_element_type=F32)
    cum_t = cum.T
    lane = lax.broadcasted_iota(jnp.int32, (1, LANE), 1)
    sub = lax.broadcasted_iota(jnp.int32, (LANE, 1), 0)
    lastcol = (lax.broadcasted_iota(jnp.int32, (1, L), 1) == L - 1).astype(F32)
    ys, news = [], []
    for h in range(SSD_HEADS):
        g = h // (SSD_HEADS // 2)
        oh = (lane == h).astype(F32)
        dth = jnp.sum(dt * oh, axis=1, keepdims=True)
        csh = jnp.sum(cum * oh, axis=1, keepdims=True)
        csr = jnp.sum(cum_t * (sub == h).astype(F32), axis=0, keepdims=True)
        cl = jnp.sum(csr * lastcol, axis=1, keepdims=True)
        dskh = jnp.sum(dsk * oh, axis=1, keepdims=True)
        x, bm, cm, prev = xs[h], bs[g], cs_[g], ps[h]
        xdt = x * dth
        decay = jnp.exp(jnp.where(tril, csh - csr, -jnp.inf))
        scores = _dotf(cm, bm, NT) * decay
        y_diag = _dotf(scores, xdt, NN)
        bd = bm * jnp.exp(cl - csh)
        cst = _dotf(xdt, bd, TN)
        news.append(prev * jnp.exp(cl) + cst)
        y_off = _dotf(cm, prev, NT) * jnp.exp(csh)
        ys.append(y_diag + y_off + x * dskh)
    return (*ys, *news)


SSD_STEP = 2


def _ssd_operands(x_ref, dt_ref, par_ref, prev, rows):
    xs = [x_ref[rows, h * SSD_HEAD_DIM:(h + 1) * SSD_HEAD_DIM] for h in range(SSD_HEADS)]
    bs = [x_ref[rows, SSD_DIM + g * SSD_STATE:SSD_DIM + (g + 1) * SSD_STATE] for g in range(2)]
    cs_ = [x_ref[rows, SSD_DIM + 2 * SSD_STATE + g * SSD_STATE:SSD_DIM + 2 * SSD_STATE + (g + 1) * SSD_STATE] for g in range(2)]
    return (*xs, *bs, *cs_, dt_ref[rows, :], *prev, par_ref[0:1, :], par_ref[1:2, :], par_ref[2:3, :])


def _ssd_fwd(xbc, dtraw, par, T, dt_blk=0):
    L = SSD_CHUNK
    nc = T // L
    P = SSD_HEAD_DIM
    U = SSD_STEP if nc % SSD_STEP == 0 else 1

    def body(x_ref, dt_ref, par_ref, y_ref, st_ref, state):
        @pl.when(pl.program_id(0) == 0)
        def _():
            state[...] = jnp.zeros_like(state)

        for u in range(U):
            rows = slice(u * L, (u + 1) * L)
            st_ref[u] = state[...]
            prev = [state[h * P:(h + 1) * P, :] for h in range(SSD_HEADS)]
            res = _ssd_chunk(*_ssd_operands(x_ref, dt_ref, par_ref, prev, rows))
            for h in range(SSD_HEADS):
                y_ref[rows, h * P:(h + 1) * P] = res[h]
                state[h * P:(h + 1) * P, :] = res[SSD_HEADS + h]

    return pl.pallas_call(
        body, name="ssd_scan_fwd", grid=(nc // U,),
        in_specs=[pl.BlockSpec((U * L, SSD_CONV_DIM), lambda c: (c, 0)), pl.BlockSpec((U * L, LANE), lambda c: (c, dt_blk)),
                  pl.BlockSpec((8, LANE), lambda c: (0, 0))],
        out_specs=[pl.BlockSpec((U * L, SSD_DIM), lambda c: (c, 0)), pl.BlockSpec((U, SSD_DIM, SSD_STATE), lambda c: (c, 0, 0))],
        out_shape=[jax.ShapeDtypeStruct((T, SSD_DIM), F32), jax.ShapeDtypeStruct((nc, SSD_DIM, SSD_STATE), F32)],
        scratch_shapes=[pltpu.VMEM((SSD_DIM, SSD_STATE), F32)],
        compiler_params=pltpu.CompilerParams(dimension_semantics=("arbitrary",)),
    )(xbc, dtraw, par)


def _ssd_bwd(xbc, dtraw, par, states, dy, T, dt_blk=0):
    L = SSD_CHUNK
    nc = T // L
    P = SSD_HEAD_DIM
    U = SSD_STEP if nc % SSD_STEP == 0 else 1
    ns = nc // U

    def body(x_ref, dt_ref, par_ref, st_ref, dy_ref, dx_ref, ddt_ref, dpar_ref, dstate):
        @pl.when(pl.program_id(0) == 0)
        def _():
            dstate[...] = jnp.zeros_like(dstate)
            dpar_ref[...] = jnp.zeros_like(dpar_ref)

        for u in reversed(range(U)):
            rows = slice(u * L, (u + 1) * L)
            prev = [st_ref[u, h * P:(h + 1) * P, :] for h in range(SSD_HEADS)]
            prim = _ssd_operands(x_ref, dt_ref, par_ref, prev, rows)
            _, pull = jax.vjp(_ssd_chunk, *prim)
            cots = tuple(dy_ref[rows, h * P:(h + 1) * P] for h in range(SSD_HEADS)) + tuple(
                dstate[h * P:(h + 1) * P, :] for h in range(SSD_HEADS))
            g = pull(cots)
            for h in range(SSD_HEADS):
                dx_ref[rows, h * P:(h + 1) * P] = g[h]
                dstate[h * P:(h + 1) * P, :] = g[9 + h]
            for k in range(2):
                dx_ref[rows, SSD_DIM + k * SSD_STATE:SSD_DIM + (k + 1) * SSD_STATE] = g[4 + k]
                dx_ref[rows, SSD_DIM + 2 * SSD_STATE + k * SSD_STATE:SSD_DIM + 2 * SSD_STATE + (k + 1) * SSD_STATE] = g[6 + k]
            ddt_ref[rows, :] = g[8]
            for r in range(3):
                dpar_ref[r:r + 1, :] += g[13 + r]

    rev = lambda c: (ns - 1 - c, 0)
    return pl.pallas_call(
        body, name="ssd_scan_bwd", grid=(ns,),
        in_specs=[pl.BlockSpec((U * L, SSD_CONV_DIM), rev), pl.BlockSpec((U * L, LANE), lambda c: (ns - 1 - c, dt_blk)),
                  pl.BlockSpec((8, LANE), lambda c: (0, 0)),
                  pl.BlockSpec((U, SSD_DIM, SSD_STATE), lambda c: (ns - 1 - c, 0, 0)), pl.BlockSpec((U * L, SSD_DIM), rev)],
        out_specs=[pl.BlockSpec((U * L, SSD_CONV_DIM), rev), pl.BlockSpec((U * L, LANE), rev), pl.BlockSpec((8, LANE), lambda c: (0, 0))],
        out_shape=[jax.ShapeDtypeStruct((T, SSD_CONV_DIM), F32), jax.ShapeDtypeStruct((T, LANE), F32),
                   jax.ShapeDtypeStruct((8, LANE), F32)],
        scratch_shapes=[pltpu.VMEM((SSD_DIM, SSD_STATE), F32)],
        compiler_params=pltpu.CompilerParams(dimension_semantics=("arbitrary",)),
    )(xbc, dtraw, par, states, dy)


def _causal_pairs(nq, by_query):
    if by_query:
        pairs = [(i, j) for i in range(nq) for j in range(i + 1)]
    else:
        pairs = [(i, j) for j in range(nq) for i in range(j, nq)]
    return jnp.asarray([p[0] for p in pairs], jnp.int32), jnp.asarray([p[1] for p in pairs], jnp.int32)


def _flash_fwd(q, k, kv, T, carry=()):
    tq = tk = min(512, T)
    nq = T // tq
    G = FLASH_HEADS
    rep = tk // HP
    nc = len(carry)
    qi, kj = _causal_pairs(nq, by_query=True)
    nh, nt = HEADS // G, qi.shape[0]

    def body(qi_ref, kj_ref, q_ref, k_ref, v_ref, *rest):
        w_refs, o_ref, g_refs = rest[:nc], rest[nc], rest[nc + 1:2 * nc + 1]
        m_ref, l_ref, acc_ref = rest[2 * nc + 1:2 * nc + 4]
        h, t = pl.program_id(0), pl.program_id(1)
        i, j = qi_ref[t], kj_ref[t]
        if nc:
            plan = lambda: _ag_plan(w_refs, g_refs, rest[2 * nc + 4:])

            @pl.when((h == 0) & (t == 0))
            def _():
                for cp in plan()[0]:
                    cp.start()

        @pl.when(j == 0)
        def _():
            m_ref[...] = jnp.full_like(m_ref, -jnp.inf)
            l_ref[...] = jnp.zeros_like(l_ref)
            acc_ref[...] = jnp.zeros_like(acc_ref)

        def step(diagonal):
            for g in range(G):
                sl = slice(g * HP, (g + 1) * HP)
                s = _dotf(q_ref[:, sl], k_ref[:, sl], NT) * QK_SCALE
                if diagonal:
                    rows = lax.broadcasted_iota(jnp.int32, (tq, tk), 0)
                    cols = lax.broadcasted_iota(jnp.int32, (tq, tk), 1)
                    s = jnp.where(rows >= cols, s, -jnp.inf)
                m_old = m_ref[:, sl]
                m_new = jnp.maximum(m_old, jnp.max(s, axis=1, keepdims=True))
                p = jnp.exp(s - jnp.tile(m_new, (1, rep)))
                alpha = jnp.exp(m_old - m_new)
                l_ref[:, sl] = alpha * l_ref[:, sl] + jnp.sum(p, axis=1, keepdims=True)
                acc_ref[:, sl] = alpha * acc_ref[:, sl] + _dotf(p, v_ref[:, sl], NN)
                m_ref[:, sl] = m_new

        @pl.when(j < i)
        def _():
            step(False)

        @pl.when(j == i)
        def _():
            step(True)
            lane = lax.broadcasted_iota(jnp.int32, (tq, HP), 1)
            for g in range(G):
                sl = slice(g * HP, (g + 1) * HP)
                l = l_ref[:, sl]
                o_ref[:, sl] = jnp.where(lane < VDIM, acc_ref[:, sl] / l, m_ref[:, sl] + jnp.log(l))

        if nc:
            @pl.when((h == nh - 1) & (t == 0))
            def _():
                _, lands, forwards, _ = plan()
                for land, fw in zip(lands, forwards):
                    land.wait_recv()
                    fw.start()

            @pl.when((h == nh - 1) & (t == nt - 1))
            def _():
                sends, _, forwards, finals = plan()
                for cp in finals:
                    cp.wait_recv()
                for cp in sends + forwards:
                    cp.wait_send()

    W = G * HP
    res = pl.pallas_call(
        body, name="mla_flash_fwd",
        grid_spec=pltpu.PrefetchScalarGridSpec(
            num_scalar_prefetch=2, grid=(nh, nt),
            in_specs=[pl.BlockSpec((tq, W), lambda h, t, qi, kj: (qi[t], h)),
                      pl.BlockSpec((tk, W), lambda h, t, qi, kj: (kj[t], h)),
                      pl.BlockSpec((tk, W), lambda h, t, qi, kj: (kj[t], HEADS // G + h))] + [ANY] * nc,
            out_specs=[pl.BlockSpec((tq, W), lambda h, t, qi, kj: (qi[t], h))] + [ANY] * nc,
            scratch_shapes=[pltpu.VMEM((tq, W), F32), pltpu.VMEM((tq, W), F32), pltpu.VMEM((tq, W), F32)] + (_ag_sems(nc) if nc else [])),
        out_shape=[jax.ShapeDtypeStruct((T, HEADS * HP), F32)] + [jax.ShapeDtypeStruct((N_CHIPS,) + w.shape, w.dtype) for w in carry],
        compiler_params=pltpu.CompilerParams(dimension_semantics=("arbitrary", "arbitrary")),
    )(qi, kj, q, k, kv, *carry)
    return res[0] if not nc else (res[0], [_own_slot(g, w) for g, w in zip(res[1:], carry)])


def _flash_bwd(q, k, kv, o, dycat, T, carry=()):
    tq = tk = min(512, T)
    nq = T // tq
    G = FLASH_HEADS
    nc = len(carry)
    qi, kj = _causal_pairs(nq, by_query=False)
    nh, nt = HEADS // G, qi.shape[0]

    def body(qi_ref, kj_ref, q_ref, k_ref, v_ref, o_ref, do_ref, *rest):
        p_refs, (dq_ref, dk_ref, dv_ref), part_refs = rest[:nc], rest[nc:nc + 3], rest[nc + 3:2 * nc + 3]
        h, t = pl.program_id(0), pl.program_id(1)
        i, j = qi_ref[t], kj_ref[t]
        if nc:
            plan = lambda: _chip_plan(p_refs, part_refs, rest[2 * nc + 3:])

            @pl.when((h == 0) & (t == 0))
            def _():
                for cp in plan()[0]:
                    cp.start()

        @pl.when(t == 0)
        def _():
            dq_ref[...] = jnp.zeros_like(dq_ref)

        @pl.when(i == j)
        def _():
            dk_ref[...] = jnp.zeros_like(dk_ref)
            dv_ref[...] = jnp.zeros_like(dv_ref)

        def step(diagonal):
            r0 = pl.multiple_of(i * tq, tq)
            for g in range(G):
                sl = slice(g * HP, (g + 1) * HP)
                qv, kv, vv, ov, dov = q_ref[:, sl], k_ref[:, sl], v_ref[:, sl], o_ref[:, sl], do_ref[:, sl]
                s = _dotf(qv, kv, NT) * QK_SCALE
                p = jnp.exp(s - ov[:, VDIM:VDIM + 1])
                if diagonal:
                    rows = lax.broadcasted_iota(jnp.int32, (tq, tk), 0)
                    cols = lax.broadcasted_iota(jnp.int32, (tq, tk), 1)
                    p = jnp.where(rows >= cols, p, 0.0)
                dsum = jnp.sum(dov * ov, axis=1, keepdims=True)
                dv_ref[:, sl] += _dotf(p, dov, TN)
                dp = _dotf(dov, vv, NT)
                ds = p * (dp - dsum) * QK_SCALE
                dk_ref[:, sl] += _dotf(ds, qv, TN)
                dq_ref[pl.ds(r0, tq), sl] += _dotf(ds, kv, NN)

        @pl.when(i > j)
        def _():
            step(False)

        @pl.when(i == j)
        def _():
            step(True)

        if nc:
            @pl.when((h == nh - 1) & (t == nt - 1))
            def _():
                sends, lands = plan()
                for cp in lands:
                    cp.wait_recv()
                for cp in sends:
                    cp.wait_send()

    W = G * HP
    qmap = lambda h, t, qi, kj: (qi[t], h)
    kmap = lambda h, t, qi, kj: (kj[t], h)
    vmap = lambda h, t, qi, kj: (kj[t], HEADS // G + h)
    res = pl.pallas_call(
        body, name="mla_flash_bwd",
        grid_spec=pltpu.PrefetchScalarGridSpec(
            num_scalar_prefetch=2, grid=(nh, nt),
            in_specs=[pl.BlockSpec((tq, W), qmap), pl.BlockSpec((tk, W), kmap), pl.BlockSpec((tk, W), vmap),
                      pl.BlockSpec((tq, W), qmap), pl.BlockSpec((tq, W), qmap)] + [ANY] * nc,
            out_specs=[pl.BlockSpec((T, W), lambda h, t, qi, kj: (0, h)), pl.BlockSpec((tk, W), kmap), pl.BlockSpec((tk, W), kmap)]
            + [ANY] * nc,
            scratch_shapes=_chip_sems(nc) if nc else []),
        out_shape=[jax.ShapeDtypeStruct((T, HEADS * HP), F32)] * 3 + [jax.ShapeDtypeStruct(p.shape, p.dtype) for p in carry],
        compiler_params=pltpu.CompilerParams(dimension_semantics=("arbitrary", "arbitrary")),
    )(qi, kj, q, k, kv, o, dycat, *carry)
    return tuple(res[:3]) if not nc else (*res[:3], _chip_parts(res[3:], carry))


_IN_SRC = (0, 256, 384, 416, 672, 928, 1184, 1440, 2208, 2212)
_IN_DST = (Z_CQ, Z_CKV, Z_KR + KR_LANE, Z_SCB, Z_SCC, Z_SCH, Z_SSZ, Z_XBC, Z_DT)


def _pad_rows_in(w):
    ax = w.ndim - 2

    def zeros(n):
        return jnp.zeros(w.shape[:ax] + (n,) + w.shape[ax + 1:], w.dtype)

    def whole_tiles(p):
        n = p.shape[ax]
        return p if n % SLAB_ALIGN == 0 else jnp.pad(p, [(0, 0)] * ax + [(0, -n % SLAB_ALIGN), (0, 0)])

    parts, at = [], 0
    for s0, s1, d0 in zip(_IN_SRC[:-1], _IN_SRC[1:], _IN_DST):
        if d0 > at:
            parts.append(zeros(d0 - at))
        parts.append(whole_tiles(lax.slice_in_dim(w, s0, s1, axis=ax)))
        at = d0 + parts[-1].shape[ax]
    parts.append(zeros(ZIN - at))
    return jnp.concatenate(parts, axis=ax)


def _unpad_rows_in(w):
    ax = w.ndim - 2
    groups = list(zip(_IN_SRC[:-1], _IN_SRC[1:], _IN_DST))
    parts = [lax.slice_in_dim(w, d0, d0 + -(-(s1 - s0) // SLAB_ALIGN) * SLAB_ALIGN, axis=ax) for s0, s1, d0 in groups]
    return lax.slice_in_dim(jnp.concatenate(parts, axis=ax), 0, _IN_SRC[-1], axis=ax)


def _pad_heads(w, width):
    w = w.reshape(w.shape[:-1] + (HEADS, width))
    w = jnp.pad(w, [(0, 0)] * (w.ndim - 1) + [(0, HP - width)])
    return w.reshape(w.shape[:-2] + (HEADS * HP,))


def _unpad_heads(w, width):
    w = w.reshape(w.shape[:-1] + (HEADS, HP))[..., :width]
    return w.reshape(w.shape[:-2] + (HEADS * width,))


def _pad_kv(w):
    w = w.reshape(w.shape[:-1] + (HEADS, NOPE + VDIM))
    return jnp.concatenate([_pad_heads(w[..., :NOPE].reshape(w.shape[:-2] + (HEADS * NOPE,)), NOPE),
                            _pad_heads(w[..., NOPE:].reshape(w.shape[:-2] + (HEADS * VDIM,)), VDIM)], axis=-1)


def _unpad_kv(w):
    k = _unpad_heads(w[..., :HEADS * HP], NOPE).reshape(w.shape[:-1] + (HEADS, NOPE))
    v = _unpad_heads(w[..., HEADS * HP:], VDIM).reshape(w.shape[:-1] + (HEADS, VDIM))
    return jnp.concatenate([k, v], axis=-1).reshape(w.shape[:-1] + (HEADS * (NOPE + VDIM),))


def _pad_out_rows(w):
    lead, d = w.shape[:-2], w.shape[-1]
    att = w[..., :HEADS * VDIM, :].reshape(lead + (HEADS, VDIM, d))
    att = jnp.pad(att, [(0, 0)] * (att.ndim - 2) + [(0, HP - VDIM), (0, 0)]).reshape(lead + (HEADS * HP, d))
    return jnp.concatenate([att, w[..., HEADS * VDIM:, :]], axis=-2)


def _unpad_out_rows(w):
    lead, d = w.shape[:-2], w.shape[-1]
    att = w[..., :HEADS * HP, :].reshape(lead + (HEADS, HP, d))[..., :VDIM, :].reshape(lead + (HEADS * VDIM, d))
    return jnp.concatenate([att, w[..., HEADS * HP:, :]], axis=-2)


def _rows8(w):
    return jnp.pad(w.astype(F32), [(0, 0)] * (w.ndim - 2) + [(0, 8 - w.shape[-2]), (0, 0)])


def _row8(*vecs):
    c = vecs[0].shape[-1]
    return jnp.concatenate([v.reshape(1, c).astype(F32) for v in vecs] + [jnp.zeros((8 - len(vecs), c), F32)], axis=0)


def _lanes(v):
    return jnp.pad(v.astype(F32), (0, LANE - v.shape[0])).reshape(1, LANE)


def _rope_tables(positions):
    inv_freq = 1.0 / (ROPE_THETA ** (jnp.arange(0, ROPE, 2, dtype=F32) / ROPE))
    ang = positions.astype(F32)[:, None] * inv_freq
    cos, sin = jnp.cos(ang), jnp.sin(ang)
    T = positions.shape[0]
    half = ROPE // 2
    one = jnp.ones((T, KR_LANE), F32)
    zero = jnp.zeros((T, KR_LANE), F32)
    tail1 = jnp.ones((T, HP - KR_LANE - ROPE), F32)
    tail0 = jnp.zeros((T, HP - KR_LANE - ROPE), F32)
    z16 = jnp.zeros((T, half), F32)
    cosf = jnp.concatenate([one, cos, cos, tail1], axis=1)
    sina = jnp.concatenate([zero, -sin, z16, tail0], axis=1)
    sinb = jnp.concatenate([zero, z16, sin, tail0], axis=1)
    return cosf, sina, sinb


def _kernel_weights(W):
    c = lambda a: a.astype(MXU_DTYPE)
    forms = dict(
        w_in=("w_in", lambda w: c(_pad_rows_in(w))),
        w_q=("mla_w_q_up", lambda w: c(_pad_heads(w, NOPE + ROPE))),
        w_kv=("mla_w_kv_up", lambda w: c(_pad_kv(w))),
        w_out=("w_out", lambda w: c(_pad_out_rows(w))),
        w_up=("ffn_w_up", c),
        w_down=("ffn_w_down", c),
        sc_w=("sc_conv_w", _rows8),
        ssd_w=("ssd_conv_w", _rows8),
        ffn_w=("ffn_conv_w", _rows8),
    )
    return {k: f(W[n]) for k, (n, f) in forms.items() if n in W}


def _layer_weights(KW, l):
    return {k: (v[l] if k in ("sc_w", "ssd_w", "ffn_w") else (v, l)) for k, v in KW.items()}


def _local_step(x, positions, target, W, S, ex=None):
    T = x.shape[0]
    tm = min(ROW_BLOCK, T)
    tm_ffn = min(FFN_ROWS, T)
    cosf, sina, sinb = _rope_tables(positions)
    if ex is None:
        KW = _kernel_weights(W)
    else:
        early = _all_gather_weights(ex.shard(0, "early"))
    saved = []
    xl = x
    for l in range(DEPTH):
        lw = _layer_weights(KW, l) if ex is None else _kernel_weights(ex.weights(early, "early"))
        g_pre = S["norm_mix_pre"][l].reshape(1, -1)
        g_post = S["norm_mix_post"][l].reshape(1, -1)
        g_fpre = S["norm_ffn_pre"][l].reshape(1, -1)
        g_fpost = S["norm_ffn_post"][l].reshape(1, -1)
        qn = S["mla_q_norm"][l].reshape(1, -1)
        kvn = S["mla_kv_norm"][l].reshape(1, -1)
        ssd_b = S["ssd_conv_b"][l].reshape(1, -1)
        ssd_par = _row8(jnp.pad(S["ssd_dt_bias"][l], (0, LANE - SSD_HEADS)), jnp.pad(S["ssd_a_log"][l], (0, LANE - SSD_HEADS)),
                        jnp.pad(S["ssd_d"][l], (0, LANE - SSD_HEADS)))
        ssd_nw = S["ssd_norm"][l].reshape(1, -1)
        ffn_b = S["ffn_conv_b"][l].reshape(1, -1)

        (h1,) = _rows(lambda i, n, *v: _f_premix(*v), T, tm, [_cur(xl)], [_cst(g_pre)], [_out(D_MODEL, BF16)], [], "pre_mix_norm")
        zin = _mm(h1, lw["w_in"], "nt", F32, "mm_in")
        qlat, kvlat = _rows(lambda i, n, *v: _f_mla_pre(*v), T, tm, [_cur(zin, Q_LORA, 0), _cur(zin, KV_LORA, Z_CKV // KV_LORA)],
                            [_cst(qn), _cst(kvn)], [_out(Q_LORA, BF16), _out(KV_LORA, BF16)], [], "mla_pre_norm")
        qpad = _mm(qlat, lw["w_q"], "nn", F32, "mm_q_up")
        kvpad = _mm(kvlat, lw["w_kv"], "nn", BF16, "mm_kv_up")
        qr, kr = _rows(_k_rope_fwd, T, tm, [_cur(qpad), _cur(kvpad, HEADS * HP, 0), _cur(zin, LANE, Z_KR // LANE),
                                            _cur(cosf), _cur(sina), _cur(sinb)], [],
                       [_out(HEADS * HP, BF16), _out(HEADS * HP, BF16)], [], "mla_rope")
        if ex is None:
            o = _flash_fwd(qr, kr, kvpad, T)
        else:
            nlate = len(ex.layouts["late"])
            o, got = _flash_fwd(qr, kr, kvpad, T, carry=ex.shard(l, "late") + (ex.shard(l + 1, "early") if l + 1 < DEPTH else []))
            lw.update(_kernel_weights(ex.weights(got[:nlate], "late")))
            early = got[nlate:]
        (yconv,) = _rows(_k_sconv_fwd, T, tm, [_cur(zin, SC_DIM, Z_SCB // SC_DIM), _cur(zin, SC_DIM, Z_SCC // SC_DIM),
                                               _cur(zin, SC_DIM, Z_SCH // SC_DIM), _halo(zin, "prev", SC_DIM, Z_SCC // SC_DIM),
                                               _halo(zin, "prev", SC_DIM, Z_SCH // SC_DIM)], [_cst(lw["sc_w"])],
                         [_out(SC_DIM, F32)], [], "short_conv_fwd")
        (xbc,) = _rows(_k_ssdconv_fwd, T, tm, [_cur(zin, SSD_CONV_DIM, Z_XBC // SSD_CONV_DIM),
                                               _halo(zin, "prev", SSD_CONV_DIM, Z_XBC // SSD_CONV_DIM)],
                       [_cst(lw["ssd_w"]), _cst(ssd_b)], [_out(SSD_CONV_DIM, F32)], [], "ssd_conv_fwd")
        yscan, states = _ssd_fwd(xbc, zin, ssd_par, T, Z_DT // LANE)
        (yssd,) = _rows(lambda i, n, *v: _f_ssd_gate(*v), T, tm, [_cur(yscan), _cur(zin, SSD_DIM, Z_SSZ // SSD_DIM)], [_cst(ssd_nw)],
                        [_out(SSD_DIM, F32)], [], "ssd_gate_fwd")
        ycat = jnp.concatenate([o.astype(BF16), yconv.astype(BF16), yssd.astype(BF16)], axis=1)
        mixed = _mm(ycat, lw["w_out"], "nn", F32, "mm_out")
        x1, h2 = _rows(lambda i, n, *v: _f_post_mix(*v), T, tm, [_cur(xl), _cur(mixed)], [_cst(g_post), _cst(g_fpre)],
                       [_out(D_MODEL, F32), _out(D_MODEL, BF16)], [], "post_mix_fwd")
        upre = _mm(h2, lw["w_up"], "nn", F32, "mm_up")
        nt = FFN_DIM // FFN_TILE
        gcol, ucol = (lambda j: j), (lambda j: j + nt)
        (act,) = _rows(_k_ffnact_fwd, T, tm_ffn,
                       [(upre, FFN_TILE, gcol, "cur"), (upre, FFN_TILE, ucol, "cur"), (upre, FFN_TILE, gcol, "prev"),
                        (upre, FFN_TILE, ucol, "prev")],
                       [(lw["ffn_w"], FFN_TILE, gcol), (lw["ffn_w"], FFN_TILE, ucol), (ffn_b, FFN_TILE, gcol), (ffn_b, FFN_TILE, ucol)],
                       [(FFN_DIM, BF16, FFN_TILE, gcol)], [], "ffn_act_fwd", ncol=nt)
        dn = _mm(act, lw["w_down"], "nn", F32, "mm_down")
        (x2,) = _rows(lambda i, n, *v: _f_post_ffn(*v), T, tm, [_cur(x1), _cur(dn)], [_cst(g_fpost)], [_out(D_MODEL, F32)], [], "post_ffn_fwd")
        saved.append(dict(lw=lw, x=xl, h1=h1, zin=zin, qlat=qlat, kvlat=kvlat, qr=qr, kr=kr, kvpad=kvpad, o=o, xbc=xbc,
                          yscan=yscan, states=states, ycat=ycat, mixed=mixed, x1=x1, h2=h2, upre=upre, act=act, dn=dn,
                          g_pre=g_pre, g_post=g_post, g_fpre=g_fpre, g_fpost=g_fpost, qn=qn, kvn=kvn, ssd_b=ssd_b,
                          ssd_par=ssd_par, ssd_nw=ssd_nw, ffn_b=ffn_b))
        xl = x2

    gx, loss_part = _rows(_k_loss, T, tm, [_cur(xl), _cur(target)], [], [_out(D_MODEL, F32)], [_acc(1, LANE)], "loss_head")

    GW = {k: [None] * DEPTH for k in ("w_in", "mla_w_q_up", "mla_w_kv_up", "sc_conv_w", "ssd_conv_w", "w_out", "ffn_w_up",
                                      "ffn_conv_w", "ffn_w_down")}
    GS = {k: [None] * DEPTH for k in ("norm_mix_pre", "norm_mix_post", "norm_ffn_pre", "norm_ffn_post", "mla_q_norm", "mla_kv_norm",
                                      "ssd_conv_b", "ssd_dt_bias", "ssd_a_log", "ssd_d", "ssd_norm", "ffn_conv_b")}
    nt = FFN_DIM // FFN_TILE
    gcol, ucol = (lambda j: j), (lambda j: j + nt)
    pending = None
    for l in reversed(range(DEPTH)):
        s = saved[l]
        lw = s["lw"]
        gx1, ddn, dgf = _rows_vjp(_f_post_ffn, T, tm, [s["x1"], s["dn"]], [s["g_fpost"]], [gx], [F32, BF16], "post_ffn_bwd")
        GS["norm_ffn_post"][l] = dgf[0]
        dact = _mm(ddn, lw["w_down"], "nt", F32, "mm_down_dx")
        GW["ffn_w_down"][l] = _mm(s["act"], ddn, "tn", BF16, "mm_down_dw")
        up = s["upre"]
        dug, duu, dwg, dwu, dbg, dbu = _rows(
            _k_ffnact_bwd, T, tm_ffn,
            [(up, FFN_TILE, gcol, "cur"), (up, FFN_TILE, ucol, "cur"), (dact, FFN_TILE, gcol, "cur"), (up, FFN_TILE, gcol, "prev"),
             (up, FFN_TILE, ucol, "prev"), (up, FFN_TILE, gcol, "next"), (up, FFN_TILE, ucol, "next"), (dact, FFN_TILE, gcol, "next")],
            [(lw["ffn_w"], FFN_TILE, gcol), (lw["ffn_w"], FFN_TILE, ucol), (s["ffn_b"], FFN_TILE, gcol), (s["ffn_b"], FFN_TILE, ucol)],
            [(FFN_DIM, BF16, FFN_TILE, gcol)] * 2,
            [(HALO, FFN_DIM, FFN_TILE, gcol)] * 2 + [(1, FFN_DIM, FFN_TILE, gcol)] * 2, "ffn_act_bwd", ncol=nt)
        GW["ffn_conv_w"][l] = jnp.concatenate([dwg[:3], dwu[:3]], axis=1)
        GS["ffn_conv_b"][l] = jnp.concatenate([dbg[0], dbu[0]])
        dh2 = _mm((dug, duu), lw["w_up"], "nt", F32, "mm_up_dx")
        GW["ffn_w_up"][l] = (_mm(s["h2"], dug, "tn", BF16, "mm_up_dw_gate"), _mm(s["h2"], duu, "tn", BF16, "mm_up_dw_up"))
        gx0, dmixed, dgp, dgf = _rows_vjp(_f_post_mix, T, tm, [s["x"], s["mixed"]], [s["g_post"], s["g_fpre"]], [gx1, dh2],
                                          [F32, BF16], "post_mix_bwd")
        GS["norm_mix_post"][l], GS["norm_ffn_pre"][l] = dgp[0], dgf[0]
        dycat = _mm(dmixed, lw["w_out"], "nt", F32, "mm_out_dx")
        GW["w_out"][l] = _unpad_out_rows(_mm(s["ycat"], dmixed, "tn", BF16, "mm_out_dw"))
        zin = s["zin"]
        dyscan, dz, dnw = _rows(_vjp_wrap(_f_ssd_gate, 2, 1), T, tm,
                                [_cur(s["yscan"]), _cur(zin, SSD_DIM, Z_SSZ // SSD_DIM), _cur(dycat, SSD_DIM, (HEADS * HP + SC_DIM) // SSD_DIM)],
                                [_cst(s["ssd_nw"])], [_out(SSD_DIM, F32), _out(SSD_DIM, BF16)], [_acc(1, SSD_DIM)], "ssd_gate_bwd")
        GS["ssd_norm"][l] = dnw[0]
        dxbc, ddtraw, dpar = _ssd_bwd(s["xbc"], zin, s["ssd_par"], s["states"], dyscan, T, Z_DT // LANE)
        GS["ssd_dt_bias"][l], GS["ssd_a_log"][l], GS["ssd_d"][l] = dpar[0, :SSD_HEADS], dpar[1, :SSD_HEADS], dpar[2, :SSD_HEADS]
        xb = Z_XBC // SSD_CONV_DIM
        dxraw, dsw, dsb = _rows(_k_ssdconv_bwd, T, tm,
                                [_cur(zin, SSD_CONV_DIM, xb), _cur(dxbc), _halo(zin, "prev", SSD_CONV_DIM, xb),
                                 _halo(zin, "next", SSD_CONV_DIM, xb), _halo(dxbc, "next")],
                                [_cst(lw["ssd_w"]), _cst(s["ssd_b"])], [_out(SSD_CONV_DIM, BF16)],
                                [_acc(HALO, SSD_CONV_DIM), _acc(1, SSD_CONV_DIM)], "ssd_conv_bwd")
        GW["ssd_conv_w"][l] = dsw[:4]
        GS["ssd_conv_b"][l] = dsb[0]
        cb = (HEADS * HP) // SC_DIM
        dscb, dscc, dsch, dscw = _rows(_k_sconv_bwd, T, tm,
                                       [_cur(zin, SC_DIM, Z_SCB // SC_DIM), _cur(zin, SC_DIM, Z_SCC // SC_DIM),
                                        _cur(zin, SC_DIM, Z_SCH // SC_DIM), _cur(dycat, SC_DIM, cb),
                                        _halo(zin, "prev", SC_DIM, Z_SCC // SC_DIM), _halo(zin, "prev", SC_DIM, Z_SCH // SC_DIM),
                                        _halo(zin, "next", SC_DIM, Z_SCB // SC_DIM), _halo(dycat, "next", SC_DIM, cb)],
                                       [_cst(lw["sc_w"])], [_out(SC_DIM, BF16)] * 3, [_acc(HALO, SC_DIM)], "short_conv_bwd")
        GW["sc_conv_w"][l] = dscw[:3]
        if ex is None:
            dq, dk, dv = _flash_bwd(s["qr"], s["kr"], s["kvpad"], s["o"], dycat, T)
        else:
            late = ex.submit({n: GW[n][l] for ns in LATE for n in ns}, "late")
            dq, dk, dv, parts = _flash_bwd(s["qr"], s["kr"], s["kvpad"], s["o"], dycat, T, carry=late + (pending or []))
            ex.collect(l, "late", parts[:len(late)])
            if pending:
                ex.collect(l + 1, "early", parts[len(late):])
        dqpad, dkvpad, dkr = _rows(_k_rope_bwd, T, tm, [_cur(dq), _cur(dk), _cur(dv), _cur(cosf), _cur(sina), _cur(sinb)], [],
                                   [_out(HEADS * HP, BF16), _out(2 * HEADS * HP, BF16), _out(LANE, BF16)], [], "mla_rope_bwd")
        dqlat = _mm(dqpad, lw["w_q"], "nt", F32, "mm_q_dx")
        GW["mla_w_q_up"][l] = _unpad_heads(_mm(s["qlat"], dqpad, "tn", BF16, "mm_q_dw"), NOPE + ROPE)
        dkvlat = _mm(dkvpad, lw["w_kv"], "nt", F32, "mm_kv_dx")
        GW["mla_w_kv_up"][l] = _unpad_kv(_mm(s["kvlat"], dkvpad, "tn", BF16, "mm_kv_dw"))
        dcq, dckv, dqn, dkvn = _rows(_vjp_wrap(_f_mla_pre, 2, 2), T, tm,
                                     [_cur(zin, Q_LORA, 0), _cur(zin, KV_LORA, Z_CKV // KV_LORA), _cur(dqlat), _cur(dkvlat)],
                                     [_cst(s["qn"]), _cst(s["kvn"])], [_out(Q_LORA, BF16), _out(KV_LORA, BF16)],
                                     [_acc(1, Q_LORA), _acc(1, KV_LORA)], "mla_pre_bwd")
        GS["mla_q_norm"][l], GS["mla_kv_norm"][l] = dqn[0], dkvn[0]
        dzin = jnp.concatenate([dcq, dckv, dkr, dscb, dscc, dsch, dz, dxraw, ddtraw.astype(BF16), jnp.zeros((T, ZIN - Z_DT - LANE), BF16)], axis=1)
        dh1 = _mm(dzin, lw["w_in"], "nn", F32, "mm_in_dx")
        GW["w_in"][l] = _unpad_rows_in(_mm(dzin, s["h1"], "tn", BF16, "mm_in_dw"))
        gx, dgp = _rows(_vjp_wrap(_f_premix, 1, 1, add_first=True), T, tm, [_cur(s["x"]), _cur(dh1), _cur(gx0)], [_cst(s["g_pre"])],
                        [_out(D_MODEL, F32)], [_acc(1, D_MODEL)], "pre_mix_bwd")
        GS["norm_mix_pre"][l] = dgp[0]
        if ex is not None:
            pending = ex.submit({n: GW[n][l] for ns in EARLY for n in ns}, "early")
    if ex is not None:
        ex.collect(0, "early", _rs_chip_exchange(pending))
    GS = {k: jnp.stack(v) for k, v in GS.items()}
    return loss_part[0, 0], gx, GW, GS


WEIGHTS = ("norm_mix_pre", "norm_mix_post", "norm_ffn_pre", "norm_ffn_post", "w_in", "mla_q_norm", "mla_w_q_up", "mla_kv_norm",
           "mla_w_kv_up", "sc_conv_w", "ssd_conv_w", "ssd_conv_b", "ssd_dt_bias", "ssd_a_log", "ssd_d", "ssd_norm", "w_out",
           "ffn_w_up", "ffn_conv_w", "ffn_conv_b", "ffn_w_down")
SHARDED = (("w_in", 2), ("mla_w_q_up", 2), ("mla_w_kv_up", 2), ("sc_conv_w", 2), ("ssd_conv_w", 2), ("w_out", 1),
           ("ffn_w_up", 2), ("ffn_conv_w", 2), ("ffn_w_down", 1))
SMALL = tuple(n for n in WEIGHTS if n not in dict(SHARDED))
N_CHIPS = 4
N_DEV = 8
ROW_ALIGN = 64
SLAB_ALIGN = 16
EARLY = (("w_in", "mla_w_q_up", "mla_w_kv_up", "sc_conv_w", "ssd_conv_w"),)
LATE = (("ffn_w_down", "w_out"), ("ffn_w_up", "ffn_conv_w"))
TRANSPOSED = ("w_in",)


def _is_rows(shape, width):
    return shape[-1] == width and math.prod(shape[:-1]) % SLAB_ALIGN == 0


def _is_short(shape, width):
    return len(shape) == 2 and shape[1] == width and not _is_rows(shape, width)


def _slab_rows(shape, width):
    if _is_rows(shape, width):
        return math.prod(shape[:-1])
    if _is_short(shape, width):
        return -(-shape[0] // SLAB_ALIGN) * SLAB_ALIGN
    return -(-math.prod(shape) // (width * SLAB_ALIGN)) * SLAB_ALIGN


def _slab(piece, width, dtype, lead=0):
    ld, shape = piece.shape[:lead], piece.shape[lead:]
    rows = _slab_rows(shape, width)
    if _is_rows(shape, width):
        return piece.astype(dtype).reshape(ld + (rows, width))
    if _is_short(shape, width):
        return jnp.pad(piece.astype(dtype), [(0, 0)] * lead + [(0, rows - shape[0]), (0, 0)])
    flat = piece.astype(dtype).reshape(ld + (-1,))
    return jnp.pad(flat, [(0, 0)] * lead + [(0, rows * width - flat.shape[-1])]).reshape(ld + (rows, width))


def _unslab(slab, shape, lead=0):
    ld = slab.shape[:lead]
    if _is_rows(shape, slab.shape[-1]):
        return slab.reshape(ld + tuple(shape))
    if _is_short(shape, slab.shape[-1]):
        return slab[..., :shape[0], :]
    return slab.reshape(ld + (-1,))[..., :math.prod(shape)].reshape(ld + tuple(shape))


def _layout(shapes, names, width):
    ents, off = [], 0
    for n in names:
        shp = tuple(shapes[n])
        todo = [(None, False, shp), (None, True, shp)] if n.endswith("conv_w") else [(l, False, shp[1:]) for l in range(shp[0])]
        for l, lo, ps in todo:
            r = _slab_rows(ps, width)
            ents.append((n, l, lo, ps, off, r))
            off += r
    return width, -(-off // ROW_ALIGN) * ROW_ALIGN, ents


def _pack(layout, piece, dtype, lead=0):
    width, rows, ents = layout
    slabs, ld = [], None
    for n, l, lo, ps, off, r in ents:
        p = piece(n, l, lo)
        slabs.append(None if p is None else _slab(p, width, dtype, lead))
        ld = ld if p is None else p.shape[:lead]
    used = ents[-1][4] + ents[-1][5]
    slabs = [jnp.zeros(ld + (e[5], width), dtype) if s is None else s for s, e in zip(slabs, ents)]
    if rows > used:
        slabs.append(jnp.zeros(ld + (rows - used, width), dtype))
    return jnp.concatenate(slabs, axis=lead)


ANY = pl.BlockSpec(memory_space=pl.ANY)


def _pos():
    return lax.axis_index("x"), lax.axis_index("y"), lax.axis_index("c")


def _other_chips(x, y):
    return ((1 - x, y), (x, 1 - y), (1 - x, 1 - y))


def _remote(src, dst, ssem, rsem, dev):
    return pltpu.make_async_remote_copy(src_ref=src, dst_ref=dst, send_sem=ssem, recv_sem=rsem, device_id=dev, device_id_type=MESH)


AG_CHUNKS = 2


def _chip_index():
    return 2 * lax.axis_index("x") + lax.axis_index("y")


def _ag_sems(nbuf):
    return [pltpu.SemaphoreType.DMA((nbuf * 3 * AG_CHUNKS,))] * 4


def _ag_plan(w_refs, out_refs, sems):
    isend, irecv, dsend, drecv = sems
    x, y, c = _pos()
    k = 2 * x + y
    sib = (x, y, 1 - c)
    sends, lands, forwards, finals = [], [], [], []
    s = 0
    for w_ref, out_ref in zip(w_refs, out_refs):
        H = w_ref.shape[0] // 2
        CH = H // AG_CHUNKS
        for cx, cy in _other_chips(x, y):
            for ch in range(AG_CHUNKS):
                mine = out_ref.at[k, pl.ds(c * H + ch * CH, CH), :]
                near = out_ref.at[2 * cx + cy, pl.ds(c * H + ch * CH, CH), :]
                far = out_ref.at[2 * cx + cy, pl.ds((1 - c) * H + ch * CH, CH), :]
                sends.append(_remote(w_ref.at[pl.ds(c * H + ch * CH, CH), :], mine, isend.at[s], irecv.at[s], (cx, cy, c)))
                lands.append(_remote(near, near, isend.at[s], irecv.at[s], (cx, cy, c)))
                forwards.append(_remote(near, near, dsend.at[s], drecv.at[s], sib))
                finals.append(_remote(far, far, dsend.at[s], drecv.at[s], sib))
                s += 1
    return sends, lands, forwards, finals


def _own_slot(got, own):
    return lax.dynamic_update_slice(got, own[None], (_chip_index(), 0, 0))


def _all_gather_weights(ws):
    nb = len(ws)

    def body(*refs):
        sends, lands, forwards, finals = _ag_plan(refs[:nb], refs[nb:2 * nb], refs[2 * nb:])
        for cp in sends:
            cp.start()
        for land, fw in zip(lands, forwards):
            land.wait_recv()
            fw.start()
        for cp in finals:
            cp.wait_recv()
        for cp in sends + forwards:
            cp.wait_send()

    got = pl.pallas_call(
        body, name="all_gather_weights", in_specs=[ANY] * nb, out_specs=[ANY] * nb,
        out_shape=[jax.ShapeDtypeStruct((N_CHIPS,) + w.shape, w.dtype) for w in ws], scratch_shapes=_ag_sems(nb),
    )(*ws)
    return [_own_slot(g, w) for g, w in zip(got, ws)]


def _rs_pair_exchange(gs):
    nb = len(gs)

    def body(*refs):
        g_refs, got_refs, (ssem, rsem) = refs[:nb], refs[nb:2 * nb], refs[2 * nb:]
        x, y, c = _pos()
        cps = []
        for b, (g_ref, got_ref) in enumerate(zip(g_refs, got_refs)):
            H = g_ref.shape[1] // 2
            for kk in range(N_CHIPS):
                s = b * N_CHIPS + kk
                cps.append(_remote(g_ref.at[kk, pl.ds((1 - c) * H, H), :], got_ref.at[kk], ssem.at[s], rsem.at[s], (x, y, 1 - c)))
        for cp in cps:
            cp.start()
        for cp in cps:
            cp.wait()

    return pl.pallas_call(
        body, name="rs_pair_exchange", in_specs=[ANY] * nb, out_specs=[ANY] * nb,
        out_shape=[jax.ShapeDtypeStruct((N_CHIPS, g.shape[1] // 2, g.shape[2]), g.dtype) for g in gs],
        scratch_shapes=[pltpu.SemaphoreType.DMA((nb * N_CHIPS,))] * 2,
    )(*gs)


def _chip_sems(nbuf):
    return [pltpu.SemaphoreType.DMA((nbuf * 3,))] * 2


def _chip_plan(p_refs, out_refs, sems):
    ssem, rsem = sems
    x, y, c = _pos()
    sends, lands = [], []
    s = 0
    for p_ref, out_ref in zip(p_refs, out_refs):
        for cx, cy in _other_chips(x, y):
            sends.append(_remote(p_ref.at[2 * cx + cy], out_ref.at[2 * x + y], ssem.at[s], rsem.at[s], (cx, cy, c)))
            land = out_ref.at[2 * cx + cy]
            lands.append(_remote(land, land, ssem.at[s], rsem.at[s], (cx, cy, c)))
            s += 1
    return sends, lands


def _chip_parts(got, ps):
    k = _chip_index()
    return [lax.dynamic_update_slice(g, lax.dynamic_slice_in_dim(p, k, 1, axis=0), (k, 0, 0)) for g, p in zip(got, ps)]


def _rs_chip_exchange(ps):
    nb = len(ps)

    def body(*refs):
        sends, lands = _chip_plan(refs[:nb], refs[nb:2 * nb], refs[2 * nb:])
        for cp in sends:
            cp.start()
        for cp in lands:
            cp.wait_recv()
        for cp in sends:
            cp.wait_send()

    got = pl.pallas_call(
        body, name="rs_chip_exchange", in_specs=[ANY] * nb, out_specs=[ANY] * nb,
        out_shape=[jax.ShapeDtypeStruct(p.shape, p.dtype) for p in ps], scratch_shapes=_chip_sems(nb),
    )(*ps)
    return _chip_parts(got, ps)


def _rs_pair_share(fs):
    nb = len(fs)

    def body(*refs):
        f_refs, out_refs, (ssem, rsem) = refs[:nb], refs[nb:2 * nb], refs[2 * nb:]
        x, y, c = _pos()
        sends, lands = [], []
        for b, (f_ref, out_ref) in enumerate(zip(f_refs, out_refs)):
            sends.append(_remote(f_ref, out_ref.at[c], ssem.at[b], rsem.at[b], (x, y, 1 - c)))
            land = out_ref.at[1 - c]
            lands.append(_remote(land, land, ssem.at[b], rsem.at[b], (x, y, 1 - c)))
        for cp in sends:
            cp.start()
        for cp in lands:
            cp.wait_recv()
        for cp in sends:
            cp.wait_send()

    got = pl.pallas_call(
        body, name="rs_pair_share", in_specs=[ANY] * nb, out_specs=[ANY] * nb,
        out_shape=[jax.ShapeDtypeStruct((2,) + f.shape, f.dtype) for f in fs],
        scratch_shapes=[pltpu.SemaphoreType.DMA((nb,))] * 2,
    )(*fs)
    return [lax.dynamic_update_slice(g, f[None], (lax.axis_index("c"), 0, 0)) for g, f in zip(got, fs)]


def _all_reduce_small(s):
    r, C = s.shape

    def body(s_ref, o_ref, buf, ssem, rsem):
        x, y, c = _pos()
        me = 4 * x + 2 * y + c
        buf[me] = s_ref[...]
        cps = []
        for m in range(1, N_DEV):
            mx, my, mc = (m >> 2) & 1, (m >> 1) & 1, m & 1
            peer = (x ^ mx, y ^ my, c ^ mc)
            cp = _remote(s_ref, buf.at[me], ssem.at[m - 1], rsem.at[m - 1], peer)
            cp.start()
            cps.append(cp)
        for m in range(1, N_DEV):
            mx, my, mc = (m >> 2) & 1, (m >> 1) & 1, m & 1
            src = 4 * (x ^ mx) + 2 * (y ^ my) + (c ^ mc)
            _remote(s_ref, buf.at[src], ssem.at[m - 1], rsem.at[m - 1], (x ^ mx, y ^ my, c ^ mc)).wait_recv()
        for cp in cps:
            cp.wait_send()
        acc = buf[0]
        for j in range(1, N_DEV):
            acc = acc + buf[j]
        o_ref[...] = acc

    return pl.pallas_call(
        body, name="all_reduce_small", in_specs=[pl.BlockSpec(memory_space=pltpu.VMEM)],
        out_specs=pl.BlockSpec(memory_space=pltpu.VMEM), out_shape=jax.ShapeDtypeStruct((r, C), F32),
        scratch_shapes=[pltpu.VMEM((N_DEV, r, C), F32), pltpu.SemaphoreType.DMA((N_DEV - 1,)), pltpu.SemaphoreType.DMA((N_DEV - 1,))],
    )(s)


def _rtile(n, pref):
    if n <= pref:
        return n
    t = (pref // 16) * 16
    while t >= 16:
        if n % t == 0:
            return t
        t -= 16
    raise ValueError(f"no row tile for {n}")


def _rs_pair_sums(gpks):
    gots = _rs_pair_exchange(gpks)
    out = []
    for gpk, got in zip(gpks, gots):
        _, R, C = gpk.shape
        H = R // 2
        own = lax.dynamic_index_in_dim(gpk.reshape(N_CHIPS, 2, H, C), lax.axis_index("c"), axis=1, keepdims=False)
        (part,) = _rows(lambda i, n, a, b: (a.astype(F32) + b.astype(F32),), N_CHIPS * H, _rtile(N_CHIPS * H, 512),
                        [_cur(own.reshape(N_CHIPS * H, C)), _cur(got.reshape(N_CHIPS * H, C))], [], [_out(C, BF16)], [], "rs_pair_add")
        out.append(part.reshape(N_CHIPS, H, C))
    return out


def _rs_chip_sums(parts):
    def add4(i, n, a, b, c, d):
        return (((a.astype(F32) + b.astype(F32)) + c.astype(F32)) + d.astype(F32),)

    out = []
    for p in parts:
        _, H, C = p.shape
        tm = _rtile(H, 1024)
        (red,) = _rows(add4, H, tm, [(p.reshape(N_CHIPS * H, C), C, functools.partial(_const, v=0), j * (H // tm)) for j in range(N_CHIPS)],
                       [], [_out(C, F32)], [], "rs_chip_add")
        out.append(red)
    return out


class _Exchange:
    def __init__(self, a):
        self.a = a
        self.axis = {n: (1 if n in TRANSPOSED else ax) for n, ax in SHARDED}
        shapes = {n: (1,) + tuple(self.packed(n, a[n]).shape[1:]) for n in self.axis}
        widths = lambda names: shapes[names[0]][-1] if names[0] == "ffn_w_up" else PACK_COLS
        self.layouts = {"early": [_layout(shapes, ns, widths(ns)) for ns in EARLY], "late": [_layout(shapes, ns, widths(ns)) for ns in LATE]}
        self.reduced = {}

    @staticmethod
    def packed(n, w):
        return jnp.swapaxes(w, -1, -2) if n in TRANSPOSED else w

    def shard(self, l, group):
        def piece(n, li, lo):
            w = self.packed(n, self.a[n][l:l + 1] if li is None else self.a[n][l])
            return w - w.astype(BF16).astype(F32) if lo else w
        return [_pack(lay, piece, BF16) for lay in self.layouts[group]]

    def weights(self, gathered, group):
        W, resid = {}, {}
        for (width, rows, ents), g in zip(self.layouts[group], gathered):
            for n, li, lo, ps, off, r in ents:
                parts = _unslab(g[:, off:off + r], ps, lead=1)
                ax = self.axis[n] + (1 if li is None else 0)
                full = jnp.moveaxis(parts, 0, ax - 1)
                full = full.reshape(full.shape[:ax - 1] + (-1,) + full.shape[ax + 1:])
                (resid if lo else W)[n] = full[0] if li is None else full
        for n in resid:
            W[n] = W[n].astype(F32) + resid[n].astype(F32)
        return W

    def submit(self, GW, group):
        def by_chip(g, ax, parts=N_CHIPS):
            g = g.reshape(g.shape[:ax] + (parts, g.shape[ax] // parts) + g.shape[ax + 1:])
            return jnp.moveaxis(g, ax, 0)

        def piece(n, li, lo):
            if lo:
                return None
            g = GW[n]
            if isinstance(g, tuple):
                return jnp.concatenate([by_chip(h, self.axis[n] - 1, N_CHIPS // 2) for h in g])
            return by_chip(g[None], self.axis[n]) if li is None else by_chip(g, self.axis[n] - 1)

        return _rs_pair_sums([_pack(lay, piece, BF16, lead=1) for lay in self.layouts[group]])

    def collect(self, l, group, parts):
        self.reduced[l, group] = _rs_chip_sums(parts)

    def finish(self):
        keys = [(l, g) for l in range(DEPTH) for g in self.layouts]
        flat = _rs_pair_share([f for key in keys for f in self.reduced[key]])
        both, at = {}, 0
        for key in keys:
            both[key] = flat[at:at + len(self.layouts[key[1]])]
            at += len(self.layouts[key[1]])
        grads = {}
        for group, lays in self.layouts.items():
            for b, (width, rows, ents) in enumerate(lays):
                for n, li, lo, ps, off, r in ents:
                    if not lo:
                        per_layer = [self.packed(n, _unslab(both[l, group][b].reshape(rows, width)[off:off + r], ps)) for l in range(DEPTH)]
                        grads[n] = jnp.concatenate(per_layer) if li is None else jnp.stack(per_layer)
        return grads


def _adam(w, g, m, v, name, g_row=0):
    shp = w.shape
    two = lambda a: a.reshape(-1, shp[-1])
    rows = math.prod(shp[:-1])
    tm = _rtile(rows, 256)
    assert g_row % tm == 0
    g_in = (two(g), shp[-1], functools.partial(_const, v=0), g_row // tm)
    res = _rows(_k_adam, rows, tm, [_cur(two(w)), g_in, _cur(two(m)), _cur(two(v))], [], [_out(shp[-1], F32)] * 4, [], name)
    return tuple(r.reshape(shp) for r in res)


def _pack_flat(parts, rows):
    flat = jnp.concatenate([p.astype(F32).reshape(-1) for p in parts])
    return jnp.pad(flat, (0, rows * PACK_COLS - flat.shape[0])).reshape(rows, PACK_COLS)


def _unpack_flat(buf, shapes):
    flat, out, off = buf.reshape(-1), [], 0
    for shp in shapes:
        n = math.prod(shp)
        out.append(flat[off:off + n].reshape(shp))
        off += n
    return out


def kernel(x, positions, norm_mix_pre, norm_mix_post, norm_ffn_pre, norm_ffn_post, w_in, mla_q_norm, mla_w_q_up, mla_kv_norm, mla_w_kv_up, sc_conv_w, ssd_conv_w, ssd_conv_b, ssd_dt_bias, ssd_a_log, ssd_d, ssd_norm, w_out, ffn_w_up, ffn_conv_w, ffn_conv_b, ffn_w_down, loss_target, m_norm_mix_pre, m_norm_mix_post, m_norm_ffn_pre, m_norm_ffn_post, m_w_in, m_mla_q_norm, m_mla_w_q_up, m_mla_kv_norm, m_mla_w_kv_up, m_sc_conv_w, m_ssd_conv_w, m_ssd_conv_b, m_ssd_dt_bias, m_ssd_a_log, m_ssd_d, m_ssd_norm, m_w_out, m_ffn_w_up, m_ffn_conv_w, m_ffn_conv_b, m_ffn_w_down, v_norm_mix_pre, v_norm_mix_post, v_norm_ffn_pre, v_norm_ffn_post, v_w_in, v_mla_q_norm, v_mla_w_q_up, v_mla_kv_norm, v_mla_w_kv_up, v_sc_conv_w, v_ssd_conv_w, v_ssd_conv_b, v_ssd_dt_bias, v_ssd_a_log, v_ssd_d, v_ssd_norm, v_w_out, v_ffn_w_up, v_ffn_conv_w, v_ffn_conv_b, v_ffn_w_down):
    a = dict(locals())
    ex = _Exchange(a)
    S = {n: a[n] for n in SMALL}
    loss_part, gx, _, GS = _local_step(a["x"][0], a["positions"][0], a["loss_target"][0], None, S, ex)

    grads, delta, new_m, new_v = {}, {}, {}, {}
    for n, g in ex.finish().items():
        grads[n], delta[n], new_m[n], new_v[n] = _adam(a[n], g, a["m_" + n], a["v_" + n], "adamw_" + n)

    small_shapes = [a[n].shape for n in SMALL]
    rs = -(-(sum(math.prod(s) for s in small_shapes) + 1) // (PACK_COLS * SLAB_ALIGN)) * SLAB_ALIGN
    red = _all_reduce_small(_pack_flat([GS[n] for n in SMALL] + [loss_part.reshape(1)], rs))
    loss = _unpack_flat(red, small_shapes + [(1,)])[-1][0]
    pk = lambda pre: _pack_flat([a[pre + n] for n in SMALL], rs)
    for dst, buf in zip((grads, delta, new_m, new_v), _adam(pk(""), red, pk("m_"), pk("v_"), "adamw_small")):
        dst.update(zip(SMALL, _unpack_flat(buf, small_shapes)))

    return (loss, gx[None], *[grads[n] for n in WEIGHTS], *[delta[n] for n in WEIGHTS], *[new_m[n] for n in WEIGHTS],
            *[new_v[n] for n in WEIGHTS])
```

```python
import functools
import math

import jax
import jax.numpy as jnp
from jax import lax
from jax.experimental import pallas as pl
from jax.experimental.pallas import tpu as pltpu

F32 = jnp.float32
BF16 = jnp.bfloat16
MXU_DTYPE = jnp.bfloat16
HIGHEST = lax.Precision.HIGHEST
MESH = pl.DeviceIdType.MESH

D_MODEL = 1024
DEPTH = 4
HEADS = 8
Q_LORA = 256
KV_LORA = 128
NOPE = 64
ROPE = 32
VDIM = 64
ROPE_THETA = 10000.0
SC_DIM = 256
SSD_HEADS = 4
SSD_HEAD_DIM = 64
SSD_STATE = 128
SSD_DIM = 256
SSD_CONV_DIM = 768
SSD_CHUNK = 128
FFN_DIM = 2816
NORM_EPS = 1e-6
QK_SCALE = (NOPE + ROPE) ** -0.5
LANE = 128
HP = 128
FLASH_HEADS = 4

ZIN = 2560
Z_CQ, Z_CKV, Z_KR, Z_SCB, Z_SCC, Z_SCH, Z_SSZ, Z_XBC, Z_DT = 0, 256, 384, 512, 768, 1024, 1280, 1536, 2304
KR_LANE = 64
YCAT = HEADS * HP + SC_DIM + SSD_DIM
FFN_TILE = 256
FFN_ROWS = 1024
ROW_BLOCK = 512

ADAM_LR, ADAM_B1, ADAM_B2, ADAM_EPS, ADAM_WD, ADAM_STEP = 0.001, 0.9, 0.999, 1e-08, 0.01, 10

PACK_COLS = 1024


def _tile(n, pref):
    if n <= pref:
        return n
    t = (pref // LANE) * LANE
    while t >= LANE:
        if n % t == 0:
            return t
        t -= LANE
    raise ValueError(f"no tile for {n}")


MM_TM, MM_TN, MM_TK = 1024, 1408, 1536


def _mm(a, b, mode, out_dtype, name, tm=None, tn=MM_TN, tkmax=MM_TK):
    pair = isinstance(a, tuple)
    a_list = list(a) if pair else [a]
    layer = None
    if isinstance(b, tuple):
        b, layer = b
    bshape = b.shape[-2:]
    if mode == "nn":
        (M, Ka), (_, N) = a_list[0].shape, bshape
    elif mode == "nt":
        (M, Ka), (N, _) = a_list[0].shape, bshape
    else:
        (Ka, M), (_, N) = a_list[0].shape, bshape
    tm = (MM_TN if mode == "tn" else MM_TM) if tm is None else tm
    tm, tn, tk = _tile(M, tm), _tile(N, tn), _tile(Ka, tkmax)
    nka = Ka // tk
    nk = nka * len(a_list)

    def bspec(shape, index):
        if layer is None:
            return pl.BlockSpec(shape, index)
        return pl.BlockSpec((None,) + shape, lambda i, j, k: (layer,) + index(i, j, k))

    if mode == "nn":
        a_specs = [pl.BlockSpec((tm, tk), lambda i, j, k: (i, jnp.minimum(k, nka - 1))),
                   pl.BlockSpec((tm, tk), lambda i, j, k: (i, jnp.maximum(k - nka, 0)))][:len(a_list)]
        b_spec = bspec((tk, tn), lambda i, j, k: (k, j))
        dims = NN
    elif mode == "nt":
        a_specs = [pl.BlockSpec((tm, tk), lambda i, j, k: (i, jnp.minimum(k, nka - 1))),
                   pl.BlockSpec((tm, tk), lambda i, j, k: (i, jnp.maximum(k - nka, 0)))][:len(a_list)]
        b_spec = bspec((tn, tk), lambda i, j, k: (j, k))
        dims = NT
    else:
        a_specs = [pl.BlockSpec((tk, tm), lambda i, j, k: (k, i))]
        b_spec = pl.BlockSpec((tk, tn), lambda i, j, k: (k, j))
        dims = TN
    na = len(a_list)

    def body(*refs):
        a_refs, b_ref, o_ref = refs[:na], refs[na], refs[na + 1]
        k = pl.program_id(2)

        def prod(a_ref):
            return lax.dot_general(a_ref[...].astype(MXU_DTYPE), b_ref[...].astype(MXU_DTYPE), dims, preferred_element_type=F32)

        if nk == 1:
            o_ref[...] = prod(a_refs[0]).astype(o_ref.dtype)
            return
        acc_ref = refs[na + 2]

        @pl.when(k == 0)
        def _():
            acc_ref[...] = prod(a_refs[0])

        @pl.when((k > 0) & (k < nka))
        def _():
            acc_ref[...] += prod(a_refs[0])

        if pair:
            @pl.when(k >= nka)
            def _():
                acc_ref[...] += prod(a_refs[1])

        @pl.when(k == nk - 1)
        def _():
            o_ref[...] = acc_ref[...].astype(o_ref.dtype)

    return pl.pallas_call(
        body, name=name, grid=(M // tm, N // tn, nk),
        in_specs=a_specs + [b_spec], out_specs=pl.BlockSpec((tm, tn), lambda i, j, k: (i, j)),
        out_shape=jax.ShapeDtypeStruct((M, N), out_dtype),
        scratch_shapes=[pltpu.VMEM((tm, tn), F32)] if nk > 1 else [],
        compiler_params=pltpu.CompilerParams(dimension_semantics=("parallel", "parallel", "arbitrary")),
    )(*a_list, b)


HALO = 8


def _const(j, v):
    return v


def _rows(fn, T, tm, ins, consts, outs, accs, name, ncol=1):
    n = T // tm
    hb = tm // HALO
    last = T // HALO - 1
    in_specs, args = [], []
    for arr, bc, cb, kind in ins:
        if isinstance(kind, int):
            in_specs.append(pl.BlockSpec((tm, bc), lambda j, i, cb=cb, off=kind: (i + off, cb(j))))
        elif kind == "cur":
            in_specs.append(pl.BlockSpec((tm, bc), lambda j, i, cb=cb: (i, cb(j))))
        elif kind == "prev":
            in_specs.append(pl.BlockSpec((HALO, bc), lambda j, i, cb=cb: (jnp.maximum(i * hb - 1, 0), cb(j))))
        else:
            in_specs.append(pl.BlockSpec((HALO, bc), lambda j, i, cb=cb: (jnp.minimum((i + 1) * hb, last), cb(j))))
        args.append(arr)
    for arr, bc, cb in consts:
        in_specs.append(pl.BlockSpec((arr.shape[0], bc), lambda j, i, cb=cb: (0, cb(j))))
        args.append(arr)
    out_specs, out_shape = [], []
    for tc, dt, bc, cb in outs:
        out_specs.append(pl.BlockSpec((tm, bc), lambda j, i, cb=cb: (i, cb(j))))
        out_shape.append(jax.ShapeDtypeStruct((T, tc), dt))
    for r, tc, bc, cb in accs:
        out_specs.append(pl.BlockSpec((r, bc), lambda j, i, cb=cb: (0, cb(j))))
        out_shape.append(jax.ShapeDtypeStruct((r, tc), F32))
    nin, nout, nacc = len(args), len(outs), len(accs)

    def body(*refs):
        i = pl.program_id(1)
        res = fn(i, n, *[r[...] for r in refs[:nin]])
        for r, v in zip(refs[nin:nin + nout], res[:nout]):
            r[...] = v.astype(r.dtype)
        if nacc:
            acc_refs = refs[nin + nout:nin + nout + nacc]

            @pl.when(i == 0)
            def _():
                for r in acc_refs:
                    r[...] = jnp.zeros_like(r)

            for r, v in zip(acc_refs, res[nout:]):
                r[...] += v.astype(F32)

    res = pl.pallas_call(
        body, name=name, grid=(ncol, n), in_specs=in_specs, out_specs=out_specs, out_shape=out_shape,
        compiler_params=pltpu.CompilerParams(dimension_semantics=("arbitrary", "arbitrary")),
    )(*args)
    return res


def _cur(arr, bc=None, blk=0):
    bc = arr.shape[1] if bc is None else bc
    return (arr, bc, functools.partial(_const, v=blk), "cur")


def _halo(arr, kind, bc=None, blk=0):
    bc = arr.shape[1] if bc is None else bc
    return (arr, bc, functools.partial(_const, v=blk), kind)


def _cst(arr):
    return (arr, arr.shape[1], functools.partial(_const, v=0))


def _out(cols, dt):
    return (cols, dt, cols, functools.partial(_const, v=0))


def _acc(rows, cols):
    return (rows, cols, cols, functools.partial(_const, v=0))


def _rms(x, w):
    return x * lax.rsqrt(jnp.mean(x * x, axis=-1, keepdims=True) + NORM_EPS) * w


def _sigmoid(x):
    return 0.5 * jnp.tanh(0.5 * x) + 0.5


def _silu(x):
    return x * _sigmoid(x)


def _dsilu(x):
    s = _sigmoid(x)
    return s * (1.0 + x * (1.0 - s))


def _softplus(x):
    return jnp.maximum(x, 0.0) + jnp.log1p(jnp.exp(-jnp.abs(x)))


def _shift(a, k):
    return pltpu.roll(a, k % a.shape[0], 0)


def _lroll(a, k):
    return pltpu.roll(a, k % a.shape[1], 1)


def _vjp_wrap(f, nrow, nconst, add_first=False):
    def g(i, n, *vals):
        rows, consts, mid = vals[:nrow], vals[len(vals) - nconst:], vals[nrow:len(vals) - nconst]
        cots = mid[:-1] if add_first else mid
        outs, pull = jax.vjp(f, *rows, *consts)
        grads = list(pull(tuple(c.astype(o.dtype) for c, o in zip(cots, outs))))
        if add_first:
            grads[0] = grads[0] + mid[-1]
        return tuple(grads)
    return g


def _rows_vjp(f, T, tm, rows, consts, cots, out_dtypes, name):
    return _rows(_vjp_wrap(f, len(rows), len(consts)), T, tm, [_cur(r) for r in rows] + [_cur(c) for c in cots],
                 [_cst(c) for c in consts], [_out(r.shape[1], dt) for r, dt in zip(rows, out_dtypes)],
                 [_acc(1, c.shape[1]) for c in consts], name)


def _f_premix(x, g):
    return (_rms(x, g),)


def _f_mla_pre(cq, ckv, qn, kvn):
    return _rms(cq, qn), _rms(ckv, kvn)


def _f_ssd_gate(y, z, nw):
    return (_rms(y * _silu(z), nw),)


def _f_post_mix(x, mixed, gpost, gffn):
    x1 = x + _rms(mixed, gpost)
    return x1, _rms(x1, gffn)


def _f_post_ffn(x1, d, gpost):
    return (x1 + _rms(d, gpost),)


def _rope_fwd(v, cosf, sina, sinb):
    return v * cosf + _lroll(v, -16) * sina + _lroll(v, 16) * sinb


def _rope_bwd(g, cosf, sina, sinb):
    return g * cosf + _lroll(g * sina, 16) + _lroll(g * sinb, -16)


def _k_rope_fwd(i, n, qpad, kvpad, kr, cosf, sina, sinb):
    qs, ks = [], []
    krr = _rope_fwd(kr, cosf, sina, sinb)
    for h in range(HEADS):
        sl = slice(h * HP, (h + 1) * HP)
        qs.append(_rope_fwd(qpad[:, sl], cosf, sina, sinb))
        ks.append(kvpad[:, sl].astype(F32) + krr)
    return jnp.concatenate(qs, axis=1), jnp.concatenate(ks, axis=1)


def _k_rope_bwd(i, n, dq, dk, dv, cosf, sina, sinb):
    lane = lax.broadcasted_iota(jnp.int32, (1, HP), 1)
    rmask = ((lane >= KR_LANE) & (lane < KR_LANE + ROPE)).astype(F32)
    dqs, dks = [], []
    dkr = jnp.zeros((dq.shape[0], HP), F32)
    for h in range(HEADS):
        sl = slice(h * HP, (h + 1) * HP)
        dqs.append(_rope_bwd(dq[:, sl], cosf, sina, sinb))
        dkh = dk[:, sl]
        dkr = dkr + dkh * rmask
        dks.append(dkh * (1.0 - rmask))
    dkr = _rope_bwd(dkr, cosf, sina, sinb) * rmask
    return jnp.concatenate(dqs, axis=1), jnp.concatenate(dks + [dv], axis=1), dkr


def _k_sconv_fwd(i, n, b, c, h, cp, hp, w):
    m = b.shape[0]
    up = jnp.where(i > 0, cp * hp, 0.0)
    ue = jnp.concatenate([up, c * h], axis=0)
    conv = w[2:3] * ue + w[1:2] * _shift(ue, 1) + w[0:1] * _shift(ue, 2)
    return (b * conv[HALO:],)


def _k_sconv_bwd(i, n, b, c, h, dy, cp, hp, bn, dyn, w):
    m = b.shape[0]
    up = jnp.where(i > 0, cp * hp, 0.0)
    ue = jnp.concatenate([up, c * h], axis=0)
    u1, u2 = _shift(ue, 1), _shift(ue, 2)
    conv = (w[2:3] * ue + w[1:2] * u1 + w[0:1] * u2)[HALO:]
    dc_cur = dy * b
    dce = jnp.concatenate([dc_cur, jnp.where(i < n - 1, dyn * bn, 0.0)], axis=0)
    du = (w[2:3] * dce + w[1:2] * _shift(dce, -1) + w[0:1] * _shift(dce, -2))[:m]
    dw = jnp.concatenate([
        jnp.sum(dc_cur * u2[HALO:], axis=0, keepdims=True),
        jnp.sum(dc_cur * u1[HALO:], axis=0, keepdims=True),
        jnp.sum(dc_cur * ue[HALO:], axis=0, keepdims=True),
        jnp.zeros((HALO - 3, b.shape[1]), F32)], axis=0)
    return dy * conv, du * h, du * c, dw


def _conv4(ue, w):
    return w[3:4] * ue + w[2:3] * _shift(ue, 1) + w[1:2] * _shift(ue, 2) + w[0:1] * _shift(ue, 3)


def _k_ssdconv_fwd(i, n, u, up, w, bias):
    ue = jnp.concatenate([jnp.where(i > 0, up, 0.0), u], axis=0)
    return (_silu(_conv4(ue, w)[HALO:] + bias),)


def _k_ssdconv_bwd(i, n, u, dout, up, un, doutn, w, bias):
    m = u.shape[0]
    ue = jnp.concatenate([jnp.where(i > 0, up, 0.0), u, un], axis=0)
    u1, u2, u3 = _shift(ue, 1), _shift(ue, 2), _shift(ue, 3)
    pre = (w[3:4] * ue + w[2:3] * u1 + w[1:2] * u2 + w[0:1] * u3)[HALO:] + bias
    doe = jnp.concatenate([dout, jnp.where(i < n - 1, doutn, 0.0)], axis=0)
    dpre = doe * _dsilu(pre)
    du = (w[3:4] * dpre + w[2:3] * _shift(dpre, -1) + w[1:2] * _shift(dpre, -2) + w[0:1] * _shift(dpre, -3))[:m]
    dp = dpre[:m]
    cur = slice(HALO, HALO + m)
    dw = jnp.concatenate([
        jnp.sum(dp * u3[cur], axis=0, keepdims=True),
        jnp.sum(dp * u2[cur], axis=0, keepdims=True),
        jnp.sum(dp * u1[cur], axis=0, keepdims=True),
        jnp.sum(dp * ue[cur], axis=0, keepdims=True),
        jnp.zeros((HALO - 4, u.shape[1]), F32)], axis=0)
    db = jnp.sum(dp, axis=0, keepdims=True)
    return du, dw, db


def _conv3(ue, w):
    return w[2:3] * ue + w[1:2] * _shift(ue, 1) + w[0:1] * _shift(ue, 2)


def _k_ffnact_fwd(i, n, ug, uu, ugp, uup, wg, wu, bg, bu):
    gate = _conv3(jnp.concatenate([jnp.where(i > 0, ugp, 0.0), ug], axis=0), wg)[HALO:] + bg
    upv = _conv3(jnp.concatenate([jnp.where(i > 0, uup, 0.0), uu], axis=0), wu)[HALO:] + bu
    return (_silu(gate) * upv,)


def _k_ffnact_bwd(i, n, ug, uu, dact, ugp, uup, ugn, uun, dactn, wg, wu, bg, bu):
    m = ug.shape[0]
    cur = slice(HALO, HALO + m)

    def taps(p, c, nx):
        e = jnp.concatenate([jnp.where(i > 0, p, 0.0), c, nx], axis=0)
        return e, _shift(e, 1), _shift(e, 2)

    def back(d, w):
        return (w[2:3] * d + w[1:2] * _shift(d, -1) + w[0:1] * _shift(d, -2))[:m]

    def wgrad(d, t):
        return jnp.concatenate([jnp.sum(d[:m] * t[2][cur], axis=0, keepdims=True), jnp.sum(d[:m] * t[1][cur], axis=0, keepdims=True),
                                jnp.sum(d[:m] * t[0][cur], axis=0, keepdims=True), jnp.zeros((HALO - 3, d.shape[1]), F32)], axis=0)

    tg, tu = taps(ugp, ug, ugn), taps(uup, uu, uun)
    gate = (wg[2:3] * tg[0] + wg[1:2] * tg[1] + wg[0:1] * tg[2])[HALO:] + bg
    upv = (wu[2:3] * tu[0] + wu[1:2] * tu[1] + wu[0:1] * tu[2])[HALO:] + bu
    dae = jnp.concatenate([dact, jnp.where(i < n - 1, dactn, 0.0)], axis=0)
    sg = _sigmoid(gate)
    dg = dae * upv * (sg * (1.0 + gate * (1.0 - sg)))
    dup = dae * (gate * sg)
    return (back(dg, wg), back(dup, wu), wgrad(dg, tg), wgrad(dup, tu),
            jnp.sum(dg[:m], axis=0, keepdims=True), jnp.sum(dup[:m], axis=0, keepdims=True))


def _k_loss(i, n, y, tgt):
    e = y - tgt
    part = 0.5 * jnp.sum(jnp.sum(e * e, axis=1, keepdims=True) / D_MODEL, axis=0, keepdims=True)
    return e * (1.0 / D_MODEL), jnp.broadcast_to(part, (1, LANE))


def _k_adam(i, n, w, g, m, v):
    m = ADAM_B1 * m + (1.0 - ADAM_B1) * g
    v = ADAM_B2 * v + (1.0 - ADAM_B2) * (g * g)
    m_hat = m / (1.0 - ADAM_B1 ** ADAM_STEP)
    v_hat = v / (1.0 - ADAM_B2 ** ADAM_STEP)
    delta = -ADAM_LR * (m_hat / (jnp.sqrt(v_hat) + ADAM_EPS) + ADAM_WD * w)
    return g, delta, m, v


def _dotf(a, b, dims):
    return lax.dot_general(a.astype(MXU_DTYPE), b.astype(MXU_DTYPE), dims, preferred_element_type=F32)


NN = (((1,), (0,)), ((), ()))
NT = (((1,), (1,)), ((), ()))
TN = (((0,), (0,)), ((), ()))


def _ssd_chunk(x0, x1, x2, x3, b0, b1, c0, c1, dtraw, p0, p1, p2, p3, dtb, alog, dsk):
    xs, bs, cs_, ps = (x0, x1, x2, x3), (b0, b1), (c0, c1), (p0, p1, p2, p3)
    L = dtraw.shape[0]
    dt = _softplus(dtraw + dtb)
    adt = dt * (-jnp.exp(alog))
    row = lax.broadcasted_iota(jnp.int32, (L, L), 0)
    col = lax.broadcasted_iota(jnp.int32, (L, L), 1)
    tril = row >= col
    cum = jnp.dot(tril.astype(F32), adt, precision=HIGHEST, preferred_element_type=F32)
    cum_t = cum.T
    lane = lax.broadcasted_iota(jnp.int32, (1, LANE), 1)
    sub = lax.broadcasted_iota(jnp.int32, (LANE, 1), 0)
    lastcol = (lax.broadcasted_iota(jnp.int32, (1, L), 1) == L - 1).astype(F32)
    ys, news = [], []
    for h in range(SSD_HEADS):
        g = h // (SSD_HEADS // 2)
        oh = (lane == h).astype(F32)
        dth = jnp.sum(dt * oh, axis=1, keepdims=True)
        csh = jnp.sum(cum * oh, axis=1, keepdims=True)
        csr = jnp.sum(cum_t * (sub == h).astype(F32), axis=0, keepdims=True)
        cl = jnp.sum(csr * lastcol, axis=1, keepdims=True)
        dskh = jnp.sum(dsk * oh, axis=1, keepdims=True)
        x, bm, cm, prev = xs[h], bs[g], cs_[g], ps[h]
        xdt = x * dth
        decay = jnp.exp(jnp.where(tril, csh - csr, -jnp.inf))
        scores = _dotf(cm, bm, NT) * decay
        y_diag = _dotf(scores, xdt, NN)
        bd = bm * jnp.exp(cl - csh)
        cst = _dotf(xdt, bd, TN)
        news.append(prev * jnp.exp(cl) + cst)
        y_off = _dotf(cm, prev, NT) * jnp.exp(csh)
        ys.append(y_diag + y_off + x * dskh)
    return (*ys, *news)


SSD_STEP = 2


def _ssd_operands(x_ref, dt_ref, par_ref, prev, rows):
    xs = [x_ref[rows, h * SSD_HEAD_DIM:(h + 1) * SSD_HEAD_DIM] for h in range(SSD_HEADS)]
    bs = [x_ref[rows, SSD_DIM + g * SSD_STATE:SSD_DIM + (g + 1) * SSD_STATE] for g in range(2)]
    cs_ = [x_ref[rows, SSD_DIM + 2 * SSD_STATE + g * SSD_STATE:SSD_DIM + 2 * SSD_STATE + (g + 1) * SSD_STATE] for g in range(2)]
    return (*xs, *bs, *cs_, dt_ref[rows, :], *prev, par_ref[0:1, :], par_ref[1:2, :], par_ref[2:3, :])


def _ssd_fwd(xbc, dtraw, par, T, dt_blk=0):
    L = SSD_CHUNK
    nc = T // L
    P = SSD_HEAD_DIM
    U = SSD_STEP if nc % SSD_STEP == 0 else 1

    def body(x_ref, dt_ref, par_ref, y_ref, st_ref, state):
        @pl.when(pl.program_id(0) == 0)
        def _():
            state[...] = jnp.zeros_like(state)

        for u in range(U):
            rows = slice(u * L, (u + 1) * L)
            st_ref[u] = state[...]
            prev = [state[h * P:(h + 1) * P, :] for h in range(SSD_HEADS)]
            res = _ssd_chunk(*_ssd_operands(x_ref, dt_ref, par_ref, prev, rows))
            for h in range(SSD_HEADS):
                y_ref[rows, h * P:(h + 1) * P] = res[h]
                state[h * P:(h + 1) * P, :] = res[SSD_HEADS + h]

    return pl.pallas_call(
        body, name="ssd_scan_fwd", grid=(nc // U,),
        in_specs=[pl.BlockSpec((U * L, SSD_CONV_DIM), lambda c: (c, 0)), pl.BlockSpec((U * L, LANE), lambda c: (c, dt_blk)),
                  pl.BlockSpec((8, LANE), lambda c: (0, 0))],
        out_specs=[pl.BlockSpec((U * L, SSD_DIM), lambda c: (c, 0)), pl.BlockSpec((U, SSD_DIM, SSD_STATE), lambda c: (c, 0, 0))],
        out_shape=[jax.ShapeDtypeStruct((T, SSD_DIM), F32), jax.ShapeDtypeStruct((nc, SSD_DIM, SSD_STATE), F32)],
        scratch_shapes=[pltpu.VMEM((SSD_DIM, SSD_STATE), F32)],
        compiler_params=pltpu.CompilerParams(dimension_semantics=("arbitrary",)),
    )(xbc, dtraw, par)


def _ssd_bwd(xbc, dtraw, par, states, dy, T, dt_blk=0):
    L = SSD_CHUNK
    nc = T // L
    P = SSD_HEAD_DIM
    U = SSD_STEP if nc % SSD_STEP == 0 else 1
    ns = nc // U

    def body(x_ref, dt_ref, par_ref, st_ref, dy_ref, dx_ref, ddt_ref, dpar_ref, dstate):
        @pl.when(pl.program_id(0) == 0)
        def _():
            dstate[...] = jnp.zeros_like(dstate)
            dpar_ref[...] = jnp.zeros_like(dpar_ref)

        for u in reversed(range(U)):
            rows = slice(u * L, (u + 1) * L)
            prev = [st_ref[u, h * P:(h + 1) * P, :] for h in range(SSD_HEADS)]
            prim = _ssd_operands(x_ref, dt_ref, par_ref, prev, rows)
            _, pull = jax.vjp(_ssd_chunk, *prim)
            cots = tuple(dy_ref[rows, h * P:(h + 1) * P] for h in range(SSD_HEADS)) + tuple(
                dstate[h * P:(h + 1) * P, :] for h in range(SSD_HEADS))
            g = pull(cots)
            for h in range(SSD_HEADS):
                dx_ref[rows, h * P:(h + 1) * P] = g[h]
                dstate[h * P:(h + 1) * P, :] = g[9 + h]
            for k in range(2):
                dx_ref[rows, SSD_DIM + k * SSD_STATE:SSD_DIM + (k + 1) * SSD_STATE] = g[4 + k]
                dx_ref[rows, SSD_DIM + 2 * SSD_STATE + k * SSD_STATE:SSD_DIM + 2 * SSD_STATE + (k + 1) * SSD_STATE] = g[6 + k]
            ddt_ref[rows, :] = g[8]
            for r in range(3):
                dpar_ref[r:r + 1, :] += g[13 + r]

    rev = lambda c: (ns - 1 - c, 0)
    return pl.pallas_call(
        body, name="ssd_scan_bwd", grid=(ns,),
        in_specs=[pl.BlockSpec((U * L, SSD_CONV_DIM), rev), pl.BlockSpec((U * L, LANE), lambda c: (ns - 1 - c, dt_blk)),
                  pl.BlockSpec((8, LANE), lambda c: (0, 0)),
                  pl.BlockSpec((U, SSD_DIM, SSD_STATE), lambda c: (ns - 1 - c, 0, 0)), pl.BlockSpec((U * L, SSD_DIM), rev)],
        out_specs=[pl.BlockSpec((U * L, SSD_CONV_DIM), rev), pl.BlockSpec((U * L, LANE), rev), pl.BlockSpec((8, LANE), lambda c: (0, 0))],
        out_shape=[jax.ShapeDtypeStruct((T, SSD_CONV_DIM), F32), jax.ShapeDtypeStruct((T, LANE), F32),
                   jax.ShapeDtypeStruct((8, LANE), F32)],
        scratch_shapes=[pltpu.VMEM((SSD_DIM, SSD_STATE), F32)],
        compiler_params=pltpu.CompilerParams(dimension_semantics=("arbitrary",)),
    )(xbc, dtraw, par, states, dy)


def _causal_pairs(nq, by_query):
    if by_query:
        pairs = [(i, j) for i in range(nq) for j in range(i + 1)]
    else:
        pairs = [(i, j) for j in range(nq) for i in range(j, nq)]
    return jnp.asarray([p[0] for p in pairs], jnp.int32), jnp.asarray([p[1] for p in pairs], jnp.int32)


def _flash_fwd(q, k, kv, T, carry=()):
    tq = tk = min(512, T)
    nq = T // tq
    G = FLASH_HEADS
    rep = tk // HP
    nc = len(carry)
    qi, kj = _causal_pairs(nq, by_query=True)
    nh, nt = HEADS // G, qi.shape[0]

    def body(qi_ref, kj_ref, q_ref, k_ref, v_ref, *rest):
        w_refs, o_ref, g_refs = rest[:nc], rest[nc], rest[nc + 1:2 * nc + 1]
        m_ref, l_ref, acc_ref = rest[2 * nc + 1:2 * nc + 4]
        h, t = pl.program_id(0), pl.program_id(1)
        i, j = qi_ref[t], kj_ref[t]
        if nc:
            plan = lambda: _ag_plan(w_refs, g_refs, rest[2 * nc + 4:])

            @pl.when((h == 0) & (t == 0))
            def _():
                for cp in plan()[0]:
                    cp.start()

        @pl.when(j == 0)
        def _():
            m_ref[...] = jnp.full_like(m_ref, -jnp.inf)
            l_ref[...] = jnp.zeros_like(l_ref)
            acc_ref[...] = jnp.zeros_like(acc_ref)

        def step(diagonal):
            for g in range(G):
                sl = slice(g * HP, (g + 1) * HP)
                s = _dotf(q_ref[:, sl], k_ref[:, sl], NT) * QK_SCALE
                if diagonal:
                    rows = lax.broadcasted_iota(jnp.int32, (tq, tk), 0)
                    cols = lax.broadcasted_iota(jnp.int32, (tq, tk), 1)
                    s = jnp.where(rows >= cols, s, -jnp.inf)
                m_old = m_ref[:, sl]
                m_new = jnp.maximum(m_old, jnp.max(s, axis=1, keepdims=True))
                p = jnp.exp(s - jnp.tile(m_new, (1, rep)))
                alpha = jnp.exp(m_old - m_new)
                l_ref[:, sl] = alpha * l_ref[:, sl] + jnp.sum(p, axis=1, keepdims=True)
                acc_ref[:, sl] = alpha * acc_ref[:, sl] + _dotf(p, v_ref[:, sl], NN)
                m_ref[:, sl] = m_new

        @pl.when(j < i)
        def _():
            step(False)

        @pl.when(j == i)
        def _():
            step(True)
            lane = lax.broadcasted_iota(jnp.int32, (tq, HP), 1)
            for g in range(G):
                sl = slice(g * HP, (g + 1) * HP)
                l = l_ref[:, sl]
                o_ref[:, sl] = jnp.where(lane < VDIM, acc_ref[:, sl] / l, m_ref[:, sl] + jnp.log(l))

        if nc:
            @pl.when((h == nh - 1) & (t == nt // 2))
            def _():
                _, lands, forwards, _ = plan()
                for land, fw in zip(lands, forwards):
                    land.wait_recv()
                    fw.start()

            @pl.when((h == nh - 1) & (t == nt - 1))
            def _():
                sends, _, forwards, finals = plan()
                for cp in finals:
                    cp.wait_recv()
                for cp in sends + forwards:
                    cp.wait_send()

    W = G * HP
    res = pl.pallas_call(
        body, name="mla_flash_fwd",
        grid_spec=pltpu.PrefetchScalarGridSpec(
            num_scalar_prefetch=2, grid=(nh, nt),
            in_specs=[pl.BlockSpec((tq, W), lambda h, t, qi, kj: (qi[t], h)),
                      pl.BlockSpec((tk, W), lambda h, t, qi, kj: (kj[t], h)),
                      pl.BlockSpec((tk, W), lambda h, t, qi, kj: (kj[t], HEADS // G + h))] + [ANY] * nc,
            out_specs=[pl.BlockSpec((tq, W), lambda h, t, qi, kj: (qi[t], h))] + [ANY] * nc,
            scratch_shapes=[pltpu.VMEM((tq, W), F32), pltpu.VMEM((tq, W), F32), pltpu.VMEM((tq, W), F32)] + (_ag_sems(nc) if nc else [])),
        out_shape=[jax.ShapeDtypeStruct((T, HEADS * HP), F32)] + [jax.ShapeDtypeStruct((N_CHIPS,) + w.shape, w.dtype) for w in carry],
        compiler_params=pltpu.CompilerParams(dimension_semantics=("arbitrary", "arbitrary")),
    )(qi, kj, q, k, kv, *carry)
    return res[0] if not nc else (res[0], [_own_slot(g, w) for g, w in zip(res[1:], carry)])


def _flash_bwd(q, k, kv, o, dycat, T, carry=()):
    tq = tk = min(512, T)
    nq = T // tq
    G = FLASH_HEADS
    nc = len(carry)
    qi, kj = _causal_pairs(nq, by_query=False)
    nh, nt = HEADS // G, qi.shape[0]

    def body(qi_ref, kj_ref, q_ref, k_ref, v_ref, o_ref, do_ref, *rest):
        p_refs, (dq_ref, dk_ref, dv_ref), part_refs = rest[:nc], rest[nc:nc + 3], rest[nc + 3:2 * nc + 3]
        h, t = pl.program_id(0), pl.program_id(1)
        i, j = qi_ref[t], kj_ref[t]
        if nc:
            plan = lambda: _chip_plan(p_refs, part_refs, rest[2 * nc + 3:])

            @pl.when((h == 0) & (t == 0))
            def _():
                for cp in plan()[0]:
                    cp.start()

        @pl.when(t == 0)
        def _():
            dq_ref[...] = jnp.zeros_like(dq_ref)

        @pl.when(i == j)
        def _():
            dk_ref[...] = jnp.zeros_like(dk_ref)
            dv_ref[...] = jnp.zeros_like(dv_ref)

        def step(diagonal):
            r0 = pl.multiple_of(i * tq, tq)
            for g in range(G):
                sl = slice(g * HP, (g + 1) * HP)
                qv, kv, vv, ov, dov = q_ref[:, sl], k_ref[:, sl], v_ref[:, sl], o_ref[:, sl], do_ref[:, sl]
                s = _dotf(qv, kv, NT) * QK_SCALE
                p = jnp.exp(s - ov[:, VDIM:VDIM + 1])
                if diagonal:
                    rows = lax.broadcasted_iota(jnp.int32, (tq, tk), 0)
                    cols = lax.broadcasted_iota(jnp.int32, (tq, tk), 1)
                    p = jnp.where(rows >= cols, p, 0.0)
                dsum = jnp.sum(dov * ov, axis=1, keepdims=True)
                dv_ref[:, sl] += _dotf(p, dov, TN)
                dp = _dotf(dov, vv, NT)
                ds = p * (dp - dsum) * QK_SCALE
                dk_ref[:, sl] += _dotf(ds, qv, TN)
                dq_ref[pl.ds(r0, tq), sl] += _dotf(ds, kv, NN)

        @pl.when(i > j)
        def _():
            step(False)

        @pl.when(i == j)
        def _():
            step(True)

        if nc:
            @pl.when((h == nh - 1) & (t == nt - 1))
            def _():
                sends, lands = plan()
                for cp in lands:
                    cp.wait_recv()
                for cp in sends:
                    cp.wait_send()

    W = G * HP
    qmap = lambda h, t, qi, kj: (qi[t], h)
    kmap = lambda h, t, qi, kj: (kj[t], h)
    vmap = lambda h, t, qi, kj: (kj[t], HEADS // G + h)
    res = pl.pallas_call(
        body, name="mla_flash_bwd",
        grid_spec=pltpu.PrefetchScalarGridSpec(
            num_scalar_prefetch=2, grid=(nh, nt),
            in_specs=[pl.BlockSpec((tq, W), qmap), pl.BlockSpec((tk, W), kmap), pl.BlockSpec((tk, W), vmap),
                      pl.BlockSpec((tq, W), qmap), pl.BlockSpec((tq, W), qmap)] + [ANY] * nc,
            out_specs=[pl.BlockSpec((T, W), lambda h, t, qi, kj: (0, h)), pl.BlockSpec((tk, W), kmap), pl.BlockSpec((tk, W), kmap)]
            + [ANY] * nc,
            scratch_shapes=_chip_sems(nc) if nc else []),
        out_shape=[jax.ShapeDtypeStruct((T, HEADS * HP), F32)] * 3 + [jax.ShapeDtypeStruct(p.shape, p.dtype) for p in carry],
        compiler_params=pltpu.CompilerParams(dimension_semantics=("arbitrary", "arbitrary")),
    )(qi, kj, q, k, kv, o, dycat, *carry)
    return tuple(res[:3]) if not nc else (*res[:3], _chip_parts(res[3:], carry))


_IN_SRC = (0, 256, 384, 416, 672, 928, 1184, 1440, 2208, 2212)
_IN_DST = (Z_CQ, Z_CKV, Z_KR + KR_LANE, Z_SCB, Z_SCC, Z_SCH, Z_SSZ, Z_XBC, Z_DT)


def _pad_rows_in(w):
    ax = w.ndim - 2

    def zeros(n):
        return jnp.zeros(w.shape[:ax] + (n,) + w.shape[ax + 1:], w.dtype)

    def whole_tiles(p):
        n = p.shape[ax]
        return p if n % SLAB_ALIGN == 0 else jnp.pad(p, [(0, 0)] * ax + [(0, -n % SLAB_ALIGN), (0, 0)])

    parts, at = [], 0
    for s0, s1, d0 in zip(_IN_SRC[:-1], _IN_SRC[1:], _IN_DST):
        if d0 > at:
            parts.append(zeros(d0 - at))
        parts.append(whole_tiles(lax.slice_in_dim(w, s0, s1, axis=ax)))
        at = d0 + parts[-1].shape[ax]
    parts.append(zeros(ZIN - at))
    return jnp.concatenate(parts, axis=ax)


def _unpad_rows_in(w):
    ax = w.ndim - 2
    groups = list(zip(_IN_SRC[:-1], _IN_SRC[1:], _IN_DST))
    parts = [lax.slice_in_dim(w, d0, d0 + -(-(s1 - s0) // SLAB_ALIGN) * SLAB_ALIGN, axis=ax) for s0, s1, d0 in groups]
    return lax.slice_in_dim(jnp.concatenate(parts, axis=ax), 0, _IN_SRC[-1], axis=ax)


def _pad_heads(w, width):
    w = w.reshape(w.shape[:-1] + (HEADS, width))
    w = jnp.pad(w, [(0, 0)] * (w.ndim - 1) + [(0, HP - width)])
    return w.reshape(w.shape[:-2] + (HEADS * HP,))


def _unpad_heads(w, width):
    w = w.reshape(w.shape[:-1] + (HEADS, HP))[..., :width]
    return w.reshape(w.shape[:-2] + (HEADS * width,))


def _pad_kv(w):
    w = w.reshape(w.shape[:-1] + (HEADS, NOPE + VDIM))
    return jnp.concatenate([_pad_heads(w[..., :NOPE].reshape(w.shape[:-2] + (HEADS * NOPE,)), NOPE),
                            _pad_heads(w[..., NOPE:].reshape(w.shape[:-2] + (HEADS * VDIM,)), VDIM)], axis=-1)


def _unpad_kv(w):
    k = _unpad_heads(w[..., :HEADS * HP], NOPE).reshape(w.shape[:-1] + (HEADS, NOPE))
    v = _unpad_heads(w[..., HEADS * HP:], VDIM).reshape(w.shape[:-1] + (HEADS, VDIM))
    return jnp.concatenate([k, v], axis=-1).reshape(w.shape[:-1] + (HEADS * (NOPE + VDIM),))


def _pad_out_rows(w):
    lead, d = w.shape[:-2], w.shape[-1]
    att = w[..., :HEADS * VDIM, :].reshape(lead + (HEADS, VDIM, d))
    att = jnp.pad(att, [(0, 0)] * (att.ndim - 2) + [(0, HP - VDIM), (0, 0)]).reshape(lead + (HEADS * HP, d))
    return jnp.concatenate([att, w[..., HEADS * VDIM:, :]], axis=-2)


def _unpad_out_rows(w):
    lead, d = w.shape[:-2], w.shape[-1]
    att = w[..., :HEADS * HP, :].reshape(lead + (HEADS, HP, d))[..., :VDIM, :].reshape(lead + (HEADS * VDIM, d))
    return jnp.concatenate([att, w[..., HEADS * HP:, :]], axis=-2)


def _rows8(w):
    return jnp.pad(w.astype(F32), [(0, 0)] * (w.ndim - 2) + [(0, 8 - w.shape[-2]), (0, 0)])


def _row8(*vecs):
    c = vecs[0].shape[-1]
    return jnp.concatenate([v.reshape(1, c).astype(F32) for v in vecs] + [jnp.zeros((8 - len(vecs), c), F32)], axis=0)


def _lanes(v):
    return jnp.pad(v.astype(F32), (0, LANE - v.shape[0])).reshape(1, LANE)


def _rope_tables(positions):
    inv_freq = 1.0 / (ROPE_THETA ** (jnp.arange(0, ROPE, 2, dtype=F32) / ROPE))
    ang = positions.astype(F32)[:, None] * inv_freq
    cos, sin = jnp.cos(ang), jnp.sin(ang)
    T = positions.shape[0]
    half = ROPE // 2
    one = jnp.ones((T, KR_LANE), F32)
    zero = jnp.zeros((T, KR_LANE), F32)
    tail1 = jnp.ones((T, HP - KR_LANE - ROPE), F32)
    tail0 = jnp.zeros((T, HP - KR_LANE - ROPE), F32)
    z16 = jnp.zeros((T, half), F32)
    cosf = jnp.concatenate([one, cos, cos, tail1], axis=1)
    sina = jnp.concatenate([zero, -sin, z16, tail0], axis=1)
    sinb = jnp.concatenate([zero, z16, sin, tail0], axis=1)
    return cosf, sina, sinb


def _kernel_weights(W):
    c = lambda a: a.astype(MXU_DTYPE)
    forms = dict(
        w_in=("w_in", lambda w: c(_pad_rows_in(w))),
        w_q=("mla_w_q_up", lambda w: c(_pad_heads(w, NOPE + ROPE))),
        w_kv=("mla_w_kv_up", lambda w: c(_pad_kv(w))),
        w_out=("w_out", lambda w: c(_pad_out_rows(w))),
        w_up=("ffn_w_up", c),
        w_down=("ffn_w_down", c),
        sc_w=("sc_conv_w", _rows8),
        ssd_w=("ssd_conv_w", _rows8),
        ffn_w=("ffn_conv_w", _rows8),
    )
    return {k: f(W[n]) for k, (n, f) in forms.items() if n in W}


def _layer_weights(KW, l):
    return {k: (v[l] if k in ("sc_w", "ssd_w", "ffn_w") else (v, l)) for k, v in KW.items()}


def _local_step(x, positions, target, W, S, ex=None):
    T = x.shape[0]
    tm = min(ROW_BLOCK, T)
    tm_ffn = min(FFN_ROWS, T)
    cosf, sina, sinb = _rope_tables(positions)
    if ex is None:
        KW = _kernel_weights(W)
    else:
        early = _all_gather_weights(ex.shard(0, "early"))
    saved = []
    xl = x
    for l in range(DEPTH):
        lw = _layer_weights(KW, l) if ex is None else _kernel_weights(ex.weights(early, "early"))
        g_pre = S["norm_mix_pre"][l].reshape(1, -1)
        g_post = S["norm_mix_post"][l].reshape(1, -1)
        g_fpre = S["norm_ffn_pre"][l].reshape(1, -1)
        g_fpost = S["norm_ffn_post"][l].reshape(1, -1)
        qn = S["mla_q_norm"][l].reshape(1, -1)
        kvn = S["mla_kv_norm"][l].reshape(1, -1)
        ssd_b = S["ssd_conv_b"][l].reshape(1, -1)
        ssd_par = _row8(jnp.pad(S["ssd_dt_bias"][l], (0, LANE - SSD_HEADS)), jnp.pad(S["ssd_a_log"][l], (0, LANE - SSD_HEADS)),
                        jnp.pad(S["ssd_d"][l], (0, LANE - SSD_HEADS)))
        ssd_nw = S["ssd_norm"][l].reshape(1, -1)
        ffn_b = S["ffn_conv_b"][l].reshape(1, -1)

        (h1,) = _rows(lambda i, n, *v: _f_premix(*v), T, tm, [_cur(xl)], [_cst(g_pre)], [_out(D_MODEL, BF16)], [], "pre_mix_norm")
        zin = _mm(h1, lw["w_in"], "nt", F32, "mm_in")
        qlat, kvlat = _rows(lambda i, n, *v: _f_mla_pre(*v), T, tm, [_cur(zin, Q_LORA, 0), _cur(zin, KV_LORA, Z_CKV // KV_LORA)],
                            [_cst(qn), _cst(kvn)], [_out(Q_LORA, BF16), _out(KV_LORA, BF16)], [], "mla_pre_norm")
        qpad = _mm(qlat, lw["w_q"], "nn", F32, "mm_q_up")
        kvpad = _mm(kvlat, lw["w_kv"], "nn", BF16, "mm_kv_up")
        qr, kr = _rows(_k_rope_fwd, T, tm, [_cur(qpad), _cur(kvpad, HEADS * HP, 0), _cur(zin, LANE, Z_KR // LANE),
                                            _cur(cosf), _cur(sina), _cur(sinb)], [],
                       [_out(HEADS * HP, BF16), _out(HEADS * HP, BF16)], [], "mla_rope")
        if ex is None:
            o = _flash_fwd(qr, kr, kvpad, T)
        else:
            nlate = len(ex.layouts["late"])
            o, got = _flash_fwd(qr, kr, kvpad, T, carry=ex.shard(l, "late") + (ex.shard(l + 1, "early") if l + 1 < DEPTH else []))
            lw.update(_kernel_weights(ex.weights(got[:nlate], "late")))
            early = got[nlate:]
        (yconv,) = _rows(_k_sconv_fwd, T, tm, [_cur(zin, SC_DIM, Z_SCB // SC_DIM), _cur(zin, SC_DIM, Z_SCC // SC_DIM),
                                               _cur(zin, SC_DIM, Z_SCH // SC_DIM), _halo(zin, "prev", SC_DIM, Z_SCC // SC_DIM),
                                               _halo(zin, "prev", SC_DIM, Z_SCH // SC_DIM)], [_cst(lw["sc_w"])],
                         [_out(SC_DIM, F32)], [], "short_conv_fwd")
        (xbc,) = _rows(_k_ssdconv_fwd, T, tm, [_cur(zin, SSD_CONV_DIM, Z_XBC // SSD_CONV_DIM),
                                               _halo(zin, "prev", SSD_CONV_DIM, Z_XBC // SSD_CONV_DIM)],
                       [_cst(lw["ssd_w"]), _cst(ssd_b)], [_out(SSD_CONV_DIM, F32)], [], "ssd_conv_fwd")
        yscan, states = _ssd_fwd(xbc, zin, ssd_par, T, Z_DT // LANE)
        (yssd,) = _rows(lambda i, n, *v: _f_ssd_gate(*v), T, tm, [_cur(yscan), _cur(zin, SSD_DIM, Z_SSZ // SSD_DIM)], [_cst(ssd_nw)],
                        [_out(SSD_DIM, F32)], [], "ssd_gate_fwd")
        ycat = jnp.concatenate([o.astype(BF16), yconv.astype(BF16), yssd.astype(BF16)], axis=1)
        mixed = _mm(ycat, lw["w_out"], "nn", F32, "mm_out")
        x1, h2 = _rows(lambda i, n, *v: _f_post_mix(*v), T, tm, [_cur(xl), _cur(mixed)], [_cst(g_post), _cst(g_fpre)],
                       [_out(D_MODEL, F32), _out(D_MODEL, BF16)], [], "post_mix_fwd")
        upre = _mm(h2, lw["w_up"], "nn", F32, "mm_up")
        nt = FFN_DIM // FFN_TILE
        gcol, ucol = (lambda j: j), (lambda j: j + nt)
        (act,) = _rows(_k_ffnact_fwd, T, tm_ffn,
                       [(upre, FFN_TILE, gcol, "cur"), (upre, FFN_TILE, ucol, "cur"), (upre, FFN_TILE, gcol, "prev"),
                        (upre, FFN_TILE, ucol, "prev")],
                       [(lw["ffn_w"], FFN_TILE, gcol), (lw["ffn_w"], FFN_TILE, ucol), (ffn_b, FFN_TILE, gcol), (ffn_b, FFN_TILE, ucol)],
                       [(FFN_DIM, BF16, FFN_TILE, gcol)], [], "ffn_act_fwd", ncol=nt)
        dn = _mm(act, lw["w_down"], "nn", F32, "mm_down")
        (x2,) = _rows(lambda i, n, *v: _f_post_ffn(*v), T, tm, [_cur(x1), _cur(dn)], [_cst(g_fpost)], [_out(D_MODEL, F32)], [], "post_ffn_fwd")
        saved.append(dict(lw=lw, x=xl, h1=h1, zin=zin, qlat=qlat, kvlat=kvlat, qr=qr, kr=kr, kvpad=kvpad, o=o, xbc=xbc,
                          yscan=yscan, states=states, ycat=ycat, mixed=mixed, x1=x1, h2=h2, upre=upre, act=act, dn=dn,
                          g_pre=g_pre, g_post=g_post, g_fpre=g_fpre, g_fpost=g_fpost, qn=qn, kvn=kvn, ssd_b=ssd_b,
                          ssd_par=ssd_par, ssd_nw=ssd_nw, ffn_b=ffn_b))
        xl = x2

    gx, loss_part = _rows(_k_loss, T, tm, [_cur(xl), _cur(target)], [], [_out(D_MODEL, F32)], [_acc(1, LANE)], "loss_head")

    GW = {k: [None] * DEPTH for k in ("w_in", "mla_w_q_up", "mla_w_kv_up", "sc_conv_w", "ssd_conv_w", "w_out", "ffn_w_up",
                                      "ffn_conv_w", "ffn_w_down")}
    GS = {k: [None] * DEPTH for k in ("norm_mix_pre", "norm_mix_post", "norm_ffn_pre", "norm_ffn_post", "mla_q_norm", "mla_kv_norm",
                                      "ssd_conv_b", "ssd_dt_bias", "ssd_a_log", "ssd_d", "ssd_norm", "ffn_conv_b")}
    nt = FFN_DIM // FFN_TILE
    gcol, ucol = (lambda j: j), (lambda j: j + nt)
    pending = None
    for l in reversed(range(DEPTH)):
        s = saved[l]
        lw = s["lw"]
        gx1, ddn, dgf = _rows_vjp(_f_post_ffn, T, tm, [s["x1"], s["dn"]], [s["g_fpost"]], [gx], [F32, BF16], "post_ffn_bwd")
        GS["norm_ffn_post"][l] = dgf[0]
        dact = _mm(ddn, lw["w_down"], "nt", F32, "mm_down_dx")
        GW["ffn_w_down"][l] = _mm(s["act"], ddn, "tn", BF16, "mm_down_dw")
        up = s["upre"]
        dug, duu, dwg, dwu, dbg, dbu = _rows(
            _k_ffnact_bwd, T, tm_ffn,
            [(up, FFN_TILE, gcol, "cur"), (up, FFN_TILE, ucol, "cur"), (dact, FFN_TILE, gcol, "cur"), (up, FFN_TILE, gcol, "prev"),
             (up, FFN_TILE, ucol, "prev"), (up, FFN_TILE, gcol, "next"), (up, FFN_TILE, ucol, "next"), (dact, FFN_TILE, gcol, "next")],
            [(lw["ffn_w"], FFN_TILE, gcol), (lw["ffn_w"], FFN_TILE, ucol), (s["ffn_b"], FFN_TILE, gcol), (s["ffn_b"], FFN_TILE, ucol)],
            [(FFN_DIM, BF16, FFN_TILE, gcol)] * 2,
            [(HALO, FFN_DIM, FFN_TILE, gcol)] * 2 + [(1, FFN_DIM, FFN_TILE, gcol)] * 2, "ffn_act_bwd", ncol=nt)
        GW["ffn_conv_w"][l] = jnp.concatenate([dwg[:3], dwu[:3]], axis=1)
        GS["ffn_conv_b"][l] = jnp.concatenate([dbg[0], dbu[0]])
        dh2 = _mm((dug, duu), lw["w_up"], "nt", F32, "mm_up_dx")
        GW["ffn_w_up"][l] = (_mm(s["h2"], dug, "tn", BF16, "mm_up_dw_gate"), _mm(s["h2"], duu, "tn", BF16, "mm_up_dw_up"))
        gx0, dmixed, dgp, dgf = _rows_vjp(_f_post_mix, T, tm, [s["x"], s["mixed"]], [s["g_post"], s["g_fpre"]], [gx1, dh2],
                                          [F32, BF16], "post_mix_bwd")
        GS["norm_mix_post"][l], GS["norm_ffn_pre"][l] = dgp[0], dgf[0]
        dycat = _mm(dmixed, lw["w_out"], "nt", F32, "mm_out_dx")
        GW["w_out"][l] = _unpad_out_rows(_mm(s["ycat"], dmixed, "tn", BF16, "mm_out_dw"))
        zin = s["zin"]
        dyscan, dz, dnw = _rows(_vjp_wrap(_f_ssd_gate, 2, 1), T, tm,
                                [_cur(s["yscan"]), _cur(zin, SSD_DIM, Z_SSZ // SSD_DIM), _cur(dycat, SSD_DIM, (HEADS * HP + SC_DIM) // SSD_DIM)],
                                [_cst(s["ssd_nw"])], [_out(SSD_DIM, F32), _out(SSD_DIM, BF16)], [_acc(1, SSD_DIM)], "ssd_gate_bwd")
        GS["ssd_norm"][l] = dnw[0]
        dxbc, ddtraw, dpar = _ssd_bwd(s["xbc"], zin, s["ssd_par"], s["states"], dyscan, T, Z_DT // LANE)
        GS["ssd_dt_bias"][l], GS["ssd_a_log"][l], GS["ssd_d"][l] = dpar[0, :SSD_HEADS], dpar[1, :SSD_HEADS], dpar[2, :SSD_HEADS]
        xb = Z_XBC // SSD_CONV_DIM
        dxraw, dsw, dsb = _rows(_k_ssdconv_bwd, T, tm,
                                [_cur(zin, SSD_CONV_DIM, xb), _cur(dxbc), _halo(zin, "prev", SSD_CONV_DIM, xb),
                                 _halo(zin, "next", SSD_CONV_DIM, xb), _halo(dxbc, "next")],
                                [_cst(lw["ssd_w"]), _cst(s["ssd_b"])], [_out(SSD_CONV_DIM, BF16)],
                                [_acc(HALO, SSD_CONV_DIM), _acc(1, SSD_CONV_DIM)], "ssd_conv_bwd")
        GW["ssd_conv_w"][l] = dsw[:4]
        GS["ssd_conv_b"][l] = dsb[0]
        cb = (HEADS * HP) // SC_DIM
        dscb, dscc, dsch, dscw = _rows(_k_sconv_bwd, T, tm,
                                       [_cur(zin, SC_DIM, Z_SCB // SC_DIM), _cur(zin, SC_DIM, Z_SCC // SC_DIM),
                                        _cur(zin, SC_DIM, Z_SCH // SC_DIM), _cur(dycat, SC_DIM, cb),
                                        _halo(zin, "prev", SC_DIM, Z_SCC // SC_DIM), _halo(zin, "prev", SC_DIM, Z_SCH // SC_DIM),
                                        _halo(zin, "next", SC_DIM, Z_SCB // SC_DIM), _halo(dycat, "next", SC_DIM, cb)],
                                       [_cst(lw["sc_w"])], [_out(SC_DIM, BF16)] * 3, [_acc(HALO, SC_DIM)], "short_conv_bwd")
        GW["sc_conv_w"][l] = dscw[:3]
        if ex is None:
            dq, dk, dv = _flash_bwd(s["qr"], s["kr"], s["kvpad"], s["o"], dycat, T)
        else:
            late = ex.submit({n: GW[n][l] for ns in LATE for n in ns}, "late")
            dq, dk, dv, parts = _flash_bwd(s["qr"], s["kr"], s["kvpad"], s["o"], dycat, T, carry=late + (pending or []))
            ex.collect(l, "late", parts[:len(late)])
            if pending:
                ex.collect(l + 1, "early", parts[len(late):])
        dqpad, dkvpad, dkr = _rows(_k_rope_bwd, T, tm, [_cur(dq), _cur(dk), _cur(dv), _cur(cosf), _cur(sina), _cur(sinb)], [],
                                   [_out(HEADS * HP, BF16), _out(2 * HEADS * HP, BF16), _out(LANE, BF16)], [], "mla_rope_bwd")
        dqlat = _mm(dqpad, lw["w_q"], "nt", F32, "mm_q_dx")
        GW["mla_w_q_up"][l] = _unpad_heads(_mm(s["qlat"], dqpad, "tn", BF16, "mm_q_dw"), NOPE + ROPE)
        dkvlat = _mm(dkvpad, lw["w_kv"], "nt", F32, "mm_kv_dx")
        GW["mla_w_kv_up"][l] = _unpad_kv(_mm(s["kvlat"], dkvpad, "tn", BF16, "mm_kv_dw"))
        dcq, dckv, dqn, dkvn = _rows(_vjp_wrap(_f_mla_pre, 2, 2), T, tm,
                                     [_cur(zin, Q_LORA, 0), _cur(zin, KV_LORA, Z_CKV // KV_LORA), _cur(dqlat), _cur(dkvlat)],
                                     [_cst(s["qn"]), _cst(s["kvn"])], [_out(Q_LORA, BF16), _out(KV_LORA, BF16)],
                                     [_acc(1, Q_LORA), _acc(1, KV_LORA)], "mla_pre_bwd")
        GS["mla_q_norm"][l], GS["mla_kv_norm"][l] = dqn[0], dkvn[0]
        dzin = jnp.concatenate([dcq, dckv, dkr, dscb, dscc, dsch, dz, dxraw, ddtraw.astype(BF16), jnp.zeros((T, ZIN - Z_DT - LANE), BF16)], axis=1)
        dh1 = _mm(dzin, lw["w_in"], "nn", F32, "mm_in_dx")
        GW["w_in"][l] = _unpad_rows_in(_mm(dzin, s["h1"], "tn", BF16, "mm_in_dw"))
        gx, dgp = _rows(_vjp_wrap(_f_premix, 1, 1, add_first=True), T, tm, [_cur(s["x"]), _cur(dh1), _cur(gx0)], [_cst(s["g_pre"])],
                        [_out(D_MODEL, F32)], [_acc(1, D_MODEL)], "pre_mix_bwd")
        GS["norm_mix_pre"][l] = dgp[0]
        if ex is not None:
            pending = ex.submit({n: GW[n][l] for ns in EARLY for n in ns}, "early")
    if ex is not None:
        ex.collect(0, "early", _rs_chip_exchange(pending))
    GS = {k: jnp.stack(v) for k, v in GS.items()}
    return loss_part[0, 0], gx, GW, GS


WEIGHTS = ("norm_mix_pre", "norm_mix_post", "norm_ffn_pre", "norm_ffn_post", "w_in", "mla_q_norm", "mla_w_q_up", "mla_kv_norm",
           "mla_w_kv_up", "sc_conv_w", "ssd_conv_w", "ssd_conv_b", "ssd_dt_bias", "ssd_a_log", "ssd_d", "ssd_norm", "w_out",
           "ffn_w_up", "ffn_conv_w", "ffn_conv_b", "ffn_w_down")
SHARDED = (("w_in", 2), ("mla_w_q_up", 2), ("mla_w_kv_up", 2), ("sc_conv_w", 2), ("ssd_conv_w", 2), ("w_out", 1),
           ("ffn_w_up", 2), ("ffn_conv_w", 2), ("ffn_w_down", 1))
SMALL = tuple(n for n in WEIGHTS if n not in dict(SHARDED))
N_CHIPS = 4
N_DEV = 8
ROW_ALIGN = 64
SLAB_ALIGN = 16
EARLY = (("w_in", "mla_w_q_up", "mla_w_kv_up", "sc_conv_w", "ssd_conv_w"),)
LATE = (("ffn_w_down", "w_out"), ("ffn_w_up", "ffn_conv_w"))
TRANSPOSED = ("w_in",)


def _is_rows(shape, width):
    return shape[-1] == width and math.prod(shape[:-1]) % SLAB_ALIGN == 0


def _is_short(shape, width):
    return len(shape) == 2 and shape[1] == width and not _is_rows(shape, width)


def _slab_rows(shape, width):
    if _is_rows(shape, width):
        return math.prod(shape[:-1])
    if _is_short(shape, width):
        return -(-shape[0] // SLAB_ALIGN) * SLAB_ALIGN
    return -(-math.prod(shape) // (width * SLAB_ALIGN)) * SLAB_ALIGN


def _slab(piece, width, dtype, lead=0):
    ld, shape = piece.shape[:lead], piece.shape[lead:]
    rows = _slab_rows(shape, width)
    if _is_rows(shape, width):
        return piece.astype(dtype).reshape(ld + (rows, width))
    if _is_short(shape, width):
        return jnp.pad(piece.astype(dtype), [(0, 0)] * lead + [(0, rows - shape[0]), (0, 0)])
    flat = piece.astype(dtype).reshape(ld + (-1,))
    return jnp.pad(flat, [(0, 0)] * lead + [(0, rows * width - flat.shape[-1])]).reshape(ld + (rows, width))


def _unslab(slab, shape, lead=0):
    ld = slab.shape[:lead]
    if _is_rows(shape, slab.shape[-1]):
        return slab.reshape(ld + tuple(shape))
    if _is_short(shape, slab.shape[-1]):
        return slab[..., :shape[0], :]
    return slab.reshape(ld + (-1,))[..., :math.prod(shape)].reshape(ld + tuple(shape))


def _layout(shapes, names, width):
    ents, off = [], 0
    for n in names:
        shp = tuple(shapes[n])
        todo = [(None, False, shp), (None, True, shp)] if n.endswith("conv_w") else [(l, False, shp[1:]) for l in range(shp[0])]
        for l, lo, ps in todo:
            r = _slab_rows(ps, width)
            ents.append((n, l, lo, ps, off, r))
            off += r
    return width, -(-off // ROW_ALIGN) * ROW_ALIGN, ents


def _pack(layout, piece, dtype, lead=0):
    width, rows, ents = layout
    slabs, ld = [], None
    for n, l, lo, ps, off, r in ents:
        p = piece(n, l, lo)
        slabs.append(None if p is None else _slab(p, width, dtype, lead))
        ld = ld if p is None else p.shape[:lead]
    used = ents[-1][4] + ents[-1][5]
    slabs = [jnp.zeros(ld + (e[5], width), dtype) if s is None else s for s, e in zip(slabs, ents)]
    if rows > used:
        slabs.append(jnp.zeros(ld + (rows - used, width), dtype))
    return jnp.concatenate(slabs, axis=lead)


ANY = pl.BlockSpec(memory_space=pl.ANY)


def _pos():
    return lax.axis_index("x"), lax.axis_index("y"), lax.axis_index("c")


def _other_chips(x, y):
    return ((1 - x, y), (x, 1 - y), (1 - x, 1 - y))


def _remote(src, dst, ssem, rsem, dev):
    return pltpu.make_async_remote_copy(src_ref=src, dst_ref=dst, send_sem=ssem, recv_sem=rsem, device_id=dev, device_id_type=MESH)


AG_CHUNKS = 2


def _chip_index():
    return 2 * lax.axis_index("x") + lax.axis_index("y")


def _ag_sems(nbuf):
    return [pltpu.SemaphoreType.DMA((nbuf * 3 * AG_CHUNKS,))] * 4


def _ag_plan(w_refs, out_refs, sems):
    isend, irecv, dsend, drecv = sems
    x, y, c = _pos()
    k = 2 * x + y
    sib = (x, y, 1 - c)
    sends, lands, forwards, finals = [], [], [], []
    s = 0
    for w_ref, out_ref in zip(w_refs, out_refs):
        H = w_ref.shape[0] // 2
        CH = H // AG_CHUNKS
        for cx, cy in _other_chips(x, y):
            for ch in range(AG_CHUNKS):
                mine = out_ref.at[k, pl.ds(c * H + ch * CH, CH), :]
                near = out_ref.at[2 * cx + cy, pl.ds(c * H + ch * CH, CH), :]
                far = out_ref.at[2 * cx + cy, pl.ds((1 - c) * H + ch * CH, CH), :]
                sends.append(_remote(w_ref.at[pl.ds(c * H + ch * CH, CH), :], mine, isend.at[s], irecv.at[s], (cx, cy, c)))
                lands.append(_remote(near, near, isend.at[s], irecv.at[s], (cx, cy, c)))
                forwards.append(_remote(near, near, dsend.at[s], drecv.at[s], sib))
                finals.append(_remote(far, far, dsend.at[s], drecv.at[s], sib))
                s += 1
    return sends, lands, forwards, finals


def _own_slot(got, own):
    return lax.dynamic_update_slice(got, own[None], (_chip_index(), 0, 0))


def _all_gather_weights(ws):
    nb = len(ws)

    def body(*refs):
        sends, lands, forwards, finals = _ag_plan(refs[:nb], refs[nb:2 * nb], refs[2 * nb:])
        for cp in sends:
            cp.start()
        for land, fw in zip(lands, forwards):
            land.wait_recv()
            fw.start()
        for cp in finals:
            cp.wait_recv()
        for cp in sends + forwards:
            cp.wait_send()

    got = pl.pallas_call(
        body, name="all_gather_weights", in_specs=[ANY] * nb, out_specs=[ANY] * nb,
        out_shape=[jax.ShapeDtypeStruct((N_CHIPS,) + w.shape, w.dtype) for w in ws], scratch_shapes=_ag_sems(nb),
    )(*ws)
    return [_own_slot(g, w) for g, w in zip(got, ws)]


def _rs_pair_exchange(gs):
    nb = len(gs)

    def body(*refs):
        g_refs, got_refs, (ssem, rsem) = refs[:nb], refs[nb:2 * nb], refs[2 * nb:]
        x, y, c = _pos()
        cps = []
        for b, (g_ref, got_ref) in enumerate(zip(g_refs, got_refs)):
            H = g_ref.shape[1] // 2
            for kk in range(N_CHIPS):
                s = b * N_CHIPS + kk
                cps.append(_remote(g_ref.at[kk, pl.ds((1 - c) * H, H), :], got_ref.at[kk], ssem.at[s], rsem.at[s], (x, y, 1 - c)))
        for cp in cps:
            cp.start()
        for cp in cps:
            cp.wait()

    return pl.pallas_call(
        body, name="rs_pair_exchange", in_specs=[ANY] * nb, out_specs=[ANY] * nb,
        out_shape=[jax.ShapeDtypeStruct((N_CHIPS, g.shape[1] // 2, g.shape[2]), g.dtype) for g in gs],
        scratch_shapes=[pltpu.SemaphoreType.DMA((nb * N_CHIPS,))] * 2,
    )(*gs)


def _chip_sems(nbuf):
    return [pltpu.SemaphoreType.DMA((nbuf * 3,))] * 2


def _chip_plan(p_refs, out_refs, sems):
    ssem, rsem = sems
    x, y, c = _pos()
    sends, lands = [], []
    s = 0
    for p_ref, out_ref in zip(p_refs, out_refs):
        for cx, cy in _other_chips(x, y):
            sends.append(_remote(p_ref.at[2 * cx + cy], out_ref.at[2 * x + y], ssem.at[s], rsem.at[s], (cx, cy, c)))
            land = out_ref.at[2 * cx + cy]
            lands.append(_remote(land, land, ssem.at[s], rsem.at[s], (cx, cy, c)))
            s += 1
    return sends, lands


def _chip_parts(got, ps):
    k = _chip_index()
    return [lax.dynamic_update_slice(g, lax.dynamic_slice_in_dim(p, k, 1, axis=0), (k, 0, 0)) for g, p in zip(got, ps)]


def _rs_chip_exchange(ps):
    nb = len(ps)

    def body(*refs):
        sends, lands = _chip_plan(refs[:nb], refs[nb:2 * nb], refs[2 * nb:])
        for cp in sends:
            cp.start()
        for cp in lands:
            cp.wait_recv()
        for cp in sends:
            cp.wait_send()

    got = pl.pallas_call(
        body, name="rs_chip_exchange", in_specs=[ANY] * nb, out_specs=[ANY] * nb,
        out_shape=[jax.ShapeDtypeStruct(p.shape, p.dtype) for p in ps], scratch_shapes=_chip_sems(nb),
    )(*ps)
    return _chip_parts(got, ps)


def _rs_pair_share(fs):
    nb = len(fs)

    def body(*refs):
        f_refs, out_refs, (ssem, rsem) = refs[:nb], refs[nb:2 * nb], refs[2 * nb:]
        x, y, c = _pos()
        sends, lands = [], []
        for b, (f_ref, out_ref) in enumerate(zip(f_refs, out_refs)):
            sends.append(_remote(f_ref, out_ref.at[c], ssem.at[b], rsem.at[b], (x, y, 1 - c)))
            land = out_ref.at[1 - c]
            lands.append(_remote(land, land, ssem.at[b], rsem.at[b], (x, y, 1 - c)))
        for cp in sends:
            cp.start()
        for cp in lands:
            cp.wait_recv()
        for cp in sends:
            cp.wait_send()

    got = pl.pallas_call(
        body, name="rs_pair_share", in_specs=[ANY] * nb, out_specs=[ANY] * nb,
        out_shape=[jax.ShapeDtypeStruct((2,) + f.shape, f.dtype) for f in fs],
        scratch_shapes=[pltpu.SemaphoreType.DMA((nb,))] * 2,
    )(*fs)
    return [lax.dynamic_update_slice(g, f[None], (lax.axis_index("c"), 0, 0)) for g, f in zip(got, fs)]


def _all_reduce_small(s):
    r, C = s.shape

    def body(s_ref, o_ref, buf, ssem, rsem):
        x, y, c = _pos()
        me = 4 * x + 2 * y + c
        buf[me] = s_ref[...]
        cps = []
        for m in range(1, N_DEV):
            mx, my, mc = (m >> 2) & 1, (m >> 1) & 1, m & 1
            peer = (x ^ mx, y ^ my, c ^ mc)
            cp = _remote(s_ref, buf.at[me], ssem.at[m - 1], rsem.at[m - 1], peer)
            cp.start()
            cps.append(cp)
        for m in range(1, N_DEV):
            mx, my, mc = (m >> 2) & 1, (m >> 1) & 1, m & 1
            src = 4 * (x ^ mx) + 2 * (y ^ my) + (c ^ mc)
            _remote(s_ref, buf.at[src], ssem.at[m - 1], rsem.at[m - 1], (x ^ mx, y ^ my, c ^ mc)).wait_recv()
        for cp in cps:
            cp.wait_send()
        acc = buf[0]
        for j in range(1, N_DEV):
            acc = acc + buf[j]
        o_ref[...] = acc

    return pl.pallas_call(
        body, name="all_reduce_small", in_specs=[pl.BlockSpec(memory_space=pltpu.VMEM)],
        out_specs=pl.BlockSpec(memory_space=pltpu.VMEM), out_shape=jax.ShapeDtypeStruct((r, C), F32),
        scratch_shapes=[pltpu.VMEM((N_DEV, r, C), F32), pltpu.SemaphoreType.DMA((N_DEV - 1,)), pltpu.SemaphoreType.DMA((N_DEV - 1,))],
    )(s)


def _rtile(n, pref):
    if n <= pref:
        return n
    t = (pref // 16) * 16
    while t >= 16:
        if n % t == 0:
            return t
        t -= 16
    raise ValueError(f"no row tile for {n}")


def _rs_pair_sums(gpks):
    gots = _rs_pair_exchange(gpks)
    out = []
    for gpk, got in zip(gpks, gots):
        _, R, C = gpk.shape
        H = R // 2
        own = lax.dynamic_index_in_dim(gpk.reshape(N_CHIPS, 2, H, C), lax.axis_index("c"), axis=1, keepdims=False)
        (part,) = _rows(lambda i, n, a, b: (a.astype(F32) + b.astype(F32),), N_CHIPS * H, _rtile(N_CHIPS * H, 512),
                        [_cur(own.reshape(N_CHIPS * H, C)), _cur(got.reshape(N_CHIPS * H, C))], [], [_out(C, BF16)], [], "rs_pair_add")
        out.append(part.reshape(N_CHIPS, H, C))
    return out


def _rs_chip_sums(parts):
    def add4(i, n, a, b, c, d):
        return (((a.astype(F32) + b.astype(F32)) + c.astype(F32)) + d.astype(F32),)

    out = []
    for p in parts:
        _, H, C = p.shape
        tm = _rtile(H, 1024)
        (red,) = _rows(add4, H, tm, [(p.reshape(N_CHIPS * H, C), C, functools.partial(_const, v=0), j * (H // tm)) for j in range(N_CHIPS)],
                       [], [_out(C, F32)], [], "rs_chip_add")
        out.append(red)
    return out


class _Exchange:
    def __init__(self, a):
        self.a = a
        self.axis = {n: (1 if n in TRANSPOSED else ax) for n, ax in SHARDED}
        shapes = {n: (1,) + tuple(self.packed(n, a[n]).shape[1:]) for n in self.axis}
        widths = lambda names: shapes[names[0]][-1] if names[0] == "ffn_w_up" else PACK_COLS
        self.layouts = {"early": [_layout(shapes, ns, widths(ns)) for ns in EARLY], "late": [_layout(shapes, ns, widths(ns)) for ns in LATE]}
        self.reduced = {}

    @staticmethod
    def packed(n, w):
        return jnp.swapaxes(w, -1, -2) if n in TRANSPOSED else w

    def shard(self, l, group):
        def piece(n, li, lo):
            w = self.packed(n, self.a[n][l:l + 1] if li is None else self.a[n][l])
            return w - w.astype(BF16).astype(F32) if lo else w
        return [_pack(lay, piece, BF16) for lay in self.layouts[group]]

    def weights(self, gathered, group):
        W, resid = {}, {}
        for (width, rows, ents), g in zip(self.layouts[group], gathered):
            for n, li, lo, ps, off, r in ents:
                parts = _unslab(g[:, off:off + r], ps, lead=1)
                ax = self.axis[n] + (1 if li is None else 0)
                full = jnp.moveaxis(parts, 0, ax - 1)
                full = full.reshape(full.shape[:ax - 1] + (-1,) + full.shape[ax + 1:])
                (resid if lo else W)[n] = full[0] if li is None else full
        for n in resid:
            W[n] = W[n].astype(F32) + resid[n].astype(F32)
        return W

    def submit(self, GW, group):
        def by_chip(g, ax, parts=N_CHIPS):
            g = g.reshape(g.shape[:ax] + (parts, g.shape[ax] // parts) + g.shape[ax + 1:])
            return jnp.moveaxis(g, ax, 0)

        def piece(n, li, lo):
            if lo:
                return None
            g = GW[n]
            if isinstance(g, tuple):
                return jnp.concatenate([by_chip(h, self.axis[n] - 1, N_CHIPS // 2) for h in g])
            return by_chip(g[None], self.axis[n]) if li is None else by_chip(g, self.axis[n] - 1)

        return _rs_pair_sums([_pack(lay, piece, BF16, lead=1) for lay in self.layouts[group]])

    def collect(self, l, group, parts):
        self.reduced[l, group] = _rs_chip_sums(parts)

    def finish(self):
        keys = [(l, g) for l in range(DEPTH) for g in self.layouts]
        flat = _rs_pair_share([f for key in keys for f in self.reduced[key]])
        both, at = {}, 0
        for key in keys:
            both[key] = flat[at:at + len(self.layouts[key[1]])]
            at += len(self.layouts[key[1]])
        grads = {}
        for group, lays in self.layouts.items():
            for b, (width, rows, ents) in enumerate(lays):
                for n, li, lo, ps, off, r in ents:
                    if not lo:
                        per_layer = [self.packed(n, _unslab(both[l, group][b].reshape(rows, width)[off:off + r], ps)) for l in range(DEPTH)]
                        grads[n] = jnp.concatenate(per_layer) if li is None else jnp.stack(per_layer)
        return grads


def _adam(w, g, m, v, name, g_row=0):
    shp = w.shape
    two = lambda a: a.reshape(-1, shp[-1])
    rows = math.prod(shp[:-1])
    tm = _rtile(rows, 256)
    assert g_row % tm == 0
    g_in = (two(g), shp[-1], functools.partial(_const, v=0), g_row // tm)
    res = _rows(_k_adam, rows, tm, [_cur(two(w)), g_in, _cur(two(m)), _cur(two(v))], [], [_out(shp[-1], F32)] * 4, [], name)
    return tuple(r.reshape(shp) for r in res)


def _pack_flat(parts, rows):
    flat = jnp.concatenate([p.astype(F32).reshape(-1) for p in parts])
    return jnp.pad(flat, (0, rows * PACK_COLS - flat.shape[0])).reshape(rows, PACK_COLS)


def _unpack_flat(buf, shapes):
    flat, out, off = buf.reshape(-1), [], 0
    for shp in shapes:
        n = math.prod(shp)
        out.append(flat[off:off + n].reshape(shp))
        off += n
    return out


def kernel(x, positions, norm_mix_pre, norm_mix_post, norm_ffn_pre, norm_ffn_post, w_in, mla_q_norm, mla_w_q_up, mla_kv_norm, mla_w_kv_up, sc_conv_w, ssd_conv_w, ssd_conv_b, ssd_dt_bias, ssd_a_log, ssd_d, ssd_norm, w_out, ffn_w_up, ffn_conv_w, ffn_conv_b, ffn_w_down, loss_target, m_norm_mix_pre, m_norm_mix_post, m_norm_ffn_pre, m_norm_ffn_post, m_w_in, m_mla_q_norm, m_mla_w_q_up, m_mla_kv_norm, m_mla_w_kv_up, m_sc_conv_w, m_ssd_conv_w, m_ssd_conv_b, m_ssd_dt_bias, m_ssd_a_log, m_ssd_d, m_ssd_norm, m_w_out, m_ffn_w_up, m_ffn_conv_w, m_ffn_conv_b, m_ffn_w_down, v_norm_mix_pre, v_norm_mix_post, v_norm_ffn_pre, v_norm_ffn_post, v_w_in, v_mla_q_norm, v_mla_w_q_up, v_mla_kv_norm, v_mla_w_kv_up, v_sc_conv_w, v_ssd_conv_w, v_ssd_conv_b, v_ssd_dt_bias, v_ssd_a_log, v_ssd_d, v_ssd_norm, v_w_out, v_ffn_w_up, v_ffn_conv_w, v_ffn_conv_b, v_ffn_w_down):
    a = dict(locals())
    ex = _Exchange(a)
    S = {n: a[n] for n in SMALL}
    loss_part, gx, _, GS = _local_step(a["x"][0], a["positions"][0], a["loss_target"][0], None, S, ex)

    grads, delta, new_m, new_v = {}, {}, {}, {}
    for n, g in ex.finish().items():
        grads[n], delta[n], new_m[n], new_v[n] = _adam(a[n], g, a["m_" + n], a["v_" + n], "adamw_" + n)

    small_shapes = [a[n].shape for n in SMALL]
    rs = -(-(sum(math.prod(s) for s in small_shapes) + 1) // (PACK_COLS * SLAB_ALIGN)) * SLAB_ALIGN
    red = _all_reduce_small(_pack_flat([GS[n] for n in SMALL] + [loss_part.reshape(1)], rs))
    loss = _unpack_flat(red, small_shapes + [(1,)])[-1][0]
    pk = lambda pre: _pack_flat([a[pre + n] for n in SMALL], rs)
    for dst, buf in zip((grads, delta, new_m, new_v), _adam(pk(""), red, pk("m_"), pk("v_"), "adamw_small")):
        dst.update(zip(SMALL, _unpack_flat(buf, small_shapes)))

    return (loss, gx[None], *[grads[n] for n in WEIGHTS], *[delta[n] for n in WEIGHTS], *[new_m[n] for n in WEIGHTS],
            *[new_v[n] for n in WEIGHTS])
```

```python
import functools
import math

import jax
import jax.numpy as jnp
from jax import lax
from jax.experimental import pallas as pl
from jax.experimental.pallas import tpu as pltpu

F32 = jnp.float32
BF16 = jnp.bfloat16
MXU_DTYPE = jnp.bfloat16
HIGHEST = lax.Precision.HIGHEST
MESH = pl.DeviceIdType.MESH

D_MODEL = 1024
DEPTH = 4
HEADS = 8
Q_LORA = 256
KV_LORA = 128
NOPE = 64
ROPE = 32
VDIM = 64
ROPE_THETA = 10000.0
SC_DIM = 256
SSD_HEADS = 4
SSD_HEAD_DIM = 64
SSD_STATE = 128
SSD_DIM = 256
SSD_CONV_DIM = 768
SSD_CHUNK = 128
FFN_DIM = 2816
NORM_EPS = 1e-6
QK_SCALE = (NOPE + ROPE) ** -0.5
LANE = 128
HP = 128
FLASH_HEADS = 4
FLASH_HEADS_FWD = 8

ZIN = 2560
Z_CQ, Z_CKV, Z_KR, Z_SCB, Z_SCC, Z_SCH, Z_SSZ, Z_XBC, Z_DT = 0, 256, 384, 512, 768, 1024, 1280, 1536, 2304
KR_LANE = 64
YCAT = HEADS * HP + SC_DIM + SSD_DIM
FFN_TILE = 256
FFN_ROWS = 2048
ROW_BLOCK = 512

ADAM_LR, ADAM_B1, ADAM_B2, ADAM_EPS, ADAM_WD, ADAM_STEP = 0.001, 0.9, 0.999, 1e-08, 0.01, 10

PACK_COLS = 1024


def _tile(n, pref):
    if n <= pref:
        return n
    t = (pref // LANE) * LANE
    while t >= LANE:
        if n % t == 0:
            return t
        t -= LANE
    raise ValueError(f"no tile for {n}")


MM_TM, MM_TN, MM_TK = 1024, 1408, 1536


def _mm(a, b, mode, out_dtype, name, tm=None, tn=MM_TN, tkmax=MM_TK):
    pair = isinstance(a, tuple)
    a_list = list(a) if pair else [a]
    layer = None
    if isinstance(b, tuple):
        b, layer = b
    bshape = b.shape[-2:]
    if mode == "nn":
        (M, Ka), (_, N) = a_list[0].shape, bshape
    elif mode == "nt":
        (M, Ka), (N, _) = a_list[0].shape, bshape
    else:
        (Ka, M), (_, N) = a_list[0].shape, bshape
    tm = (MM_TN if mode == "tn" else MM_TM) if tm is None else tm
    tm, tn, tk = _tile(M, tm), _tile(N, tn), _tile(Ka, tkmax)
    nka = Ka // tk
    nk = nka * len(a_list)

    def bspec(shape, index):
        if layer is None:
            return pl.BlockSpec(shape, index)
        return pl.BlockSpec((None,) + shape, lambda i, j, k: (layer,) + index(i, j, k))

    if mode == "nn":
        a_specs = [pl.BlockSpec((tm, tk), lambda i, j, k: (i, jnp.minimum(k, nka - 1))),
                   pl.BlockSpec((tm, tk), lambda i, j, k: (i, jnp.maximum(k - nka, 0)))][:len(a_list)]
        b_spec = bspec((tk, tn), lambda i, j, k: (k, j))
        dims = NN
    elif mode == "nt":
        a_specs = [pl.BlockSpec((tm, tk), lambda i, j, k: (i, jnp.minimum(k, nka - 1))),
                   pl.BlockSpec((tm, tk), lambda i, j, k: (i, jnp.maximum(k - nka, 0)))][:len(a_list)]
        b_spec = bspec((tn, tk), lambda i, j, k: (j, k))
        dims = NT
    else:
        a_specs = [pl.BlockSpec((tk, tm), lambda i, j, k: (k, i))]
        b_spec = pl.BlockSpec((tk, tn), lambda i, j, k: (k, j))
        dims = TN
    na = len(a_list)

    def body(*refs):
        a_refs, b_ref, o_ref = refs[:na], refs[na], refs[na + 1]
        k = pl.program_id(2)

        def prod(a_ref):
            return lax.dot_general(a_ref[...].astype(MXU_DTYPE), b_ref[...].astype(MXU_DTYPE), dims, preferred_element_type=F32)

        if nk == 1:
            o_ref[...] = prod(a_refs[0]).astype(o_ref.dtype)
            return
        acc_ref = refs[na + 2]

        @pl.when(k == 0)
        def _():
            acc_ref[...] = prod(a_refs[0])

        @pl.when((k > 0) & (k < nka))
        def _():
            acc_ref[...] += prod(a_refs[0])

        if pair:
            @pl.when(k >= nka)
            def _():
                acc_ref[...] += prod(a_refs[1])

        @pl.when(k == nk - 1)
        def _():
            o_ref[...] = acc_ref[...].astype(o_ref.dtype)

    return pl.pallas_call(
        body, name=name, grid=(M // tm, N // tn, nk),
        in_specs=a_specs + [b_spec], out_specs=pl.BlockSpec((tm, tn), lambda i, j, k: (i, j)),
        out_shape=jax.ShapeDtypeStruct((M, N), out_dtype),
        scratch_shapes=[pltpu.VMEM((tm, tn), F32)] if nk > 1 else [],
        compiler_params=pltpu.CompilerParams(dimension_semantics=("parallel", "parallel", "arbitrary")),
    )(*a_list, b)


HALO = 8


def _const(j, v):
    return v


def _rows(fn, T, tm, ins, consts, outs, accs, name, ncol=1):
    n = T // tm
    hb = tm // HALO
    last = T // HALO - 1
    in_specs, args = [], []
    for arr, bc, cb, kind in ins:
        if isinstance(kind, int):
            in_specs.append(pl.BlockSpec((tm, bc), lambda j, i, cb=cb, off=kind: (i + off, cb(j))))
        elif kind == "cur":
            in_specs.append(pl.BlockSpec((tm, bc), lambda j, i, cb=cb: (i, cb(j))))
        elif kind == "prev":
            in_specs.append(pl.BlockSpec((HALO, bc), lambda j, i, cb=cb: (jnp.maximum(i * hb - 1, 0), cb(j))))
        else:
            in_specs.append(pl.BlockSpec((HALO, bc), lambda j, i, cb=cb: (jnp.minimum((i + 1) * hb, last), cb(j))))
        args.append(arr)
    for arr, bc, cb in consts:
        in_specs.append(pl.BlockSpec((arr.shape[0], bc), lambda j, i, cb=cb: (0, cb(j))))
        args.append(arr)
    out_specs, out_shape = [], []
    for tc, dt, bc, cb in outs:
        out_specs.append(pl.BlockSpec((tm, bc), lambda j, i, cb=cb: (i, cb(j))))
        out_shape.append(jax.ShapeDtypeStruct((T, tc), dt))
    for r, tc, bc, cb in accs:
        out_specs.append(pl.BlockSpec((r, bc), lambda j, i, cb=cb: (0, cb(j))))
        out_shape.append(jax.ShapeDtypeStruct((r, tc), F32))
    nin, nout, nacc = len(args), len(outs), len(accs)

    def body(*refs):
        i = pl.program_id(1)
        res = fn(i, n, *[r[...] for r in refs[:nin]])
        for r, v in zip(refs[nin:nin + nout], res[:nout]):
            r[...] = v.astype(r.dtype)
        if nacc:
            acc_refs = refs[nin + nout:nin + nout + nacc]

            @pl.when(i == 0)
            def _():
                for r in acc_refs:
                    r[...] = jnp.zeros_like(r)

            for r, v in zip(acc_refs, res[nout:]):
                r[...] += v.astype(F32)

    res = pl.pallas_call(
        body, name=name, grid=(ncol, n), in_specs=in_specs, out_specs=out_specs, out_shape=out_shape,
        compiler_params=pltpu.CompilerParams(dimension_semantics=("arbitrary", "arbitrary")),
    )(*args)
    return res


def _cur(arr, bc=None, blk=0):
    bc = arr.shape[1] if bc is None else bc
    return (arr, bc, functools.partial(_const, v=blk), "cur")


def _halo(arr, kind, bc=None, blk=0):
    bc = arr.shape[1] if bc is None else bc
    return (arr, bc, functools.partial(_const, v=blk), kind)


def _cst(arr):
    return (arr, arr.shape[1], functools.partial(_const, v=0))


def _out(cols, dt):
    return (cols, dt, cols, functools.partial(_const, v=0))


def _acc(rows, cols):
    return (rows, cols, cols, functools.partial(_const, v=0))


def _rms(x, w):
    return x * lax.rsqrt(jnp.mean(x * x, axis=-1, keepdims=True) + NORM_EPS) * w


def _sigmoid(x):
    return 0.5 * jnp.tanh(0.5 * x) + 0.5


def _silu(x):
    return x * _sigmoid(x)


def _dsilu(x):
    s = _sigmoid(x)
    return s * (1.0 + x * (1.0 - s))


def _softplus(x):
    return jnp.maximum(x, 0.0) + jnp.log1p(jnp.exp(-jnp.abs(x)))


def _shift(a, k):
    return pltpu.roll(a, k % a.shape[0], 0)


def _lroll(a, k):
    return pltpu.roll(a, k % a.shape[1], 1)


def _vjp_wrap(f, nrow, nconst, add_first=False):
    def g(i, n, *vals):
        rows, consts, mid = vals[:nrow], vals[len(vals) - nconst:], vals[nrow:len(vals) - nconst]
        cots = mid[:-1] if add_first else mid
        outs, pull = jax.vjp(f, *rows, *consts)
        grads = list(pull(tuple(c.astype(o.dtype) for c, o in zip(cots, outs))))
        if add_first:
            grads[0] = grads[0] + mid[-1]
        return tuple(grads)
    return g


def _rows_vjp(f, T, tm, rows, consts, cots, out_dtypes, name):
    return _rows(_vjp_wrap(f, len(rows), len(consts)), T, tm, [_cur(r) for r in rows] + [_cur(c) for c in cots],
                 [_cst(c) for c in consts], [_out(r.shape[1], dt) for r, dt in zip(rows, out_dtypes)],
                 [_acc(1, c.shape[1]) for c in consts], name)


def _f_premix(x, g):
    return (_rms(x, g),)


def _f_mla_pre(cq, ckv, qn, kvn):
    return _rms(cq, qn), _rms(ckv, kvn)


def _f_ssd_gate(y, z, nw):
    return (_rms(y * _silu(z), nw),)


def _f_post_mix(x, mixed, gpost, gffn):
    x1 = x + _rms(mixed, gpost)
    return x1, _rms(x1, gffn)


def _f_post_ffn(x1, d, gpost):
    return (x1 + _rms(d, gpost),)


def _rope_fwd(v, cosf, sina, sinb):
    return v * cosf + _lroll(v, -16) * sina + _lroll(v, 16) * sinb


def _rope_bwd(g, cosf, sina, sinb):
    return g * cosf + _lroll(g * sina, 16) + _lroll(g * sinb, -16)


def _k_rope_fwd(i, n, qpad, kvpad, kr, cosf, sina, sinb):
    qs, ks = [], []
    krr = _rope_fwd(kr, cosf, sina, sinb)
    for h in range(HEADS):
        sl = slice(h * HP, (h + 1) * HP)
        qs.append(_rope_fwd(qpad[:, sl], cosf, sina, sinb))
        ks.append(kvpad[:, sl].astype(F32) + krr)
    return jnp.concatenate(qs, axis=1), jnp.concatenate(ks, axis=1)


def _k_rope_bwd(i, n, dq, dk, dv, cosf, sina, sinb):
    lane = lax.broadcasted_iota(jnp.int32, (1, HP), 1)
    rmask = ((lane >= KR_LANE) & (lane < KR_LANE + ROPE)).astype(F32)
    dqs, dks = [], []
    dkr = jnp.zeros((dq.shape[0], HP), F32)
    for h in range(HEADS):
        sl = slice(h * HP, (h + 1) * HP)
        dqs.append(_rope_bwd(dq[:, sl], cosf, sina, sinb))
        dkh = dk[:, sl]
        dkr = dkr + dkh * rmask
        dks.append(dkh * (1.0 - rmask))
    dkr = _rope_bwd(dkr, cosf, sina, sinb) * rmask
    return jnp.concatenate(dqs, axis=1), jnp.concatenate(dks + [dv], axis=1), dkr


def _k_sconv_fwd(i, n, b, c, h, cp, hp, w):
    m = b.shape[0]
    up = jnp.where(i > 0, cp * hp, 0.0)
    ue = jnp.concatenate([up, c * h], axis=0)
    conv = w[2:3] * ue + w[1:2] * _shift(ue, 1) + w[0:1] * _shift(ue, 2)
    return (b * conv[HALO:],)


def _k_sconv_bwd(i, n, b, c, h, dy, cp, hp, bn, dyn, w):
    m = b.shape[0]
    up = jnp.where(i > 0, cp * hp, 0.0)
    ue = jnp.concatenate([up, c * h], axis=0)
    u1, u2 = _shift(ue, 1), _shift(ue, 2)
    conv = (w[2:3] * ue + w[1:2] * u1 + w[0:1] * u2)[HALO:]
    dc_cur = dy * b
    dce = jnp.concatenate([dc_cur, jnp.where(i < n - 1, dyn * bn, 0.0)], axis=0)
    du = (w[2:3] * dce + w[1:2] * _shift(dce, -1) + w[0:1] * _shift(dce, -2))[:m]
    dw = jnp.concatenate([
        jnp.sum(dc_cur * u2[HALO:], axis=0, keepdims=True),
        jnp.sum(dc_cur * u1[HALO:], axis=0, keepdims=True),
        jnp.sum(dc_cur * ue[HALO:], axis=0, keepdims=True),
        jnp.zeros((HALO - 3, b.shape[1]), F32)], axis=0)
    return dy * conv, du * h, du * c, dw


def _conv4(ue, w):
    return w[3:4] * ue + w[2:3] * _shift(ue, 1) + w[1:2] * _shift(ue, 2) + w[0:1] * _shift(ue, 3)


def _k_ssdconv_fwd(i, n, u, up, w, bias):
    ue = jnp.concatenate([jnp.where(i > 0, up, 0.0), u], axis=0)
    return (_silu(_conv4(ue, w)[HALO:] + bias),)


def _k_ssdconv_bwd(i, n, u, dout, up, un, doutn, w, bias):
    m = u.shape[0]
    ue = jnp.concatenate([jnp.where(i > 0, up, 0.0), u, un], axis=0)
    u1, u2, u3 = _shift(ue, 1), _shift(ue, 2), _shift(ue, 3)
    pre = (w[3:4] * ue + w[2:3] * u1 + w[1:2] * u2 + w[0:1] * u3)[HALO:] + bias
    doe = jnp.concatenate([dout, jnp.where(i < n - 1, doutn, 0.0)], axis=0)
    dpre = doe * _dsilu(pre)
    du = (w[3:4] * dpre + w[2:3] * _shift(dpre, -1) + w[1:2] * _shift(dpre, -2) + w[0:1] * _shift(dpre, -3))[:m]
    dp = dpre[:m]
    cur = slice(HALO, HALO + m)
    dw = jnp.concatenate([
        jnp.sum(dp * u3[cur], axis=0, keepdims=True),
        jnp.sum(dp * u2[cur], axis=0, keepdims=True),
        jnp.sum(dp * u1[cur], axis=0, keepdims=True),
        jnp.sum(dp * ue[cur], axis=0, keepdims=True),
        jnp.zeros((HALO - 4, u.shape[1]), F32)], axis=0)
    db = jnp.sum(dp, axis=0, keepdims=True)
    return du, dw, db


def _conv3(ue, w):
    return w[2:3] * ue + w[1:2] * _shift(ue, 1) + w[0:1] * _shift(ue, 2)


def _k_ffnact_fwd(i, n, ug, uu, ugp, uup, wg, wu, bg, bu):
    gate = _conv3(jnp.concatenate([jnp.where(i > 0, ugp, 0.0), ug], axis=0), wg)[HALO:] + bg
    upv = _conv3(jnp.concatenate([jnp.where(i > 0, uup, 0.0), uu], axis=0), wu)[HALO:] + bu
    return (_silu(gate) * upv,)


def _k_ffnact_bwd(i, n, ug, uu, dact, ugp, uup, ugn, uun, dactn, wg, wu, bg, bu):
    m = ug.shape[0]
    cur = slice(HALO, HALO + m)

    def taps(p, c, nx):
        e = jnp.concatenate([jnp.where(i > 0, p, 0.0), c, nx], axis=0)
        return e, _shift(e, 1), _shift(e, 2)

    def back(d, w):
        return (w[2:3] * d + w[1:2] * _shift(d, -1) + w[0:1] * _shift(d, -2))[:m]

    def wgrad(d, t):
        return jnp.concatenate([jnp.sum(d[:m] * t[2][cur], axis=0, keepdims=True), jnp.sum(d[:m] * t[1][cur], axis=0, keepdims=True),
                                jnp.sum(d[:m] * t[0][cur], axis=0, keepdims=True), jnp.zeros((HALO - 3, d.shape[1]), F32)], axis=0)

    tg, tu = taps(ugp, ug, ugn), taps(uup, uu, uun)
    gate = (wg[2:3] * tg[0] + wg[1:2] * tg[1] + wg[0:1] * tg[2])[HALO:] + bg
    upv = (wu[2:3] * tu[0] + wu[1:2] * tu[1] + wu[0:1] * tu[2])[HALO:] + bu
    dae = jnp.concatenate([dact, jnp.where(i < n - 1, dactn, 0.0)], axis=0)
    sg = _sigmoid(gate)
    dg = dae * upv * (sg * (1.0 + gate * (1.0 - sg)))
    dup = dae * (gate * sg)
    return (back(dg, wg), back(dup, wu), wgrad(dg, tg), wgrad(dup, tu),
            jnp.sum(dg[:m], axis=0, keepdims=True), jnp.sum(dup[:m], axis=0, keepdims=True))


def _k_loss(i, n, y, tgt):
    e = y - tgt
    part = 0.5 * jnp.sum(jnp.sum(e * e, axis=1, keepdims=True) / D_MODEL, axis=0, keepdims=True)
    return e * (1.0 / D_MODEL), jnp.broadcast_to(part, (1, LANE))


def _k_adam(i, n, w, g, m, v):
    m = ADAM_B1 * m + (1.0 - ADAM_B1) * g
    v = ADAM_B2 * v + (1.0 - ADAM_B2) * (g * g)
    m_hat = m / (1.0 - ADAM_B1 ** ADAM_STEP)
    v_hat = v / (1.0 - ADAM_B2 ** ADAM_STEP)
    delta = -ADAM_LR * (m_hat / (jnp.sqrt(v_hat) + ADAM_EPS) + ADAM_WD * w)
    return g, delta, m, v


def _dotf(a, b, dims):
    return lax.dot_general(a.astype(MXU_DTYPE), b.astype(MXU_DTYPE), dims, preferred_element_type=F32)


NN = (((1,), (0,)), ((), ()))
NT = (((1,), (1,)), ((), ()))
TN = (((0,), (0,)), ((), ()))


def _ssd_chunk(x0, x1, x2, x3, b0, b1, c0, c1, dtraw, p0, p1, p2, p3, dtb, alog, dsk):
    xs, bs, cs_, ps = (x0, x1, x2, x3), (b0, b1), (c0, c1), (p0, p1, p2, p3)
    L = dtraw.shape[0]
    dt = _softplus(dtraw + dtb)
    adt = dt * (-jnp.exp(alog))
    row = lax.broadcasted_iota(jnp.int32, (L, L), 0)
    col = lax.broadcasted_iota(jnp.int32, (L, L), 1)
    tril = row >= col
    cum = jnp.dot(tril.astype(F32), adt, precision=HIGHEST, preferred_element_type=F32)
    cum_t = cum.T
    lane = lax.broadcasted_iota(jnp.int32, (1, LANE), 1)
    sub = lax.broadcasted_iota(jnp.int32, (LANE, 1), 0)
    lastcol = (lax.broadcasted_iota(jnp.int32, (1, L), 1) == L - 1).astype(F32)
    ys, news = [], []
    for h in range(SSD_HEADS):
        g = h // (SSD_HEADS // 2)
        oh = (lane == h).astype(F32)
        dth = jnp.sum(dt * oh, axis=1, keepdims=True)
        csh = jnp.sum(cum * oh, axis=1, keepdims=True)
        csr = jnp.sum(cum_t * (sub == h).astype(F32), axis=0, keepdims=True)
        cl = jnp.sum(csr * lastcol, axis=1, keepdims=True)
        dskh = jnp.sum(dsk * oh, axis=1, keepdims=True)
        x, bm, cm, prev = xs[h], bs[g], cs_[g], ps[h]
        xdt = x * dth
        decay = jnp.exp(jnp.where(tril, csh - csr, -jnp.inf))
        scores = _dotf(cm, bm, NT) * decay
        y_diag = _dotf(scores, xdt, NN)
        bd = bm * jnp.exp(cl - csh)
        cst = _dotf(xdt, bd, TN)
        news.append(prev * jnp.exp(cl) + cst)
        y_off = _dotf(cm, prev, NT) * jnp.exp(csh)
        ys.append(y_diag + y_off + x * dskh)
    return (*ys, *news)


SSD_STEP = 2


def _ssd_operands(x_ref, dt_ref, par_ref, prev, rows):
    xs = [x_ref[rows, h * SSD_HEAD_DIM:(h + 1) * SSD_HEAD_DIM] for h in range(SSD_HEADS)]
    bs = [x_ref[rows, SSD_DIM + g * SSD_STATE:SSD_DIM + (g + 1) * SSD_STATE] for g in range(2)]
    cs_ = [x_ref[rows, SSD_DIM + 2 * SSD_STATE + g * SSD_STATE:SSD_DIM + 2 * SSD_STATE + (g + 1) * SSD_STATE] for g in range(2)]
    return (*xs, *bs, *cs_, dt_ref[rows, :], *prev, par_ref[0:1, :], par_ref[1:2, :], par_ref[2:3, :])


def _ssd_fwd(xbc, dtraw, par, T, dt_blk=0):
    L = SSD_CHUNK
    nc = T // L
    P = SSD_HEAD_DIM
    U = SSD_STEP if nc % SSD_STEP == 0 else 1

    def body(x_ref, dt_ref, par_ref, y_ref, st_ref, state):
        @pl.when(pl.program_id(0) == 0)
        def _():
            state[...] = jnp.zeros_like(state)

        for u in range(U):
            rows = slice(u * L, (u + 1) * L)
            st_ref[u] = state[...]
            prev = [state[h * P:(h + 1) * P, :] for h in range(SSD_HEADS)]
            res = _ssd_chunk(*_ssd_operands(x_ref, dt_ref, par_ref, prev, rows))
            for h in range(SSD_HEADS):
                y_ref[rows, h * P:(h + 1) * P] = res[h]
                state[h * P:(h + 1) * P, :] = res[SSD_HEADS + h]

    return pl.pallas_call(
        body, name="ssd_scan_fwd", grid=(nc // U,),
        in_specs=[pl.BlockSpec((U * L, SSD_CONV_DIM), lambda c: (c, 0)), pl.BlockSpec((U * L, LANE), lambda c: (c, dt_blk)),
                  pl.BlockSpec((8, LANE), lambda c: (0, 0))],
        out_specs=[pl.BlockSpec((U * L, SSD_DIM), lambda c: (c, 0)), pl.BlockSpec((U, SSD_DIM, SSD_STATE), lambda c: (c, 0, 0))],
        out_shape=[jax.ShapeDtypeStruct((T, SSD_DIM), F32), jax.ShapeDtypeStruct((nc, SSD_DIM, SSD_STATE), F32)],
        scratch_shapes=[pltpu.VMEM((SSD_DIM, SSD_STATE), F32)],
        compiler_params=pltpu.CompilerParams(dimension_semantics=("arbitrary",)),
    )(xbc, dtraw, par)


def _ssd_bwd(xbc, dtraw, par, states, dy, T, dt_blk=0):
    L = SSD_CHUNK
    nc = T // L
    P = SSD_HEAD_DIM
    U = SSD_STEP if nc % SSD_STEP == 0 else 1
    ns = nc // U

    def body(x_ref, dt_ref, par_ref, st_ref, dy_ref, dx_ref, ddt_ref, dpar_ref, dstate):
        @pl.when(pl.program_id(0) == 0)
        def _():
            dstate[...] = jnp.zeros_like(dstate)
            dpar_ref[...] = jnp.zeros_like(dpar_ref)

        for u in reversed(range(U)):
            rows = slice(u * L, (u + 1) * L)
            prev = [st_ref[u, h * P:(h + 1) * P, :] for h in range(SSD_HEADS)]
            prim = _ssd_operands(x_ref, dt_ref, par_ref, prev, rows)
            _, pull = jax.vjp(_ssd_chunk, *prim)
            cots = tuple(dy_ref[rows, h * P:(h + 1) * P] for h in range(SSD_HEADS)) + tuple(
                dstate[h * P:(h + 1) * P, :] for h in range(SSD_HEADS))
            g = pull(cots)
            for h in range(SSD_HEADS):
                dx_ref[rows, h * P:(h + 1) * P] = g[h]
                dstate[h * P:(h + 1) * P, :] = g[9 + h]
            for k in range(2):
                dx_ref[rows, SSD_DIM + k * SSD_STATE:SSD_DIM + (k + 1) * SSD_STATE] = g[4 + k]
                dx_ref[rows, SSD_DIM + 2 * SSD_STATE + k * SSD_STATE:SSD_DIM + 2 * SSD_STATE + (k + 1) * SSD_STATE] = g[6 + k]
            ddt_ref[rows, :] = g[8]
            for r in range(3):
                dpar_ref[r:r + 1, :] += g[13 + r]

    rev = lambda c: (ns - 1 - c, 0)
    return pl.pallas_call(
        body, name="ssd_scan_bwd", grid=(ns,),
        in_specs=[pl.BlockSpec((U * L, SSD_CONV_DIM), rev), pl.BlockSpec((U * L, LANE), lambda c: (ns - 1 - c, dt_blk)),
                  pl.BlockSpec((8, LANE), lambda c: (0, 0)),
                  pl.BlockSpec((U, SSD_DIM, SSD_STATE), lambda c: (ns - 1 - c, 0, 0)), pl.BlockSpec((U * L, SSD_DIM), rev)],
        out_specs=[pl.BlockSpec((U * L, SSD_CONV_DIM), rev), pl.BlockSpec((U * L, LANE), rev), pl.BlockSpec((8, LANE), lambda c: (0, 0))],
        out_shape=[jax.ShapeDtypeStruct((T, SSD_CONV_DIM), F32), jax.ShapeDtypeStruct((T, LANE), F32),
                   jax.ShapeDtypeStruct((8, LANE), F32)],
        scratch_shapes=[pltpu.VMEM((SSD_DIM, SSD_STATE), F32)],
        compiler_params=pltpu.CompilerParams(dimension_semantics=("arbitrary",)),
    )(xbc, dtraw, par, states, dy)


def _causal_pairs(nq, by_query):
    if by_query:
        pairs = [(i, j) for i in range(nq) for j in range(i + 1)]
    else:
        pairs = [(i, j) for j in range(nq) for i in range(j, nq)]
    return jnp.asarray([p[0] for p in pairs], jnp.int32), jnp.asarray([p[1] for p in pairs], jnp.int32)


def _flash_fwd(q, k, kv, T, carry=()):
    tq = tk = min(512, T)
    nq = T // tq
    G = FLASH_HEADS_FWD
    rep = tk // HP
    nc = len(carry)
    qi, kj = _causal_pairs(nq, by_query=True)
    nh, nt = HEADS // G, qi.shape[0]

    def body(qi_ref, kj_ref, q_ref, k_ref, v_ref, *rest):
        w_refs, o_ref, g_refs = rest[:nc], rest[nc], rest[nc + 1:2 * nc + 1]
        m_ref, l_ref, acc_ref = rest[2 * nc + 1:2 * nc + 4]
        h, t = pl.program_id(0), pl.program_id(1)
        i, j = qi_ref[t], kj_ref[t]
        if nc:
            plan = lambda: _ag_plan(w_refs, g_refs, rest[2 * nc + 4:])

            @pl.when((h == 0) & (t == 0))
            def _():
                for cp in plan()[0]:
                    cp.start()

        @pl.when(j == 0)
        def _():
            m_ref[...] = jnp.full_like(m_ref, -jnp.inf)
            l_ref[...] = jnp.zeros_like(l_ref)
            acc_ref[...] = jnp.zeros_like(acc_ref)

        def step(diagonal):
            for g in range(G):
                sl = slice(g * HP, (g + 1) * HP)
                s = _dotf(q_ref[:, sl], k_ref[:, sl], NT) * QK_SCALE
                if diagonal:
                    rows = lax.broadcasted_iota(jnp.int32, (tq, tk), 0)
                    cols = lax.broadcasted_iota(jnp.int32, (tq, tk), 1)
                    s = jnp.where(rows >= cols, s, -jnp.inf)
                m_old = m_ref[:, sl]
                m_new = jnp.maximum(m_old, jnp.max(s, axis=1, keepdims=True))
                p = jnp.exp(s - jnp.tile(m_new, (1, rep)))
                alpha = jnp.exp(m_old - m_new)
                l_ref[:, sl] = alpha * l_ref[:, sl] + jnp.sum(p, axis=1, keepdims=True)
                acc_ref[:, sl] = alpha * acc_ref[:, sl] + _dotf(p, v_ref[:, sl], NN)
                m_ref[:, sl] = m_new

        @pl.when(j < i)
        def _():
            step(False)

        @pl.when(j == i)
        def _():
            step(True)
            lane = lax.broadcasted_iota(jnp.int32, (tq, HP), 1)
            for g in range(G):
                sl = slice(g * HP, (g + 1) * HP)
                l = l_ref[:, sl]
                o_ref[:, sl] = jnp.where(lane < VDIM, acc_ref[:, sl] / l, m_ref[:, sl] + jnp.log(l))

        if nc:
            @pl.when(h * nt + t == (3 * nh * nt) // 4)
            def _():
                _, lands, forwards, _ = plan()
                for land, fw in zip(lands, forwards):
                    land.wait_recv()
                    fw.start()

            @pl.when((h == nh - 1) & (t == nt - 1))
            def _():
                sends, _, forwards, finals = plan()
                for cp in finals:
                    cp.wait_recv()
                for cp in sends + forwards:
                    cp.wait_send()

    W = G * HP
    res = pl.pallas_call(
        body, name="mla_flash_fwd",
        grid_spec=pltpu.PrefetchScalarGridSpec(
            num_scalar_prefetch=2, grid=(nh, nt),
            in_specs=[pl.BlockSpec((tq, W), lambda h, t, qi, kj: (qi[t], h)),
                      pl.BlockSpec((tk, W), lambda h, t, qi, kj: (kj[t], h)),
                      pl.BlockSpec((tk, W), lambda h, t, qi, kj: (kj[t], HEADS // G + h))] + [ANY] * nc,
            out_specs=[pl.BlockSpec((tq, W), lambda h, t, qi, kj: (qi[t], h))] + [ANY] * nc,
            scratch_shapes=[pltpu.VMEM((tq, W), F32), pltpu.VMEM((tq, W), F32), pltpu.VMEM((tq, W), F32)] + (_ag_sems(nc) if nc else [])),
        out_shape=[jax.ShapeDtypeStruct((T, HEADS * HP), F32)] + [jax.ShapeDtypeStruct((N_CHIPS,) + w.shape, w.dtype) for w in carry],
        compiler_params=pltpu.CompilerParams(dimension_semantics=("arbitrary", "arbitrary")),
    )(qi, kj, q, k, kv, *carry)
    return res[0] if not nc else (res[0], [_own_slot(g, w) for g, w in zip(res[1:], carry)])


def _flash_bwd(q, k, kv, o, dycat, T, carry=()):
    tq = tk = min(512, T)
    nq = T // tq
    G = FLASH_HEADS
    nc = len(carry)
    qi, kj = _causal_pairs(nq, by_query=False)
    nh, nt = HEADS // G, qi.shape[0]

    def body(qi_ref, kj_ref, q_ref, k_ref, v_ref, o_ref, do_ref, *rest):
        p_refs, (dq_ref, dk_ref, dv_ref), part_refs = rest[:nc], rest[nc:nc + 3], rest[nc + 3:2 * nc + 3]
        h, t = pl.program_id(0), pl.program_id(1)
        i, j = qi_ref[t], kj_ref[t]
        if nc:
            plan = lambda: _chip_plan(p_refs, part_refs, rest[2 * nc + 3:])

            @pl.when((h == 0) & (t == 0))
            def _():
                for cp in plan()[0]:
                    cp.start()

        @pl.when(t == 0)
        def _():
            dq_ref[...] = jnp.zeros_like(dq_ref)

        @pl.when(i == j)
        def _():
            dk_ref[...] = jnp.zeros_like(dk_ref)
            dv_ref[...] = jnp.zeros_like(dv_ref)

        def step(diagonal):
            r0 = pl.multiple_of(i * tq, tq)
            for g in range(G):
                sl = slice(g * HP, (g + 1) * HP)
                qv, kv, vv, ov, dov = q_ref[:, sl], k_ref[:, sl], v_ref[:, sl], o_ref[:, sl], do_ref[:, sl]
                s = _dotf(qv, kv, NT) * QK_SCALE
                p = jnp.exp(s - ov[:, VDIM:VDIM + 1])
                if diagonal:
                    rows = lax.broadcasted_iota(jnp.int32, (tq, tk), 0)
                    cols = lax.broadcasted_iota(jnp.int32, (tq, tk), 1)
                    p = jnp.where(rows >= cols, p, 0.0)
                dsum = jnp.sum(dov * ov, axis=1, keepdims=True)
                dv_ref[:, sl] += _dotf(p, dov, TN)
                dp = _dotf(dov, vv, NT)
                ds = p * (dp - dsum) * QK_SCALE
                dk_ref[:, sl] += _dotf(ds, qv, TN)
                dq_ref[pl.ds(r0, tq), sl] += _dotf(ds, kv, NN)

        @pl.when(i > j)
        def _():
            step(False)

        @pl.when(i == j)
        def _():
            step(True)

        if nc:
            @pl.when((h == nh - 1) & (t == nt - 1))
            def _():
                sends, lands = plan()
                for cp in lands:
                    cp.wait_recv()
                for cp in sends:
                    cp.wait_send()

    W = G * HP
    qmap = lambda h, t, qi, kj: (qi[t], h)
    kmap = lambda h, t, qi, kj: (kj[t], h)
    vmap = lambda h, t, qi, kj: (kj[t], HEADS // G + h)
    res = pl.pallas_call(
        body, name="mla_flash_bwd",
        grid_spec=pltpu.PrefetchScalarGridSpec(
            num_scalar_prefetch=2, grid=(nh, nt),
            in_specs=[pl.BlockSpec((tq, W), qmap), pl.BlockSpec((tk, W), kmap), pl.BlockSpec((tk, W), vmap),
                      pl.BlockSpec((tq, W), qmap), pl.BlockSpec((tq, W), qmap)] + [ANY] * nc,
            out_specs=[pl.BlockSpec((T, W), lambda h, t, qi, kj: (0, h)), pl.BlockSpec((tk, W), kmap), pl.BlockSpec((tk, W), kmap)]
            + [ANY] * nc,
            scratch_shapes=_chip_sems(nc) if nc else []),
        out_shape=[jax.ShapeDtypeStruct((T, HEADS * HP), F32)] * 3 + [jax.ShapeDtypeStruct(p.shape, p.dtype) for p in carry],
        compiler_params=pltpu.CompilerParams(dimension_semantics=("arbitrary", "arbitrary")),
    )(qi, kj, q, k, kv, o, dycat, *carry)
    return tuple(res[:3]) if not nc else (*res[:3], _chip_parts(res[3:], carry))


_IN_SRC = (0, 256, 384, 416, 672, 928, 1184, 1440, 2208, 2212)
_IN_DST = (Z_CQ, Z_CKV, Z_KR + KR_LANE, Z_SCB, Z_SCC, Z_SCH, Z_SSZ, Z_XBC, Z_DT)


def _pad_rows_in(w):
    ax = w.ndim - 2

    def zeros(n):
        return jnp.zeros(w.shape[:ax] + (n,) + w.shape[ax + 1:], w.dtype)

    def whole_tiles(p):
        n = p.shape[ax]
        return p if n % SLAB_ALIGN == 0 else jnp.pad(p, [(0, 0)] * ax + [(0, -n % SLAB_ALIGN), (0, 0)])

    parts, at = [], 0
    for s0, s1, d0 in zip(_IN_SRC[:-1], _IN_SRC[1:], _IN_DST):
        if d0 > at:
            parts.append(zeros(d0 - at))
        parts.append(whole_tiles(lax.slice_in_dim(w, s0, s1, axis=ax)))
        at = d0 + parts[-1].shape[ax]
    parts.append(zeros(ZIN - at))
    return jnp.concatenate(parts, axis=ax)


def _unpad_rows_in(w):
    ax = w.ndim - 2
    groups = list(zip(_IN_SRC[:-1], _IN_SRC[1:], _IN_DST))
    parts = [lax.slice_in_dim(w, d0, d0 + -(-(s1 - s0) // SLAB_ALIGN) * SLAB_ALIGN, axis=ax) for s0, s1, d0 in groups]
    return lax.slice_in_dim(jnp.concatenate(parts, axis=ax), 0, _IN_SRC[-1], axis=ax)


def _pad_heads(w, width):
    w = w.reshape(w.shape[:-1] + (HEADS, width))
    w = jnp.pad(w, [(0, 0)] * (w.ndim - 1) + [(0, HP - width)])
    return w.reshape(w.shape[:-2] + (HEADS * HP,))


def _unpad_heads(w, width):
    w = w.reshape(w.shape[:-1] + (HEADS, HP))[..., :width]
    return w.reshape(w.shape[:-2] + (HEADS * width,))


def _pad_kv(w):
    w = w.reshape(w.shape[:-1] + (HEADS, NOPE + VDIM))
    return jnp.concatenate([_pad_heads(w[..., :NOPE].reshape(w.shape[:-2] + (HEADS * NOPE,)), NOPE),
                            _pad_heads(w[..., NOPE:].reshape(w.shape[:-2] + (HEADS * VDIM,)), VDIM)], axis=-1)


def _unpad_kv(w):
    k = _unpad_heads(w[..., :HEADS * HP], NOPE).reshape(w.shape[:-1] + (HEADS, NOPE))
    v = _unpad_heads(w[..., HEADS * HP:], VDIM).reshape(w.shape[:-1] + (HEADS, VDIM))
    return jnp.concatenate([k, v], axis=-1).reshape(w.shape[:-1] + (HEADS * (NOPE + VDIM),))


def _pad_out_rows(w):
    lead, d = w.shape[:-2], w.shape[-1]
    att = w[..., :HEADS * VDIM, :].reshape(lead + (HEADS, VDIM, d))
    att = jnp.pad(att, [(0, 0)] * (att.ndim - 2) + [(0, HP - VDIM), (0, 0)]).reshape(lead + (HEADS * HP, d))
    return jnp.concatenate([att, w[..., HEADS * VDIM:, :]], axis=-2)


def _unpad_out_rows(w):
    lead, d = w.shape[:-2], w.shape[-1]
    att = w[..., :HEADS * HP, :].reshape(lead + (HEADS, HP, d))[..., :VDIM, :].reshape(lead + (HEADS * VDIM, d))
    return jnp.concatenate([att, w[..., HEADS * HP:, :]], axis=-2)


def _rows8(w):
    return jnp.pad(w.astype(F32), [(0, 0)] * (w.ndim - 2) + [(0, 8 - w.shape[-2]), (0, 0)])


def _row8(*vecs):
    c = vecs[0].shape[-1]
    return jnp.concatenate([v.reshape(1, c).astype(F32) for v in vecs] + [jnp.zeros((8 - len(vecs), c), F32)], axis=0)


def _lanes(v):
    return jnp.pad(v.astype(F32), (0, LANE - v.shape[0])).reshape(1, LANE)


def _rope_tables(positions):
    inv_freq = 1.0 / (ROPE_THETA ** (jnp.arange(0, ROPE, 2, dtype=F32) / ROPE))
    ang = positions.astype(F32)[:, None] * inv_freq
    cos, sin = jnp.cos(ang), jnp.sin(ang)
    T = positions.shape[0]
    half = ROPE // 2
    one = jnp.ones((T, KR_LANE), F32)
    zero = jnp.zeros((T, KR_LANE), F32)
    tail1 = jnp.ones((T, HP - KR_LANE - ROPE), F32)
    tail0 = jnp.zeros((T, HP - KR_LANE - ROPE), F32)
    z16 = jnp.zeros((T, half), F32)
    cosf = jnp.concatenate([one, cos, cos, tail1], axis=1)
    sina = jnp.concatenate([zero, -sin, z16, tail0], axis=1)
    sinb = jnp.concatenate([zero, z16, sin, tail0], axis=1)
    return cosf, sina, sinb


def _kernel_weights(W):
    c = lambda a: a.astype(MXU_DTYPE)
    forms = dict(
        w_in=("w_in", lambda w: c(_pad_rows_in(w))),
        w_q=("mla_w_q_up", lambda w: c(_pad_heads(w, NOPE + ROPE))),
        w_kv=("mla_w_kv_up", lambda w: c(_pad_kv(w))),
        w_out=("w_out", lambda w: c(_pad_out_rows(w))),
        w_up=("ffn_w_up", c),
        w_down=("ffn_w_down", c),
        sc_w=("sc_conv_w", _rows8),
        ssd_w=("ssd_conv_w", _rows8),
        ffn_w=("ffn_conv_w", _rows8),
    )
    return {k: f(W[n]) for k, (n, f) in forms.items() if n in W}


def _layer_weights(KW, l):
    return {k: (v[l] if k in ("sc_w", "ssd_w", "ffn_w") else (v, l)) for k, v in KW.items()}


def _local_step(x, positions, target, W, S, ex=None):
    T = x.shape[0]
    tm = min(ROW_BLOCK, T)
    tm_ffn = min(FFN_ROWS, T)
    cosf, sina, sinb = _rope_tables(positions)
    if ex is None:
        KW = _kernel_weights(W)
    else:
        early = _all_gather_weights(ex.shard(0, "early"))
    saved = []
    xl = x
    for l in range(DEPTH):
        lw = _layer_weights(KW, l) if ex is None else _kernel_weights(ex.weights(early, "early"))
        g_pre = S["norm_mix_pre"][l].reshape(1, -1)
        g_post = S["norm_mix_post"][l].reshape(1, -1)
        g_fpre = S["norm_ffn_pre"][l].reshape(1, -1)
        g_fpost = S["norm_ffn_post"][l].reshape(1, -1)
        qn = S["mla_q_norm"][l].reshape(1, -1)
        kvn = S["mla_kv_norm"][l].reshape(1, -1)
        ssd_b = S["ssd_conv_b"][l].reshape(1, -1)
        ssd_par = _row8(jnp.pad(S["ssd_dt_bias"][l], (0, LANE - SSD_HEADS)), jnp.pad(S["ssd_a_log"][l], (0, LANE - SSD_HEADS)),
                        jnp.pad(S["ssd_d"][l], (0, LANE - SSD_HEADS)))
        ssd_nw = S["ssd_norm"][l].reshape(1, -1)
        ffn_b = S["ffn_conv_b"][l].reshape(1, -1)

        (h1,) = _rows(lambda i, n, *v: _f_premix(*v), T, tm, [_cur(xl)], [_cst(g_pre)], [_out(D_MODEL, BF16)], [], "pre_mix_norm")
        zin = _mm(h1, lw["w_in"], "nt", F32, "mm_in")
        qlat, kvlat = _rows(lambda i, n, *v: _f_mla_pre(*v), T, tm, [_cur(zin, Q_LORA, 0), _cur(zin, KV_LORA, Z_CKV // KV_LORA)],
                            [_cst(qn), _cst(kvn)], [_out(Q_LORA, BF16), _out(KV_LORA, BF16)], [], "mla_pre_norm")
        qpad = _mm(qlat, lw["w_q"], "nn", F32, "mm_q_up")
        kvpad = _mm(kvlat, lw["w_kv"], "nn", BF16, "mm_kv_up")
        qr, kr = _rows(_k_rope_fwd, T, tm, [_cur(qpad), _cur(kvpad, HEADS * HP, 0), _cur(zin, LANE, Z_KR // LANE),
                                            _cur(cosf), _cur(sina), _cur(sinb)], [],
                       [_out(HEADS * HP, BF16), _out(HEADS * HP, BF16)], [], "mla_rope")
        if ex is None:
            o = _flash_fwd(qr, kr, kvpad, T)
        else:
            nlate = len(ex.layouts["late"])
            o, got = _flash_fwd(qr, kr, kvpad, T, carry=ex.shard(l, "late") + (ex.shard(l + 1, "early") if l + 1 < DEPTH else []))
            lw.update(_kernel_weights(ex.weights(got[:nlate], "late")))
            early = got[nlate:]
        (yconv,) = _rows(_k_sconv_fwd, T, tm, [_cur(zin, SC_DIM, Z_SCB // SC_DIM), _cur(zin, SC_DIM, Z_SCC // SC_DIM),
                                               _cur(zin, SC_DIM, Z_SCH // SC_DIM), _halo(zin, "prev", SC_DIM, Z_SCC // SC_DIM),
                                               _halo(zin, "prev", SC_DIM, Z_SCH // SC_DIM)], [_cst(lw["sc_w"])],
                         [_out(SC_DIM, F32)], [], "short_conv_fwd")
        (xbc,) = _rows(_k_ssdconv_fwd, T, tm, [_cur(zin, SSD_CONV_DIM, Z_XBC // SSD_CONV_DIM),
                                               _halo(zin, "prev", SSD_CONV_DIM, Z_XBC // SSD_CONV_DIM)],
                       [_cst(lw["ssd_w"]), _cst(ssd_b)], [_out(SSD_CONV_DIM, F32)], [], "ssd_conv_fwd")
        yscan, states = _ssd_fwd(xbc, zin, ssd_par, T, Z_DT // LANE)
        (yssd,) = _rows(lambda i, n, *v: _f_ssd_gate(*v), T, tm, [_cur(yscan), _cur(zin, SSD_DIM, Z_SSZ // SSD_DIM)], [_cst(ssd_nw)],
                        [_out(SSD_DIM, F32)], [], "ssd_gate_fwd")
        ycat = jnp.concatenate([o.astype(BF16), yconv.astype(BF16), yssd.astype(BF16)], axis=1)
        mixed = _mm(ycat, lw["w_out"], "nn", F32, "mm_out")
        x1, h2 = _rows(lambda i, n, *v: _f_post_mix(*v), T, tm, [_cur(xl), _cur(mixed)], [_cst(g_post), _cst(g_fpre)],
                       [_out(D_MODEL, F32), _out(D_MODEL, BF16)], [], "post_mix_fwd")
        upre = _mm(h2, lw["w_up"], "nn", F32, "mm_up")
        nt = FFN_DIM // FFN_TILE
        gcol, ucol = (lambda j: j), (lambda j: j + nt)
        (act,) = _rows(_k_ffnact_fwd, T, tm_ffn,
                       [(upre, FFN_TILE, gcol, "cur"), (upre, FFN_TILE, ucol, "cur"), (upre, FFN_TILE, gcol, "prev"),
                        (upre, FFN_TILE, ucol, "prev")],
                       [(lw["ffn_w"], FFN_TILE, gcol), (lw["ffn_w"], FFN_TILE, ucol), (ffn_b, FFN_TILE, gcol), (ffn_b, FFN_TILE, ucol)],
                       [(FFN_DIM, BF16, FFN_TILE, gcol)], [], "ffn_act_fwd", ncol=nt)
        dn = _mm(act, lw["w_down"], "nn", F32, "mm_down")
        (x2,) = _rows(lambda i, n, *v: _f_post_ffn(*v), T, tm, [_cur(x1), _cur(dn)], [_cst(g_fpost)], [_out(D_MODEL, F32)], [], "post_ffn_fwd")
        saved.append(dict(lw=lw, x=xl, h1=h1, zin=zin, qlat=qlat, kvlat=kvlat, qr=qr, kr=kr, kvpad=kvpad, o=o, xbc=xbc,
                          yscan=yscan, states=states, ycat=ycat, mixed=mixed, x1=x1, h2=h2, upre=upre, act=act, dn=dn,
                          g_pre=g_pre, g_post=g_post, g_fpre=g_fpre, g_fpost=g_fpost, qn=qn, kvn=kvn, ssd_b=ssd_b,
                          ssd_par=ssd_par, ssd_nw=ssd_nw, ffn_b=ffn_b))
        xl = x2

    gx, loss_part = _rows(_k_loss, T, tm, [_cur(xl), _cur(target)], [], [_out(D_MODEL, F32)], [_acc(1, LANE)], "loss_head")

    GW = {k: [None] * DEPTH for k in ("w_in", "mla_w_q_up", "mla_w_kv_up", "sc_conv_w", "ssd_conv_w", "w_out", "ffn_w_up",
                                      "ffn_conv_w", "ffn_w_down")}
    GS = {k: [None] * DEPTH for k in ("norm_mix_pre", "norm_mix_post", "norm_ffn_pre", "norm_ffn_post", "mla_q_norm", "mla_kv_norm",
                                      "ssd_conv_b", "ssd_dt_bias", "ssd_a_log", "ssd_d", "ssd_norm", "ffn_conv_b")}
    nt = FFN_DIM // FFN_TILE
    gcol, ucol = (lambda j: j), (lambda j: j + nt)
    pending = None
    for l in reversed(range(DEPTH)):
        s = saved[l]
        lw = s["lw"]
        gx1, ddn, dgf = _rows_vjp(_f_post_ffn, T, tm, [s["x1"], s["dn"]], [s["g_fpost"]], [gx], [F32, BF16], "post_ffn_bwd")
        GS["norm_ffn_post"][l] = dgf[0]
        dact = _mm(ddn, lw["w_down"], "nt", F32, "mm_down_dx")
        GW["ffn_w_down"][l] = _mm(s["act"], ddn, "tn", BF16, "mm_down_dw")
        up = s["upre"]
        dug, duu, dwg, dwu, dbg, dbu = _rows(
            _k_ffnact_bwd, T, tm_ffn,
            [(up, FFN_TILE, gcol, "cur"), (up, FFN_TILE, ucol, "cur"), (dact, FFN_TILE, gcol, "cur"), (up, FFN_TILE, gcol, "prev"),
             (up, FFN_TILE, ucol, "prev"), (up, FFN_TILE, gcol, "next"), (up, FFN_TILE, ucol, "next"), (dact, FFN_TILE, gcol, "next")],
            [(lw["ffn_w"], FFN_TILE, gcol), (lw["ffn_w"], FFN_TILE, ucol), (s["ffn_b"], FFN_TILE, gcol), (s["ffn_b"], FFN_TILE, ucol)],
            [(FFN_DIM, BF16, FFN_TILE, gcol)] * 2,
            [(HALO, FFN_DIM, FFN_TILE, gcol)] * 2 + [(1, FFN_DIM, FFN_TILE, gcol)] * 2, "ffn_act_bwd", ncol=nt)
        GW["ffn_conv_w"][l] = jnp.concatenate([dwg[:3], dwu[:3]], axis=1)
        GS["ffn_conv_b"][l] = jnp.concatenate([dbg[0], dbu[0]])
        dh2 = _mm((dug, duu), lw["w_up"], "nt", F32, "mm_up_dx")
        GW["ffn_w_up"][l] = (_mm(s["h2"], dug, "tn", BF16, "mm_up_dw_gate"), _mm(s["h2"], duu, "tn", BF16, "mm_up_dw_up"))
        gx0, dmixed, dgp, dgf = _rows_vjp(_f_post_mix, T, tm, [s["x"], s["mixed"]], [s["g_post"], s["g_fpre"]], [gx1, dh2],
                                          [F32, BF16], "post_mix_bwd")
        GS["norm_mix_post"][l], GS["norm_ffn_pre"][l] = dgp[0], dgf[0]
        dycat = _mm(dmixed, lw["w_out"], "nt", F32, "mm_out_dx")
        GW["w_out"][l] = _unpad_out_rows(_mm(s["ycat"], dmixed, "tn", BF16, "mm_out_dw"))
        zin = s["zin"]
        dyscan, dz, dnw = _rows(_vjp_wrap(_f_ssd_gate, 2, 1), T, tm,
                                [_cur(s["yscan"]), _cur(zin, SSD_DIM, Z_SSZ // SSD_DIM), _cur(dycat, SSD_DIM, (HEADS * HP + SC_DIM) // SSD_DIM)],
                                [_cst(s["ssd_nw"])], [_out(SSD_DIM, F32), _out(SSD_DIM, BF16)], [_acc(1, SSD_DIM)], "ssd_gate_bwd")
        GS["ssd_norm"][l] = dnw[0]
        dxbc, ddtraw, dpar = _ssd_bwd(s["xbc"], zin, s["ssd_par"], s["states"], dyscan, T, Z_DT // LANE)
        GS["ssd_dt_bias"][l], GS["ssd_a_log"][l], GS["ssd_d"][l] = dpar[0, :SSD_HEADS], dpar[1, :SSD_HEADS], dpar[2, :SSD_HEADS]
        xb = Z_XBC // SSD_CONV_DIM
        dxraw, dsw, dsb = _rows(_k_ssdconv_bwd, T, tm,
                                [_cur(zin, SSD_CONV_DIM, xb), _cur(dxbc), _halo(zin, "prev", SSD_CONV_DIM, xb),
                                 _halo(zin, "next", SSD_CONV_DIM, xb), _halo(dxbc, "next")],
                                [_cst(lw["ssd_w"]), _cst(s["ssd_b"])], [_out(SSD_CONV_DIM, BF16)],
                                [_acc(HALO, SSD_CONV_DIM), _acc(1, SSD_CONV_DIM)], "ssd_conv_bwd")
        GW["ssd_conv_w"][l] = dsw[:4]
        GS["ssd_conv_b"][l] = dsb[0]
        cb = (HEADS * HP) // SC_DIM
        dscb, dscc, dsch, dscw = _rows(_k_sconv_bwd, T, tm,
                                       [_cur(zin, SC_DIM, Z_SCB // SC_DIM), _cur(zin, SC_DIM, Z_SCC // SC_DIM),
                                        _cur(zin, SC_DIM, Z_SCH // SC_DIM), _cur(dycat, SC_DIM, cb),
                                        _halo(zin, "prev", SC_DIM, Z_SCC // SC_DIM), _halo(zin, "prev", SC_DIM, Z_SCH // SC_DIM),
                                        _halo(zin, "next", SC_DIM, Z_SCB // SC_DIM), _halo(dycat, "next", SC_DIM, cb)],
                                       [_cst(lw["sc_w"])], [_out(SC_DIM, BF16)] * 3, [_acc(HALO, SC_DIM)], "short_conv_bwd")
        GW["sc_conv_w"][l] = dscw[:3]
        if ex is None:
            dq, dk, dv = _flash_bwd(s["qr"], s["kr"], s["kvpad"], s["o"], dycat, T)
        else:
            late = ex.submit({n: GW[n][l] for ns in LATE for n in ns}, "late")
            dq, dk, dv, parts = _flash_bwd(s["qr"], s["kr"], s["kvpad"], s["o"], dycat, T, carry=late + (pending or []))
            ex.collect(l, "late", parts[:len(late)])
            if pending:
                ex.collect(l + 1, "early", parts[len(late):])
        dqpad, dkvpad, dkr = _rows(_k_rope_bwd, T, tm, [_cur(dq), _cur(dk), _cur(dv), _cur(cosf), _cur(sina), _cur(sinb)], [],
                                   [_out(HEADS * HP, BF16), _out(2 * HEADS * HP, BF16), _out(LANE, BF16)], [], "mla_rope_bwd")
        dqlat = _mm(dqpad, lw["w_q"], "nt", F32, "mm_q_dx")
        GW["mla_w_q_up"][l] = _unpad_heads(_mm(s["qlat"], dqpad, "tn", BF16, "mm_q_dw"), NOPE + ROPE)
        dkvlat = _mm(dkvpad, lw["w_kv"], "nt", F32, "mm_kv_dx")
        GW["mla_w_kv_up"][l] = _unpad_kv(_mm(s["kvlat"], dkvpad, "tn", BF16, "mm_kv_dw"))
        dcq, dckv, dqn, dkvn = _rows(_vjp_wrap(_f_mla_pre, 2, 2), T, tm,
                                     [_cur(zin, Q_LORA, 0), _cur(zin, KV_LORA, Z_CKV // KV_LORA), _cur(dqlat), _cur(dkvlat)],
                                     [_cst(s["qn"]), _cst(s["kvn"])], [_out(Q_LORA, BF16), _out(KV_LORA, BF16)],
                                     [_acc(1, Q_LORA), _acc(1, KV_LORA)], "mla_pre_bwd")
        GS["mla_q_norm"][l], GS["mla_kv_norm"][l] = dqn[0], dkvn[0]
        dzin = jnp.concatenate([dcq, dckv, dkr, dscb, dscc, dsch, dz, dxraw, ddtraw.astype(BF16), jnp.zeros((T, ZIN - Z_DT - LANE), BF16)], axis=1)
        dh1 = _mm(dzin, lw["w_in"], "nn", F32, "mm_in_dx")
        GW["w_in"][l] = _unpad_rows_in(_mm(dzin, s["h1"], "tn", BF16, "mm_in_dw"))
        gx, dgp = _rows(_vjp_wrap(_f_premix, 1, 1, add_first=True), T, tm, [_cur(s["x"]), _cur(dh1), _cur(gx0)], [_cst(s["g_pre"])],
                        [_out(D_MODEL, F32)], [_acc(1, D_MODEL)], "pre_mix_bwd")
        GS["norm_mix_pre"][l] = dgp[0]
        if ex is not None:
            pending = ex.submit({n: GW[n][l] for ns in EARLY for n in ns}, "early")
    if ex is not None:
        ex.collect(0, "early", _rs_chip_exchange(pending))
    GS = {k: jnp.stack(v) for k, v in GS.items()}
    return loss_part[0, 0], gx, GW, GS


WEIGHTS = ("norm_mix_pre", "norm_mix_post", "norm_ffn_pre", "norm_ffn_post", "w_in", "mla_q_norm", "mla_w_q_up", "mla_kv_norm",
           "mla_w_kv_up", "sc_conv_w", "ssd_conv_w", "ssd_conv_b", "ssd_dt_bias", "ssd_a_log", "ssd_d", "ssd_norm", "w_out",
           "ffn_w_up", "ffn_conv_w", "ffn_conv_b", "ffn_w_down")
SHARDED = (("w_in", 2), ("mla_w_q_up", 2), ("mla_w_kv_up", 2), ("sc_conv_w", 2), ("ssd_conv_w", 2), ("w_out", 1),
           ("ffn_w_up", 2), ("ffn_conv_w", 2), ("ffn_w_down", 1))
SMALL = tuple(n for n in WEIGHTS if n not in dict(SHARDED))
N_CHIPS = 4
N_DEV = 8
ROW_ALIGN = 64
SLAB_ALIGN = 16
EARLY = (("w_in", "mla_w_q_up", "mla_w_kv_up", "sc_conv_w", "ssd_conv_w"),)
LATE = (("ffn_w_down", "w_out"), ("ffn_w_up", "ffn_conv_w"))
TRANSPOSED = ("w_in",)


def _is_rows(shape, width):
    return shape[-1] == width and math.prod(shape[:-1]) % SLAB_ALIGN == 0


def _is_short(shape, width):
    return len(shape) == 2 and shape[1] == width and not _is_rows(shape, width)


def _slab_rows(shape, width):
    if _is_rows(shape, width):
        return math.prod(shape[:-1])
    if _is_short(shape, width):
        return -(-shape[0] // SLAB_ALIGN) * SLAB_ALIGN
    return -(-math.prod(shape) // (width * SLAB_ALIGN)) * SLAB_ALIGN


def _slab(piece, width, dtype, lead=0):
    ld, shape = piece.shape[:lead], piece.shape[lead:]
    rows = _slab_rows(shape, width)
    if _is_rows(shape, width):
        return piece.astype(dtype).reshape(ld + (rows, width))
    if _is_short(shape, width):
        return jnp.pad(piece.astype(dtype), [(0, 0)] * lead + [(0, rows - shape[0]), (0, 0)])
    flat = piece.astype(dtype).reshape(ld + (-1,))
    return jnp.pad(flat, [(0, 0)] * lead + [(0, rows * width - flat.shape[-1])]).reshape(ld + (rows, width))


def _unslab(slab, shape, lead=0):
    ld = slab.shape[:lead]
    if _is_rows(shape, slab.shape[-1]):
        return slab.reshape(ld + tuple(shape))
    if _is_short(shape, slab.shape[-1]):
        return slab[..., :shape[0], :]
    return slab.reshape(ld + (-1,))[..., :math.prod(shape)].reshape(ld + tuple(shape))


def _layout(shapes, names, width):
    ents, off = [], 0
    for n in names:
        shp = tuple(shapes[n])
        todo = [(None, False, shp), (None, True, shp)] if n.endswith("conv_w") else [(l, False, shp[1:]) for l in range(shp[0])]
        for l, lo, ps in todo:
            r = _slab_rows(ps, width)
            ents.append((n, l, lo, ps, off, r))
            off += r
    return width, -(-off // ROW_ALIGN) * ROW_ALIGN, ents


def _pack(layout, piece, dtype, lead=0):
    width, rows, ents = layout
    slabs, ld = [], None
    for n, l, lo, ps, off, r in ents:
        p = piece(n, l, lo)
        slabs.append(None if p is None else _slab(p, width, dtype, lead))
        ld = ld if p is None else p.shape[:lead]
    used = ents[-1][4] + ents[-1][5]
    slabs = [jnp.zeros(ld + (e[5], width), dtype) if s is None else s for s, e in zip(slabs, ents)]
    if rows > used:
        slabs.append(jnp.zeros(ld + (rows - used, width), dtype))
    return jnp.concatenate(slabs, axis=lead)


ANY = pl.BlockSpec(memory_space=pl.ANY)


def _pos():
    return lax.axis_index("x"), lax.axis_index("y"), lax.axis_index("c")


def _other_chips(x, y):
    return ((1 - x, y), (x, 1 - y), (1 - x, 1 - y))


def _remote(src, dst, ssem, rsem, dev):
    return pltpu.make_async_remote_copy(src_ref=src, dst_ref=dst, send_sem=ssem, recv_sem=rsem, device_id=dev, device_id_type=MESH)


AG_CHUNKS = 2


def _chip_index():
    return 2 * lax.axis_index("x") + lax.axis_index("y")


def _ag_sems(nbuf):
    return [pltpu.SemaphoreType.DMA((nbuf * 3 * AG_CHUNKS,))] * 4


def _ag_plan(w_refs, out_refs, sems):
    isend, irecv, dsend, drecv = sems
    x, y, c = _pos()
    k = 2 * x + y
    sib = (x, y, 1 - c)
    sends, lands, forwards, finals = [], [], [], []
    s = 0
    for w_ref, out_ref in zip(w_refs, out_refs):
        H = w_ref.shape[0] // 2
        CH = H // AG_CHUNKS
        for cx, cy in _other_chips(x, y):
            for ch in range(AG_CHUNKS):
                mine = out_ref.at[k, pl.ds(c * H + ch * CH, CH), :]
                near = out_ref.at[2 * cx + cy, pl.ds(c * H + ch * CH, CH), :]
                far = out_ref.at[2 * cx + cy, pl.ds((1 - c) * H + ch * CH, CH), :]
                sends.append(_remote(w_ref.at[pl.ds(c * H + ch * CH, CH), :], mine, isend.at[s], irecv.at[s], (cx, cy, c)))
                lands.append(_remote(near, near, isend.at[s], irecv.at[s], (cx, cy, c)))
                forwards.append(_remote(near, near, dsend.at[s], drecv.at[s], sib))
                finals.append(_remote(far, far, dsend.at[s], drecv.at[s], sib))
                s += 1
    return sends, lands, forwards, finals


def _own_slot(got, own):
    return lax.dynamic_update_slice(got, own[None], (_chip_index(), 0, 0))


def _all_gather_weights(ws):
    nb = len(ws)

    def body(*refs):
        sends, lands, forwards, finals = _ag_plan(refs[:nb], refs[nb:2 * nb], refs[2 * nb:])
        for cp in sends:
            cp.start()
        for land, fw in zip(lands, forwards):
            land.wait_recv()
            fw.start()
        for cp in finals:
            cp.wait_recv()
        for cp in sends + forwards:
            cp.wait_send()

    got = pl.pallas_call(
        body, name="all_gather_weights", in_specs=[ANY] * nb, out_specs=[ANY] * nb,
        out_shape=[jax.ShapeDtypeStruct((N_CHIPS,) + w.shape, w.dtype) for w in ws], scratch_shapes=_ag_sems(nb),
    )(*ws)
    return [_own_slot(g, w) for g, w in zip(got, ws)]


def _rs_pair_exchange(gs):
    nb = len(gs)

    def body(*refs):
        g_refs, got_refs, (ssem, rsem) = refs[:nb], refs[nb:2 * nb], refs[2 * nb:]
        x, y, c = _pos()
        cps = []
        for b, (g_ref, got_ref) in enumerate(zip(g_refs, got_refs)):
            H = g_ref.shape[1] // 2
            for kk in range(N_CHIPS):
                s = b * N_CHIPS + kk
                cps.append(_remote(g_ref.at[kk, pl.ds((1 - c) * H, H), :], got_ref.at[kk], ssem.at[s], rsem.at[s], (x, y, 1 - c)))
        for cp in cps:
            cp.start()
        for cp in cps:
            cp.wait()

    return pl.pallas_call(
        body, name="rs_pair_exchange", in_specs=[ANY] * nb, out_specs=[ANY] * nb,
        out_shape=[jax.ShapeDtypeStruct((N_CHIPS, g.shape[1] // 2, g.shape[2]), g.dtype) for g in gs],
        scratch_shapes=[pltpu.SemaphoreType.DMA((nb * N_CHIPS,))] * 2,
    )(*gs)


def _chip_sems(nbuf):
    return [pltpu.SemaphoreType.DMA((nbuf * 3,))] * 2


def _chip_plan(p_refs, out_refs, sems):
    ssem, rsem = sems
    x, y, c = _pos()
    sends, lands = [], []
    s = 0
    for p_ref, out_ref in zip(p_refs, out_refs):
        for cx, cy in _other_chips(x, y):
            sends.append(_remote(p_ref.at[2 * cx + cy], out_ref.at[2 * x + y], ssem.at[s], rsem.at[s], (cx, cy, c)))
            land = out_ref.at[2 * cx + cy]
            lands.append(_remote(land, land, ssem.at[s], rsem.at[s], (cx, cy, c)))
            s += 1
    return sends, lands


def _chip_parts(got, ps):
    k = _chip_index()
    return [lax.dynamic_update_slice(g, lax.dynamic_slice_in_dim(p, k, 1, axis=0), (k, 0, 0)) for g, p in zip(got, ps)]


def _rs_chip_exchange(ps):
    nb = len(ps)

    def body(*refs):
        sends, lands = _chip_plan(refs[:nb], refs[nb:2 * nb], refs[2 * nb:])
        for cp in sends:
            cp.start()
        for cp in lands:
            cp.wait_recv()
        for cp in sends:
            cp.wait_send()

    got = pl.pallas_call(
        body, name="rs_chip_exchange", in_specs=[ANY] * nb, out_specs=[ANY] * nb,
        out_shape=[jax.ShapeDtypeStruct(p.shape, p.dtype) for p in ps], scratch_shapes=_chip_sems(nb),
    )(*ps)
    return _chip_parts(got, ps)


def _rs_pair_share(fs):
    nb = len(fs)

    def body(*refs):
        f_refs, out_refs, (ssem, rsem) = refs[:nb], refs[nb:2 * nb], refs[2 * nb:]
        x, y, c = _pos()
        sends, lands = [], []
        for b, (f_ref, out_ref) in enumerate(zip(f_refs, out_refs)):
            sends.append(_remote(f_ref, out_ref.at[c], ssem.at[b], rsem.at[b], (x, y, 1 - c)))
            land = out_ref.at[1 - c]
            lands.append(_remote(land, land, ssem.at[b], rsem.at[b], (x, y, 1 - c)))
        for cp in sends:
            cp.start()
        for cp in lands:
            cp.wait_recv()
        for cp in sends:
            cp.wait_send()

    got = pl.pallas_call(
        body, name="rs_pair_share", in_specs=[ANY] * nb, out_specs=[ANY] * nb,
        out_shape=[jax.ShapeDtypeStruct((2,) + f.shape, f.dtype) for f in fs],
        scratch_shapes=[pltpu.SemaphoreType.DMA((nb,))] * 2,
    )(*fs)
    return [lax.dynamic_update_slice(g, f[None], (lax.axis_index("c"), 0, 0)) for g, f in zip(got, fs)]


def _all_reduce_small(s):
    r, C = s.shape

    def body(s_ref, o_ref, buf, ssem, rsem):
        x, y, c = _pos()
        me = 4 * x + 2 * y + c
        buf[me] = s_ref[...]
        cps = []
        for m in range(1, N_DEV):
            mx, my, mc = (m >> 2) & 1, (m >> 1) & 1, m & 1
            peer = (x ^ mx, y ^ my, c ^ mc)
            cp = _remote(s_ref, buf.at[me], ssem.at[m - 1], rsem.at[m - 1], peer)
            cp.start()
            cps.append(cp)
        for m in range(1, N_DEV):
            mx, my, mc = (m >> 2) & 1, (m >> 1) & 1, m & 1
            src = 4 * (x ^ mx) + 2 * (y ^ my) + (c ^ mc)
            _remote(s_ref, buf.at[src], ssem.at[m - 1], rsem.at[m - 1], (x ^ mx, y ^ my, c ^ mc)).wait_recv()
        for cp in cps:
            cp.wait_send()
        acc = buf[0]
        for j in range(1, N_DEV):
            acc = acc + buf[j]
        o_ref[...] = acc

    return pl.pallas_call(
        body, name="all_reduce_small", in_specs=[pl.BlockSpec(memory_space=pltpu.VMEM)],
        out_specs=pl.BlockSpec(memory_space=pltpu.VMEM), out_shape=jax.ShapeDtypeStruct((r, C), F32),
        scratch_shapes=[pltpu.VMEM((N_DEV, r, C), F32), pltpu.SemaphoreType.DMA((N_DEV - 1,)), pltpu.SemaphoreType.DMA((N_DEV - 1,))],
    )(s)


def _rtile(n, pref):
    if n <= pref:
        return n
    t = (pref // 16) * 16
    while t >= 16:
        if n % t == 0:
            return t
        t -= 16
    raise ValueError(f"no row tile for {n}")


def _rs_pair_sums(gpks):
    gots = _rs_pair_exchange(gpks)
    out = []
    for gpk, got in zip(gpks, gots):
        _, R, C = gpk.shape
        H = R // 2
        own = lax.dynamic_index_in_dim(gpk.reshape(N_CHIPS, 2, H, C), lax.axis_index("c"), axis=1, keepdims=False)
        (part,) = _rows(lambda i, n, a, b: (a.astype(F32) + b.astype(F32),), N_CHIPS * H, _rtile(N_CHIPS * H, 512),
                        [_cur(own.reshape(N_CHIPS * H, C)), _cur(got.reshape(N_CHIPS * H, C))], [], [_out(C, BF16)], [], "rs_pair_add")
        out.append(part.reshape(N_CHIPS, H, C))
    return out


def _rs_chip_sums(parts):
    def add4(i, n, a, b, c, d):
        return (((a.astype(F32) + b.astype(F32)) + c.astype(F32)) + d.astype(F32),)

    out = []
    for p in parts:
        _, H, C = p.shape
        tm = _rtile(H, 1024)
        (red,) = _rows(add4, H, tm, [(p.reshape(N_CHIPS * H, C), C, functools.partial(_const, v=0), j * (H // tm)) for j in range(N_CHIPS)],
                       [], [_out(C, F32)], [], "rs_chip_add")
        out.append(red)
    return out


class _Exchange:
    def __init__(self, a):
        self.a = a
        self.axis = {n: (1 if n in TRANSPOSED else ax) for n, ax in SHARDED}
        shapes = {n: (1,) + tuple(self.packed(n, a[n]).shape[1:]) for n in self.axis}
        widths = lambda names: shapes[names[0]][-1] if names[0] == "ffn_w_up" else PACK_COLS
        self.layouts = {"early": [_layout(shapes, ns, widths(ns)) for ns in EARLY], "late": [_layout(shapes, ns, widths(ns)) for ns in LATE]}
        self.reduced = {}

    @staticmethod
    def packed(n, w):
        return jnp.swapaxes(w, -1, -2) if n in TRANSPOSED else w

    def shard(self, l, group):
        def piece(n, li, lo):
            w = self.packed(n, self.a[n][l:l + 1] if li is None else self.a[n][l])
            return w - w.astype(BF16).astype(F32) if lo else w
        return [_pack(lay, piece, BF16) for lay in self.layouts[group]]

    def weights(self, gathered, group):
        W, resid = {}, {}
        for (width, rows, ents), g in zip(self.layouts[group], gathered):
            for n, li, lo, ps, off, r in ents:
                parts = _unslab(g[:, off:off + r], ps, lead=1)
                ax = self.axis[n] + (1 if li is None else 0)
                full = jnp.moveaxis(parts, 0, ax - 1)
                full = full.reshape(full.shape[:ax - 1] + (-1,) + full.shape[ax + 1:])
                (resid if lo else W)[n] = full[0] if li is None else full
        for n in resid:
            W[n] = W[n].astype(F32) + resid[n].astype(F32)
        return W

    def submit(self, GW, group):
        def by_chip(g, ax, parts=N_CHIPS):
            g = g.reshape(g.shape[:ax] + (parts, g.shape[ax] // parts) + g.shape[ax + 1:])
            return jnp.moveaxis(g, ax, 0)

        def piece(n, li, lo):
            if lo:
                return None
            g = GW[n]
            if isinstance(g, tuple):
                return jnp.concatenate([by_chip(h, self.axis[n] - 1, N_CHIPS // 2) for h in g])
            return by_chip(g[None], self.axis[n]) if li is None else by_chip(g, self.axis[n] - 1)

        return _rs_pair_sums([_pack(lay, piece, BF16, lead=1) for lay in self.layouts[group]])

    def collect(self, l, group, parts):
        self.reduced[l, group] = _rs_chip_sums(parts)

    def finish(self):
        keys = [(l, g) for l in range(DEPTH) for g in self.layouts]
        flat = _rs_pair_share([f for key in keys for f in self.reduced[key]])
        both, at = {}, 0
        for key in keys:
            both[key] = flat[at:at + len(self.layouts[key[1]])]
            at += len(self.layouts[key[1]])
        grads = {}
        for group, lays in self.layouts.items():
            for b, (width, rows, ents) in enumerate(lays):
                for n, li, lo, ps, off, r in ents:
                    if not lo:
                        per_layer = [self.packed(n, _unslab(both[l, group][b].reshape(rows, width)[off:off + r], ps)) for l in range(DEPTH)]
                        grads[n] = jnp.concatenate(per_layer) if li is None else jnp.stack(per_layer)
        return grads


def _adam(w, g, m, v, name, g_row=0):
    shp = w.shape
    two = lambda a: a.reshape(-1, shp[-1])
    rows = math.prod(shp[:-1])
    tm = _rtile(rows, 256)
    assert g_row % tm == 0
    g_in = (two(g), shp[-1], functools.partial(_const, v=0), g_row // tm)
    res = _rows(_k_adam, rows, tm, [_cur(two(w)), g_in, _cur(two(m)), _cur(two(v))], [], [_out(shp[-1], F32)] * 4, [], name)
    return tuple(r.reshape(shp) for r in res)


def _pack_flat(parts, rows):
    flat = jnp.concatenate([p.astype(F32).reshape(-1) for p in parts])
    return jnp.pad(flat, (0, rows * PACK_COLS - flat.shape[0])).reshape(rows, PACK_COLS)


def _unpack_flat(buf, shapes):
    flat, out, off = buf.reshape(-1), [], 0
    for shp in shapes:
        n = math.prod(shp)
        out.append(flat[off:off + n].reshape(shp))
        off += n
    return out


def kernel(x, positions, norm_mix_pre, norm_mix_post, norm_ffn_pre, norm_ffn_post, w_in, mla_q_norm, mla_w_q_up, mla_kv_norm, mla_w_kv_up, sc_conv_w, ssd_conv_w, ssd_conv_b, ssd_dt_bias, ssd_a_log, ssd_d, ssd_norm, w_out, ffn_w_up, ffn_conv_w, ffn_conv_b, ffn_w_down, loss_target, m_norm_mix_pre, m_norm_mix_post, m_norm_ffn_pre, m_norm_ffn_post, m_w_in, m_mla_q_norm, m_mla_w_q_up, m_mla_kv_norm, m_mla_w_kv_up, m_sc_conv_w, m_ssd_conv_w, m_ssd_conv_b, m_ssd_dt_bias, m_ssd_a_log, m_ssd_d, m_ssd_norm, m_w_out, m_ffn_w_up, m_ffn_conv_w, m_ffn_conv_b, m_ffn_w_down, v_norm_mix_pre, v_norm_mix_post, v_norm_ffn_pre, v_norm_ffn_post, v_w_in, v_mla_q_norm, v_mla_w_q_up, v_mla_kv_norm, v_mla_w_kv_up, v_sc_conv_w, v_ssd_conv_w, v_ssd_conv_b, v_ssd_dt_bias, v_ssd_a_log, v_ssd_d, v_ssd_norm, v_w_out, v_ffn_w_up, v_ffn_conv_w, v_ffn_conv_b, v_ffn_w_down):
    a = dict(locals())
    ex = _Exchange(a)
    S = {n: a[n] for n in SMALL}
    loss_part, gx, _, GS = _local_step(a["x"][0], a["positions"][0], a["loss_target"][0], None, S, ex)

    grads, delta, new_m, new_v = {}, {}, {}, {}
    for n, g in ex.finish().items():
        grads[n], delta[n], new_m[n], new_v[n] = _adam(a[n], g, a["m_" + n], a["v_" + n], "adamw_" + n)

    small_shapes = [a[n].shape for n in SMALL]
    rs = -(-(sum(math.prod(s) for s in small_shapes) + 1) // (PACK_COLS * SLAB_ALIGN)) * SLAB_ALIGN
    red = _all_reduce_small(_pack_flat([GS[n] for n in SMALL] + [loss_part.reshape(1)], rs))
    loss = _unpack_flat(red, small_shapes + [(1,)])[-1][0]
    pk = lambda pre: _pack_flat([a[pre + n] for n in SMALL], rs)
    for dst, buf in zip((grads, delta, new_m, new_v), _adam(pk(""), red, pk("m_"), pk("v_"), "adamw_small")):
        dst.update(zip(SMALL, _unpack_flat(buf, small_shapes)))

    return (loss, gx[None], *[grads[n] for n in WEIGHTS], *[delta[n] for n in WEIGHTS], *[new_m[n] for n in WEIGHTS],
            *[new_v[n] for n in WEIGHTS])
```

```python
import functools
import math

import jax
import jax.numpy as jnp
from jax import lax
from jax.experimental import pallas as pl
from jax.experimental.pallas import tpu as pltpu

F32 = jnp.float32
BF16 = jnp.bfloat16
MXU_DTYPE = jnp.bfloat16
HIGHEST = lax.Precision.HIGHEST
MESH = pl.DeviceIdType.MESH

D_MODEL = 1024
DEPTH = 4
HEADS = 8
Q_LORA = 256
KV_LORA = 128
NOPE = 64
ROPE = 32
VDIM = 64
ROPE_THETA = 10000.0
SC_DIM = 256
SSD_HEADS = 4
SSD_HEAD_DIM = 64
SSD_STATE = 128
SSD_DIM = 256
SSD_CONV_DIM = 768
SSD_CHUNK = 128
FFN_DIM = 2816
NORM_EPS = 1e-6
QK_SCALE = (NOPE + ROPE) ** -0.5
LANE = 128
HP = 128
FLASH_HEADS = 4
FLASH_HEADS_FWD = 8

ZIN = 2560
Z_CQ, Z_CKV, Z_KR, Z_SCB, Z_SCC, Z_SCH, Z_SSZ, Z_XBC, Z_DT = 0, 256, 384, 512, 768, 1024, 1280, 1536, 2304
KR_LANE = 64
YCAT = HEADS * HP + SC_DIM + SSD_DIM
FFN_TILE = 256
FFN_ROWS = 2048
ROW_BLOCK = 512

ADAM_LR, ADAM_B1, ADAM_B2, ADAM_EPS, ADAM_WD, ADAM_STEP = 0.001, 0.9, 0.999, 1e-08, 0.01, 10

PACK_COLS = 1024


def _tile(n, pref):
    if n <= pref:
        return n
    t = (pref // LANE) * LANE
    while t >= LANE:
        if n % t == 0:
            return t
        t -= LANE
    raise ValueError(f"no tile for {n}")


MM_TM, MM_TN, MM_TK = 1024, 1408, 1536


def _mm(a, b, mode, out_dtype, name, tm=None, tn=MM_TN, tkmax=MM_TK):
    pair = isinstance(a, tuple)
    a_list = list(a) if pair else [a]
    layer = None
    if isinstance(b, tuple):
        b, layer = b
    bshape = b.shape[-2:]
    if mode == "nn":
        (M, Ka), (_, N) = a_list[0].shape, bshape
    elif mode == "nt":
        (M, Ka), (N, _) = a_list[0].shape, bshape
    else:
        (Ka, M), (_, N) = a_list[0].shape, bshape
    tm = (MM_TN if mode == "tn" else MM_TM) if tm is None else tm
    tm, tn, tk = _tile(M, tm), _tile(N, tn), _tile(Ka, tkmax)
    nka = Ka // tk
    nk = nka * len(a_list)

    def bspec(shape, index):
        if layer is None:
            return pl.BlockSpec(shape, index)
        return pl.BlockSpec((None,) + shape, lambda i, j, k: (layer,) + index(i, j, k))

    if mode == "nn":
        a_specs = [pl.BlockSpec((tm, tk), lambda i, j, k: (i, jnp.minimum(k, nka - 1))),
                   pl.BlockSpec((tm, tk), lambda i, j, k: (i, jnp.maximum(k - nka, 0)))][:len(a_list)]
        b_spec = bspec((tk, tn), lambda i, j, k: (k, j))
        dims = NN
    elif mode == "nt":
        a_specs = [pl.BlockSpec((tm, tk), lambda i, j, k: (i, jnp.minimum(k, nka - 1))),
                   pl.BlockSpec((tm, tk), lambda i, j, k: (i, jnp.maximum(k - nka, 0)))][:len(a_list)]
        b_spec = bspec((tn, tk), lambda i, j, k: (j, k))
        dims = NT
    else:
        a_specs = [pl.BlockSpec((tk, tm), lambda i, j, k: (k, i))]
        b_spec = pl.BlockSpec((tk, tn), lambda i, j, k: (k, j))
        dims = TN
    na = len(a_list)

    def body(*refs):
        a_refs, b_ref, o_ref = refs[:na], refs[na], refs[na + 1]
        k = pl.program_id(2)

        def prod(a_ref):
            return lax.dot_general(a_ref[...].astype(MXU_DTYPE), b_ref[...].astype(MXU_DTYPE), dims, preferred_element_type=F32)

        if nk == 1:
            o_ref[...] = prod(a_refs[0]).astype(o_ref.dtype)
            return
        acc_ref = refs[na + 2]

        @pl.when(k == 0)
        def _():
            acc_ref[...] = prod(a_refs[0])

        @pl.when((k > 0) & (k < nka))
        def _():
            acc_ref[...] += prod(a_refs[0])

        if pair:
            @pl.when(k >= nka)
            def _():
                acc_ref[...] += prod(a_refs[1])

        @pl.when(k == nk - 1)
        def _():
            o_ref[...] = acc_ref[...].astype(o_ref.dtype)

    return pl.pallas_call(
        body, name=name, grid=(M // tm, N // tn, nk),
        in_specs=a_specs + [b_spec], out_specs=pl.BlockSpec((tm, tn), lambda i, j, k: (i, j)),
        out_shape=jax.ShapeDtypeStruct((M, N), out_dtype),
        scratch_shapes=[pltpu.VMEM((tm, tn), F32)] if nk > 1 else [],
        compiler_params=pltpu.CompilerParams(dimension_semantics=("parallel", "parallel", "arbitrary")),
    )(*a_list, b)


HALO = 8


def _const(j, v):
    return v


def _rows(fn, T, tm, ins, consts, outs, accs, name, ncol=1):
    n = T // tm
    hb = tm // HALO
    last = T // HALO - 1
    in_specs, args = [], []
    for arr, bc, cb, kind in ins:
        if isinstance(kind, int):
            in_specs.append(pl.BlockSpec((tm, bc), lambda j, i, cb=cb, off=kind: (i + off, cb(j))))
        elif kind == "cur":
            in_specs.append(pl.BlockSpec((tm, bc), lambda j, i, cb=cb: (i, cb(j))))
        elif kind == "prev":
            in_specs.append(pl.BlockSpec((HALO, bc), lambda j, i, cb=cb: (jnp.maximum(i * hb - 1, 0), cb(j))))
        else:
            in_specs.append(pl.BlockSpec((HALO, bc), lambda j, i, cb=cb: (jnp.minimum((i + 1) * hb, last), cb(j))))
        args.append(arr)
    for arr, bc, cb in consts:
        in_specs.append(pl.BlockSpec((arr.shape[0], bc), lambda j, i, cb=cb: (0, cb(j))))
        args.append(arr)
    out_specs, out_shape = [], []
    for tc, dt, bc, cb in outs:
        out_specs.append(pl.BlockSpec((tm, bc), lambda j, i, cb=cb: (i, cb(j))))
        out_shape.append(jax.ShapeDtypeStruct((T, tc), dt))
    for r, tc, bc, cb in accs:
        out_specs.append(pl.BlockSpec((r, bc), lambda j, i, cb=cb: (0, cb(j))))
        out_shape.append(jax.ShapeDtypeStruct((r, tc), F32))
    nin, nout, nacc = len(args), len(outs), len(accs)

    def body(*refs):
        i = pl.program_id(1)
        res = fn(i, n, *[r[...] for r in refs[:nin]])
        for r, v in zip(refs[nin:nin + nout], res[:nout]):
            r[...] = v.astype(r.dtype)
        if nacc:
            acc_refs = refs[nin + nout:nin + nout + nacc]

            @pl.when(i == 0)
            def _():
                for r in acc_refs:
                    r[...] = jnp.zeros_like(r)

            for r, v in zip(acc_refs, res[nout:]):
                r[...] += v.astype(F32)

    res = pl.pallas_call(
        body, name=name, grid=(ncol, n), in_specs=in_specs, out_specs=out_specs, out_shape=out_shape,
        compiler_params=pltpu.CompilerParams(dimension_semantics=("arbitrary", "arbitrary")),
    )(*args)
    return res


def _cur(arr, bc=None, blk=0):
    bc = arr.shape[1] if bc is None else bc
    return (arr, bc, functools.partial(_const, v=blk), "cur")


def _halo(arr, kind, bc=None, blk=0):
    bc = arr.shape[1] if bc is None else bc
    return (arr, bc, functools.partial(_const, v=blk), kind)


def _cst(arr):
    return (arr, arr.shape[1], functools.partial(_const, v=0))


def _out(cols, dt):
    return (cols, dt, cols, functools.partial(_const, v=0))


def _acc(rows, cols):
    return (rows, cols, cols, functools.partial(_const, v=0))


def _rms(x, w):
    return x * lax.rsqrt(jnp.mean(x * x, axis=-1, keepdims=True) + NORM_EPS) * w


def _sigmoid(x):
    return 0.5 * jnp.tanh(0.5 * x) + 0.5


def _silu(x):
    return x * _sigmoid(x)


def _dsilu(x):
    s = _sigmoid(x)
    return s * (1.0 + x * (1.0 - s))


def _softplus(x):
    return jnp.maximum(x, 0.0) + jnp.log1p(jnp.exp(-jnp.abs(x)))


def _shift(a, k):
    return pltpu.roll(a, k % a.shape[0], 0)


def _lroll(a, k):
    return pltpu.roll(a, k % a.shape[1], 1)


def _vjp_wrap(f, nrow, nconst, add_first=False):
    def g(i, n, *vals):
        rows, consts, mid = vals[:nrow], vals[len(vals) - nconst:], vals[nrow:len(vals) - nconst]
        cots = mid[:-1] if add_first else mid
        outs, pull = jax.vjp(f, *rows, *consts)
        grads = list(pull(tuple(c.astype(o.dtype) for c, o in zip(cots, outs))))
        if add_first:
            grads[0] = grads[0] + mid[-1]
        return tuple(grads)
    return g


def _rows_vjp(f, T, tm, rows, consts, cots, out_dtypes, name):
    return _rows(_vjp_wrap(f, len(rows), len(consts)), T, tm, [_cur(r) for r in rows] + [_cur(c) for c in cots],
                 [_cst(c) for c in consts], [_out(r.shape[1], dt) for r, dt in zip(rows, out_dtypes)],
                 [_acc(1, c.shape[1]) for c in consts], name)


def _f_premix(x, g):
    return (_rms(x, g),)


def _f_mla_pre(cq, ckv, qn, kvn):
    return _rms(cq, qn), _rms(ckv, kvn)


def _f_ssd_gate(y, z, nw):
    return (_rms(y * _silu(z), nw),)


def _f_post_mix(x, mixed, gpost, gffn):
    x1 = x + _rms(mixed, gpost)
    return x1, _rms(x1, gffn)


def _f_post_ffn(x1, d, gpost):
    return (x1 + _rms(d, gpost),)


def _rope_fwd(v, cosf, sina, sinb):
    return v * cosf + _lroll(v, -16) * sina + _lroll(v, 16) * sinb


def _rope_bwd(g, cosf, sina, sinb):
    return g * cosf + _lroll(g * sina, 16) + _lroll(g * sinb, -16)


def _k_rope_fwd(i, n, qpad, kvpad, kr, cosf, sina, sinb):
    qs, ks = [], []
    krr = _rope_fwd(kr, cosf, sina, sinb)
    for h in range(HEADS):
        sl = slice(h * HP, (h + 1) * HP)
        qs.append(_rope_fwd(qpad[:, sl], cosf, sina, sinb))
        ks.append(kvpad[:, sl].astype(F32) + krr)
    return jnp.concatenate(qs, axis=1), jnp.concatenate(ks, axis=1)


def _k_rope_bwd(i, n, dq, dk, dv, cosf, sina, sinb):
    lane = lax.broadcasted_iota(jnp.int32, (1, HP), 1)
    rmask = ((lane >= KR_LANE) & (lane < KR_LANE + ROPE)).astype(F32)
    dqs, dks = [], []
    dkr = jnp.zeros((dq.shape[0], HP), F32)
    for h in range(HEADS):
        sl = slice(h * HP, (h + 1) * HP)
        dqs.append(_rope_bwd(dq[:, sl], cosf, sina, sinb))
        dkh = dk[:, sl]
        dkr = dkr + dkh * rmask
        dks.append(dkh * (1.0 - rmask))
    dkr = _rope_bwd(dkr, cosf, sina, sinb) * rmask
    return jnp.concatenate(dqs, axis=1), jnp.concatenate(dks + [dv], axis=1), dkr


def _k_sconv_fwd(i, n, b, c, h, cp, hp, w):
    m = b.shape[0]
    up = jnp.where(i > 0, cp * hp, 0.0)
    ue = jnp.concatenate([up, c * h], axis=0)
    conv = w[2:3] * ue + w[1:2] * _shift(ue, 1) + w[0:1] * _shift(ue, 2)
    return (b * conv[HALO:],)


def _k_sconv_bwd(i, n, b, c, h, dy, cp, hp, bn, dyn, w):
    m = b.shape[0]
    up = jnp.where(i > 0, cp * hp, 0.0)
    ue = jnp.concatenate([up, c * h], axis=0)
    u1, u2 = _shift(ue, 1), _shift(ue, 2)
    conv = (w[2:3] * ue + w[1:2] * u1 + w[0:1] * u2)[HALO:]
    dc_cur = dy * b
    dce = jnp.concatenate([dc_cur, jnp.where(i < n - 1, dyn * bn, 0.0)], axis=0)
    du = (w[2:3] * dce + w[1:2] * _shift(dce, -1) + w[0:1] * _shift(dce, -2))[:m]
    dw = jnp.concatenate([
        jnp.sum(dc_cur * u2[HALO:], axis=0, keepdims=True),
        jnp.sum(dc_cur * u1[HALO:], axis=0, keepdims=True),
        jnp.sum(dc_cur * ue[HALO:], axis=0, keepdims=True),
        jnp.zeros((HALO - 3, b.shape[1]), F32)], axis=0)
    return dy * conv, du * h, du * c, dw


def _conv4(ue, w):
    return w[3:4] * ue + w[2:3] * _shift(ue, 1) + w[1:2] * _shift(ue, 2) + w[0:1] * _shift(ue, 3)


def _k_ssdconv_fwd(i, n, u, up, w, bias):
    ue = jnp.concatenate([jnp.where(i > 0, up, 0.0), u], axis=0)
    return (_silu(_conv4(ue, w)[HALO:] + bias),)


def _k_ssdconv_bwd(i, n, u, dout, up, un, doutn, w, bias):
    m = u.shape[0]
    ue = jnp.concatenate([jnp.where(i > 0, up, 0.0), u, un], axis=0)
    u1, u2, u3 = _shift(ue, 1), _shift(ue, 2), _shift(ue, 3)
    pre = (w[3:4] * ue + w[2:3] * u1 + w[1:2] * u2 + w[0:1] * u3)[HALO:] + bias
    doe = jnp.concatenate([dout, jnp.where(i < n - 1, doutn, 0.0)], axis=0)
    dpre = doe * _dsilu(pre)
    du = (w[3:4] * dpre + w[2:3] * _shift(dpre, -1) + w[1:2] * _shift(dpre, -2) + w[0:1] * _shift(dpre, -3))[:m]
    dp = dpre[:m]
    cur = slice(HALO, HALO + m)
    dw = jnp.concatenate([
        jnp.sum(dp * u3[cur], axis=0, keepdims=True),
        jnp.sum(dp * u2[cur], axis=0, keepdims=True),
        jnp.sum(dp * u1[cur], axis=0, keepdims=True),
        jnp.sum(dp * ue[cur], axis=0, keepdims=True),
        jnp.zeros((HALO - 4, u.shape[1]), F32)], axis=0)
    db = jnp.sum(dp, axis=0, keepdims=True)
    return du, dw, db


def _conv3(ue, w):
    return w[2:3] * ue + w[1:2] * _shift(ue, 1) + w[0:1] * _shift(ue, 2)


def _k_ffnact_fwd(i, n, ug, uu, ugp, uup, wg, wu, bg, bu):
    gate = _conv3(jnp.concatenate([jnp.where(i > 0, ugp, 0.0), ug], axis=0), wg)[HALO:] + bg
    upv = _conv3(jnp.concatenate([jnp.where(i > 0, uup, 0.0), uu], axis=0), wu)[HALO:] + bu
    return (_silu(gate) * upv,)


def _k_ffnact_bwd(i, n, ug, uu, dact, ugp, uup, ugn, uun, dactn, wg, wu, bg, bu):
    m = ug.shape[0]
    cur = slice(HALO, HALO + m)

    def taps(p, c, nx):
        e = jnp.concatenate([jnp.where(i > 0, p, 0.0), c, nx], axis=0)
        return e, _shift(e, 1), _shift(e, 2)

    def back(d, w):
        return (w[2:3] * d + w[1:2] * _shift(d, -1) + w[0:1] * _shift(d, -2))[:m]

    def wgrad(d, t):
        return jnp.concatenate([jnp.sum(d[:m] * t[2][cur], axis=0, keepdims=True), jnp.sum(d[:m] * t[1][cur], axis=0, keepdims=True),
                                jnp.sum(d[:m] * t[0][cur], axis=0, keepdims=True), jnp.zeros((HALO - 3, d.shape[1]), F32)], axis=0)

    tg, tu = taps(ugp, ug, ugn), taps(uup, uu, uun)
    gate = (wg[2:3] * tg[0] + wg[1:2] * tg[1] + wg[0:1] * tg[2])[HALO:] + bg
    upv = (wu[2:3] * tu[0] + wu[1:2] * tu[1] + wu[0:1] * tu[2])[HALO:] + bu
    dae = jnp.concatenate([dact, jnp.where(i < n - 1, dactn, 0.0)], axis=0)
    sg = _sigmoid(gate)
    dg = dae * upv * (sg * (1.0 + gate * (1.0 - sg)))
    dup = dae * (gate * sg)
    return (back(dg, wg), back(dup, wu), wgrad(dg, tg), wgrad(dup, tu),
            jnp.sum(dg[:m], axis=0, keepdims=True), jnp.sum(dup[:m], axis=0, keepdims=True))


def _k_loss(i, n, y, tgt):
    e = y - tgt
    part = 0.5 * jnp.sum(jnp.sum(e * e, axis=1, keepdims=True) / D_MODEL, axis=0, keepdims=True)
    return e * (1.0 / D_MODEL), jnp.broadcast_to(part, (1, LANE))


def _k_adam(i, n, w, g, m, v):
    m = ADAM_B1 * m + (1.0 - ADAM_B1) * g
    v = ADAM_B2 * v + (1.0 - ADAM_B2) * (g * g)
    m_hat = m / (1.0 - ADAM_B1 ** ADAM_STEP)
    v_hat = v / (1.0 - ADAM_B2 ** ADAM_STEP)
    delta = -ADAM_LR * (m_hat / (jnp.sqrt(v_hat) + ADAM_EPS) + ADAM_WD * w)
    return g, delta, m, v


def _dotf(a, b, dims):
    return lax.dot_general(a.astype(MXU_DTYPE), b.astype(MXU_DTYPE), dims, preferred_element_type=F32)


NN = (((1,), (0,)), ((), ()))
NT = (((1,), (1,)), ((), ()))
TN = (((0,), (0,)), ((), ()))


def _ssd_chunk(x0, x1, x2, x3, b0, b1, c0, c1, dtraw, p0, p1, p2, p3, dtb, alog, dsk):
    xs, bs, cs_, ps = (x0, x1, x2, x3), (b0, b1), (c0, c1), (p0, p1, p2, p3)
    L = dtraw.shape[0]
    dt = _softplus(dtraw + dtb)
    adt = dt * (-jnp.exp(alog))
    row = lax.broadcasted_iota(jnp.int32, (L, L), 0)
    col = lax.broadcasted_iota(jnp.int32, (L, L), 1)
    tril = row >= col
    cum = jnp.dot(tril.astype(F32), adt, precision=HIGHEST, preferred_element_type=F32)
    cum_t = cum.T
    lane = lax.broadcasted_iota(jnp.int32, (1, LANE), 1)
    sub = lax.broadcasted_iota(jnp.int32, (LANE, 1), 0)
    lastcol = (lax.broadcasted_iota(jnp.int32, (1, L), 1) == L - 1).astype(F32)
    ys, news = [], []
    for h in range(SSD_HEADS):
        g = h // (SSD_HEADS // 2)
        oh = (lane == h).astype(F32)
        dth = jnp.sum(dt * oh, axis=1, keepdims=True)
        csh = jnp.sum(cum * oh, axis=1, keepdims=True)
        csr = jnp.sum(cum_t * (sub == h).astype(F32), axis=0, keepdims=True)
        cl = jnp.sum(csr * lastcol, axis=1, keepdims=True)
        dskh = jnp.sum(dsk * oh, axis=1, keepdims=True)
        x, bm, cm, prev = xs[h], bs[g], cs_[g], ps[h]
        xdt = x * dth
        decay = jnp.exp(jnp.where(tril, csh - csr, -jnp.inf))
        scores = _dotf(cm, bm, NT) * decay
        y_diag = _dotf(scores, xdt, NN)
        bd = bm * jnp.exp(cl - csh)
        cst = _dotf(xdt, bd, TN)
        news.append(prev * jnp.exp(cl) + cst)
        y_off = _dotf(cm, prev, NT) * jnp.exp(csh)
        ys.append(y_diag + y_off + x * dskh)
    return (*ys, *news)


SSD_STEP = 4


def _ssd_operands(x_ref, dt_ref, par_ref, prev, rows):
    xs = [x_ref[rows, h * SSD_HEAD_DIM:(h + 1) * SSD_HEAD_DIM] for h in range(SSD_HEADS)]
    bs = [x_ref[rows, SSD_DIM + g * SSD_STATE:SSD_DIM + (g + 1) * SSD_STATE] for g in range(2)]
    cs_ = [x_ref[rows, SSD_DIM + 2 * SSD_STATE + g * SSD_STATE:SSD_DIM + 2 * SSD_STATE + (g + 1) * SSD_STATE] for g in range(2)]
    return (*xs, *bs, *cs_, dt_ref[rows, :], *prev, par_ref[0:1, :], par_ref[1:2, :], par_ref[2:3, :])


def _ssd_fwd(xbc, dtraw, par, T, dt_blk=0):
    L = SSD_CHUNK
    nc = T // L
    P = SSD_HEAD_DIM
    U = SSD_STEP if nc % SSD_STEP == 0 else 1

    def body(x_ref, dt_ref, par_ref, y_ref, st_ref, state):
        @pl.when(pl.program_id(0) == 0)
        def _():
            state[...] = jnp.zeros_like(state)

        for u in range(U):
            rows = slice(u * L, (u + 1) * L)
            st_ref[u] = state[...]
            prev = [state[h * P:(h + 1) * P, :] for h in range(SSD_HEADS)]
            res = _ssd_chunk(*_ssd_operands(x_ref, dt_ref, par_ref, prev, rows))
            for h in range(SSD_HEADS):
                y_ref[rows, h * P:(h + 1) * P] = res[h]
                state[h * P:(h + 1) * P, :] = res[SSD_HEADS + h]

    return pl.pallas_call(
        body, name="ssd_scan_fwd", grid=(nc // U,),
        in_specs=[pl.BlockSpec((U * L, SSD_CONV_DIM), lambda c: (c, 0)), pl.BlockSpec((U * L, LANE), lambda c: (c, dt_blk)),
                  pl.BlockSpec((8, LANE), lambda c: (0, 0))],
        out_specs=[pl.BlockSpec((U * L, SSD_DIM), lambda c: (c, 0)), pl.BlockSpec((U, SSD_DIM, SSD_STATE), lambda c: (c, 0, 0))],
        out_shape=[jax.ShapeDtypeStruct((T, SSD_DIM), F32), jax.ShapeDtypeStruct((nc, SSD_DIM, SSD_STATE), F32)],
        scratch_shapes=[pltpu.VMEM((SSD_DIM, SSD_STATE), F32)],
        compiler_params=pltpu.CompilerParams(dimension_semantics=("arbitrary",)),
    )(xbc, dtraw, par)


def _ssd_bwd(xbc, dtraw, par, states, dy, T, dt_blk=0):
    L = SSD_CHUNK
    nc = T // L
    P = SSD_HEAD_DIM
    U = SSD_STEP if nc % SSD_STEP == 0 else 1
    ns = nc // U

    def body(x_ref, dt_ref, par_ref, st_ref, dy_ref, dx_ref, ddt_ref, dpar_ref, dstate):
        @pl.when(pl.program_id(0) == 0)
        def _():
            dstate[...] = jnp.zeros_like(dstate)
            dpar_ref[...] = jnp.zeros_like(dpar_ref)

        for u in reversed(range(U)):
            rows = slice(u * L, (u + 1) * L)
            prev = [st_ref[u, h * P:(h + 1) * P, :] for h in range(SSD_HEADS)]
            prim = _ssd_operands(x_ref, dt_ref, par_ref, prev, rows)
            _, pull = jax.vjp(_ssd_chunk, *prim)
            cots = tuple(dy_ref[rows, h * P:(h + 1) * P] for h in range(SSD_HEADS)) + tuple(
                dstate[h * P:(h + 1) * P, :] for h in range(SSD_HEADS))
            g = pull(cots)
            for h in range(SSD_HEADS):
                dx_ref[rows, h * P:(h + 1) * P] = g[h]
                dstate[h * P:(h + 1) * P, :] = g[9 + h]
            for k in range(2):
                dx_ref[rows, SSD_DIM + k * SSD_STATE:SSD_DIM + (k + 1) * SSD_STATE] = g[4 + k]
                dx_ref[rows, SSD_DIM + 2 * SSD_STATE + k * SSD_STATE:SSD_DIM + 2 * SSD_STATE + (k + 1) * SSD_STATE] = g[6 + k]
            ddt_ref[rows, :] = g[8]
            for r in range(3):
                dpar_ref[r:r + 1, :] += g[13 + r]

    rev = lambda c: (ns - 1 - c, 0)
    return pl.pallas_call(
        body, name="ssd_scan_bwd", grid=(ns,),
        in_specs=[pl.BlockSpec((U * L, SSD_CONV_DIM), rev), pl.BlockSpec((U * L, LANE), lambda c: (ns - 1 - c, dt_blk)),
                  pl.BlockSpec((8, LANE), lambda c: (0, 0)),
                  pl.BlockSpec((U, SSD_DIM, SSD_STATE), lambda c: (ns - 1 - c, 0, 0)), pl.BlockSpec((U * L, SSD_DIM), rev)],
        out_specs=[pl.BlockSpec((U * L, SSD_CONV_DIM), rev), pl.BlockSpec((U * L, LANE), rev), pl.BlockSpec((8, LANE), lambda c: (0, 0))],
        out_shape=[jax.ShapeDtypeStruct((T, SSD_CONV_DIM), F32), jax.ShapeDtypeStruct((T, LANE), F32),
                   jax.ShapeDtypeStruct((8, LANE), F32)],
        scratch_shapes=[pltpu.VMEM((SSD_DIM, SSD_STATE), F32)],
        compiler_params=pltpu.CompilerParams(dimension_semantics=("arbitrary",)),
    )(xbc, dtraw, par, states, dy)


def _causal_pairs(nq, by_query):
    if by_query:
        pairs = [(i, j) for i in range(nq) for j in range(i + 1)]
    else:
        pairs = [(i, j) for j in range(nq) for i in range(j, nq)]
    return jnp.asarray([p[0] for p in pairs], jnp.int32), jnp.asarray([p[1] for p in pairs], jnp.int32)


def _flash_fwd(q, k, kv, T, carry=()):
    tq = tk = min(512, T)
    nq = T // tq
    G = FLASH_HEADS_FWD
    rep = tk // HP
    nc = len(carry)
    qi, kj = _causal_pairs(nq, by_query=True)
    nh, nt = HEADS // G, qi.shape[0]

    def body(qi_ref, kj_ref, q_ref, k_ref, v_ref, *rest):
        w_refs, o_ref, g_refs = rest[:nc], rest[nc], rest[nc + 1:2 * nc + 1]
        m_ref, l_ref, acc_ref = rest[2 * nc + 1:2 * nc + 4]
        h, t = pl.program_id(0), pl.program_id(1)
        i, j = qi_ref[t], kj_ref[t]
        if nc:
            plan = lambda: _ag_plan(w_refs, g_refs, rest[2 * nc + 4:])

            @pl.when((h == 0) & (t == 0))
            def _():
                for cp in plan()[0]:
                    cp.start()

        @pl.when(j == 0)
        def _():
            m_ref[...] = jnp.full_like(m_ref, -jnp.inf)
            l_ref[...] = jnp.zeros_like(l_ref)
            acc_ref[...] = jnp.zeros_like(acc_ref)

        def step(diagonal):
            for g in range(G):
                sl = slice(g * HP, (g + 1) * HP)
                s = _dotf(q_ref[:, sl], k_ref[:, sl], NT) * QK_SCALE
                if diagonal:
                    rows = lax.broadcasted_iota(jnp.int32, (tq, tk), 0)
                    cols = lax.broadcasted_iota(jnp.int32, (tq, tk), 1)
                    s = jnp.where(rows >= cols, s, -jnp.inf)
                m_old = m_ref[:, sl]
                m_new = jnp.maximum(m_old, jnp.max(s, axis=1, keepdims=True))
                p = jnp.exp(s - jnp.tile(m_new, (1, rep)))
                alpha = jnp.exp(m_old - m_new)
                l_ref[:, sl] = alpha * l_ref[:, sl] + jnp.sum(p, axis=1, keepdims=True)
                acc_ref[:, sl] = alpha * acc_ref[:, sl] + _dotf(p, v_ref[:, sl], NN)
                m_ref[:, sl] = m_new

        @pl.when(j < i)
        def _():
            step(False)

        @pl.when(j == i)
        def _():
            step(True)
            lane = lax.broadcasted_iota(jnp.int32, (tq, HP), 1)
            for g in range(G):
                sl = slice(g * HP, (g + 1) * HP)
                l = l_ref[:, sl]
                o_ref[:, sl] = jnp.where(lane < VDIM, acc_ref[:, sl] / l, m_ref[:, sl] + jnp.log(l))

        if nc:
            @pl.when(h * nt + t == (3 * nh * nt) // 4)
            def _():
                _, lands, forwards, _ = plan()
                for land, fw in zip(lands, forwards):
                    land.wait_recv()
                    fw.start()

            @pl.when((h == nh - 1) & (t == nt - 1))
            def _():
                sends, _, forwards, finals = plan()
                for cp in finals:
                    cp.wait_recv()
                for cp in sends + forwards:
                    cp.wait_send()

    W = G * HP
    res = pl.pallas_call(
        body, name="mla_flash_fwd",
        grid_spec=pltpu.PrefetchScalarGridSpec(
            num_scalar_prefetch=2, grid=(nh, nt),
            in_specs=[pl.BlockSpec((tq, W), lambda h, t, qi, kj: (qi[t], h)),
                      pl.BlockSpec((tk, W), lambda h, t, qi, kj: (kj[t], h)),
                      pl.BlockSpec((tk, W), lambda h, t, qi, kj: (kj[t], HEADS // G + h))] + [ANY] * nc,
            out_specs=[pl.BlockSpec((tq, W), lambda h, t, qi, kj: (qi[t], h))] + [ANY] * nc,
            scratch_shapes=[pltpu.VMEM((tq, W), F32), pltpu.VMEM((tq, W), F32), pltpu.VMEM((tq, W), F32)] + (_ag_sems(nc) if nc else [])),
        out_shape=[jax.ShapeDtypeStruct((T, HEADS * HP), F32)] + [jax.ShapeDtypeStruct((N_CHIPS,) + w.shape, w.dtype) for w in carry],
        compiler_params=pltpu.CompilerParams(dimension_semantics=("arbitrary", "arbitrary")),
    )(qi, kj, q, k, kv, *carry)
    return res[0] if not nc else (res[0], [_own_slot(g, w) for g, w in zip(res[1:], carry)])


def _flash_bwd(q, k, kv, o, dycat, T, carry=()):
    tq = tk = min(512, T)
    nq = T // tq
    G = FLASH_HEADS
    nc = len(carry)
    qi, kj = _causal_pairs(nq, by_query=False)
    nh, nt = HEADS // G, qi.shape[0]

    def body(qi_ref, kj_ref, q_ref, k_ref, v_ref, o_ref, do_ref, *rest):
        p_refs, (dq_ref, dk_ref, dv_ref), part_refs = rest[:nc], rest[nc:nc + 3], rest[nc + 3:2 * nc + 3]
        h, t = pl.program_id(0), pl.program_id(1)
        i, j = qi_ref[t], kj_ref[t]
        if nc:
            plan = lambda: _chip_plan(p_refs, part_refs, rest[2 * nc + 3:])

            @pl.when((h == 0) & (t == 0))
            def _():
                for cp in plan()[0]:
                    cp.start()

        @pl.when(t == 0)
        def _():
            dq_ref[...] = jnp.zeros_like(dq_ref)

        @pl.when(i == j)
        def _():
            dk_ref[...] = jnp.zeros_like(dk_ref)
            dv_ref[...] = jnp.zeros_like(dv_ref)

        def step(diagonal):
            r0 = pl.multiple_of(i * tq, tq)
            for g in range(G):
                sl = slice(g * HP, (g + 1) * HP)
                qv, kv, vv, ov, dov = q_ref[:, sl], k_ref[:, sl], v_ref[:, sl], o_ref[:, sl], do_ref[:, sl]
                s = _dotf(qv, kv, NT) * QK_SCALE
                p = jnp.exp(s - ov[:, VDIM:VDIM + 1])
                if diagonal:
                    rows = lax.broadcasted_iota(jnp.int32, (tq, tk), 0)
                    cols = lax.broadcasted_iota(jnp.int32, (tq, tk), 1)
                    p = jnp.where(rows >= cols, p, 0.0)
                dsum = jnp.sum(dov * ov, axis=1, keepdims=True)
                dv_ref[:, sl] += _dotf(p, dov, TN)
                dp = _dotf(dov, vv, NT)
                ds = p * (dp - dsum) * QK_SCALE
                dk_ref[:, sl] += _dotf(ds, qv, TN)
                dq_ref[pl.ds(r0, tq), sl] += _dotf(ds, kv, NN)

        @pl.when(i > j)
        def _():
            step(False)

        @pl.when(i == j)
        def _():
            step(True)

        if nc:
            @pl.when((h == nh - 1) & (t == nt - 1))
            def _():
                sends, lands = plan()
                for cp in lands:
                    cp.wait_recv()
                for cp in sends:
                    cp.wait_send()

    W = G * HP
    qmap = lambda h, t, qi, kj: (qi[t], h)
    kmap = lambda h, t, qi, kj: (kj[t], h)
    vmap = lambda h, t, qi, kj: (kj[t], HEADS // G + h)
    res = pl.pallas_call(
        body, name="mla_flash_bwd",
        grid_spec=pltpu.PrefetchScalarGridSpec(
            num_scalar_prefetch=2, grid=(nh, nt),
            in_specs=[pl.BlockSpec((tq, W), qmap), pl.BlockSpec((tk, W), kmap), pl.BlockSpec((tk, W), vmap),
                      pl.BlockSpec((tq, W), qmap), pl.BlockSpec((tq, W), qmap)] + [ANY] * nc,
            out_specs=[pl.BlockSpec((T, W), lambda h, t, qi, kj: (0, h)), pl.BlockSpec((tk, W), kmap), pl.BlockSpec((tk, W), kmap)]
            + [ANY] * nc,
            scratch_shapes=_chip_sems(nc) if nc else []),
        out_shape=[jax.ShapeDtypeStruct((T, HEADS * HP), F32)] * 3 + [jax.ShapeDtypeStruct(p.shape, p.dtype) for p in carry],
        compiler_params=pltpu.CompilerParams(dimension_semantics=("arbitrary", "arbitrary")),
    )(qi, kj, q, k, kv, o, dycat, *carry)
    return tuple(res[:3]) if not nc else (*res[:3], _chip_parts(res[3:], carry))


_IN_SRC = (0, 256, 384, 416, 672, 928, 1184, 1440, 2208, 2212)
_IN_DST = (Z_CQ, Z_CKV, Z_KR + KR_LANE, Z_SCB, Z_SCC, Z_SCH, Z_SSZ, Z_XBC, Z_DT)


def _pad_rows_in(w):
    ax = w.ndim - 2

    def zeros(n):
        return jnp.zeros(w.shape[:ax] + (n,) + w.shape[ax + 1:], w.dtype)

    def whole_tiles(p):
        n = p.shape[ax]
        return p if n % SLAB_ALIGN == 0 else jnp.pad(p, [(0, 0)] * ax + [(0, -n % SLAB_ALIGN), (0, 0)])

    parts, at = [], 0
    for s0, s1, d0 in zip(_IN_SRC[:-1], _IN_SRC[1:], _IN_DST):
        if d0 > at:
            parts.append(zeros(d0 - at))
        parts.append(whole_tiles(lax.slice_in_dim(w, s0, s1, axis=ax)))
        at = d0 + parts[-1].shape[ax]
    parts.append(zeros(ZIN - at))
    return jnp.concatenate(parts, axis=ax)


def _unpad_rows_in(w):
    ax = w.ndim - 2
    groups = list(zip(_IN_SRC[:-1], _IN_SRC[1:], _IN_DST))
    parts = [lax.slice_in_dim(w, d0, d0 + -(-(s1 - s0) // SLAB_ALIGN) * SLAB_ALIGN, axis=ax) for s0, s1, d0 in groups]
    return lax.slice_in_dim(jnp.concatenate(parts, axis=ax), 0, _IN_SRC[-1], axis=ax)


def _pad_heads(w, width):
    w = w.reshape(w.shape[:-1] + (HEADS, width))
    w = jnp.pad(w, [(0, 0)] * (w.ndim - 1) + [(0, HP - width)])
    return w.reshape(w.shape[:-2] + (HEADS * HP,))


def _unpad_heads(w, width):
    w = w.reshape(w.shape[:-1] + (HEADS, HP))[..., :width]
    return w.reshape(w.shape[:-2] + (HEADS * width,))


def _pad_kv(w):
    w = w.reshape(w.shape[:-1] + (HEADS, NOPE + VDIM))
    return jnp.concatenate([_pad_heads(w[..., :NOPE].reshape(w.shape[:-2] + (HEADS * NOPE,)), NOPE),
                            _pad_heads(w[..., NOPE:].reshape(w.shape[:-2] + (HEADS * VDIM,)), VDIM)], axis=-1)


def _unpad_kv(w):
    k = _unpad_heads(w[..., :HEADS * HP], NOPE).reshape(w.shape[:-1] + (HEADS, NOPE))
    v = _unpad_heads(w[..., HEADS * HP:], VDIM).reshape(w.shape[:-1] + (HEADS, VDIM))
    return jnp.concatenate([k, v], axis=-1).reshape(w.shape[:-1] + (HEADS * (NOPE + VDIM),))


def _pad_out_rows(w):
    lead, d = w.shape[:-2], w.shape[-1]
    att = w[..., :HEADS * VDIM, :].reshape(lead + (HEADS, VDIM, d))
    att = jnp.pad(att, [(0, 0)] * (att.ndim - 2) + [(0, HP - VDIM), (0, 0)]).reshape(lead + (HEADS * HP, d))
    return jnp.concatenate([att, w[..., HEADS * VDIM:, :]], axis=-2)


def _unpad_out_rows(w):
    lead, d = w.shape[:-2], w.shape[-1]
    att = w[..., :HEADS * HP, :].reshape(lead + (HEADS, HP, d))[..., :VDIM, :].reshape(lead + (HEADS * VDIM, d))
    return jnp.concatenate([att, w[..., HEADS * HP:, :]], axis=-2)


def _rows8(w):
    return jnp.pad(w.astype(F32), [(0, 0)] * (w.ndim - 2) + [(0, 8 - w.shape[-2]), (0, 0)])


def _row8(*vecs):
    c = vecs[0].shape[-1]
    return jnp.concatenate([v.reshape(1, c).astype(F32) for v in vecs] + [jnp.zeros((8 - len(vecs), c), F32)], axis=0)


def _lanes(v):
    return jnp.pad(v.astype(F32), (0, LANE - v.shape[0])).reshape(1, LANE)


def _rope_tables(positions):
    inv_freq = 1.0 / (ROPE_THETA ** (jnp.arange(0, ROPE, 2, dtype=F32) / ROPE))
    ang = positions.astype(F32)[:, None] * inv_freq
    cos, sin = jnp.cos(ang), jnp.sin(ang)
    T = positions.shape[0]
    half = ROPE // 2
    one = jnp.ones((T, KR_LANE), F32)
    zero = jnp.zeros((T, KR_LANE), F32)
    tail1 = jnp.ones((T, HP - KR_LANE - ROPE), F32)
    tail0 = jnp.zeros((T, HP - KR_LANE - ROPE), F32)
    z16 = jnp.zeros((T, half), F32)
    cosf = jnp.concatenate([one, cos, cos, tail1], axis=1)
    sina = jnp.concatenate([zero, -sin, z16, tail0], axis=1)
    sinb = jnp.concatenate([zero, z16, sin, tail0], axis=1)
    return cosf, sina, sinb


def _kernel_weights(W):
    c = lambda a: a.astype(MXU_DTYPE)
    forms = dict(
        w_in=("w_in", lambda w: c(_pad_rows_in(w))),
        w_q=("mla_w_q_up", lambda w: c(_pad_heads(w, NOPE + ROPE))),
        w_kv=("mla_w_kv_up", lambda w: c(_pad_kv(w))),
        w_out=("w_out", lambda w: c(_pad_out_rows(w))),
        w_up=("ffn_w_up", c),
        w_down=("ffn_w_down", c),
        sc_w=("sc_conv_w", _rows8),
        ssd_w=("ssd_conv_w", _rows8),
        ffn_w=("ffn_conv_w", _rows8),
    )
    return {k: f(W[n]) for k, (n, f) in forms.items() if n in W}


def _layer_weights(KW, l):
    return {k: (v[l] if k in ("sc_w", "ssd_w", "ffn_w") else (v, l)) for k, v in KW.items()}


def _local_step(x, positions, target, W, S, ex=None):
    T = x.shape[0]
    tm = min(ROW_BLOCK, T)
    tm_ffn = min(FFN_ROWS, T)
    cosf, sina, sinb = _rope_tables(positions)
    if ex is None:
        KW = _kernel_weights(W)
    else:
        early = _all_gather_weights(ex.shard(0, "early"))
    saved = []
    xl = x
    for l in range(DEPTH):
        lw = _layer_weights(KW, l) if ex is None else _kernel_weights(ex.weights(early, "early"))
        g_pre = S["norm_mix_pre"][l].reshape(1, -1)
        g_post = S["norm_mix_post"][l].reshape(1, -1)
        g_fpre = S["norm_ffn_pre"][l].reshape(1, -1)
        g_fpost = S["norm_ffn_post"][l].reshape(1, -1)
        qn = S["mla_q_norm"][l].reshape(1, -1)
        kvn = S["mla_kv_norm"][l].reshape(1, -1)
        ssd_b = S["ssd_conv_b"][l].reshape(1, -1)
        ssd_par = _row8(jnp.pad(S["ssd_dt_bias"][l], (0, LANE - SSD_HEADS)), jnp.pad(S["ssd_a_log"][l], (0, LANE - SSD_HEADS)),
                        jnp.pad(S["ssd_d"][l], (0, LANE - SSD_HEADS)))
        ssd_nw = S["ssd_norm"][l].reshape(1, -1)
        ffn_b = S["ffn_conv_b"][l].reshape(1, -1)

        (h1,) = _rows(lambda i, n, *v: _f_premix(*v), T, tm, [_cur(xl)], [_cst(g_pre)], [_out(D_MODEL, BF16)], [], "pre_mix_norm")
        zin = _mm(h1, lw["w_in"], "nt", F32, "mm_in")
        qlat, kvlat = _rows(lambda i, n, *v: _f_mla_pre(*v), T, tm, [_cur(zin, Q_LORA, 0), _cur(zin, KV_LORA, Z_CKV // KV_LORA)],
                            [_cst(qn), _cst(kvn)], [_out(Q_LORA, BF16), _out(KV_LORA, BF16)], [], "mla_pre_norm")
        qpad = _mm(qlat, lw["w_q"], "nn", F32, "mm_q_up")
        kvpad = _mm(kvlat, lw["w_kv"], "nn", BF16, "mm_kv_up")
        qr, kr = _rows(_k_rope_fwd, T, tm, [_cur(qpad), _cur(kvpad, HEADS * HP, 0), _cur(zin, LANE, Z_KR // LANE),
                                            _cur(cosf), _cur(sina), _cur(sinb)], [],
                       [_out(HEADS * HP, BF16), _out(HEADS * HP, BF16)], [], "mla_rope")
        if ex is None:
            o = _flash_fwd(qr, kr, kvpad, T)
        else:
            nlate = len(ex.layouts["late"])
            o, got = _flash_fwd(qr, kr, kvpad, T, carry=ex.shard(l, "late") + (ex.shard(l + 1, "early") if l + 1 < DEPTH else []))
            lw.update(_kernel_weights(ex.weights(got[:nlate], "late")))
            early = got[nlate:]
        (yconv,) = _rows(_k_sconv_fwd, T, tm, [_cur(zin, SC_DIM, Z_SCB // SC_DIM), _cur(zin, SC_DIM, Z_SCC // SC_DIM),
                                               _cur(zin, SC_DIM, Z_SCH // SC_DIM), _halo(zin, "prev", SC_DIM, Z_SCC // SC_DIM),
                                               _halo(zin, "prev", SC_DIM, Z_SCH // SC_DIM)], [_cst(lw["sc_w"])],
                         [_out(SC_DIM, F32)], [], "short_conv_fwd")
        (xbc,) = _rows(_k_ssdconv_fwd, T, tm, [_cur(zin, SSD_CONV_DIM, Z_XBC // SSD_CONV_DIM),
                                               _halo(zin, "prev", SSD_CONV_DIM, Z_XBC // SSD_CONV_DIM)],
                       [_cst(lw["ssd_w"]), _cst(ssd_b)], [_out(SSD_CONV_DIM, F32)], [], "ssd_conv_fwd")
        yscan, states = _ssd_fwd(xbc, zin, ssd_par, T, Z_DT // LANE)
        (yssd,) = _rows(lambda i, n, *v: _f_ssd_gate(*v), T, tm, [_cur(yscan), _cur(zin, SSD_DIM, Z_SSZ // SSD_DIM)], [_cst(ssd_nw)],
                        [_out(SSD_DIM, F32)], [], "ssd_gate_fwd")
        ycat = jnp.concatenate([o.astype(BF16), yconv.astype(BF16), yssd.astype(BF16)], axis=1)
        mixed = _mm(ycat, lw["w_out"], "nn", F32, "mm_out")
        x1, h2 = _rows(lambda i, n, *v: _f_post_mix(*v), T, tm, [_cur(xl), _cur(mixed)], [_cst(g_post), _cst(g_fpre)],
                       [_out(D_MODEL, F32), _out(D_MODEL, BF16)], [], "post_mix_fwd")
        upre = _mm(h2, lw["w_up"], "nn", F32, "mm_up")
        nt = FFN_DIM // FFN_TILE
        gcol, ucol = (lambda j: j), (lambda j: j + nt)
        (act,) = _rows(_k_ffnact_fwd, T, tm_ffn,
                       [(upre, FFN_TILE, gcol, "cur"), (upre, FFN_TILE, ucol, "cur"), (upre, FFN_TILE, gcol, "prev"),
                        (upre, FFN_TILE, ucol, "prev")],
                       [(lw["ffn_w"], FFN_TILE, gcol), (lw["ffn_w"], FFN_TILE, ucol), (ffn_b, FFN_TILE, gcol), (ffn_b, FFN_TILE, ucol)],
                       [(FFN_DIM, BF16, FFN_TILE, gcol)], [], "ffn_act_fwd", ncol=nt)
        dn = _mm(act, lw["w_down"], "nn", F32, "mm_down")
        (x2,) = _rows(lambda i, n, *v: _f_post_ffn(*v), T, tm, [_cur(x1), _cur(dn)], [_cst(g_fpost)], [_out(D_MODEL, F32)], [], "post_ffn_fwd")
        saved.append(dict(lw=lw, x=xl, h1=h1, zin=zin, qlat=qlat, kvlat=kvlat, qr=qr, kr=kr, kvpad=kvpad, o=o, xbc=xbc,
                          yscan=yscan, states=states, ycat=ycat, mixed=mixed, x1=x1, h2=h2, upre=upre, act=act, dn=dn,
                          g_pre=g_pre, g_post=g_post, g_fpre=g_fpre, g_fpost=g_fpost, qn=qn, kvn=kvn, ssd_b=ssd_b,
                          ssd_par=ssd_par, ssd_nw=ssd_nw, ffn_b=ffn_b))
        xl = x2

    gx, loss_part = _rows(_k_loss, T, tm, [_cur(xl), _cur(target)], [], [_out(D_MODEL, F32)], [_acc(1, LANE)], "loss_head")

    GW = {k: [None] * DEPTH for k in ("w_in", "mla_w_q_up", "mla_w_kv_up", "sc_conv_w", "ssd_conv_w", "w_out", "ffn_w_up",
                                      "ffn_conv_w", "ffn_w_down")}
    GS = {k: [None] * DEPTH for k in ("norm_mix_pre", "norm_mix_post", "norm_ffn_pre", "norm_ffn_post", "mla_q_norm", "mla_kv_norm",
                                      "ssd_conv_b", "ssd_dt_bias", "ssd_a_log", "ssd_d", "ssd_norm", "ffn_conv_b")}
    nt = FFN_DIM // FFN_TILE
    gcol, ucol = (lambda j: j), (lambda j: j + nt)
    pending = None
    for l in reversed(range(DEPTH)):
        s = saved[l]
        lw = s["lw"]
        gx1, ddn, dgf = _rows_vjp(_f_post_ffn, T, tm, [s["x1"], s["dn"]], [s["g_fpost"]], [gx], [F32, BF16], "post_ffn_bwd")
        GS["norm_ffn_post"][l] = dgf[0]
        dact = _mm(ddn, lw["w_down"], "nt", F32, "mm_down_dx")
        GW["ffn_w_down"][l] = _mm(s["act"], ddn, "tn", BF16, "mm_down_dw")
        up = s["upre"]
        dug, duu, dwg, dwu, dbg, dbu = _rows(
            _k_ffnact_bwd, T, tm_ffn,
            [(up, FFN_TILE, gcol, "cur"), (up, FFN_TILE, ucol, "cur"), (dact, FFN_TILE, gcol, "cur"), (up, FFN_TILE, gcol, "prev"),
             (up, FFN_TILE, ucol, "prev"), (up, FFN_TILE, gcol, "next"), (up, FFN_TILE, ucol, "next"), (dact, FFN_TILE, gcol, "next")],
            [(lw["ffn_w"], FFN_TILE, gcol), (lw["ffn_w"], FFN_TILE, ucol), (s["ffn_b"], FFN_TILE, gcol), (s["ffn_b"], FFN_TILE, ucol)],
            [(FFN_DIM, BF16, FFN_TILE, gcol)] * 2,
            [(HALO, FFN_DIM, FFN_TILE, gcol)] * 2 + [(1, FFN_DIM, FFN_TILE, gcol)] * 2, "ffn_act_bwd", ncol=nt)
        GW["ffn_conv_w"][l] = jnp.concatenate([dwg[:3], dwu[:3]], axis=1)
        GS["ffn_conv_b"][l] = jnp.concatenate([dbg[0], dbu[0]])
        dh2 = _mm((dug, duu), lw["w_up"], "nt", F32, "mm_up_dx")
        GW["ffn_w_up"][l] = (_mm(s["h2"], dug, "tn", BF16, "mm_up_dw_gate"), _mm(s["h2"], duu, "tn", BF16, "mm_up_dw_up"))
        gx0, dmixed, dgp, dgf = _rows_vjp(_f_post_mix, T, tm, [s["x"], s["mixed"]], [s["g_post"], s["g_fpre"]], [gx1, dh2],
                                          [F32, BF16], "post_mix_bwd")
        GS["norm_mix_post"][l], GS["norm_ffn_pre"][l] = dgp[0], dgf[0]
        dycat = _mm(dmixed, lw["w_out"], "nt", F32, "mm_out_dx")
        GW["w_out"][l] = _unpad_out_rows(_mm(s["ycat"], dmixed, "tn", BF16, "mm_out_dw"))
        zin = s["zin"]
        dyscan, dz, dnw = _rows(_vjp_wrap(_f_ssd_gate, 2, 1), T, tm,
                                [_cur(s["yscan"]), _cur(zin, SSD_DIM, Z_SSZ // SSD_DIM), _cur(dycat, SSD_DIM, (HEADS * HP + SC_DIM) // SSD_DIM)],
                                [_cst(s["ssd_nw"])], [_out(SSD_DIM, F32), _out(SSD_DIM, BF16)], [_acc(1, SSD_DIM)], "ssd_gate_bwd")
        GS["ssd_norm"][l] = dnw[0]
        dxbc, ddtraw, dpar = _ssd_bwd(s["xbc"], zin, s["ssd_par"], s["states"], dyscan, T, Z_DT // LANE)
        GS["ssd_dt_bias"][l], GS["ssd_a_log"][l], GS["ssd_d"][l] = dpar[0, :SSD_HEADS], dpar[1, :SSD_HEADS], dpar[2, :SSD_HEADS]
        xb = Z_XBC // SSD_CONV_DIM
        dxraw, dsw, dsb = _rows(_k_ssdconv_bwd, T, tm,
                                [_cur(zin, SSD_CONV_DIM, xb), _cur(dxbc), _halo(zin, "prev", SSD_CONV_DIM, xb),
                                 _halo(zin, "next", SSD_CONV_DIM, xb), _halo(dxbc, "next")],
                                [_cst(lw["ssd_w"]), _cst(s["ssd_b"])], [_out(SSD_CONV_DIM, BF16)],
                                [_acc(HALO, SSD_CONV_DIM), _acc(1, SSD_CONV_DIM)], "ssd_conv_bwd")
        GW["ssd_conv_w"][l] = dsw[:4]
        GS["ssd_conv_b"][l] = dsb[0]
        cb = (HEADS * HP) // SC_DIM
        dscb, dscc, dsch, dscw = _rows(_k_sconv_bwd, T, tm,
                                       [_cur(zin, SC_DIM, Z_SCB // SC_DIM), _cur(zin, SC_DIM, Z_SCC // SC_DIM),
                                        _cur(zin, SC_DIM, Z_SCH // SC_DIM), _cur(dycat, SC_DIM, cb),
                                        _halo(zin, "prev", SC_DIM, Z_SCC // SC_DIM), _halo(zin, "prev", SC_DIM, Z_SCH // SC_DIM),
                                        _halo(zin, "next", SC_DIM, Z_SCB // SC_DIM), _halo(dycat, "next", SC_DIM, cb)],
                                       [_cst(lw["sc_w"])], [_out(SC_DIM, BF16)] * 3, [_acc(HALO, SC_DIM)], "short_conv_bwd")
        GW["sc_conv_w"][l] = dscw[:3]
        if ex is None:
            dq, dk, dv = _flash_bwd(s["qr"], s["kr"], s["kvpad"], s["o"], dycat, T)
        else:
            sums = ex.submit([({n: GW[n][l] for ns in LATE for n in ns}, "late")] + ([(pending, "early")] if pending else []))
            late = sums[0]
            dq, dk, dv, parts = _flash_bwd(s["qr"], s["kr"], s["kvpad"], s["o"], dycat, T, carry=[p for ps in sums for p in ps])
            ex.collect(l, "late", parts[:len(late)])
            if pending:
                ex.collect(l + 1, "early", parts[len(late):])
        dqpad, dkvpad, dkr = _rows(_k_rope_bwd, T, tm, [_cur(dq), _cur(dk), _cur(dv), _cur(cosf), _cur(sina), _cur(sinb)], [],
                                   [_out(HEADS * HP, BF16), _out(2 * HEADS * HP, BF16), _out(LANE, BF16)], [], "mla_rope_bwd")
        dqlat = _mm(dqpad, lw["w_q"], "nt", F32, "mm_q_dx")
        GW["mla_w_q_up"][l] = _unpad_heads(_mm(s["qlat"], dqpad, "tn", BF16, "mm_q_dw"), NOPE + ROPE)
        dkvlat = _mm(dkvpad, lw["w_kv"], "nt", F32, "mm_kv_dx")
        GW["mla_w_kv_up"][l] = _unpad_kv(_mm(s["kvlat"], dkvpad, "tn", BF16, "mm_kv_dw"))
        dcq, dckv, dqn, dkvn = _rows(_vjp_wrap(_f_mla_pre, 2, 2), T, tm,
                                     [_cur(zin, Q_LORA, 0), _cur(zin, KV_LORA, Z_CKV // KV_LORA), _cur(dqlat), _cur(dkvlat)],
                                     [_cst(s["qn"]), _cst(s["kvn"])], [_out(Q_LORA, BF16), _out(KV_LORA, BF16)],
                                     [_acc(1, Q_LORA), _acc(1, KV_LORA)], "mla_pre_bwd")
        GS["mla_q_norm"][l], GS["mla_kv_norm"][l] = dqn[0], dkvn[0]
        dzin = jnp.concatenate([dcq, dckv, dkr, dscb, dscc, dsch, dz, dxraw, ddtraw.astype(BF16), jnp.zeros((T, ZIN - Z_DT - LANE), BF16)], axis=1)
        dh1 = _mm(dzin, lw["w_in"], "nn", F32, "mm_in_dx")
        GW["w_in"][l] = _unpad_rows_in(_mm(dzin, s["h1"], "tn", BF16, "mm_in_dw"))
        gx, dgp = _rows(_vjp_wrap(_f_premix, 1, 1, add_first=True), T, tm, [_cur(s["x"]), _cur(dh1), _cur(gx0)], [_cst(s["g_pre"])],
                        [_out(D_MODEL, F32)], [_acc(1, D_MODEL)], "pre_mix_bwd")
        GS["norm_mix_pre"][l] = dgp[0]
        if ex is not None:
            pending = {n: GW[n][l] for ns in EARLY for n in ns}
    if ex is not None:
        ex.collect(0, "early", _rs_chip_exchange(ex.submit([(pending, "early")])[0]))
    GS = {k: jnp.stack(v) for k, v in GS.items()}
    return loss_part[0, 0], gx, GW, GS


WEIGHTS = ("norm_mix_pre", "norm_mix_post", "norm_ffn_pre", "norm_ffn_post", "w_in", "mla_q_norm", "mla_w_q_up", "mla_kv_norm",
           "mla_w_kv_up", "sc_conv_w", "ssd_conv_w", "ssd_conv_b", "ssd_dt_bias", "ssd_a_log", "ssd_d", "ssd_norm", "w_out",
           "ffn_w_up", "ffn_conv_w", "ffn_conv_b", "ffn_w_down")
SHARDED = (("w_in", 2), ("mla_w_q_up", 2), ("mla_w_kv_up", 2), ("sc_conv_w", 2), ("ssd_conv_w", 2), ("w_out", 1),
           ("ffn_w_up", 2), ("ffn_conv_w", 2), ("ffn_w_down", 1))
SMALL = tuple(n for n in WEIGHTS if n not in dict(SHARDED))
N_CHIPS = 4
N_DEV = 8
ROW_ALIGN = 64
SLAB_ALIGN = 16
EARLY = (("w_in", "mla_w_q_up", "mla_w_kv_up", "sc_conv_w", "ssd_conv_w"),)
LATE = (("ffn_w_down", "w_out"), ("ffn_w_up", "ffn_conv_w"))
TRANSPOSED = ("w_in",)


def _is_rows(shape, width):
    return shape[-1] == width and math.prod(shape[:-1]) % SLAB_ALIGN == 0


def _is_short(shape, width):
    return len(shape) == 2 and shape[1] == width and not _is_rows(shape, width)


def _slab_rows(shape, width):
    if _is_rows(shape, width):
        return math.prod(shape[:-1])
    if _is_short(shape, width):
        return -(-shape[0] // SLAB_ALIGN) * SLAB_ALIGN
    return -(-math.prod(shape) // (width * SLAB_ALIGN)) * SLAB_ALIGN


def _slab(piece, width, dtype, lead=0):
    ld, shape = piece.shape[:lead], piece.shape[lead:]
    rows = _slab_rows(shape, width)
    if _is_rows(shape, width):
        return piece.astype(dtype).reshape(ld + (rows, width))
    if _is_short(shape, width):
        return jnp.pad(piece.astype(dtype), [(0, 0)] * lead + [(0, rows - shape[0]), (0, 0)])
    flat = piece.astype(dtype).reshape(ld + (-1,))
    return jnp.pad(flat, [(0, 0)] * lead + [(0, rows * width - flat.shape[-1])]).reshape(ld + (rows, width))


def _unslab(slab, shape, lead=0):
    ld = slab.shape[:lead]
    if _is_rows(shape, slab.shape[-1]):
        return slab.reshape(ld + tuple(shape))
    if _is_short(shape, slab.shape[-1]):
        return slab[..., :shape[0], :]
    return slab.reshape(ld + (-1,))[..., :math.prod(shape)].reshape(ld + tuple(shape))


def _layout(shapes, names, width):
    ents, off = [], 0
    for n in names:
        shp = tuple(shapes[n])
        todo = [(None, False, shp), (None, True, shp)] if n.endswith("conv_w") else [(l, False, shp[1:]) for l in range(shp[0])]
        for l, lo, ps in todo:
            r = _slab_rows(ps, width)
            ents.append((n, l, lo, ps, off, r))
            off += r
    return width, -(-off // ROW_ALIGN) * ROW_ALIGN, ents


def _pack(layout, piece, dtype, lead=0):
    width, rows, ents = layout
    slabs, ld = [], None
    for n, l, lo, ps, off, r in ents:
        p = piece(n, l, lo)
        slabs.append(None if p is None else _slab(p, width, dtype, lead))
        ld = ld if p is None else p.shape[:lead]
    used = ents[-1][4] + ents[-1][5]
    slabs = [jnp.zeros(ld + (e[5], width), dtype) if s is None else s for s, e in zip(slabs, ents)]
    if rows > used:
        slabs.append(jnp.zeros(ld + (rows - used, width), dtype))
    return jnp.concatenate(slabs, axis=lead)


ANY = pl.BlockSpec(memory_space=pl.ANY)


def _pos():
    return lax.axis_index("x"), lax.axis_index("y"), lax.axis_index("c")


def _other_chips(x, y):
    return ((1 - x, y), (x, 1 - y), (1 - x, 1 - y))


def _remote(src, dst, ssem, rsem, dev):
    return pltpu.make_async_remote_copy(src_ref=src, dst_ref=dst, send_sem=ssem, recv_sem=rsem, device_id=dev, device_id_type=MESH)


AG_CHUNKS = 2


def _chip_index():
    return 2 * lax.axis_index("x") + lax.axis_index("y")


def _ag_sems(nbuf):
    return [pltpu.SemaphoreType.DMA((nbuf * 3 * AG_CHUNKS,))] * 4


def _ag_plan(w_refs, out_refs, sems):
    isend, irecv, dsend, drecv = sems
    x, y, c = _pos()
    k = 2 * x + y
    sib = (x, y, 1 - c)
    sends, lands, forwards, finals = [], [], [], []
    s = 0
    for w_ref, out_ref in zip(w_refs, out_refs):
        H = w_ref.shape[0] // 2
        CH = H // AG_CHUNKS
        for cx, cy in _other_chips(x, y):
            for ch in range(AG_CHUNKS):
                mine = out_ref.at[k, pl.ds(c * H + ch * CH, CH), :]
                near = out_ref.at[2 * cx + cy, pl.ds(c * H + ch * CH, CH), :]
                far = out_ref.at[2 * cx + cy, pl.ds((1 - c) * H + ch * CH, CH), :]
                sends.append(_remote(w_ref.at[pl.ds(c * H + ch * CH, CH), :], mine, isend.at[s], irecv.at[s], (cx, cy, c)))
                lands.append(_remote(near, near, isend.at[s], irecv.at[s], (cx, cy, c)))
                forwards.append(_remote(near, near, dsend.at[s], drecv.at[s], sib))
                finals.append(_remote(far, far, dsend.at[s], drecv.at[s], sib))
                s += 1
    return sends, lands, forwards, finals


def _own_slot(got, own):
    return lax.dynamic_update_slice(got, own[None], (_chip_index(), 0, 0))


def _all_gather_weights(ws):
    nb = len(ws)

    def body(*refs):
        sends, lands, forwards, finals = _ag_plan(refs[:nb], refs[nb:2 * nb], refs[2 * nb:])
        for cp in sends:
            cp.start()
        for land, fw in zip(lands, forwards):
            land.wait_recv()
            fw.start()
        for cp in finals:
            cp.wait_recv()
        for cp in sends + forwards:
            cp.wait_send()

    got = pl.pallas_call(
        body, name="all_gather_weights", in_specs=[ANY] * nb, out_specs=[ANY] * nb,
        out_shape=[jax.ShapeDtypeStruct((N_CHIPS,) + w.shape, w.dtype) for w in ws], scratch_shapes=_ag_sems(nb),
    )(*ws)
    return [_own_slot(g, w) for g, w in zip(got, ws)]


def _rs_pair_exchange(gs):
    nb = len(gs)

    def body(*refs):
        g_refs, got_refs, (ssem, rsem) = refs[:nb], refs[nb:2 * nb], refs[2 * nb:]
        x, y, c = _pos()
        cps = []
        for b, (g_ref, got_ref) in enumerate(zip(g_refs, got_refs)):
            H = g_ref.shape[1] // 2
            for kk in range(N_CHIPS):
                s = b * N_CHIPS + kk
                cps.append(_remote(g_ref.at[kk, pl.ds((1 - c) * H, H), :], got_ref.at[kk], ssem.at[s], rsem.at[s], (x, y, 1 - c)))
        for cp in cps:
            cp.start()
        for cp in cps:
            cp.wait()

    return pl.pallas_call(
        body, name="rs_pair_exchange", in_specs=[ANY] * nb, out_specs=[ANY] * nb,
        out_shape=[jax.ShapeDtypeStruct((N_CHIPS, g.shape[1] // 2, g.shape[2]), g.dtype) for g in gs],
        scratch_shapes=[pltpu.SemaphoreType.DMA((nb * N_CHIPS,))] * 2,
    )(*gs)


def _chip_sems(nbuf):
    return [pltpu.SemaphoreType.DMA((nbuf * 3,))] * 2


def _chip_plan(p_refs, out_refs, sems):
    ssem, rsem = sems
    x, y, c = _pos()
    sends, lands = [], []
    s = 0
    for p_ref, out_ref in zip(p_refs, out_refs):
        for cx, cy in _other_chips(x, y):
            sends.append(_remote(p_ref.at[2 * cx + cy], out_ref.at[2 * x + y], ssem.at[s], rsem.at[s], (cx, cy, c)))
            land = out_ref.at[2 * cx + cy]
            lands.append(_remote(land, land, ssem.at[s], rsem.at[s], (cx, cy, c)))
            s += 1
    return sends, lands


def _chip_parts(got, ps):
    k = _chip_index()
    return [lax.dynamic_update_slice(g, lax.dynamic_slice_in_dim(p, k, 1, axis=0), (k, 0, 0)) for g, p in zip(got, ps)]


def _rs_chip_exchange(ps):
    nb = len(ps)

    def body(*refs):
        sends, lands = _chip_plan(refs[:nb], refs[nb:2 * nb], refs[2 * nb:])
        for cp in sends:
            cp.start()
        for cp in lands:
            cp.wait_recv()
        for cp in sends:
            cp.wait_send()

    got = pl.pallas_call(
        body, name="rs_chip_exchange", in_specs=[ANY] * nb, out_specs=[ANY] * nb,
        out_shape=[jax.ShapeDtypeStruct(p.shape, p.dtype) for p in ps], scratch_shapes=_chip_sems(nb),
    )(*ps)
    return _chip_parts(got, ps)


def _rs_pair_share(fs):
    nb = len(fs)

    def body(*refs):
        f_refs, out_refs, (ssem, rsem) = refs[:nb], refs[nb:2 * nb], refs[2 * nb:]
        x, y, c = _pos()
        sends, lands = [], []
        for b, (f_ref, out_ref) in enumerate(zip(f_refs, out_refs)):
            sends.append(_remote(f_ref, out_ref.at[c], ssem.at[b], rsem.at[b], (x, y, 1 - c)))
            land = out_ref.at[1 - c]
            lands.append(_remote(land, land, ssem.at[b], rsem.at[b], (x, y, 1 - c)))
        for cp in sends:
            cp.start()
        for cp in lands:
            cp.wait_recv()
        for cp in sends:
            cp.wait_send()

    got = pl.pallas_call(
        body, name="rs_pair_share", in_specs=[ANY] * nb, out_specs=[ANY] * nb,
        out_shape=[jax.ShapeDtypeStruct((2,) + f.shape, f.dtype) for f in fs],
        scratch_shapes=[pltpu.SemaphoreType.DMA((nb,))] * 2,
    )(*fs)
    return [lax.dynamic_update_slice(g, f[None], (lax.axis_index("c"), 0, 0)) for g, f in zip(got, fs)]


def _all_reduce_small(s):
    r, C = s.shape

    def body(s_ref, o_ref, buf, ssem, rsem):
        x, y, c = _pos()
        me = 4 * x + 2 * y + c
        buf[me] = s_ref[...]
        cps = []
        for m in range(1, N_DEV):
            mx, my, mc = (m >> 2) & 1, (m >> 1) & 1, m & 1
            peer = (x ^ mx, y ^ my, c ^ mc)
            cp = _remote(s_ref, buf.at[me], ssem.at[m - 1], rsem.at[m - 1], peer)
            cp.start()
            cps.append(cp)
        for m in range(1, N_DEV):
            mx, my, mc = (m >> 2) & 1, (m >> 1) & 1, m & 1
            src = 4 * (x ^ mx) + 2 * (y ^ my) + (c ^ mc)
            _remote(s_ref, buf.at[src], ssem.at[m - 1], rsem.at[m - 1], (x ^ mx, y ^ my, c ^ mc)).wait_recv()
        for cp in cps:
            cp.wait_send()
        acc = buf[0]
        for j in range(1, N_DEV):
            acc = acc + buf[j]
        o_ref[...] = acc

    return pl.pallas_call(
        body, name="all_reduce_small", in_specs=[pl.BlockSpec(memory_space=pltpu.VMEM)],
        out_specs=pl.BlockSpec(memory_space=pltpu.VMEM), out_shape=jax.ShapeDtypeStruct((r, C), F32),
        scratch_shapes=[pltpu.VMEM((N_DEV, r, C), F32), pltpu.SemaphoreType.DMA((N_DEV - 1,)), pltpu.SemaphoreType.DMA((N_DEV - 1,))],
    )(s)


def _rtile(n, pref):
    if n <= pref:
        return n
    t = (pref // 16) * 16
    while t >= 16:
        if n % t == 0:
            return t
        t -= 16
    raise ValueError(f"no row tile for {n}")


def _rs_pair_sums(gpks):
    gots = _rs_pair_exchange(gpks)
    out = []
    for gpk, got in zip(gpks, gots):
        _, R, C = gpk.shape
        H = R // 2
        own = lax.dynamic_index_in_dim(gpk.reshape(N_CHIPS, 2, H, C), lax.axis_index("c"), axis=1, keepdims=False)
        (part,) = _rows(lambda i, n, a, b: (a.astype(F32) + b.astype(F32),), N_CHIPS * H, _rtile(N_CHIPS * H, 512),
                        [_cur(own.reshape(N_CHIPS * H, C)), _cur(got.reshape(N_CHIPS * H, C))], [], [_out(C, BF16)], [], "rs_pair_add")
        out.append(part.reshape(N_CHIPS, H, C))
    return out


def _rs_chip_sums(parts):
    def add4(i, n, a, b, c, d):
        return (((a.astype(F32) + b.astype(F32)) + c.astype(F32)) + d.astype(F32),)

    out = []
    for p in parts:
        _, H, C = p.shape
        tm = _rtile(H, 1024)
        (red,) = _rows(add4, H, tm, [(p.reshape(N_CHIPS * H, C), C, functools.partial(_const, v=0), j * (H // tm)) for j in range(N_CHIPS)],
                       [], [_out(C, F32)], [], "rs_chip_add")
        out.append(red)
    return out


class _Exchange:
    def __init__(self, a):
        self.a = a
        self.axis = {n: (1 if n in TRANSPOSED else ax) for n, ax in SHARDED}
        shapes = {n: (1,) + tuple(self.packed(n, a[n]).shape[1:]) for n in self.axis}
        widths = lambda names: shapes[names[0]][-1] if names[0] == "ffn_w_up" else PACK_COLS
        self.layouts = {"early": [_layout(shapes, ns, widths(ns)) for ns in EARLY], "late": [_layout(shapes, ns, widths(ns)) for ns in LATE]}
        self.reduced = {}

    @staticmethod
    def packed(n, w):
        return jnp.swapaxes(w, -1, -2) if n in TRANSPOSED else w

    def shard(self, l, group):
        def piece(n, li, lo):
            w = self.packed(n, self.a[n][l:l + 1] if li is None else self.a[n][l])
            return w - w.astype(BF16).astype(F32) if lo else w
        return [_pack(lay, piece, BF16) for lay in self.layouts[group]]

    def weights(self, gathered, group):
        W, resid = {}, {}
        for (width, rows, ents), g in zip(self.layouts[group], gathered):
            for n, li, lo, ps, off, r in ents:
                parts = _unslab(g[:, off:off + r], ps, lead=1)
                ax = self.axis[n] + (1 if li is None else 0)
                full = jnp.moveaxis(parts, 0, ax - 1)
                full = full.reshape(full.shape[:ax - 1] + (-1,) + full.shape[ax + 1:])
                (resid if lo else W)[n] = full[0] if li is None else full
        for n in resid:
            W[n] = W[n].astype(F32) + resid[n].astype(F32)
        return W

    def submit(self, jobs):
        def by_chip(g, ax, parts=N_CHIPS):
            g = g.reshape(g.shape[:ax] + (parts, g.shape[ax] // parts) + g.shape[ax + 1:])
            return jnp.moveaxis(g, ax, 0)

        def pieces_of(GW):
            def piece(n, li, lo):
                if lo:
                    return None
                g = GW[n]
                if isinstance(g, tuple):
                    return jnp.concatenate([by_chip(h, self.axis[n] - 1, N_CHIPS // 2) for h in g])
                return by_chip(g[None], self.axis[n]) if li is None else by_chip(g, self.axis[n] - 1)
            return piece

        sums = _rs_pair_sums([_pack(lay, pieces_of(GW), BF16, lead=1) for GW, group in jobs for lay in self.layouts[group]])
        out, at = [], 0
        for _, group in jobs:
            out.append(sums[at:at + len(self.layouts[group])])
            at += len(self.layouts[group])
        return out

    def collect(self, l, group, parts):
        self.reduced[l, group] = _rs_chip_sums(parts)

    def finish(self):
        keys = [(l, g) for l in range(DEPTH) for g in self.layouts]
        flat = _rs_pair_share([f for key in keys for f in self.reduced[key]])
        both, at = {}, 0
        for key in keys:
            both[key] = flat[at:at + len(self.layouts[key[1]])]
            at += len(self.layouts[key[1]])
        grads = {}
        for group, lays in self.layouts.items():
            for b, (width, rows, ents) in enumerate(lays):
                for n, li, lo, ps, off, r in ents:
                    if not lo:
                        per_layer = [self.packed(n, _unslab(both[l, group][b].reshape(rows, width)[off:off + r], ps)) for l in range(DEPTH)]
                        grads[n] = jnp.concatenate(per_layer) if li is None else jnp.stack(per_layer)
        return grads


def _adam(w, g, m, v, name, g_row=0):
    shp = w.shape
    two = lambda a: a.reshape(-1, shp[-1])
    rows = math.prod(shp[:-1])
    tm = _rtile(rows, 256)
    assert g_row % tm == 0
    g_in = (two(g), shp[-1], functools.partial(_const, v=0), g_row // tm)
    res = _rows(_k_adam, rows, tm, [_cur(two(w)), g_in, _cur(two(m)), _cur(two(v))], [], [_out(shp[-1], F32)] * 4, [], name)
    return tuple(r.reshape(shp) for r in res)


def _pack_flat(parts, rows):
    flat = jnp.concatenate([p.astype(F32).reshape(-1) for p in parts])
    return jnp.pad(flat, (0, rows * PACK_COLS - flat.shape[0])).reshape(rows, PACK_COLS)


def _unpack_flat(buf, shapes):
    flat, out, off = buf.reshape(-1), [], 0
    for shp in shapes:
        n = math.prod(shp)
        out.append(flat[off:off + n].reshape(shp))
        off += n
    return out


def kernel(x, positions, norm_mix_pre, norm_mix_post, norm_ffn_pre, norm_ffn_post, w_in, mla_q_norm, mla_w_q_up, mla_kv_norm, mla_w_kv_up, sc_conv_w, ssd_conv_w, ssd_conv_b, ssd_dt_bias, ssd_a_log, ssd_d, ssd_norm, w_out, ffn_w_up, ffn_conv_w, ffn_conv_b, ffn_w_down, loss_target, m_norm_mix_pre, m_norm_mix_post, m_norm_ffn_pre, m_norm_ffn_post, m_w_in, m_mla_q_norm, m_mla_w_q_up, m_mla_kv_norm, m_mla_w_kv_up, m_sc_conv_w, m_ssd_conv_w, m_ssd_conv_b, m_ssd_dt_bias, m_ssd_a_log, m_ssd_d, m_ssd_norm, m_w_out, m_ffn_w_up, m_ffn_conv_w, m_ffn_conv_b, m_ffn_w_down, v_norm_mix_pre, v_norm_mix_post, v_norm_ffn_pre, v_norm_ffn_post, v_w_in, v_mla_q_norm, v_mla_w_q_up, v_mla_kv_norm, v_mla_w_kv_up, v_sc_conv_w, v_ssd_conv_w, v_ssd_conv_b, v_ssd_dt_bias, v_ssd_a_log, v_ssd_d, v_ssd_norm, v_w_out, v_ffn_w_up, v_ffn_conv_w, v_ffn_conv_b, v_ffn_w_down):
    a = dict(locals())
    ex = _Exchange(a)
    S = {n: a[n] for n in SMALL}
    loss_part, gx, _, GS = _local_step(a["x"][0], a["positions"][0], a["loss_target"][0], None, S, ex)

    grads, delta, new_m, new_v = {}, {}, {}, {}
    for n, g in ex.finish().items():
        grads[n], delta[n], new_m[n], new_v[n] = _adam(a[n], g, a["m_" + n], a["v_" + n], "adamw_" + n)

    small_shapes = [a[n].shape for n in SMALL]
    rs = -(-(sum(math.prod(s) for s in small_shapes) + 1) // (PACK_COLS * SLAB_ALIGN)) * SLAB_ALIGN
    red = _all_reduce_small(_pack_flat([GS[n] for n in SMALL] + [loss_part.reshape(1)], rs))
    loss = _unpack_flat(red, small_shapes + [(1,)])[-1][0]
    pk = lambda pre: _pack_flat([a[pre + n] for n in SMALL], rs)
    for dst, buf in zip((grads, delta, new_m, new_v), _adam(pk(""), red, pk("m_"), pk("v_"), "adamw_small")):
        dst.update(zip(SMALL, _unpack_flat(buf, small_shapes)))

    return (loss, gx[None], *[grads[n] for n in WEIGHTS], *[delta[n] for n in WEIGHTS], *[new_m[n] for n in WEIGHTS],
            *[new_v[n] for n in WEIGHTS])
```

```python
import functools
import math

import jax
import jax.numpy as jnp
from jax import lax
from jax.experimental import pallas as pl
from jax.experimental.pallas import tpu as pltpu

F32 = jnp.float32
BF16 = jnp.bfloat16
MXU_DTYPE = jnp.bfloat16
HIGHEST = lax.Precision.HIGHEST
MESH = pl.DeviceIdType.MESH

D_MODEL = 1024
DEPTH = 4
HEADS = 8
Q_LORA = 256
KV_LORA = 128
NOPE = 64
ROPE = 32
VDIM = 64
ROPE_THETA = 10000.0
SC_DIM = 256
SSD_HEADS = 4
SSD_HEAD_DIM = 64
SSD_STATE = 128
SSD_DIM = 256
SSD_CONV_DIM = 768
SSD_CHUNK = 128
FFN_DIM = 2816
NORM_EPS = 1e-6
QK_SCALE = (NOPE + ROPE) ** -0.5
LANE = 128
HP = 128
FLASH_HEADS = 4
FLASH_HEADS_FWD = 8
FLASH_BLOCK = 512

ZIN = 2560
Z_CQ, Z_CKV, Z_KR, Z_SCB, Z_SCC, Z_SCH, Z_SSZ, Z_XBC, Z_DT = 0, 256, 384, 512, 768, 1024, 1280, 1536, 2304
KR_LANE = 64
FFN_TILE = 256
FFN_ROWS = 2048
ROW_BLOCK = 512

ADAM_LR, ADAM_B1, ADAM_B2, ADAM_EPS, ADAM_WD, ADAM_STEP = 0.001, 0.9, 0.999, 1e-08, 0.01, 10

PACK_COLS = 1024


def _tile(n, pref):
    if n <= pref:
        return n
    t = (pref // LANE) * LANE
    while t >= LANE:
        if n % t == 0:
            return t
        t -= LANE
    raise ValueError(f"no tile for {n}")


MM_TM, MM_TN, MM_TK = 1024, 1408, 1536


def _mm(a, b, mode, out_dtype, name, tm=None, tn=MM_TN, tkmax=MM_TK):
    pair = isinstance(a, tuple)
    a_list = list(a) if pair else [a]
    layer = None
    if isinstance(b, tuple):
        b, layer = b
    bshape = b.shape[-2:]
    if mode == "nn":
        (M, Ka), (_, N) = a_list[0].shape, bshape
    elif mode == "nt":
        (M, Ka), (N, _) = a_list[0].shape, bshape
    else:
        (Ka, M), (_, N) = a_list[0].shape, bshape
    tm = (MM_TN if mode == "tn" else MM_TM) if tm is None else tm
    tm, tn, tk = _tile(M, tm), _tile(N, tn), _tile(Ka, tkmax)
    nka = Ka // tk
    nk = nka * len(a_list)

    def bspec(shape, index):
        if layer is None:
            return pl.BlockSpec(shape, index)
        return pl.BlockSpec((None,) + shape, lambda i, j, k: (layer,) + index(i, j, k))

    if mode == "nn":
        a_specs = [pl.BlockSpec((tm, tk), lambda i, j, k: (i, jnp.minimum(k, nka - 1))),
                   pl.BlockSpec((tm, tk), lambda i, j, k: (i, jnp.maximum(k - nka, 0)))][:len(a_list)]
        b_spec = bspec((tk, tn), lambda i, j, k: (k, j))
        dims = NN
    elif mode == "nt":
        a_specs = [pl.BlockSpec((tm, tk), lambda i, j, k: (i, jnp.minimum(k, nka - 1))),
                   pl.BlockSpec((tm, tk), lambda i, j, k: (i, jnp.maximum(k - nka, 0)))][:len(a_list)]
        b_spec = bspec((tn, tk), lambda i, j, k: (j, k))
        dims = NT
    else:
        a_specs = [pl.BlockSpec((tk, tm), lambda i, j, k: (k, i))]
        b_spec = pl.BlockSpec((tk, tn), lambda i, j, k: (k, j))
        dims = TN
    na = len(a_list)

    def body(*refs):
        a_refs, b_ref, o_ref = refs[:na], refs[na], refs[na + 1]
        k = pl.program_id(2)

        def prod(a_ref):
            return lax.dot_general(a_ref[...].astype(MXU_DTYPE), b_ref[...].astype(MXU_DTYPE), dims, preferred_element_type=F32)

        if nk == 1:
            o_ref[...] = prod(a_refs[0]).astype(o_ref.dtype)
            return
        acc_ref = refs[na + 2]

        @pl.when(k == 0)
        def _():
            acc_ref[...] = prod(a_refs[0])

        @pl.when((k > 0) & (k < nka))
        def _():
            acc_ref[...] += prod(a_refs[0])

        if pair:
            @pl.when(k >= nka)
            def _():
                acc_ref[...] += prod(a_refs[1])

        @pl.when(k == nk - 1)
        def _():
            o_ref[...] = acc_ref[...].astype(o_ref.dtype)

    return pl.pallas_call(
        body, name=name, grid=(M // tm, N // tn, nk),
        in_specs=a_specs + [b_spec], out_specs=pl.BlockSpec((tm, tn), lambda i, j, k: (i, j)),
        out_shape=jax.ShapeDtypeStruct((M, N), out_dtype),
        scratch_shapes=[pltpu.VMEM((tm, tn), F32)] if nk > 1 else [],
        compiler_params=pltpu.CompilerParams(dimension_semantics=("parallel", "parallel", "arbitrary")),
    )(*a_list, b)


HALO = 8


def _const(j, v):
    return v


def _rows(fn, T, tm, ins, consts, outs, accs, name, ncol=1):
    n = T // tm
    hb = tm // HALO
    last = T // HALO - 1
    in_specs, args = [], []
    for arr, bc, cb, kind in ins:
        if isinstance(kind, int):
            in_specs.append(pl.BlockSpec((tm, bc), lambda j, i, cb=cb, off=kind: (i + off, cb(j))))
        elif kind == "cur":
            in_specs.append(pl.BlockSpec((tm, bc), lambda j, i, cb=cb: (i, cb(j))))
        elif kind == "prev":
            in_specs.append(pl.BlockSpec((HALO, bc), lambda j, i, cb=cb: (jnp.maximum(i * hb - 1, 0), cb(j))))
        else:
            in_specs.append(pl.BlockSpec((HALO, bc), lambda j, i, cb=cb: (jnp.minimum((i + 1) * hb, last), cb(j))))
        args.append(arr)
    for arr, bc, cb in consts:
        in_specs.append(pl.BlockSpec((arr.shape[0], bc), lambda j, i, cb=cb: (0, cb(j))))
        args.append(arr)
    out_specs, out_shape = [], []
    for tc, dt, bc, cb in outs:
        out_specs.append(pl.BlockSpec((tm, bc), lambda j, i, cb=cb: (i, cb(j))))
        out_shape.append(jax.ShapeDtypeStruct((T, tc), dt))
    for r, tc, bc, cb in accs:
        out_specs.append(pl.BlockSpec((r, bc), lambda j, i, cb=cb: (0, cb(j))))
        out_shape.append(jax.ShapeDtypeStruct((r, tc), F32))
    nin, nout, nacc = len(args), len(outs), len(accs)

    def body(*refs):
        i = pl.program_id(1)
        res = fn(i, n, *[r[...] for r in refs[:nin]])
        for r, v in zip(refs[nin:nin + nout], res[:nout]):
            r[...] = v.astype(r.dtype)
        if nacc:
            acc_refs = refs[nin + nout:nin + nout + nacc]

            @pl.when(i == 0)
            def _():
                for r in acc_refs:
                    r[...] = jnp.zeros_like(r)

            for r, v in zip(acc_refs, res[nout:]):
                r[...] += v.astype(F32)

    res = pl.pallas_call(
        body, name=name, grid=(ncol, n), in_specs=in_specs, out_specs=out_specs, out_shape=out_shape,
        compiler_params=pltpu.CompilerParams(dimension_semantics=("arbitrary", "arbitrary")),
    )(*args)
    return res


def _cur(arr, bc=None, blk=0):
    bc = arr.shape[1] if bc is None else bc
    return (arr, bc, functools.partial(_const, v=blk), "cur")


def _halo(arr, kind, bc=None, blk=0):
    bc = arr.shape[1] if bc is None else bc
    return (arr, bc, functools.partial(_const, v=blk), kind)


def _cst(arr):
    return (arr, arr.shape[1], functools.partial(_const, v=0))


def _out(cols, dt):
    return (cols, dt, cols, functools.partial(_const, v=0))


def _acc(rows, cols):
    return (rows, cols, cols, functools.partial(_const, v=0))


def _rms(x, w):
    return x * lax.rsqrt(jnp.mean(x * x, axis=-1, keepdims=True) + NORM_EPS) * w


def _sigmoid(x):
    return 0.5 * jnp.tanh(0.5 * x) + 0.5


def _silu(x):
    return x * _sigmoid(x)


def _dsilu(x):
    s = _sigmoid(x)
    return s * (1.0 + x * (1.0 - s))


def _softplus(x):
    return jnp.maximum(x, 0.0) + jnp.log1p(jnp.exp(-jnp.abs(x)))


def _shift(a, k):
    return pltpu.roll(a, k % a.shape[0], 0)


def _lroll(a, k):
    return pltpu.roll(a, k % a.shape[1], 1)


def _vjp_wrap(f, nrow, nconst, add_first=False):
    def g(i, n, *vals):
        rows, consts, mid = vals[:nrow], vals[len(vals) - nconst:], vals[nrow:len(vals) - nconst]
        cots = mid[:-1] if add_first else mid
        outs, pull = jax.vjp(f, *rows, *consts)
        grads = list(pull(tuple(c.astype(o.dtype) for c, o in zip(cots, outs))))
        if add_first:
            grads[0] = grads[0] + mid[-1]
        return tuple(grads)
    return g


def _rows_vjp(f, T, tm, rows, consts, cots, out_dtypes, name):
    return _rows(_vjp_wrap(f, len(rows), len(consts)), T, tm, [_cur(r) for r in rows] + [_cur(c) for c in cots],
                 [_cst(c) for c in consts], [_out(r.shape[1], dt) for r, dt in zip(rows, out_dtypes)],
                 [_acc(1, c.shape[1]) for c in consts], name)


def _f_premix(x, g):
    return (_rms(x, g),)


def _f_mla_pre(cq, ckv, qn, kvn):
    return _rms(cq, qn), _rms(ckv, kvn)


def _f_ssd_gate(y, z, nw):
    return (_rms(y * _silu(z), nw),)


def _f_post_mix(x, mixed, gpost, gffn):
    x1 = x + _rms(mixed, gpost)
    return x1, _rms(x1, gffn)


def _f_post_ffn(x1, d, gpost):
    return (x1 + _rms(d, gpost),)


def _rope_fwd(v, cosf, sina, sinb):
    return v * cosf + _lroll(v, -16) * sina + _lroll(v, 16) * sinb


def _rope_bwd(g, cosf, sina, sinb):
    return g * cosf + _lroll(g * sina, 16) + _lroll(g * sinb, -16)


def _k_rope_fwd(i, n, qpad, kvpad, kr, cosf, sina, sinb):
    qs, ks = [], []
    krr = _rope_fwd(kr, cosf, sina, sinb)
    for h in range(HEADS):
        sl = slice(h * HP, (h + 1) * HP)
        qs.append(_rope_fwd(qpad[:, sl], cosf, sina, sinb))
        ks.append(kvpad[:, sl].astype(F32) + krr)
    return jnp.concatenate(qs, axis=1), jnp.concatenate(ks, axis=1)


def _k_rope_bwd(i, n, dq, dk, dv, cosf, sina, sinb):
    lane = lax.broadcasted_iota(jnp.int32, (1, HP), 1)
    rmask = ((lane >= KR_LANE) & (lane < KR_LANE + ROPE)).astype(F32)
    dqs, dks = [], []
    dkr = jnp.zeros((dq.shape[0], HP), F32)
    for h in range(HEADS):
        sl = slice(h * HP, (h + 1) * HP)
        dqs.append(_rope_bwd(dq[:, sl], cosf, sina, sinb))
        dkh = dk[:, sl]
        dkr = dkr + dkh * rmask
        dks.append(dkh * (1.0 - rmask))
    dkr = _rope_bwd(dkr, cosf, sina, sinb) * rmask
    return jnp.concatenate(dqs, axis=1), jnp.concatenate(dks + [dv], axis=1), dkr


def _k_sconv_fwd(i, n, b, c, h, cp, hp, w):
    m = b.shape[0]
    up = jnp.where(i > 0, cp * hp, 0.0)
    ue = jnp.concatenate([up, c * h], axis=0)
    conv = w[2:3] * ue + w[1:2] * _shift(ue, 1) + w[0:1] * _shift(ue, 2)
    return (b * conv[HALO:],)


def _k_sconv_bwd(i, n, b, c, h, dy, cp, hp, bn, dyn, w):
    m = b.shape[0]
    up = jnp.where(i > 0, cp * hp, 0.0)
    ue = jnp.concatenate([up, c * h], axis=0)
    u1, u2 = _shift(ue, 1), _shift(ue, 2)
    conv = (w[2:3] * ue + w[1:2] * u1 + w[0:1] * u2)[HALO:]
    dc_cur = dy * b
    dce = jnp.concatenate([dc_cur, jnp.where(i < n - 1, dyn * bn, 0.0)], axis=0)
    du = (w[2:3] * dce + w[1:2] * _shift(dce, -1) + w[0:1] * _shift(dce, -2))[:m]
    dw = jnp.concatenate([
        jnp.sum(dc_cur * u2[HALO:], axis=0, keepdims=True),
        jnp.sum(dc_cur * u1[HALO:], axis=0, keepdims=True),
        jnp.sum(dc_cur * ue[HALO:], axis=0, keepdims=True),
        jnp.zeros((HALO - 3, b.shape[1]), F32)], axis=0)
    return dy * conv, du * h, du * c, dw


def _conv4(ue, w):
    return w[3:4] * ue + w[2:3] * _shift(ue, 1) + w[1:2] * _shift(ue, 2) + w[0:1] * _shift(ue, 3)


def _k_ssdconv_fwd(i, n, u, up, w, bias):
    ue = jnp.concatenate([jnp.where(i > 0, up, 0.0), u], axis=0)
    return (_silu(_conv4(ue, w)[HALO:] + bias),)


def _k_ssdconv_bwd(i, n, u, dout, up, un, doutn, w, bias):
    m = u.shape[0]
    ue = jnp.concatenate([jnp.where(i > 0, up, 0.0), u, un], axis=0)
    u1, u2, u3 = _shift(ue, 1), _shift(ue, 2), _shift(ue, 3)
    pre = (w[3:4] * ue + w[2:3] * u1 + w[1:2] * u2 + w[0:1] * u3)[HALO:] + bias
    doe = jnp.concatenate([dout, jnp.where(i < n - 1, doutn, 0.0)], axis=0)
    dpre = doe * _dsilu(pre)
    du = (w[3:4] * dpre + w[2:3] * _shift(dpre, -1) + w[1:2] * _shift(dpre, -2) + w[0:1] * _shift(dpre, -3))[:m]
    dp = dpre[:m]
    cur = slice(HALO, HALO + m)
    dw = jnp.concatenate([
        jnp.sum(dp * u3[cur], axis=0, keepdims=True),
        jnp.sum(dp * u2[cur], axis=0, keepdims=True),
        jnp.sum(dp * u1[cur], axis=0, keepdims=True),
        jnp.sum(dp * ue[cur], axis=0, keepdims=True),
        jnp.zeros((HALO - 4, u.shape[1]), F32)], axis=0)
    db = jnp.sum(dp, axis=0, keepdims=True)
    return du, dw, db


def _conv3(ue, w):
    return w[2:3] * ue + w[1:2] * _shift(ue, 1) + w[0:1] * _shift(ue, 2)


def _k_ffnact_fwd(i, n, ug, uu, ugp, uup, wg, wu, bg, bu):
    gate = _conv3(jnp.concatenate([jnp.where(i > 0, ugp, 0.0), ug], axis=0), wg)[HALO:] + bg
    upv = _conv3(jnp.concatenate([jnp.where(i > 0, uup, 0.0), uu], axis=0), wu)[HALO:] + bu
    return (_silu(gate) * upv,)


def _k_ffnact_bwd(i, n, ug, uu, dact, ugp, uup, ugn, uun, dactn, wg, wu, bg, bu):
    m = ug.shape[0]
    cur = slice(HALO, HALO + m)

    def taps(p, c, nx):
        e = jnp.concatenate([jnp.where(i > 0, p, 0.0), c, nx], axis=0)
        return e, _shift(e, 1), _shift(e, 2)

    def back(d, w):
        return (w[2:3] * d + w[1:2] * _shift(d, -1) + w[0:1] * _shift(d, -2))[:m]

    def wgrad(d, t):
        return jnp.concatenate([jnp.sum(d[:m] * t[2][cur], axis=0, keepdims=True), jnp.sum(d[:m] * t[1][cur], axis=0, keepdims=True),
                                jnp.sum(d[:m] * t[0][cur], axis=0, keepdims=True), jnp.zeros((HALO - 3, d.shape[1]), F32)], axis=0)

    tg, tu = taps(ugp, ug, ugn), taps(uup, uu, uun)
    gate = (wg[2:3] * tg[0] + wg[1:2] * tg[1] + wg[0:1] * tg[2])[HALO:] + bg
    upv = (wu[2:3] * tu[0] + wu[1:2] * tu[1] + wu[0:1] * tu[2])[HALO:] + bu
    dae = jnp.concatenate([dact, jnp.where(i < n - 1, dactn, 0.0)], axis=0)
    sg = _sigmoid(gate)
    dg = dae * upv * (sg * (1.0 + gate * (1.0 - sg)))
    dup = dae * (gate * sg)
    return (back(dg, wg), back(dup, wu), wgrad(dg, tg), wgrad(dup, tu),
            jnp.sum(dg[:m], axis=0, keepdims=True), jnp.sum(dup[:m], axis=0, keepdims=True))


def _k_loss(i, n, y, tgt):
    e = y - tgt
    part = 0.5 * jnp.sum(jnp.sum(e * e, axis=1, keepdims=True) / D_MODEL, axis=0, keepdims=True)
    return e * (1.0 / D_MODEL), jnp.broadcast_to(part, (1, LANE))


def _k_adam(i, n, w, g, m, v):
    m = ADAM_B1 * m + (1.0 - ADAM_B1) * g
    v = ADAM_B2 * v + (1.0 - ADAM_B2) * (g * g)
    m_hat = m / (1.0 - ADAM_B1 ** ADAM_STEP)
    v_hat = v / (1.0 - ADAM_B2 ** ADAM_STEP)
    delta = -ADAM_LR * (m_hat / (jnp.sqrt(v_hat) + ADAM_EPS) + ADAM_WD * w)
    return g, delta, m, v


def _dotf(a, b, dims):
    return lax.dot_general(a.astype(MXU_DTYPE), b.astype(MXU_DTYPE), dims, preferred_element_type=F32)


NN = (((1,), (0,)), ((), ()))
NT = (((1,), (1,)), ((), ()))
TN = (((0,), (0,)), ((), ()))


def _ssd_chunk(x0, x1, x2, x3, b0, b1, c0, c1, dtraw, p0, p1, p2, p3, dtb, alog, dsk):
    xs, bs, cs_, ps = (x0, x1, x2, x3), (b0, b1), (c0, c1), (p0, p1, p2, p3)
    L = dtraw.shape[0]
    dt = _softplus(dtraw + dtb)
    adt = dt * (-jnp.exp(alog))
    row = lax.broadcasted_iota(jnp.int32, (L, L), 0)
    col = lax.broadcasted_iota(jnp.int32, (L, L), 1)
    tril = row >= col
    cum = jnp.dot(tril.astype(F32), adt, precision=HIGHEST, preferred_element_type=F32)
    cum_t = cum.T
    lane = lax.broadcasted_iota(jnp.int32, (1, LANE), 1)
    sub = lax.broadcasted_iota(jnp.int32, (LANE, 1), 0)
    lastcol = (lax.broadcasted_iota(jnp.int32, (1, L), 1) == L - 1).astype(F32)
    ys, news = [], []
    for h in range(SSD_HEADS):
        g = h // (SSD_HEADS // 2)
        oh = (lane == h).astype(F32)
        dth = jnp.sum(dt * oh, axis=1, keepdims=True)
        csh = jnp.sum(cum * oh, axis=1, keepdims=True)
        csr = jnp.sum(cum_t * (sub == h).astype(F32), axis=0, keepdims=True)
        cl = jnp.sum(csr * lastcol, axis=1, keepdims=True)
        dskh = jnp.sum(dsk * oh, axis=1, keepdims=True)
        x, bm, cm, prev = xs[h], bs[g], cs_[g], ps[h]
        xdt = x * dth
        decay = jnp.exp(jnp.where(tril, csh - csr, -jnp.inf))
        scores = _dotf(cm, bm, NT) * decay
        y_diag = _dotf(scores, xdt, NN)
        bd = bm * jnp.exp(cl - csh)
        cst = _dotf(xdt, bd, TN)
        news.append(prev * jnp.exp(cl) + cst)
        y_off = _dotf(cm, prev, NT) * jnp.exp(csh)
        ys.append(y_diag + y_off + x * dskh)
    return (*ys, *news)


SSD_STEP = 2


def _ssd_operands(x_ref, dt_ref, par_ref, prev, rows):
    xs = [x_ref[rows, h * SSD_HEAD_DIM:(h + 1) * SSD_HEAD_DIM] for h in range(SSD_HEADS)]
    bs = [x_ref[rows, SSD_DIM + g * SSD_STATE:SSD_DIM + (g + 1) * SSD_STATE] for g in range(2)]
    cs_ = [x_ref[rows, SSD_DIM + 2 * SSD_STATE + g * SSD_STATE:SSD_DIM + 2 * SSD_STATE + (g + 1) * SSD_STATE] for g in range(2)]
    return (*xs, *bs, *cs_, dt_ref[rows, :], *prev, par_ref[0:1, :], par_ref[1:2, :], par_ref[2:3, :])


def _ssd_fwd(xbc, dtraw, par, T, dt_blk=0):
    L = SSD_CHUNK
    nc = T // L
    P = SSD_HEAD_DIM
    U = SSD_STEP if nc % SSD_STEP == 0 else 1

    def body(x_ref, dt_ref, par_ref, y_ref, st_ref, state):
        @pl.when(pl.program_id(0) == 0)
        def _():
            state[...] = jnp.zeros_like(state)

        for u in range(U):
            rows = slice(u * L, (u + 1) * L)
            st_ref[u] = state[...]
            prev = [state[h * P:(h + 1) * P, :] for h in range(SSD_HEADS)]
            res = _ssd_chunk(*_ssd_operands(x_ref, dt_ref, par_ref, prev, rows))
            for h in range(SSD_HEADS):
                y_ref[rows, h * P:(h + 1) * P] = res[h]
                state[h * P:(h + 1) * P, :] = res[SSD_HEADS + h]

    return pl.pallas_call(
        body, name="ssd_scan_fwd", grid=(nc // U,),
        in_specs=[pl.BlockSpec((U * L, SSD_CONV_DIM), lambda c: (c, 0)), pl.BlockSpec((U * L, LANE), lambda c: (c, dt_blk)),
                  pl.BlockSpec((8, LANE), lambda c: (0, 0))],
        out_specs=[pl.BlockSpec((U * L, SSD_DIM), lambda c: (c, 0)), pl.BlockSpec((U, SSD_DIM, SSD_STATE), lambda c: (c, 0, 0))],
        out_shape=[jax.ShapeDtypeStruct((T, SSD_DIM), F32), jax.ShapeDtypeStruct((nc, SSD_DIM, SSD_STATE), F32)],
        scratch_shapes=[pltpu.VMEM((SSD_DIM, SSD_STATE), F32)],
        compiler_params=pltpu.CompilerParams(dimension_semantics=("arbitrary",)),
    )(xbc, dtraw, par)


def _ssd_bwd(xbc, dtraw, par, states, dy, T, dt_blk=0):
    L = SSD_CHUNK
    nc = T // L
    P = SSD_HEAD_DIM
    U = SSD_STEP if nc % SSD_STEP == 0 else 1
    ns = nc // U

    def body(x_ref, dt_ref, par_ref, st_ref, dy_ref, dx_ref, ddt_ref, dpar_ref, dstate):
        @pl.when(pl.program_id(0) == 0)
        def _():
            dstate[...] = jnp.zeros_like(dstate)
            dpar_ref[...] = jnp.zeros_like(dpar_ref)

        for u in reversed(range(U)):
            rows = slice(u * L, (u + 1) * L)
            prev = [st_ref[u, h * P:(h + 1) * P, :] for h in range(SSD_HEADS)]
            prim = _ssd_operands(x_ref, dt_ref, par_ref, prev, rows)
            _, pull = jax.vjp(_ssd_chunk, *prim)
            cots = tuple(dy_ref[rows, h * P:(h + 1) * P] for h in range(SSD_HEADS)) + tuple(
                dstate[h * P:(h + 1) * P, :] for h in range(SSD_HEADS))
            g = pull(cots)
            for h in range(SSD_HEADS):
                dx_ref[rows, h * P:(h + 1) * P] = g[h]
                dstate[h * P:(h + 1) * P, :] = g[9 + h]
            for k in range(2):
                dx_ref[rows, SSD_DIM + k * SSD_STATE:SSD_DIM + (k + 1) * SSD_STATE] = g[4 + k]
                dx_ref[rows, SSD_DIM + 2 * SSD_STATE + k * SSD_STATE:SSD_DIM + 2 * SSD_STATE + (k + 1) * SSD_STATE] = g[6 + k]
            ddt_ref[rows, :] = g[8]
            for r in range(3):
                dpar_ref[r:r + 1, :] += g[13 + r]

    rev = lambda c: (ns - 1 - c, 0)
    return pl.pallas_call(
        body, name="ssd_scan_bwd", grid=(ns,),
        in_specs=[pl.BlockSpec((U * L, SSD_CONV_DIM), rev), pl.BlockSpec((U * L, LANE), lambda c: (ns - 1 - c, dt_blk)),
                  pl.BlockSpec((8, LANE), lambda c: (0, 0)),
                  pl.BlockSpec((U, SSD_DIM, SSD_STATE), lambda c: (ns - 1 - c, 0, 0)), pl.BlockSpec((U * L, SSD_DIM), rev)],
        out_specs=[pl.BlockSpec((U * L, SSD_CONV_DIM), rev), pl.BlockSpec((U * L, LANE), rev), pl.BlockSpec((8, LANE), lambda c: (0, 0))],
        out_shape=[jax.ShapeDtypeStruct((T, SSD_CONV_DIM), F32), jax.ShapeDtypeStruct((T, LANE), F32),
                   jax.ShapeDtypeStruct((8, LANE), F32)],
        scratch_shapes=[pltpu.VMEM((SSD_DIM, SSD_STATE), F32)],
        compiler_params=pltpu.CompilerParams(dimension_semantics=("arbitrary",)),
    )(xbc, dtraw, par, states, dy)


def _causal_pairs(nq, by_query):
    if by_query:
        pairs = [(i, j) for i in range(nq) for j in range(i + 1)]
    else:
        pairs = [(i, j) for j in range(nq) for i in range(j, nq)]
    return jnp.asarray([p[0] for p in pairs], jnp.int32), jnp.asarray([p[1] for p in pairs], jnp.int32)


def _flash_fwd(q, k, kv, T, carry=()):
    tq = tk = min(FLASH_BLOCK, T)
    nq = T // tq
    G = FLASH_HEADS_FWD
    rep = tk // HP
    nc = len(carry)
    qi, kj = _causal_pairs(nq, by_query=True)
    nh, nt = HEADS // G, qi.shape[0]

    def body(qi_ref, kj_ref, q_ref, k_ref, v_ref, *rest):
        w_refs, o_ref, g_refs = rest[:nc], rest[nc], rest[nc + 1:2 * nc + 1]
        m_ref, l_ref, acc_ref = rest[2 * nc + 1:2 * nc + 4]
        h, t = pl.program_id(0), pl.program_id(1)
        i, j = qi_ref[t], kj_ref[t]
        if nc:
            plan = lambda: _ag_plan(w_refs, g_refs, rest[2 * nc + 4:])

            @pl.when((h == 0) & (t == 0))
            def _():
                for cp in plan()[0]:
                    cp.start()

        @pl.when(j == 0)
        def _():
            m_ref[...] = jnp.full_like(m_ref, -jnp.inf)
            l_ref[...] = jnp.zeros_like(l_ref)
            acc_ref[...] = jnp.zeros_like(acc_ref)

        def step(diagonal):
            for g in range(G):
                sl = slice(g * HP, (g + 1) * HP)
                s = _dotf(q_ref[:, sl], k_ref[:, sl], NT) * QK_SCALE
                if diagonal:
                    rows = lax.broadcasted_iota(jnp.int32, (tq, tk), 0)
                    cols = lax.broadcasted_iota(jnp.int32, (tq, tk), 1)
                    s = jnp.where(rows >= cols, s, -jnp.inf)
                m_old = m_ref[:, sl]
                m_new = jnp.maximum(m_old, jnp.max(s, axis=1, keepdims=True))
                p = jnp.exp(s - jnp.tile(m_new, (1, rep)))
                alpha = jnp.exp(m_old - m_new)
                l_ref[:, sl] = alpha * l_ref[:, sl] + jnp.sum(p, axis=1, keepdims=True)
                acc_ref[:, sl] = alpha * acc_ref[:, sl] + _dotf(p, v_ref[:, sl], NN)
                m_ref[:, sl] = m_new

        @pl.when(j < i)
        def _():
            step(False)

        @pl.when(j == i)
        def _():
            step(True)
            lane = lax.broadcasted_iota(jnp.int32, (tq, HP), 1)
            for g in range(G):
                sl = slice(g * HP, (g + 1) * HP)
                l = l_ref[:, sl]
                o_ref[:, sl] = jnp.where(lane < VDIM, acc_ref[:, sl] / l, m_ref[:, sl] + jnp.log(l))

        if nc:
            @pl.when(h * nt + t == (3 * nh * nt) // 4)
            def _():
                _, lands, forwards, _ = plan()
                for land, fw in zip(lands, forwards):
                    land.wait_recv()
                    fw.start()

            @pl.when((h == nh - 1) & (t == nt - 1))
            def _():
                sends, _, forwards, finals = plan()
                for cp in finals:
                    cp.wait_recv()
                for cp in sends + forwards:
                    cp.wait_send()

    W = G * HP
    res = pl.pallas_call(
        body, name="mla_flash_fwd",
        grid_spec=pltpu.PrefetchScalarGridSpec(
            num_scalar_prefetch=2, grid=(nh, nt),
            in_specs=[pl.BlockSpec((tq, W), lambda h, t, qi, kj: (qi[t], h)),
                      pl.BlockSpec((tk, W), lambda h, t, qi, kj: (kj[t], h)),
                      pl.BlockSpec((tk, W), lambda h, t, qi, kj: (kj[t], HEADS // G + h))] + [ANY] * nc,
            out_specs=[pl.BlockSpec((tq, W), lambda h, t, qi, kj: (qi[t], h))] + [ANY] * nc,
            scratch_shapes=[pltpu.VMEM((tq, W), F32), pltpu.VMEM((tq, W), F32), pltpu.VMEM((tq, W), F32)] + (_ag_sems(nc) if nc else [])),
        out_shape=[jax.ShapeDtypeStruct((T, HEADS * HP), F32)] + [jax.ShapeDtypeStruct((N_CHIPS,) + w.shape, w.dtype) for w in carry],
        compiler_params=pltpu.CompilerParams(dimension_semantics=("arbitrary", "arbitrary")),
    )(qi, kj, q, k, kv, *carry)
    return res[0] if not nc else (res[0], [_own_slot(g, w) for g, w in zip(res[1:], carry)])


def _flash_bwd(q, k, kv, o, dycat, T, carry=()):
    tq = tk = min(FLASH_BLOCK, T)
    nq = T // tq
    G = FLASH_HEADS
    nc = len(carry)
    qi, kj = _causal_pairs(nq, by_query=False)
    nh, nt = HEADS // G, qi.shape[0]

    def body(qi_ref, kj_ref, q_ref, k_ref, v_ref, o_ref, do_ref, *rest):
        p_refs, (dq_ref, dk_ref, dv_ref), part_refs = rest[:nc], rest[nc:nc + 3], rest[nc + 3:2 * nc + 3]
        h, t = pl.program_id(0), pl.program_id(1)
        i, j = qi_ref[t], kj_ref[t]
        if nc:
            plan = lambda: _chip_plan(p_refs, part_refs, rest[2 * nc + 3:])

            @pl.when((h == 0) & (t == 0))
            def _():
                for cp in plan()[0]:
                    cp.start()

        @pl.when(t == 0)
        def _():
            dq_ref[...] = jnp.zeros_like(dq_ref)

        @pl.when(i == j)
        def _():
            dk_ref[...] = jnp.zeros_like(dk_ref)
            dv_ref[...] = jnp.zeros_like(dv_ref)

        def step(diagonal):
            r0 = pl.multiple_of(i * tq, tq)
            for g in range(G):
                sl = slice(g * HP, (g + 1) * HP)
                qv, kv, vv, ov, dov = q_ref[:, sl], k_ref[:, sl], v_ref[:, sl], o_ref[:, sl], do_ref[:, sl]
                s = _dotf(qv, kv, NT) * QK_SCALE
                p = jnp.exp(s - ov[:, VDIM:VDIM + 1])
                if diagonal:
                    rows = lax.broadcasted_iota(jnp.int32, (tq, tk), 0)
                    cols = lax.broadcasted_iota(jnp.int32, (tq, tk), 1)
                    p = jnp.where(rows >= cols, p, 0.0)
                dsum = jnp.sum(dov * ov, axis=1, keepdims=True)
                dv_ref[:, sl] += _dotf(p, dov, TN)
                dp = _dotf(dov, vv, NT)
                ds = p * (dp - dsum) * QK_SCALE
                dk_ref[:, sl] += _dotf(ds, qv, TN)
                dq_ref[pl.ds(r0, tq), sl] += _dotf(ds, kv, NN)

        @pl.when(i > j)
        def _():
            step(False)

        @pl.when(i == j)
        def _():
            step(True)

        if nc:
            @pl.when((h == nh - 1) & (t == nt - 1))
            def _():
                sends, lands = plan()
                for cp in lands:
                    cp.wait_recv()
                for cp in sends:
                    cp.wait_send()

    W = G * HP
    qmap = lambda h, t, qi, kj: (qi[t], h)
    kmap = lambda h, t, qi, kj: (kj[t], h)
    vmap = lambda h, t, qi, kj: (kj[t], HEADS // G + h)
    res = pl.pallas_call(
        body, name="mla_flash_bwd",
        grid_spec=pltpu.PrefetchScalarGridSpec(
            num_scalar_prefetch=2, grid=(nh, nt),
            in_specs=[pl.BlockSpec((tq, W), qmap), pl.BlockSpec((tk, W), kmap), pl.BlockSpec((tk, W), vmap),
                      pl.BlockSpec((tq, W), qmap), pl.BlockSpec((tq, W), qmap)] + [ANY] * nc,
            out_specs=[pl.BlockSpec((T, W), lambda h, t, qi, kj: (0, h)), pl.BlockSpec((tk, W), kmap), pl.BlockSpec((tk, W), kmap)]
            + [ANY] * nc,
            scratch_shapes=_chip_sems(nc) if nc else []),
        out_shape=[jax.ShapeDtypeStruct((T, HEADS * HP), F32)] * 3 + [jax.ShapeDtypeStruct(p.shape, p.dtype) for p in carry],
        compiler_params=pltpu.CompilerParams(dimension_semantics=("arbitrary", "arbitrary")),
    )(qi, kj, q, k, kv, o, dycat, *carry)
    return tuple(res[:3]) if not nc else (*res[:3], _chip_parts(res[3:], carry))


_IN_SRC = (0, 256, 384, 416, 672, 928, 1184, 1440, 2208, 2212)
_IN_DST = (Z_CQ, Z_CKV, Z_KR + KR_LANE, Z_SCB, Z_SCC, Z_SCH, Z_SSZ, Z_XBC, Z_DT)


def _pad_rows_in(w):
    ax = w.ndim - 2

    def zeros(n):
        return jnp.zeros(w.shape[:ax] + (n,) + w.shape[ax + 1:], w.dtype)

    def whole_tiles(p):
        n = p.shape[ax]
        return p if n % SLAB_ALIGN == 0 else jnp.pad(p, [(0, 0)] * ax + [(0, -n % SLAB_ALIGN), (0, 0)])

    parts, at = [], 0
    for s0, s1, d0 in zip(_IN_SRC[:-1], _IN_SRC[1:], _IN_DST):
        if d0 > at:
            parts.append(zeros(d0 - at))
        parts.append(whole_tiles(lax.slice_in_dim(w, s0, s1, axis=ax)))
        at = d0 + parts[-1].shape[ax]
    parts.append(zeros(ZIN - at))
    return jnp.concatenate(parts, axis=ax)


def _unpad_rows_in(w):
    ax = w.ndim - 2
    groups = list(zip(_IN_SRC[:-1], _IN_SRC[1:], _IN_DST))
    parts = [lax.slice_in_dim(w, d0, d0 + -(-(s1 - s0) // SLAB_ALIGN) * SLAB_ALIGN, axis=ax) for s0, s1, d0 in groups]
    return lax.slice_in_dim(jnp.concatenate(parts, axis=ax), 0, _IN_SRC[-1], axis=ax)


def _pad_heads(w, width):
    w = w.reshape(w.shape[:-1] + (HEADS, width))
    w = jnp.pad(w, [(0, 0)] * (w.ndim - 1) + [(0, HP - width)])
    return w.reshape(w.shape[:-2] + (HEADS * HP,))


def _unpad_heads(w, width):
    w = w.reshape(w.shape[:-1] + (HEADS, HP))[..., :width]
    return w.reshape(w.shape[:-2] + (HEADS * width,))


def _pad_kv(w):
    w = w.reshape(w.shape[:-1] + (HEADS, NOPE + VDIM))
    return jnp.concatenate([_pad_heads(w[..., :NOPE].reshape(w.shape[:-2] + (HEADS * NOPE,)), NOPE),
                            _pad_heads(w[..., NOPE:].reshape(w.shape[:-2] + (HEADS * VDIM,)), VDIM)], axis=-1)


def _unpad_kv(w):
    k = _unpad_heads(w[..., :HEADS * HP], NOPE).reshape(w.shape[:-1] + (HEADS, NOPE))
    v = _unpad_heads(w[..., HEADS * HP:], VDIM).reshape(w.shape[:-1] + (HEADS, VDIM))
    return jnp.concatenate([k, v], axis=-1).reshape(w.shape[:-1] + (HEADS * (NOPE + VDIM),))


def _pad_out_rows(w):
    lead, d = w.shape[:-2], w.shape[-1]
    att = w[..., :HEADS * VDIM, :].reshape(lead + (HEADS, VDIM, d))
    att = jnp.pad(att, [(0, 0)] * (att.ndim - 2) + [(0, HP - VDIM), (0, 0)]).reshape(lead + (HEADS * HP, d))
    return jnp.concatenate([att, w[..., HEADS * VDIM:, :]], axis=-2)


def _unpad_out_rows(w):
    lead, d = w.shape[:-2], w.shape[-1]
    att = w[..., :HEADS * HP, :].reshape(lead + (HEADS, HP, d))[..., :VDIM, :].reshape(lead + (HEADS * VDIM, d))
    return jnp.concatenate([att, w[..., HEADS * HP:, :]], axis=-2)


def _rows8(w):
    return jnp.pad(w.astype(F32), [(0, 0)] * (w.ndim - 2) + [(0, 8 - w.shape[-2]), (0, 0)])


def _row8(*vecs):
    c = vecs[0].shape[-1]
    return jnp.concatenate([v.reshape(1, c).astype(F32) for v in vecs] + [jnp.zeros((8 - len(vecs), c), F32)], axis=0)


def _rope_tables(positions):
    inv_freq = 1.0 / (ROPE_THETA ** (jnp.arange(0, ROPE, 2, dtype=F32) / ROPE))
    ang = positions.astype(F32)[:, None] * inv_freq
    cos, sin = jnp.cos(ang), jnp.sin(ang)
    T = positions.shape[0]
    half = ROPE // 2
    one = jnp.ones((T, KR_LANE), F32)
    zero = jnp.zeros((T, KR_LANE), F32)
    tail1 = jnp.ones((T, HP - KR_LANE - ROPE), F32)
    tail0 = jnp.zeros((T, HP - KR_LANE - ROPE), F32)
    z16 = jnp.zeros((T, half), F32)
    cosf = jnp.concatenate([one, cos, cos, tail1], axis=1)
    sina = jnp.concatenate([zero, -sin, z16, tail0], axis=1)
    sinb = jnp.concatenate([zero, z16, sin, tail0], axis=1)
    return cosf, sina, sinb


def _kernel_weights(W):
    c = lambda a: a.astype(MXU_DTYPE)
    forms = dict(
        w_in=("w_in", lambda w: c(_pad_rows_in(w))),
        w_q=("mla_w_q_up", lambda w: c(_pad_heads(w, NOPE + ROPE))),
        w_kv=("mla_w_kv_up", lambda w: c(_pad_kv(w))),
        w_out=("w_out", lambda w: c(_pad_out_rows(w))),
        w_up=("ffn_w_up", c),
        w_down=("ffn_w_down", c),
        sc_w=("sc_conv_w", _rows8),
        ssd_w=("ssd_conv_w", _rows8),
        ffn_w=("ffn_conv_w", _rows8),
    )
    return {k: f(W[n]) for k, (n, f) in forms.items() if n in W}


def _layer_weights(KW, l):
    return {k: (v[l] if k in ("sc_w", "ssd_w", "ffn_w") else (v, l)) for k, v in KW.items()}


def _local_step(x, positions, target, W, S, ex=None):
    T = x.shape[0]
    tm = min(ROW_BLOCK, T)
    tm_ffn = min(FFN_ROWS, T)
    cosf, sina, sinb = _rope_tables(positions)
    if ex is None:
        KW = _kernel_weights(W)
    else:
        early = _all_gather_weights(ex.shard(0, "early"))
    saved = []
    xl = x
    for l in range(DEPTH):
        lw = _layer_weights(KW, l) if ex is None else _kernel_weights(ex.weights(early, "early"))
        g_pre = S["norm_mix_pre"][l].reshape(1, -1)
        g_post = S["norm_mix_post"][l].reshape(1, -1)
        g_fpre = S["norm_ffn_pre"][l].reshape(1, -1)
        g_fpost = S["norm_ffn_post"][l].reshape(1, -1)
        qn = S["mla_q_norm"][l].reshape(1, -1)
        kvn = S["mla_kv_norm"][l].reshape(1, -1)
        ssd_b = S["ssd_conv_b"][l].reshape(1, -1)
        ssd_par = _row8(jnp.pad(S["ssd_dt_bias"][l], (0, LANE - SSD_HEADS)), jnp.pad(S["ssd_a_log"][l], (0, LANE - SSD_HEADS)),
                        jnp.pad(S["ssd_d"][l], (0, LANE - SSD_HEADS)))
        ssd_nw = S["ssd_norm"][l].reshape(1, -1)
        ffn_b = S["ffn_conv_b"][l].reshape(1, -1)

        (h1,) = _rows(lambda i, n, *v: _f_premix(*v), T, tm, [_cur(xl)], [_cst(g_pre)], [_out(D_MODEL, BF16)], [], "pre_mix_norm")
        zin = _mm(h1, lw["w_in"], "nt", F32, "mm_in")
        qlat, kvlat = _rows(lambda i, n, *v: _f_mla_pre(*v), T, tm, [_cur(zin, Q_LORA, 0), _cur(zin, KV_LORA, Z_CKV // KV_LORA)],
                            [_cst(qn), _cst(kvn)], [_out(Q_LORA, BF16), _out(KV_LORA, BF16)], [], "mla_pre_norm")
        qpad = _mm(qlat, lw["w_q"], "nn", F32, "mm_q_up")
        kvpad = _mm(kvlat, lw["w_kv"], "nn", BF16, "mm_kv_up")
        qr, kr = _rows(_k_rope_fwd, T, tm, [_cur(qpad), _cur(kvpad, HEADS * HP, 0), _cur(zin, LANE, Z_KR // LANE),
                                            _cur(cosf), _cur(sina), _cur(sinb)], [],
                       [_out(HEADS * HP, BF16), _out(HEADS * HP, BF16)], [], "mla_rope")
        if ex is None:
            o = _flash_fwd(qr, kr, kvpad, T)
        else:
            nlate = len(ex.layouts["late"])
            o, got = _flash_fwd(qr, kr, kvpad, T, carry=ex.shard(l, "late") + (ex.shard(l + 1, "early") if l + 1 < DEPTH else []))
            lw.update(_kernel_weights(ex.weights(got[:nlate], "late")))
            early = got[nlate:]
        (yconv,) = _rows(_k_sconv_fwd, T, tm, [_cur(zin, SC_DIM, Z_SCB // SC_DIM), _cur(zin, SC_DIM, Z_SCC // SC_DIM),
                                               _cur(zin, SC_DIM, Z_SCH // SC_DIM), _halo(zin, "prev", SC_DIM, Z_SCC // SC_DIM),
                                               _halo(zin, "prev", SC_DIM, Z_SCH // SC_DIM)], [_cst(lw["sc_w"])],
                         [_out(SC_DIM, F32)], [], "short_conv_fwd")
        (xbc,) = _rows(_k_ssdconv_fwd, T, tm, [_cur(zin, SSD_CONV_DIM, Z_XBC // SSD_CONV_DIM),
                                               _halo(zin, "prev", SSD_CONV_DIM, Z_XBC // SSD_CONV_DIM)],
                       [_cst(lw["ssd_w"]), _cst(ssd_b)], [_out(SSD_CONV_DIM, F32)], [], "ssd_conv_fwd")
        yscan, states = _ssd_fwd(xbc, zin, ssd_par, T, Z_DT // LANE)
        (yssd,) = _rows(lambda i, n, *v: _f_ssd_gate(*v), T, tm, [_cur(yscan), _cur(zin, SSD_DIM, Z_SSZ // SSD_DIM)], [_cst(ssd_nw)],
                        [_out(SSD_DIM, F32)], [], "ssd_gate_fwd")
        ycat = jnp.concatenate([o.astype(BF16), yconv.astype(BF16), yssd.astype(BF16)], axis=1)
        mixed = _mm(ycat, lw["w_out"], "nn", F32, "mm_out")
        x1, h2 = _rows(lambda i, n, *v: _f_post_mix(*v), T, tm, [_cur(xl), _cur(mixed)], [_cst(g_post), _cst(g_fpre)],
                       [_out(D_MODEL, F32), _out(D_MODEL, BF16)], [], "post_mix_fwd")
        upre = _mm(h2, lw["w_up"], "nn", F32, "mm_up")
        nt = FFN_DIM // FFN_TILE
        gcol, ucol = (lambda j: j), (lambda j: j + nt)
        (act,) = _rows(_k_ffnact_fwd, T, tm_ffn,
                       [(upre, FFN_TILE, gcol, "cur"), (upre, FFN_TILE, ucol, "cur"), (upre, FFN_TILE, gcol, "prev"),
                        (upre, FFN_TILE, ucol, "prev")],
                       [(lw["ffn_w"], FFN_TILE, gcol), (lw["ffn_w"], FFN_TILE, ucol), (ffn_b, FFN_TILE, gcol), (ffn_b, FFN_TILE, ucol)],
                       [(FFN_DIM, BF16, FFN_TILE, gcol)], [], "ffn_act_fwd", ncol=nt)
        dn = _mm(act, lw["w_down"], "nn", F32, "mm_down")
        (x2,) = _rows(lambda i, n, *v: _f_post_ffn(*v), T, tm, [_cur(x1), _cur(dn)], [_cst(g_fpost)], [_out(D_MODEL, F32)], [], "post_ffn_fwd")
        saved.append(dict(lw=lw, x=xl, h1=h1, zin=zin, qlat=qlat, kvlat=kvlat, qr=qr, kr=kr, kvpad=kvpad, o=o, xbc=xbc,
                          yscan=yscan, states=states, ycat=ycat, mixed=mixed, x1=x1, h2=h2, upre=upre, act=act, dn=dn,
                          g_pre=g_pre, g_post=g_post, g_fpre=g_fpre, g_fpost=g_fpost, qn=qn, kvn=kvn, ssd_b=ssd_b,
                          ssd_par=ssd_par, ssd_nw=ssd_nw, ffn_b=ffn_b))
        xl = x2

    gx, loss_part = _rows(_k_loss, T, tm, [_cur(xl), _cur(target)], [], [_out(D_MODEL, F32)], [_acc(1, LANE)], "loss_head")

    GW = {k: [None] * DEPTH for k in ("w_in", "mla_w_q_up", "mla_w_kv_up", "sc_conv_w", "ssd_conv_w", "w_out", "ffn_w_up",
                                      "ffn_conv_w", "ffn_w_down")}
    GS = {k: [None] * DEPTH for k in ("norm_mix_pre", "norm_mix_post", "norm_ffn_pre", "norm_ffn_post", "mla_q_norm", "mla_kv_norm",
                                      "ssd_conv_b", "ssd_dt_bias", "ssd_a_log", "ssd_d", "ssd_norm", "ffn_conv_b")}
    nt = FFN_DIM // FFN_TILE
    gcol, ucol = (lambda j: j), (lambda j: j + nt)
    pending = None
    for l in reversed(range(DEPTH)):
        s = saved[l]
        lw = s["lw"]
        gx1, ddn, dgf = _rows_vjp(_f_post_ffn, T, tm, [s["x1"], s["dn"]], [s["g_fpost"]], [gx], [F32, BF16], "post_ffn_bwd")
        GS["norm_ffn_post"][l] = dgf[0]
        dact = _mm(ddn, lw["w_down"], "nt", F32, "mm_down_dx")
        GW["ffn_w_down"][l] = _mm(s["act"], ddn, "tn", BF16, "mm_down_dw")
        up = s["upre"]
        dug, duu, dwg, dwu, dbg, dbu = _rows(
            _k_ffnact_bwd, T, tm_ffn,
            [(up, FFN_TILE, gcol, "cur"), (up, FFN_TILE, ucol, "cur"), (dact, FFN_TILE, gcol, "cur"), (up, FFN_TILE, gcol, "prev"),
             (up, FFN_TILE, ucol, "prev"), (up, FFN_TILE, gcol, "next"), (up, FFN_TILE, ucol, "next"), (dact, FFN_TILE, gcol, "next")],
            [(lw["ffn_w"], FFN_TILE, gcol), (lw["ffn_w"], FFN_TILE, ucol), (s["ffn_b"], FFN_TILE, gcol), (s["ffn_b"], FFN_TILE, ucol)],
            [(FFN_DIM, BF16, FFN_TILE, gcol)] * 2,
            [(HALO, FFN_DIM, FFN_TILE, gcol)] * 2 + [(1, FFN_DIM, FFN_TILE, gcol)] * 2, "ffn_act_bwd", ncol=nt)
        GW["ffn_conv_w"][l] = jnp.concatenate([dwg[:3], dwu[:3]], axis=1)
        GS["ffn_conv_b"][l] = jnp.concatenate([dbg[0], dbu[0]])
        dh2 = _mm((dug, duu), lw["w_up"], "nt", F32, "mm_up_dx")
        GW["ffn_w_up"][l] = (_mm(s["h2"], dug, "tn", BF16, "mm_up_dw_gate"), _mm(s["h2"], duu, "tn", BF16, "mm_up_dw_up"))
        gx0, dmixed, dgp, dgf = _rows_vjp(_f_post_mix, T, tm, [s["x"], s["mixed"]], [s["g_post"], s["g_fpre"]], [gx1, dh2],
                                          [F32, BF16], "post_mix_bwd")
        GS["norm_mix_post"][l], GS["norm_ffn_pre"][l] = dgp[0], dgf[0]
        dycat = _mm(dmixed, lw["w_out"], "nt", F32, "mm_out_dx")
        GW["w_out"][l] = _unpad_out_rows(_mm(s["ycat"], dmixed, "tn", BF16, "mm_out_dw"))
        zin = s["zin"]
        dyscan, dz, dnw = _rows(_vjp_wrap(_f_ssd_gate, 2, 1), T, tm,
                                [_cur(s["yscan"]), _cur(zin, SSD_DIM, Z_SSZ // SSD_DIM), _cur(dycat, SSD_DIM, (HEADS * HP + SC_DIM) // SSD_DIM)],
                                [_cst(s["ssd_nw"])], [_out(SSD_DIM, F32), _out(SSD_DIM, BF16)], [_acc(1, SSD_DIM)], "ssd_gate_bwd")
        GS["ssd_norm"][l] = dnw[0]
        dxbc, ddtraw, dpar = _ssd_bwd(s["xbc"], zin, s["ssd_par"], s["states"], dyscan, T, Z_DT // LANE)
        GS["ssd_dt_bias"][l], GS["ssd_a_log"][l], GS["ssd_d"][l] = dpar[0, :SSD_HEADS], dpar[1, :SSD_HEADS], dpar[2, :SSD_HEADS]
        xb = Z_XBC // SSD_CONV_DIM
        dxraw, dsw, dsb = _rows(_k_ssdconv_bwd, T, tm,
                                [_cur(zin, SSD_CONV_DIM, xb), _cur(dxbc), _halo(zin, "prev", SSD_CONV_DIM, xb),
                                 _halo(zin, "next", SSD_CONV_DIM, xb), _halo(dxbc, "next")],
                                [_cst(lw["ssd_w"]), _cst(s["ssd_b"])], [_out(SSD_CONV_DIM, BF16)],
                                [_acc(HALO, SSD_CONV_DIM), _acc(1, SSD_CONV_DIM)], "ssd_conv_bwd")
        GW["ssd_conv_w"][l] = dsw[:4]
        GS["ssd_conv_b"][l] = dsb[0]
        cb = (HEADS * HP) // SC_DIM
        dscb, dscc, dsch, dscw = _rows(_k_sconv_bwd, T, tm,
                                       [_cur(zin, SC_DIM, Z_SCB // SC_DIM), _cur(zin, SC_DIM, Z_SCC // SC_DIM),
                                        _cur(zin, SC_DIM, Z_SCH // SC_DIM), _cur(dycat, SC_DIM, cb),
                                        _halo(zin, "prev", SC_DIM, Z_SCC // SC_DIM), _halo(zin, "prev", SC_DIM, Z_SCH // SC_DIM),
                                        _halo(zin, "next", SC_DIM, Z_SCB // SC_DIM), _halo(dycat, "next", SC_DIM, cb)],
                                       [_cst(lw["sc_w"])], [_out(SC_DIM, BF16)] * 3, [_acc(HALO, SC_DIM)], "short_conv_bwd")
        GW["sc_conv_w"][l] = dscw[:3]
        if ex is None:
            dq, dk, dv = _flash_bwd(s["qr"], s["kr"], s["kvpad"], s["o"], dycat, T)
        else:
            sums = ex.submit([({n: GW[n][l] for ns in LATE for n in ns}, "late")] + ([(pending, "early")] if pending else []))
            late = sums[0]
            dq, dk, dv, parts = _flash_bwd(s["qr"], s["kr"], s["kvpad"], s["o"], dycat, T, carry=[p for ps in sums for p in ps])
            ex.collect(l, "late", parts[:len(late)])
            if pending:
                ex.collect(l + 1, "early", parts[len(late):])
        dqpad, dkvpad, dkr = _rows(_k_rope_bwd, T, tm, [_cur(dq), _cur(dk), _cur(dv), _cur(cosf), _cur(sina), _cur(sinb)], [],
                                   [_out(HEADS * HP, BF16), _out(2 * HEADS * HP, BF16), _out(LANE, BF16)], [], "mla_rope_bwd")
        dqlat = _mm(dqpad, lw["w_q"], "nt", F32, "mm_q_dx")
        GW["mla_w_q_up"][l] = _unpad_heads(_mm(s["qlat"], dqpad, "tn", BF16, "mm_q_dw"), NOPE + ROPE)
        dkvlat = _mm(dkvpad, lw["w_kv"], "nt", F32, "mm_kv_dx")
        GW["mla_w_kv_up"][l] = _unpad_kv(_mm(s["kvlat"], dkvpad, "tn", BF16, "mm_kv_dw"))
        dcq, dckv, dqn, dkvn = _rows(_vjp_wrap(_f_mla_pre, 2, 2), T, tm,
                                     [_cur(zin, Q_LORA, 0), _cur(zin, KV_LORA, Z_CKV // KV_LORA), _cur(dqlat), _cur(dkvlat)],
                                     [_cst(s["qn"]), _cst(s["kvn"])], [_out(Q_LORA, BF16), _out(KV_LORA, BF16)],
                                     [_acc(1, Q_LORA), _acc(1, KV_LORA)], "mla_pre_bwd")
        GS["mla_q_norm"][l], GS["mla_kv_norm"][l] = dqn[0], dkvn[0]
        dzin = jnp.concatenate([dcq, dckv, dkr, dscb, dscc, dsch, dz, dxraw, ddtraw.astype(BF16), jnp.zeros((T, ZIN - Z_DT - LANE), BF16)], axis=1)
        dh1 = _mm(dzin, lw["w_in"], "nn", F32, "mm_in_dx")
        GW["w_in"][l] = _unpad_rows_in(_mm(dzin, s["h1"], "tn", BF16, "mm_in_dw"))
        gx, dgp = _rows(_vjp_wrap(_f_premix, 1, 1, add_first=True), T, tm, [_cur(s["x"]), _cur(dh1), _cur(gx0)], [_cst(s["g_pre"])],
                        [_out(D_MODEL, F32)], [_acc(1, D_MODEL)], "pre_mix_bwd")
        GS["norm_mix_pre"][l] = dgp[0]
        if ex is not None:
            pending = {n: GW[n][l] for ns in EARLY for n in ns}
    if ex is not None:
        ex.collect(0, "early", _rs_chip_exchange(ex.submit([(pending, "early")])[0]))
    GS = {k: jnp.stack(v) for k, v in GS.items()}
    return loss_part[0, 0], gx, GW, GS


WEIGHTS = ("norm_mix_pre", "norm_mix_post", "norm_ffn_pre", "norm_ffn_post", "w_in", "mla_q_norm", "mla_w_q_up", "mla_kv_norm",
           "mla_w_kv_up", "sc_conv_w", "ssd_conv_w", "ssd_conv_b", "ssd_dt_bias", "ssd_a_log", "ssd_d", "ssd_norm", "w_out",
           "ffn_w_up", "ffn_conv_w", "ffn_conv_b", "ffn_w_down")
SHARDED = (("w_in", 2), ("mla_w_q_up", 2), ("mla_w_kv_up", 2), ("sc_conv_w", 2), ("ssd_conv_w", 2), ("w_out", 1),
           ("ffn_w_up", 2), ("ffn_conv_w", 2), ("ffn_w_down", 1))
SMALL = tuple(n for n in WEIGHTS if n not in dict(SHARDED))
N_CHIPS = 4
N_DEV = 8
ROW_ALIGN = 64
SLAB_ALIGN = 16
EARLY = (("w_in", "mla_w_q_up", "mla_w_kv_up", "sc_conv_w", "ssd_conv_w"),)
LATE = (("ffn_w_down", "w_out"), ("ffn_w_up", "ffn_conv_w"))
TRANSPOSED = ("w_in",)


def _is_rows(shape, width):
    return shape[-1] == width and math.prod(shape[:-1]) % SLAB_ALIGN == 0


def _is_short(shape, width):
    return len(shape) == 2 and shape[1] == width and not _is_rows(shape, width)


def _slab_rows(shape, width):
    if _is_rows(shape, width):
        return math.prod(shape[:-1])
    if _is_short(shape, width):
        return -(-shape[0] // SLAB_ALIGN) * SLAB_ALIGN
    return -(-math.prod(shape) // (width * SLAB_ALIGN)) * SLAB_ALIGN


def _slab(piece, width, dtype, lead=0):
    ld, shape = piece.shape[:lead], piece.shape[lead:]
    rows = _slab_rows(shape, width)
    if _is_rows(shape, width):
        return piece.astype(dtype).reshape(ld + (rows, width))
    if _is_short(shape, width):
        return jnp.pad(piece.astype(dtype), [(0, 0)] * lead + [(0, rows - shape[0]), (0, 0)])
    flat = piece.astype(dtype).reshape(ld + (-1,))
    return jnp.pad(flat, [(0, 0)] * lead + [(0, rows * width - flat.shape[-1])]).reshape(ld + (rows, width))


def _unslab(slab, shape, lead=0):
    ld = slab.shape[:lead]
    if _is_rows(shape, slab.shape[-1]):
        return slab.reshape(ld + tuple(shape))
    if _is_short(shape, slab.shape[-1]):
        return slab[..., :shape[0], :]
    return slab.reshape(ld + (-1,))[..., :math.prod(shape)].reshape(ld + tuple(shape))


def _layout(shapes, names, width):
    ents, off = [], 0
    for n in names:
        shp = tuple(shapes[n])
        todo = [(None, False, shp), (None, True, shp)] if n.endswith("conv_w") else [(l, False, shp[1:]) for l in range(shp[0])]
        for l, lo, ps in todo:
            r = _slab_rows(ps, width)
            ents.append((n, l, lo, ps, off, r))
            off += r
    return width, -(-off // ROW_ALIGN) * ROW_ALIGN, ents


def _pack(layout, piece, dtype, lead=0):
    width, rows, ents = layout
    slabs, ld = [], None
    for n, l, lo, ps, off, r in ents:
        p = piece(n, l, lo)
        slabs.append(None if p is None else _slab(p, width, dtype, lead))
        ld = ld if p is None else p.shape[:lead]
    used = ents[-1][4] + ents[-1][5]
    slabs = [jnp.zeros(ld + (e[5], width), dtype) if s is None else s for s, e in zip(slabs, ents)]
    if rows > used:
        slabs.append(jnp.zeros(ld + (rows - used, width), dtype))
    return jnp.concatenate(slabs, axis=lead)


ANY = pl.BlockSpec(memory_space=pl.ANY)


def _pos():
    return lax.axis_index("x"), lax.axis_index("y"), lax.axis_index("c")


def _other_chips(x, y):
    return ((1 - x, y), (x, 1 - y), (1 - x, 1 - y))


def _remote(src, dst, ssem, rsem, dev):
    return pltpu.make_async_remote_copy(src_ref=src, dst_ref=dst, send_sem=ssem, recv_sem=rsem, device_id=dev, device_id_type=MESH)


AG_CHUNKS = 2


def _chip_index():
    return 2 * lax.axis_index("x") + lax.axis_index("y")


def _ag_sems(nbuf):
    return [pltpu.SemaphoreType.DMA((nbuf * 3 * AG_CHUNKS,))] * 4


def _ag_plan(w_refs, out_refs, sems):
    isend, irecv, dsend, drecv = sems
    x, y, c = _pos()
    k = 2 * x + y
    sib = (x, y, 1 - c)
    sends, lands, forwards, finals = [], [], [], []
    s = 0
    for w_ref, out_ref in zip(w_refs, out_refs):
        H = w_ref.shape[0] // 2
        CH = H // AG_CHUNKS
        for cx, cy in _other_chips(x, y):
            for ch in range(AG_CHUNKS):
                mine = out_ref.at[k, pl.ds(c * H + ch * CH, CH), :]
                near = out_ref.at[2 * cx + cy, pl.ds(c * H + ch * CH, CH), :]
                far = out_ref.at[2 * cx + cy, pl.ds((1 - c) * H + ch * CH, CH), :]
                sends.append(_remote(w_ref.at[pl.ds(c * H + ch * CH, CH), :], mine, isend.at[s], irecv.at[s], (cx, cy, c)))
                lands.append(_remote(near, near, isend.at[s], irecv.at[s], (cx, cy, c)))
                forwards.append(_remote(near, near, dsend.at[s], drecv.at[s], sib))
                finals.append(_remote(far, far, dsend.at[s], drecv.at[s], sib))
                s += 1
    return sends, lands, forwards, finals


def _own_slot(got, own):
    return lax.dynamic_update_slice(got, own[None], (_chip_index(), 0, 0))


def _all_gather_weights(ws):
    nb = len(ws)

    def body(*refs):
        sends, lands, forwards, finals = _ag_plan(refs[:nb], refs[nb:2 * nb], refs[2 * nb:])
        for cp in sends:
            cp.start()
        for land, fw in zip(lands, forwards):
            land.wait_recv()
            fw.start()
        for cp in finals:
            cp.wait_recv()
        for cp in sends + forwards:
            cp.wait_send()

    got = pl.pallas_call(
        body, name="all_gather_weights", in_specs=[ANY] * nb, out_specs=[ANY] * nb,
        out_shape=[jax.ShapeDtypeStruct((N_CHIPS,) + w.shape, w.dtype) for w in ws], scratch_shapes=_ag_sems(nb),
    )(*ws)
    return [_own_slot(g, w) for g, w in zip(got, ws)]


def _rs_pair_exchange(gs):
    nb = len(gs)

    def body(*refs):
        g_refs, got_refs, (ssem, rsem) = refs[:nb], refs[nb:2 * nb], refs[2 * nb:]
        x, y, c = _pos()
        cps = []
        for b, (g_ref, got_ref) in enumerate(zip(g_refs, got_refs)):
            H = g_ref.shape[1] // 2
            for kk in range(N_CHIPS):
                s = b * N_CHIPS + kk
                cps.append(_remote(g_ref.at[kk, pl.ds((1 - c) * H, H), :], got_ref.at[kk], ssem.at[s], rsem.at[s], (x, y, 1 - c)))
        for cp in cps:
            cp.start()
        for cp in cps:
            cp.wait()

    return pl.pallas_call(
        body, name="rs_pair_exchange", in_specs=[ANY] * nb, out_specs=[ANY] * nb,
        out_shape=[jax.ShapeDtypeStruct((N_CHIPS, g.shape[1] // 2, g.shape[2]), g.dtype) for g in gs],
        scratch_shapes=[pltpu.SemaphoreType.DMA((nb * N_CHIPS,))] * 2,
    )(*gs)


def _chip_sems(nbuf):
    return [pltpu.SemaphoreType.DMA((nbuf * 3,))] * 2


def _chip_plan(p_refs, out_refs, sems):
    ssem, rsem = sems
    x, y, c = _pos()
    sends, lands = [], []
    s = 0
    for p_ref, out_ref in zip(p_refs, out_refs):
        for cx, cy in _other_chips(x, y):
            sends.append(_remote(p_ref.at[2 * cx + cy], out_ref.at[2 * x + y], ssem.at[s], rsem.at[s], (cx, cy, c)))
            land = out_ref.at[2 * cx + cy]
            lands.append(_remote(land, land, ssem.at[s], rsem.at[s], (cx, cy, c)))
            s += 1
    return sends, lands


def _chip_parts(got, ps):
    k = _chip_index()
    return [lax.dynamic_update_slice(g, lax.dynamic_slice_in_dim(p, k, 1, axis=0), (k, 0, 0)) for g, p in zip(got, ps)]


def _rs_chip_exchange(ps):
    nb = len(ps)

    def body(*refs):
        sends, lands = _chip_plan(refs[:nb], refs[nb:2 * nb], refs[2 * nb:])
        for cp in sends:
            cp.start()
        for cp in lands:
            cp.wait_recv()
        for cp in sends:
            cp.wait_send()

    got = pl.pallas_call(
        body, name="rs_chip_exchange", in_specs=[ANY] * nb, out_specs=[ANY] * nb,
        out_shape=[jax.ShapeDtypeStruct(p.shape, p.dtype) for p in ps], scratch_shapes=_chip_sems(nb),
    )(*ps)
    return _chip_parts(got, ps)


def _rs_pair_share(fs):
    nb = len(fs)

    def body(*refs):
        f_refs, out_refs, (ssem, rsem) = refs[:nb], refs[nb:2 * nb], refs[2 * nb:]
        x, y, c = _pos()
        sends, lands = [], []
        for b, (f_ref, out_ref) in enumerate(zip(f_refs, out_refs)):
            sends.append(_remote(f_ref, out_ref.at[c], ssem.at[b], rsem.at[b], (x, y, 1 - c)))
            land = out_ref.at[1 - c]
            lands.append(_remote(land, land, ssem.at[b], rsem.at[b], (x, y, 1 - c)))
        for cp in sends:
            cp.start()
        for cp in lands:
            cp.wait_recv()
        for cp in sends:
            cp.wait_send()

    got = pl.pallas_call(
        body, name="rs_pair_share", in_specs=[ANY] * nb, out_specs=[ANY] * nb,
        out_shape=[jax.ShapeDtypeStruct((2,) + f.shape, f.dtype) for f in fs],
        scratch_shapes=[pltpu.SemaphoreType.DMA((nb,))] * 2,
    )(*fs)
    return [lax.dynamic_update_slice(g, f[None], (lax.axis_index("c"), 0, 0)) for g, f in zip(got, fs)]


def _all_reduce_small(s):
    r, C = s.shape

    def body(s_ref, o_ref, buf, ssem, rsem):
        x, y, c = _pos()
        me = 4 * x + 2 * y + c
        buf[me] = s_ref[...]
        cps = []
        for m in range(1, N_DEV):
            mx, my, mc = (m >> 2) & 1, (m >> 1) & 1, m & 1
            peer = (x ^ mx, y ^ my, c ^ mc)
            cp = _remote(s_ref, buf.at[me], ssem.at[m - 1], rsem.at[m - 1], peer)
            cp.start()
            cps.append(cp)
        for m in range(1, N_DEV):
            mx, my, mc = (m >> 2) & 1, (m >> 1) & 1, m & 1
            src = 4 * (x ^ mx) + 2 * (y ^ my) + (c ^ mc)
            _remote(s_ref, buf.at[src], ssem.at[m - 1], rsem.at[m - 1], (x ^ mx, y ^ my, c ^ mc)).wait_recv()
        for cp in cps:
            cp.wait_send()
        acc = buf[0]
        for j in range(1, N_DEV):
            acc = acc + buf[j]
        o_ref[...] = acc

    return pl.pallas_call(
        body, name="all_reduce_small", in_specs=[pl.BlockSpec(memory_space=pltpu.VMEM)],
        out_specs=pl.BlockSpec(memory_space=pltpu.VMEM), out_shape=jax.ShapeDtypeStruct((r, C), F32),
        scratch_shapes=[pltpu.VMEM((N_DEV, r, C), F32), pltpu.SemaphoreType.DMA((N_DEV - 1,)), pltpu.SemaphoreType.DMA((N_DEV - 1,))],
    )(s)


def _rtile(n, pref):
    if n <= pref:
        return n
    t = (pref // 16) * 16
    while t >= 16:
        if n % t == 0:
            return t
        t -= 16
    raise ValueError(f"no row tile for {n}")


def _rs_pair_sums(gpks):
    gots = _rs_pair_exchange(gpks)
    out = []
    for gpk, got in zip(gpks, gots):
        _, R, C = gpk.shape
        H = R // 2
        own = lax.dynamic_index_in_dim(gpk.reshape(N_CHIPS, 2, H, C), lax.axis_index("c"), axis=1, keepdims=False)
        (part,) = _rows(lambda i, n, a, b: (a.astype(F32) + b.astype(F32),), N_CHIPS * H, _rtile(N_CHIPS * H, 512),
                        [_cur(own.reshape(N_CHIPS * H, C)), _cur(got.reshape(N_CHIPS * H, C))], [], [_out(C, BF16)], [], "rs_pair_add")
        out.append(part.reshape(N_CHIPS, H, C))
    return out


def _rs_chip_sums(parts):
    def add4(i, n, a, b, c, d):
        return (((a.astype(F32) + b.astype(F32)) + c.astype(F32)) + d.astype(F32),)

    out = []
    for p in parts:
        _, H, C = p.shape
        tm = _rtile(H, 1024)
        (red,) = _rows(add4, H, tm, [(p.reshape(N_CHIPS * H, C), C, functools.partial(_const, v=0), j * (H // tm)) for j in range(N_CHIPS)],
                       [], [_out(C, F32)], [], "rs_chip_add")
        out.append(red)
    return out


class _Exchange:
    def __init__(self, a):
        self.a = a
        self.axis = {n: (1 if n in TRANSPOSED else ax) for n, ax in SHARDED}
        shapes = {n: (1,) + tuple(self.packed(n, a[n]).shape[1:]) for n in self.axis}
        widths = lambda names: shapes[names[0]][-1] if names[0] == "ffn_w_up" else PACK_COLS
        self.layouts = {"early": [_layout(shapes, ns, widths(ns)) for ns in EARLY], "late": [_layout(shapes, ns, widths(ns)) for ns in LATE]}
        self.reduced = {}

    @staticmethod
    def packed(n, w):
        return jnp.swapaxes(w, -1, -2) if n in TRANSPOSED else w

    def shard(self, l, group):
        def piece(n, li, lo):
            w = self.packed(n, self.a[n][l:l + 1] if li is None else self.a[n][l])
            return w - w.astype(BF16).astype(F32) if lo else w
        return [_pack(lay, piece, BF16) for lay in self.layouts[group]]

    def weights(self, gathered, group):
        W, resid = {}, {}
        for (width, rows, ents), g in zip(self.layouts[group], gathered):
            for n, li, lo, ps, off, r in ents:
                parts = _unslab(g[:, off:off + r], ps, lead=1)
                ax = self.axis[n] + (1 if li is None else 0)
                full = jnp.moveaxis(parts, 0, ax - 1)
                full = full.reshape(full.shape[:ax - 1] + (-1,) + full.shape[ax + 1:])
                (resid if lo else W)[n] = full[0] if li is None else full
        for n in resid:
            W[n] = W[n].astype(F32) + resid[n].astype(F32)
        return W

    def submit(self, jobs):
        def by_chip(g, ax, parts=N_CHIPS):
            g = g.reshape(g.shape[:ax] + (parts, g.shape[ax] // parts) + g.shape[ax + 1:])
            return jnp.moveaxis(g, ax, 0)

        def pieces_of(GW):
            def piece(n, li, lo):
                if lo:
                    return None
                g = GW[n]
                if isinstance(g, tuple):
                    return jnp.concatenate([by_chip(h, self.axis[n] - 1, N_CHIPS // 2) for h in g])
                return by_chip(g[None], self.axis[n]) if li is None else by_chip(g, self.axis[n] - 1)
            return piece

        sums = _rs_pair_sums([_pack(lay, pieces_of(GW), BF16, lead=1) for GW, group in jobs for lay in self.layouts[group]])
        out, at = [], 0
        for _, group in jobs:
            out.append(sums[at:at + len(self.layouts[group])])
            at += len(self.layouts[group])
        return out

    def collect(self, l, group, parts):
        self.reduced[l, group] = _rs_chip_sums(parts)

    def finish(self):
        keys = [(l, g) for l in range(DEPTH) for g in self.layouts]
        flat = _rs_pair_share([f for key in keys for f in self.reduced[key]])
        both, at = {}, 0
        for key in keys:
            both[key] = flat[at:at + len(self.layouts[key[1]])]
            at += len(self.layouts[key[1]])
        grads = {}
        for group, lays in self.layouts.items():
            for b, (width, rows, ents) in enumerate(lays):
                for n, li, lo, ps, off, r in ents:
                    if not lo:
                        per_layer = [self.packed(n, _unslab(both[l, group][b].reshape(rows, width)[off:off + r], ps)) for l in range(DEPTH)]
                        grads[n] = jnp.concatenate(per_layer) if li is None else jnp.stack(per_layer)
        return grads


def _adam(w, g, m, v, name, g_row=0):
    shp = w.shape
    two = lambda a: a.reshape(-1, shp[-1])
    rows = math.prod(shp[:-1])
    tm = _rtile(rows, 256)
    assert g_row % tm == 0
    g_in = (two(g), shp[-1], functools.partial(_const, v=0), g_row // tm)
    res = _rows(_k_adam, rows, tm, [_cur(two(w)), g_in, _cur(two(m)), _cur(two(v))], [], [_out(shp[-1], F32)] * 4, [], name)
    return tuple(r.reshape(shp) for r in res)


def _pack_flat(parts, rows):
    flat = jnp.concatenate([p.astype(F32).reshape(-1) for p in parts])
    return jnp.pad(flat, (0, rows * PACK_COLS - flat.shape[0])).reshape(rows, PACK_COLS)


def _unpack_flat(buf, shapes):
    flat, out, off = buf.reshape(-1), [], 0
    for shp in shapes:
        n = math.prod(shp)
        out.append(flat[off:off + n].reshape(shp))
        off += n
    return out


def kernel(x, positions, norm_mix_pre, norm_mix_post, norm_ffn_pre, norm_ffn_post, w_in, mla_q_norm, mla_w_q_up, mla_kv_norm, mla_w_kv_up, sc_conv_w, ssd_conv_w, ssd_conv_b, ssd_dt_bias, ssd_a_log, ssd_d, ssd_norm, w_out, ffn_w_up, ffn_conv_w, ffn_conv_b, ffn_w_down, loss_target, m_norm_mix_pre, m_norm_mix_post, m_norm_ffn_pre, m_norm_ffn_post, m_w_in, m_mla_q_norm, m_mla_w_q_up, m_mla_kv_norm, m_mla_w_kv_up, m_sc_conv_w, m_ssd_conv_w, m_ssd_conv_b, m_ssd_dt_bias, m_ssd_a_log, m_ssd_d, m_ssd_norm, m_w_out, m_ffn_w_up, m_ffn_conv_w, m_ffn_conv_b, m_ffn_w_down, v_norm_mix_pre, v_norm_mix_post, v_norm_ffn_pre, v_norm_ffn_post, v_w_in, v_mla_q_norm, v_mla_w_q_up, v_mla_kv_norm, v_mla_w_kv_up, v_sc_conv_w, v_ssd_conv_w, v_ssd_conv_b, v_ssd_dt_bias, v_ssd_a_log, v_ssd_d, v_ssd_norm, v_w_out, v_ffn_w_up, v_ffn_conv_w, v_ffn_conv_b, v_ffn_w_down):
    a = dict(locals())
    ex = _Exchange(a)
    S = {n: a[n] for n in SMALL}
    loss_part, gx, _, GS = _local_step(a["x"][0], a["positions"][0], a["loss_target"][0], None, S, ex)

    grads, delta, new_m, new_v = {}, {}, {}, {}
    for n, g in ex.finish().items():
        grads[n], delta[n], new_m[n], new_v[n] = _adam(a[n], g, a["m_" + n], a["v_" + n], "adamw_" + n)

    small_shapes = [a[n].shape for n in SMALL]
    rs = -(-(sum(math.prod(s) for s in small_shapes) + 1) // (PACK_COLS * SLAB_ALIGN)) * SLAB_ALIGN
    red = _all_reduce_small(_pack_flat([GS[n] for n in SMALL] + [loss_part.reshape(1)], rs))
    loss = _unpack_flat(red, small_shapes + [(1,)])[-1][0]
    pk = lambda pre: _pack_flat([a[pre + n] for n in SMALL], rs)
    for dst, buf in zip((grads, delta, new_m, new_v), _adam(pk(""), red, pk("m_"), pk("v_"), "adamw_small")):
        dst.update(zip(SMALL, _unpack_flat(buf, small_shapes)))

    return (loss, gx[None], *[grads[n] for n in WEIGHTS], *[delta[n] for n in WEIGHTS], *[new_m[n] for n in WEIGHTS],
            *[new_v[n] for n in WEIGHTS])
```

```python
import functools
import math

import jax
import jax.numpy as jnp
from jax import lax
from jax.experimental import pallas as pl
from jax.experimental.pallas import tpu as pltpu

F32 = jnp.float32
BF16 = jnp.bfloat16
MXU_DTYPE = jnp.bfloat16
HIGHEST = lax.Precision.HIGHEST
MESH = pl.DeviceIdType.MESH

D_MODEL = 1024
DEPTH = 4
HEADS = 8
Q_LORA = 256
KV_LORA = 128
NOPE = 64
ROPE = 32
VDIM = 64
ROPE_THETA = 10000.0
SC_DIM = 256
SSD_HEADS = 4
SSD_HEAD_DIM = 64
SSD_STATE = 128
SSD_DIM = 256
SSD_CONV_DIM = 768
SSD_CHUNK = 128
FFN_DIM = 2816
NORM_EPS = 1e-6
QK_SCALE = (NOPE + ROPE) ** -0.5
LANE = 128
HP = 128
FLASH_HEADS = 8
FLASH_HEADS_FWD = 8
FLASH_BLOCK = 512

ZIN = 2560
Z_CQ, Z_CKV, Z_KR, Z_SCB, Z_SCC, Z_SCH, Z_SSZ, Z_XBC, Z_DT = 0, 256, 384, 512, 768, 1024, 1280, 1536, 2304
KR_LANE = 64
FFN_TILE = 256
FFN_ROWS = 2048
ROW_BLOCK = 512

ADAM_LR, ADAM_B1, ADAM_B2, ADAM_EPS, ADAM_WD, ADAM_STEP = 0.001, 0.9, 0.999, 1e-08, 0.01, 10

PACK_COLS = 1024


def _tile(n, pref):
    if n <= pref:
        return n
    t = (pref // LANE) * LANE
    while t >= LANE:
        if n % t == 0:
            return t
        t -= LANE
    raise ValueError(f"no tile for {n}")


MM_TM, MM_TN, MM_TK = 1024, 1408, 1536


def _mm(a, b, mode, out_dtype, name, tm=None, tn=MM_TN, tkmax=MM_TK):
    pair = isinstance(a, tuple)
    a_list = list(a) if pair else [a]
    layer = None
    if isinstance(b, tuple):
        b, layer = b
    bshape = b.shape[-2:]
    if mode == "nn":
        (M, Ka), (_, N) = a_list[0].shape, bshape
    elif mode == "nt":
        (M, Ka), (N, _) = a_list[0].shape, bshape
    else:
        (Ka, M), (_, N) = a_list[0].shape, bshape
    tm = (MM_TN if mode == "tn" else MM_TM) if tm is None else tm
    tm, tn, tk = _tile(M, tm), _tile(N, tn), _tile(Ka, tkmax)
    nka = Ka // tk
    nk = nka * len(a_list)

    def bspec(shape, index):
        if layer is None:
            return pl.BlockSpec(shape, index)
        return pl.BlockSpec((None,) + shape, lambda i, j, k: (layer,) + index(i, j, k))

    if mode == "nn":
        a_specs = [pl.BlockSpec((tm, tk), lambda i, j, k: (i, jnp.minimum(k, nka - 1))),
                   pl.BlockSpec((tm, tk), lambda i, j, k: (i, jnp.maximum(k - nka, 0)))][:len(a_list)]
        b_spec = bspec((tk, tn), lambda i, j, k: (k, j))
        dims = NN
    elif mode == "nt":
        a_specs = [pl.BlockSpec((tm, tk), lambda i, j, k: (i, jnp.minimum(k, nka - 1))),
                   pl.BlockSpec((tm, tk), lambda i, j, k: (i, jnp.maximum(k - nka, 0)))][:len(a_list)]
        b_spec = bspec((tn, tk), lambda i, j, k: (j, k))
        dims = NT
    else:
        a_specs = [pl.BlockSpec((tk, tm), lambda i, j, k: (k, i))]
        b_spec = pl.BlockSpec((tk, tn), lambda i, j, k: (k, j))
        dims = TN
    na = len(a_list)

    def body(*refs):
        a_refs, b_ref, o_ref = refs[:na], refs[na], refs[na + 1]
        k = pl.program_id(2)

        def prod(a_ref):
            return lax.dot_general(a_ref[...].astype(MXU_DTYPE), b_ref[...].astype(MXU_DTYPE), dims, preferred_element_type=F32)

        if nk == 1:
            o_ref[...] = prod(a_refs[0]).astype(o_ref.dtype)
            return
        acc_ref = refs[na + 2]

        @pl.when(k == 0)
        def _():
            acc_ref[...] = prod(a_refs[0])

        @pl.when((k > 0) & (k < nka))
        def _():
            acc_ref[...] += prod(a_refs[0])

        if pair:
            @pl.when(k >= nka)
            def _():
                acc_ref[...] += prod(a_refs[1])

        @pl.when(k == nk - 1)
        def _():
            o_ref[...] = acc_ref[...].astype(o_ref.dtype)

    return pl.pallas_call(
        body, name=name, grid=(M // tm, N // tn, nk),
        in_specs=a_specs + [b_spec], out_specs=pl.BlockSpec((tm, tn), lambda i, j, k: (i, j)),
        out_shape=jax.ShapeDtypeStruct((M, N), out_dtype),
        scratch_shapes=[pltpu.VMEM((tm, tn), F32)] if nk > 1 else [],
        compiler_params=pltpu.CompilerParams(dimension_semantics=("parallel", "parallel", "arbitrary")),
    )(*a_list, b)


HALO = 8


def _const(j, v):
    return v


def _rows(fn, T, tm, ins, consts, outs, accs, name, ncol=1):
    n = T // tm
    hb = tm // HALO
    last = T // HALO - 1
    in_specs, args = [], []
    for arr, bc, cb, kind in ins:
        if isinstance(kind, int):
            in_specs.append(pl.BlockSpec((tm, bc), lambda j, i, cb=cb, off=kind: (i + off, cb(j))))
        elif kind == "cur":
            in_specs.append(pl.BlockSpec((tm, bc), lambda j, i, cb=cb: (i, cb(j))))
        elif kind == "prev":
            in_specs.append(pl.BlockSpec((HALO, bc), lambda j, i, cb=cb: (jnp.maximum(i * hb - 1, 0), cb(j))))
        else:
            in_specs.append(pl.BlockSpec((HALO, bc), lambda j, i, cb=cb: (jnp.minimum((i + 1) * hb, last), cb(j))))
        args.append(arr)
    for arr, bc, cb in consts:
        in_specs.append(pl.BlockSpec((arr.shape[0], bc), lambda j, i, cb=cb: (0, cb(j))))
        args.append(arr)
    out_specs, out_shape = [], []
    for tc, dt, bc, cb in outs:
        out_specs.append(pl.BlockSpec((tm, bc), lambda j, i, cb=cb: (i, cb(j))))
        out_shape.append(jax.ShapeDtypeStruct((T, tc), dt))
    for r, tc, bc, cb in accs:
        out_specs.append(pl.BlockSpec((r, bc), lambda j, i, cb=cb: (0, cb(j))))
        out_shape.append(jax.ShapeDtypeStruct((r, tc), F32))
    nin, nout, nacc = len(args), len(outs), len(accs)

    def body(*refs):
        i = pl.program_id(1)
        res = fn(i, n, *[r[...] for r in refs[:nin]])
        for r, v in zip(refs[nin:nin + nout], res[:nout]):
            r[...] = v.astype(r.dtype)
        if nacc:
            acc_refs = refs[nin + nout:nin + nout + nacc]

            @pl.when(i == 0)
            def _():
                for r in acc_refs:
                    r[...] = jnp.zeros_like(r)

            for r, v in zip(acc_refs, res[nout:]):
                r[...] += v.astype(F32)

    res = pl.pallas_call(
        body, name=name, grid=(ncol, n), in_specs=in_specs, out_specs=out_specs, out_shape=out_shape,
        compiler_params=pltpu.CompilerParams(dimension_semantics=("arbitrary", "arbitrary")),
    )(*args)
    return res


def _cur(arr, bc=None, blk=0):
    bc = arr.shape[1] if bc is None else bc
    return (arr, bc, functools.partial(_const, v=blk), "cur")


def _halo(arr, kind, bc=None, blk=0):
    bc = arr.shape[1] if bc is None else bc
    return (arr, bc, functools.partial(_const, v=blk), kind)


def _cst(arr):
    return (arr, arr.shape[1], functools.partial(_const, v=0))


def _out(cols, dt):
    return (cols, dt, cols, functools.partial(_const, v=0))


def _acc(rows, cols):
    return (rows, cols, cols, functools.partial(_const, v=0))


def _rms(x, w):
    return x * lax.rsqrt(jnp.mean(x * x, axis=-1, keepdims=True) + NORM_EPS) * w


def _sigmoid(x):
    return 0.5 * jnp.tanh(0.5 * x) + 0.5


def _silu(x):
    return x * _sigmoid(x)


def _dsilu(x):
    s = _sigmoid(x)
    return s * (1.0 + x * (1.0 - s))


def _softplus(x):
    return jnp.maximum(x, 0.0) + jnp.log1p(jnp.exp(-jnp.abs(x)))


def _shift(a, k):
    return pltpu.roll(a, k % a.shape[0], 0)


def _lroll(a, k):
    return pltpu.roll(a, k % a.shape[1], 1)


def _vjp_wrap(f, nrow, nconst, add_first=False):
    def g(i, n, *vals):
        rows, consts, mid = vals[:nrow], vals[len(vals) - nconst:], vals[nrow:len(vals) - nconst]
        cots = mid[:-1] if add_first else mid
        outs, pull = jax.vjp(f, *rows, *consts)
        grads = list(pull(tuple(c.astype(o.dtype) for c, o in zip(cots, outs))))
        if add_first:
            grads[0] = grads[0] + mid[-1]
        return tuple(grads)
    return g


def _rows_vjp(f, T, tm, rows, consts, cots, out_dtypes, name):
    return _rows(_vjp_wrap(f, len(rows), len(consts)), T, tm, [_cur(r) for r in rows] + [_cur(c) for c in cots],
                 [_cst(c) for c in consts], [_out(r.shape[1], dt) for r, dt in zip(rows, out_dtypes)],
                 [_acc(1, c.shape[1]) for c in consts], name)


def _f_premix(x, g):
    return (_rms(x, g),)


def _f_mla_pre(cq, ckv, qn, kvn):
    return _rms(cq, qn), _rms(ckv, kvn)


def _f_ssd_gate(y, z, nw):
    return (_rms(y * _silu(z), nw),)


def _f_post_mix(x, mixed, gpost, gffn):
    x1 = x + _rms(mixed, gpost)
    return x1, _rms(x1, gffn)


def _f_post_ffn(x1, d, gpost):
    return (x1 + _rms(d, gpost),)


def _rope_fwd(v, cosf, sina, sinb):
    return v * cosf + _lroll(v, -16) * sina + _lroll(v, 16) * sinb


def _rope_bwd(g, cosf, sina, sinb):
    return g * cosf + _lroll(g * sina, 16) + _lroll(g * sinb, -16)


def _k_rope_fwd(i, n, qpad, kvpad, kr, cosf, sina, sinb):
    qs, ks = [], []
    krr = _rope_fwd(kr, cosf, sina, sinb)
    for h in range(HEADS):
        sl = slice(h * HP, (h + 1) * HP)
        qs.append(_rope_fwd(qpad[:, sl], cosf, sina, sinb))
        ks.append(kvpad[:, sl].astype(F32) + krr)
    return jnp.concatenate(qs, axis=1), jnp.concatenate(ks, axis=1)


def _k_rope_bwd(i, n, dq, dk, dv, cosf, sina, sinb):
    lane = lax.broadcasted_iota(jnp.int32, (1, HP), 1)
    rmask = ((lane >= KR_LANE) & (lane < KR_LANE + ROPE)).astype(F32)
    dqs, dks = [], []
    dkr = jnp.zeros((dq.shape[0], HP), F32)
    for h in range(HEADS):
        sl = slice(h * HP, (h + 1) * HP)
        dqs.append(_rope_bwd(dq[:, sl], cosf, sina, sinb))
        dkh = dk[:, sl]
        dkr = dkr + dkh * rmask
        dks.append(dkh * (1.0 - rmask))
    dkr = _rope_bwd(dkr, cosf, sina, sinb) * rmask
    return jnp.concatenate(dqs, axis=1), jnp.concatenate(dks + [dv], axis=1), dkr


def _k_sconv_fwd(i, n, b, c, h, cp, hp, w):
    m = b.shape[0]
    up = jnp.where(i > 0, cp * hp, 0.0)
    ue = jnp.concatenate([up, c * h], axis=0)
    conv = w[2:3] * ue + w[1:2] * _shift(ue, 1) + w[0:1] * _shift(ue, 2)
    return (b * conv[HALO:],)


def _k_sconv_bwd(i, n, b, c, h, dy, cp, hp, bn, dyn, w):
    m = b.shape[0]
    up = jnp.where(i > 0, cp * hp, 0.0)
    ue = jnp.concatenate([up, c * h], axis=0)
    u1, u2 = _shift(ue, 1), _shift(ue, 2)
    conv = (w[2:3] * ue + w[1:2] * u1 + w[0:1] * u2)[HALO:]
    dc_cur = dy * b
    dce = jnp.concatenate([dc_cur, jnp.where(i < n - 1, dyn * bn, 0.0)], axis=0)
    du = (w[2:3] * dce + w[1:2] * _shift(dce, -1) + w[0:1] * _shift(dce, -2))[:m]
    dw = jnp.concatenate([
        jnp.sum(dc_cur * u2[HALO:], axis=0, keepdims=True),
        jnp.sum(dc_cur * u1[HALO:], axis=0, keepdims=True),
        jnp.sum(dc_cur * ue[HALO:], axis=0, keepdims=True),
        jnp.zeros((HALO - 3, b.shape[1]), F32)], axis=0)
    return dy * conv, du * h, du * c, dw


def _conv4(ue, w):
    return w[3:4] * ue + w[2:3] * _shift(ue, 1) + w[1:2] * _shift(ue, 2) + w[0:1] * _shift(ue, 3)


def _k_ssdconv_fwd(i, n, u, up, w, bias):
    ue = jnp.concatenate([jnp.where(i > 0, up, 0.0), u], axis=0)
    return (_silu(_conv4(ue, w)[HALO:] + bias),)


def _k_ssdconv_bwd(i, n, u, dout, up, un, doutn, w, bias):
    m = u.shape[0]
    ue = jnp.concatenate([jnp.where(i > 0, up, 0.0), u, un], axis=0)
    u1, u2, u3 = _shift(ue, 1), _shift(ue, 2), _shift(ue, 3)
    pre = (w[3:4] * ue + w[2:3] * u1 + w[1:2] * u2 + w[0:1] * u3)[HALO:] + bias
    doe = jnp.concatenate([dout, jnp.where(i < n - 1, doutn, 0.0)], axis=0)
    dpre = doe * _dsilu(pre)
    du = (w[3:4] * dpre + w[2:3] * _shift(dpre, -1) + w[1:2] * _shift(dpre, -2) + w[0:1] * _shift(dpre, -3))[:m]
    dp = dpre[:m]
    cur = slice(HALO, HALO + m)
    dw = jnp.concatenate([
        jnp.sum(dp * u3[cur], axis=0, keepdims=True),
        jnp.sum(dp * u2[cur], axis=0, keepdims=True),
        jnp.sum(dp * u1[cur], axis=0, keepdims=True),
        jnp.sum(dp * ue[cur], axis=0, keepdims=True),
        jnp.zeros((HALO - 4, u.shape[1]), F32)], axis=0)
    db = jnp.sum(dp, axis=0, keepdims=True)
    return du, dw, db


def _conv3(ue, w):
    return w[2:3] * ue + w[1:2] * _shift(ue, 1) + w[0:1] * _shift(ue, 2)


def _k_ffnact_fwd(i, n, ug, uu, ugp, uup, wg, wu, bg, bu):
    gate = _conv3(jnp.concatenate([jnp.where(i > 0, ugp, 0.0), ug], axis=0), wg)[HALO:] + bg
    upv = _conv3(jnp.concatenate([jnp.where(i > 0, uup, 0.0), uu], axis=0), wu)[HALO:] + bu
    return (_silu(gate) * upv,)


def _k_ffnact_bwd(i, n, ug, uu, dact, ugp, uup, ugn, uun, dactn, wg, wu, bg, bu):
    m = ug.shape[0]
    cur = slice(HALO, HALO + m)

    def taps(p, c, nx):
        e = jnp.concatenate([jnp.where(i > 0, p, 0.0), c, nx], axis=0)
        return e, _shift(e, 1), _shift(e, 2)

    def back(d, w):
        return (w[2:3] * d + w[1:2] * _shift(d, -1) + w[0:1] * _shift(d, -2))[:m]

    def wgrad(d, t):
        return jnp.concatenate([jnp.sum(d[:m] * t[2][cur], axis=0, keepdims=True), jnp.sum(d[:m] * t[1][cur], axis=0, keepdims=True),
                                jnp.sum(d[:m] * t[0][cur], axis=0, keepdims=True), jnp.zeros((HALO - 3, d.shape[1]), F32)], axis=0)

    tg, tu = taps(ugp, ug, ugn), taps(uup, uu, uun)
    gate = (wg[2:3] * tg[0] + wg[1:2] * tg[1] + wg[0:1] * tg[2])[HALO:] + bg
    upv = (wu[2:3] * tu[0] + wu[1:2] * tu[1] + wu[0:1] * tu[2])[HALO:] + bu
    dae = jnp.concatenate([dact, jnp.where(i < n - 1, dactn, 0.0)], axis=0)
    sg = _sigmoid(gate)
    dg = dae * upv * (sg * (1.0 + gate * (1.0 - sg)))
    dup = dae * (gate * sg)
    return (back(dg, wg), back(dup, wu), wgrad(dg, tg), wgrad(dup, tu),
            jnp.sum(dg[:m], axis=0, keepdims=True), jnp.sum(dup[:m], axis=0, keepdims=True))


def _k_loss(i, n, y, tgt):
    e = y - tgt
    part = 0.5 * jnp.sum(jnp.sum(e * e, axis=1, keepdims=True) / D_MODEL, axis=0, keepdims=True)
    return e * (1.0 / D_MODEL), jnp.broadcast_to(part, (1, LANE))


def _k_adam(i, n, w, g, m, v):
    m = ADAM_B1 * m + (1.0 - ADAM_B1) * g
    v = ADAM_B2 * v + (1.0 - ADAM_B2) * (g * g)
    m_hat = m / (1.0 - ADAM_B1 ** ADAM_STEP)
    v_hat = v / (1.0 - ADAM_B2 ** ADAM_STEP)
    delta = -ADAM_LR * (m_hat / (jnp.sqrt(v_hat) + ADAM_EPS) + ADAM_WD * w)
    return g, delta, m, v


def _dotf(a, b, dims):
    return lax.dot_general(a.astype(MXU_DTYPE), b.astype(MXU_DTYPE), dims, preferred_element_type=F32)


NN = (((1,), (0,)), ((), ()))
NT = (((1,), (1,)), ((), ()))
TN = (((0,), (0,)), ((), ()))


def _ssd_chunk(x0, x1, x2, x3, b0, b1, c0, c1, dtraw, p0, p1, p2, p3, dtb, alog, dsk):
    xs, bs, cs_, ps = (x0, x1, x2, x3), (b0, b1), (c0, c1), (p0, p1, p2, p3)
    L = dtraw.shape[0]
    dt = _softplus(dtraw + dtb)
    adt = dt * (-jnp.exp(alog))
    row = lax.broadcasted_iota(jnp.int32, (L, L), 0)
    col = lax.broadcasted_iota(jnp.int32, (L, L), 1)
    tril = row >= col
    cum = jnp.dot(tril.astype(F32), adt, precision=HIGHEST, preferred_element_type=F32)
    cum_t = cum.T
    lane = lax.broadcasted_iota(jnp.int32, (1, LANE), 1)
    sub = lax.broadcasted_iota(jnp.int32, (LANE, 1), 0)
    lastcol = (lax.broadcasted_iota(jnp.int32, (1, L), 1) == L - 1).astype(F32)
    ys, news = [], []
    for h in range(SSD_HEADS):
        g = h // (SSD_HEADS // 2)
        oh = (lane == h).astype(F32)
        dth = jnp.sum(dt * oh, axis=1, keepdims=True)
        csh = jnp.sum(cum * oh, axis=1, keepdims=True)
        csr = jnp.sum(cum_t * (sub == h).astype(F32), axis=0, keepdims=True)
        cl = jnp.sum(csr * lastcol, axis=1, keepdims=True)
        dskh = jnp.sum(dsk * oh, axis=1, keepdims=True)
        x, bm, cm, prev = xs[h], bs[g], cs_[g], ps[h]
        xdt = x * dth
        decay = jnp.exp(jnp.where(tril, csh - csr, -jnp.inf))
        scores = _dotf(cm, bm, NT) * decay
        y_diag = _dotf(scores, xdt, NN)
        bd = bm * jnp.exp(cl - csh)
        cst = _dotf(xdt, bd, TN)
        news.append(prev * jnp.exp(cl) + cst)
        y_off = _dotf(cm, prev, NT) * jnp.exp(csh)
        ys.append(y_diag + y_off + x * dskh)
    return (*ys, *news)


SSD_STEP = 2


def _ssd_operands(x_ref, dt_ref, par_ref, prev, rows):
    xs = [x_ref[rows, h * SSD_HEAD_DIM:(h + 1) * SSD_HEAD_DIM] for h in range(SSD_HEADS)]
    bs = [x_ref[rows, SSD_DIM + g * SSD_STATE:SSD_DIM + (g + 1) * SSD_STATE] for g in range(2)]
    cs_ = [x_ref[rows, SSD_DIM + 2 * SSD_STATE + g * SSD_STATE:SSD_DIM + 2 * SSD_STATE + (g + 1) * SSD_STATE] for g in range(2)]
    return (*xs, *bs, *cs_, dt_ref[rows, :], *prev, par_ref[0:1, :], par_ref[1:2, :], par_ref[2:3, :])


def _ssd_fwd(xbc, dtraw, par, T, dt_blk=0):
    L = SSD_CHUNK
    nc = T // L
    P = SSD_HEAD_DIM
    U = SSD_STEP if nc % SSD_STEP == 0 else 1

    def body(x_ref, dt_ref, par_ref, y_ref, st_ref, state):
        @pl.when(pl.program_id(0) == 0)
        def _():
            state[...] = jnp.zeros_like(state)

        for u in range(U):
            rows = slice(u * L, (u + 1) * L)
            st_ref[u] = state[...]
            prev = [state[h * P:(h + 1) * P, :] for h in range(SSD_HEADS)]
            res = _ssd_chunk(*_ssd_operands(x_ref, dt_ref, par_ref, prev, rows))
            for h in range(SSD_HEADS):
                y_ref[rows, h * P:(h + 1) * P] = res[h]
                state[h * P:(h + 1) * P, :] = res[SSD_HEADS + h]

    return pl.pallas_call(
        body, name="ssd_scan_fwd", grid=(nc // U,),
        in_specs=[pl.BlockSpec((U * L, SSD_CONV_DIM), lambda c: (c, 0)), pl.BlockSpec((U * L, LANE), lambda c: (c, dt_blk)),
                  pl.BlockSpec((8, LANE), lambda c: (0, 0))],
        out_specs=[pl.BlockSpec((U * L, SSD_DIM), lambda c: (c, 0)), pl.BlockSpec((U, SSD_DIM, SSD_STATE), lambda c: (c, 0, 0))],
        out_shape=[jax.ShapeDtypeStruct((T, SSD_DIM), F32), jax.ShapeDtypeStruct((nc, SSD_DIM, SSD_STATE), F32)],
        scratch_shapes=[pltpu.VMEM((SSD_DIM, SSD_STATE), F32)],
        compiler_params=pltpu.CompilerParams(dimension_semantics=("arbitrary",)),
    )(xbc, dtraw, par)


def _ssd_bwd(xbc, dtraw, par, states, dy, T, dt_blk=0):
    L = SSD_CHUNK
    nc = T // L
    P = SSD_HEAD_DIM
    U = SSD_STEP if nc % SSD_STEP == 0 else 1
    ns = nc // U

    def body(x_ref, dt_ref, par_ref, st_ref, dy_ref, dx_ref, ddt_ref, dpar_ref, dstate):
        @pl.when(pl.program_id(0) == 0)
        def _():
            dstate[...] = jnp.zeros_like(dstate)
            dpar_ref[...] = jnp.zeros_like(dpar_ref)

        for u in reversed(range(U)):
            rows = slice(u * L, (u + 1) * L)
            prev = [st_ref[u, h * P:(h + 1) * P, :] for h in range(SSD_HEADS)]
            prim = _ssd_operands(x_ref, dt_ref, par_ref, prev, rows)
            _, pull = jax.vjp(_ssd_chunk, *prim)
            cots = tuple(dy_ref[rows, h * P:(h + 1) * P] for h in range(SSD_HEADS)) + tuple(
                dstate[h * P:(h + 1) * P, :] for h in range(SSD_HEADS))
            g = pull(cots)
            for h in range(SSD_HEADS):
                dx_ref[rows, h * P:(h + 1) * P] = g[h]
                dstate[h * P:(h + 1) * P, :] = g[9 + h]
            for k in range(2):
                dx_ref[rows, SSD_DIM + k * SSD_STATE:SSD_DIM + (k + 1) * SSD_STATE] = g[4 + k]
                dx_ref[rows, SSD_DIM + 2 * SSD_STATE + k * SSD_STATE:SSD_DIM + 2 * SSD_STATE + (k + 1) * SSD_STATE] = g[6 + k]
            ddt_ref[rows, :] = g[8]
            for r in range(3):
                dpar_ref[r:r + 1, :] += g[13 + r]

    rev = lambda c: (ns - 1 - c, 0)
    return pl.pallas_call(
        body, name="ssd_scan_bwd", grid=(ns,),
        in_specs=[pl.BlockSpec((U * L, SSD_CONV_DIM), rev), pl.BlockSpec((U * L, LANE), lambda c: (ns - 1 - c, dt_blk)),
                  pl.BlockSpec((8, LANE), lambda c: (0, 0)),
                  pl.BlockSpec((U, SSD_DIM, SSD_STATE), lambda c: (ns - 1 - c, 0, 0)), pl.BlockSpec((U * L, SSD_DIM), rev)],
        out_specs=[pl.BlockSpec((U * L, SSD_CONV_DIM), rev), pl.BlockSpec((U * L, LANE), rev), pl.BlockSpec((8, LANE), lambda c: (0, 0))],
        out_shape=[jax.ShapeDtypeStruct((T, SSD_CONV_DIM), F32), jax.ShapeDtypeStruct((T, LANE), F32),
                   jax.ShapeDtypeStruct((8, LANE), F32)],
        scratch_shapes=[pltpu.VMEM((SSD_DIM, SSD_STATE), F32)],
        compiler_params=pltpu.CompilerParams(dimension_semantics=("arbitrary",)),
    )(xbc, dtraw, par, states, dy)


def _causal_pairs(nq, by_query):
    if by_query:
        pairs = [(i, j) for i in range(nq) for j in range(i + 1)]
    else:
        pairs = [(i, j) for j in range(nq) for i in range(j, nq)]
    return jnp.asarray([p[0] for p in pairs], jnp.int32), jnp.asarray([p[1] for p in pairs], jnp.int32)


def _flash_fwd(q, k, kv, T, carry=()):
    tq = tk = min(FLASH_BLOCK, T)
    nq = T // tq
    G = FLASH_HEADS_FWD
    rep = tk // HP
    nc = len(carry)
    qi, kj = _causal_pairs(nq, by_query=True)
    nh, nt = HEADS // G, qi.shape[0]

    def body(qi_ref, kj_ref, q_ref, k_ref, v_ref, *rest):
        w_refs, o_ref, g_refs = rest[:nc], rest[nc], rest[nc + 1:2 * nc + 1]
        m_ref, l_ref, acc_ref = rest[2 * nc + 1:2 * nc + 4]
        h, t = pl.program_id(0), pl.program_id(1)
        i, j = qi_ref[t], kj_ref[t]
        if nc:
            plan = lambda: _ag_plan(w_refs, g_refs, rest[2 * nc + 4:])

            @pl.when((h == 0) & (t == 0))
            def _():
                for cp in plan()[0]:
                    cp.start()

        @pl.when(j == 0)
        def _():
            m_ref[...] = jnp.full_like(m_ref, -jnp.inf)
            l_ref[...] = jnp.zeros_like(l_ref)
            acc_ref[...] = jnp.zeros_like(acc_ref)

        def step(diagonal):
            for g in range(G):
                sl = slice(g * HP, (g + 1) * HP)
                s = _dotf(q_ref[:, sl], k_ref[:, sl], NT) * QK_SCALE
                if diagonal:
                    rows = lax.broadcasted_iota(jnp.int32, (tq, tk), 0)
                    cols = lax.broadcasted_iota(jnp.int32, (tq, tk), 1)
                    s = jnp.where(rows >= cols, s, -jnp.inf)
                m_old = m_ref[:, sl]
                m_new = jnp.maximum(m_old, jnp.max(s, axis=1, keepdims=True))
                p = jnp.exp(s - jnp.tile(m_new, (1, rep)))
                alpha = jnp.exp(m_old - m_new)
                l_ref[:, sl] = alpha * l_ref[:, sl] + jnp.sum(p, axis=1, keepdims=True)
                acc_ref[:, sl] = alpha * acc_ref[:, sl] + _dotf(p, v_ref[:, sl], NN)
                m_ref[:, sl] = m_new

        @pl.when(j < i)
        def _():
            step(False)

        @pl.when(j == i)
        def _():
            step(True)
            lane = lax.broadcasted_iota(jnp.int32, (tq, HP), 1)
            for g in range(G):
                sl = slice(g * HP, (g + 1) * HP)
                l = l_ref[:, sl]
                o_ref[:, sl] = jnp.where(lane < VDIM, acc_ref[:, sl] / l, m_ref[:, sl] + jnp.log(l))

        if nc:
            @pl.when(h * nt + t == (3 * nh * nt) // 4)
            def _():
                _, lands, forwards, _ = plan()
                for land, fw in zip(lands, forwards):
                    land.wait_recv()
                    fw.start()

            @pl.when((h == nh - 1) & (t == nt - 1))
            def _():
                sends, _, forwards, finals = plan()
                for cp in finals:
                    cp.wait_recv()
                for cp in sends + forwards:
                    cp.wait_send()

    W = G * HP
    res = pl.pallas_call(
        body, name="mla_flash_fwd",
        grid_spec=pltpu.PrefetchScalarGridSpec(
            num_scalar_prefetch=2, grid=(nh, nt),
            in_specs=[pl.BlockSpec((tq, W), lambda h, t, qi, kj: (qi[t], h)),
                      pl.BlockSpec((tk, W), lambda h, t, qi, kj: (kj[t], h)),
                      pl.BlockSpec((tk, W), lambda h, t, qi, kj: (kj[t], HEADS // G + h))] + [ANY] * nc,
            out_specs=[pl.BlockSpec((tq, W), lambda h, t, qi, kj: (qi[t], h))] + [ANY] * nc,
            scratch_shapes=[pltpu.VMEM((tq, W), F32), pltpu.VMEM((tq, W), F32), pltpu.VMEM((tq, W), F32)] + (_ag_sems(nc) if nc else [])),
        out_shape=[jax.ShapeDtypeStruct((T, HEADS * HP), F32)] + [jax.ShapeDtypeStruct((N_CHIPS,) + w.shape, w.dtype) for w in carry],
        compiler_params=pltpu.CompilerParams(dimension_semantics=("arbitrary", "arbitrary")),
    )(qi, kj, q, k, kv, *carry)
    return res[0] if not nc else (res[0], [_own_slot(g, w) for g, w in zip(res[1:], carry)])


def _flash_bwd(q, k, kv, o, dycat, T, carry=()):
    tq = tk = min(FLASH_BLOCK, T)
    nq = T // tq
    G = FLASH_HEADS
    nc = len(carry)
    qi, kj = _causal_pairs(nq, by_query=False)
    nh, nt = HEADS // G, qi.shape[0]

    W = G * HP

    def body(qi_ref, kj_ref, q_ref, k_ref, v_ref, o_ref, do_ref, *rest):
        p_refs, (dq_out, dk_ref, dv_ref), part_refs = rest[:nc], rest[nc:nc + 3], rest[nc + 3:2 * nc + 3]
        dq_ref, dq_sem = rest[2 * nc + 3:2 * nc + 5]
        h, t = pl.program_id(0), pl.program_id(1)
        i, j = qi_ref[t], kj_ref[t]
        if nc:
            plan = lambda: _chip_plan(p_refs, part_refs, rest[2 * nc + 5:])

            @pl.when((h == 0) & (t == 0))
            def _():
                for cp in plan()[0]:
                    cp.start()

        @pl.when(t == 0)
        def _():
            dq_ref[...] = jnp.zeros_like(dq_ref)

        @pl.when(i == j)
        def _():
            dk_ref[...] = jnp.zeros_like(dk_ref)
            dv_ref[...] = jnp.zeros_like(dv_ref)

        def step(diagonal):
            r0 = pl.multiple_of(i * tq, tq)
            for g in range(G):
                sl = slice(g * HP, (g + 1) * HP)
                qv, kv, vv, ov, dov = q_ref[:, sl], k_ref[:, sl], v_ref[:, sl], o_ref[:, sl], do_ref[:, sl]
                s = _dotf(qv, kv, NT) * QK_SCALE
                p = jnp.exp(s - ov[:, VDIM:VDIM + 1])
                if diagonal:
                    rows = lax.broadcasted_iota(jnp.int32, (tq, tk), 0)
                    cols = lax.broadcasted_iota(jnp.int32, (tq, tk), 1)
                    p = jnp.where(rows >= cols, p, 0.0)
                dsum = jnp.sum(dov * ov, axis=1, keepdims=True)
                dv_ref[:, sl] += _dotf(p, dov, TN)
                dp = _dotf(dov, vv, NT)
                ds = p * (dp - dsum) * QK_SCALE
                dk_ref[:, sl] += _dotf(ds, qv, TN)
                dq_ref[pl.ds(r0, tq), sl] += _dotf(ds, kv, NN)

        @pl.when(i > j)
        def _():
            step(False)

        @pl.when(i == j)
        def _():
            step(True)

        @pl.when(t == nt - 1)
        def _():
            out = pltpu.make_async_copy(dq_ref, dq_out.at[:, pl.ds(pl.multiple_of(h * W, W), W)], dq_sem)
            out.start()
            out.wait()

        if nc:
            @pl.when((h == nh - 1) & (t == nt - 1))
            def _():
                sends, lands = plan()
                for cp in lands:
                    cp.wait_recv()
                for cp in sends:
                    cp.wait_send()

    qmap = lambda h, t, qi, kj: (qi[t], h)
    kmap = lambda h, t, qi, kj: (kj[t], h)
    vmap = lambda h, t, qi, kj: (kj[t], HEADS // G + h)
    res = pl.pallas_call(
        body, name="mla_flash_bwd",
        grid_spec=pltpu.PrefetchScalarGridSpec(
            num_scalar_prefetch=2, grid=(nh, nt),
            in_specs=[pl.BlockSpec((tq, W), qmap), pl.BlockSpec((tk, W), kmap), pl.BlockSpec((tk, W), vmap),
                      pl.BlockSpec((tq, W), qmap), pl.BlockSpec((tq, W), qmap)] + [ANY] * nc,
            out_specs=[ANY, pl.BlockSpec((tk, W), kmap), pl.BlockSpec((tk, W), kmap)] + [ANY] * nc,
            scratch_shapes=[pltpu.VMEM((T, W), F32), pltpu.SemaphoreType.DMA] + (_chip_sems(nc) if nc else [])),
        out_shape=[jax.ShapeDtypeStruct((T, HEADS * HP), F32)] * 3 + [jax.ShapeDtypeStruct(p.shape, p.dtype) for p in carry],
        compiler_params=pltpu.CompilerParams(dimension_semantics=("arbitrary", "arbitrary")),
    )(qi, kj, q, k, kv, o, dycat, *carry)
    return tuple(res[:3]) if not nc else (*res[:3], _chip_parts(res[3:], carry))


_IN_SRC = (0, 256, 384, 416, 672, 928, 1184, 1440, 2208, 2212)
_IN_DST = (Z_CQ, Z_CKV, Z_KR + KR_LANE, Z_SCB, Z_SCC, Z_SCH, Z_SSZ, Z_XBC, Z_DT)


def _pad_rows_in(w):
    ax = w.ndim - 2

    def zeros(n):
        return jnp.zeros(w.shape[:ax] + (n,) + w.shape[ax + 1:], w.dtype)

    def whole_tiles(p):
        n = p.shape[ax]
        return p if n % SLAB_ALIGN == 0 else jnp.pad(p, [(0, 0)] * ax + [(0, -n % SLAB_ALIGN), (0, 0)])

    parts, at = [], 0
    for s0, s1, d0 in zip(_IN_SRC[:-1], _IN_SRC[1:], _IN_DST):
        if d0 > at:
            parts.append(zeros(d0 - at))
        parts.append(whole_tiles(lax.slice_in_dim(w, s0, s1, axis=ax)))
        at = d0 + parts[-1].shape[ax]
    parts.append(zeros(ZIN - at))
    return jnp.concatenate(parts, axis=ax)


def _unpad_rows_in(w):
    ax = w.ndim - 2
    groups = list(zip(_IN_SRC[:-1], _IN_SRC[1:], _IN_DST))
    parts = [lax.slice_in_dim(w, d0, d0 + -(-(s1 - s0) // SLAB_ALIGN) * SLAB_ALIGN, axis=ax) for s0, s1, d0 in groups]
    return lax.slice_in_dim(jnp.concatenate(parts, axis=ax), 0, _IN_SRC[-1], axis=ax)


def _pad_heads(w, width):
    w = w.reshape(w.shape[:-1] + (HEADS, width))
    w = jnp.pad(w, [(0, 0)] * (w.ndim - 1) + [(0, HP - width)])
    return w.reshape(w.shape[:-2] + (HEADS * HP,))


def _unpad_heads(w, width):
    w = w.reshape(w.shape[:-1] + (HEADS, HP))[..., :width]
    return w.reshape(w.shape[:-2] + (HEADS * width,))


def _pad_kv(w):
    w = w.reshape(w.shape[:-1] + (HEADS, NOPE + VDIM))
    return jnp.concatenate([_pad_heads(w[..., :NOPE].reshape(w.shape[:-2] + (HEADS * NOPE,)), NOPE),
                            _pad_heads(w[..., NOPE:].reshape(w.shape[:-2] + (HEADS * VDIM,)), VDIM)], axis=-1)


def _unpad_kv(w):
    k = _unpad_heads(w[..., :HEADS * HP], NOPE).reshape(w.shape[:-1] + (HEADS, NOPE))
    v = _unpad_heads(w[..., HEADS * HP:], VDIM).reshape(w.shape[:-1] + (HEADS, VDIM))
    return jnp.concatenate([k, v], axis=-1).reshape(w.shape[:-1] + (HEADS * (NOPE + VDIM),))


def _pad_out_rows(w):
    lead, d = w.shape[:-2], w.shape[-1]
    att = w[..., :HEADS * VDIM, :].reshape(lead + (HEADS, VDIM, d))
    att = jnp.pad(att, [(0, 0)] * (att.ndim - 2) + [(0, HP - VDIM), (0, 0)]).reshape(lead + (HEADS * HP, d))
    return jnp.concatenate([att, w[..., HEADS * VDIM:, :]], axis=-2)


def _unpad_out_rows(w):
    lead, d = w.shape[:-2], w.shape[-1]
    att = w[..., :HEADS * HP, :].reshape(lead + (HEADS, HP, d))[..., :VDIM, :].reshape(lead + (HEADS * VDIM, d))
    return jnp.concatenate([att, w[..., HEADS * HP:, :]], axis=-2)


def _rows8(w):
    return jnp.pad(w.astype(F32), [(0, 0)] * (w.ndim - 2) + [(0, 8 - w.shape[-2]), (0, 0)])


def _row8(*vecs):
    c = vecs[0].shape[-1]
    return jnp.concatenate([v.reshape(1, c).astype(F32) for v in vecs] + [jnp.zeros((8 - len(vecs), c), F32)], axis=0)


def _rope_tables(positions):
    inv_freq = 1.0 / (ROPE_THETA ** (jnp.arange(0, ROPE, 2, dtype=F32) / ROPE))
    ang = positions.astype(F32)[:, None] * inv_freq
    cos, sin = jnp.cos(ang), jnp.sin(ang)
    T = positions.shape[0]
    half = ROPE // 2
    one = jnp.ones((T, KR_LANE), F32)
    zero = jnp.zeros((T, KR_LANE), F32)
    tail1 = jnp.ones((T, HP - KR_LANE - ROPE), F32)
    tail0 = jnp.zeros((T, HP - KR_LANE - ROPE), F32)
    z16 = jnp.zeros((T, half), F32)
    cosf = jnp.concatenate([one, cos, cos, tail1], axis=1)
    sina = jnp.concatenate([zero, -sin, z16, tail0], axis=1)
    sinb = jnp.concatenate([zero, z16, sin, tail0], axis=1)
    return cosf, sina, sinb


def _kernel_weights(W):
    c = lambda a: a.astype(MXU_DTYPE)
    forms = dict(
        w_in=("w_in", lambda w: c(_pad_rows_in(w))),
        w_q=("mla_w_q_up", lambda w: c(_pad_heads(w, NOPE + ROPE))),
        w_kv=("mla_w_kv_up", lambda w: c(_pad_kv(w))),
        w_out=("w_out", lambda w: c(_pad_out_rows(w))),
        w_up=("ffn_w_up", c),
        w_down=("ffn_w_down", c),
        sc_w=("sc_conv_w", _rows8),
        ssd_w=("ssd_conv_w", _rows8),
        ffn_w=("ffn_conv_w", _rows8),
    )
    return {k: f(W[n]) for k, (n, f) in forms.items() if n in W}


def _layer_weights(KW, l):
    return {k: (v[l] if k in ("sc_w", "ssd_w", "ffn_w") else (v, l)) for k, v in KW.items()}


def _local_step(x, positions, target, W, S, ex=None):
    T = x.shape[0]
    tm = min(ROW_BLOCK, T)
    tm_ffn = min(FFN_ROWS, T)
    cosf, sina, sinb = _rope_tables(positions)
    if ex is None:
        KW = _kernel_weights(W)
    else:
        early = _all_gather_weights(ex.shard(0, "early"))
    saved = []
    xl = x
    for l in range(DEPTH):
        lw = _layer_weights(KW, l) if ex is None else _kernel_weights(ex.weights(early, "early"))
        g_pre = S["norm_mix_pre"][l].reshape(1, -1)
        g_post = S["norm_mix_post"][l].reshape(1, -1)
        g_fpre = S["norm_ffn_pre"][l].reshape(1, -1)
        g_fpost = S["norm_ffn_post"][l].reshape(1, -1)
        qn = S["mla_q_norm"][l].reshape(1, -1)
        kvn = S["mla_kv_norm"][l].reshape(1, -1)
        ssd_b = S["ssd_conv_b"][l].reshape(1, -1)
        ssd_par = _row8(jnp.pad(S["ssd_dt_bias"][l], (0, LANE - SSD_HEADS)), jnp.pad(S["ssd_a_log"][l], (0, LANE - SSD_HEADS)),
                        jnp.pad(S["ssd_d"][l], (0, LANE - SSD_HEADS)))
        ssd_nw = S["ssd_norm"][l].reshape(1, -1)
        ffn_b = S["ffn_conv_b"][l].reshape(1, -1)

        (h1,) = _rows(lambda i, n, *v: _f_premix(*v), T, tm, [_cur(xl)], [_cst(g_pre)], [_out(D_MODEL, BF16)], [], "pre_mix_norm")
        zin = _mm(h1, lw["w_in"], "nt", F32, "mm_in")
        qlat, kvlat = _rows(lambda i, n, *v: _f_mla_pre(*v), T, tm, [_cur(zin, Q_LORA, 0), _cur(zin, KV_LORA, Z_CKV // KV_LORA)],
                            [_cst(qn), _cst(kvn)], [_out(Q_LORA, BF16), _out(KV_LORA, BF16)], [], "mla_pre_norm")
        qpad = _mm(qlat, lw["w_q"], "nn", F32, "mm_q_up")
        kvpad = _mm(kvlat, lw["w_kv"], "nn", BF16, "mm_kv_up")
        qr, kr = _rows(_k_rope_fwd, T, tm, [_cur(qpad), _cur(kvpad, HEADS * HP, 0), _cur(zin, LANE, Z_KR // LANE),
                                            _cur(cosf), _cur(sina), _cur(sinb)], [],
                       [_out(HEADS * HP, BF16), _out(HEADS * HP, BF16)], [], "mla_rope")
        if ex is None:
            o = _flash_fwd(qr, kr, kvpad, T)
        else:
            nlate = len(ex.layouts["late"])
            o, got = _flash_fwd(qr, kr, kvpad, T, carry=ex.shard(l, "late") + (ex.shard(l + 1, "early") if l + 1 < DEPTH else []))
            lw.update(_kernel_weights(ex.weights(got[:nlate], "late")))
            early = got[nlate:]
        (yconv,) = _rows(_k_sconv_fwd, T, tm, [_cur(zin, SC_DIM, Z_SCB // SC_DIM), _cur(zin, SC_DIM, Z_SCC // SC_DIM),
                                               _cur(zin, SC_DIM, Z_SCH // SC_DIM), _halo(zin, "prev", SC_DIM, Z_SCC // SC_DIM),
                                               _halo(zin, "prev", SC_DIM, Z_SCH // SC_DIM)], [_cst(lw["sc_w"])],
                         [_out(SC_DIM, F32)], [], "short_conv_fwd")
        (xbc,) = _rows(_k_ssdconv_fwd, T, tm, [_cur(zin, SSD_CONV_DIM, Z_XBC // SSD_CONV_DIM),
                                               _halo(zin, "prev", SSD_CONV_DIM, Z_XBC // SSD_CONV_DIM)],
                       [_cst(lw["ssd_w"]), _cst(ssd_b)], [_out(SSD_CONV_DIM, F32)], [], "ssd_conv_fwd")
        yscan, states = _ssd_fwd(xbc, zin, ssd_par, T, Z_DT // LANE)
        (yssd,) = _rows(lambda i, n, *v: _f_ssd_gate(*v), T, tm, [_cur(yscan), _cur(zin, SSD_DIM, Z_SSZ // SSD_DIM)], [_cst(ssd_nw)],
                        [_out(SSD_DIM, F32)], [], "ssd_gate_fwd")
        ycat = jnp.concatenate([o.astype(BF16), yconv.astype(BF16), yssd.astype(BF16)], axis=1)
        mixed = _mm(ycat, lw["w_out"], "nn", F32, "mm_out")
        x1, h2 = _rows(lambda i, n, *v: _f_post_mix(*v), T, tm, [_cur(xl), _cur(mixed)], [_cst(g_post), _cst(g_fpre)],
                       [_out(D_MODEL, F32), _out(D_MODEL, BF16)], [], "post_mix_fwd")
        upre = _mm(h2, lw["w_up"], "nn", F32, "mm_up")
        nt = FFN_DIM // FFN_TILE
        gcol, ucol = (lambda j: j), (lambda j: j + nt)
        (act,) = _rows(_k_ffnact_fwd, T, tm_ffn,
                       [(upre, FFN_TILE, gcol, "cur"), (upre, FFN_TILE, ucol, "cur"), (upre, FFN_TILE, gcol, "prev"),
                        (upre, FFN_TILE, ucol, "prev")],
                       [(lw["ffn_w"], FFN_TILE, gcol), (lw["ffn_w"], FFN_TILE, ucol), (ffn_b, FFN_TILE, gcol), (ffn_b, FFN_TILE, ucol)],
                       [(FFN_DIM, BF16, FFN_TILE, gcol)], [], "ffn_act_fwd", ncol=nt)
        dn = _mm(act, lw["w_down"], "nn", F32, "mm_down")
        (x2,) = _rows(lambda i, n, *v: _f_post_ffn(*v), T, tm, [_cur(x1), _cur(dn)], [_cst(g_fpost)], [_out(D_MODEL, F32)], [], "post_ffn_fwd")
        saved.append(dict(lw=lw, x=xl, h1=h1, zin=zin, qlat=qlat, kvlat=kvlat, qr=qr, kr=kr, kvpad=kvpad, o=o, xbc=xbc,
                          yscan=yscan, states=states, ycat=ycat, mixed=mixed, x1=x1, h2=h2, upre=upre, act=act, dn=dn,
                          g_pre=g_pre, g_post=g_post, g_fpre=g_fpre, g_fpost=g_fpost, qn=qn, kvn=kvn, ssd_b=ssd_b,
                          ssd_par=ssd_par, ssd_nw=ssd_nw, ffn_b=ffn_b))
        xl = x2

    gx, loss_part = _rows(_k_loss, T, tm, [_cur(xl), _cur(target)], [], [_out(D_MODEL, F32)], [_acc(1, LANE)], "loss_head")

    GW = {k: [None] * DEPTH for k in ("w_in", "mla_w_q_up", "mla_w_kv_up", "sc_conv_w", "ssd_conv_w", "w_out", "ffn_w_up",
                                      "ffn_conv_w", "ffn_w_down")}
    GS = {k: [None] * DEPTH for k in ("norm_mix_pre", "norm_mix_post", "norm_ffn_pre", "norm_ffn_post", "mla_q_norm", "mla_kv_norm",
                                      "ssd_conv_b", "ssd_dt_bias", "ssd_a_log", "ssd_d", "ssd_norm", "ffn_conv_b")}
    nt = FFN_DIM // FFN_TILE
    gcol, ucol = (lambda j: j), (lambda j: j + nt)
    pending = None
    for l in reversed(range(DEPTH)):
        s = saved[l]
        lw = s["lw"]
        gx1, ddn, dgf = _rows_vjp(_f_post_ffn, T, tm, [s["x1"], s["dn"]], [s["g_fpost"]], [gx], [F32, BF16], "post_ffn_bwd")
        GS["norm_ffn_post"][l] = dgf[0]
        dact = _mm(ddn, lw["w_down"], "nt", F32, "mm_down_dx")
        GW["ffn_w_down"][l] = _mm(s["act"], ddn, "tn", BF16, "mm_down_dw")
        up = s["upre"]
        dug, duu, dwg, dwu, dbg, dbu = _rows(
            _k_ffnact_bwd, T, tm_ffn,
            [(up, FFN_TILE, gcol, "cur"), (up, FFN_TILE, ucol, "cur"), (dact, FFN_TILE, gcol, "cur"), (up, FFN_TILE, gcol, "prev"),
             (up, FFN_TILE, ucol, "prev"), (up, FFN_TILE, gcol, "next"), (up, FFN_TILE, ucol, "next"), (dact, FFN_TILE, gcol, "next")],
            [(lw["ffn_w"], FFN_TILE, gcol), (lw["ffn_w"], FFN_TILE, ucol), (s["ffn_b"], FFN_TILE, gcol), (s["ffn_b"], FFN_TILE, ucol)],
            [(FFN_DIM, BF16, FFN_TILE, gcol)] * 2,
            [(HALO, FFN_DIM, FFN_TILE, gcol)] * 2 + [(1, FFN_DIM, FFN_TILE, gcol)] * 2, "ffn_act_bwd", ncol=nt)
        GW["ffn_conv_w"][l] = jnp.concatenate([dwg[:3], dwu[:3]], axis=1)
        GS["ffn_conv_b"][l] = jnp.concatenate([dbg[0], dbu[0]])
        dh2 = _mm((dug, duu), lw["w_up"], "nt", F32, "mm_up_dx")
        GW["ffn_w_up"][l] = (_mm(s["h2"], dug, "tn", BF16, "mm_up_dw_gate"), _mm(s["h2"], duu, "tn", BF16, "mm_up_dw_up"))
        gx0, dmixed, dgp, dgf = _rows_vjp(_f_post_mix, T, tm, [s["x"], s["mixed"]], [s["g_post"], s["g_fpre"]], [gx1, dh2],
                                          [F32, BF16], "post_mix_bwd")
        GS["norm_mix_post"][l], GS["norm_ffn_pre"][l] = dgp[0], dgf[0]
        dycat = _mm(dmixed, lw["w_out"], "nt", F32, "mm_out_dx")
        GW["w_out"][l] = _unpad_out_rows(_mm(s["ycat"], dmixed, "tn", BF16, "mm_out_dw"))
        zin = s["zin"]
        dyscan, dz, dnw = _rows(_vjp_wrap(_f_ssd_gate, 2, 1), T, tm,
                                [_cur(s["yscan"]), _cur(zin, SSD_DIM, Z_SSZ // SSD_DIM), _cur(dycat, SSD_DIM, (HEADS * HP + SC_DIM) // SSD_DIM)],
                                [_cst(s["ssd_nw"])], [_out(SSD_DIM, F32), _out(SSD_DIM, BF16)], [_acc(1, SSD_DIM)], "ssd_gate_bwd")
        GS["ssd_norm"][l] = dnw[0]
        dxbc, ddtraw, dpar = _ssd_bwd(s["xbc"], zin, s["ssd_par"], s["states"], dyscan, T, Z_DT // LANE)
        GS["ssd_dt_bias"][l], GS["ssd_a_log"][l], GS["ssd_d"][l] = dpar[0, :SSD_HEADS], dpar[1, :SSD_HEADS], dpar[2, :SSD_HEADS]
        xb = Z_XBC // SSD_CONV_DIM
        dxraw, dsw, dsb = _rows(_k_ssdconv_bwd, T, tm,
                                [_cur(zin, SSD_CONV_DIM, xb), _cur(dxbc), _halo(zin, "prev", SSD_CONV_DIM, xb),
                                 _halo(zin, "next", SSD_CONV_DIM, xb), _halo(dxbc, "next")],
                                [_cst(lw["ssd_w"]), _cst(s["ssd_b"])], [_out(SSD_CONV_DIM, BF16)],
                                [_acc(HALO, SSD_CONV_DIM), _acc(1, SSD_CONV_DIM)], "ssd_conv_bwd")
        GW["ssd_conv_w"][l] = dsw[:4]
        GS["ssd_conv_b"][l] = dsb[0]
        cb = (HEADS * HP) // SC_DIM
        dscb, dscc, dsch, dscw = _rows(_k_sconv_bwd, T, tm,
                                       [_cur(zin, SC_DIM, Z_SCB // SC_DIM), _cur(zin, SC_DIM, Z_SCC // SC_DIM),
                                        _cur(zin, SC_DIM, Z_SCH // SC_DIM), _cur(dycat, SC_DIM, cb),
                                        _halo(zin, "prev", SC_DIM, Z_SCC // SC_DIM), _halo(zin, "prev", SC_DIM, Z_SCH // SC_DIM),
                                        _halo(zin, "next", SC_DIM, Z_SCB // SC_DIM), _halo(dycat, "next", SC_DIM, cb)],
                                       [_cst(lw["sc_w"])], [_out(SC_DIM, BF16)] * 3, [_acc(HALO, SC_DIM)], "short_conv_bwd")
        GW["sc_conv_w"][l] = dscw[:3]
        if ex is None:
            dq, dk, dv = _flash_bwd(s["qr"], s["kr"], s["kvpad"], s["o"], dycat, T)
        else:
            sums = ex.submit([({n: GW[n][l] for ns in LATE for n in ns}, "late")] + ([(pending, "early")] if pending else []))
            late = sums[0]
            dq, dk, dv, parts = _flash_bwd(s["qr"], s["kr"], s["kvpad"], s["o"], dycat, T, carry=[p for ps in sums for p in ps])
            ex.collect(l, "late", parts[:len(late)])
            if pending:
                ex.collect(l + 1, "early", parts[len(late):])
        dqpad, dkvpad, dkr = _rows(_k_rope_bwd, T, tm, [_cur(dq), _cur(dk), _cur(dv), _cur(cosf), _cur(sina), _cur(sinb)], [],
                                   [_out(HEADS * HP, BF16), _out(2 * HEADS * HP, BF16), _out(LANE, BF16)], [], "mla_rope_bwd")
        dqlat = _mm(dqpad, lw["w_q"], "nt", F32, "mm_q_dx")
        GW["mla_w_q_up"][l] = _unpad_heads(_mm(s["qlat"], dqpad, "tn", BF16, "mm_q_dw"), NOPE + ROPE)
        dkvlat = _mm(dkvpad, lw["w_kv"], "nt", F32, "mm_kv_dx")
        GW["mla_w_kv_up"][l] = _unpad_kv(_mm(s["kvlat"], dkvpad, "tn", BF16, "mm_kv_dw"))
        dcq, dckv, dqn, dkvn = _rows(_vjp_wrap(_f_mla_pre, 2, 2), T, tm,
                                     [_cur(zin, Q_LORA, 0), _cur(zin, KV_LORA, Z_CKV // KV_LORA), _cur(dqlat), _cur(dkvlat)],
                                     [_cst(s["qn"]), _cst(s["kvn"])], [_out(Q_LORA, BF16), _out(KV_LORA, BF16)],
                                     [_acc(1, Q_LORA), _acc(1, KV_LORA)], "mla_pre_bwd")
        GS["mla_q_norm"][l], GS["mla_kv_norm"][l] = dqn[0], dkvn[0]
        dzin = jnp.concatenate([dcq, dckv, dkr, dscb, dscc, dsch, dz, dxraw, ddtraw.astype(BF16), jnp.zeros((T, ZIN - Z_DT - LANE), BF16)], axis=1)
        dh1 = _mm(dzin, lw["w_in"], "nn", F32, "mm_in_dx")
        GW["w_in"][l] = _unpad_rows_in(_mm(dzin, s["h1"], "tn", BF16, "mm_in_dw"))
        gx, dgp = _rows(_vjp_wrap(_f_premix, 1, 1, add_first=True), T, tm, [_cur(s["x"]), _cur(dh1), _cur(gx0)], [_cst(s["g_pre"])],
                        [_out(D_MODEL, F32)], [_acc(1, D_MODEL)], "pre_mix_bwd")
        GS["norm_mix_pre"][l] = dgp[0]
        if ex is not None:
            pending = {n: GW[n][l] for ns in EARLY for n in ns}
    if ex is not None:
        ex.collect(0, "early", _rs_chip_exchange(ex.submit([(pending, "early")])[0]))
    GS = {k: jnp.stack(v) for k, v in GS.items()}
    return loss_part[0, 0], gx, GW, GS


WEIGHTS = ("norm_mix_pre", "norm_mix_post", "norm_ffn_pre", "norm_ffn_post", "w_in", "mla_q_norm", "mla_w_q_up", "mla_kv_norm",
           "mla_w_kv_up", "sc_conv_w", "ssd_conv_w", "ssd_conv_b", "ssd_dt_bias", "ssd_a_log", "ssd_d", "ssd_norm", "w_out",
           "ffn_w_up", "ffn_conv_w", "ffn_conv_b", "ffn_w_down")
SHARDED = (("w_in", 2), ("mla_w_q_up", 2), ("mla_w_kv_up", 2), ("sc_conv_w", 2), ("ssd_conv_w", 2), ("w_out", 1),
           ("ffn_w_up", 2), ("ffn_conv_w", 2), ("ffn_w_down", 1))
SMALL = tuple(n for n in WEIGHTS if n not in dict(SHARDED))
N_CHIPS = 4
N_DEV = 8
ROW_ALIGN = 64
SLAB_ALIGN = 16
EARLY = (("w_in", "mla_w_q_up", "mla_w_kv_up", "sc_conv_w", "ssd_conv_w"),)
LATE = (("ffn_w_down", "w_out"), ("ffn_w_up", "ffn_conv_w"))
TRANSPOSED = ("w_in",)


def _is_rows(shape, width):
    return shape[-1] == width and math.prod(shape[:-1]) % SLAB_ALIGN == 0


def _is_short(shape, width):
    return len(shape) == 2 and shape[1] == width and not _is_rows(shape, width)


def _slab_rows(shape, width):
    if _is_rows(shape, width):
        return math.prod(shape[:-1])
    if _is_short(shape, width):
        return -(-shape[0] // SLAB_ALIGN) * SLAB_ALIGN
    return -(-math.prod(shape) // (width * SLAB_ALIGN)) * SLAB_ALIGN


def _slab(piece, width, dtype, lead=0):
    ld, shape = piece.shape[:lead], piece.shape[lead:]
    rows = _slab_rows(shape, width)
    if _is_rows(shape, width):
        return piece.astype(dtype).reshape(ld + (rows, width))
    if _is_short(shape, width):
        return jnp.pad(piece.astype(dtype), [(0, 0)] * lead + [(0, rows - shape[0]), (0, 0)])
    flat = piece.astype(dtype).reshape(ld + (-1,))
    return jnp.pad(flat, [(0, 0)] * lead + [(0, rows * width - flat.shape[-1])]).reshape(ld + (rows, width))


def _unslab(slab, shape, lead=0):
    ld = slab.shape[:lead]
    if _is_rows(shape, slab.shape[-1]):
        return slab.reshape(ld + tuple(shape))
    if _is_short(shape, slab.shape[-1]):
        return slab[..., :shape[0], :]
    return slab.reshape(ld + (-1,))[..., :math.prod(shape)].reshape(ld + tuple(shape))


def _layout(shapes, names, width):
    ents, off = [], 0
    for n in names:
        shp = tuple(shapes[n])
        todo = [(None, False, shp), (None, True, shp)] if n.endswith("conv_w") else [(l, False, shp[1:]) for l in range(shp[0])]
        for l, lo, ps in todo:
            r = _slab_rows(ps, width)
            ents.append((n, l, lo, ps, off, r))
            off += r
    return width, -(-off // ROW_ALIGN) * ROW_ALIGN, ents


def _pack(layout, piece, dtype, lead=0):
    width, rows, ents = layout
    slabs, ld = [], None
    for n, l, lo, ps, off, r in ents:
        p = piece(n, l, lo)
        slabs.append(None if p is None else _slab(p, width, dtype, lead))
        ld = ld if p is None else p.shape[:lead]
    used = ents[-1][4] + ents[-1][5]
    slabs = [jnp.zeros(ld + (e[5], width), dtype) if s is None else s for s, e in zip(slabs, ents)]
    if rows > used:
        slabs.append(jnp.zeros(ld + (rows - used, width), dtype))
    return jnp.concatenate(slabs, axis=lead)


ANY = pl.BlockSpec(memory_space=pl.ANY)


def _pos():
    return lax.axis_index("x"), lax.axis_index("y"), lax.axis_index("c")


def _other_chips(x, y):
    return ((1 - x, y), (x, 1 - y), (1 - x, 1 - y))


def _remote(src, dst, ssem, rsem, dev):
    return pltpu.make_async_remote_copy(src_ref=src, dst_ref=dst, send_sem=ssem, recv_sem=rsem, device_id=dev, device_id_type=MESH)


AG_CHUNKS = 2


def _chip_index():
    return 2 * lax.axis_index("x") + lax.axis_index("y")


def _ag_sems(nbuf):
    return [pltpu.SemaphoreType.DMA((nbuf * 3 * AG_CHUNKS,))] * 4


def _ag_plan(w_refs, out_refs, sems):
    isend, irecv, dsend, drecv = sems
    x, y, c = _pos()
    k = 2 * x + y
    sib = (x, y, 1 - c)
    sends, lands, forwards, finals = [], [], [], []
    s = 0
    for w_ref, out_ref in zip(w_refs, out_refs):
        H = w_ref.shape[0] // 2
        CH = H // AG_CHUNKS
        for cx, cy in _other_chips(x, y):
            for ch in range(AG_CHUNKS):
                mine = out_ref.at[k, pl.ds(c * H + ch * CH, CH), :]
                near = out_ref.at[2 * cx + cy, pl.ds(c * H + ch * CH, CH), :]
                far = out_ref.at[2 * cx + cy, pl.ds((1 - c) * H + ch * CH, CH), :]
                sends.append(_remote(w_ref.at[pl.ds(c * H + ch * CH, CH), :], mine, isend.at[s], irecv.at[s], (cx, cy, c)))
                lands.append(_remote(near, near, isend.at[s], irecv.at[s], (cx, cy, c)))
                forwards.append(_remote(near, near, dsend.at[s], drecv.at[s], sib))
                finals.append(_remote(far, far, dsend.at[s], drecv.at[s], sib))
                s += 1
    return sends, lands, forwards, finals


def _own_slot(got, own):
    return lax.dynamic_update_slice(got, own[None], (_chip_index(), 0, 0))


def _all_gather_weights(ws):
    nb = len(ws)

    def body(*refs):
        sends, lands, forwards, finals = _ag_plan(refs[:nb], refs[nb:2 * nb], refs[2 * nb:])
        for cp in sends:
            cp.start()
        for land, fw in zip(lands, forwards):
            land.wait_recv()
            fw.start()
        for cp in finals:
            cp.wait_recv()
        for cp in sends + forwards:
            cp.wait_send()

    got = pl.pallas_call(
        body, name="all_gather_weights", in_specs=[ANY] * nb, out_specs=[ANY] * nb,
        out_shape=[jax.ShapeDtypeStruct((N_CHIPS,) + w.shape, w.dtype) for w in ws], scratch_shapes=_ag_sems(nb),
    )(*ws)
    return [_own_slot(g, w) for g, w in zip(got, ws)]


def _rs_pair_exchange(gs):
    nb = len(gs)

    def body(*refs):
        g_refs, got_refs, (ssem, rsem) = refs[:nb], refs[nb:2 * nb], refs[2 * nb:]
        x, y, c = _pos()
        cps = []
        for b, (g_ref, got_ref) in enumerate(zip(g_refs, got_refs)):
            H = g_ref.shape[1] // 2
            for kk in range(N_CHIPS):
                s = b * N_CHIPS + kk
                cps.append(_remote(g_ref.at[kk, pl.ds((1 - c) * H, H), :], got_ref.at[kk], ssem.at[s], rsem.at[s], (x, y, 1 - c)))
        for cp in cps:
            cp.start()
        for cp in cps:
            cp.wait()

    return pl.pallas_call(
        body, name="rs_pair_exchange", in_specs=[ANY] * nb, out_specs=[ANY] * nb,
        out_shape=[jax.ShapeDtypeStruct((N_CHIPS, g.shape[1] // 2, g.shape[2]), g.dtype) for g in gs],
        scratch_shapes=[pltpu.SemaphoreType.DMA((nb * N_CHIPS,))] * 2,
    )(*gs)


def _chip_sems(nbuf):
    return [pltpu.SemaphoreType.DMA((nbuf * 3,))] * 2


def _chip_plan(p_refs, out_refs, sems):
    ssem, rsem = sems
    x, y, c = _pos()
    sends, lands = [], []
    s = 0
    for p_ref, out_ref in zip(p_refs, out_refs):
        for cx, cy in _other_chips(x, y):
            sends.append(_remote(p_ref.at[2 * cx + cy], out_ref.at[2 * x + y], ssem.at[s], rsem.at[s], (cx, cy, c)))
            land = out_ref.at[2 * cx + cy]
            lands.append(_remote(land, land, ssem.at[s], rsem.at[s], (cx, cy, c)))
            s += 1
    return sends, lands


def _chip_parts(got, ps):
    k = _chip_index()
    return [lax.dynamic_update_slice(g, lax.dynamic_slice_in_dim(p, k, 1, axis=0), (k, 0, 0)) for g, p in zip(got, ps)]


def _rs_chip_exchange(ps):
    nb = len(ps)

    def body(*refs):
        sends, lands = _chip_plan(refs[:nb], refs[nb:2 * nb], refs[2 * nb:])
        for cp in sends:
            cp.start()
        for cp in lands:
            cp.wait_recv()
        for cp in sends:
            cp.wait_send()

    got = pl.pallas_call(
        body, name="rs_chip_exchange", in_specs=[ANY] * nb, out_specs=[ANY] * nb,
        out_shape=[jax.ShapeDtypeStruct(p.shape, p.dtype) for p in ps], scratch_shapes=_chip_sems(nb),
    )(*ps)
    return _chip_parts(got, ps)


def _rs_pair_share(fs):
    nb = len(fs)

    def body(*refs):
        f_refs, out_refs, (ssem, rsem) = refs[:nb], refs[nb:2 * nb], refs[2 * nb:]
        x, y, c = _pos()
        sends, lands = [], []
        for b, (f_ref, out_ref) in enumerate(zip(f_refs, out_refs)):
            sends.append(_remote(f_ref, out_ref.at[c], ssem.at[b], rsem.at[b], (x, y, 1 - c)))
            land = out_ref.at[1 - c]
            lands.append(_remote(land, land, ssem.at[b], rsem.at[b], (x, y, 1 - c)))
        for cp in sends:
            cp.start()
        for cp in lands:
            cp.wait_recv()
        for cp in sends:
            cp.wait_send()

    got = pl.pallas_call(
        body, name="rs_pair_share", in_specs=[ANY] * nb, out_specs=[ANY] * nb,
        out_shape=[jax.ShapeDtypeStruct((2,) + f.shape, f.dtype) for f in fs],
        scratch_shapes=[pltpu.SemaphoreType.DMA((nb,))] * 2,
    )(*fs)
    return [lax.dynamic_update_slice(g, f[None], (lax.axis_index("c"), 0, 0)) for g, f in zip(got, fs)]


def _all_reduce_small(s):
    r, C = s.shape

    def body(s_ref, o_ref, buf, ssem, rsem):
        x, y, c = _pos()
        me = 4 * x + 2 * y + c
        buf[me] = s_ref[...]
        cps = []
        for m in range(1, N_DEV):
            mx, my, mc = (m >> 2) & 1, (m >> 1) & 1, m & 1
            peer = (x ^ mx, y ^ my, c ^ mc)
            cp = _remote(s_ref, buf.at[me], ssem.at[m - 1], rsem.at[m - 1], peer)
            cp.start()
            cps.append(cp)
        for m in range(1, N_DEV):
            mx, my, mc = (m >> 2) & 1, (m >> 1) & 1, m & 1
            src = 4 * (x ^ mx) + 2 * (y ^ my) + (c ^ mc)
            _remote(s_ref, buf.at[src], ssem.at[m - 1], rsem.at[m - 1], (x ^ mx, y ^ my, c ^ mc)).wait_recv()
        for cp in cps:
            cp.wait_send()
        acc = buf[0]
        for j in range(1, N_DEV):
            acc = acc + buf[j]
        o_ref[...] = acc

    return pl.pallas_call(
        body, name="all_reduce_small", in_specs=[pl.BlockSpec(memory_space=pltpu.VMEM)],
        out_specs=pl.BlockSpec(memory_space=pltpu.VMEM), out_shape=jax.ShapeDtypeStruct((r, C), F32),
        scratch_shapes=[pltpu.VMEM((N_DEV, r, C), F32), pltpu.SemaphoreType.DMA((N_DEV - 1,)), pltpu.SemaphoreType.DMA((N_DEV - 1,))],
    )(s)


def _rtile(n, pref):
    if n <= pref:
        return n
    t = (pref // 16) * 16
    while t >= 16:
        if n % t == 0:
            return t
        t -= 16
    raise ValueError(f"no row tile for {n}")


def _rs_pair_sums(gpks):
    gots = _rs_pair_exchange(gpks)
    out = []
    for gpk, got in zip(gpks, gots):
        _, R, C = gpk.shape
        H = R // 2
        own = lax.dynamic_index_in_dim(gpk.reshape(N_CHIPS, 2, H, C), lax.axis_index("c"), axis=1, keepdims=False)
        (part,) = _rows(lambda i, n, a, b: (a.astype(F32) + b.astype(F32),), N_CHIPS * H, _rtile(N_CHIPS * H, 512),
                        [_cur(own.reshape(N_CHIPS * H, C)), _cur(got.reshape(N_CHIPS * H, C))], [], [_out(C, BF16)], [], "rs_pair_add")
        out.append(part.reshape(N_CHIPS, H, C))
    return out


def _rs_chip_sums(parts):
    def add4(i, n, a, b, c, d):
        return (((a.astype(F32) + b.astype(F32)) + c.astype(F32)) + d.astype(F32),)

    out = []
    for p in parts:
        _, H, C = p.shape
        tm = _rtile(H, 1024)
        (red,) = _rows(add4, H, tm, [(p.reshape(N_CHIPS * H, C), C, functools.partial(_const, v=0), j * (H // tm)) for j in range(N_CHIPS)],
                       [], [_out(C, F32)], [], "rs_chip_add")
        out.append(red)
    return out


class _Exchange:
    def __init__(self, a):
        self.a = a
        self.axis = {n: (1 if n in TRANSPOSED else ax) for n, ax in SHARDED}
        shapes = {n: (1,) + tuple(self.packed(n, a[n]).shape[1:]) for n in self.axis}
        widths = lambda names: shapes[names[0]][-1] if names[0] == "ffn_w_up" else PACK_COLS
        self.layouts = {"early": [_layout(shapes, ns, widths(ns)) for ns in EARLY], "late": [_layout(shapes, ns, widths(ns)) for ns in LATE]}
        self.reduced = {}

    @staticmethod
    def packed(n, w):
        return jnp.swapaxes(w, -1, -2) if n in TRANSPOSED else w

    def shard(self, l, group):
        def piece(n, li, lo):
            w = self.packed(n, self.a[n][l:l + 1] if li is None else self.a[n][l])
            return w - w.astype(BF16).astype(F32) if lo else w
        return [_pack(lay, piece, BF16) for lay in self.layouts[group]]

    def weights(self, gathered, group):
        W, resid = {}, {}
        for (width, rows, ents), g in zip(self.layouts[group], gathered):
            for n, li, lo, ps, off, r in ents:
                parts = _unslab(g[:, off:off + r], ps, lead=1)
                ax = self.axis[n] + (1 if li is None else 0)
                full = jnp.moveaxis(parts, 0, ax - 1)
                full = full.reshape(full.shape[:ax - 1] + (-1,) + full.shape[ax + 1:])
                (resid if lo else W)[n] = full[0] if li is None else full
        for n in resid:
            W[n] = W[n].astype(F32) + resid[n].astype(F32)
        return W

    def submit(self, jobs):
        def by_chip(g, ax, parts=N_CHIPS):
            g = g.reshape(g.shape[:ax] + (parts, g.shape[ax] // parts) + g.shape[ax + 1:])
            return jnp.moveaxis(g, ax, 0)

        def pieces_of(GW):
            def piece(n, li, lo):
                if lo:
                    return None
                g = GW[n]
                if isinstance(g, tuple):
                    return jnp.concatenate([by_chip(h, self.axis[n] - 1, N_CHIPS // 2) for h in g])
                return by_chip(g[None], self.axis[n]) if li is None else by_chip(g, self.axis[n] - 1)
            return piece

        sums = _rs_pair_sums([_pack(lay, pieces_of(GW), BF16, lead=1) for GW, group in jobs for lay in self.layouts[group]])
        out, at = [], 0
        for _, group in jobs:
            out.append(sums[at:at + len(self.layouts[group])])
            at += len(self.layouts[group])
        return out

    def collect(self, l, group, parts):
        self.reduced[l, group] = _rs_chip_sums(parts)

    def finish(self):
        keys = [(l, g) for l in range(DEPTH) for g in self.layouts]
        flat = _rs_pair_share([f for key in keys for f in self.reduced[key]])
        both, at = {}, 0
        for key in keys:
            both[key] = flat[at:at + len(self.layouts[key[1]])]
            at += len(self.layouts[key[1]])
        grads = {}
        for group, lays in self.layouts.items():
            for b, (width, rows, ents) in enumerate(lays):
                for n, li, lo, ps, off, r in ents:
                    if not lo:
                        per_layer = [self.packed(n, _unslab(both[l, group][b].reshape(rows, width)[off:off + r], ps)) for l in range(DEPTH)]
                        grads[n] = jnp.concatenate(per_layer) if li is None else jnp.stack(per_layer)
        return grads


def _adam(w, g, m, v, name, g_row=0):
    shp = w.shape
    two = lambda a: a.reshape(-1, shp[-1])
    rows = math.prod(shp[:-1])
    tm = _rtile(rows, 256)
    assert g_row % tm == 0
    g_in = (two(g), shp[-1], functools.partial(_const, v=0), g_row // tm)
    res = _rows(_k_adam, rows, tm, [_cur(two(w)), g_in, _cur(two(m)), _cur(two(v))], [], [_out(shp[-1], F32)] * 4, [], name)
    return tuple(r.reshape(shp) for r in res)


def _pack_flat(parts, rows):
    flat = jnp.concatenate([p.astype(F32).reshape(-1) for p in parts])
    return jnp.pad(flat, (0, rows * PACK_COLS - flat.shape[0])).reshape(rows, PACK_COLS)


def _unpack_flat(buf, shapes):
    flat, out, off = buf.reshape(-1), [], 0
    for shp in shapes:
        n = math.prod(shp)
        out.append(flat[off:off + n].reshape(shp))
        off += n
    return out


def kernel(x, positions, norm_mix_pre, norm_mix_post, norm_ffn_pre, norm_ffn_post, w_in, mla_q_norm, mla_w_q_up, mla_kv_norm, mla_w_kv_up, sc_conv_w, ssd_conv_w, ssd_conv_b, ssd_dt_bias, ssd_a_log, ssd_d, ssd_norm, w_out, ffn_w_up, ffn_conv_w, ffn_conv_b, ffn_w_down, loss_target, m_norm_mix_pre, m_norm_mix_post, m_norm_ffn_pre, m_norm_ffn_post, m_w_in, m_mla_q_norm, m_mla_w_q_up, m_mla_kv_norm, m_mla_w_kv_up, m_sc_conv_w, m_ssd_conv_w, m_ssd_conv_b, m_ssd_dt_bias, m_ssd_a_log, m_ssd_d, m_ssd_norm, m_w_out, m_ffn_w_up, m_ffn_conv_w, m_ffn_conv_b, m_ffn_w_down, v_norm_mix_pre, v_norm_mix_post, v_norm_ffn_pre, v_norm_ffn_post, v_w_in, v_mla_q_norm, v_mla_w_q_up, v_mla_kv_norm, v_mla_w_kv_up, v_sc_conv_w, v_ssd_conv_w, v_ssd_conv_b, v_ssd_dt_bias, v_ssd_a_log, v_ssd_d, v_ssd_norm, v_w_out, v_ffn_w_up, v_ffn_conv_w, v_ffn_conv_b, v_ffn_w_down):
    a = dict(locals())
    ex = _Exchange(a)
    S = {n: a[n] for n in SMALL}
    loss_part, gx, _, GS = _local_step(a["x"][0], a["positions"][0], a["loss_target"][0], None, S, ex)

    grads, delta, new_m, new_v = {}, {}, {}, {}
    for n, g in ex.finish().items():
        grads[n], delta[n], new_m[n], new_v[n] = _adam(a[n], g, a["m_" + n], a["v_" + n], "adamw_" + n)

    small_shapes = [a[n].shape for n in SMALL]
    rs = -(-(sum(math.prod(s) for s in small_shapes) + 1) // (PACK_COLS * SLAB_ALIGN)) * SLAB_ALIGN
    red = _all_reduce_small(_pack_flat([GS[n] for n in SMALL] + [loss_part.reshape(1)], rs))
    loss = _unpack_flat(red, small_shapes + [(1,)])[-1][0]
    pk = lambda pre: _pack_flat([a[pre + n] for n in SMALL], rs)
    for dst, buf in zip((grads, delta, new_m, new_v), _adam(pk(""), red, pk("m_"), pk("v_"), "adamw_small")):
        dst.update(zip(SMALL, _unpack_flat(buf, small_shapes)))

    return (loss, gx[None], *[grads[n] for n in WEIGHTS], *[delta[n] for n in WEIGHTS], *[new_m[n] for n in WEIGHTS],
            *[new_v[n] for n in WEIGHTS])
```

```python
import functools
import math

import jax
import jax.numpy as jnp
from jax import lax
from jax.experimental import pallas as pl
from jax.experimental.pallas import tpu as pltpu

F32 = jnp.float32
BF16 = jnp.bfloat16
MXU_DTYPE = jnp.bfloat16
HIGHEST = lax.Precision.HIGHEST
MESH = pl.DeviceIdType.MESH

D_MODEL = 1024
DEPTH = 4
HEADS = 8
Q_LORA = 256
KV_LORA = 128
NOPE = 64
ROPE = 32
VDIM = 64
ROPE_THETA = 10000.0
SC_DIM = 256
SSD_HEADS = 4
SSD_HEAD_DIM = 64
SSD_STATE = 128
SSD_DIM = 256
SSD_CONV_DIM = 768
SSD_CHUNK = 128
FFN_DIM = 2816
NORM_EPS = 1e-6
QK_SCALE = (NOPE + ROPE) ** -0.5
LANE = 128
HP = 128
FLASH_HEADS = 8
FLASH_HEADS_FWD = 8
FLASH_BLOCK = 512

ZIN = 2560
Z_CQ, Z_CKV, Z_KR, Z_SCB, Z_SCC, Z_SCH, Z_SSZ, Z_XBC, Z_DT = 0, 256, 384, 512, 768, 1024, 1280, 1536, 2304
KR_LANE = 64
FFN_TILE = 256
FFN_ROWS = 2048
ROW_BLOCK = 512

ADAM_LR, ADAM_B1, ADAM_B2, ADAM_EPS, ADAM_WD, ADAM_STEP = 0.001, 0.9, 0.999, 1e-08, 0.01, 10

PACK_COLS = 1024


def _tile(n, pref):
    if n <= pref:
        return n
    t = (pref // LANE) * LANE
    while t >= LANE:
        if n % t == 0:
            return t
        t -= LANE
    raise ValueError(f"no tile for {n}")


MM_TM, MM_TN, MM_TK = 1024, 1408, 1536


def _mm(a, b, mode, out_dtype, name, tm=None, tn=MM_TN, tkmax=MM_TK):
    pair = isinstance(a, tuple)
    a_list = list(a) if pair else [a]
    layer = None
    if isinstance(b, tuple):
        b, layer = b
    bshape = b.shape[-2:]
    if mode == "nn":
        (M, Ka), (_, N) = a_list[0].shape, bshape
    elif mode == "nt":
        (M, Ka), (N, _) = a_list[0].shape, bshape
    else:
        (Ka, M), (_, N) = a_list[0].shape, bshape
    tk = _tile(Ka, tkmax)
    nka = Ka // tk
    nk = nka * len(a_list)
    if tm is None:
        tm = MM_TN if mode == "tn" else (2 * MM_TM if nk == 1 else MM_TM)
    tm, tn = _tile(M, tm), _tile(N, tn)

    def bspec(shape, index):
        if layer is None:
            return pl.BlockSpec(shape, index)
        return pl.BlockSpec((None,) + shape, lambda i, j, k: (layer,) + index(i, j, k))

    if mode == "nn":
        a_specs = [pl.BlockSpec((tm, tk), lambda i, j, k: (i, jnp.minimum(k, nka - 1))),
                   pl.BlockSpec((tm, tk), lambda i, j, k: (i, jnp.maximum(k - nka, 0)))][:len(a_list)]
        b_spec = bspec((tk, tn), lambda i, j, k: (k, j))
        dims = NN
    elif mode == "nt":
        a_specs = [pl.BlockSpec((tm, tk), lambda i, j, k: (i, jnp.minimum(k, nka - 1))),
                   pl.BlockSpec((tm, tk), lambda i, j, k: (i, jnp.maximum(k - nka, 0)))][:len(a_list)]
        b_spec = bspec((tn, tk), lambda i, j, k: (j, k))
        dims = NT
    else:
        a_specs = [pl.BlockSpec((tk, tm), lambda i, j, k: (k, i))]
        b_spec = pl.BlockSpec((tk, tn), lambda i, j, k: (k, j))
        dims = TN
    na = len(a_list)

    def body(*refs):
        a_refs, b_ref, o_ref = refs[:na], refs[na], refs[na + 1]
        k = pl.program_id(2)

        def prod(a_ref):
            return lax.dot_general(a_ref[...].astype(MXU_DTYPE), b_ref[...].astype(MXU_DTYPE), dims, preferred_element_type=F32)

        if nk == 1:
            o_ref[...] = prod(a_refs[0]).astype(o_ref.dtype)
            return
        acc_ref = refs[na + 2]

        @pl.when(k == 0)
        def _():
            acc_ref[...] = prod(a_refs[0])

        @pl.when((k > 0) & (k < nka))
        def _():
            acc_ref[...] += prod(a_refs[0])

        if pair:
            @pl.when(k >= nka)
            def _():
                acc_ref[...] += prod(a_refs[1])

        @pl.when(k == nk - 1)
        def _():
            o_ref[...] = acc_ref[...].astype(o_ref.dtype)

    return pl.pallas_call(
        body, name=name, grid=(M // tm, N // tn, nk),
        in_specs=a_specs + [b_spec], out_specs=pl.BlockSpec((tm, tn), lambda i, j, k: (i, j)),
        out_shape=jax.ShapeDtypeStruct((M, N), out_dtype),
        scratch_shapes=[pltpu.VMEM((tm, tn), F32)] if nk > 1 else [],
        compiler_params=pltpu.CompilerParams(dimension_semantics=("parallel", "parallel", "arbitrary")),
    )(*a_list, b)


HALO = 8


def _const(j, v):
    return v


def _rows(fn, T, tm, ins, consts, outs, accs, name, ncol=1):
    n = T // tm
    hb = tm // HALO
    last = T // HALO - 1
    in_specs, args = [], []
    for arr, bc, cb, kind in ins:
        if isinstance(kind, int):
            in_specs.append(pl.BlockSpec((tm, bc), lambda j, i, cb=cb, off=kind: (i + off, cb(j))))
        elif kind == "cur":
            in_specs.append(pl.BlockSpec((tm, bc), lambda j, i, cb=cb: (i, cb(j))))
        elif kind == "prev":
            in_specs.append(pl.BlockSpec((HALO, bc), lambda j, i, cb=cb: (jnp.maximum(i * hb - 1, 0), cb(j))))
        else:
            in_specs.append(pl.BlockSpec((HALO, bc), lambda j, i, cb=cb: (jnp.minimum((i + 1) * hb, last), cb(j))))
        args.append(arr)
    for arr, bc, cb in consts:
        in_specs.append(pl.BlockSpec((arr.shape[0], bc), lambda j, i, cb=cb: (0, cb(j))))
        args.append(arr)
    out_specs, out_shape = [], []
    for tc, dt, bc, cb in outs:
        out_specs.append(pl.BlockSpec((tm, bc), lambda j, i, cb=cb: (i, cb(j))))
        out_shape.append(jax.ShapeDtypeStruct((T, tc), dt))
    for r, tc, bc, cb in accs:
        out_specs.append(pl.BlockSpec((r, bc), lambda j, i, cb=cb: (0, cb(j))))
        out_shape.append(jax.ShapeDtypeStruct((r, tc), F32))
    nin, nout, nacc = len(args), len(outs), len(accs)

    def body(*refs):
        i = pl.program_id(1)
        res = fn(i, n, *[r[...] for r in refs[:nin]])
        for r, v in zip(refs[nin:nin + nout], res[:nout]):
            r[...] = v.astype(r.dtype)
        if nacc:
            acc_refs = refs[nin + nout:nin + nout + nacc]

            @pl.when(i == 0)
            def _():
                for r in acc_refs:
                    r[...] = jnp.zeros_like(r)

            for r, v in zip(acc_refs, res[nout:]):
                r[...] += v.astype(F32)

    res = pl.pallas_call(
        body, name=name, grid=(ncol, n), in_specs=in_specs, out_specs=out_specs, out_shape=out_shape,
        compiler_params=pltpu.CompilerParams(dimension_semantics=("arbitrary", "arbitrary")),
    )(*args)
    return res


def _cur(arr, bc=None, blk=0):
    bc = arr.shape[1] if bc is None else bc
    return (arr, bc, functools.partial(_const, v=blk), "cur")


def _halo(arr, kind, bc=None, blk=0):
    bc = arr.shape[1] if bc is None else bc
    return (arr, bc, functools.partial(_const, v=blk), kind)


def _cst(arr):
    return (arr, arr.shape[1], functools.partial(_const, v=0))


def _out(cols, dt):
    return (cols, dt, cols, functools.partial(_const, v=0))


def _acc(rows, cols):
    return (rows, cols, cols, functools.partial(_const, v=0))


def _rms(x, w):
    return x * lax.rsqrt(jnp.mean(x * x, axis=-1, keepdims=True) + NORM_EPS) * w


def _sigmoid(x):
    return 0.5 * jnp.tanh(0.5 * x) + 0.5


def _silu(x):
    return x * _sigmoid(x)


def _dsilu(x):
    s = _sigmoid(x)
    return s * (1.0 + x * (1.0 - s))


def _softplus(x):
    return jnp.maximum(x, 0.0) + jnp.log1p(jnp.exp(-jnp.abs(x)))


def _shift(a, k):
    return pltpu.roll(a, k % a.shape[0], 0)


def _lroll(a, k):
    return pltpu.roll(a, k % a.shape[1], 1)


def _vjp_wrap(f, nrow, nconst, add_first=False):
    def g(i, n, *vals):
        rows, consts, mid = vals[:nrow], vals[len(vals) - nconst:], vals[nrow:len(vals) - nconst]
        cots = mid[:-1] if add_first else mid
        outs, pull = jax.vjp(f, *rows, *consts)
        grads = list(pull(tuple(c.astype(o.dtype) for c, o in zip(cots, outs))))
        if add_first:
            grads[0] = grads[0] + mid[-1]
        return tuple(grads)
    return g


def _rows_vjp(f, T, tm, rows, consts, cots, out_dtypes, name):
    return _rows(_vjp_wrap(f, len(rows), len(consts)), T, tm, [_cur(r) for r in rows] + [_cur(c) for c in cots],
                 [_cst(c) for c in consts], [_out(r.shape[1], dt) for r, dt in zip(rows, out_dtypes)],
                 [_acc(1, c.shape[1]) for c in consts], name)


def _f_premix(x, g):
    return (_rms(x, g),)


def _f_mla_pre(cq, ckv, qn, kvn):
    return _rms(cq, qn), _rms(ckv, kvn)


def _f_ssd_gate(y, z, nw):
    return (_rms(y * _silu(z), nw),)


def _f_post_mix(x, mixed, gpost, gffn):
    x1 = x + _rms(mixed, gpost)
    return x1, _rms(x1, gffn)


def _f_post_ffn(x1, d, gpost):
    return (x1 + _rms(d, gpost),)


def _rope_fwd(v, cosf, sina, sinb):
    return v * cosf + _lroll(v, -16) * sina + _lroll(v, 16) * sinb


def _rope_bwd(g, cosf, sina, sinb):
    return g * cosf + _lroll(g * sina, 16) + _lroll(g * sinb, -16)


def _k_rope_fwd(i, n, qpad, kvpad, kr, cosf, sina, sinb):
    qs, ks = [], []
    krr = _rope_fwd(kr, cosf, sina, sinb)
    for h in range(HEADS):
        sl = slice(h * HP, (h + 1) * HP)
        qs.append(_rope_fwd(qpad[:, sl], cosf, sina, sinb))
        ks.append(kvpad[:, sl].astype(F32) + krr)
    return jnp.concatenate(qs, axis=1), jnp.concatenate(ks, axis=1)


def _k_rope_bwd(i, n, dq, dk, dv, cosf, sina, sinb):
    lane = lax.broadcasted_iota(jnp.int32, (1, HP), 1)
    rmask = ((lane >= KR_LANE) & (lane < KR_LANE + ROPE)).astype(F32)
    dqs, dks = [], []
    dkr = jnp.zeros((dq.shape[0], HP), F32)
    for h in range(HEADS):
        sl = slice(h * HP, (h + 1) * HP)
        dqs.append(_rope_bwd(dq[:, sl], cosf, sina, sinb))
        dkh = dk[:, sl]
        dkr = dkr + dkh * rmask
        dks.append(dkh * (1.0 - rmask))
    dkr = _rope_bwd(dkr, cosf, sina, sinb) * rmask
    return jnp.concatenate(dqs, axis=1), jnp.concatenate(dks + [dv], axis=1), dkr


def _k_sconv_fwd(i, n, b, c, h, cp, hp, w):
    m = b.shape[0]
    up = jnp.where(i > 0, cp * hp, 0.0)
    ue = jnp.concatenate([up, c * h], axis=0)
    conv = w[2:3] * ue + w[1:2] * _shift(ue, 1) + w[0:1] * _shift(ue, 2)
    return (b * conv[HALO:],)


def _k_sconv_bwd(i, n, b, c, h, dy, cp, hp, bn, dyn, w):
    m = b.shape[0]
    up = jnp.where(i > 0, cp * hp, 0.0)
    ue = jnp.concatenate([up, c * h], axis=0)
    u1, u2 = _shift(ue, 1), _shift(ue, 2)
    conv = (w[2:3] * ue + w[1:2] * u1 + w[0:1] * u2)[HALO:]
    dc_cur = dy * b
    dce = jnp.concatenate([dc_cur, jnp.where(i < n - 1, dyn * bn, 0.0)], axis=0)
    du = (w[2:3] * dce + w[1:2] * _shift(dce, -1) + w[0:1] * _shift(dce, -2))[:m]
    dw = jnp.concatenate([
        jnp.sum(dc_cur * u2[HALO:], axis=0, keepdims=True),
        jnp.sum(dc_cur * u1[HALO:], axis=0, keepdims=True),
        jnp.sum(dc_cur * ue[HALO:], axis=0, keepdims=True),
        jnp.zeros((HALO - 3, b.shape[1]), F32)], axis=0)
    return dy * conv, du * h, du * c, dw


def _conv4(ue, w):
    return w[3:4] * ue + w[2:3] * _shift(ue, 1) + w[1:2] * _shift(ue, 2) + w[0:1] * _shift(ue, 3)


def _k_ssdconv_fwd(i, n, u, up, w, bias):
    ue = jnp.concatenate([jnp.where(i > 0, up, 0.0), u], axis=0)
    return (_silu(_conv4(ue, w)[HALO:] + bias),)


def _k_ssdconv_bwd(i, n, u, dout, up, un, doutn, w, bias):
    m = u.shape[0]
    ue = jnp.concatenate([jnp.where(i > 0, up, 0.0), u, un], axis=0)
    u1, u2, u3 = _shift(ue, 1), _shift(ue, 2), _shift(ue, 3)
    pre = (w[3:4] * ue + w[2:3] * u1 + w[1:2] * u2 + w[0:1] * u3)[HALO:] + bias
    doe = jnp.concatenate([dout, jnp.where(i < n - 1, doutn, 0.0)], axis=0)
    dpre = doe * _dsilu(pre)
    du = (w[3:4] * dpre + w[2:3] * _shift(dpre, -1) + w[1:2] * _shift(dpre, -2) + w[0:1] * _shift(dpre, -3))[:m]
    dp = dpre[:m]
    cur = slice(HALO, HALO + m)
    dw = jnp.concatenate([
        jnp.sum(dp * u3[cur], axis=0, keepdims=True),
        jnp.sum(dp * u2[cur], axis=0, keepdims=True),
        jnp.sum(dp * u1[cur], axis=0, keepdims=True),
        jnp.sum(dp * ue[cur], axis=0, keepdims=True),
        jnp.zeros((HALO - 4, u.shape[1]), F32)], axis=0)
    db = jnp.sum(dp, axis=0, keepdims=True)
    return du, dw, db


def _conv3(ue, w):
    return w[2:3] * ue + w[1:2] * _shift(ue, 1) + w[0:1] * _shift(ue, 2)


def _k_ffnact_fwd(i, n, ug, uu, ugp, uup, wg, wu, bg, bu):
    gate = _conv3(jnp.concatenate([jnp.where(i > 0, ugp, 0.0), ug], axis=0), wg)[HALO:] + bg
    upv = _conv3(jnp.concatenate([jnp.where(i > 0, uup, 0.0), uu], axis=0), wu)[HALO:] + bu
    return (_silu(gate) * upv,)


def _k_ffnact_bwd(i, n, ug, uu, dact, ugp, uup, ugn, uun, dactn, wg, wu, bg, bu):
    m = ug.shape[0]
    cur = slice(HALO, HALO + m)

    def taps(p, c, nx):
        e = jnp.concatenate([jnp.where(i > 0, p, 0.0), c, nx], axis=0)
        return e, _shift(e, 1), _shift(e, 2)

    def back(d, w):
        return (w[2:3] * d + w[1:2] * _shift(d, -1) + w[0:1] * _shift(d, -2))[:m]

    def wgrad(d, t):
        return jnp.concatenate([jnp.sum(d[:m] * t[2][cur], axis=0, keepdims=True), jnp.sum(d[:m] * t[1][cur], axis=0, keepdims=True),
                                jnp.sum(d[:m] * t[0][cur], axis=0, keepdims=True), jnp.zeros((HALO - 3, d.shape[1]), F32)], axis=0)

    tg, tu = taps(ugp, ug, ugn), taps(uup, uu, uun)
    gate = (wg[2:3] * tg[0] + wg[1:2] * tg[1] + wg[0:1] * tg[2])[HALO:] + bg
    upv = (wu[2:3] * tu[0] + wu[1:2] * tu[1] + wu[0:1] * tu[2])[HALO:] + bu
    dae = jnp.concatenate([dact, jnp.where(i < n - 1, dactn, 0.0)], axis=0)
    sg = _sigmoid(gate)
    dg = dae * upv * (sg * (1.0 + gate * (1.0 - sg)))
    dup = dae * (gate * sg)
    return (back(dg, wg), back(dup, wu), wgrad(dg, tg), wgrad(dup, tu),
            jnp.sum(dg[:m], axis=0, keepdims=True), jnp.sum(dup[:m], axis=0, keepdims=True))


def _k_loss(i, n, y, tgt):
    e = y - tgt
    part = 0.5 * jnp.sum(jnp.sum(e * e, axis=1, keepdims=True) / D_MODEL, axis=0, keepdims=True)
    return e * (1.0 / D_MODEL), jnp.broadcast_to(part, (1, LANE))


def _k_adam(i, n, w, g, m, v):
    m = ADAM_B1 * m + (1.0 - ADAM_B1) * g
    v = ADAM_B2 * v + (1.0 - ADAM_B2) * (g * g)
    m_hat = m / (1.0 - ADAM_B1 ** ADAM_STEP)
    v_hat = v / (1.0 - ADAM_B2 ** ADAM_STEP)
    delta = -ADAM_LR * (m_hat / (jnp.sqrt(v_hat) + ADAM_EPS) + ADAM_WD * w)
    return g, delta, m, v


def _dotf(a, b, dims):
    return lax.dot_general(a.astype(MXU_DTYPE), b.astype(MXU_DTYPE), dims, preferred_element_type=F32)


NN = (((1,), (0,)), ((), ()))
NT = (((1,), (1,)), ((), ()))
TN = (((0,), (0,)), ((), ()))


def _ssd_chunk(x0, x1, x2, x3, b0, b1, c0, c1, dtraw, p0, p1, p2, p3, dtb, alog, dsk):
    xs, bs, cs_, ps = (x0, x1, x2, x3), (b0, b1), (c0, c1), (p0, p1, p2, p3)
    L = dtraw.shape[0]
    dt = _softplus(dtraw + dtb)
    adt = dt * (-jnp.exp(alog))
    row = lax.broadcasted_iota(jnp.int32, (L, L), 0)
    col = lax.broadcasted_iota(jnp.int32, (L, L), 1)
    tril = row >= col
    cum = jnp.dot(tril.astype(F32), adt, precision=HIGHEST, preferred_element_type=F32)
    cum_t = cum.T
    lane = lax.broadcasted_iota(jnp.int32, (1, LANE), 1)
    sub = lax.broadcasted_iota(jnp.int32, (LANE, 1), 0)
    lastcol = (lax.broadcasted_iota(jnp.int32, (1, L), 1) == L - 1).astype(F32)
    ys, news = [], []
    for h in range(SSD_HEADS):
        g = h // (SSD_HEADS // 2)
        oh = (lane == h).astype(F32)
        dth = jnp.sum(dt * oh, axis=1, keepdims=True)
        csh = jnp.sum(cum * oh, axis=1, keepdims=True)
        csr = jnp.sum(cum_t * (sub == h).astype(F32), axis=0, keepdims=True)
        cl = jnp.sum(csr * lastcol, axis=1, keepdims=True)
        dskh = jnp.sum(dsk * oh, axis=1, keepdims=True)
        x, bm, cm, prev = xs[h], bs[g], cs_[g], ps[h]
        xdt = x * dth
        decay = jnp.exp(jnp.where(tril, csh - csr, -jnp.inf))
        scores = _dotf(cm, bm, NT) * decay
        y_diag = _dotf(scores, xdt, NN)
        bd = bm * jnp.exp(cl - csh)
        cst = _dotf(xdt, bd, TN)
        news.append(prev * jnp.exp(cl) + cst)
        y_off = _dotf(cm, prev, NT) * jnp.exp(csh)
        ys.append(y_diag + y_off + x * dskh)
    return (*ys, *news)


SSD_STEP = 2


def _ssd_operands(x_ref, dt_ref, par_ref, prev, rows):
    xs = [x_ref[rows, h * SSD_HEAD_DIM:(h + 1) * SSD_HEAD_DIM] for h in range(SSD_HEADS)]
    bs = [x_ref[rows, SSD_DIM + g * SSD_STATE:SSD_DIM + (g + 1) * SSD_STATE] for g in range(2)]
    cs_ = [x_ref[rows, SSD_DIM + 2 * SSD_STATE + g * SSD_STATE:SSD_DIM + 2 * SSD_STATE + (g + 1) * SSD_STATE] for g in range(2)]
    return (*xs, *bs, *cs_, dt_ref[rows, :], *prev, par_ref[0:1, :], par_ref[1:2, :], par_ref[2:3, :])


def _ssd_fwd(xbc, dtraw, par, T, dt_blk=0):
    L = SSD_CHUNK
    nc = T // L
    P = SSD_HEAD_DIM
    U = SSD_STEP if nc % SSD_STEP == 0 else 1

    def body(x_ref, dt_ref, par_ref, y_ref, st_ref, state):
        @pl.when(pl.program_id(0) == 0)
        def _():
            state[...] = jnp.zeros_like(state)

        for u in range(U):
            rows = slice(u * L, (u + 1) * L)
            st_ref[u] = state[...]
            prev = [state[h * P:(h + 1) * P, :] for h in range(SSD_HEADS)]
            res = _ssd_chunk(*_ssd_operands(x_ref, dt_ref, par_ref, prev, rows))
            for h in range(SSD_HEADS):
                y_ref[rows, h * P:(h + 1) * P] = res[h]
                state[h * P:(h + 1) * P, :] = res[SSD_HEADS + h]

    return pl.pallas_call(
        body, name="ssd_scan_fwd", grid=(nc // U,),
        in_specs=[pl.BlockSpec((U * L, SSD_CONV_DIM), lambda c: (c, 0)), pl.BlockSpec((U * L, LANE), lambda c: (c, dt_blk)),
                  pl.BlockSpec((8, LANE), lambda c: (0, 0))],
        out_specs=[pl.BlockSpec((U * L, SSD_DIM), lambda c: (c, 0)), pl.BlockSpec((U, SSD_DIM, SSD_STATE), lambda c: (c, 0, 0))],
        out_shape=[jax.ShapeDtypeStruct((T, SSD_DIM), F32), jax.ShapeDtypeStruct((nc, SSD_DIM, SSD_STATE), F32)],
        scratch_shapes=[pltpu.VMEM((SSD_DIM, SSD_STATE), F32)],
        compiler_params=pltpu.CompilerParams(dimension_semantics=("arbitrary",)),
    )(xbc, dtraw, par)


def _ssd_bwd(xbc, dtraw, par, states, dy, T, dt_blk=0):
    L = SSD_CHUNK
    nc = T // L
    P = SSD_HEAD_DIM
    U = SSD_STEP if nc % SSD_STEP == 0 else 1
    ns = nc // U

    def body(x_ref, dt_ref, par_ref, st_ref, dy_ref, dx_ref, ddt_ref, dpar_ref, dstate):
        @pl.when(pl.program_id(0) == 0)
        def _():
            dstate[...] = jnp.zeros_like(dstate)
            dpar_ref[...] = jnp.zeros_like(dpar_ref)

        for u in reversed(range(U)):
            rows = slice(u * L, (u + 1) * L)
            prev = [st_ref[u, h * P:(h + 1) * P, :] for h in range(SSD_HEADS)]
            prim = _ssd_operands(x_ref, dt_ref, par_ref, prev, rows)
            _, pull = jax.vjp(_ssd_chunk, *prim)
            cots = tuple(dy_ref[rows, h * P:(h + 1) * P] for h in range(SSD_HEADS)) + tuple(
                dstate[h * P:(h + 1) * P, :] for h in range(SSD_HEADS))
            g = pull(cots)
            for h in range(SSD_HEADS):
                dx_ref[rows, h * P:(h + 1) * P] = g[h]
                dstate[h * P:(h + 1) * P, :] = g[9 + h]
            for k in range(2):
                dx_ref[rows, SSD_DIM + k * SSD_STATE:SSD_DIM + (k + 1) * SSD_STATE] = g[4 + k]
                dx_ref[rows, SSD_DIM + 2 * SSD_STATE + k * SSD_STATE:SSD_DIM + 2 * SSD_STATE + (k + 1) * SSD_STATE] = g[6 + k]
            ddt_ref[rows, :] = g[8]
            for r in range(3):
                dpar_ref[r:r + 1, :] += g[13 + r]

    rev = lambda c: (ns - 1 - c, 0)
    return pl.pallas_call(
        body, name="ssd_scan_bwd", grid=(ns,),
        in_specs=[pl.BlockSpec((U * L, SSD_CONV_DIM), rev), pl.BlockSpec((U * L, LANE), lambda c: (ns - 1 - c, dt_blk)),
                  pl.BlockSpec((8, LANE), lambda c: (0, 0)),
                  pl.BlockSpec((U, SSD_DIM, SSD_STATE), lambda c: (ns - 1 - c, 0, 0)), pl.BlockSpec((U * L, SSD_DIM), rev)],
        out_specs=[pl.BlockSpec((U * L, SSD_CONV_DIM), rev), pl.BlockSpec((U * L, LANE), rev), pl.BlockSpec((8, LANE), lambda c: (0, 0))],
        out_shape=[jax.ShapeDtypeStruct((T, SSD_CONV_DIM), F32), jax.ShapeDtypeStruct((T, LANE), F32),
                   jax.ShapeDtypeStruct((8, LANE), F32)],
        scratch_shapes=[pltpu.VMEM((SSD_DIM, SSD_STATE), F32)],
        compiler_params=pltpu.CompilerParams(dimension_semantics=("arbitrary",)),
    )(xbc, dtraw, par, states, dy)


def _causal_pairs(nq, by_query):
    if by_query:
        pairs = [(i, j) for i in range(nq) for j in range(i + 1)]
    else:
        pairs = [(i, j) for j in range(nq) for i in range(j, nq)]
    return jnp.asarray([p[0] for p in pairs], jnp.int32), jnp.asarray([p[1] for p in pairs], jnp.int32)


def _flash_fwd(q, k, kv, T, carry=()):
    tq = tk = min(FLASH_BLOCK, T)
    nq = T // tq
    G = FLASH_HEADS_FWD
    rep = tk // HP
    nc = len(carry)
    qi, kj = _causal_pairs(nq, by_query=True)
    nh, nt = HEADS // G, qi.shape[0]

    def body(qi_ref, kj_ref, q_ref, k_ref, v_ref, *rest):
        w_refs, o_ref, g_refs = rest[:nc], rest[nc], rest[nc + 1:2 * nc + 1]
        m_ref, l_ref, acc_ref = rest[2 * nc + 1:2 * nc + 4]
        h, t = pl.program_id(0), pl.program_id(1)
        i, j = qi_ref[t], kj_ref[t]
        if nc:
            plan = lambda: _ag_plan(w_refs, g_refs, rest[2 * nc + 4:])

            @pl.when((h == 0) & (t == 0))
            def _():
                for cp in plan()[0]:
                    cp.start()

        @pl.when(j == 0)
        def _():
            m_ref[...] = jnp.full_like(m_ref, -jnp.inf)
            l_ref[...] = jnp.zeros_like(l_ref)
            acc_ref[...] = jnp.zeros_like(acc_ref)

        def step(diagonal):
            for g in range(G):
                sl = slice(g * HP, (g + 1) * HP)
                s = _dotf(q_ref[:, sl], k_ref[:, sl], NT) * QK_SCALE
                if diagonal:
                    rows = lax.broadcasted_iota(jnp.int32, (tq, tk), 0)
                    cols = lax.broadcasted_iota(jnp.int32, (tq, tk), 1)
                    s = jnp.where(rows >= cols, s, -jnp.inf)
                m_old = m_ref[:, sl]
                m_new = jnp.maximum(m_old, jnp.max(s, axis=1, keepdims=True))
                p = jnp.exp(s - jnp.tile(m_new, (1, rep)))
                alpha = jnp.exp(m_old - m_new)
                l_ref[:, sl] = alpha * l_ref[:, sl] + jnp.sum(p, axis=1, keepdims=True)
                acc_ref[:, sl] = alpha * acc_ref[:, sl] + _dotf(p, v_ref[:, sl], NN)
                m_ref[:, sl] = m_new

        @pl.when(j < i)
        def _():
            step(False)

        @pl.when(j == i)
        def _():
            step(True)
            lane = lax.broadcasted_iota(jnp.int32, (tq, HP), 1)
            for g in range(G):
                sl = slice(g * HP, (g + 1) * HP)
                l = l_ref[:, sl]
                o_ref[:, sl] = jnp.where(lane < VDIM, acc_ref[:, sl] / l, m_ref[:, sl] + jnp.log(l))

        if nc:
            @pl.when(h * nt + t == (3 * nh * nt) // 4)
            def _():
                _, lands, forwards, _ = plan()
                for land, fw in zip(lands, forwards):
                    land.wait_recv()
                    fw.start()

            @pl.when((h == nh - 1) & (t == nt - 1))
            def _():
                sends, _, forwards, finals = plan()
                for cp in finals:
                    cp.wait_recv()
                for cp in sends + forwards:
                    cp.wait_send()

    W = G * HP
    res = pl.pallas_call(
        body, name="mla_flash_fwd",
        grid_spec=pltpu.PrefetchScalarGridSpec(
            num_scalar_prefetch=2, grid=(nh, nt),
            in_specs=[pl.BlockSpec((tq, W), lambda h, t, qi, kj: (qi[t], h)),
                      pl.BlockSpec((tk, W), lambda h, t, qi, kj: (kj[t], h)),
                      pl.BlockSpec((tk, W), lambda h, t, qi, kj: (kj[t], HEADS // G + h))] + [ANY] * nc,
            out_specs=[pl.BlockSpec((tq, W), lambda h, t, qi, kj: (qi[t], h))] + [ANY] * nc,
            scratch_shapes=[pltpu.VMEM((tq, W), F32), pltpu.VMEM((tq, W), F32), pltpu.VMEM((tq, W), F32)] + (_ag_sems(nc) if nc else [])),
        out_shape=[jax.ShapeDtypeStruct((T, HEADS * HP), F32)] + [jax.ShapeDtypeStruct((N_CHIPS,) + w.shape, w.dtype) for w in carry],
        compiler_params=pltpu.CompilerParams(dimension_semantics=("arbitrary", "arbitrary")),
    )(qi, kj, q, k, kv, *carry)
    return res[0] if not nc else (res[0], [_own_slot(g, w) for g, w in zip(res[1:], carry)])


def _flash_bwd(q, k, kv, o, dycat, T, carry=()):
    tq = tk = min(FLASH_BLOCK, T)
    nq = T // tq
    G = FLASH_HEADS
    nc = len(carry)
    qi, kj = _causal_pairs(nq, by_query=False)
    nh, nt = HEADS // G, qi.shape[0]

    W = G * HP

    def body(qi_ref, kj_ref, q_ref, k_ref, v_ref, o_ref, do_ref, *rest):
        p_refs, (dq_out, dk_ref, dv_ref), part_refs = rest[:nc], rest[nc:nc + 3], rest[nc + 3:2 * nc + 3]
        dq_ref, dq_sem = rest[2 * nc + 3:2 * nc + 5]
        h, t = pl.program_id(0), pl.program_id(1)
        i, j = qi_ref[t], kj_ref[t]
        if nc:
            plan = lambda: _chip_plan(p_refs, part_refs, rest[2 * nc + 5:])

            @pl.when((h == 0) & (t == 0))
            def _():
                for cp in plan()[0]:
                    cp.start()

        @pl.when(t == 0)
        def _():
            dq_ref[...] = jnp.zeros_like(dq_ref)

        @pl.when(i == j)
        def _():
            dk_ref[...] = jnp.zeros_like(dk_ref)
            dv_ref[...] = jnp.zeros_like(dv_ref)

        def step(diagonal):
            r0 = pl.multiple_of(i * tq, tq)
            for g in range(G):
                sl = slice(g * HP, (g + 1) * HP)
                qv, kv, vv, ov, dov = q_ref[:, sl], k_ref[:, sl], v_ref[:, sl], o_ref[:, sl], do_ref[:, sl]
                s = _dotf(qv, kv, NT) * QK_SCALE
                p = jnp.exp(s - ov[:, VDIM:VDIM + 1])
                if diagonal:
                    rows = lax.broadcasted_iota(jnp.int32, (tq, tk), 0)
                    cols = lax.broadcasted_iota(jnp.int32, (tq, tk), 1)
                    p = jnp.where(rows >= cols, p, 0.0)
                dsum = jnp.sum(dov * ov, axis=1, keepdims=True)
                dv_ref[:, sl] += _dotf(p, dov, TN)
                dp = _dotf(dov, vv, NT)
                ds = p * (dp - dsum) * QK_SCALE
                dk_ref[:, sl] += _dotf(ds, qv, TN)
                dq_ref[pl.ds(r0, tq), sl] += _dotf(ds, kv, NN)

        @pl.when(i > j)
        def _():
            step(False)

        @pl.when(i == j)
        def _():
            step(True)

        @pl.when(t == nt - 1)
        def _():
            out = pltpu.make_async_copy(dq_ref, dq_out.at[:, pl.ds(pl.multiple_of(h * W, W), W)], dq_sem)
            out.start()
            out.wait()

        if nc:
            @pl.when((h == nh - 1) & (t == nt - 1))
            def _():
                sends, lands = plan()
                for cp in lands:
                    cp.wait_recv()
                for cp in sends:
                    cp.wait_send()

    qmap = lambda h, t, qi, kj: (qi[t], h)
    kmap = lambda h, t, qi, kj: (kj[t], h)
    vmap = lambda h, t, qi, kj: (kj[t], HEADS // G + h)
    res = pl.pallas_call(
        body, name="mla_flash_bwd",
        grid_spec=pltpu.PrefetchScalarGridSpec(
            num_scalar_prefetch=2, grid=(nh, nt),
            in_specs=[pl.BlockSpec((tq, W), qmap), pl.BlockSpec((tk, W), kmap), pl.BlockSpec((tk, W), vmap),
                      pl.BlockSpec((tq, W), qmap), pl.BlockSpec((tq, W), qmap)] + [ANY] * nc,
            out_specs=[ANY, pl.BlockSpec((tk, W), kmap), pl.BlockSpec((tk, W), kmap)] + [ANY] * nc,
            scratch_shapes=[pltpu.VMEM((T, W), F32), pltpu.SemaphoreType.DMA] + (_chip_sems(nc) if nc else [])),
        out_shape=[jax.ShapeDtypeStruct((T, HEADS * HP), F32)] * 3 + [jax.ShapeDtypeStruct(p.shape, p.dtype) for p in carry],
        compiler_params=pltpu.CompilerParams(dimension_semantics=("arbitrary", "arbitrary")),
    )(qi, kj, q, k, kv, o, dycat, *carry)
    return tuple(res[:3]) if not nc else (*res[:3], _chip_parts(res[3:], carry))


_IN_SRC = (0, 256, 384, 416, 672, 928, 1184, 1440, 2208, 2212)
_IN_DST = (Z_CQ, Z_CKV, Z_KR + KR_LANE, Z_SCB, Z_SCC, Z_SCH, Z_SSZ, Z_XBC, Z_DT)


def _pad_rows_in(w):
    ax = w.ndim - 2

    def zeros(n):
        return jnp.zeros(w.shape[:ax] + (n,) + w.shape[ax + 1:], w.dtype)

    def whole_tiles(p):
        n = p.shape[ax]
        return p if n % SLAB_ALIGN == 0 else jnp.pad(p, [(0, 0)] * ax + [(0, -n % SLAB_ALIGN), (0, 0)])

    parts, at = [], 0
    for s0, s1, d0 in zip(_IN_SRC[:-1], _IN_SRC[1:], _IN_DST):
        if d0 > at:
            parts.append(zeros(d0 - at))
        parts.append(whole_tiles(lax.slice_in_dim(w, s0, s1, axis=ax)))
        at = d0 + parts[-1].shape[ax]
    parts.append(zeros(ZIN - at))
    return jnp.concatenate(parts, axis=ax)


def _unpad_rows_in(w):
    ax = w.ndim - 2
    groups = list(zip(_IN_SRC[:-1], _IN_SRC[1:], _IN_DST))
    parts = [lax.slice_in_dim(w, d0, d0 + -(-(s1 - s0) // SLAB_ALIGN) * SLAB_ALIGN, axis=ax) for s0, s1, d0 in groups]
    return lax.slice_in_dim(jnp.concatenate(parts, axis=ax), 0, _IN_SRC[-1], axis=ax)


def _pad_heads(w, width):
    w = w.reshape(w.shape[:-1] + (HEADS, width))
    w = jnp.pad(w, [(0, 0)] * (w.ndim - 1) + [(0, HP - width)])
    return w.reshape(w.shape[:-2] + (HEADS * HP,))


def _unpad_heads(w, width):
    w = w.reshape(w.shape[:-1] + (HEADS, HP))[..., :width]
    return w.reshape(w.shape[:-2] + (HEADS * width,))


def _pad_kv(w):
    w = w.reshape(w.shape[:-1] + (HEADS, NOPE + VDIM))
    return jnp.concatenate([_pad_heads(w[..., :NOPE].reshape(w.shape[:-2] + (HEADS * NOPE,)), NOPE),
                            _pad_heads(w[..., NOPE:].reshape(w.shape[:-2] + (HEADS * VDIM,)), VDIM)], axis=-1)


def _unpad_kv(w):
    k = _unpad_heads(w[..., :HEADS * HP], NOPE).reshape(w.shape[:-1] + (HEADS, NOPE))
    v = _unpad_heads(w[..., HEADS * HP:], VDIM).reshape(w.shape[:-1] + (HEADS, VDIM))
    return jnp.concatenate([k, v], axis=-1).reshape(w.shape[:-1] + (HEADS * (NOPE + VDIM),))


def _pad_out_rows(w):
    lead, d = w.shape[:-2], w.shape[-1]
    att = w[..., :HEADS * VDIM, :].reshape(lead + (HEADS, VDIM, d))
    att = jnp.pad(att, [(0, 0)] * (att.ndim - 2) + [(0, HP - VDIM), (0, 0)]).reshape(lead + (HEADS * HP, d))
    return jnp.concatenate([att, w[..., HEADS * VDIM:, :]], axis=-2)


def _unpad_out_rows(w):
    lead, d = w.shape[:-2], w.shape[-1]
    att = w[..., :HEADS * HP, :].reshape(lead + (HEADS, HP, d))[..., :VDIM, :].reshape(lead + (HEADS * VDIM, d))
    return jnp.concatenate([att, w[..., HEADS * HP:, :]], axis=-2)


def _rows8(w):
    return jnp.pad(w.astype(F32), [(0, 0)] * (w.ndim - 2) + [(0, 8 - w.shape[-2]), (0, 0)])


def _row8(*vecs):
    c = vecs[0].shape[-1]
    return jnp.concatenate([v.reshape(1, c).astype(F32) for v in vecs] + [jnp.zeros((8 - len(vecs), c), F32)], axis=0)


def _rope_tables(positions):
    inv_freq = 1.0 / (ROPE_THETA ** (jnp.arange(0, ROPE, 2, dtype=F32) / ROPE))
    ang = positions.astype(F32)[:, None] * inv_freq
    cos, sin = jnp.cos(ang), jnp.sin(ang)
    T = positions.shape[0]
    half = ROPE // 2
    one = jnp.ones((T, KR_LANE), F32)
    zero = jnp.zeros((T, KR_LANE), F32)
    tail1 = jnp.ones((T, HP - KR_LANE - ROPE), F32)
    tail0 = jnp.zeros((T, HP - KR_LANE - ROPE), F32)
    z16 = jnp.zeros((T, half), F32)
    cosf = jnp.concatenate([one, cos, cos, tail1], axis=1)
    sina = jnp.concatenate([zero, -sin, z16, tail0], axis=1)
    sinb = jnp.concatenate([zero, z16, sin, tail0], axis=1)
    return cosf, sina, sinb


def _kernel_weights(W):
    c = lambda a: a.astype(MXU_DTYPE)
    forms = dict(
        w_in=("w_in", lambda w: c(_pad_rows_in(w))),
        w_q=("mla_w_q_up", lambda w: c(_pad_heads(w, NOPE + ROPE))),
        w_kv=("mla_w_kv_up", lambda w: c(_pad_kv(w))),
        w_out=("w_out", lambda w: c(_pad_out_rows(w))),
        w_up=("ffn_w_up", c),
        w_down=("ffn_w_down", c),
        sc_w=("sc_conv_w", _rows8),
        ssd_w=("ssd_conv_w", _rows8),
        ffn_w=("ffn_conv_w", _rows8),
    )
    return {k: f(W[n]) for k, (n, f) in forms.items() if n in W}


def _layer_weights(KW, l):
    return {k: (v[l] if k in ("sc_w", "ssd_w", "ffn_w") else (v, l)) for k, v in KW.items()}


def _local_step(x, positions, target, W, S, ex=None):
    T = x.shape[0]
    tm = min(ROW_BLOCK, T)
    tm_ffn = min(FFN_ROWS, T)
    cosf, sina, sinb = _rope_tables(positions)
    if ex is None:
        KW = _kernel_weights(W)
    else:
        early = _all_gather_weights(ex.shard(0, "early"))
    saved = []
    xl = x
    for l in range(DEPTH):
        lw = _layer_weights(KW, l) if ex is None else _kernel_weights(ex.weights(early, "early"))
        g_pre = S["norm_mix_pre"][l].reshape(1, -1)
        g_post = S["norm_mix_post"][l].reshape(1, -1)
        g_fpre = S["norm_ffn_pre"][l].reshape(1, -1)
        g_fpost = S["norm_ffn_post"][l].reshape(1, -1)
        qn = S["mla_q_norm"][l].reshape(1, -1)
        kvn = S["mla_kv_norm"][l].reshape(1, -1)
        ssd_b = S["ssd_conv_b"][l].reshape(1, -1)
        ssd_par = _row8(jnp.pad(S["ssd_dt_bias"][l], (0, LANE - SSD_HEADS)), jnp.pad(S["ssd_a_log"][l], (0, LANE - SSD_HEADS)),
                        jnp.pad(S["ssd_d"][l], (0, LANE - SSD_HEADS)))
        ssd_nw = S["ssd_norm"][l].reshape(1, -1)
        ffn_b = S["ffn_conv_b"][l].reshape(1, -1)

        (h1,) = _rows(lambda i, n, *v: _f_premix(*v), T, tm, [_cur(xl)], [_cst(g_pre)], [_out(D_MODEL, BF16)], [], "pre_mix_norm")
        zin = _mm(h1, lw["w_in"], "nt", F32, "mm_in")
        qlat, kvlat = _rows(lambda i, n, *v: _f_mla_pre(*v), T, tm, [_cur(zin, Q_LORA, 0), _cur(zin, KV_LORA, Z_CKV // KV_LORA)],
                            [_cst(qn), _cst(kvn)], [_out(Q_LORA, BF16), _out(KV_LORA, BF16)], [], "mla_pre_norm")
        qpad = _mm(qlat, lw["w_q"], "nn", F32, "mm_q_up")
        kvpad = _mm(kvlat, lw["w_kv"], "nn", BF16, "mm_kv_up")
        qr, kr = _rows(_k_rope_fwd, T, tm, [_cur(qpad), _cur(kvpad, HEADS * HP, 0), _cur(zin, LANE, Z_KR // LANE),
                                            _cur(cosf), _cur(sina), _cur(sinb)], [],
                       [_out(HEADS * HP, BF16), _out(HEADS * HP, BF16)], [], "mla_rope")
        if ex is None:
            o = _flash_fwd(qr, kr, kvpad, T)
        else:
            nlate = len(ex.layouts["late"])
            o, got = _flash_fwd(qr, kr, kvpad, T, carry=ex.shard(l, "late") + (ex.shard(l + 1, "early") if l + 1 < DEPTH else []))
            lw.update(_kernel_weights(ex.weights(got[:nlate], "late")))
            early = got[nlate:]
        (yconv,) = _rows(_k_sconv_fwd, T, tm, [_cur(zin, SC_DIM, Z_SCB // SC_DIM), _cur(zin, SC_DIM, Z_SCC // SC_DIM),
                                               _cur(zin, SC_DIM, Z_SCH // SC_DIM), _halo(zin, "prev", SC_DIM, Z_SCC // SC_DIM),
                                               _halo(zin, "prev", SC_DIM, Z_SCH // SC_DIM)], [_cst(lw["sc_w"])],
                         [_out(SC_DIM, F32)], [], "short_conv_fwd")
        (xbc,) = _rows(_k_ssdconv_fwd, T, tm, [_cur(zin, SSD_CONV_DIM, Z_XBC // SSD_CONV_DIM),
                                               _halo(zin, "prev", SSD_CONV_DIM, Z_XBC // SSD_CONV_DIM)],
                       [_cst(lw["ssd_w"]), _cst(ssd_b)], [_out(SSD_CONV_DIM, F32)], [], "ssd_conv_fwd")
        yscan, states = _ssd_fwd(xbc, zin, ssd_par, T, Z_DT // LANE)
        (yssd,) = _rows(lambda i, n, *v: _f_ssd_gate(*v), T, tm, [_cur(yscan), _cur(zin, SSD_DIM, Z_SSZ // SSD_DIM)], [_cst(ssd_nw)],
                        [_out(SSD_DIM, F32)], [], "ssd_gate_fwd")
        ycat = jnp.concatenate([o.astype(BF16), yconv.astype(BF16), yssd.astype(BF16)], axis=1)
        mixed = _mm(ycat, lw["w_out"], "nn", F32, "mm_out")
        x1, h2 = _rows(lambda i, n, *v: _f_post_mix(*v), T, tm, [_cur(xl), _cur(mixed)], [_cst(g_post), _cst(g_fpre)],
                       [_out(D_MODEL, F32), _out(D_MODEL, BF16)], [], "post_mix_fwd")
        upre = _mm(h2, lw["w_up"], "nn", F32, "mm_up")
        nt = FFN_DIM // FFN_TILE
        gcol, ucol = (lambda j: j), (lambda j: j + nt)
        (act,) = _rows(_k_ffnact_fwd, T, tm_ffn,
                       [(upre, FFN_TILE, gcol, "cur"), (upre, FFN_TILE, ucol, "cur"), (upre, FFN_TILE, gcol, "prev"),
                        (upre, FFN_TILE, ucol, "prev")],
                       [(lw["ffn_w"], FFN_TILE, gcol), (lw["ffn_w"], FFN_TILE, ucol), (ffn_b, FFN_TILE, gcol), (ffn_b, FFN_TILE, ucol)],
                       [(FFN_DIM, BF16, FFN_TILE, gcol)], [], "ffn_act_fwd", ncol=nt)
        dn = _mm(act, lw["w_down"], "nn", F32, "mm_down")
        (x2,) = _rows(lambda i, n, *v: _f_post_ffn(*v), T, tm, [_cur(x1), _cur(dn)], [_cst(g_fpost)], [_out(D_MODEL, F32)], [], "post_ffn_fwd")
        saved.append(dict(lw=lw, x=xl, h1=h1, zin=zin, qlat=qlat, kvlat=kvlat, qr=qr, kr=kr, kvpad=kvpad, o=o, xbc=xbc,
                          yscan=yscan, states=states, ycat=ycat, mixed=mixed, x1=x1, h2=h2, upre=upre, act=act, dn=dn,
                          g_pre=g_pre, g_post=g_post, g_fpre=g_fpre, g_fpost=g_fpost, qn=qn, kvn=kvn, ssd_b=ssd_b,
                          ssd_par=ssd_par, ssd_nw=ssd_nw, ffn_b=ffn_b))
        xl = x2

    gx, loss_part = _rows(_k_loss, T, tm, [_cur(xl), _cur(target)], [], [_out(D_MODEL, F32)], [_acc(1, LANE)], "loss_head")

    GW = {k: [None] * DEPTH for k in ("w_in", "mla_w_q_up", "mla_w_kv_up", "sc_conv_w", "ssd_conv_w", "w_out", "ffn_w_up",
                                      "ffn_conv_w", "ffn_w_down")}
    GS = {k: [None] * DEPTH for k in ("norm_mix_pre", "norm_mix_post", "norm_ffn_pre", "norm_ffn_post", "mla_q_norm", "mla_kv_norm",
                                      "ssd_conv_b", "ssd_dt_bias", "ssd_a_log", "ssd_d", "ssd_norm", "ffn_conv_b")}
    nt = FFN_DIM // FFN_TILE
    gcol, ucol = (lambda j: j), (lambda j: j + nt)
    pending = None
    for l in reversed(range(DEPTH)):
        s = saved[l]
        lw = s["lw"]
        gx1, ddn, dgf = _rows_vjp(_f_post_ffn, T, tm, [s["x1"], s["dn"]], [s["g_fpost"]], [gx], [F32, BF16], "post_ffn_bwd")
        GS["norm_ffn_post"][l] = dgf[0]
        dact = _mm(ddn, lw["w_down"], "nt", F32, "mm_down_dx")
        GW["ffn_w_down"][l] = _mm(s["act"], ddn, "tn", BF16, "mm_down_dw")
        up = s["upre"]
        dug, duu, dwg, dwu, dbg, dbu = _rows(
            _k_ffnact_bwd, T, tm_ffn,
            [(up, FFN_TILE, gcol, "cur"), (up, FFN_TILE, ucol, "cur"), (dact, FFN_TILE, gcol, "cur"), (up, FFN_TILE, gcol, "prev"),
             (up, FFN_TILE, ucol, "prev"), (up, FFN_TILE, gcol, "next"), (up, FFN_TILE, ucol, "next"), (dact, FFN_TILE, gcol, "next")],
            [(lw["ffn_w"], FFN_TILE, gcol), (lw["ffn_w"], FFN_TILE, ucol), (s["ffn_b"], FFN_TILE, gcol), (s["ffn_b"], FFN_TILE, ucol)],
            [(FFN_DIM, BF16, FFN_TILE, gcol)] * 2,
            [(HALO, FFN_DIM, FFN_TILE, gcol)] * 2 + [(1, FFN_DIM, FFN_TILE, gcol)] * 2, "ffn_act_bwd", ncol=nt)
        GW["ffn_conv_w"][l] = jnp.concatenate([dwg[:3], dwu[:3]], axis=1)
        GS["ffn_conv_b"][l] = jnp.concatenate([dbg[0], dbu[0]])
        dh2 = _mm((dug, duu), lw["w_up"], "nt", F32, "mm_up_dx")
        GW["ffn_w_up"][l] = (_mm(s["h2"], dug, "tn", BF16, "mm_up_dw_gate"), _mm(s["h2"], duu, "tn", BF16, "mm_up_dw_up"))
        gx0, dmixed, dgp, dgf = _rows_vjp(_f_post_mix, T, tm, [s["x"], s["mixed"]], [s["g_post"], s["g_fpre"]], [gx1, dh2],
                                          [F32, BF16], "post_mix_bwd")
        GS["norm_mix_post"][l], GS["norm_ffn_pre"][l] = dgp[0], dgf[0]
        dycat = _mm(dmixed, lw["w_out"], "nt", F32, "mm_out_dx")
        GW["w_out"][l] = _unpad_out_rows(_mm(s["ycat"], dmixed, "tn", BF16, "mm_out_dw"))
        zin = s["zin"]
        dyscan, dz, dnw = _rows(_vjp_wrap(_f_ssd_gate, 2, 1), T, tm,
                                [_cur(s["yscan"]), _cur(zin, SSD_DIM, Z_SSZ // SSD_DIM), _cur(dycat, SSD_DIM, (HEADS * HP + SC_DIM) // SSD_DIM)],
                                [_cst(s["ssd_nw"])], [_out(SSD_DIM, F32), _out(SSD_DIM, BF16)], [_acc(1, SSD_DIM)], "ssd_gate_bwd")
        GS["ssd_norm"][l] = dnw[0]
        dxbc, ddtraw, dpar = _ssd_bwd(s["xbc"], zin, s["ssd_par"], s["states"], dyscan, T, Z_DT // LANE)
        GS["ssd_dt_bias"][l], GS["ssd_a_log"][l], GS["ssd_d"][l] = dpar[0, :SSD_HEADS], dpar[1, :SSD_HEADS], dpar[2, :SSD_HEADS]
        xb = Z_XBC // SSD_CONV_DIM
        dxraw, dsw, dsb = _rows(_k_ssdconv_bwd, T, tm,
                                [_cur(zin, SSD_CONV_DIM, xb), _cur(dxbc), _halo(zin, "prev", SSD_CONV_DIM, xb),
                                 _halo(zin, "next", SSD_CONV_DIM, xb), _halo(dxbc, "next")],
                                [_cst(lw["ssd_w"]), _cst(s["ssd_b"])], [_out(SSD_CONV_DIM, BF16)],
                                [_acc(HALO, SSD_CONV_DIM), _acc(1, SSD_CONV_DIM)], "ssd_conv_bwd")
        GW["ssd_conv_w"][l] = dsw[:4]
        GS["ssd_conv_b"][l] = dsb[0]
        cb = (HEADS * HP) // SC_DIM
        dscb, dscc, dsch, dscw = _rows(_k_sconv_bwd, T, tm,
                                       [_cur(zin, SC_DIM, Z_SCB // SC_DIM), _cur(zin, SC_DIM, Z_SCC // SC_DIM),
                                        _cur(zin, SC_DIM, Z_SCH // SC_DIM), _cur(dycat, SC_DIM, cb),
                                        _halo(zin, "prev", SC_DIM, Z_SCC // SC_DIM), _halo(zin, "prev", SC_DIM, Z_SCH // SC_DIM),
                                        _halo(zin, "next", SC_DIM, Z_SCB // SC_DIM), _halo(dycat, "next", SC_DIM, cb)],
                                       [_cst(lw["sc_w"])], [_out(SC_DIM, BF16)] * 3, [_acc(HALO, SC_DIM)], "short_conv_bwd")
        GW["sc_conv_w"][l] = dscw[:3]
        if ex is None:
            dq, dk, dv = _flash_bwd(s["qr"], s["kr"], s["kvpad"], s["o"], dycat, T)
        else:
            sums = ex.submit([({n: GW[n][l] for ns in LATE for n in ns}, "late")] + ([(pending, "early")] if pending else []))
            late = sums[0]
            dq, dk, dv, parts = _flash_bwd(s["qr"], s["kr"], s["kvpad"], s["o"], dycat, T, carry=[p for ps in sums for p in ps])
            ex.collect(l, "late", parts[:len(late)])
            if pending:
                ex.collect(l + 1, "early", parts[len(late):])
        dqpad, dkvpad, dkr = _rows(_k_rope_bwd, T, tm, [_cur(dq), _cur(dk), _cur(dv), _cur(cosf), _cur(sina), _cur(sinb)], [],
                                   [_out(HEADS * HP, BF16), _out(2 * HEADS * HP, BF16), _out(LANE, BF16)], [], "mla_rope_bwd")
        dqlat = _mm(dqpad, lw["w_q"], "nt", F32, "mm_q_dx")
        GW["mla_w_q_up"][l] = _unpad_heads(_mm(s["qlat"], dqpad, "tn", BF16, "mm_q_dw"), NOPE + ROPE)
        dkvlat = _mm(dkvpad, lw["w_kv"], "nt", F32, "mm_kv_dx")
        GW["mla_w_kv_up"][l] = _unpad_kv(_mm(s["kvlat"], dkvpad, "tn", BF16, "mm_kv_dw"))
        dcq, dckv, dqn, dkvn = _rows(_vjp_wrap(_f_mla_pre, 2, 2), T, tm,
                                     [_cur(zin, Q_LORA, 0), _cur(zin, KV_LORA, Z_CKV // KV_LORA), _cur(dqlat), _cur(dkvlat)],
                                     [_cst(s["qn"]), _cst(s["kvn"])], [_out(Q_LORA, BF16), _out(KV_LORA, BF16)],
                                     [_acc(1, Q_LORA), _acc(1, KV_LORA)], "mla_pre_bwd")
        GS["mla_q_norm"][l], GS["mla_kv_norm"][l] = dqn[0], dkvn[0]
        dzin = jnp.concatenate([dcq, dckv, dkr, dscb, dscc, dsch, dz, dxraw, ddtraw.astype(BF16), jnp.zeros((T, ZIN - Z_DT - LANE), BF16)], axis=1)
        dh1 = _mm(dzin, lw["w_in"], "nn", F32, "mm_in_dx")
        GW["w_in"][l] = _unpad_rows_in(_mm(dzin, s["h1"], "tn", BF16, "mm_in_dw"))
        gx, dgp = _rows(_vjp_wrap(_f_premix, 1, 1, add_first=True), T, tm, [_cur(s["x"]), _cur(dh1), _cur(gx0)], [_cst(s["g_pre"])],
                        [_out(D_MODEL, F32)], [_acc(1, D_MODEL)], "pre_mix_bwd")
        GS["norm_mix_pre"][l] = dgp[0]
        if ex is not None:
            pending = {n: GW[n][l] for ns in EARLY for n in ns}
    if ex is not None:
        ex.collect(0, "early", _rs_chip_exchange(ex.submit([(pending, "early")])[0]))
    GS = {k: jnp.stack(v) for k, v in GS.items()}
    return loss_part[0, 0], gx, GW, GS


WEIGHTS = ("norm_mix_pre", "norm_mix_post", "norm_ffn_pre", "norm_ffn_post", "w_in", "mla_q_norm", "mla_w_q_up", "mla_kv_norm",
           "mla_w_kv_up", "sc_conv_w", "ssd_conv_w", "ssd_conv_b", "ssd_dt_bias", "ssd_a_log", "ssd_d", "ssd_norm", "w_out",
           "ffn_w_up", "ffn_conv_w", "ffn_conv_b", "ffn_w_down")
SHARDED = (("w_in", 2), ("mla_w_q_up", 2), ("mla_w_kv_up", 2), ("sc_conv_w", 2), ("ssd_conv_w", 2), ("w_out", 1),
           ("ffn_w_up", 2), ("ffn_conv_w", 2), ("ffn_w_down", 1))
SMALL = tuple(n for n in WEIGHTS if n not in dict(SHARDED))
N_CHIPS = 4
N_DEV = 8
ROW_ALIGN = 64
SLAB_ALIGN = 16
EARLY = (("w_in", "mla_w_q_up", "mla_w_kv_up", "sc_conv_w", "ssd_conv_w"),)
LATE = (("ffn_w_down", "w_out"), ("ffn_w_up", "ffn_conv_w"))
TRANSPOSED = ("w_in",)


def _is_rows(shape, width):
    return shape[-1] == width and math.prod(shape[:-1]) % SLAB_ALIGN == 0


def _is_short(shape, width):
    return len(shape) == 2 and shape[1] == width and not _is_rows(shape, width)


def _slab_rows(shape, width):
    if _is_rows(shape, width):
        return math.prod(shape[:-1])
    if _is_short(shape, width):
        return -(-shape[0] // SLAB_ALIGN) * SLAB_ALIGN
    return -(-math.prod(shape) // (width * SLAB_ALIGN)) * SLAB_ALIGN


def _slab(piece, width, dtype, lead=0):
    ld, shape = piece.shape[:lead], piece.shape[lead:]
    rows = _slab_rows(shape, width)
    if _is_rows(shape, width):
        return piece.astype(dtype).reshape(ld + (rows, width))
    if _is_short(shape, width):
        return jnp.pad(piece.astype(dtype), [(0, 0)] * lead + [(0, rows - shape[0]), (0, 0)])
    flat = piece.astype(dtype).reshape(ld + (-1,))
    return jnp.pad(flat, [(0, 0)] * lead + [(0, rows * width - flat.shape[-1])]).reshape(ld + (rows, width))


def _unslab(slab, shape, lead=0):
    ld = slab.shape[:lead]
    if _is_rows(shape, slab.shape[-1]):
        return slab.reshape(ld + tuple(shape))
    if _is_short(shape, slab.shape[-1]):
        return slab[..., :shape[0], :]
    return slab.reshape(ld + (-1,))[..., :math.prod(shape)].reshape(ld + tuple(shape))


def _layout(shapes, names, width):
    ents, off = [], 0
    for n in names:
        shp = tuple(shapes[n])
        todo = [(None, False, shp), (None, True, shp)] if n.endswith("conv_w") else [(l, False, shp[1:]) for l in range(shp[0])]
        for l, lo, ps in todo:
            r = _slab_rows(ps, width)
            ents.append((n, l, lo, ps, off, r))
            off += r
    return width, -(-off // ROW_ALIGN) * ROW_ALIGN, ents


def _pack(layout, piece, dtype, lead=0):
    width, rows, ents = layout
    slabs, ld = [], None
    for n, l, lo, ps, off, r in ents:
        p = piece(n, l, lo)
        slabs.append(None if p is None else _slab(p, width, dtype, lead))
        ld = ld if p is None else p.shape[:lead]
    used = ents[-1][4] + ents[-1][5]
    slabs = [jnp.zeros(ld + (e[5], width), dtype) if s is None else s for s, e in zip(slabs, ents)]
    if rows > used:
        slabs.append(jnp.zeros(ld + (rows - used, width), dtype))
    return jnp.concatenate(slabs, axis=lead)


ANY = pl.BlockSpec(memory_space=pl.ANY)


def _pos():
    return lax.axis_index("x"), lax.axis_index("y"), lax.axis_index("c")


def _other_chips(x, y):
    return ((1 - x, y), (x, 1 - y), (1 - x, 1 - y))


def _remote(src, dst, ssem, rsem, dev):
    return pltpu.make_async_remote_copy(src_ref=src, dst_ref=dst, send_sem=ssem, recv_sem=rsem, device_id=dev, device_id_type=MESH)


AG_CHUNKS = 2


def _chip_index():
    return 2 * lax.axis_index("x") + lax.axis_index("y")


def _ag_sems(nbuf):
    return [pltpu.SemaphoreType.DMA((nbuf * 3 * AG_CHUNKS,))] * 4


def _ag_plan(w_refs, out_refs, sems):
    isend, irecv, dsend, drecv = sems
    x, y, c = _pos()
    k = 2 * x + y
    sib = (x, y, 1 - c)
    sends, lands, forwards, finals = [], [], [], []
    s = 0
    for w_ref, out_ref in zip(w_refs, out_refs):
        H = w_ref.shape[0] // 2
        CH = H // AG_CHUNKS
        for cx, cy in _other_chips(x, y):
            for ch in range(AG_CHUNKS):
                mine = out_ref.at[k, pl.ds(c * H + ch * CH, CH), :]
                near = out_ref.at[2 * cx + cy, pl.ds(c * H + ch * CH, CH), :]
                far = out_ref.at[2 * cx + cy, pl.ds((1 - c) * H + ch * CH, CH), :]
                sends.append(_remote(w_ref.at[pl.ds(c * H + ch * CH, CH), :], mine, isend.at[s], irecv.at[s], (cx, cy, c)))
                lands.append(_remote(near, near, isend.at[s], irecv.at[s], (cx, cy, c)))
                forwards.append(_remote(near, near, dsend.at[s], drecv.at[s], sib))
                finals.append(_remote(far, far, dsend.at[s], drecv.at[s], sib))
                s += 1
    return sends, lands, forwards, finals


def _own_slot(got, own):
    return lax.dynamic_update_slice(got, own[None], (_chip_index(), 0, 0))


def _all_gather_weights(ws):
    nb = len(ws)

    def body(*refs):
        sends, lands, forwards, finals = _ag_plan(refs[:nb], refs[nb:2 * nb], refs[2 * nb:])
        for cp in sends:
            cp.start()
        for land, fw in zip(lands, forwards):
            land.wait_recv()
            fw.start()
        for cp in finals:
            cp.wait_recv()
        for cp in sends + forwards:
            cp.wait_send()

    got = pl.pallas_call(
        body, name="all_gather_weights", in_specs=[ANY] * nb, out_specs=[ANY] * nb,
        out_shape=[jax.ShapeDtypeStruct((N_CHIPS,) + w.shape, w.dtype) for w in ws], scratch_shapes=_ag_sems(nb),
    )(*ws)
    return [_own_slot(g, w) for g, w in zip(got, ws)]


def _rs_pair_exchange(gs):
    nb = len(gs)

    def body(*refs):
        g_refs, got_refs, (ssem, rsem) = refs[:nb], refs[nb:2 * nb], refs[2 * nb:]
        x, y, c = _pos()
        cps = []
        for b, (g_ref, got_ref) in enumerate(zip(g_refs, got_refs)):
            H = g_ref.shape[1] // 2
            for kk in range(N_CHIPS):
                s = b * N_CHIPS + kk
                cps.append(_remote(g_ref.at[kk, pl.ds((1 - c) * H, H), :], got_ref.at[kk], ssem.at[s], rsem.at[s], (x, y, 1 - c)))
        for cp in cps:
            cp.start()
        for cp in cps:
            cp.wait()

    return pl.pallas_call(
        body, name="rs_pair_exchange", in_specs=[ANY] * nb, out_specs=[ANY] * nb,
        out_shape=[jax.ShapeDtypeStruct((N_CHIPS, g.shape[1] // 2, g.shape[2]), g.dtype) for g in gs],
        scratch_shapes=[pltpu.SemaphoreType.DMA((nb * N_CHIPS,))] * 2,
    )(*gs)


def _chip_sems(nbuf):
    return [pltpu.SemaphoreType.DMA((nbuf * 3,))] * 2


def _chip_plan(p_refs, out_refs, sems):
    ssem, rsem = sems
    x, y, c = _pos()
    sends, lands = [], []
    s = 0
    for p_ref, out_ref in zip(p_refs, out_refs):
        for cx, cy in _other_chips(x, y):
            sends.append(_remote(p_ref.at[2 * cx + cy], out_ref.at[2 * x + y], ssem.at[s], rsem.at[s], (cx, cy, c)))
            land = out_ref.at[2 * cx + cy]
            lands.append(_remote(land, land, ssem.at[s], rsem.at[s], (cx, cy, c)))
            s += 1
    return sends, lands


def _chip_parts(got, ps):
    k = _chip_index()
    return [lax.dynamic_update_slice(g, lax.dynamic_slice_in_dim(p, k, 1, axis=0), (k, 0, 0)) for g, p in zip(got, ps)]


def _rs_chip_exchange(ps):
    nb = len(ps)

    def body(*refs):
        sends, lands = _chip_plan(refs[:nb], refs[nb:2 * nb], refs[2 * nb:])
        for cp in sends:
            cp.start()
        for cp in lands:
            cp.wait_recv()
        for cp in sends:
            cp.wait_send()

    got = pl.pallas_call(
        body, name="rs_chip_exchange", in_specs=[ANY] * nb, out_specs=[ANY] * nb,
        out_shape=[jax.ShapeDtypeStruct(p.shape, p.dtype) for p in ps], scratch_shapes=_chip_sems(nb),
    )(*ps)
    return _chip_parts(got, ps)


def _rs_pair_share(fs):
    nb = len(fs)

    def body(*refs):
        f_refs, out_refs, (ssem, rsem) = refs[:nb], refs[nb:2 * nb], refs[2 * nb:]
        x, y, c = _pos()
        sends, lands = [], []
        for b, (f_ref, out_ref) in enumerate(zip(f_refs, out_refs)):
            sends.append(_remote(f_ref, out_ref.at[c], ssem.at[b], rsem.at[b], (x, y, 1 - c)))
            land = out_ref.at[1 - c]
            lands.append(_remote(land, land, ssem.at[b], rsem.at[b], (x, y, 1 - c)))
        for cp in sends:
            cp.start()
        for cp in lands:
            cp.wait_recv()
        for cp in sends:
            cp.wait_send()

    got = pl.pallas_call(
        body, name="rs_pair_share", in_specs=[ANY] * nb, out_specs=[ANY] * nb,
        out_shape=[jax.ShapeDtypeStruct((2,) + f.shape, f.dtype) for f in fs],
        scratch_shapes=[pltpu.SemaphoreType.DMA((nb,))] * 2,
    )(*fs)
    return [lax.dynamic_update_slice(g, f[None], (lax.axis_index("c"), 0, 0)) for g, f in zip(got, fs)]


def _all_reduce_small(s):
    r, C = s.shape

    def body(s_ref, o_ref, buf, ssem, rsem):
        x, y, c = _pos()
        me = 4 * x + 2 * y + c
        buf[me] = s_ref[...]
        cps = []
        for m in range(1, N_DEV):
            mx, my, mc = (m >> 2) & 1, (m >> 1) & 1, m & 1
            peer = (x ^ mx, y ^ my, c ^ mc)
            cp = _remote(s_ref, buf.at[me], ssem.at[m - 1], rsem.at[m - 1], peer)
            cp.start()
            cps.append(cp)
        for m in range(1, N_DEV):
            mx, my, mc = (m >> 2) & 1, (m >> 1) & 1, m & 1
            src = 4 * (x ^ mx) + 2 * (y ^ my) + (c ^ mc)
            _remote(s_ref, buf.at[src], ssem.at[m - 1], rsem.at[m - 1], (x ^ mx, y ^ my, c ^ mc)).wait_recv()
        for cp in cps:
            cp.wait_send()
        acc = buf[0]
        for j in range(1, N_DEV):
            acc = acc + buf[j]
        o_ref[...] = acc

    return pl.pallas_call(
        body, name="all_reduce_small", in_specs=[pl.BlockSpec(memory_space=pltpu.VMEM)],
        out_specs=pl.BlockSpec(memory_space=pltpu.VMEM), out_shape=jax.ShapeDtypeStruct((r, C), F32),
        scratch_shapes=[pltpu.VMEM((N_DEV, r, C), F32), pltpu.SemaphoreType.DMA((N_DEV - 1,)), pltpu.SemaphoreType.DMA((N_DEV - 1,))],
    )(s)


def _rtile(n, pref):
    if n <= pref:
        return n
    t = (pref // 16) * 16
    while t >= 16:
        if n % t == 0:
            return t
        t -= 16
    raise ValueError(f"no row tile for {n}")


def _rs_pair_sums(gpks):
    gots = _rs_pair_exchange(gpks)
    out = []
    for gpk, got in zip(gpks, gots):
        _, R, C = gpk.shape
        H = R // 2
        own = lax.dynamic_index_in_dim(gpk.reshape(N_CHIPS, 2, H, C), lax.axis_index("c"), axis=1, keepdims=False)
        (part,) = _rows(lambda i, n, a, b: (a.astype(F32) + b.astype(F32),), N_CHIPS * H, _rtile(N_CHIPS * H, 512),
                        [_cur(own.reshape(N_CHIPS * H, C)), _cur(got.reshape(N_CHIPS * H, C))], [], [_out(C, BF16)], [], "rs_pair_add")
        out.append(part.reshape(N_CHIPS, H, C))
    return out


def _rs_chip_sums(parts):
    def add4(i, n, a, b, c, d):
        return (((a.astype(F32) + b.astype(F32)) + c.astype(F32)) + d.astype(F32),)

    out = []
    for p in parts:
        _, H, C = p.shape
        tm = _rtile(H, 1024)
        (red,) = _rows(add4, H, tm, [(p.reshape(N_CHIPS * H, C), C, functools.partial(_const, v=0), j * (H // tm)) for j in range(N_CHIPS)],
                       [], [_out(C, F32)], [], "rs_chip_add")
        out.append(red)
    return out


class _Exchange:
    def __init__(self, a):
        self.a = a
        self.axis = {n: (1 if n in TRANSPOSED else ax) for n, ax in SHARDED}
        shapes = {n: (1,) + tuple(self.packed(n, a[n]).shape[1:]) for n in self.axis}
        widths = lambda names: shapes[names[0]][-1] if names[0] == "ffn_w_up" else PACK_COLS
        self.layouts = {"early": [_layout(shapes, ns, widths(ns)) for ns in EARLY], "late": [_layout(shapes, ns, widths(ns)) for ns in LATE]}
        self.reduced = {}

    @staticmethod
    def packed(n, w):
        return jnp.swapaxes(w, -1, -2) if n in TRANSPOSED else w

    def shard(self, l, group):
        def piece(n, li, lo):
            w = self.packed(n, self.a[n][l:l + 1] if li is None else self.a[n][l])
            return w - w.astype(BF16).astype(F32) if lo else w
        return [_pack(lay, piece, BF16) for lay in self.layouts[group]]

    def weights(self, gathered, group):
        W, resid = {}, {}
        for (width, rows, ents), g in zip(self.layouts[group], gathered):
            for n, li, lo, ps, off, r in ents:
                parts = _unslab(g[:, off:off + r], ps, lead=1)
                ax = self.axis[n] + (1 if li is None else 0)
                full = jnp.moveaxis(parts, 0, ax - 1)
                full = full.reshape(full.shape[:ax - 1] + (-1,) + full.shape[ax + 1:])
                (resid if lo else W)[n] = full[0] if li is None else full
        for n in resid:
            W[n] = W[n].astype(F32) + resid[n].astype(F32)
        return W

    def submit(self, jobs):
        def by_chip(g, ax, parts=N_CHIPS):
            g = g.reshape(g.shape[:ax] + (parts, g.shape[ax] // parts) + g.shape[ax + 1:])
            return jnp.moveaxis(g, ax, 0)

        def pieces_of(GW):
            def piece(n, li, lo):
                if lo:
                    return None
                g = GW[n]
                if isinstance(g, tuple):
                    return jnp.concatenate([by_chip(h, self.axis[n] - 1, N_CHIPS // 2) for h in g])
                return by_chip(g[None], self.axis[n]) if li is None else by_chip(g, self.axis[n] - 1)
            return piece

        sums = _rs_pair_sums([_pack(lay, pieces_of(GW), BF16, lead=1) for GW, group in jobs for lay in self.layouts[group]])
        out, at = [], 0
        for _, group in jobs:
            out.append(sums[at:at + len(self.layouts[group])])
            at += len(self.layouts[group])
        return out

    def collect(self, l, group, parts):
        self.reduced[l, group] = _rs_chip_sums(parts)

    def finish(self):
        keys = [(l, g) for l in range(DEPTH) for g in self.layouts]
        flat = _rs_pair_share([f for key in keys for f in self.reduced[key]])
        both, at = {}, 0
        for key in keys:
            both[key] = flat[at:at + len(self.layouts[key[1]])]
            at += len(self.layouts[key[1]])
        grads = {}
        for group, lays in self.layouts.items():
            for b, (width, rows, ents) in enumerate(lays):
                for n, li, lo, ps, off, r in ents:
                    if not lo:
                        per_layer = [self.packed(n, _unslab(both[l, group][b].reshape(rows, width)[off:off + r], ps)) for l in range(DEPTH)]
                        grads[n] = jnp.concatenate(per_layer) if li is None else jnp.stack(per_layer)
        return grads


def _adam(w, g, m, v, name, g_row=0):
    shp = w.shape
    two = lambda a: a.reshape(-1, shp[-1])
    rows = math.prod(shp[:-1])
    tm = _rtile(rows, 256)
    assert g_row % tm == 0
    g_in = (two(g), shp[-1], functools.partial(_const, v=0), g_row // tm)
    res = _rows(_k_adam, rows, tm, [_cur(two(w)), g_in, _cur(two(m)), _cur(two(v))], [], [_out(shp[-1], F32)] * 4, [], name)
    return tuple(r.reshape(shp) for r in res)


def _pack_flat(parts, rows):
    flat = jnp.concatenate([p.astype(F32).reshape(-1) for p in parts])
    return jnp.pad(flat, (0, rows * PACK_COLS - flat.shape[0])).reshape(rows, PACK_COLS)


def _unpack_flat(buf, shapes):
    flat, out, off = buf.reshape(-1), [], 0
    for shp in shapes:
        n = math.prod(shp)
        out.append(flat[off:off + n].reshape(shp))
        off += n
    return out


def kernel(x, positions, norm_mix_pre, norm_mix_post, norm_ffn_pre, norm_ffn_post, w_in, mla_q_norm, mla_w_q_up, mla_kv_norm, mla_w_kv_up, sc_conv_w, ssd_conv_w, ssd_conv_b, ssd_dt_bias, ssd_a_log, ssd_d, ssd_norm, w_out, ffn_w_up, ffn_conv_w, ffn_conv_b, ffn_w_down, loss_target, m_norm_mix_pre, m_norm_mix_post, m_norm_ffn_pre, m_norm_ffn_post, m_w_in, m_mla_q_norm, m_mla_w_q_up, m_mla_kv_norm, m_mla_w_kv_up, m_sc_conv_w, m_ssd_conv_w, m_ssd_conv_b, m_ssd_dt_bias, m_ssd_a_log, m_ssd_d, m_ssd_norm, m_w_out, m_ffn_w_up, m_ffn_conv_w, m_ffn_conv_b, m_ffn_w_down, v_norm_mix_pre, v_norm_mix_post, v_norm_ffn_pre, v_norm_ffn_post, v_w_in, v_mla_q_norm, v_mla_w_q_up, v_mla_kv_norm, v_mla_w_kv_up, v_sc_conv_w, v_ssd_conv_w, v_ssd_conv_b, v_ssd_dt_bias, v_ssd_a_log, v_ssd_d, v_ssd_norm, v_w_out, v_ffn_w_up, v_ffn_conv_w, v_ffn_conv_b, v_ffn_w_down):
    a = dict(locals())
    ex = _Exchange(a)
    S = {n: a[n] for n in SMALL}
    loss_part, gx, _, GS = _local_step(a["x"][0], a["positions"][0], a["loss_target"][0], None, S, ex)

    grads, delta, new_m, new_v = {}, {}, {}, {}
    for n, g in ex.finish().items():
        grads[n], delta[n], new_m[n], new_v[n] = _adam(a[n], g, a["m_" + n], a["v_" + n], "adamw_" + n)

    small_shapes = [a[n].shape for n in SMALL]
    rs = -(-(sum(math.prod(s) for s in small_shapes) + 1) // (PACK_COLS * SLAB_ALIGN)) * SLAB_ALIGN
    red = _all_reduce_small(_pack_flat([GS[n] for n in SMALL] + [loss_part.reshape(1)], rs))
    loss = _unpack_flat(red, small_shapes + [(1,)])[-1][0]
    pk = lambda pre: _pack_flat([a[pre + n] for n in SMALL], rs)
    for dst, buf in zip((grads, delta, new_m, new_v), _adam(pk(""), red, pk("m_"), pk("v_"), "adamw_small")):
        dst.update(zip(SMALL, _unpack_flat(buf, small_shapes)))

    return (loss, gx[None], *[grads[n] for n in WEIGHTS], *[delta[n] for n in WEIGHTS], *[new_m[n] for n in WEIGHTS],
            *[new_v[n] for n in WEIGHTS])
```

```python
import functools
import math

import jax
import jax.numpy as jnp
from jax import lax
from jax.experimental import pallas as pl
from jax.experimental.pallas import tpu as pltpu

F32 = jnp.float32
BF16 = jnp.bfloat16
MXU_DTYPE = jnp.bfloat16
HIGHEST = lax.Precision.HIGHEST
MESH = pl.DeviceIdType.MESH

D_MODEL = 1024
DEPTH = 4
HEADS = 8
Q_LORA = 256
KV_LORA = 128
NOPE = 64
ROPE = 32
VDIM = 64
ROPE_THETA = 10000.0
SC_DIM = 256
SSD_HEADS = 4
SSD_HEAD_DIM = 64
SSD_STATE = 128
SSD_DIM = 256
SSD_CONV_DIM = 768
SSD_CHUNK = 128
FFN_DIM = 2816
NORM_EPS = 1e-6
QK_SCALE = (NOPE + ROPE) ** -0.5
LANE = 128
HP = 128
FLASH_HEADS = 8
FLASH_HEADS_FWD = 8
FLASH_BLOCK = 512

ZIN = 2560
Z_CQ, Z_CKV, Z_KR, Z_SCB, Z_SCC, Z_SCH, Z_SSZ, Z_XBC, Z_DT = 0, 256, 384, 512, 768, 1024, 1280, 1536, 2304
KR_LANE = 64
FFN_TILE = 256
FFN_ROWS = 2048
ROW_BLOCK = 512

ADAM_LR, ADAM_B1, ADAM_B2, ADAM_EPS, ADAM_WD, ADAM_STEP = 0.001, 0.9, 0.999, 1e-08, 0.01, 10

PACK_COLS = 1024


def _tile(n, pref):
    if n <= pref:
        return n
    t = (pref // LANE) * LANE
    while t >= LANE:
        if n % t == 0:
            return t
        t -= LANE
    raise ValueError(f"no tile for {n}")


MM_TM, MM_TN, MM_TK = 1024, 1408, 1536


def _mm(a, b, mode, out_dtype, name, tm=None, tn=MM_TN, tkmax=MM_TK):
    pair = isinstance(a, tuple)
    a_list = list(a) if pair else [a]
    layer = None
    if isinstance(b, tuple):
        b, layer = b
    bshape = b.shape[-2:]
    if mode == "nn":
        (M, Ka), (_, N) = a_list[0].shape, bshape
    elif mode == "nt":
        (M, Ka), (N, _) = a_list[0].shape, bshape
    else:
        (Ka, M), (_, N) = a_list[0].shape, bshape
    tk = _tile(Ka, tkmax)
    nka = Ka // tk
    nk = nka * len(a_list)
    if tm is None:
        tm = MM_TN if mode == "tn" else (2 * MM_TM if nk == 1 else MM_TM)
    tm, tn = _tile(M, tm), _tile(N, tn)

    def bspec(shape, index):
        if layer is None:
            return pl.BlockSpec(shape, index)
        return pl.BlockSpec((None,) + shape, lambda i, j, k: (layer,) + index(i, j, k))

    if mode == "nn":
        a_specs = [pl.BlockSpec((tm, tk), lambda i, j, k: (i, jnp.minimum(k, nka - 1))),
                   pl.BlockSpec((tm, tk), lambda i, j, k: (i, jnp.maximum(k - nka, 0)))][:len(a_list)]
        b_spec = bspec((tk, tn), lambda i, j, k: (k, j))
        dims = NN
    elif mode == "nt":
        a_specs = [pl.BlockSpec((tm, tk), lambda i, j, k: (i, jnp.minimum(k, nka - 1))),
                   pl.BlockSpec((tm, tk), lambda i, j, k: (i, jnp.maximum(k - nka, 0)))][:len(a_list)]
        b_spec = bspec((tn, tk), lambda i, j, k: (j, k))
        dims = NT
    else:
        a_specs = [pl.BlockSpec((tk, tm), lambda i, j, k: (k, i))]
        b_spec = pl.BlockSpec((tk, tn), lambda i, j, k: (k, j))
        dims = TN
    na = len(a_list)

    def body(*refs):
        a_refs, b_ref, o_ref = refs[:na], refs[na], refs[na + 1]
        k = pl.program_id(2)

        def prod(a_ref):
            return lax.dot_general(a_ref[...].astype(MXU_DTYPE), b_ref[...].astype(MXU_DTYPE), dims, preferred_element_type=F32)

        if nk == 1:
            o_ref[...] = prod(a_refs[0]).astype(o_ref.dtype)
            return
        acc_ref = refs[na + 2]

        @pl.when(k == 0)
        def _():
            acc_ref[...] = prod(a_refs[0])

        @pl.when((k > 0) & (k < nka))
        def _():
            acc_ref[...] += prod(a_refs[0])

        if pair:
            @pl.when(k >= nka)
            def _():
                acc_ref[...] += prod(a_refs[1])

        @pl.when(k == nk - 1)
        def _():
            o_ref[...] = acc_ref[...].astype(o_ref.dtype)

    return pl.pallas_call(
        body, name=name, grid=(M // tm, N // tn, nk),
        in_specs=a_specs + [b_spec], out_specs=pl.BlockSpec((tm, tn), lambda i, j, k: (i, j)),
        out_shape=jax.ShapeDtypeStruct((M, N), out_dtype),
        scratch_shapes=[pltpu.VMEM((tm, tn), F32)] if nk > 1 else [],
        compiler_params=pltpu.CompilerParams(dimension_semantics=("parallel", "parallel", "arbitrary")),
    )(*a_list, b)


HALO = 8


def _const(j, v):
    return v


def _rows(fn, T, tm, ins, consts, outs, accs, name, ncol=1):
    n = T // tm
    hb = tm // HALO
    last = T // HALO - 1
    in_specs, args = [], []
    for arr, bc, cb, kind in ins:
        if isinstance(kind, int):
            in_specs.append(pl.BlockSpec((tm, bc), lambda j, i, cb=cb, off=kind: (i + off, cb(j))))
        elif kind == "cur":
            in_specs.append(pl.BlockSpec((tm, bc), lambda j, i, cb=cb: (i, cb(j))))
        elif kind == "prev":
            in_specs.append(pl.BlockSpec((HALO, bc), lambda j, i, cb=cb: (jnp.maximum(i * hb - 1, 0), cb(j))))
        else:
            in_specs.append(pl.BlockSpec((HALO, bc), lambda j, i, cb=cb: (jnp.minimum((i + 1) * hb, last), cb(j))))
        args.append(arr)
    for arr, bc, cb in consts:
        in_specs.append(pl.BlockSpec((arr.shape[0], bc), lambda j, i, cb=cb: (0, cb(j))))
        args.append(arr)
    out_specs, out_shape = [], []
    for tc, dt, bc, cb in outs:
        out_specs.append(pl.BlockSpec((tm, bc), lambda j, i, cb=cb: (i, cb(j))))
        out_shape.append(jax.ShapeDtypeStruct((T, tc), dt))
    for r, tc, bc, cb in accs:
        out_specs.append(pl.BlockSpec((r, bc), lambda j, i, cb=cb: (0, cb(j))))
        out_shape.append(jax.ShapeDtypeStruct((r, tc), F32))
    nin, nout, nacc = len(args), len(outs), len(accs)

    def body(*refs):
        i = pl.program_id(1)
        res = fn(i, n, *[r[...] for r in refs[:nin]])
        for r, v in zip(refs[nin:nin + nout], res[:nout]):
            r[...] = v.astype(r.dtype)
        if nacc:
            acc_refs = refs[nin + nout:nin + nout + nacc]

            @pl.when(i == 0)
            def _():
                for r in acc_refs:
                    r[...] = jnp.zeros_like(r)

            for r, v in zip(acc_refs, res[nout:]):
                r[...] += v.astype(F32)

    res = pl.pallas_call(
        body, name=name, grid=(ncol, n), in_specs=in_specs, out_specs=out_specs, out_shape=out_shape,
        compiler_params=pltpu.CompilerParams(dimension_semantics=("arbitrary", "arbitrary")),
    )(*args)
    return res


def _cur(arr, bc=None, blk=0):
    bc = arr.shape[1] if bc is None else bc
    return (arr, bc, functools.partial(_const, v=blk), "cur")


def _halo(arr, kind, bc=None, blk=0):
    bc = arr.shape[1] if bc is None else bc
    return (arr, bc, functools.partial(_const, v=blk), kind)


def _cst(arr):
    return (arr, arr.shape[1], functools.partial(_const, v=0))


def _out(cols, dt):
    return (cols, dt, cols, functools.partial(_const, v=0))


def _acc(rows, cols):
    return (rows, cols, cols, functools.partial(_const, v=0))


def _rms(x, w):
    return x * lax.rsqrt(jnp.mean(x * x, axis=-1, keepdims=True) + NORM_EPS) * w


def _sigmoid(x):
    return 0.5 * jnp.tanh(0.5 * x) + 0.5


def _silu(x):
    return x * _sigmoid(x)


def _dsilu(x):
    s = _sigmoid(x)
    return s * (1.0 + x * (1.0 - s))


def _softplus(x):
    return jnp.maximum(x, 0.0) + jnp.log1p(jnp.exp(-jnp.abs(x)))


def _shift(a, k):
    return pltpu.roll(a, k % a.shape[0], 0)


def _lroll(a, k):
    return pltpu.roll(a, k % a.shape[1], 1)


def _vjp_wrap(f, nrow, nconst, add_first=False):
    def g(i, n, *vals):
        rows, consts, mid = vals[:nrow], vals[len(vals) - nconst:], vals[nrow:len(vals) - nconst]
        cots = mid[:-1] if add_first else mid
        outs, pull = jax.vjp(f, *rows, *consts)
        grads = list(pull(tuple(c.astype(o.dtype) for c, o in zip(cots, outs))))
        if add_first:
            grads[0] = grads[0] + mid[-1]
        return tuple(grads)
    return g


def _rows_vjp(f, T, tm, rows, consts, cots, out_dtypes, name):
    return _rows(_vjp_wrap(f, len(rows), len(consts)), T, tm, [_cur(r) for r in rows] + [_cur(c) for c in cots],
                 [_cst(c) for c in consts], [_out(r.shape[1], dt) for r, dt in zip(rows, out_dtypes)],
                 [_acc(1, c.shape[1]) for c in consts], name)


def _f_premix(x, g):
    return (_rms(x, g),)


def _f_mla_pre(cq, ckv, qn, kvn):
    return _rms(cq, qn), _rms(ckv, kvn)


def _f_ssd_gate(y, z, nw):
    return (_rms(y * _silu(z), nw),)


def _f_post_mix(x, mixed, gpost, gffn):
    x1 = x + _rms(mixed, gpost)
    return x1, _rms(x1, gffn)


def _f_post_ffn(x1, d, gpost):
    return (x1 + _rms(d, gpost),)


def _rope_fwd(v, cosf, sina, sinb):
    return v * cosf + _lroll(v, -16) * sina + _lroll(v, 16) * sinb


def _rope_bwd(g, cosf, sina, sinb):
    return g * cosf + _lroll(g * sina, 16) + _lroll(g * sinb, -16)


def _k_rope_fwd(i, n, qpad, kvpad, kr, cosf, sina, sinb):
    qs, ks = [], []
    krr = _rope_fwd(kr, cosf, sina, sinb)
    for h in range(HEADS):
        sl = slice(h * HP, (h + 1) * HP)
        qs.append(_rope_fwd(qpad[:, sl], cosf, sina, sinb))
        ks.append(kvpad[:, sl].astype(F32) + krr)
    return jnp.concatenate(qs, axis=1), jnp.concatenate(ks, axis=1)


def _k_rope_bwd(i, n, dq, dk, dv, cosf, sina, sinb):
    lane = lax.broadcasted_iota(jnp.int32, (1, HP), 1)
    rmask = ((lane >= KR_LANE) & (lane < KR_LANE + ROPE)).astype(F32)
    dqs, dks = [], []
    dkr = jnp.zeros((dq.shape[0], HP), F32)
    for h in range(HEADS):
        sl = slice(h * HP, (h + 1) * HP)
        dqs.append(_rope_bwd(dq[:, sl], cosf, sina, sinb))
        dkh = dk[:, sl]
        dkr = dkr + dkh * rmask
        dks.append(dkh * (1.0 - rmask))
    dkr = _rope_bwd(dkr, cosf, sina, sinb) * rmask
    return jnp.concatenate(dqs, axis=1), jnp.concatenate(dks + [dv], axis=1), dkr


def _k_sconv_fwd(i, n, b, c, h, cp, hp, w):
    m = b.shape[0]
    up = jnp.where(i > 0, cp * hp, 0.0)
    ue = jnp.concatenate([up, c * h], axis=0)
    conv = w[2:3] * ue + w[1:2] * _shift(ue, 1) + w[0:1] * _shift(ue, 2)
    return (b * conv[HALO:],)


def _k_sconv_bwd(i, n, b, c, h, dy, cp, hp, bn, dyn, w):
    m = b.shape[0]
    up = jnp.where(i > 0, cp * hp, 0.0)
    ue = jnp.concatenate([up, c * h], axis=0)
    u1, u2 = _shift(ue, 1), _shift(ue, 2)
    conv = (w[2:3] * ue + w[1:2] * u1 + w[0:1] * u2)[HALO:]
    dc_cur = dy * b
    dce = jnp.concatenate([dc_cur, jnp.where(i < n - 1, dyn * bn, 0.0)], axis=0)
    du = (w[2:3] * dce + w[1:2] * _shift(dce, -1) + w[0:1] * _shift(dce, -2))[:m]
    dw = jnp.concatenate([
        jnp.sum(dc_cur * u2[HALO:], axis=0, keepdims=True),
        jnp.sum(dc_cur * u1[HALO:], axis=0, keepdims=True),
        jnp.sum(dc_cur * ue[HALO:], axis=0, keepdims=True),
        jnp.zeros((HALO - 3, b.shape[1]), F32)], axis=0)
    return dy * conv, du * h, du * c, dw


def _conv4(ue, w):
    return w[3:4] * ue + w[2:3] * _shift(ue, 1) + w[1:2] * _shift(ue, 2) + w[0:1] * _shift(ue, 3)


def _k_ssdconv_fwd(i, n, u, up, w, bias):
    ue = jnp.concatenate([jnp.where(i > 0, up, 0.0), u], axis=0)
    return (_silu(_conv4(ue, w)[HALO:] + bias),)


def _k_ssdconv_bwd(i, n, u, dout, up, un, doutn, w, bias):
    m = u.shape[0]
    ue = jnp.concatenate([jnp.where(i > 0, up, 0.0), u, un], axis=0)
    u1, u2, u3 = _shift(ue, 1), _shift(ue, 2), _shift(ue, 3)
    pre = (w[3:4] * ue + w[2:3] * u1 + w[1:2] * u2 + w[0:1] * u3)[HALO:] + bias
    doe = jnp.concatenate([dout, jnp.where(i < n - 1, doutn, 0.0)], axis=0)
    dpre = doe * _dsilu(pre)
    du = (w[3:4] * dpre + w[2:3] * _shift(dpre, -1) + w[1:2] * _shift(dpre, -2) + w[0:1] * _shift(dpre, -3))[:m]
    dp = dpre[:m]
    cur = slice(HALO, HALO + m)
    dw = jnp.concatenate([
        jnp.sum(dp * u3[cur], axis=0, keepdims=True),
        jnp.sum(dp * u2[cur], axis=0, keepdims=True),
        jnp.sum(dp * u1[cur], axis=0, keepdims=True),
        jnp.sum(dp * ue[cur], axis=0, keepdims=True),
        jnp.zeros((HALO - 4, u.shape[1]), F32)], axis=0)
    db = jnp.sum(dp, axis=0, keepdims=True)
    return du, dw, db


def _conv3(ue, w):
    return w[2:3] * ue + w[1:2] * _shift(ue, 1) + w[0:1] * _shift(ue, 2)


def _k_ffnact_fwd(i, n, ug, uu, ugp, uup, wg, wu, bg, bu):
    gate = _conv3(jnp.concatenate([jnp.where(i > 0, ugp, 0.0), ug], axis=0), wg)[HALO:] + bg
    upv = _conv3(jnp.concatenate([jnp.where(i > 0, uup, 0.0), uu], axis=0), wu)[HALO:] + bu
    return (_silu(gate) * upv,)


def _k_ffnact_bwd(i, n, ug, uu, dact, ugp, uup, ugn, uun, dactn, wg, wu, bg, bu):
    m = ug.shape[0]
    cur = slice(HALO, HALO + m)

    def taps(p, c, nx):
        e = jnp.concatenate([jnp.where(i > 0, p, 0.0), c, nx], axis=0)
        return e, _shift(e, 1), _shift(e, 2)

    def back(d, w):
        return (w[2:3] * d + w[1:2] * _shift(d, -1) + w[0:1] * _shift(d, -2))[:m]

    def wgrad(d, t):
        return jnp.concatenate([jnp.sum(d[:m] * t[2][cur], axis=0, keepdims=True), jnp.sum(d[:m] * t[1][cur], axis=0, keepdims=True),
                                jnp.sum(d[:m] * t[0][cur], axis=0, keepdims=True), jnp.zeros((HALO - 3, d.shape[1]), F32)], axis=0)

    tg, tu = taps(ugp, ug, ugn), taps(uup, uu, uun)
    gate = (wg[2:3] * tg[0] + wg[1:2] * tg[1] + wg[0:1] * tg[2])[HALO:] + bg
    upv = (wu[2:3] * tu[0] + wu[1:2] * tu[1] + wu[0:1] * tu[2])[HALO:] + bu
    dae = jnp.concatenate([dact, jnp.where(i < n - 1, dactn, 0.0)], axis=0)
    sg = _sigmoid(gate)
    dg = dae * upv * (sg * (1.0 + gate * (1.0 - sg)))
    dup = dae * (gate * sg)
    return (back(dg, wg), back(dup, wu), wgrad(dg, tg), wgrad(dup, tu),
            jnp.sum(dg[:m], axis=0, keepdims=True), jnp.sum(dup[:m], axis=0, keepdims=True))


def _k_loss(i, n, y, tgt):
    e = y - tgt
    part = 0.5 * jnp.sum(jnp.sum(e * e, axis=1, keepdims=True) / D_MODEL, axis=0, keepdims=True)
    return e * (1.0 / D_MODEL), jnp.broadcast_to(part, (1, LANE))


def _k_adam(i, n, w, g, m, v):
    m = ADAM_B1 * m + (1.0 - ADAM_B1) * g
    v = ADAM_B2 * v + (1.0 - ADAM_B2) * (g * g)
    m_hat = m / (1.0 - ADAM_B1 ** ADAM_STEP)
    v_hat = v / (1.0 - ADAM_B2 ** ADAM_STEP)
    delta = -ADAM_LR * (m_hat / (jnp.sqrt(v_hat) + ADAM_EPS) + ADAM_WD * w)
    return g, delta, m, v


def _dotf(a, b, dims):
    return lax.dot_general(a.astype(MXU_DTYPE), b.astype(MXU_DTYPE), dims, preferred_element_type=F32)


NN = (((1,), (0,)), ((), ()))
NT = (((1,), (1,)), ((), ()))
TN = (((0,), (0,)), ((), ()))


def _ssd_chunk(x0, x1, x2, x3, b0, b1, c0, c1, dtraw, p0, p1, p2, p3, dtb, alog, dsk):
    xs, bs, cs_, ps = (x0, x1, x2, x3), (b0, b1), (c0, c1), (p0, p1, p2, p3)
    L = dtraw.shape[0]
    dt = _softplus(dtraw + dtb)
    adt = dt * (-jnp.exp(alog))
    row = lax.broadcasted_iota(jnp.int32, (L, L), 0)
    col = lax.broadcasted_iota(jnp.int32, (L, L), 1)
    tril = row >= col
    cum = jnp.dot(tril.astype(F32), adt, precision=HIGHEST, preferred_element_type=F32)
    cum_t = cum.T
    lane = lax.broadcasted_iota(jnp.int32, (1, LANE), 1)
    sub = lax.broadcasted_iota(jnp.int32, (LANE, 1), 0)
    lastcol = (lax.broadcasted_iota(jnp.int32, (1, L), 1) == L - 1).astype(F32)
    ys, news = [], []
    for h in range(SSD_HEADS):
        g = h // (SSD_HEADS // 2)
        oh = (lane == h).astype(F32)
        dth = jnp.sum(dt * oh, axis=1, keepdims=True)
        csh = jnp.sum(cum * oh, axis=1, keepdims=True)
        csr = jnp.sum(cum_t * (sub == h).astype(F32), axis=0, keepdims=True)
        cl = jnp.sum(csr * lastcol, axis=1, keepdims=True)
        dskh = jnp.sum(dsk * oh, axis=1, keepdims=True)
        x, bm, cm, prev = xs[h], bs[g], cs_[g], ps[h]
        xdt = x * dth
        decay = jnp.exp(jnp.where(tril, csh - csr, -jnp.inf))
        scores = _dotf(cm, bm, NT) * decay
        y_diag = _dotf(scores, xdt, NN)
        bd = bm * jnp.exp(cl - csh)
        cst = _dotf(xdt, bd, TN)
        news.append(prev * jnp.exp(cl) + cst)
        y_off = _dotf(cm, prev, NT) * jnp.exp(csh)
        ys.append(y_diag + y_off + x * dskh)
    return (*ys, *news)


SSD_STEP = 2


def _ssd_operands(x_ref, dt_ref, par_ref, prev, rows):
    xs = [x_ref[rows, h * SSD_HEAD_DIM:(h + 1) * SSD_HEAD_DIM] for h in range(SSD_HEADS)]
    bs = [x_ref[rows, SSD_DIM + g * SSD_STATE:SSD_DIM + (g + 1) * SSD_STATE] for g in range(2)]
    cs_ = [x_ref[rows, SSD_DIM + 2 * SSD_STATE + g * SSD_STATE:SSD_DIM + 2 * SSD_STATE + (g + 1) * SSD_STATE] for g in range(2)]
    return (*xs, *bs, *cs_, dt_ref[rows, :], *prev, par_ref[0:1, :], par_ref[1:2, :], par_ref[2:3, :])


def _ssd_fwd(xbc, dtraw, par, T, dt_blk=0):
    L = SSD_CHUNK
    nc = T // L
    P = SSD_HEAD_DIM
    U = SSD_STEP if nc % SSD_STEP == 0 else 1

    def body(x_ref, dt_ref, par_ref, y_ref, st_ref, state):
        @pl.when(pl.program_id(0) == 0)
        def _():
            state[...] = jnp.zeros_like(state)

        for u in range(U):
            rows = slice(u * L, (u + 1) * L)
            st_ref[u] = state[...]
            prev = [state[h * P:(h + 1) * P, :] for h in range(SSD_HEADS)]
            res = _ssd_chunk(*_ssd_operands(x_ref, dt_ref, par_ref, prev, rows))
            for h in range(SSD_HEADS):
                y_ref[rows, h * P:(h + 1) * P] = res[h]
                state[h * P:(h + 1) * P, :] = res[SSD_HEADS + h]

    return pl.pallas_call(
        body, name="ssd_scan_fwd", grid=(nc // U,),
        in_specs=[pl.BlockSpec((U * L, SSD_CONV_DIM), lambda c: (c, 0)), pl.BlockSpec((U * L, LANE), lambda c: (c, dt_blk)),
                  pl.BlockSpec((8, LANE), lambda c: (0, 0))],
        out_specs=[pl.BlockSpec((U * L, SSD_DIM), lambda c: (c, 0)), pl.BlockSpec((U, SSD_DIM, SSD_STATE), lambda c: (c, 0, 0))],
        out_shape=[jax.ShapeDtypeStruct((T, SSD_DIM), F32), jax.ShapeDtypeStruct((nc, SSD_DIM, SSD_STATE), F32)],
        scratch_shapes=[pltpu.VMEM((SSD_DIM, SSD_STATE), F32)],
        compiler_params=pltpu.CompilerParams(dimension_semantics=("arbitrary",)),
    )(xbc, dtraw, par)


def _ssd_bwd(xbc, dtraw, par, states, dy, T, dt_blk=0):
    L = SSD_CHUNK
    nc = T // L
    P = SSD_HEAD_DIM
    U = SSD_STEP if nc % SSD_STEP == 0 else 1
    ns = nc // U

    def body(x_ref, dt_ref, par_ref, st_ref, dy_ref, dx_ref, ddt_ref, dpar_ref, dstate):
        @pl.when(pl.program_id(0) == 0)
        def _():
            dstate[...] = jnp.zeros_like(dstate)
            dpar_ref[...] = jnp.zeros_like(dpar_ref)

        for u in reversed(range(U)):
            rows = slice(u * L, (u + 1) * L)
            prev = [st_ref[u, h * P:(h + 1) * P, :] for h in range(SSD_HEADS)]
            prim = _ssd_operands(x_ref, dt_ref, par_ref, prev, rows)
            _, pull = jax.vjp(_ssd_chunk, *prim)
            cots = tuple(dy_ref[rows, h * P:(h + 1) * P] for h in range(SSD_HEADS)) + tuple(
                dstate[h * P:(h + 1) * P, :] for h in range(SSD_HEADS))
            g = pull(cots)
            for h in range(SSD_HEADS):
                dx_ref[rows, h * P:(h + 1) * P] = g[h]
                dstate[h * P:(h + 1) * P, :] = g[9 + h]
            for k in range(2):
                dx_ref[rows, SSD_DIM + k * SSD_STATE:SSD_DIM + (k + 1) * SSD_STATE] = g[4 + k]
                dx_ref[rows, SSD_DIM + 2 * SSD_STATE + k * SSD_STATE:SSD_DIM + 2 * SSD_STATE + (k + 1) * SSD_STATE] = g[6 + k]
            ddt_ref[rows, :] = g[8]
            for r in range(3):
                dpar_ref[r:r + 1, :] += g[13 + r]

    rev = lambda c: (ns - 1 - c, 0)
    return pl.pallas_call(
        body, name="ssd_scan_bwd", grid=(ns,),
        in_specs=[pl.BlockSpec((U * L, SSD_CONV_DIM), rev), pl.BlockSpec((U * L, LANE), lambda c: (ns - 1 - c, dt_blk)),
                  pl.BlockSpec((8, LANE), lambda c: (0, 0)),
                  pl.BlockSpec((U, SSD_DIM, SSD_STATE), lambda c: (ns - 1 - c, 0, 0)), pl.BlockSpec((U * L, SSD_DIM), rev)],
        out_specs=[pl.BlockSpec((U * L, SSD_CONV_DIM), rev), pl.BlockSpec((U * L, LANE), rev), pl.BlockSpec((8, LANE), lambda c: (0, 0))],
        out_shape=[jax.ShapeDtypeStruct((T, SSD_CONV_DIM), F32), jax.ShapeDtypeStruct((T, LANE), F32),
                   jax.ShapeDtypeStruct((8, LANE), F32)],
        scratch_shapes=[pltpu.VMEM((SSD_DIM, SSD_STATE), F32)],
        compiler_params=pltpu.CompilerParams(dimension_semantics=("arbitrary",)),
    )(xbc, dtraw, par, states, dy)


def _causal_pairs(nq, by_query):
    if by_query:
        pairs = [(i, j) for i in range(nq) for j in range(i + 1)]
    else:
        pairs = [(i, j) for j in range(nq) for i in range(j, nq)]
    return jnp.asarray([p[0] for p in pairs], jnp.int32), jnp.asarray([p[1] for p in pairs], jnp.int32)


def _flash_fwd(q, k, kv, T, carry=()):
    tq = tk = min(FLASH_BLOCK, T)
    nq = T // tq
    G = FLASH_HEADS_FWD
    rep = tk // HP
    nc = len(carry)
    qi, kj = _causal_pairs(nq, by_query=True)
    nh, nt = HEADS // G, qi.shape[0]

    def body(qi_ref, kj_ref, q_ref, k_ref, v_ref, *rest):
        w_refs, o_ref, g_refs = rest[:nc], rest[nc], rest[nc + 1:2 * nc + 1]
        m_ref, l_ref, acc_ref = rest[2 * nc + 1:2 * nc + 4]
        h, t = pl.program_id(0), pl.program_id(1)
        i, j = qi_ref[t], kj_ref[t]
        if nc:
            plan = lambda: _ag_plan(w_refs, g_refs, rest[2 * nc + 4:])

            @pl.when((h == 0) & (t == 0))
            def _():
                for cp in plan()[0]:
                    cp.start()

        @pl.when(j == 0)
        def _():
            m_ref[...] = jnp.full_like(m_ref, -jnp.inf)
            l_ref[...] = jnp.zeros_like(l_ref)
            acc_ref[...] = jnp.zeros_like(acc_ref)

        def step(diagonal):
            for g in range(G):
                sl = slice(g * HP, (g + 1) * HP)
                s = _dotf(q_ref[:, sl], k_ref[:, sl], NT) * QK_SCALE
                if diagonal:
                    rows = lax.broadcasted_iota(jnp.int32, (tq, tk), 0)
                    cols = lax.broadcasted_iota(jnp.int32, (tq, tk), 1)
                    s = jnp.where(rows >= cols, s, -jnp.inf)
                m_old = m_ref[:, sl]
                m_new = jnp.maximum(m_old, jnp.max(s, axis=1, keepdims=True))
                p = jnp.exp(s - jnp.tile(m_new, (1, rep)))
                alpha = jnp.exp(m_old - m_new)
                l_ref[:, sl] = alpha * l_ref[:, sl] + jnp.sum(p, axis=1, keepdims=True)
                acc_ref[:, sl] = alpha * acc_ref[:, sl] + _dotf(p, v_ref[:, sl], NN)
                m_ref[:, sl] = m_new

        @pl.when(j < i)
        def _():
            step(False)

        @pl.when(j == i)
        def _():
            step(True)
            lane = lax.broadcasted_iota(jnp.int32, (tq, HP), 1)
            for g in range(G):
                sl = slice(g * HP, (g + 1) * HP)
                l = l_ref[:, sl]
                o_ref[:, sl] = jnp.where(lane < VDIM, acc_ref[:, sl] / l, m_ref[:, sl] + jnp.log(l))

        if nc:
            @pl.when(h * nt + t == (3 * nh * nt) // 4)
            def _():
                _, lands, forwards, _ = plan()
                for land, fw in zip(lands, forwards):
                    land.wait_recv()
                    fw.start()

            @pl.when((h == nh - 1) & (t == nt - 1))
            def _():
                sends, _, forwards, finals = plan()
                for cp in finals:
                    cp.wait_recv()
                for cp in sends + forwards:
                    cp.wait_send()

    W = G * HP
    res = pl.pallas_call(
        body, name="mla_flash_fwd",
        grid_spec=pltpu.PrefetchScalarGridSpec(
            num_scalar_prefetch=2, grid=(nh, nt),
            in_specs=[pl.BlockSpec((tq, W), lambda h, t, qi, kj: (qi[t], h)),
                      pl.BlockSpec((tk, W), lambda h, t, qi, kj: (kj[t], h)),
                      pl.BlockSpec((tk, W), lambda h, t, qi, kj: (kj[t], HEADS // G + h))] + [ANY] * nc,
            out_specs=[pl.BlockSpec((tq, W), lambda h, t, qi, kj: (qi[t], h))] + [ANY] * nc,
            scratch_shapes=[pltpu.VMEM((tq, W), F32), pltpu.VMEM((tq, W), F32), pltpu.VMEM((tq, W), F32)] + (_ag_sems(nc) if nc else [])),
        out_shape=[jax.ShapeDtypeStruct((T, HEADS * HP), F32)] + [jax.ShapeDtypeStruct((N_CHIPS,) + w.shape, w.dtype) for w in carry],
        compiler_params=pltpu.CompilerParams(dimension_semantics=("arbitrary", "arbitrary")),
    )(qi, kj, q, k, kv, *carry)
    return res[0] if not nc else (res[0], [_own_slot(g, w) for g, w in zip(res[1:], carry)])


def _flash_bwd(q, k, kv, o, dycat, T, carry=()):
    tq = tk = min(FLASH_BLOCK, T)
    nq = T // tq
    G = FLASH_HEADS
    nc = len(carry)
    qi, kj = _causal_pairs(nq, by_query=False)
    nh, nt = HEADS // G, qi.shape[0]

    W = G * HP

    def body(qi_ref, kj_ref, q_ref, k_ref, v_ref, o_ref, do_ref, *rest):
        p_refs, (dq_out, dk_ref, dv_ref), part_refs = rest[:nc], rest[nc:nc + 3], rest[nc + 3:2 * nc + 3]
        dq_ref, dq_sem = rest[2 * nc + 3:2 * nc + 5]
        h, t = pl.program_id(0), pl.program_id(1)
        i, j = qi_ref[t], kj_ref[t]
        if nc:
            plan = lambda: _chip_plan(p_refs, part_refs, rest[2 * nc + 5:])

            @pl.when((h == 0) & (t == 0))
            def _():
                for cp in plan()[0]:
                    cp.start()

        @pl.when(t == 0)
        def _():
            dq_ref[...] = jnp.zeros_like(dq_ref)

        @pl.when(i == j)
        def _():
            dk_ref[...] = jnp.zeros_like(dk_ref)
            dv_ref[...] = jnp.zeros_like(dv_ref)

        def step(diagonal):
            r0 = pl.multiple_of(i * tq, tq)
            for g in range(G):
                sl = slice(g * HP, (g + 1) * HP)
                qv, kv, vv, ov, dov = q_ref[:, sl], k_ref[:, sl], v_ref[:, sl], o_ref[:, sl], do_ref[:, sl]
                s = _dotf(qv, kv, NT) * QK_SCALE
                p = jnp.exp(s - ov[:, VDIM:VDIM + 1])
                if diagonal:
                    rows = lax.broadcasted_iota(jnp.int32, (tq, tk), 0)
                    cols = lax.broadcasted_iota(jnp.int32, (tq, tk), 1)
                    p = jnp.where(rows >= cols, p, 0.0)
                dsum = jnp.sum(dov * ov, axis=1, keepdims=True)
                dv_ref[:, sl] += _dotf(p, dov, TN)
                dp = _dotf(dov, vv, NT)
                ds = p * (dp - dsum) * QK_SCALE
                dk_ref[:, sl] += _dotf(ds, qv, TN)
                dq_ref[pl.ds(r0, tq), sl] += _dotf(ds, kv, NN)

        @pl.when(i > j)
        def _():
            step(False)

        @pl.when(i == j)
        def _():
            step(True)

        @pl.when(t == nt - 1)
        def _():
            out = pltpu.make_async_copy(dq_ref, dq_out.at[:, pl.ds(pl.multiple_of(h * W, W), W)], dq_sem)
            out.start()
            out.wait()

        if nc:
            @pl.when((h == nh - 1) & (t == nt - 1))
            def _():
                sends, lands = plan()
                for cp in lands:
                    cp.wait_recv()
                for cp in sends:
                    cp.wait_send()

    qmap = lambda h, t, qi, kj: (qi[t], h)
    kmap = lambda h, t, qi, kj: (kj[t], h)
    vmap = lambda h, t, qi, kj: (kj[t], HEADS // G + h)
    res = pl.pallas_call(
        body, name="mla_flash_bwd",
        grid_spec=pltpu.PrefetchScalarGridSpec(
            num_scalar_prefetch=2, grid=(nh, nt),
            in_specs=[pl.BlockSpec((tq, W), qmap), pl.BlockSpec((tk, W), kmap), pl.BlockSpec((tk, W), vmap),
                      pl.BlockSpec((tq, W), qmap), pl.BlockSpec((tq, W), qmap)] + [ANY] * nc,
            out_specs=[ANY, pl.BlockSpec((tk, W), kmap), pl.BlockSpec((tk, W), kmap)] + [ANY] * nc,
            scratch_shapes=[pltpu.VMEM((T, W), F32), pltpu.SemaphoreType.DMA] + (_chip_sems(nc) if nc else [])),
        out_shape=[jax.ShapeDtypeStruct((T, HEADS * HP), F32)] * 3 + [jax.ShapeDtypeStruct(p.shape, p.dtype) for p in carry],
        compiler_params=pltpu.CompilerParams(dimension_semantics=("arbitrary", "arbitrary")),
    )(qi, kj, q, k, kv, o, dycat, *carry)
    return tuple(res[:3]) if not nc else (*res[:3], _chip_parts(res[3:], carry))


_IN_SRC = (0, 256, 384, 416, 672, 928, 1184, 1440, 2208, 2212)
_IN_DST = (Z_CQ, Z_CKV, Z_KR + KR_LANE, Z_SCB, Z_SCC, Z_SCH, Z_SSZ, Z_XBC, Z_DT)


def _pad_rows_in(w):
    ax = w.ndim - 2

    def zeros(n):
        return jnp.zeros(w.shape[:ax] + (n,) + w.shape[ax + 1:], w.dtype)

    def whole_tiles(p):
        n = p.shape[ax]
        return p if n % SLAB_ALIGN == 0 else jnp.pad(p, [(0, 0)] * ax + [(0, -n % SLAB_ALIGN), (0, 0)])

    parts, at = [], 0
    for s0, s1, d0 in zip(_IN_SRC[:-1], _IN_SRC[1:], _IN_DST):
        if d0 > at:
            parts.append(zeros(d0 - at))
        parts.append(whole_tiles(lax.slice_in_dim(w, s0, s1, axis=ax)))
        at = d0 + parts[-1].shape[ax]
    parts.append(zeros(ZIN - at))
    return jnp.concatenate(parts, axis=ax)


def _unpad_rows_in(w):
    ax = w.ndim - 2
    groups = list(zip(_IN_SRC[:-1], _IN_SRC[1:], _IN_DST))
    parts = [lax.slice_in_dim(w, d0, d0 + -(-(s1 - s0) // SLAB_ALIGN) * SLAB_ALIGN, axis=ax) for s0, s1, d0 in groups]
    return lax.slice_in_dim(jnp.concatenate(parts, axis=ax), 0, _IN_SRC[-1], axis=ax)


def _pad_heads(w, width):
    w = w.reshape(w.shape[:-1] + (HEADS, width))
    w = jnp.pad(w, [(0, 0)] * (w.ndim - 1) + [(0, HP - width)])
    return w.reshape(w.shape[:-2] + (HEADS * HP,))


def _unpad_heads(w, width):
    w = w.reshape(w.shape[:-1] + (HEADS, HP))[..., :width]
    return w.reshape(w.shape[:-2] + (HEADS * width,))


def _pad_kv(w):
    w = w.reshape(w.shape[:-1] + (HEADS, NOPE + VDIM))
    return jnp.concatenate([_pad_heads(w[..., :NOPE].reshape(w.shape[:-2] + (HEADS * NOPE,)), NOPE),
                            _pad_heads(w[..., NOPE:].reshape(w.shape[:-2] + (HEADS * VDIM,)), VDIM)], axis=-1)


def _unpad_kv(w):
    k = _unpad_heads(w[..., :HEADS * HP], NOPE).reshape(w.shape[:-1] + (HEADS, NOPE))
    v = _unpad_heads(w[..., HEADS * HP:], VDIM).reshape(w.shape[:-1] + (HEADS, VDIM))
    return jnp.concatenate([k, v], axis=-1).reshape(w.shape[:-1] + (HEADS * (NOPE + VDIM),))


def _pad_out_rows(w):
    lead, d = w.shape[:-2], w.shape[-1]
    att = w[..., :HEADS * VDIM, :].reshape(lead + (HEADS, VDIM, d))
    att = jnp.pad(att, [(0, 0)] * (att.ndim - 2) + [(0, HP - VDIM), (0, 0)]).reshape(lead + (HEADS * HP, d))
    return jnp.concatenate([att, w[..., HEADS * VDIM:, :]], axis=-2)


def _unpad_out_rows(w):
    lead, d = w.shape[:-2], w.shape[-1]
    att = w[..., :HEADS * HP, :].reshape(lead + (HEADS, HP, d))[..., :VDIM, :].reshape(lead + (HEADS * VDIM, d))
    return jnp.concatenate([att, w[..., HEADS * HP:, :]], axis=-2)


def _rows8(w):
    return jnp.pad(w.astype(F32), [(0, 0)] * (w.ndim - 2) + [(0, 8 - w.shape[-2]), (0, 0)])


def _row8(*vecs):
    c = vecs[0].shape[-1]
    return jnp.concatenate([v.reshape(1, c).astype(F32) for v in vecs] + [jnp.zeros((8 - len(vecs), c), F32)], axis=0)


def _rope_tables(positions):
    inv_freq = 1.0 / (ROPE_THETA ** (jnp.arange(0, ROPE, 2, dtype=F32) / ROPE))
    ang = positions.astype(F32)[:, None] * inv_freq
    cos, sin = jnp.cos(ang), jnp.sin(ang)
    T = positions.shape[0]
    half = ROPE // 2
    one = jnp.ones((T, KR_LANE), F32)
    zero = jnp.zeros((T, KR_LANE), F32)
    tail1 = jnp.ones((T, HP - KR_LANE - ROPE), F32)
    tail0 = jnp.zeros((T, HP - KR_LANE - ROPE), F32)
    z16 = jnp.zeros((T, half), F32)
    cosf = jnp.concatenate([one, cos, cos, tail1], axis=1)
    sina = jnp.concatenate([zero, -sin, z16, tail0], axis=1)
    sinb = jnp.concatenate([zero, z16, sin, tail0], axis=1)
    return cosf, sina, sinb


def _kernel_weights(W):
    c = lambda a: a.astype(MXU_DTYPE)
    forms = dict(
        w_in=("w_in", lambda w: c(_pad_rows_in(w))),
        w_q=("mla_w_q_up", lambda w: c(_pad_heads(w, NOPE + ROPE))),
        w_kv=("mla_w_kv_up", lambda w: c(_pad_kv(w))),
        w_out=("w_out", lambda w: c(_pad_out_rows(w))),
        w_up=("ffn_w_up", c),
        w_down=("ffn_w_down", c),
        sc_w=("sc_conv_w", _rows8),
        ssd_w=("ssd_conv_w", _rows8),
        ffn_w=("ffn_conv_w", _rows8),
    )
    return {k: f(W[n]) for k, (n, f) in forms.items() if n in W}


def _layer_weights(KW, l):
    return {k: (v[l] if k in ("sc_w", "ssd_w", "ffn_w") else (v, l)) for k, v in KW.items()}


def _local_step(x, positions, target, W, S, ex=None):
    T = x.shape[0]
    tm = min(ROW_BLOCK, T)
    tmf = min(2 * ROW_BLOCK, T)
    tm_ffn = min(FFN_ROWS, T)
    cosf, sina, sinb = _rope_tables(positions)
    if ex is None:
        KW = _kernel_weights(W)
    else:
        early = _all_gather_weights(ex.shard(0, "early"))
    saved = []
    xl = x
    for l in range(DEPTH):
        lw = _layer_weights(KW, l) if ex is None else _kernel_weights(ex.weights(early, "early"))
        g_pre = S["norm_mix_pre"][l].reshape(1, -1)
        g_post = S["norm_mix_post"][l].reshape(1, -1)
        g_fpre = S["norm_ffn_pre"][l].reshape(1, -1)
        g_fpost = S["norm_ffn_post"][l].reshape(1, -1)
        qn = S["mla_q_norm"][l].reshape(1, -1)
        kvn = S["mla_kv_norm"][l].reshape(1, -1)
        ssd_b = S["ssd_conv_b"][l].reshape(1, -1)
        ssd_par = _row8(jnp.pad(S["ssd_dt_bias"][l], (0, LANE - SSD_HEADS)), jnp.pad(S["ssd_a_log"][l], (0, LANE - SSD_HEADS)),
                        jnp.pad(S["ssd_d"][l], (0, LANE - SSD_HEADS)))
        ssd_nw = S["ssd_norm"][l].reshape(1, -1)
        ffn_b = S["ffn_conv_b"][l].reshape(1, -1)

        (h1,) = _rows(lambda i, n, *v: _f_premix(*v), T, tmf, [_cur(xl)], [_cst(g_pre)], [_out(D_MODEL, BF16)], [], "pre_mix_norm")
        zin = _mm(h1, lw["w_in"], "nt", F32, "mm_in")
        qlat, kvlat = _rows(lambda i, n, *v: _f_mla_pre(*v), T, tmf, [_cur(zin, Q_LORA, 0), _cur(zin, KV_LORA, Z_CKV // KV_LORA)],
                            [_cst(qn), _cst(kvn)], [_out(Q_LORA, BF16), _out(KV_LORA, BF16)], [], "mla_pre_norm")
        qpad = _mm(qlat, lw["w_q"], "nn", F32, "mm_q_up")
        kvpad = _mm(kvlat, lw["w_kv"], "nn", BF16, "mm_kv_up")
        qr, kr = _rows(_k_rope_fwd, T, tmf, [_cur(qpad), _cur(kvpad, HEADS * HP, 0), _cur(zin, LANE, Z_KR // LANE),
                                            _cur(cosf), _cur(sina), _cur(sinb)], [],
                       [_out(HEADS * HP, BF16), _out(HEADS * HP, BF16)], [], "mla_rope")
        if ex is None:
            o = _flash_fwd(qr, kr, kvpad, T)
        else:
            nlate = len(ex.layouts["late"])
            o, got = _flash_fwd(qr, kr, kvpad, T, carry=ex.shard(l, "late") + (ex.shard(l + 1, "early") if l + 1 < DEPTH else []))
            lw.update(_kernel_weights(ex.weights(got[:nlate], "late")))
            early = got[nlate:]
        (yconv,) = _rows(_k_sconv_fwd, T, tmf, [_cur(zin, SC_DIM, Z_SCB // SC_DIM), _cur(zin, SC_DIM, Z_SCC // SC_DIM),
                                               _cur(zin, SC_DIM, Z_SCH // SC_DIM), _halo(zin, "prev", SC_DIM, Z_SCC // SC_DIM),
                                               _halo(zin, "prev", SC_DIM, Z_SCH // SC_DIM)], [_cst(lw["sc_w"])],
                         [_out(SC_DIM, F32)], [], "short_conv_fwd")
        (xbc,) = _rows(_k_ssdconv_fwd, T, tmf, [_cur(zin, SSD_CONV_DIM, Z_XBC // SSD_CONV_DIM),
                                               _halo(zin, "prev", SSD_CONV_DIM, Z_XBC // SSD_CONV_DIM)],
                       [_cst(lw["ssd_w"]), _cst(ssd_b)], [_out(SSD_CONV_DIM, F32)], [], "ssd_conv_fwd")
        yscan, states = _ssd_fwd(xbc, zin, ssd_par, T, Z_DT // LANE)
        (yssd,) = _rows(lambda i, n, *v: _f_ssd_gate(*v), T, tmf, [_cur(yscan), _cur(zin, SSD_DIM, Z_SSZ // SSD_DIM)], [_cst(ssd_nw)],
                        [_out(SSD_DIM, F32)], [], "ssd_gate_fwd")
        ycat = jnp.concatenate([o.astype(BF16), yconv.astype(BF16), yssd.astype(BF16)], axis=1)
        mixed = _mm(ycat, lw["w_out"], "nn", F32, "mm_out")
        x1, h2 = _rows(lambda i, n, *v: _f_post_mix(*v), T, tmf, [_cur(xl), _cur(mixed)], [_cst(g_post), _cst(g_fpre)],
                       [_out(D_MODEL, F32), _out(D_MODEL, BF16)], [], "post_mix_fwd")
        upre = _mm(h2, lw["w_up"], "nn", F32, "mm_up")
        nt = FFN_DIM // FFN_TILE
        gcol, ucol = (lambda j: j), (lambda j: j + nt)
        (act,) = _rows(_k_ffnact_fwd, T, tm_ffn,
                       [(upre, FFN_TILE, gcol, "cur"), (upre, FFN_TILE, ucol, "cur"), (upre, FFN_TILE, gcol, "prev"),
                        (upre, FFN_TILE, ucol, "prev")],
                       [(lw["ffn_w"], FFN_TILE, gcol), (lw["ffn_w"], FFN_TILE, ucol), (ffn_b, FFN_TILE, gcol), (ffn_b, FFN_TILE, ucol)],
                       [(FFN_DIM, BF16, FFN_TILE, gcol)], [], "ffn_act_fwd", ncol=nt)
        dn = _mm(act, lw["w_down"], "nn", F32, "mm_down")
        (x2,) = _rows(lambda i, n, *v: _f_post_ffn(*v), T, tmf, [_cur(x1), _cur(dn)], [_cst(g_fpost)], [_out(D_MODEL, F32)], [], "post_ffn_fwd")
        saved.append(dict(lw=lw, x=xl, h1=h1, zin=zin, qlat=qlat, kvlat=kvlat, qr=qr, kr=kr, kvpad=kvpad, o=o, xbc=xbc,
                          yscan=yscan, states=states, ycat=ycat, mixed=mixed, x1=x1, h2=h2, upre=upre, act=act, dn=dn,
                          g_pre=g_pre, g_post=g_post, g_fpre=g_fpre, g_fpost=g_fpost, qn=qn, kvn=kvn, ssd_b=ssd_b,
                          ssd_par=ssd_par, ssd_nw=ssd_nw, ffn_b=ffn_b))
        xl = x2

    gx, loss_part = _rows(_k_loss, T, tmf, [_cur(xl), _cur(target)], [], [_out(D_MODEL, F32)], [_acc(1, LANE)], "loss_head")

    GW = {k: [None] * DEPTH for k in ("w_in", "mla_w_q_up", "mla_w_kv_up", "sc_conv_w", "ssd_conv_w", "w_out", "ffn_w_up",
                                      "ffn_conv_w", "ffn_w_down")}
    GS = {k: [None] * DEPTH for k in ("norm_mix_pre", "norm_mix_post", "norm_ffn_pre", "norm_ffn_post", "mla_q_norm", "mla_kv_norm",
                                      "ssd_conv_b", "ssd_dt_bias", "ssd_a_log", "ssd_d", "ssd_norm", "ffn_conv_b")}
    nt = FFN_DIM // FFN_TILE
    gcol, ucol = (lambda j: j), (lambda j: j + nt)
    pending = None
    for l in reversed(range(DEPTH)):
        s = saved[l]
        lw = s["lw"]
        gx1, ddn, dgf = _rows_vjp(_f_post_ffn, T, tm, [s["x1"], s["dn"]], [s["g_fpost"]], [gx], [F32, BF16], "post_ffn_bwd")
        GS["norm_ffn_post"][l] = dgf[0]
        dact = _mm(ddn, lw["w_down"], "nt", F32, "mm_down_dx")
        GW["ffn_w_down"][l] = _mm(s["act"], ddn, "tn", BF16, "mm_down_dw")
        up = s["upre"]
        dug, duu, dwg, dwu, dbg, dbu = _rows(
            _k_ffnact_bwd, T, tm_ffn,
            [(up, FFN_TILE, gcol, "cur"), (up, FFN_TILE, ucol, "cur"), (dact, FFN_TILE, gcol, "cur"), (up, FFN_TILE, gcol, "prev"),
             (up, FFN_TILE, ucol, "prev"), (up, FFN_TILE, gcol, "next"), (up, FFN_TILE, ucol, "next"), (dact, FFN_TILE, gcol, "next")],
            [(lw["ffn_w"], FFN_TILE, gcol), (lw["ffn_w"], FFN_TILE, ucol), (s["ffn_b"], FFN_TILE, gcol), (s["ffn_b"], FFN_TILE, ucol)],
            [(FFN_DIM, BF16, FFN_TILE, gcol)] * 2,
            [(HALO, FFN_DIM, FFN_TILE, gcol)] * 2 + [(1, FFN_DIM, FFN_TILE, gcol)] * 2, "ffn_act_bwd", ncol=nt)
        GW["ffn_conv_w"][l] = jnp.concatenate([dwg[:3], dwu[:3]], axis=1)
        GS["ffn_conv_b"][l] = jnp.concatenate([dbg[0], dbu[0]])
        dh2 = _mm((dug, duu), lw["w_up"], "nt", F32, "mm_up_dx")
        GW["ffn_w_up"][l] = (_mm(s["h2"], dug, "tn", BF16, "mm_up_dw_gate"), _mm(s["h2"], duu, "tn", BF16, "mm_up_dw_up"))
        gx0, dmixed, dgp, dgf = _rows_vjp(_f_post_mix, T, tm, [s["x"], s["mixed"]], [s["g_post"], s["g_fpre"]], [gx1, dh2],
                                          [F32, BF16], "post_mix_bwd")
        GS["norm_mix_post"][l], GS["norm_ffn_pre"][l] = dgp[0], dgf[0]
        dycat = _mm(dmixed, lw["w_out"], "nt", F32, "mm_out_dx")
        GW["w_out"][l] = _unpad_out_rows(_mm(s["ycat"], dmixed, "tn", BF16, "mm_out_dw"))
        zin = s["zin"]
        dyscan, dz, dnw = _rows(_vjp_wrap(_f_ssd_gate, 2, 1), T, tm,
                                [_cur(s["yscan"]), _cur(zin, SSD_DIM, Z_SSZ // SSD_DIM), _cur(dycat, SSD_DIM, (HEADS * HP + SC_DIM) // SSD_DIM)],
                                [_cst(s["ssd_nw"])], [_out(SSD_DIM, F32), _out(SSD_DIM, BF16)], [_acc(1, SSD_DIM)], "ssd_gate_bwd")
        GS["ssd_norm"][l] = dnw[0]
        dxbc, ddtraw, dpar = _ssd_bwd(s["xbc"], zin, s["ssd_par"], s["states"], dyscan, T, Z_DT // LANE)
        GS["ssd_dt_bias"][l], GS["ssd_a_log"][l], GS["ssd_d"][l] = dpar[0, :SSD_HEADS], dpar[1, :SSD_HEADS], dpar[2, :SSD_HEADS]
        xb = Z_XBC // SSD_CONV_DIM
        dxraw, dsw, dsb = _rows(_k_ssdconv_bwd, T, tm,
                                [_cur(zin, SSD_CONV_DIM, xb), _cur(dxbc), _halo(zin, "prev", SSD_CONV_DIM, xb),
                                 _halo(zin, "next", SSD_CONV_DIM, xb), _halo(dxbc, "next")],
                                [_cst(lw["ssd_w"]), _cst(s["ssd_b"])], [_out(SSD_CONV_DIM, BF16)],
                                [_acc(HALO, SSD_CONV_DIM), _acc(1, SSD_CONV_DIM)], "ssd_conv_bwd")
        GW["ssd_conv_w"][l] = dsw[:4]
        GS["ssd_conv_b"][l] = dsb[0]
        cb = (HEADS * HP) // SC_DIM
        dscb, dscc, dsch, dscw = _rows(_k_sconv_bwd, T, tm,
                                       [_cur(zin, SC_DIM, Z_SCB // SC_DIM), _cur(zin, SC_DIM, Z_SCC // SC_DIM),
                                        _cur(zin, SC_DIM, Z_SCH // SC_DIM), _cur(dycat, SC_DIM, cb),
                                        _halo(zin, "prev", SC_DIM, Z_SCC // SC_DIM), _halo(zin, "prev", SC_DIM, Z_SCH // SC_DIM),
                                        _halo(zin, "next", SC_DIM, Z_SCB // SC_DIM), _halo(dycat, "next", SC_DIM, cb)],
                                       [_cst(lw["sc_w"])], [_out(SC_DIM, BF16)] * 3, [_acc(HALO, SC_DIM)], "short_conv_bwd")
        GW["sc_conv_w"][l] = dscw[:3]
        if ex is None:
            dq, dk, dv = _flash_bwd(s["qr"], s["kr"], s["kvpad"], s["o"], dycat, T)
        else:
            sums = ex.submit([({n: GW[n][l] for ns in LATE for n in ns}, "late")] + ([(pending, "early")] if pending else []))
            late = sums[0]
            dq, dk, dv, parts = _flash_bwd(s["qr"], s["kr"], s["kvpad"], s["o"], dycat, T, carry=[p for ps in sums for p in ps])
            ex.collect(l, "late", parts[:len(late)])
            if pending:
                ex.collect(l + 1, "early", parts[len(late):])
        dqpad, dkvpad, dkr = _rows(_k_rope_bwd, T, tm, [_cur(dq), _cur(dk), _cur(dv), _cur(cosf), _cur(sina), _cur(sinb)], [],
                                   [_out(HEADS * HP, BF16), _out(2 * HEADS * HP, BF16), _out(LANE, BF16)], [], "mla_rope_bwd")
        dqlat = _mm(dqpad, lw["w_q"], "nt", F32, "mm_q_dx")
        GW["mla_w_q_up"][l] = _unpad_heads(_mm(s["qlat"], dqpad, "tn", BF16, "mm_q_dw"), NOPE + ROPE)
        dkvlat = _mm(dkvpad, lw["w_kv"], "nt", F32, "mm_kv_dx")
        GW["mla_w_kv_up"][l] = _unpad_kv(_mm(s["kvlat"], dkvpad, "tn", BF16, "mm_kv_dw"))
        dcq, dckv, dqn, dkvn = _rows(_vjp_wrap(_f_mla_pre, 2, 2), T, tm,
                                     [_cur(zin, Q_LORA, 0), _cur(zin, KV_LORA, Z_CKV // KV_LORA), _cur(dqlat), _cur(dkvlat)],
                                     [_cst(s["qn"]), _cst(s["kvn"])], [_out(Q_LORA, BF16), _out(KV_LORA, BF16)],
                                     [_acc(1, Q_LORA), _acc(1, KV_LORA)], "mla_pre_bwd")
        GS["mla_q_norm"][l], GS["mla_kv_norm"][l] = dqn[0], dkvn[0]
        dzin = jnp.concatenate([dcq, dckv, dkr, dscb, dscc, dsch, dz, dxraw, ddtraw.astype(BF16), jnp.zeros((T, ZIN - Z_DT - LANE), BF16)], axis=1)
        dh1 = _mm(dzin, lw["w_in"], "nn", F32, "mm_in_dx")
        GW["w_in"][l] = _unpad_rows_in(_mm(dzin, s["h1"], "tn", BF16, "mm_in_dw"))
        gx, dgp = _rows(_vjp_wrap(_f_premix, 1, 1, add_first=True), T, tm, [_cur(s["x"]), _cur(dh1), _cur(gx0)], [_cst(s["g_pre"])],
                        [_out(D_MODEL, F32)], [_acc(1, D_MODEL)], "pre_mix_bwd")
        GS["norm_mix_pre"][l] = dgp[0]
        if ex is not None:
            pending = {n: GW[n][l] for ns in EARLY for n in ns}
    if ex is not None:
        ex.collect(0, "early", _rs_chip_exchange(ex.submit([(pending, "early")])[0]))
    GS = {k: jnp.stack(v) for k, v in GS.items()}
    return loss_part[0, 0], gx, GW, GS


WEIGHTS = ("norm_mix_pre", "norm_mix_post", "norm_ffn_pre", "norm_ffn_post", "w_in", "mla_q_norm", "mla_w_q_up", "mla_kv_norm",
           "mla_w_kv_up", "sc_conv_w", "ssd_conv_w", "ssd_conv_b", "ssd_dt_bias", "ssd_a_log", "ssd_d", "ssd_norm", "w_out",
           "ffn_w_up", "ffn_conv_w", "ffn_conv_b", "ffn_w_down")
SHARDED = (("w_in", 2), ("mla_w_q_up", 2), ("mla_w_kv_up", 2), ("sc_conv_w", 2), ("ssd_conv_w", 2), ("w_out", 1),
           ("ffn_w_up", 2), ("ffn_conv_w", 2), ("ffn_w_down", 1))
SMALL = tuple(n for n in WEIGHTS if n not in dict(SHARDED))
N_CHIPS = 4
N_DEV = 8
ROW_ALIGN = 64
SLAB_ALIGN = 16
EARLY = (("w_in", "mla_w_q_up", "mla_w_kv_up", "sc_conv_w", "ssd_conv_w"),)
LATE = (("ffn_w_down", "w_out"), ("ffn_w_up", "ffn_conv_w"))
TRANSPOSED = ("w_in",)


def _is_rows(shape, width):
    return shape[-1] == width and math.prod(shape[:-1]) % SLAB_ALIGN == 0


def _is_short(shape, width):
    return len(shape) == 2 and shape[1] == width and not _is_rows(shape, width)


def _slab_rows(shape, width):
    if _is_rows(shape, width):
        return math.prod(shape[:-1])
    if _is_short(shape, width):
        return -(-shape[0] // SLAB_ALIGN) * SLAB_ALIGN
    return -(-math.prod(shape) // (width * SLAB_ALIGN)) * SLAB_ALIGN


def _slab(piece, width, dtype, lead=0):
    ld, shape = piece.shape[:lead], piece.shape[lead:]
    rows = _slab_rows(shape, width)
    if _is_rows(shape, width):
        return piece.astype(dtype).reshape(ld + (rows, width))
    if _is_short(shape, width):
        return jnp.pad(piece.astype(dtype), [(0, 0)] * lead + [(0, rows - shape[0]), (0, 0)])
    flat = piece.astype(dtype).reshape(ld + (-1,))
    return jnp.pad(flat, [(0, 0)] * lead + [(0, rows * width - flat.shape[-1])]).reshape(ld + (rows, width))


def _unslab(slab, shape, lead=0):
    ld = slab.shape[:lead]
    if _is_rows(shape, slab.shape[-1]):
        return slab.reshape(ld + tuple(shape))
    if _is_short(shape, slab.shape[-1]):
        return slab[..., :shape[0], :]
    return slab.reshape(ld + (-1,))[..., :math.prod(shape)].reshape(ld + tuple(shape))


def _layout(shapes, names, width):
    ents, off = [], 0
    for n in names:
        shp = tuple(shapes[n])
        todo = [(None, False, shp), (None, True, shp)] if n.endswith("conv_w") else [(l, False, shp[1:]) for l in range(shp[0])]
        for l, lo, ps in todo:
            r = _slab_rows(ps, width)
            ents.append((n, l, lo, ps, off, r))
            off += r
    return width, -(-off // ROW_ALIGN) * ROW_ALIGN, ents


def _pack(layout, piece, dtype, lead=0):
    width, rows, ents = layout
    slabs, ld = [], None
    for n, l, lo, ps, off, r in ents:
        p = piece(n, l, lo)
        slabs.append(None if p is None else _slab(p, width, dtype, lead))
        ld = ld if p is None else p.shape[:lead]
    used = ents[-1][4] + ents[-1][5]
    slabs = [jnp.zeros(ld + (e[5], width), dtype) if s is None else s for s, e in zip(slabs, ents)]
    if rows > used:
        slabs.append(jnp.zeros(ld + (rows - used, width), dtype))
    return jnp.concatenate(slabs, axis=lead)


ANY = pl.BlockSpec(memory_space=pl.ANY)


def _pos():
    return lax.axis_index("x"), lax.axis_index("y"), lax.axis_index("c")


def _other_chips(x, y):
    return ((1 - x, y), (x, 1 - y), (1 - x, 1 - y))


def _remote(src, dst, ssem, rsem, dev):
    return pltpu.make_async_remote_copy(src_ref=src, dst_ref=dst, send_sem=ssem, recv_sem=rsem, device_id=dev, device_id_type=MESH)


AG_CHUNKS = 2


def _chip_index():
    return 2 * lax.axis_index("x") + lax.axis_index("y")


def _ag_sems(nbuf):
    return [pltpu.SemaphoreType.DMA((nbuf * 3 * AG_CHUNKS,))] * 4


def _ag_plan(w_refs, out_refs, sems):
    isend, irecv, dsend, drecv = sems
    x, y, c = _pos()
    k = 2 * x + y
    sib = (x, y, 1 - c)
    sends, lands, forwards, finals = [], [], [], []
    s = 0
    for w_ref, out_ref in zip(w_refs, out_refs):
        H = w_ref.shape[0] // 2
        CH = H // AG_CHUNKS
        for cx, cy in _other_chips(x, y):
            for ch in range(AG_CHUNKS):
                mine = out_ref.at[k, pl.ds(c * H + ch * CH, CH), :]
                near = out_ref.at[2 * cx + cy, pl.ds(c * H + ch * CH, CH), :]
                far = out_ref.at[2 * cx + cy, pl.ds((1 - c) * H + ch * CH, CH), :]
                sends.append(_remote(w_ref.at[pl.ds(c * H + ch * CH, CH), :], mine, isend.at[s], irecv.at[s], (cx, cy, c)))
                lands.append(_remote(near, near, isend.at[s], irecv.at[s], (cx, cy, c)))
                forwards.append(_remote(near, near, dsend.at[s], drecv.at[s], sib))
                finals.append(_remote(far, far, dsend.at[s], drecv.at[s], sib))
                s += 1
    return sends, lands, forwards, finals


def _own_slot(got, own):
    return lax.dynamic_update_slice(got, own[None], (_chip_index(), 0, 0))


def _all_gather_weights(ws):
    nb = len(ws)

    def body(*refs):
        sends, lands, forwards, finals = _ag_plan(refs[:nb], refs[nb:2 * nb], refs[2 * nb:])
        for cp in sends:
            cp.start()
        for land, fw in zip(lands, forwards):
            land.wait_recv()
            fw.start()
        for cp in finals:
            cp.wait_recv()
        for cp in sends + forwards:
            cp.wait_send()

    got = pl.pallas_call(
        body, name="all_gather_weights", in_specs=[ANY] * nb, out_specs=[ANY] * nb,
        out_shape=[jax.ShapeDtypeStruct((N_CHIPS,) + w.shape, w.dtype) for w in ws], scratch_shapes=_ag_sems(nb),
    )(*ws)
    return [_own_slot(g, w) for g, w in zip(got, ws)]


def _rs_pair_exchange(gs):
    nb = len(gs)

    def body(*refs):
        g_refs, got_refs, (ssem, rsem) = refs[:nb], refs[nb:2 * nb], refs[2 * nb:]
        x, y, c = _pos()
        cps = []
        for b, (g_ref, got_ref) in enumerate(zip(g_refs, got_refs)):
            H = g_ref.shape[1] // 2
            for kk in range(N_CHIPS):
                s = b * N_CHIPS + kk
                cps.append(_remote(g_ref.at[kk, pl.ds((1 - c) * H, H), :], got_ref.at[kk], ssem.at[s], rsem.at[s], (x, y, 1 - c)))
        for cp in cps:
            cp.start()
        for cp in cps:
            cp.wait()

    return pl.pallas_call(
        body, name="rs_pair_exchange", in_specs=[ANY] * nb, out_specs=[ANY] * nb,
        out_shape=[jax.ShapeDtypeStruct((N_CHIPS, g.shape[1] // 2, g.shape[2]), g.dtype) for g in gs],
        scratch_shapes=[pltpu.SemaphoreType.DMA((nb * N_CHIPS,))] * 2,
    )(*gs)


def _chip_sems(nbuf):
    return [pltpu.SemaphoreType.DMA((nbuf * 3,))] * 2


def _chip_plan(p_refs, out_refs, sems):
    ssem, rsem = sems
    x, y, c = _pos()
    sends, lands = [], []
    s = 0
    for p_ref, out_ref in zip(p_refs, out_refs):
        for cx, cy in _other_chips(x, y):
            sends.append(_remote(p_ref.at[2 * cx + cy], out_ref.at[2 * x + y], ssem.at[s], rsem.at[s], (cx, cy, c)))
            land = out_ref.at[2 * cx + cy]
            lands.append(_remote(land, land, ssem.at[s], rsem.at[s], (cx, cy, c)))
            s += 1
    return sends, lands


def _chip_parts(got, ps):
    k = _chip_index()
    return [lax.dynamic_update_slice(g, lax.dynamic_slice_in_dim(p, k, 1, axis=0), (k, 0, 0)) for g, p in zip(got, ps)]


def _rs_chip_exchange(ps):
    nb = len(ps)

    def body(*refs):
        sends, lands = _chip_plan(refs[:nb], refs[nb:2 * nb], refs[2 * nb:])
        for cp in sends:
            cp.start()
        for cp in lands:
            cp.wait_recv()
        for cp in sends:
            cp.wait_send()

    got = pl.pallas_call(
        body, name="rs_chip_exchange", in_specs=[ANY] * nb, out_specs=[ANY] * nb,
        out_shape=[jax.ShapeDtypeStruct(p.shape, p.dtype) for p in ps], scratch_shapes=_chip_sems(nb),
    )(*ps)
    return _chip_parts(got, ps)


def _rs_pair_share(fs):
    nb = len(fs)

    def body(*refs):
        f_refs, out_refs, (ssem, rsem) = refs[:nb], refs[nb:2 * nb], refs[2 * nb:]
        x, y, c = _pos()
        sends, lands = [], []
        for b, (f_ref, out_ref) in enumerate(zip(f_refs, out_refs)):
            sends.append(_remote(f_ref, out_ref.at[c], ssem.at[b], rsem.at[b], (x, y, 1 - c)))
            land = out_ref.at[1 - c]
            lands.append(_remote(land, land, ssem.at[b], rsem.at[b], (x, y, 1 - c)))
        for cp in sends:
            cp.start()
        for cp in lands:
            cp.wait_recv()
        for cp in sends:
            cp.wait_send()

    got = pl.pallas_call(
        body, name="rs_pair_share", in_specs=[ANY] * nb, out_specs=[ANY] * nb,
        out_shape=[jax.ShapeDtypeStruct((2,) + f.shape, f.dtype) for f in fs],
        scratch_shapes=[pltpu.SemaphoreType.DMA((nb,))] * 2,
    )(*fs)
    return [lax.dynamic_update_slice(g, f[None], (lax.axis_index("c"), 0, 0)) for g, f in zip(got, fs)]


def _all_reduce_small(s):
    r, C = s.shape

    def body(s_ref, o_ref, buf, ssem, rsem):
        x, y, c = _pos()
        me = 4 * x + 2 * y + c
        buf[me] = s_ref[...]
        cps = []
        for m in range(1, N_DEV):
            mx, my, mc = (m >> 2) & 1, (m >> 1) & 1, m & 1
            peer = (x ^ mx, y ^ my, c ^ mc)
            cp = _remote(s_ref, buf.at[me], ssem.at[m - 1], rsem.at[m - 1], peer)
            cp.start()
            cps.append(cp)
        for m in range(1, N_DEV):
            mx, my, mc = (m >> 2) & 1, (m >> 1) & 1, m & 1
            src = 4 * (x ^ mx) + 2 * (y ^ my) + (c ^ mc)
            _remote(s_ref, buf.at[src], ssem.at[m - 1], rsem.at[m - 1], (x ^ mx, y ^ my, c ^ mc)).wait_recv()
        for cp in cps:
            cp.wait_send()
        acc = buf[0]
        for j in range(1, N_DEV):
            acc = acc + buf[j]
        o_ref[...] = acc

    return pl.pallas_call(
        body, name="all_reduce_small", in_specs=[pl.BlockSpec(memory_space=pltpu.VMEM)],
        out_specs=pl.BlockSpec(memory_space=pltpu.VMEM), out_shape=jax.ShapeDtypeStruct((r, C), F32),
        scratch_shapes=[pltpu.VMEM((N_DEV, r, C), F32), pltpu.SemaphoreType.DMA((N_DEV - 1,)), pltpu.SemaphoreType.DMA((N_DEV - 1,))],
    )(s)


def _rtile(n, pref):
    if n <= pref:
        return n
    t = (pref // 16) * 16
    while t >= 16:
        if n % t == 0:
            return t
        t -= 16
    raise ValueError(f"no row tile for {n}")


def _rs_pair_sums(gpks):
    gots = _rs_pair_exchange(gpks)
    out = []
    for gpk, got in zip(gpks, gots):
        _, R, C = gpk.shape
        H = R // 2
        own = lax.dynamic_index_in_dim(gpk.reshape(N_CHIPS, 2, H, C), lax.axis_index("c"), axis=1, keepdims=False)
        (part,) = _rows(lambda i, n, a, b: (a.astype(F32) + b.astype(F32),), N_CHIPS * H, _rtile(N_CHIPS * H, 512),
                        [_cur(own.reshape(N_CHIPS * H, C)), _cur(got.reshape(N_CHIPS * H, C))], [], [_out(C, BF16)], [], "rs_pair_add")
        out.append(part.reshape(N_CHIPS, H, C))
    return out


def _rs_chip_sums(parts):
    def add4(i, n, a, b, c, d):
        return (((a.astype(F32) + b.astype(F32)) + c.astype(F32)) + d.astype(F32),)

    out = []
    for p in parts:
        _, H, C = p.shape
        tm = _rtile(H, 1024)
        (red,) = _rows(add4, H, tm, [(p.reshape(N_CHIPS * H, C), C, functools.partial(_const, v=0), j * (H // tm)) for j in range(N_CHIPS)],
                       [], [_out(C, F32)], [], "rs_chip_add")
        out.append(red)
    return out


class _Exchange:
    def __init__(self, a):
        self.a = a
        self.axis = {n: (1 if n in TRANSPOSED else ax) for n, ax in SHARDED}
        shapes = {n: (1,) + tuple(self.packed(n, a[n]).shape[1:]) for n in self.axis}
        widths = lambda names: shapes[names[0]][-1] if names[0] == "ffn_w_up" else PACK_COLS
        self.layouts = {"early": [_layout(shapes, ns, widths(ns)) for ns in EARLY], "late": [_layout(shapes, ns, widths(ns)) for ns in LATE]}
        self.reduced = {}

    @staticmethod
    def packed(n, w):
        return jnp.swapaxes(w, -1, -2) if n in TRANSPOSED else w

    def shard(self, l, group):
        def piece(n, li, lo):
            w = self.packed(n, self.a[n][l:l + 1] if li is None else self.a[n][l])
            return w - w.astype(BF16).astype(F32) if lo else w
        return [_pack(lay, piece, BF16) for lay in self.layouts[group]]

    def weights(self, gathered, group):
        W, resid = {}, {}
        for (width, rows, ents), g in zip(self.layouts[group], gathered):
            for n, li, lo, ps, off, r in ents:
                parts = _unslab(g[:, off:off + r], ps, lead=1)
                ax = self.axis[n] + (1 if li is None else 0)
                full = jnp.moveaxis(parts, 0, ax - 1)
                full = full.reshape(full.shape[:ax - 1] + (-1,) + full.shape[ax + 1:])
                (resid if lo else W)[n] = full[0] if li is None else full
        for n in resid:
            W[n] = W[n].astype(F32) + resid[n].astype(F32)
        return W

    def submit(self, jobs):
        def by_chip(g, ax, parts=N_CHIPS):
            g = g.reshape(g.shape[:ax] + (parts, g.shape[ax] // parts) + g.shape[ax + 1:])
            return jnp.moveaxis(g, ax, 0)

        def pieces_of(GW):
            def piece(n, li, lo):
                if lo:
                    return None
                g = GW[n]
                if isinstance(g, tuple):
                    return jnp.concatenate([by_chip(h, self.axis[n] - 1, N_CHIPS // 2) for h in g])
                return by_chip(g[None], self.axis[n]) if li is None else by_chip(g, self.axis[n] - 1)
            return piece

        sums = _rs_pair_sums([_pack(lay, pieces_of(GW), BF16, lead=1) for GW, group in jobs for lay in self.layouts[group]])
        out, at = [], 0
        for _, group in jobs:
            out.append(sums[at:at + len(self.layouts[group])])
            at += len(self.layouts[group])
        return out

    def collect(self, l, group, parts):
        self.reduced[l, group] = _rs_chip_sums(parts)

    def finish(self):
        keys = [(l, g) for l in range(DEPTH) for g in self.layouts]
        flat = _rs_pair_share([f for key in keys for f in self.reduced[key]])
        both, at = {}, 0
        for key in keys:
            both[key] = flat[at:at + len(self.layouts[key[1]])]
            at += len(self.layouts[key[1]])
        grads = {}
        for group, lays in self.layouts.items():
            for b, (width, rows, ents) in enumerate(lays):
                for n, li, lo, ps, off, r in ents:
                    if not lo:
                        per_layer = [self.packed(n, _unslab(both[l, group][b].reshape(rows, width)[off:off + r], ps)) for l in range(DEPTH)]
                        grads[n] = jnp.concatenate(per_layer) if li is None else jnp.stack(per_layer)
        return grads


def _adam(w, g, m, v, name, g_row=0):
    shp = w.shape
    two = lambda a: a.reshape(-1, shp[-1])
    rows = math.prod(shp[:-1])
    tm = _rtile(rows, 256)
    assert g_row % tm == 0
    g_in = (two(g), shp[-1], functools.partial(_const, v=0), g_row // tm)
    res = _rows(_k_adam, rows, tm, [_cur(two(w)), g_in, _cur(two(m)), _cur(two(v))], [], [_out(shp[-1], F32)] * 4, [], name)
    return tuple(r.reshape(shp) for r in res)


def _pack_flat(parts, rows):
    flat = jnp.concatenate([p.astype(F32).reshape(-1) for p in parts])
    return jnp.pad(flat, (0, rows * PACK_COLS - flat.shape[0])).reshape(rows, PACK_COLS)


def _unpack_flat(buf, shapes):
    flat, out, off = buf.reshape(-1), [], 0
    for shp in shapes:
        n = math.prod(shp)
        out.append(flat[off:off + n].reshape(shp))
        off += n
    return out


def kernel(x, positions, norm_mix_pre, norm_mix_post, norm_ffn_pre, norm_ffn_post, w_in, mla_q_norm, mla_w_q_up, mla_kv_norm, mla_w_kv_up, sc_conv_w, ssd_conv_w, ssd_conv_b, ssd_dt_bias, ssd_a_log, ssd_d, ssd_norm, w_out, ffn_w_up, ffn_conv_w, ffn_conv_b, ffn_w_down, loss_target, m_norm_mix_pre, m_norm_mix_post, m_norm_ffn_pre, m_norm_ffn_post, m_w_in, m_mla_q_norm, m_mla_w_q_up, m_mla_kv_norm, m_mla_w_kv_up, m_sc_conv_w, m_ssd_conv_w, m_ssd_conv_b, m_ssd_dt_bias, m_ssd_a_log, m_ssd_d, m_ssd_norm, m_w_out, m_ffn_w_up, m_ffn_conv_w, m_ffn_conv_b, m_ffn_w_down, v_norm_mix_pre, v_norm_mix_post, v_norm_ffn_pre, v_norm_ffn_post, v_w_in, v_mla_q_norm, v_mla_w_q_up, v_mla_kv_norm, v_mla_w_kv_up, v_sc_conv_w, v_ssd_conv_w, v_ssd_conv_b, v_ssd_dt_bias, v_ssd_a_log, v_ssd_d, v_ssd_norm, v_w_out, v_ffn_w_up, v_ffn_conv_w, v_ffn_conv_b, v_ffn_w_down):
    a = dict(locals())
    ex = _Exchange(a)
    S = {n: a[n] for n in SMALL}
    loss_part, gx, _, GS = _local_step(a["x"][0], a["positions"][0], a["loss_target"][0], None, S, ex)

    grads, delta, new_m, new_v = {}, {}, {}, {}
    for n, g in ex.finish().items():
        grads[n], delta[n], new_m[n], new_v[n] = _adam(a[n], g, a["m_" + n], a["v_" + n], "adamw_" + n)

    small_shapes = [a[n].shape for n in SMALL]
    rs = -(-(sum(math.prod(s) for s in small_shapes) + 1) // (PACK_COLS * SLAB_ALIGN)) * SLAB_ALIGN
    red = _all_reduce_small(_pack_flat([GS[n] for n in SMALL] + [loss_part.reshape(1)], rs))
    loss = _unpack_flat(red, small_shapes + [(1,)])[-1][0]
    pk = lambda pre: _pack_flat([a[pre + n] for n in SMALL], rs)
    for dst, buf in zip((grads, delta, new_m, new_v), _adam(pk(""), red, pk("m_"), pk("v_"), "adamw_small")):
        dst.update(zip(SMALL, _unpack_flat(buf, small_shapes)))

    return (loss, gx[None], *[grads[n] for n in WEIGHTS], *[delta[n] for n in WEIGHTS], *[new_m[n] for n in WEIGHTS],
            *[new_v[n] for n in WEIGHTS])
```

```python
import functools
import math

import jax
import jax.numpy as jnp
from jax import lax
from jax.experimental import pallas as pl
from jax.experimental.pallas import tpu as pltpu

F32 = jnp.float32
BF16 = jnp.bfloat16
MXU_DTYPE = jnp.bfloat16
HIGHEST = lax.Precision.HIGHEST
MESH = pl.DeviceIdType.MESH

D_MODEL = 1024
DEPTH = 4
HEADS = 8
Q_LORA = 256
KV_LORA = 128
NOPE = 64
ROPE = 32
VDIM = 64
ROPE_THETA = 10000.0
SC_DIM = 256
SSD_HEADS = 4
SSD_HEAD_DIM = 64
SSD_STATE = 128
SSD_DIM = 256
SSD_CONV_DIM = 768
SSD_CHUNK = 128
FFN_DIM = 2816
NORM_EPS = 1e-6
QK_SCALE = (NOPE + ROPE) ** -0.5
LANE = 128
HP = 128
FLASH_HEADS = 8
FLASH_HEADS_FWD = 8
FLASH_BLOCK = 512

ZIN = 2560
Z_CQ, Z_CKV, Z_KR, Z_SCB, Z_SCC, Z_SCH, Z_SSZ, Z_XBC, Z_DT = 0, 256, 384, 512, 768, 1024, 1280, 1536, 2304
KR_LANE = 64
FFN_TILE = 256
FFN_ROWS = 2048
ROW_BLOCK = 512

ADAM_LR, ADAM_B1, ADAM_B2, ADAM_EPS, ADAM_WD, ADAM_STEP = 0.001, 0.9, 0.999, 1e-08, 0.01, 10

PACK_COLS = 1024


def _tile(n, pref):
    if n <= pref:
        return n
    t = (pref // LANE) * LANE
    while t >= LANE:
        if n % t == 0:
            return t
        t -= LANE
    raise ValueError(f"no tile for {n}")


MM_TM, MM_TN, MM_TK = 1024, 1408, 1536


def _mm(a, b, mode, out_dtype, name, tm=None, tn=MM_TN, tkmax=MM_TK):
    pair = isinstance(a, tuple)
    a_list = list(a) if pair else [a]
    layer = None
    if isinstance(b, tuple):
        b, layer = b
    bshape = b.shape[-2:]
    if mode == "nn":
        (M, Ka), (_, N) = a_list[0].shape, bshape
    elif mode == "nt":
        (M, Ka), (N, _) = a_list[0].shape, bshape
    else:
        (Ka, M), (_, N) = a_list[0].shape, bshape
    tk = _tile(Ka, tkmax)
    nka = Ka // tk
    nk = nka * len(a_list)
    if tm is None:
        tm = MM_TN if mode == "tn" else (2 * MM_TM if nk == 1 else MM_TM)
    tm, tn = _tile(M, tm), _tile(N, tn)

    def bspec(shape, index):
        if layer is None:
            return pl.BlockSpec(shape, index)
        return pl.BlockSpec((None,) + shape, lambda i, j, k: (layer,) + index(i, j, k))

    if mode == "nn":
        a_specs = [pl.BlockSpec((tm, tk), lambda i, j, k: (i, jnp.minimum(k, nka - 1))),
                   pl.BlockSpec((tm, tk), lambda i, j, k: (i, jnp.maximum(k - nka, 0)))][:len(a_list)]
        b_spec = bspec((tk, tn), lambda i, j, k: (k, j))
        dims = NN
    elif mode == "nt":
        a_specs = [pl.BlockSpec((tm, tk), lambda i, j, k: (i, jnp.minimum(k, nka - 1))),
                   pl.BlockSpec((tm, tk), lambda i, j, k: (i, jnp.maximum(k - nka, 0)))][:len(a_list)]
        b_spec = bspec((tn, tk), lambda i, j, k: (j, k))
        dims = NT
    else:
        a_specs = [pl.BlockSpec((tk, tm), lambda i, j, k: (k, i))]
        b_spec = pl.BlockSpec((tk, tn), lambda i, j, k: (k, j))
        dims = TN
    na = len(a_list)

    def body(*refs):
        a_refs, b_ref, o_ref = refs[:na], refs[na], refs[na + 1]
        k = pl.program_id(2)

        def prod(a_ref):
            return lax.dot_general(a_ref[...].astype(MXU_DTYPE), b_ref[...].astype(MXU_DTYPE), dims, preferred_element_type=F32)

        if nk == 1:
            o_ref[...] = prod(a_refs[0]).astype(o_ref.dtype)
            return
        acc_ref = refs[na + 2]

        @pl.when(k == 0)
        def _():
            acc_ref[...] = prod(a_refs[0])

        @pl.when((k > 0) & (k < nka))
        def _():
            acc_ref[...] += prod(a_refs[0])

        if pair:
            @pl.when(k >= nka)
            def _():
                acc_ref[...] += prod(a_refs[1])

        @pl.when(k == nk - 1)
        def _():
            o_ref[...] = acc_ref[...].astype(o_ref.dtype)

    return pl.pallas_call(
        body, name=name, grid=(M // tm, N // tn, nk),
        in_specs=a_specs + [b_spec], out_specs=pl.BlockSpec((tm, tn), lambda i, j, k: (i, j)),
        out_shape=jax.ShapeDtypeStruct((M, N), out_dtype),
        scratch_shapes=[pltpu.VMEM((tm, tn), F32)] if nk > 1 else [],
        compiler_params=pltpu.CompilerParams(dimension_semantics=("parallel", "parallel", "arbitrary")),
    )(*a_list, b)


HALO = 8


def _const(j, v):
    return v


def _rows(fn, T, tm, ins, consts, outs, accs, name, ncol=1):
    n = T // tm
    hb = tm // HALO
    last = T // HALO - 1
    in_specs, args = [], []
    for arr, bc, cb, kind in ins:
        if isinstance(kind, int):
            in_specs.append(pl.BlockSpec((tm, bc), lambda j, i, cb=cb, off=kind: (i + off, cb(j))))
        elif kind == "cur":
            in_specs.append(pl.BlockSpec((tm, bc), lambda j, i, cb=cb: (i, cb(j))))
        elif kind == "prev":
            in_specs.append(pl.BlockSpec((HALO, bc), lambda j, i, cb=cb: (jnp.maximum(i * hb - 1, 0), cb(j))))
        else:
            in_specs.append(pl.BlockSpec((HALO, bc), lambda j, i, cb=cb: (jnp.minimum((i + 1) * hb, last), cb(j))))
        args.append(arr)
    for arr, bc, cb in consts:
        in_specs.append(pl.BlockSpec((arr.shape[0], bc), lambda j, i, cb=cb: (0, cb(j))))
        args.append(arr)
    out_specs, out_shape = [], []
    for tc, dt, bc, cb in outs:
        out_specs.append(pl.BlockSpec((tm, bc), lambda j, i, cb=cb: (i, cb(j))))
        out_shape.append(jax.ShapeDtypeStruct((T, tc), dt))
    for r, tc, bc, cb in accs:
        out_specs.append(pl.BlockSpec((r, bc), lambda j, i, cb=cb: (0, cb(j))))
        out_shape.append(jax.ShapeDtypeStruct((r, tc), F32))
    nin, nout, nacc = len(args), len(outs), len(accs)

    def body(*refs):
        i = pl.program_id(1)
        res = fn(i, n, *[r[...] for r in refs[:nin]])
        for r, v in zip(refs[nin:nin + nout], res[:nout]):
            r[...] = v.astype(r.dtype)
        if nacc:
            acc_refs = refs[nin + nout:nin + nout + nacc]

            @pl.when(i == 0)
            def _():
                for r in acc_refs:
                    r[...] = jnp.zeros_like(r)

            for r, v in zip(acc_refs, res[nout:]):
                r[...] += v.astype(F32)

    res = pl.pallas_call(
        body, name=name, grid=(ncol, n), in_specs=in_specs, out_specs=out_specs, out_shape=out_shape,
        compiler_params=pltpu.CompilerParams(dimension_semantics=("arbitrary", "arbitrary")),
    )(*args)
    return res


def _cur(arr, bc=None, blk=0):
    bc = arr.shape[1] if bc is None else bc
    return (arr, bc, functools.partial(_const, v=blk), "cur")


def _halo(arr, kind, bc=None, blk=0):
    bc = arr.shape[1] if bc is None else bc
    return (arr, bc, functools.partial(_const, v=blk), kind)


def _cst(arr):
    return (arr, arr.shape[1], functools.partial(_const, v=0))


def _out(cols, dt):
    return (cols, dt, cols, functools.partial(_const, v=0))


def _acc(rows, cols):
    return (rows, cols, cols, functools.partial(_const, v=0))


def _rms(x, w):
    return x * lax.rsqrt(jnp.mean(x * x, axis=-1, keepdims=True) + NORM_EPS) * w


def _sigmoid(x):
    return 0.5 * jnp.tanh(0.5 * x) + 0.5


def _silu(x):
    return x * _sigmoid(x)


def _dsilu(x):
    s = _sigmoid(x)
    return s * (1.0 + x * (1.0 - s))


def _softplus(x):
    return jnp.maximum(x, 0.0) + jnp.log1p(jnp.exp(-jnp.abs(x)))


def _shift(a, k):
    return pltpu.roll(a, k % a.shape[0], 0)


def _lroll(a, k):
    return pltpu.roll(a, k % a.shape[1], 1)


def _vjp_wrap(f, nrow, nconst, add_first=False):
    def g(i, n, *vals):
        rows, consts, mid = vals[:nrow], vals[len(vals) - nconst:], vals[nrow:len(vals) - nconst]
        cots = mid[:-1] if add_first else mid
        outs, pull = jax.vjp(f, *rows, *consts)
        grads = list(pull(tuple(c.astype(o.dtype) for c, o in zip(cots, outs))))
        if add_first:
            grads[0] = grads[0] + mid[-1]
        return tuple(grads)
    return g


def _rows_vjp(f, T, tm, rows, consts, cots, out_dtypes, name):
    return _rows(_vjp_wrap(f, len(rows), len(consts)), T, tm, [_cur(r) for r in rows] + [_cur(c) for c in cots],
                 [_cst(c) for c in consts], [_out(r.shape[1], dt) for r, dt in zip(rows, out_dtypes)],
                 [_acc(1, c.shape[1]) for c in consts], name)


def _f_premix(x, g):
    return (_rms(x, g),)


def _f_mla_pre(cq, ckv, qn, kvn):
    return _rms(cq, qn), _rms(ckv, kvn)


def _f_ssd_gate(y, z, nw):
    return (_rms(y * _silu(z), nw),)


def _f_post_mix(x, mixed, gpost, gffn):
    x1 = x + _rms(mixed, gpost)
    return x1, _rms(x1, gffn)


def _f_post_ffn(x1, d, gpost):
    return (x1 + _rms(d, gpost),)


def _rope_fwd(v, cosf, sina, sinb):
    return v * cosf + _lroll(v, -16) * sina + _lroll(v, 16) * sinb


def _rope_bwd(g, cosf, sina, sinb):
    return g * cosf + _lroll(g * sina, 16) + _lroll(g * sinb, -16)


def _k_rope_fwd(i, n, qpad, kvpad, kr, cosf, sina, sinb):
    qs, ks = [], []
    krr = _rope_fwd(kr, cosf, sina, sinb)
    for h in range(HEADS):
        sl = slice(h * HP, (h + 1) * HP)
        qs.append(_rope_fwd(qpad[:, sl], cosf, sina, sinb))
        ks.append(kvpad[:, sl].astype(F32) + krr)
    return jnp.concatenate(qs, axis=1), jnp.concatenate(ks, axis=1)


def _k_rope_bwd(i, n, dq, dk, dv, cosf, sina, sinb):
    lane = lax.broadcasted_iota(jnp.int32, (1, HP), 1)
    rmask = ((lane >= KR_LANE) & (lane < KR_LANE + ROPE)).astype(F32)
    dqs, dks = [], []
    dkr = jnp.zeros((dq.shape[0], HP), F32)
    for h in range(HEADS):
        sl = slice(h * HP, (h + 1) * HP)
        dqs.append(_rope_bwd(dq[:, sl], cosf, sina, sinb))
        dkh = dk[:, sl]
        dkr = dkr + dkh * rmask
        dks.append(dkh * (1.0 - rmask))
    dkr = _rope_bwd(dkr, cosf, sina, sinb) * rmask
    return jnp.concatenate(dqs, axis=1), jnp.concatenate(dks + [dv], axis=1), dkr


def _k_sconv_fwd(i, n, b, c, h, cp, hp, w):
    m = b.shape[0]
    up = jnp.where(i > 0, cp * hp, 0.0)
    ue = jnp.concatenate([up, c * h], axis=0)
    conv = w[2:3] * ue + w[1:2] * _shift(ue, 1) + w[0:1] * _shift(ue, 2)
    return (b * conv[HALO:],)


def _k_sconv_bwd(i, n, b, c, h, dy, cp, hp, bn, dyn, w):
    m = b.shape[0]
    up = jnp.where(i > 0, cp * hp, 0.0)
    ue = jnp.concatenate([up, c * h], axis=0)
    u1, u2 = _shift(ue, 1), _shift(ue, 2)
    conv = (w[2:3] * ue + w[1:2] * u1 + w[0:1] * u2)[HALO:]
    dc_cur = dy * b
    dce = jnp.concatenate([dc_cur, jnp.where(i < n - 1, dyn * bn, 0.0)], axis=0)
    du = (w[2:3] * dce + w[1:2] * _shift(dce, -1) + w[0:1] * _shift(dce, -2))[:m]
    dw = jnp.concatenate([
        jnp.sum(dc_cur * u2[HALO:], axis=0, keepdims=True),
        jnp.sum(dc_cur * u1[HALO:], axis=0, keepdims=True),
        jnp.sum(dc_cur * ue[HALO:], axis=0, keepdims=True),
        jnp.zeros((HALO - 3, b.shape[1]), F32)], axis=0)
    return dy * conv, du * h, du * c, dw


def _conv4(ue, w):
    return w[3:4] * ue + w[2:3] * _shift(ue, 1) + w[1:2] * _shift(ue, 2) + w[0:1] * _shift(ue, 3)


def _k_ssdconv_fwd(i, n, u, up, w, bias):
    ue = jnp.concatenate([jnp.where(i > 0, up, 0.0), u], axis=0)
    return (_silu(_conv4(ue, w)[HALO:] + bias),)


def _k_ssdconv_bwd(i, n, u, dout, up, un, doutn, w, bias):
    m = u.shape[0]
    ue = jnp.concatenate([jnp.where(i > 0, up, 0.0), u, un], axis=0)
    u1, u2, u3 = _shift(ue, 1), _shift(ue, 2), _shift(ue, 3)
    pre = (w[3:4] * ue + w[2:3] * u1 + w[1:2] * u2 + w[0:1] * u3)[HALO:] + bias
    doe = jnp.concatenate([dout, jnp.where(i < n - 1, doutn, 0.0)], axis=0)
    dpre = doe * _dsilu(pre)
    du = (w[3:4] * dpre + w[2:3] * _shift(dpre, -1) + w[1:2] * _shift(dpre, -2) + w[0:1] * _shift(dpre, -3))[:m]
    dp = dpre[:m]
    cur = slice(HALO, HALO + m)
    dw = jnp.concatenate([
        jnp.sum(dp * u3[cur], axis=0, keepdims=True),
        jnp.sum(dp * u2[cur], axis=0, keepdims=True),
        jnp.sum(dp * u1[cur], axis=0, keepdims=True),
        jnp.sum(dp * ue[cur], axis=0, keepdims=True),
        jnp.zeros((HALO - 4, u.shape[1]), F32)], axis=0)
    db = jnp.sum(dp, axis=0, keepdims=True)
    return du, dw, db


def _conv3(ue, w):
    return w[2:3] * ue + w[1:2] * _shift(ue, 1) + w[0:1] * _shift(ue, 2)


def _k_ffnact_fwd(i, n, ug, uu, ugp, uup, wg, wu, bg, bu):
    gate = _conv3(jnp.concatenate([jnp.where(i > 0, ugp, 0.0), ug], axis=0), wg)[HALO:] + bg
    upv = _conv3(jnp.concatenate([jnp.where(i > 0, uup, 0.0), uu], axis=0), wu)[HALO:] + bu
    return (_silu(gate) * upv,)


def _k_ffnact_bwd(i, n, ug, uu, dact, ugp, uup, ugn, uun, dactn, wg, wu, bg, bu):
    m = ug.shape[0]
    cur = slice(HALO, HALO + m)

    def taps(p, c, nx):
        e = jnp.concatenate([jnp.where(i > 0, p, 0.0), c, nx], axis=0)
        return e, _shift(e, 1), _shift(e, 2)

    def back(d, w):
        return (w[2:3] * d + w[1:2] * _shift(d, -1) + w[0:1] * _shift(d, -2))[:m]

    def wgrad(d, t):
        return jnp.concatenate([jnp.sum(d[:m] * t[2][cur], axis=0, keepdims=True), jnp.sum(d[:m] * t[1][cur], axis=0, keepdims=True),
                                jnp.sum(d[:m] * t[0][cur], axis=0, keepdims=True), jnp.zeros((HALO - 3, d.shape[1]), F32)], axis=0)

    tg, tu = taps(ugp, ug, ugn), taps(uup, uu, uun)
    gate = (wg[2:3] * tg[0] + wg[1:2] * tg[1] + wg[0:1] * tg[2])[HALO:] + bg
    upv = (wu[2:3] * tu[0] + wu[1:2] * tu[1] + wu[0:1] * tu[2])[HALO:] + bu
    dae = jnp.concatenate([dact, jnp.where(i < n - 1, dactn, 0.0)], axis=0)
    sg = _sigmoid(gate)
    dg = dae * upv * (sg * (1.0 + gate * (1.0 - sg)))
    dup = dae * (gate * sg)
    return (back(dg, wg), back(dup, wu), wgrad(dg, tg), wgrad(dup, tu),
            jnp.sum(dg[:m], axis=0, keepdims=True), jnp.sum(dup[:m], axis=0, keepdims=True))


def _k_loss(i, n, y, tgt):
    e = y - tgt
    part = 0.5 * jnp.sum(jnp.sum(e * e, axis=1, keepdims=True) / D_MODEL, axis=0, keepdims=True)
    return e * (1.0 / D_MODEL), jnp.broadcast_to(part, (1, LANE))


def _k_adam(i, n, w, g, m, v):
    m = ADAM_B1 * m + (1.0 - ADAM_B1) * g
    v = ADAM_B2 * v + (1.0 - ADAM_B2) * (g * g)
    m_hat = m / (1.0 - ADAM_B1 ** ADAM_STEP)
    v_hat = v / (1.0 - ADAM_B2 ** ADAM_STEP)
    delta = -ADAM_LR * (m_hat / (jnp.sqrt(v_hat) + ADAM_EPS) + ADAM_WD * w)
    return g, delta, m, v


def _dotf(a, b, dims):
    return lax.dot_general(a.astype(MXU_DTYPE), b.astype(MXU_DTYPE), dims, preferred_element_type=F32)


NN = (((1,), (0,)), ((), ()))
NT = (((1,), (1,)), ((), ()))
TN = (((0,), (0,)), ((), ()))


def _ssd_chunk(x0, x1, x2, x3, b0, b1, c0, c1, dtraw, p0, p1, p2, p3, dtb, alog, dsk):
    xs, bs, cs_, ps = (x0, x1, x2, x3), (b0, b1), (c0, c1), (p0, p1, p2, p3)
    L = dtraw.shape[0]
    dt = _softplus(dtraw + dtb)
    adt = dt * (-jnp.exp(alog))
    row = lax.broadcasted_iota(jnp.int32, (L, L), 0)
    col = lax.broadcasted_iota(jnp.int32, (L, L), 1)
    tril = row >= col
    cum = jnp.dot(tril.astype(F32), adt, precision=HIGHEST, preferred_element_type=F32)
    cum_t = cum.T
    lane = lax.broadcasted_iota(jnp.int32, (1, LANE), 1)
    sub = lax.broadcasted_iota(jnp.int32, (LANE, 1), 0)
    lastcol = (lax.broadcasted_iota(jnp.int32, (1, L), 1) == L - 1).astype(F32)
    ys, news = [], []
    for h in range(SSD_HEADS):
        g = h // (SSD_HEADS // 2)
        oh = (lane == h).astype(F32)
        dth = jnp.sum(dt * oh, axis=1, keepdims=True)
        csh = jnp.sum(cum * oh, axis=1, keepdims=True)
        csr = jnp.sum(cum_t * (sub == h).astype(F32), axis=0, keepdims=True)
        cl = jnp.sum(csr * lastcol, axis=1, keepdims=True)
        dskh = jnp.sum(dsk * oh, axis=1, keepdims=True)
        x, bm, cm, prev = xs[h], bs[g], cs_[g], ps[h]
        xdt = x * dth
        decay = jnp.exp(jnp.where(tril, csh - csr, -jnp.inf))
        scores = _dotf(cm, bm, NT) * decay
        y_diag = _dotf(scores, xdt, NN)
        bd = bm * jnp.exp(cl - csh)
        cst = _dotf(xdt, bd, TN)
        news.append(prev * jnp.exp(cl) + cst)
        y_off = _dotf(cm, prev, NT) * jnp.exp(csh)
        ys.append(y_diag + y_off + x * dskh)
    return (*ys, *news)


SSD_STEP = 2


def _ssd_operands(x_ref, dt_ref, par_ref, prev, rows):
    xs = [x_ref[rows, h * SSD_HEAD_DIM:(h + 1) * SSD_HEAD_DIM] for h in range(SSD_HEADS)]
    bs = [x_ref[rows, SSD_DIM + g * SSD_STATE:SSD_DIM + (g + 1) * SSD_STATE] for g in range(2)]
    cs_ = [x_ref[rows, SSD_DIM + 2 * SSD_STATE + g * SSD_STATE:SSD_DIM + 2 * SSD_STATE + (g + 1) * SSD_STATE] for g in range(2)]
    return (*xs, *bs, *cs_, dt_ref[rows, :], *prev, par_ref[0:1, :], par_ref[1:2, :], par_ref[2:3, :])


def _ssd_fwd(xbc, dtraw, par, T, dt_blk=0):
    L = SSD_CHUNK
    nc = T // L
    P = SSD_HEAD_DIM
    U = SSD_STEP if nc % SSD_STEP == 0 else 1

    def body(x_ref, dt_ref, par_ref, y_ref, st_ref, state):
        @pl.when(pl.program_id(0) == 0)
        def _():
            state[...] = jnp.zeros_like(state)

        for u in range(U):
            rows = slice(u * L, (u + 1) * L)
            st_ref[u] = state[...]
            prev = [state[h * P:(h + 1) * P, :] for h in range(SSD_HEADS)]
            res = _ssd_chunk(*_ssd_operands(x_ref, dt_ref, par_ref, prev, rows))
            for h in range(SSD_HEADS):
                y_ref[rows, h * P:(h + 1) * P] = res[h]
                state[h * P:(h + 1) * P, :] = res[SSD_HEADS + h]

    return pl.pallas_call(
        body, name="ssd_scan_fwd", grid=(nc // U,),
        in_specs=[pl.BlockSpec((U * L, SSD_CONV_DIM), lambda c: (c, 0)), pl.BlockSpec((U * L, LANE), lambda c: (c, dt_blk)),
                  pl.BlockSpec((8, LANE), lambda c: (0, 0))],
        out_specs=[pl.BlockSpec((U * L, SSD_DIM), lambda c: (c, 0)), pl.BlockSpec((U, SSD_DIM, SSD_STATE), lambda c: (c, 0, 0))],
        out_shape=[jax.ShapeDtypeStruct((T, SSD_DIM), F32), jax.ShapeDtypeStruct((nc, SSD_DIM, SSD_STATE), F32)],
        scratch_shapes=[pltpu.VMEM((SSD_DIM, SSD_STATE), F32)],
        compiler_params=pltpu.CompilerParams(dimension_semantics=("arbitrary",)),
    )(xbc, dtraw, par)


def _ssd_bwd(xbc, dtraw, par, states, dy, T, dt_blk=0):
    L = SSD_CHUNK
    nc = T // L
    P = SSD_HEAD_DIM
    U = SSD_STEP if nc % SSD_STEP == 0 else 1
    ns = nc // U

    def body(x_ref, dt_ref, par_ref, st_ref, dy_ref, dx_ref, ddt_ref, dpar_ref, dstate):
        @pl.when(pl.program_id(0) == 0)
        def _():
            dstate[...] = jnp.zeros_like(dstate)
            dpar_ref[...] = jnp.zeros_like(dpar_ref)

        for u in reversed(range(U)):
            rows = slice(u * L, (u + 1) * L)
            prev = [st_ref[u, h * P:(h + 1) * P, :] for h in range(SSD_HEADS)]
            prim = _ssd_operands(x_ref, dt_ref, par_ref, prev, rows)
            _, pull = jax.vjp(_ssd_chunk, *prim)
            cots = tuple(dy_ref[rows, h * P:(h + 1) * P] for h in range(SSD_HEADS)) + tuple(
                dstate[h * P:(h + 1) * P, :] for h in range(SSD_HEADS))
            g = pull(cots)
            for h in range(SSD_HEADS):
                dx_ref[rows, h * P:(h + 1) * P] = g[h]
                dstate[h * P:(h + 1) * P, :] = g[9 + h]
            for k in range(2):
                dx_ref[rows, SSD_DIM + k * SSD_STATE:SSD_DIM + (k + 1) * SSD_STATE] = g[4 + k]
                dx_ref[rows, SSD_DIM + 2 * SSD_STATE + k * SSD_STATE:SSD_DIM + 2 * SSD_STATE + (k + 1) * SSD_STATE] = g[6 + k]
            ddt_ref[rows, :] = g[8]
            for r in range(3):
                dpar_ref[r:r + 1, :] += g[13 + r]

    rev = lambda c: (ns - 1 - c, 0)
    return pl.pallas_call(
        body, name="ssd_scan_bwd", grid=(ns,),
        in_specs=[pl.BlockSpec((U * L, SSD_CONV_DIM), rev), pl.BlockSpec((U * L, LANE), lambda c: (ns - 1 - c, dt_blk)),
                  pl.BlockSpec((8, LANE), lambda c: (0, 0)),
                  pl.BlockSpec((U, SSD_DIM, SSD_STATE), lambda c: (ns - 1 - c, 0, 0)), pl.BlockSpec((U * L, SSD_DIM), rev)],
        out_specs=[pl.BlockSpec((U * L, SSD_CONV_DIM), rev), pl.BlockSpec((U * L, LANE), rev), pl.BlockSpec((8, LANE), lambda c: (0, 0))],
        out_shape=[jax.ShapeDtypeStruct((T, SSD_CONV_DIM), F32), jax.ShapeDtypeStruct((T, LANE), F32),
                   jax.ShapeDtypeStruct((8, LANE), F32)],
        scratch_shapes=[pltpu.VMEM((SSD_DIM, SSD_STATE), F32)],
        compiler_params=pltpu.CompilerParams(dimension_semantics=("arbitrary",)),
    )(xbc, dtraw, par, states, dy)


def _causal_pairs(nq, by_query):
    if by_query:
        pairs = [(i, j) for i in range(nq) for j in range(i + 1)]
    else:
        pairs = [(i, j) for j in range(nq) for i in range(j, nq)]
    return jnp.asarray([p[0] for p in pairs], jnp.int32), jnp.asarray([p[1] for p in pairs], jnp.int32)


def _flash_fwd(q, k, kv, T, carry=()):
    tq = tk = min(FLASH_BLOCK, T)
    nq = T // tq
    G = FLASH_HEADS_FWD
    rep = tk // HP
    nc = len(carry)
    qi, kj = _causal_pairs(nq, by_query=True)
    nh, nt = HEADS // G, qi.shape[0]

    def body(qi_ref, kj_ref, q_ref, k_ref, v_ref, *rest):
        w_refs, o_ref, g_refs = rest[:nc], rest[nc], rest[nc + 1:2 * nc + 1]
        m_ref, l_ref, acc_ref = rest[2 * nc + 1:2 * nc + 4]
        h, t = pl.program_id(0), pl.program_id(1)
        i, j = qi_ref[t], kj_ref[t]
        if nc:
            plan = lambda: _ag_plan(w_refs, g_refs, rest[2 * nc + 4:])

            @pl.when((h == 0) & (t == 0))
            def _():
                for cp in plan()[0]:
                    cp.start()

        @pl.when(j == 0)
        def _():
            m_ref[...] = jnp.full_like(m_ref, -jnp.inf)
            l_ref[...] = jnp.zeros_like(l_ref)
            acc_ref[...] = jnp.zeros_like(acc_ref)

        def step(diagonal):
            for g in range(G):
                sl = slice(g * HP, (g + 1) * HP)
                s = _dotf(q_ref[:, sl], k_ref[:, sl], NT) * QK_SCALE
                if diagonal:
                    rows = lax.broadcasted_iota(jnp.int32, (tq, tk), 0)
                    cols = lax.broadcasted_iota(jnp.int32, (tq, tk), 1)
                    s = jnp.where(rows >= cols, s, -jnp.inf)
                m_old = m_ref[:, sl]
                m_new = jnp.maximum(m_old, jnp.max(s, axis=1, keepdims=True))
                p = jnp.exp(s - jnp.tile(m_new, (1, rep)))
                alpha = jnp.exp(m_old - m_new)
                l_ref[:, sl] = alpha * l_ref[:, sl] + jnp.sum(p, axis=1, keepdims=True)
                acc_ref[:, sl] = alpha * acc_ref[:, sl] + _dotf(p, v_ref[:, sl], NN)
                m_ref[:, sl] = m_new

        @pl.when(j < i)
        def _():
            step(False)

        @pl.when(j == i)
        def _():
            step(True)
            lane = lax.broadcasted_iota(jnp.int32, (tq, HP), 1)
            for g in range(G):
                sl = slice(g * HP, (g + 1) * HP)
                l = l_ref[:, sl]
                o_ref[:, sl] = jnp.where(lane < VDIM, acc_ref[:, sl] / l, m_ref[:, sl] + jnp.log(l))

        if nc:
            @pl.when(h * nt + t == (3 * nh * nt) // 4)
            def _():
                _, lands, forwards, _ = plan()
                for land, fw in zip(lands, forwards):
                    land.wait_recv()
                    fw.start()

            @pl.when((h == nh - 1) & (t == nt - 1))
            def _():
                sends, _, forwards, finals = plan()
                for cp in finals:
                    cp.wait_recv()
                for cp in sends + forwards:
                    cp.wait_send()

    W = G * HP
    res = pl.pallas_call(
        body, name="mla_flash_fwd",
        grid_spec=pltpu.PrefetchScalarGridSpec(
            num_scalar_prefetch=2, grid=(nh, nt),
            in_specs=[pl.BlockSpec((tq, W), lambda h, t, qi, kj: (qi[t], h)),
                      pl.BlockSpec((tk, W), lambda h, t, qi, kj: (kj[t], h)),
                      pl.BlockSpec((tk, W), lambda h, t, qi, kj: (kj[t], HEADS // G + h))] + [ANY] * nc,
            out_specs=[pl.BlockSpec((tq, W), lambda h, t, qi, kj: (qi[t], h))] + [ANY] * nc,
            scratch_shapes=[pltpu.VMEM((tq, W), F32), pltpu.VMEM((tq, W), F32), pltpu.VMEM((tq, W), F32)] + (_ag_sems(nc) if nc else [])),
        out_shape=[jax.ShapeDtypeStruct((T, HEADS * HP), F32)] + [jax.ShapeDtypeStruct((N_CHIPS,) + w.shape, w.dtype) for w in carry],
        compiler_params=pltpu.CompilerParams(dimension_semantics=("arbitrary", "arbitrary")),
    )(qi, kj, q, k, kv, *carry)
    return res[0] if not nc else (res[0], [_own_slot(g, w) for g, w in zip(res[1:], carry)])


def _flash_bwd(q, k, kv, o, dycat, T, carry=()):
    tq = tk = min(FLASH_BLOCK, T)
    nq = T // tq
    G = FLASH_HEADS
    nc = len(carry)
    qi, kj = _causal_pairs(nq, by_query=False)
    nh, nt = HEADS // G, qi.shape[0]

    W = G * HP

    def body(qi_ref, kj_ref, q_ref, k_ref, v_ref, o_ref, do_ref, *rest):
        p_refs, (dq_out, dk_ref, dv_ref), part_refs = rest[:nc], rest[nc:nc + 3], rest[nc + 3:2 * nc + 3]
        dq_ref, dq_sem = rest[2 * nc + 3:2 * nc + 5]
        h, t = pl.program_id(0), pl.program_id(1)
        i, j = qi_ref[t], kj_ref[t]
        if nc:
            plan = lambda: _chip_plan(p_refs, part_refs, rest[2 * nc + 5:])

            @pl.when((h == 0) & (t == 0))
            def _():
                for cp in plan()[0]:
                    cp.start()

        @pl.when(t == 0)
        def _():
            dq_ref[...] = jnp.zeros_like(dq_ref)

        @pl.when(i == j)
        def _():
            dk_ref[...] = jnp.zeros_like(dk_ref)
            dv_ref[...] = jnp.zeros_like(dv_ref)

        def step(diagonal):
            r0 = pl.multiple_of(i * tq, tq)
            for g in range(G):
                sl = slice(g * HP, (g + 1) * HP)
                qv, kv, vv, ov, dov = q_ref[:, sl], k_ref[:, sl], v_ref[:, sl], o_ref[:, sl], do_ref[:, sl]
                s = _dotf(qv, kv, NT) * QK_SCALE
                p = jnp.exp(s - ov[:, VDIM:VDIM + 1])
                if diagonal:
                    rows = lax.broadcasted_iota(jnp.int32, (tq, tk), 0)
                    cols = lax.broadcasted_iota(jnp.int32, (tq, tk), 1)
                    p = jnp.where(rows >= cols, p, 0.0)
                dsum = jnp.sum(dov * ov, axis=1, keepdims=True)
                dv_ref[:, sl] += _dotf(p, dov, TN)
                dp = _dotf(dov, vv, NT)
                ds = p * (dp - dsum) * QK_SCALE
                dk_ref[:, sl] += _dotf(ds, qv, TN)
                dq_ref[pl.ds(r0, tq), sl] += _dotf(ds, kv, NN)

        @pl.when(i > j)
        def _():
            step(False)

        @pl.when(i == j)
        def _():
            step(True)

        @pl.when(t == nt - 1)
        def _():
            out = pltpu.make_async_copy(dq_ref, dq_out.at[:, pl.ds(pl.multiple_of(h * W, W), W)], dq_sem)
            out.start()
            out.wait()

        if nc:
            @pl.when((h == nh - 1) & (t == nt - 1))
            def _():
                sends, lands = plan()
                for cp in lands:
                    cp.wait_recv()
                for cp in sends:
                    cp.wait_send()

    qmap = lambda h, t, qi, kj: (qi[t], h)
    kmap = lambda h, t, qi, kj: (kj[t], h)
    vmap = lambda h, t, qi, kj: (kj[t], HEADS // G + h)
    res = pl.pallas_call(
        body, name="mla_flash_bwd",
        grid_spec=pltpu.PrefetchScalarGridSpec(
            num_scalar_prefetch=2, grid=(nh, nt),
            in_specs=[pl.BlockSpec((tq, W), qmap), pl.BlockSpec((tk, W), kmap), pl.BlockSpec((tk, W), vmap),
                      pl.BlockSpec((tq, W), qmap), pl.BlockSpec((tq, W), qmap)] + [ANY] * nc,
            out_specs=[ANY, pl.BlockSpec((tk, W), kmap), pl.BlockSpec((tk, W), kmap)] + [ANY] * nc,
            scratch_shapes=[pltpu.VMEM((T, W), F32), pltpu.SemaphoreType.DMA] + (_chip_sems(nc) if nc else [])),
        out_shape=[jax.ShapeDtypeStruct((T, HEADS * HP), F32)] * 3 + [jax.ShapeDtypeStruct(p.shape, p.dtype) for p in carry],
        compiler_params=pltpu.CompilerParams(dimension_semantics=("arbitrary", "arbitrary")),
    )(qi, kj, q, k, kv, o, dycat, *carry)
    return tuple(res[:3]) if not nc else (*res[:3], _chip_parts(res[3:], carry))


_IN_SRC = (0, 256, 384, 416, 672, 928, 1184, 1440, 2208, 2212)
_IN_DST = (Z_CQ, Z_CKV, Z_KR + KR_LANE, Z_SCB, Z_SCC, Z_SCH, Z_SSZ, Z_XBC, Z_DT)


def _pad_rows_in(w):
    ax = w.ndim - 2

    def zeros(n):
        return jnp.zeros(w.shape[:ax] + (n,) + w.shape[ax + 1:], w.dtype)

    def whole_tiles(p):
        n = p.shape[ax]
        return p if n % SLAB_ALIGN == 0 else jnp.pad(p, [(0, 0)] * ax + [(0, -n % SLAB_ALIGN), (0, 0)])

    parts, at = [], 0
    for s0, s1, d0 in zip(_IN_SRC[:-1], _IN_SRC[1:], _IN_DST):
        if d0 > at:
            parts.append(zeros(d0 - at))
        parts.append(whole_tiles(lax.slice_in_dim(w, s0, s1, axis=ax)))
        at = d0 + parts[-1].shape[ax]
    parts.append(zeros(ZIN - at))
    return jnp.concatenate(parts, axis=ax)


def _unpad_rows_in(w):
    ax = w.ndim - 2
    groups = list(zip(_IN_SRC[:-1], _IN_SRC[1:], _IN_DST))
    parts = [lax.slice_in_dim(w, d0, d0 + -(-(s1 - s0) // SLAB_ALIGN) * SLAB_ALIGN, axis=ax) for s0, s1, d0 in groups]
    return lax.slice_in_dim(jnp.concatenate(parts, axis=ax), 0, _IN_SRC[-1], axis=ax)


def _pad_heads(w, width):
    w = w.reshape(w.shape[:-1] + (HEADS, width))
    w = jnp.pad(w, [(0, 0)] * (w.ndim - 1) + [(0, HP - width)])
    return w.reshape(w.shape[:-2] + (HEADS * HP,))


def _unpad_heads(w, width):
    w = w.reshape(w.shape[:-1] + (HEADS, HP))[..., :width]
    return w.reshape(w.shape[:-2] + (HEADS * width,))


def _pad_kv(w):
    w = w.reshape(w.shape[:-1] + (HEADS, NOPE + VDIM))
    return jnp.concatenate([_pad_heads(w[..., :NOPE].reshape(w.shape[:-2] + (HEADS * NOPE,)), NOPE),
                            _pad_heads(w[..., NOPE:].reshape(w.shape[:-2] + (HEADS * VDIM,)), VDIM)], axis=-1)


def _unpad_kv(w):
    k = _unpad_heads(w[..., :HEADS * HP], NOPE).reshape(w.shape[:-1] + (HEADS, NOPE))
    v = _unpad_heads(w[..., HEADS * HP:], VDIM).reshape(w.shape[:-1] + (HEADS, VDIM))
    return jnp.concatenate([k, v], axis=-1).reshape(w.shape[:-1] + (HEADS * (NOPE + VDIM),))


def _pad_out_rows(w):
    lead, d = w.shape[:-2], w.shape[-1]
    att = w[..., :HEADS * VDIM, :].reshape(lead + (HEADS, VDIM, d))
    att = jnp.pad(att, [(0, 0)] * (att.ndim - 2) + [(0, HP - VDIM), (0, 0)]).reshape(lead + (HEADS * HP, d))
    return jnp.concatenate([att, w[..., HEADS * VDIM:, :]], axis=-2)


def _unpad_out_rows(w):
    lead, d = w.shape[:-2], w.shape[-1]
    att = w[..., :HEADS * HP, :].reshape(lead + (HEADS, HP, d))[..., :VDIM, :].reshape(lead + (HEADS * VDIM, d))
    return jnp.concatenate([att, w[..., HEADS * HP:, :]], axis=-2)


def _rows8(w):
    return jnp.pad(w.astype(F32), [(0, 0)] * (w.ndim - 2) + [(0, 8 - w.shape[-2]), (0, 0)])


def _row8(*vecs):
    c = vecs[0].shape[-1]
    return jnp.concatenate([v.reshape(1, c).astype(F32) for v in vecs] + [jnp.zeros((8 - len(vecs), c), F32)], axis=0)


def _rope_tables(positions):
    inv_freq = 1.0 / (ROPE_THETA ** (jnp.arange(0, ROPE, 2, dtype=F32) / ROPE))
    ang = positions.astype(F32)[:, None] * inv_freq
    cos, sin = jnp.cos(ang), jnp.sin(ang)
    T = positions.shape[0]
    half = ROPE // 2
    one = jnp.ones((T, KR_LANE), F32)
    zero = jnp.zeros((T, KR_LANE), F32)
    tail1 = jnp.ones((T, HP - KR_LANE - ROPE), F32)
    tail0 = jnp.zeros((T, HP - KR_LANE - ROPE), F32)
    z16 = jnp.zeros((T, half), F32)
    cosf = jnp.concatenate([one, cos, cos, tail1], axis=1)
    sina = jnp.concatenate([zero, -sin, z16, tail0], axis=1)
    sinb = jnp.concatenate([zero, z16, sin, tail0], axis=1)
    return cosf, sina, sinb


def _kernel_weights(W):
    c = lambda a: a.astype(MXU_DTYPE)
    forms = dict(
        w_in=("w_in", lambda w: c(_pad_rows_in(w))),
        w_q=("mla_w_q_up", lambda w: c(_pad_heads(w, NOPE + ROPE))),
        w_kv=("mla_w_kv_up", lambda w: c(_pad_kv(w))),
        w_out=("w_out", lambda w: c(_pad_out_rows(w))),
        w_up=("ffn_w_up", c),
        w_down=("ffn_w_down", c),
        sc_w=("sc_conv_w", _rows8),
        ssd_w=("ssd_conv_w", _rows8),
        ffn_w=("ffn_conv_w", _rows8),
    )
    return {k: f(W[n]) for k, (n, f) in forms.items() if n in W}


def _layer_weights(KW, l):
    return {k: (v[l] if k in ("sc_w", "ssd_w", "ffn_w") else (v, l)) for k, v in KW.items()}


def _local_step(x, positions, target, W, S, ex=None):
    T = x.shape[0]
    tm = min(ROW_BLOCK, T)
    tmf = min(2 * ROW_BLOCK, T)
    tm_ffn = min(FFN_ROWS, T)
    cosf, sina, sinb = _rope_tables(positions)
    if ex is None:
        KW = _kernel_weights(W)
    else:
        early = _all_gather_weights(ex.shard(0, "early"))
    saved = []
    xl = x
    for l in range(DEPTH):
        lw = _layer_weights(KW, l) if ex is None else _kernel_weights(ex.weights(early, "early"))
        g_pre = S["norm_mix_pre"][l].reshape(1, -1)
        g_post = S["norm_mix_post"][l].reshape(1, -1)
        g_fpre = S["norm_ffn_pre"][l].reshape(1, -1)
        g_fpost = S["norm_ffn_post"][l].reshape(1, -1)
        qn = S["mla_q_norm"][l].reshape(1, -1)
        kvn = S["mla_kv_norm"][l].reshape(1, -1)
        ssd_b = S["ssd_conv_b"][l].reshape(1, -1)
        ssd_par = _row8(jnp.pad(S["ssd_dt_bias"][l], (0, LANE - SSD_HEADS)), jnp.pad(S["ssd_a_log"][l], (0, LANE - SSD_HEADS)),
                        jnp.pad(S["ssd_d"][l], (0, LANE - SSD_HEADS)))
        ssd_nw = S["ssd_norm"][l].reshape(1, -1)
        ffn_b = S["ffn_conv_b"][l].reshape(1, -1)

        (h1,) = _rows(lambda i, n, *v: _f_premix(*v), T, tmf, [_cur(xl)], [_cst(g_pre)], [_out(D_MODEL, BF16)], [], "pre_mix_norm")
        zin = _mm(h1, lw["w_in"], "nt", F32, "mm_in")
        qlat, kvlat = _rows(lambda i, n, *v: _f_mla_pre(*v), T, tmf, [_cur(zin, Q_LORA, 0), _cur(zin, KV_LORA, Z_CKV // KV_LORA)],
                            [_cst(qn), _cst(kvn)], [_out(Q_LORA, BF16), _out(KV_LORA, BF16)], [], "mla_pre_norm")
        qpad = _mm(qlat, lw["w_q"], "nn", F32, "mm_q_up")
        kvpad = _mm(kvlat, lw["w_kv"], "nn", BF16, "mm_kv_up")
        qr, kr = _rows(_k_rope_fwd, T, tmf, [_cur(qpad), _cur(kvpad, HEADS * HP, 0), _cur(zin, LANE, Z_KR // LANE),
                                            _cur(cosf), _cur(sina), _cur(sinb)], [],
                       [_out(HEADS * HP, BF16), _out(HEADS * HP, BF16)], [], "mla_rope")
        if ex is None:
            o = _flash_fwd(qr, kr, kvpad, T)
        else:
            nlate = len(ex.layouts["late"])
            o, got = _flash_fwd(qr, kr, kvpad, T, carry=ex.shard(l, "late") + (ex.shard(l + 1, "early") if l + 1 < DEPTH else []))
            lw.update(_kernel_weights(ex.weights(got[:nlate], "late")))
            early = got[nlate:]
        (yconv,) = _rows(_k_sconv_fwd, T, tmf, [_cur(zin, SC_DIM, Z_SCB // SC_DIM), _cur(zin, SC_DIM, Z_SCC // SC_DIM),
                                               _cur(zin, SC_DIM, Z_SCH // SC_DIM), _halo(zin, "prev", SC_DIM, Z_SCC // SC_DIM),
                                               _halo(zin, "prev", SC_DIM, Z_SCH // SC_DIM)], [_cst(lw["sc_w"])],
                         [_out(SC_DIM, F32)], [], "short_conv_fwd")
        (xbc,) = _rows(_k_ssdconv_fwd, T, tmf, [_cur(zin, SSD_CONV_DIM, Z_XBC // SSD_CONV_DIM),
                                               _halo(zin, "prev", SSD_CONV_DIM, Z_XBC // SSD_CONV_DIM)],
                       [_cst(lw["ssd_w"]), _cst(ssd_b)], [_out(SSD_CONV_DIM, F32)], [], "ssd_conv_fwd")
        yscan, states = _ssd_fwd(xbc, zin, ssd_par, T, Z_DT // LANE)
        (yssd,) = _rows(lambda i, n, *v: _f_ssd_gate(*v), T, tmf, [_cur(yscan), _cur(zin, SSD_DIM, Z_SSZ // SSD_DIM)], [_cst(ssd_nw)],
                        [_out(SSD_DIM, F32)], [], "ssd_gate_fwd")
        ycat = jnp.concatenate([o.astype(BF16), yconv.astype(BF16), yssd.astype(BF16)], axis=1)
        mixed = _mm(ycat, lw["w_out"], "nn", F32, "mm_out", tm=MM_TM)
        x1, h2 = _rows(lambda i, n, *v: _f_post_mix(*v), T, tmf, [_cur(xl), _cur(mixed)], [_cst(g_post), _cst(g_fpre)],
                       [_out(D_MODEL, F32), _out(D_MODEL, BF16)], [], "post_mix_fwd")
        upre = _mm(h2, lw["w_up"], "nn", F32, "mm_up")
        nt = FFN_DIM // FFN_TILE
        gcol, ucol = (lambda j: j), (lambda j: j + nt)
        (act,) = _rows(_k_ffnact_fwd, T, tm_ffn,
                       [(upre, FFN_TILE, gcol, "cur"), (upre, FFN_TILE, ucol, "cur"), (upre, FFN_TILE, gcol, "prev"),
                        (upre, FFN_TILE, ucol, "prev")],
                       [(lw["ffn_w"], FFN_TILE, gcol), (lw["ffn_w"], FFN_TILE, ucol), (ffn_b, FFN_TILE, gcol), (ffn_b, FFN_TILE, ucol)],
                       [(FFN_DIM, BF16, FFN_TILE, gcol)], [], "ffn_act_fwd", ncol=nt)
        dn = _mm(act, lw["w_down"], "nn", F32, "mm_down")
        (x2,) = _rows(lambda i, n, *v: _f_post_ffn(*v), T, tmf, [_cur(x1), _cur(dn)], [_cst(g_fpost)], [_out(D_MODEL, F32)], [], "post_ffn_fwd")
        saved.append(dict(lw=lw, x=xl, h1=h1, zin=zin, qlat=qlat, kvlat=kvlat, qr=qr, kr=kr, kvpad=kvpad, o=o, xbc=xbc,
                          yscan=yscan, states=states, ycat=ycat, mixed=mixed, x1=x1, h2=h2, upre=upre, act=act, dn=dn,
                          g_pre=g_pre, g_post=g_post, g_fpre=g_fpre, g_fpost=g_fpost, qn=qn, kvn=kvn, ssd_b=ssd_b,
                          ssd_par=ssd_par, ssd_nw=ssd_nw, ffn_b=ffn_b))
        xl = x2

    gx, loss_part = _rows(_k_loss, T, tmf, [_cur(xl), _cur(target)], [], [_out(D_MODEL, F32)], [_acc(1, LANE)], "loss_head")

    GW = {k: [None] * DEPTH for k in ("w_in", "mla_w_q_up", "mla_w_kv_up", "sc_conv_w", "ssd_conv_w", "w_out", "ffn_w_up",
                                      "ffn_conv_w", "ffn_w_down")}
    GS = {k: [None] * DEPTH for k in ("norm_mix_pre", "norm_mix_post", "norm_ffn_pre", "norm_ffn_post", "mla_q_norm", "mla_kv_norm",
                                      "ssd_conv_b", "ssd_dt_bias", "ssd_a_log", "ssd_d", "ssd_norm", "ffn_conv_b")}
    nt = FFN_DIM // FFN_TILE
    gcol, ucol = (lambda j: j), (lambda j: j + nt)
    pending = None
    for l in reversed(range(DEPTH)):
        s = saved[l]
        lw = s["lw"]
        gx1, ddn, dgf = _rows_vjp(_f_post_ffn, T, tm, [s["x1"], s["dn"]], [s["g_fpost"]], [gx], [F32, BF16], "post_ffn_bwd")
        GS["norm_ffn_post"][l] = dgf[0]
        dact = _mm(ddn, lw["w_down"], "nt", F32, "mm_down_dx")
        GW["ffn_w_down"][l] = _mm(s["act"], ddn, "tn", BF16, "mm_down_dw")
        up = s["upre"]
        dug, duu, dwg, dwu, dbg, dbu = _rows(
            _k_ffnact_bwd, T, tm_ffn,
            [(up, FFN_TILE, gcol, "cur"), (up, FFN_TILE, ucol, "cur"), (dact, FFN_TILE, gcol, "cur"), (up, FFN_TILE, gcol, "prev"),
             (up, FFN_TILE, ucol, "prev"), (up, FFN_TILE, gcol, "next"), (up, FFN_TILE, ucol, "next"), (dact, FFN_TILE, gcol, "next")],
            [(lw["ffn_w"], FFN_TILE, gcol), (lw["ffn_w"], FFN_TILE, ucol), (s["ffn_b"], FFN_TILE, gcol), (s["ffn_b"], FFN_TILE, ucol)],
            [(FFN_DIM, BF16, FFN_TILE, gcol)] * 2,
            [(HALO, FFN_DIM, FFN_TILE, gcol)] * 2 + [(1, FFN_DIM, FFN_TILE, gcol)] * 2, "ffn_act_bwd", ncol=nt)
        GW["ffn_conv_w"][l] = jnp.concatenate([dwg[:3], dwu[:3]], axis=1)
        GS["ffn_conv_b"][l] = jnp.concatenate([dbg[0], dbu[0]])
        dh2 = _mm((dug, duu), lw["w_up"], "nt", F32, "mm_up_dx")
        GW["ffn_w_up"][l] = (_mm(s["h2"], dug, "tn", BF16, "mm_up_dw_gate"), _mm(s["h2"], duu, "tn", BF16, "mm_up_dw_up"))
        gx0, dmixed, dgp, dgf = _rows_vjp(_f_post_mix, T, tm, [s["x"], s["mixed"]], [s["g_post"], s["g_fpre"]], [gx1, dh2],
                                          [F32, BF16], "post_mix_bwd")
        GS["norm_mix_post"][l], GS["norm_ffn_pre"][l] = dgp[0], dgf[0]
        dycat = _mm(dmixed, lw["w_out"], "nt", F32, "mm_out_dx")
        GW["w_out"][l] = _unpad_out_rows(_mm(s["ycat"], dmixed, "tn", BF16, "mm_out_dw"))
        zin = s["zin"]
        dyscan, dz, dnw = _rows(_vjp_wrap(_f_ssd_gate, 2, 1), T, tm,
                                [_cur(s["yscan"]), _cur(zin, SSD_DIM, Z_SSZ // SSD_DIM), _cur(dycat, SSD_DIM, (HEADS * HP + SC_DIM) // SSD_DIM)],
                                [_cst(s["ssd_nw"])], [_out(SSD_DIM, F32), _out(SSD_DIM, BF16)], [_acc(1, SSD_DIM)], "ssd_gate_bwd")
        GS["ssd_norm"][l] = dnw[0]
        dxbc, ddtraw, dpar = _ssd_bwd(s["xbc"], zin, s["ssd_par"], s["states"], dyscan, T, Z_DT // LANE)
        GS["ssd_dt_bias"][l], GS["ssd_a_log"][l], GS["ssd_d"][l] = dpar[0, :SSD_HEADS], dpar[1, :SSD_HEADS], dpar[2, :SSD_HEADS]
        xb = Z_XBC // SSD_CONV_DIM
        dxraw, dsw, dsb = _rows(_k_ssdconv_bwd, T, tm,
                                [_cur(zin, SSD_CONV_DIM, xb), _cur(dxbc), _halo(zin, "prev", SSD_CONV_DIM, xb),
                                 _halo(zin, "next", SSD_CONV_DIM, xb), _halo(dxbc, "next")],
                                [_cst(lw["ssd_w"]), _cst(s["ssd_b"])], [_out(SSD_CONV_DIM, BF16)],
                                [_acc(HALO, SSD_CONV_DIM), _acc(1, SSD_CONV_DIM)], "ssd_conv_bwd")
        GW["ssd_conv_w"][l] = dsw[:4]
        GS["ssd_conv_b"][l] = dsb[0]
        cb = (HEADS * HP) // SC_DIM
        dscb, dscc, dsch, dscw = _rows(_k_sconv_bwd, T, tm,
                                       [_cur(zin, SC_DIM, Z_SCB // SC_DIM), _cur(zin, SC_DIM, Z_SCC // SC_DIM),
                                        _cur(zin, SC_DIM, Z_SCH // SC_DIM), _cur(dycat, SC_DIM, cb),
                                        _halo(zin, "prev", SC_DIM, Z_SCC // SC_DIM), _halo(zin, "prev", SC_DIM, Z_SCH // SC_DIM),
                                        _halo(zin, "next", SC_DIM, Z_SCB // SC_DIM), _halo(dycat, "next", SC_DIM, cb)],
                                       [_cst(lw["sc_w"])], [_out(SC_DIM, BF16)] * 3, [_acc(HALO, SC_DIM)], "short_conv_bwd")
        GW["sc_conv_w"][l] = dscw[:3]
        if ex is None:
            dq, dk, dv = _flash_bwd(s["qr"], s["kr"], s["kvpad"], s["o"], dycat, T)
        else:
            sums = ex.submit([({n: GW[n][l] for ns in LATE for n in ns}, "late")] + ([(pending, "early")] if pending else []))
            late = sums[0]
            dq, dk, dv, parts = _flash_bwd(s["qr"], s["kr"], s["kvpad"], s["o"], dycat, T, carry=[p for ps in sums for p in ps])
            ex.collect(l, "late", parts[:len(late)])
            if pending:
                ex.collect(l + 1, "early", parts[len(late):])
        dqpad, dkvpad, dkr = _rows(_k_rope_bwd, T, tm, [_cur(dq), _cur(dk), _cur(dv), _cur(cosf), _cur(sina), _cur(sinb)], [],
                                   [_out(HEADS * HP, BF16), _out(2 * HEADS * HP, BF16), _out(LANE, BF16)], [], "mla_rope_bwd")
        dqlat = _mm(dqpad, lw["w_q"], "nt", F32, "mm_q_dx")
        GW["mla_w_q_up"][l] = _unpad_heads(_mm(s["qlat"], dqpad, "tn", BF16, "mm_q_dw"), NOPE + ROPE)
        dkvlat = _mm(dkvpad, lw["w_kv"], "nt", F32, "mm_kv_dx")
        GW["mla_w_kv_up"][l] = _unpad_kv(_mm(s["kvlat"], dkvpad, "tn", BF16, "mm_kv_dw"))
        dcq, dckv, dqn, dkvn = _rows(_vjp_wrap(_f_mla_pre, 2, 2), T, tm,
                                     [_cur(zin, Q_LORA, 0), _cur(zin, KV_LORA, Z_CKV // KV_LORA), _cur(dqlat), _cur(dkvlat)],
                                     [_cst(s["qn"]), _cst(s["kvn"])], [_out(Q_LORA, BF16), _out(KV_LORA, BF16)],
                                     [_acc(1, Q_LORA), _acc(1, KV_LORA)], "mla_pre_bwd")
        GS["mla_q_norm"][l], GS["mla_kv_norm"][l] = dqn[0], dkvn[0]
        dzin = jnp.concatenate([dcq, dckv, dkr, dscb, dscc, dsch, dz, dxraw, ddtraw.astype(BF16), jnp.zeros((T, ZIN - Z_DT - LANE), BF16)], axis=1)
        dh1 = _mm(dzin, lw["w_in"], "nn", F32, "mm_in_dx")
        GW["w_in"][l] = _unpad_rows_in(_mm(dzin, s["h1"], "tn", BF16, "mm_in_dw"))
        gx, dgp = _rows(_vjp_wrap(_f_premix, 1, 1, add_first=True), T, tm, [_cur(s["x"]), _cur(dh1), _cur(gx0)], [_cst(s["g_pre"])],
                        [_out(D_MODEL, F32)], [_acc(1, D_MODEL)], "pre_mix_bwd")
        GS["norm_mix_pre"][l] = dgp[0]
        if ex is not None:
            pending = {n: GW[n][l] for ns in EARLY for n in ns}
    if ex is not None:
        ex.collect(0, "early", _rs_chip_exchange(ex.submit([(pending, "early")])[0]))
    GS = {k: jnp.stack(v) for k, v in GS.items()}
    return loss_part[0, 0], gx, GW, GS


WEIGHTS = ("norm_mix_pre", "norm_mix_post", "norm_ffn_pre", "norm_ffn_post", "w_in", "mla_q_norm", "mla_w_q_up", "mla_kv_norm",
           "mla_w_kv_up", "sc_conv_w", "ssd_conv_w", "ssd_conv_b", "ssd_dt_bias", "ssd_a_log", "ssd_d", "ssd_norm", "w_out",
           "ffn_w_up", "ffn_conv_w", "ffn_conv_b", "ffn_w_down")
SHARDED = (("w_in", 2), ("mla_w_q_up", 2), ("mla_w_kv_up", 2), ("sc_conv_w", 2), ("ssd_conv_w", 2), ("w_out", 1),
           ("ffn_w_up", 2), ("ffn_conv_w", 2), ("ffn_w_down", 1))
SMALL = tuple(n for n in WEIGHTS if n not in dict(SHARDED))
N_CHIPS = 4
N_DEV = 8
ROW_ALIGN = 64
SLAB_ALIGN = 16
EARLY = (("w_in", "mla_w_q_up", "mla_w_kv_up", "sc_conv_w", "ssd_conv_w"),)
LATE = (("ffn_w_down", "w_out"), ("ffn_w_up", "ffn_conv_w"))
TRANSPOSED = ("w_in",)


def _is_rows(shape, width):
    return shape[-1] == width and math.prod(shape[:-1]) % SLAB_ALIGN == 0


def _is_short(shape, width):
    return len(shape) == 2 and shape[1] == width and not _is_rows(shape, width)


def _slab_rows(shape, width):
    if _is_rows(shape, width):
        return math.prod(shape[:-1])
    if _is_short(shape, width):
        return -(-shape[0] // SLAB_ALIGN) * SLAB_ALIGN
    return -(-math.prod(shape) // (width * SLAB_ALIGN)) * SLAB_ALIGN


def _slab(piece, width, dtype, lead=0):
    ld, shape = piece.shape[:lead], piece.shape[lead:]
    rows = _slab_rows(shape, width)
    if _is_rows(shape, width):
        return piece.astype(dtype).reshape(ld + (rows, width))
    if _is_short(shape, width):
        return jnp.pad(piece.astype(dtype), [(0, 0)] * lead + [(0, rows - shape[0]), (0, 0)])
    flat = piece.astype(dtype).reshape(ld + (-1,))
    return jnp.pad(flat, [(0, 0)] * lead + [(0, rows * width - flat.shape[-1])]).reshape(ld + (rows, width))


def _unslab(slab, shape, lead=0):
    ld = slab.shape[:lead]
    if _is_rows(shape, slab.shape[-1]):
        return slab.reshape(ld + tuple(shape))
    if _is_short(shape, slab.shape[-1]):
        return slab[..., :shape[0], :]
    return slab.reshape(ld + (-1,))[..., :math.prod(shape)].reshape(ld + tuple(shape))


def _layout(shapes, names, width):
    ents, off = [], 0
    for n in names:
        shp = tuple(shapes[n])
        todo = [(None, False, shp), (None, True, shp)] if n.endswith("conv_w") else [(l, False, shp[1:]) for l in range(shp[0])]
        for l, lo, ps in todo:
            r = _slab_rows(ps, width)
            ents.append((n, l, lo, ps, off, r))
            off += r
    return width, -(-off // ROW_ALIGN) * ROW_ALIGN, ents


def _pack(layout, piece, dtype, lead=0):
    width, rows, ents = layout
    slabs, ld = [], None
    for n, l, lo, ps, off, r in ents:
        p = piece(n, l, lo)
        slabs.append(None if p is None else _slab(p, width, dtype, lead))
        ld = ld if p is None else p.shape[:lead]
    used = ents[-1][4] + ents[-1][5]
    slabs = [jnp.zeros(ld + (e[5], width), dtype) if s is None else s for s, e in zip(slabs, ents)]
    if rows > used:
        slabs.append(jnp.zeros(ld + (rows - used, width), dtype))
    return jnp.concatenate(slabs, axis=lead)


ANY = pl.BlockSpec(memory_space=pl.ANY)


def _pos():
    return lax.axis_index("x"), lax.axis_index("y"), lax.axis_index("c")


def _other_chips(x, y):
    return ((1 - x, y), (x, 1 - y), (1 - x, 1 - y))


def _remote(src, dst, ssem, rsem, dev):
    return pltpu.make_async_remote_copy(src_ref=src, dst_ref=dst, send_sem=ssem, recv_sem=rsem, device_id=dev, device_id_type=MESH)


AG_CHUNKS = 2


def _chip_index():
    return 2 * lax.axis_index("x") + lax.axis_index("y")


def _ag_sems(nbuf):
    return [pltpu.SemaphoreType.DMA((nbuf * 3 * AG_CHUNKS,))] * 4


def _ag_plan(w_refs, out_refs, sems):
    isend, irecv, dsend, drecv = sems
    x, y, c = _pos()
    k = 2 * x + y
    sib = (x, y, 1 - c)
    sends, lands, forwards, finals = [], [], [], []
    s = 0
    for w_ref, out_ref in zip(w_refs, out_refs):
        H = w_ref.shape[0] // 2
        CH = H // AG_CHUNKS
        for cx, cy in _other_chips(x, y):
            for ch in range(AG_CHUNKS):
                mine = out_ref.at[k, pl.ds(c * H + ch * CH, CH), :]
                near = out_ref.at[2 * cx + cy, pl.ds(c * H + ch * CH, CH), :]
                far = out_ref.at[2 * cx + cy, pl.ds((1 - c) * H + ch * CH, CH), :]
                sends.append(_remote(w_ref.at[pl.ds(c * H + ch * CH, CH), :], mine, isend.at[s], irecv.at[s], (cx, cy, c)))
                lands.append(_remote(near, near, isend.at[s], irecv.at[s], (cx, cy, c)))
                forwards.append(_remote(near, near, dsend.at[s], drecv.at[s], sib))
                finals.append(_remote(far, far, dsend.at[s], drecv.at[s], sib))
                s += 1
    return sends, lands, forwards, finals


def _own_slot(got, own):
    return lax.dynamic_update_slice(got, own[None], (_chip_index(), 0, 0))


def _all_gather_weights(ws):
    nb = len(ws)

    def body(*refs):
        sends, lands, forwards, finals = _ag_plan(refs[:nb], refs[nb:2 * nb], refs[2 * nb:])
        for cp in sends:
            cp.start()
        for land, fw in zip(lands, forwards):
            land.wait_recv()
            fw.start()
        for cp in finals:
            cp.wait_recv()
        for cp in sends + forwards:
            cp.wait_send()

    got = pl.pallas_call(
        body, name="all_gather_weights", in_specs=[ANY] * nb, out_specs=[ANY] * nb,
        out_shape=[jax.ShapeDtypeStruct((N_CHIPS,) + w.shape, w.dtype) for w in ws], scratch_shapes=_ag_sems(nb),
    )(*ws)
    return [_own_slot(g, w) for g, w in zip(got, ws)]


def _rs_pair_exchange(gs):
    nb = len(gs)

    def body(*refs):
        g_refs, got_refs, (ssem, rsem) = refs[:nb], refs[nb:2 * nb], refs[2 * nb:]
        x, y, c = _pos()
        cps = []
        for b, (g_ref, got_ref) in enumerate(zip(g_refs, got_refs)):
            H = g_ref.shape[1] // 2
            for kk in range(N_CHIPS):
                s = b * N_CHIPS + kk
                cps.append(_remote(g_ref.at[kk, pl.ds((1 - c) * H, H), :], got_ref.at[kk], ssem.at[s], rsem.at[s], (x, y, 1 - c)))
        for cp in cps:
            cp.start()
        for cp in cps:
            cp.wait()

    return pl.pallas_call(
        body, name="rs_pair_exchange", in_specs=[ANY] * nb, out_specs=[ANY] * nb,
        out_shape=[jax.ShapeDtypeStruct((N_CHIPS, g.shape[1] // 2, g.shape[2]), g.dtype) for g in gs],
        scratch_shapes=[pltpu.SemaphoreType.DMA((nb * N_CHIPS,))] * 2,
    )(*gs)


def _chip_sems(nbuf):
    return [pltpu.SemaphoreType.DMA((nbuf * 3,))] * 2


def _chip_plan(p_refs, out_refs, sems):
    ssem, rsem = sems
    x, y, c = _pos()
    sends, lands = [], []
    s = 0
    for p_ref, out_ref in zip(p_refs, out_refs):
        for cx, cy in _other_chips(x, y):
            sends.append(_remote(p_ref.at[2 * cx + cy], out_ref.at[2 * x + y], ssem.at[s], rsem.at[s], (cx, cy, c)))
            land = out_ref.at[2 * cx + cy]
            lands.append(_remote(land, land, ssem.at[s], rsem.at[s], (cx, cy, c)))
            s += 1
    return sends, lands


def _chip_parts(got, ps):
    k = _chip_index()
    return [lax.dynamic_update_slice(g, lax.dynamic_slice_in_dim(p, k, 1, axis=0), (k, 0, 0)) for g, p in zip(got, ps)]


def _rs_chip_exchange(ps):
    nb = len(ps)

    def body(*refs):
        sends, lands = _chip_plan(refs[:nb], refs[nb:2 * nb], refs[2 * nb:])
        for cp in sends:
            cp.start()
        for cp in lands:
            cp.wait_recv()
        for cp in sends:
            cp.wait_send()

    got = pl.pallas_call(
        body, name="rs_chip_exchange", in_specs=[ANY] * nb, out_specs=[ANY] * nb,
        out_shape=[jax.ShapeDtypeStruct(p.shape, p.dtype) for p in ps], scratch_shapes=_chip_sems(nb),
    )(*ps)
    return _chip_parts(got, ps)


def _rs_pair_share(fs):
    nb = len(fs)

    def body(*refs):
        f_refs, out_refs, (ssem, rsem) = refs[:nb], refs[nb:2 * nb], refs[2 * nb:]
        x, y, c = _pos()
        sends, lands = [], []
        for b, (f_ref, out_ref) in enumerate(zip(f_refs, out_refs)):
            sends.append(_remote(f_ref, out_ref.at[c], ssem.at[b], rsem.at[b], (x, y, 1 - c)))
            land = out_ref.at[1 - c]
            lands.append(_remote(land, land, ssem.at[b], rsem.at[b], (x, y, 1 - c)))
        for cp in sends:
            cp.start()
        for cp in lands:
            cp.wait_recv()
        for cp in sends:
            cp.wait_send()

    got = pl.pallas_call(
        body, name="rs_pair_share", in_specs=[ANY] * nb, out_specs=[ANY] * nb,
        out_shape=[jax.ShapeDtypeStruct((2,) + f.shape, f.dtype) for f in fs],
        scratch_shapes=[pltpu.SemaphoreType.DMA((nb,))] * 2,
    )(*fs)
    return [lax.dynamic_update_slice(g, f[None], (lax.axis_index("c"), 0, 0)) for g, f in zip(got, fs)]


def _all_reduce_small(s):
    r, C = s.shape

    def body(s_ref, o_ref, buf, ssem, rsem):
        x, y, c = _pos()
        me = 4 * x + 2 * y + c
        buf[me] = s_ref[...]
        cps = []
        for m in range(1, N_DEV):
            mx, my, mc = (m >> 2) & 1, (m >> 1) & 1, m & 1
            peer = (x ^ mx, y ^ my, c ^ mc)
            cp = _remote(s_ref, buf.at[me], ssem.at[m - 1], rsem.at[m - 1], peer)
            cp.start()
            cps.append(cp)
        for m in range(1, N_DEV):
            mx, my, mc = (m >> 2) & 1, (m >> 1) & 1, m & 1
            src = 4 * (x ^ mx) + 2 * (y ^ my) + (c ^ mc)
            _remote(s_ref, buf.at[src], ssem.at[m - 1], rsem.at[m - 1], (x ^ mx, y ^ my, c ^ mc)).wait_recv()
        for cp in cps:
            cp.wait_send()
        acc = buf[0]
        for j in range(1, N_DEV):
            acc = acc + buf[j]
        o_ref[...] = acc

    return pl.pallas_call(
        body, name="all_reduce_small", in_specs=[pl.BlockSpec(memory_space=pltpu.VMEM)],
        out_specs=pl.BlockSpec(memory_space=pltpu.VMEM), out_shape=jax.ShapeDtypeStruct((r, C), F32),
        scratch_shapes=[pltpu.VMEM((N_DEV, r, C), F32), pltpu.SemaphoreType.DMA((N_DEV - 1,)), pltpu.SemaphoreType.DMA((N_DEV - 1,))],
    )(s)


def _rtile(n, pref):
    if n <= pref:
        return n
    t = (pref // 16) * 16
    while t >= 16:
        if n % t == 0:
            return t
        t -= 16
    raise ValueError(f"no row tile for {n}")


def _rs_pair_sums(gpks):
    gots = _rs_pair_exchange(gpks)
    out = []
    for gpk, got in zip(gpks, gots):
        _, R, C = gpk.shape
        H = R // 2
        own = lax.dynamic_index_in_dim(gpk.reshape(N_CHIPS, 2, H, C), lax.axis_index("c"), axis=1, keepdims=False)
        (part,) = _rows(lambda i, n, a, b: (a.astype(F32) + b.astype(F32),), N_CHIPS * H, _rtile(N_CHIPS * H, 512),
                        [_cur(own.reshape(N_CHIPS * H, C)), _cur(got.reshape(N_CHIPS * H, C))], [], [_out(C, BF16)], [], "rs_pair_add")
        out.append(part.reshape(N_CHIPS, H, C))
    return out


def _rs_chip_sums(parts):
    def add4(i, n, a, b, c, d):
        return (((a.astype(F32) + b.astype(F32)) + c.astype(F32)) + d.astype(F32),)

    out = []
    for p in parts:
        _, H, C = p.shape
        tm = _rtile(H, 1024)
        (red,) = _rows(add4, H, tm, [(p.reshape(N_CHIPS * H, C), C, functools.partial(_const, v=0), j * (H // tm)) for j in range(N_CHIPS)],
                       [], [_out(C, F32)], [], "rs_chip_add")
        out.append(red)
    return out


class _Exchange:
    def __init__(self, a):
        self.a = a
        self.axis = {n: (1 if n in TRANSPOSED else ax) for n, ax in SHARDED}
        shapes = {n: (1,) + tuple(self.packed(n, a[n]).shape[1:]) for n in self.axis}
        widths = lambda names: shapes[names[0]][-1] if names[0] == "ffn_w_up" else PACK_COLS
        self.layouts = {"early": [_layout(shapes, ns, widths(ns)) for ns in EARLY], "late": [_layout(shapes, ns, widths(ns)) for ns in LATE]}
        self.reduced = {}

    @staticmethod
    def packed(n, w):
        return jnp.swapaxes(w, -1, -2) if n in TRANSPOSED else w

    def shard(self, l, group):
        def piece(n, li, lo):
            w = self.packed(n, self.a[n][l:l + 1] if li is None else self.a[n][l])
            return w - w.astype(BF16).astype(F32) if lo else w
        return [_pack(lay, piece, BF16) for lay in self.layouts[group]]

    def weights(self, gathered, group):
        W, resid = {}, {}
        for (width, rows, ents), g in zip(self.layouts[group], gathered):
            for n, li, lo, ps, off, r in ents:
                parts = _unslab(g[:, off:off + r], ps, lead=1)
                ax = self.axis[n] + (1 if li is None else 0)
                full = jnp.moveaxis(parts, 0, ax - 1)
                full = full.reshape(full.shape[:ax - 1] + (-1,) + full.shape[ax + 1:])
                (resid if lo else W)[n] = full[0] if li is None else full
        for n in resid:
            W[n] = W[n].astype(F32) + resid[n].astype(F32)
        return W

    def submit(self, jobs):
        def by_chip(g, ax, parts=N_CHIPS):
            g = g.reshape(g.shape[:ax] + (parts, g.shape[ax] // parts) + g.shape[ax + 1:])
            return jnp.moveaxis(g, ax, 0)

        def pieces_of(GW):
            def piece(n, li, lo):
                if lo:
                    return None
                g = GW[n]
                if isinstance(g, tuple):
                    return jnp.concatenate([by_chip(h, self.axis[n] - 1, N_CHIPS // 2) for h in g])
                return by_chip(g[None], self.axis[n]) if li is None else by_chip(g, self.axis[n] - 1)
            return piece

        sums = _rs_pair_sums([_pack(lay, pieces_of(GW), BF16, lead=1) for GW, group in jobs for lay in self.layouts[group]])
        out, at = [], 0
        for _, group in jobs:
            out.append(sums[at:at + len(self.layouts[group])])
            at += len(self.layouts[group])
        return out

    def collect(self, l, group, parts):
        self.reduced[l, group] = _rs_chip_sums(parts)

    def finish(self):
        keys = [(l, g) for l in range(DEPTH) for g in self.layouts]
        flat = _rs_pair_share([f for key in keys for f in self.reduced[key]])
        both, at = {}, 0
        for key in keys:
            both[key] = flat[at:at + len(self.layouts[key[1]])]
            at += len(self.layouts[key[1]])
        grads = {}
        for group, lays in self.layouts.items():
            for b, (width, rows, ents) in enumerate(lays):
                for n, li, lo, ps, off, r in ents:
                    if not lo:
                        per_layer = [self.packed(n, _unslab(both[l, group][b].reshape(rows, width)[off:off + r], ps)) for l in range(DEPTH)]
                        grads[n] = jnp.concatenate(per_layer) if li is None else jnp.stack(per_layer)
        return grads


def _adam(w, g, m, v, name, g_row=0):
    shp = w.shape
    two = lambda a: a.reshape(-1, shp[-1])
    rows = math.prod(shp[:-1])
    tm = _rtile(rows, ROW_BLOCK if shp[-1] <= PACK_COLS else ROW_BLOCK // 2)
    assert g_row % tm == 0
    g_in = (two(g), shp[-1], functools.partial(_const, v=0), g_row // tm)
    res = _rows(_k_adam, rows, tm, [_cur(two(w)), g_in, _cur(two(m)), _cur(two(v))], [], [_out(shp[-1], F32)] * 4, [], name)
    return tuple(r.reshape(shp) for r in res)


def _pack_flat(parts, rows):
    flat = jnp.concatenate([p.astype(F32).reshape(-1) for p in parts])
    return jnp.pad(flat, (0, rows * PACK_COLS - flat.shape[0])).reshape(rows, PACK_COLS)


def _unpack_flat(buf, shapes):
    flat, out, off = buf.reshape(-1), [], 0
    for shp in shapes:
        n = math.prod(shp)
        out.append(flat[off:off + n].reshape(shp))
        off += n
    return out


def kernel(x, positions, norm_mix_pre, norm_mix_post, norm_ffn_pre, norm_ffn_post, w_in, mla_q_norm, mla_w_q_up, mla_kv_norm, mla_w_kv_up, sc_conv_w, ssd_conv_w, ssd_conv_b, ssd_dt_bias, ssd_a_log, ssd_d, ssd_norm, w_out, ffn_w_up, ffn_conv_w, ffn_conv_b, ffn_w_down, loss_target, m_norm_mix_pre, m_norm_mix_post, m_norm_ffn_pre, m_norm_ffn_post, m_w_in, m_mla_q_norm, m_mla_w_q_up, m_mla_kv_norm, m_mla_w_kv_up, m_sc_conv_w, m_ssd_conv_w, m_ssd_conv_b, m_ssd_dt_bias, m_ssd_a_log, m_ssd_d, m_ssd_norm, m_w_out, m_ffn_w_up, m_ffn_conv_w, m_ffn_conv_b, m_ffn_w_down, v_norm_mix_pre, v_norm_mix_post, v_norm_ffn_pre, v_norm_ffn_post, v_w_in, v_mla_q_norm, v_mla_w_q_up, v_mla_kv_norm, v_mla_w_kv_up, v_sc_conv_w, v_ssd_conv_w, v_ssd_conv_b, v_ssd_dt_bias, v_ssd_a_log, v_ssd_d, v_ssd_norm, v_w_out, v_ffn_w_up, v_ffn_conv_w, v_ffn_conv_b, v_ffn_w_down):
    a = dict(locals())
    ex = _Exchange(a)
    S = {n: a[n] for n in SMALL}
    loss_part, gx, _, GS = _local_step(a["x"][0], a["positions"][0], a["loss_target"][0], None, S, ex)

    grads, delta, new_m, new_v = {}, {}, {}, {}
    for n, g in ex.finish().items():
        grads[n], delta[n], new_m[n], new_v[n] = _adam(a[n], g, a["m_" + n], a["v_" + n], "adamw_" + n)

    small_shapes = [a[n].shape for n in SMALL]
    rs = -(-(sum(math.prod(s) for s in small_shapes) + 1) // (PACK_COLS * SLAB_ALIGN)) * SLAB_ALIGN
    red = _all_reduce_small(_pack_flat([GS[n] for n in SMALL] + [loss_part.reshape(1)], rs))
    loss = _unpack_flat(red, small_shapes + [(1,)])[-1][0]
    pk = lambda pre: _pack_flat([a[pre + n] for n in SMALL], rs)
    for dst, buf in zip((grads, delta, new_m, new_v), _adam(pk(""), red, pk("m_"), pk("v_"), "adamw_small")):
        dst.update(zip(SMALL, _unpack_flat(buf, small_shapes)))

    return (loss, gx[None], *[grads[n] for n in WEIGHTS], *[delta[n] for n in WEIGHTS], *[new_m[n] for n in WEIGHTS],
            *[new_v[n] for n in WEIGHTS])
```

```python
import functools
import math

import jax
import jax.numpy as jnp
from jax import lax
from jax.experimental import pallas as pl
from jax.experimental.pallas import tpu as pltpu

F32 = jnp.float32
BF16 = jnp.bfloat16
MXU_DTYPE = jnp.bfloat16
HIGHEST = lax.Precision.HIGHEST
MESH = pl.DeviceIdType.MESH

D_MODEL = 1024
DEPTH = 4
HEADS = 8
Q_LORA = 256
KV_LORA = 128
NOPE = 64
ROPE = 32
VDIM = 64
ROPE_THETA = 10000.0
SC_DIM = 256
SSD_HEADS = 4
SSD_HEAD_DIM = 64
SSD_STATE = 128
SSD_DIM = 256
SSD_CONV_DIM = 768
SSD_CHUNK = 128
FFN_DIM = 2816
NORM_EPS = 1e-6
QK_SCALE = (NOPE + ROPE) ** -0.5
LANE = 128
HP = 128
FLASH_HEADS = 8
FLASH_HEADS_FWD = 8
FLASH_BLOCK = 512

ZIN = 2560
Z_CQ, Z_CKV, Z_KR, Z_SCB, Z_SCC, Z_SCH, Z_SSZ, Z_XBC, Z_DT = 0, 256, 384, 512, 768, 1024, 1280, 1536, 2304
KR_LANE = 64
FFN_TILE = 256
FFN_ROWS = 2048
ROW_BLOCK = 512

ADAM_LR, ADAM_B1, ADAM_B2, ADAM_EPS, ADAM_WD, ADAM_STEP = 0.001, 0.9, 0.999, 1e-08, 0.01, 10

PACK_COLS = 1024


def _tile(n, pref):
    if n <= pref:
        return n
    t = (pref // LANE) * LANE
    while t >= LANE:
        if n % t == 0:
            return t
        t -= LANE
    raise ValueError(f"no tile for {n}")


MM_TM, MM_TN, MM_TK = 1024, 1408, 1536


def _mm(a, b, mode, out_dtype, name, tm=None, tn=MM_TN, tkmax=MM_TK):
    pair = isinstance(a, tuple)
    a_list = list(a) if pair else [a]
    layer = None
    if isinstance(b, tuple):
        b, layer = b
    bshape = b.shape[-2:]
    if mode == "nn":
        (M, Ka), (_, N) = a_list[0].shape, bshape
    elif mode == "nt":
        (M, Ka), (N, _) = a_list[0].shape, bshape
    else:
        (Ka, M), (_, N) = a_list[0].shape, bshape
    tk = _tile(Ka, tkmax)
    nka = Ka // tk
    nk = nka * len(a_list)
    if tm is None:
        tm = MM_TN if mode == "tn" else (2 * MM_TM if nk == 1 else MM_TM)
    tm, tn = _tile(M, tm), _tile(N, tn)

    def bspec(shape, index):
        if layer is None:
            return pl.BlockSpec(shape, index)
        return pl.BlockSpec((None,) + shape, lambda i, j, k: (layer,) + index(i, j, k))

    if mode == "nn":
        a_specs = [pl.BlockSpec((tm, tk), lambda i, j, k: (i, jnp.minimum(k, nka - 1))),
                   pl.BlockSpec((tm, tk), lambda i, j, k: (i, jnp.maximum(k - nka, 0)))][:len(a_list)]
        b_spec = bspec((tk, tn), lambda i, j, k: (k, j))
        dims = NN
    elif mode == "nt":
        a_specs = [pl.BlockSpec((tm, tk), lambda i, j, k: (i, jnp.minimum(k, nka - 1))),
                   pl.BlockSpec((tm, tk), lambda i, j, k: (i, jnp.maximum(k - nka, 0)))][:len(a_list)]
        b_spec = bspec((tn, tk), lambda i, j, k: (j, k))
        dims = NT
    else:
        a_specs = [pl.BlockSpec((tk, tm), lambda i, j, k: (k, i))]
        b_spec = pl.BlockSpec((tk, tn), lambda i, j, k: (k, j))
        dims = TN
    na = len(a_list)

    def body(*refs):
        a_refs, b_ref, o_ref = refs[:na], refs[na], refs[na + 1]
        k = pl.program_id(2)

        def prod(a_ref):
            return lax.dot_general(a_ref[...].astype(MXU_DTYPE), b_ref[...].astype(MXU_DTYPE), dims, preferred_element_type=F32)

        if nk == 1:
            o_ref[...] = prod(a_refs[0]).astype(o_ref.dtype)
            return
        acc_ref = refs[na + 2]

        @pl.when(k == 0)
        def _():
            acc_ref[...] = prod(a_refs[0])

        @pl.when((k > 0) & (k < nka))
        def _():
            acc_ref[...] += prod(a_refs[0])

        if pair:
            @pl.when(k >= nka)
            def _():
                acc_ref[...] += prod(a_refs[1])

        @pl.when(k == nk - 1)
        def _():
            o_ref[...] = acc_ref[...].astype(o_ref.dtype)

    return pl.pallas_call(
        body, name=name, grid=(M // tm, N // tn, nk),
        in_specs=a_specs + [b_spec], out_specs=pl.BlockSpec((tm, tn), lambda i, j, k: (i, j)),
        out_shape=jax.ShapeDtypeStruct((M, N), out_dtype),
        scratch_shapes=[pltpu.VMEM((tm, tn), F32)] if nk > 1 else [],
        compiler_params=pltpu.CompilerParams(dimension_semantics=("parallel", "parallel", "arbitrary")),
    )(*a_list, b)


HALO = 8


def _const(j, v):
    return v


def _rows(fn, T, tm, ins, consts, outs, accs, name, ncol=1):
    n = T // tm
    hb = tm // HALO
    last = T // HALO - 1
    in_specs, args = [], []
    for arr, bc, cb, kind in ins:
        if isinstance(kind, int):
            in_specs.append(pl.BlockSpec((tm, bc), lambda j, i, cb=cb, off=kind: (i + off, cb(j))))
        elif kind == "cur":
            in_specs.append(pl.BlockSpec((tm, bc), lambda j, i, cb=cb: (i, cb(j))))
        elif kind == "prev":
            in_specs.append(pl.BlockSpec((HALO, bc), lambda j, i, cb=cb: (jnp.maximum(i * hb - 1, 0), cb(j))))
        else:
            in_specs.append(pl.BlockSpec((HALO, bc), lambda j, i, cb=cb: (jnp.minimum((i + 1) * hb, last), cb(j))))
        args.append(arr)
    for arr, bc, cb in consts:
        in_specs.append(pl.BlockSpec((arr.shape[0], bc), lambda j, i, cb=cb: (0, cb(j))))
        args.append(arr)
    out_specs, out_shape = [], []
    for tc, dt, bc, cb in outs:
        out_specs.append(pl.BlockSpec((tm, bc), lambda j, i, cb=cb: (i, cb(j))))
        out_shape.append(jax.ShapeDtypeStruct((T, tc), dt))
    for r, tc, bc, cb in accs:
        out_specs.append(pl.BlockSpec((r, bc), lambda j, i, cb=cb: (0, cb(j))))
        out_shape.append(jax.ShapeDtypeStruct((r, tc), F32))
    nin, nout, nacc = len(args), len(outs), len(accs)

    def body(*refs):
        i = pl.program_id(1)
        res = fn(i, n, *[r[...] for r in refs[:nin]])
        for r, v in zip(refs[nin:nin + nout], res[:nout]):
            r[...] = v.astype(r.dtype)
        if nacc:
            acc_refs = refs[nin + nout:nin + nout + nacc]

            @pl.when(i == 0)
            def _():
                for r in acc_refs:
                    r[...] = jnp.zeros_like(r)

            for r, v in zip(acc_refs, res[nout:]):
                r[...] += v.astype(F32)

    res = pl.pallas_call(
        body, name=name, grid=(ncol, n), in_specs=in_specs, out_specs=out_specs, out_shape=out_shape,
        compiler_params=pltpu.CompilerParams(dimension_semantics=("arbitrary", "arbitrary")),
    )(*args)
    return res


def _cur(arr, bc=None, blk=0):
    bc = arr.shape[1] if bc is None else bc
    return (arr, bc, functools.partial(_const, v=blk), "cur")


def _halo(arr, kind, bc=None, blk=0):
    bc = arr.shape[1] if bc is None else bc
    return (arr, bc, functools.partial(_const, v=blk), kind)


def _cst(arr):
    return (arr, arr.shape[1], functools.partial(_const, v=0))


def _out(cols, dt):
    return (cols, dt, cols, functools.partial(_const, v=0))


def _acc(rows, cols):
    return (rows, cols, cols, functools.partial(_const, v=0))


def _rms(x, w):
    return x * lax.rsqrt(jnp.mean(x * x, axis=-1, keepdims=True) + NORM_EPS) * w


def _sigmoid(x):
    return 0.5 * jnp.tanh(0.5 * x) + 0.5


def _silu(x):
    return x * _sigmoid(x)


def _dsilu(x):
    s = _sigmoid(x)
    return s * (1.0 + x * (1.0 - s))


def _softplus(x):
    return jnp.maximum(x, 0.0) + jnp.log1p(jnp.exp(-jnp.abs(x)))


def _shift(a, k):
    return pltpu.roll(a, k % a.shape[0], 0)


def _lroll(a, k):
    return pltpu.roll(a, k % a.shape[1], 1)


def _vjp_wrap(f, nrow, nconst, add_first=False):
    def g(i, n, *vals):
        rows, consts, mid = vals[:nrow], vals[len(vals) - nconst:], vals[nrow:len(vals) - nconst]
        cots = mid[:-1] if add_first else mid
        outs, pull = jax.vjp(f, *rows, *consts)
        grads = list(pull(tuple(c.astype(o.dtype) for c, o in zip(cots, outs))))
        if add_first:
            grads[0] = grads[0] + mid[-1]
        return tuple(grads)
    return g


def _rows_vjp(f, T, tm, rows, consts, cots, out_dtypes, name):
    return _rows(_vjp_wrap(f, len(rows), len(consts)), T, tm, [_cur(r) for r in rows] + [_cur(c) for c in cots],
                 [_cst(c) for c in consts], [_out(r.shape[1], dt) for r, dt in zip(rows, out_dtypes)],
                 [_acc(1, c.shape[1]) for c in consts], name)


def _f_premix(x, g):
    return (_rms(x, g),)


def _f_mla_pre(cq, ckv, qn, kvn):
    return _rms(cq, qn), _rms(ckv, kvn)


def _f_ssd_gate(y, z, nw):
    return (_rms(y * _silu(z), nw),)


def _f_post_mix(x, mixed, gpost, gffn):
    x1 = x + _rms(mixed, gpost)
    return x1, _rms(x1, gffn)


def _f_post_ffn(x1, d, gpost):
    return (x1 + _rms(d, gpost),)


def _rope_fwd(v, cosf, sina, sinb):
    return v * cosf + _lroll(v, -16) * sina + _lroll(v, 16) * sinb


def _rope_bwd(g, cosf, sina, sinb):
    return g * cosf + _lroll(g * sina, 16) + _lroll(g * sinb, -16)


def _k_rope_fwd(i, n, qpad, kvpad, kr, cosf, sina, sinb):
    qs, ks = [], []
    krr = _rope_fwd(kr, cosf, sina, sinb)
    for h in range(HEADS):
        sl = slice(h * HP, (h + 1) * HP)
        qs.append(_rope_fwd(qpad[:, sl], cosf, sina, sinb) * QK_SCALE)
        ks.append(kvpad[:, sl].astype(F32) + krr)
    return jnp.concatenate(qs, axis=1), jnp.concatenate(ks, axis=1)


def _k_rope_bwd(i, n, dq, dk, dv, cosf, sina, sinb):
    lane = lax.broadcasted_iota(jnp.int32, (1, HP), 1)
    rmask = ((lane >= KR_LANE) & (lane < KR_LANE + ROPE)).astype(F32)
    dqs, dks = [], []
    dkr = jnp.zeros((dq.shape[0], HP), F32)
    for h in range(HEADS):
        sl = slice(h * HP, (h + 1) * HP)
        dqs.append(_rope_bwd(dq[:, sl] * QK_SCALE, cosf, sina, sinb))
        dkh = dk[:, sl]
        dkr = dkr + dkh * rmask
        dks.append(dkh * (1.0 - rmask))
    dkr = _rope_bwd(dkr, cosf, sina, sinb) * rmask
    return jnp.concatenate(dqs, axis=1), jnp.concatenate(dks + [dv], axis=1), dkr


def _k_sconv_fwd(i, n, b, c, h, cp, hp, w):
    m = b.shape[0]
    up = jnp.where(i > 0, cp * hp, 0.0)
    ue = jnp.concatenate([up, c * h], axis=0)
    conv = w[2:3] * ue + w[1:2] * _shift(ue, 1) + w[0:1] * _shift(ue, 2)
    return (b * conv[HALO:],)


def _k_sconv_bwd(i, n, b, c, h, dy, cp, hp, bn, dyn, w):
    m = b.shape[0]
    up = jnp.where(i > 0, cp * hp, 0.0)
    ue = jnp.concatenate([up, c * h], axis=0)
    u1, u2 = _shift(ue, 1), _shift(ue, 2)
    conv = (w[2:3] * ue + w[1:2] * u1 + w[0:1] * u2)[HALO:]
    dc_cur = dy * b
    dce = jnp.concatenate([dc_cur, jnp.where(i < n - 1, dyn * bn, 0.0)], axis=0)
    du = (w[2:3] * dce + w[1:2] * _shift(dce, -1) + w[0:1] * _shift(dce, -2))[:m]
    dw = jnp.concatenate([
        jnp.sum(dc_cur * u2[HALO:], axis=0, keepdims=True),
        jnp.sum(dc_cur * u1[HALO:], axis=0, keepdims=True),
        jnp.sum(dc_cur * ue[HALO:], axis=0, keepdims=True),
        jnp.zeros((HALO - 3, b.shape[1]), F32)], axis=0)
    return dy * conv, du * h, du * c, dw


def _conv4(ue, w):
    return w[3:4] * ue + w[2:3] * _shift(ue, 1) + w[1:2] * _shift(ue, 2) + w[0:1] * _shift(ue, 3)


def _k_ssdconv_fwd(i, n, u, up, w, bias):
    ue = jnp.concatenate([jnp.where(i > 0, up, 0.0), u], axis=0)
    return (_silu(_conv4(ue, w)[HALO:] + bias),)


def _k_ssdconv_bwd(i, n, u, dout, up, un, doutn, w, bias):
    m = u.shape[0]
    ue = jnp.concatenate([jnp.where(i > 0, up, 0.0), u, un], axis=0)
    u1, u2, u3 = _shift(ue, 1), _shift(ue, 2), _shift(ue, 3)
    pre = (w[3:4] * ue + w[2:3] * u1 + w[1:2] * u2 + w[0:1] * u3)[HALO:] + bias
    doe = jnp.concatenate([dout, jnp.where(i < n - 1, doutn, 0.0)], axis=0)
    dpre = doe * _dsilu(pre)
    du = (w[3:4] * dpre + w[2:3] * _shift(dpre, -1) + w[1:2] * _shift(dpre, -2) + w[0:1] * _shift(dpre, -3))[:m]
    dp = dpre[:m]
    cur = slice(HALO, HALO + m)
    dw = jnp.concatenate([
        jnp.sum(dp * u3[cur], axis=0, keepdims=True),
        jnp.sum(dp * u2[cur], axis=0, keepdims=True),
        jnp.sum(dp * u1[cur], axis=0, keepdims=True),
        jnp.sum(dp * ue[cur], axis=0, keepdims=True),
        jnp.zeros((HALO - 4, u.shape[1]), F32)], axis=0)
    db = jnp.sum(dp, axis=0, keepdims=True)
    return du, dw, db


def _conv3(ue, w):
    return w[2:3] * ue + w[1:2] * _shift(ue, 1) + w[0:1] * _shift(ue, 2)


def _k_ffnact_fwd(i, n, ug, uu, ugp, uup, wg, wu, bg, bu):
    gate = _conv3(jnp.concatenate([jnp.where(i > 0, ugp, 0.0), ug], axis=0), wg)[HALO:] + bg
    upv = _conv3(jnp.concatenate([jnp.where(i > 0, uup, 0.0), uu], axis=0), wu)[HALO:] + bu
    return (_silu(gate) * upv,)


def _k_ffnact_bwd(i, n, ug, uu, dact, ugp, uup, ugn, uun, dactn, wg, wu, bg, bu):
    m = ug.shape[0]
    cur = slice(HALO, HALO + m)

    def taps(p, c, nx):
        e = jnp.concatenate([jnp.where(i > 0, p, 0.0), c, nx], axis=0)
        return e, _shift(e, 1), _shift(e, 2)

    def back(d, w):
        return (w[2:3] * d + w[1:2] * _shift(d, -1) + w[0:1] * _shift(d, -2))[:m]

    def wgrad(d, t):
        return jnp.concatenate([jnp.sum(d[:m] * t[2][cur], axis=0, keepdims=True), jnp.sum(d[:m] * t[1][cur], axis=0, keepdims=True),
                                jnp.sum(d[:m] * t[0][cur], axis=0, keepdims=True), jnp.zeros((HALO - 3, d.shape[1]), F32)], axis=0)

    tg, tu = taps(ugp, ug, ugn), taps(uup, uu, uun)
    gate = (wg[2:3] * tg[0] + wg[1:2] * tg[1] + wg[0:1] * tg[2])[HALO:] + bg
    upv = (wu[2:3] * tu[0] + wu[1:2] * tu[1] + wu[0:1] * tu[2])[HALO:] + bu
    dae = jnp.concatenate([dact, jnp.where(i < n - 1, dactn, 0.0)], axis=0)
    sg = _sigmoid(gate)
    dg = dae * upv * (sg * (1.0 + gate * (1.0 - sg)))
    dup = dae * (gate * sg)
    return (back(dg, wg), back(dup, wu), wgrad(dg, tg), wgrad(dup, tu),
            jnp.sum(dg[:m], axis=0, keepdims=True), jnp.sum(dup[:m], axis=0, keepdims=True))


def _k_loss(i, n, y, tgt):
    e = y - tgt
    part = 0.5 * jnp.sum(jnp.sum(e * e, axis=1, keepdims=True) / D_MODEL, axis=0, keepdims=True)
    return e * (1.0 / D_MODEL), jnp.broadcast_to(part, (1, LANE))


def _k_adam(i, n, w, g, m, v):
    m = ADAM_B1 * m + (1.0 - ADAM_B1) * g
    v = ADAM_B2 * v + (1.0 - ADAM_B2) * (g * g)
    m_hat = m / (1.0 - ADAM_B1 ** ADAM_STEP)
    v_hat = v / (1.0 - ADAM_B2 ** ADAM_STEP)
    delta = -ADAM_LR * (m_hat / (jnp.sqrt(v_hat) + ADAM_EPS) + ADAM_WD * w)
    return g, delta, m, v


def _dotf(a, b, dims):
    return lax.dot_general(a.astype(MXU_DTYPE), b.astype(MXU_DTYPE), dims, preferred_element_type=F32)


NN = (((1,), (0,)), ((), ()))
NT = (((1,), (1,)), ((), ()))
TN = (((0,), (0,)), ((), ()))


def _ssd_chunk(x0, x1, x2, x3, b0, b1, c0, c1, dtraw, p0, p1, p2, p3, dtb, alog, dsk):
    xs, bs, cs_, ps = (x0, x1, x2, x3), (b0, b1), (c0, c1), (p0, p1, p2, p3)
    L = dtraw.shape[0]
    dt = _softplus(dtraw + dtb)
    adt = dt * (-jnp.exp(alog))
    row = lax.broadcasted_iota(jnp.int32, (L, L), 0)
    col = lax.broadcasted_iota(jnp.int32, (L, L), 1)
    tril = row >= col
    cum = jnp.dot(tril.astype(F32), adt, precision=HIGHEST, preferred_element_type=F32)
    cum_t = cum.T
    lane = lax.broadcasted_iota(jnp.int32, (1, LANE), 1)
    sub = lax.broadcasted_iota(jnp.int32, (LANE, 1), 0)
    lastcol = (lax.broadcasted_iota(jnp.int32, (1, L), 1) == L - 1).astype(F32)
    ys, news = [], []
    for h in range(SSD_HEADS):
        g = h // (SSD_HEADS // 2)
        oh = (lane == h).astype(F32)
        dth = jnp.sum(dt * oh, axis=1, keepdims=True)
        csh = jnp.sum(cum * oh, axis=1, keepdims=True)
        csr = jnp.sum(cum_t * (sub == h).astype(F32), axis=0, keepdims=True)
        cl = jnp.sum(csr * lastcol, axis=1, keepdims=True)
        dskh = jnp.sum(dsk * oh, axis=1, keepdims=True)
        x, bm, cm, prev = xs[h], bs[g], cs_[g], ps[h]
        xdt = x * dth
        decay = jnp.exp(jnp.where(tril, csh - csr, -jnp.inf))
        scores = _dotf(cm, bm, NT) * decay
        y_diag = _dotf(scores, xdt, NN)
        bd = bm * jnp.exp(cl - csh)
        cst = _dotf(xdt, bd, TN)
        news.append(prev * jnp.exp(cl) + cst)
        y_off = _dotf(cm, prev, NT) * jnp.exp(csh)
        ys.append(y_diag + y_off + x * dskh)
    return (*ys, *news)


SSD_STEP = 2


def _ssd_operands(x_ref, dt_ref, par_ref, prev, rows):
    xs = [x_ref[rows, h * SSD_HEAD_DIM:(h + 1) * SSD_HEAD_DIM] for h in range(SSD_HEADS)]
    bs = [x_ref[rows, SSD_DIM + g * SSD_STATE:SSD_DIM + (g + 1) * SSD_STATE] for g in range(2)]
    cs_ = [x_ref[rows, SSD_DIM + 2 * SSD_STATE + g * SSD_STATE:SSD_DIM + 2 * SSD_STATE + (g + 1) * SSD_STATE] for g in range(2)]
    return (*xs, *bs, *cs_, dt_ref[rows, :], *prev, par_ref[0:1, :], par_ref[1:2, :], par_ref[2:3, :])


def _ssd_fwd(xbc, dtraw, par, T, dt_blk=0):
    L = SSD_CHUNK
    nc = T // L
    P = SSD_HEAD_DIM
    U = SSD_STEP if nc % SSD_STEP == 0 else 1

    def body(x_ref, dt_ref, par_ref, y_ref, st_ref, state):
        @pl.when(pl.program_id(0) == 0)
        def _():
            state[...] = jnp.zeros_like(state)

        for u in range(U):
            rows = slice(u * L, (u + 1) * L)
            st_ref[u] = state[...]
            prev = [state[h * P:(h + 1) * P, :] for h in range(SSD_HEADS)]
            res = _ssd_chunk(*_ssd_operands(x_ref, dt_ref, par_ref, prev, rows))
            for h in range(SSD_HEADS):
                y_ref[rows, h * P:(h + 1) * P] = res[h]
                state[h * P:(h + 1) * P, :] = res[SSD_HEADS + h]

    return pl.pallas_call(
        body, name="ssd_scan_fwd", grid=(nc // U,),
        in_specs=[pl.BlockSpec((U * L, SSD_CONV_DIM), lambda c: (c, 0)), pl.BlockSpec((U * L, LANE), lambda c: (c, dt_blk)),
                  pl.BlockSpec((8, LANE), lambda c: (0, 0))],
        out_specs=[pl.BlockSpec((U * L, SSD_DIM), lambda c: (c, 0)), pl.BlockSpec((U, SSD_DIM, SSD_STATE), lambda c: (c, 0, 0))],
        out_shape=[jax.ShapeDtypeStruct((T, SSD_DIM), F32), jax.ShapeDtypeStruct((nc, SSD_DIM, SSD_STATE), F32)],
        scratch_shapes=[pltpu.VMEM((SSD_DIM, SSD_STATE), F32)],
        compiler_params=pltpu.CompilerParams(dimension_semantics=("arbitrary",)),
    )(xbc, dtraw, par)


def _ssd_bwd(xbc, dtraw, par, states, dy, T, dt_blk=0):
    L = SSD_CHUNK
    nc = T // L
    P = SSD_HEAD_DIM
    U = SSD_STEP if nc % SSD_STEP == 0 else 1
    ns = nc // U

    def body(x_ref, dt_ref, par_ref, st_ref, dy_ref, dx_ref, ddt_ref, dpar_ref, dstate):
        @pl.when(pl.program_id(0) == 0)
        def _():
            dstate[...] = jnp.zeros_like(dstate)
            dpar_ref[...] = jnp.zeros_like(dpar_ref)

        for u in reversed(range(U)):
            rows = slice(u * L, (u + 1) * L)
            prev = [st_ref[u, h * P:(h + 1) * P, :] for h in range(SSD_HEADS)]
            prim = _ssd_operands(x_ref, dt_ref, par_ref, prev, rows)
            _, pull = jax.vjp(_ssd_chunk, *prim)
            cots = tuple(dy_ref[rows, h * P:(h + 1) * P] for h in range(SSD_HEADS)) + tuple(
                dstate[h * P:(h + 1) * P, :] for h in range(SSD_HEADS))
            g = pull(cots)
            for h in range(SSD_HEADS):
                dx_ref[rows, h * P:(h + 1) * P] = g[h]
                dstate[h * P:(h + 1) * P, :] = g[9 + h]
            for k in range(2):
                dx_ref[rows, SSD_DIM + k * SSD_STATE:SSD_DIM + (k + 1) * SSD_STATE] = g[4 + k]
                dx_ref[rows, SSD_DIM + 2 * SSD_STATE + k * SSD_STATE:SSD_DIM + 2 * SSD_STATE + (k + 1) * SSD_STATE] = g[6 + k]
            ddt_ref[rows, :] = g[8]
            for r in range(3):
                dpar_ref[r:r + 1, :] += g[13 + r]

    rev = lambda c: (ns - 1 - c, 0)
    return pl.pallas_call(
        body, name="ssd_scan_bwd", grid=(ns,),
        in_specs=[pl.BlockSpec((U * L, SSD_CONV_DIM), rev), pl.BlockSpec((U * L, LANE), lambda c: (ns - 1 - c, dt_blk)),
                  pl.BlockSpec((8, LANE), lambda c: (0, 0)),
                  pl.BlockSpec((U, SSD_DIM, SSD_STATE), lambda c: (ns - 1 - c, 0, 0)), pl.BlockSpec((U * L, SSD_DIM), rev)],
        out_specs=[pl.BlockSpec((U * L, SSD_CONV_DIM), rev), pl.BlockSpec((U * L, LANE), rev), pl.BlockSpec((8, LANE), lambda c: (0, 0))],
        out_shape=[jax.ShapeDtypeStruct((T, SSD_CONV_DIM), F32), jax.ShapeDtypeStruct((T, LANE), F32),
                   jax.ShapeDtypeStruct((8, LANE), F32)],
        scratch_shapes=[pltpu.VMEM((SSD_DIM, SSD_STATE), F32)],
        compiler_params=pltpu.CompilerParams(dimension_semantics=("arbitrary",)),
    )(xbc, dtraw, par, states, dy)


def _causal_pairs(nq, by_query):
    if by_query:
        pairs = [(i, j) for i in range(nq) for j in range(i + 1)]
    else:
        pairs = [(i, j) for j in range(nq) for i in range(j, nq)]
    return jnp.asarray([p[0] for p in pairs], jnp.int32), jnp.asarray([p[1] for p in pairs], jnp.int32)


def _flash_fwd(q, k, kv, T, carry=()):
    tq = tk = min(FLASH_BLOCK, T)
    nq = T // tq
    G = FLASH_HEADS_FWD
    rep = tk // HP
    nc = len(carry)
    qi, kj = _causal_pairs(nq, by_query=True)
    nh, nt = HEADS // G, qi.shape[0]

    def body(qi_ref, kj_ref, q_ref, k_ref, v_ref, *rest):
        w_refs, o_ref, g_refs = rest[:nc], rest[nc], rest[nc + 1:2 * nc + 1]
        m_ref, l_ref, acc_ref = rest[2 * nc + 1:2 * nc + 4]
        h, t = pl.program_id(0), pl.program_id(1)
        i, j = qi_ref[t], kj_ref[t]
        if nc:
            plan = lambda: _ag_plan(w_refs, g_refs, rest[2 * nc + 4:])

            @pl.when((h == 0) & (t == 0))
            def _():
                for cp in plan()[0]:
                    cp.start()

        @pl.when(j == 0)
        def _():
            m_ref[...] = jnp.full_like(m_ref, -jnp.inf)
            l_ref[...] = jnp.zeros_like(l_ref)
            acc_ref[...] = jnp.zeros_like(acc_ref)

        def step(diagonal):
            for g in range(G):
                sl = slice(g * HP, (g + 1) * HP)
                s = _dotf(q_ref[:, sl], k_ref[:, sl], NT)
                if diagonal:
                    rows = lax.broadcasted_iota(jnp.int32, (tq, tk), 0)
                    cols = lax.broadcasted_iota(jnp.int32, (tq, tk), 1)
                    s = jnp.where(rows >= cols, s, -jnp.inf)
                m_old = m_ref[:, sl]
                m_new = jnp.maximum(m_old, jnp.max(s, axis=1, keepdims=True))
                p = jnp.exp(s - jnp.tile(m_new, (1, rep)))
                alpha = jnp.exp(m_old - m_new)
                l_ref[:, sl] = alpha * l_ref[:, sl] + jnp.sum(p, axis=1, keepdims=True)
                acc_ref[:, sl] = alpha * acc_ref[:, sl] + _dotf(p, v_ref[:, sl], NN)
                m_ref[:, sl] = m_new

        @pl.when(j < i)
        def _():
            step(False)

        @pl.when(j == i)
        def _():
            step(True)
            lane = lax.broadcasted_iota(jnp.int32, (tq, HP), 1)
            for g in range(G):
                sl = slice(g * HP, (g + 1) * HP)
                l = l_ref[:, sl]
                o_ref[:, sl] = jnp.where(lane < VDIM, acc_ref[:, sl] / l, m_ref[:, sl] + jnp.log(l))

        if nc:
            @pl.when(h * nt + t == (3 * nh * nt) // 4)
            def _():
                _, lands, forwards, _ = plan()
                for land, fw in zip(lands, forwards):
                    land.wait_recv()
                    fw.start()

            @pl.when((h == nh - 1) & (t == nt - 1))
            def _():
                sends, _, forwards, finals = plan()
                for cp in finals:
                    cp.wait_recv()
                for cp in sends + forwards:
                    cp.wait_send()

    W = G * HP
    res = pl.pallas_call(
        body, name="mla_flash_fwd",
        grid_spec=pltpu.PrefetchScalarGridSpec(
            num_scalar_prefetch=2, grid=(nh, nt),
            in_specs=[pl.BlockSpec((tq, W), lambda h, t, qi, kj: (qi[t], h)),
                      pl.BlockSpec((tk, W), lambda h, t, qi, kj: (kj[t], h)),
                      pl.BlockSpec((tk, W), lambda h, t, qi, kj: (kj[t], HEADS // G + h))] + [ANY] * nc,
            out_specs=[pl.BlockSpec((tq, W), lambda h, t, qi, kj: (qi[t], h))] + [ANY] * nc,
            scratch_shapes=[pltpu.VMEM((tq, W), F32), pltpu.VMEM((tq, W), F32), pltpu.VMEM((tq, W), F32)] + (_ag_sems(nc) if nc else [])),
        out_shape=[jax.ShapeDtypeStruct((T, HEADS * HP), F32)] + [jax.ShapeDtypeStruct((N_CHIPS,) + w.shape, w.dtype) for w in carry],
        compiler_params=pltpu.CompilerParams(dimension_semantics=("arbitrary", "arbitrary")),
    )(qi, kj, q, k, kv, *carry)
    return res[0] if not nc else (res[0], [_own_slot(g, w) for g, w in zip(res[1:], carry)])


def _flash_bwd(q, k, kv, o, dycat, T, carry=()):
    tq = tk = min(FLASH_BLOCK, T)
    nq = T // tq
    G = FLASH_HEADS
    nc = len(carry)
    qi, kj = _causal_pairs(nq, by_query=False)
    nh, nt = HEADS // G, qi.shape[0]

    W = G * HP

    def body(qi_ref, kj_ref, q_ref, k_ref, v_ref, o_ref, do_ref, *rest):
        p_refs, (dq_out, dk_ref, dv_ref), part_refs = rest[:nc], rest[nc:nc + 3], rest[nc + 3:2 * nc + 3]
        dq_ref, dq_sem = rest[2 * nc + 3:2 * nc + 5]
        h, t = pl.program_id(0), pl.program_id(1)
        i, j = qi_ref[t], kj_ref[t]
        if nc:
            plan = lambda: _chip_plan(p_refs, part_refs, rest[2 * nc + 5:])

            @pl.when((h == 0) & (t == 0))
            def _():
                for cp in plan()[0]:
                    cp.start()

        @pl.when(t == 0)
        def _():
            dq_ref[...] = jnp.zeros_like(dq_ref)

        @pl.when(i == j)
        def _():
            dk_ref[...] = jnp.zeros_like(dk_ref)
            dv_ref[...] = jnp.zeros_like(dv_ref)

        def step(diagonal):
            r0 = pl.multiple_of(i * tq, tq)
            for g in range(G):
                sl = slice(g * HP, (g + 1) * HP)
                qv, kv, vv, ov, dov = q_ref[:, sl], k_ref[:, sl], v_ref[:, sl], o_ref[:, sl], do_ref[:, sl]
                s = _dotf(qv, kv, NT)
                p = jnp.exp(s - ov[:, VDIM:VDIM + 1])
                if diagonal:
                    rows = lax.broadcasted_iota(jnp.int32, (tq, tk), 0)
                    cols = lax.broadcasted_iota(jnp.int32, (tq, tk), 1)
                    p = jnp.where(rows >= cols, p, 0.0)
                dsum = jnp.sum(dov * ov, axis=1, keepdims=True)
                dv_ref[:, sl] += _dotf(p, dov, TN)
                dp = _dotf(dov, vv, NT)
                ds = p * (dp - dsum)
                dk_ref[:, sl] += _dotf(ds, qv, TN)
                dq_ref[pl.ds(r0, tq), sl] += _dotf(ds, kv, NN)

        @pl.when(i > j)
        def _():
            step(False)

        @pl.when(i == j)
        def _():
            step(True)

        @pl.when(t == nt - 1)
        def _():
            out = pltpu.make_async_copy(dq_ref, dq_out.at[:, pl.ds(pl.multiple_of(h * W, W), W)], dq_sem)
            out.start()
            out.wait()

        if nc:
            @pl.when((h == nh - 1) & (t == nt - 1))
            def _():
                sends, lands = plan()
                for cp in lands:
                    cp.wait_recv()
                for cp in sends:
                    cp.wait_send()

    qmap = lambda h, t, qi, kj: (qi[t], h)
    kmap = lambda h, t, qi, kj: (kj[t], h)
    vmap = lambda h, t, qi, kj: (kj[t], HEADS // G + h)
    res = pl.pallas_call(
        body, name="mla_flash_bwd",
        grid_spec=pltpu.PrefetchScalarGridSpec(
            num_scalar_prefetch=2, grid=(nh, nt),
            in_specs=[pl.BlockSpec((tq, W), qmap), pl.BlockSpec((tk, W), kmap), pl.BlockSpec((tk, W), vmap),
                      pl.BlockSpec((tq, W), qmap), pl.BlockSpec((tq, W), qmap)] + [ANY] * nc,
            out_specs=[ANY, pl.BlockSpec((tk, W), kmap), pl.BlockSpec((tk, W), kmap)] + [ANY] * nc,
            scratch_shapes=[pltpu.VMEM((T, W), F32), pltpu.SemaphoreType.DMA] + (_chip_sems(nc) if nc else [])),
        out_shape=[jax.ShapeDtypeStruct((T, HEADS * HP), F32)] * 3 + [jax.ShapeDtypeStruct(p.shape, p.dtype) for p in carry],
        compiler_params=pltpu.CompilerParams(dimension_semantics=("arbitrary", "arbitrary")),
    )(qi, kj, q, k, kv, o, dycat, *carry)
    return tuple(res[:3]) if not nc else (*res[:3], _chip_parts(res[3:], carry))


_IN_SRC = (0, 256, 384, 416, 672, 928, 1184, 1440, 2208, 2212)
_IN_DST = (Z_CQ, Z_CKV, Z_KR + KR_LANE, Z_SCB, Z_SCC, Z_SCH, Z_SSZ, Z_XBC, Z_DT)


def _pad_rows_in(w):
    ax = w.ndim - 2

    def zeros(n):
        return jnp.zeros(w.shape[:ax] + (n,) + w.shape[ax + 1:], w.dtype)

    def whole_tiles(p):
        n = p.shape[ax]
        return p if n % SLAB_ALIGN == 0 else jnp.pad(p, [(0, 0)] * ax + [(0, -n % SLAB_ALIGN), (0, 0)])

    parts, at = [], 0
    for s0, s1, d0 in zip(_IN_SRC[:-1], _IN_SRC[1:], _IN_DST):
        if d0 > at:
            parts.append(zeros(d0 - at))
        parts.append(whole_tiles(lax.slice_in_dim(w, s0, s1, axis=ax)))
        at = d0 + parts[-1].shape[ax]
    parts.append(zeros(ZIN - at))
    return jnp.concatenate(parts, axis=ax)


def _unpad_rows_in(w):
    ax = w.ndim - 2
    groups = list(zip(_IN_SRC[:-1], _IN_SRC[1:], _IN_DST))
    parts = [lax.slice_in_dim(w, d0, d0 + -(-(s1 - s0) // SLAB_ALIGN) * SLAB_ALIGN, axis=ax) for s0, s1, d0 in groups]
    return lax.slice_in_dim(jnp.concatenate(parts, axis=ax), 0, _IN_SRC[-1], axis=ax)


def _pad_heads(w, width):
    w = w.reshape(w.shape[:-1] + (HEADS, width))
    w = jnp.pad(w, [(0, 0)] * (w.ndim - 1) + [(0, HP - width)])
    return w.reshape(w.shape[:-2] + (HEADS * HP,))


def _unpad_heads(w, width):
    w = w.reshape(w.shape[:-1] + (HEADS, HP))[..., :width]
    return w.reshape(w.shape[:-2] + (HEADS * width,))


def _pad_kv(w):
    w = w.reshape(w.shape[:-1] + (HEADS, NOPE + VDIM))
    return jnp.concatenate([_pad_heads(w[..., :NOPE].reshape(w.shape[:-2] + (HEADS * NOPE,)), NOPE),
                            _pad_heads(w[..., NOPE:].reshape(w.shape[:-2] + (HEADS * VDIM,)), VDIM)], axis=-1)


def _unpad_kv(w):
    k = _unpad_heads(w[..., :HEADS * HP], NOPE).reshape(w.shape[:-1] + (HEADS, NOPE))
    v = _unpad_heads(w[..., HEADS * HP:], VDIM).reshape(w.shape[:-1] + (HEADS, VDIM))
    return jnp.concatenate([k, v], axis=-1).reshape(w.shape[:-1] + (HEADS * (NOPE + VDIM),))


def _pad_out_rows(w):
    lead, d = w.shape[:-2], w.shape[-1]
    att = w[..., :HEADS * VDIM, :].reshape(lead + (HEADS, VDIM, d))
    att = jnp.pad(att, [(0, 0)] * (att.ndim - 2) + [(0, HP - VDIM), (0, 0)]).reshape(lead + (HEADS * HP, d))
    return jnp.concatenate([att, w[..., HEADS * VDIM:, :]], axis=-2)


def _unpad_out_rows(w):
    lead, d = w.shape[:-2], w.shape[-1]
    att = w[..., :HEADS * HP, :].reshape(lead + (HEADS, HP, d))[..., :VDIM, :].reshape(lead + (HEADS * VDIM, d))
    return jnp.concatenate([att, w[..., HEADS * HP:, :]], axis=-2)


def _rows8(w):
    return jnp.pad(w.astype(F32), [(0, 0)] * (w.ndim - 2) + [(0, 8 - w.shape[-2]), (0, 0)])


def _row8(*vecs):
    c = vecs[0].shape[-1]
    return jnp.concatenate([v.reshape(1, c).astype(F32) for v in vecs] + [jnp.zeros((8 - len(vecs), c), F32)], axis=0)


def _rope_tables(positions):
    inv_freq = 1.0 / (ROPE_THETA ** (jnp.arange(0, ROPE, 2, dtype=F32) / ROPE))
    ang = positions.astype(F32)[:, None] * inv_freq
    cos, sin = jnp.cos(ang), jnp.sin(ang)
    T = positions.shape[0]
    half = ROPE // 2
    one = jnp.ones((T, KR_LANE), F32)
    zero = jnp.zeros((T, KR_LANE), F32)
    tail1 = jnp.ones((T, HP - KR_LANE - ROPE), F32)
    tail0 = jnp.zeros((T, HP - KR_LANE - ROPE), F32)
    z16 = jnp.zeros((T, half), F32)
    cosf = jnp.concatenate([one, cos, cos, tail1], axis=1)
    sina = jnp.concatenate([zero, -sin, z16, tail0], axis=1)
    sinb = jnp.concatenate([zero, z16, sin, tail0], axis=1)
    return cosf, sina, sinb


def _kernel_weights(W):
    c = lambda a: a.astype(MXU_DTYPE)
    forms = dict(
        w_in=("w_in", lambda w: c(_pad_rows_in(w))),
        w_q=("mla_w_q_up", lambda w: c(_pad_heads(w, NOPE + ROPE))),
        w_kv=("mla_w_kv_up", lambda w: c(_pad_kv(w))),
        w_out=("w_out", lambda w: c(_pad_out_rows(w))),
        w_up=("ffn_w_up", c),
        w_down=("ffn_w_down", c),
        sc_w=("sc_conv_w", _rows8),
        ssd_w=("ssd_conv_w", _rows8),
        ffn_w=("ffn_conv_w", _rows8),
    )
    return {k: f(W[n]) for k, (n, f) in forms.items() if n in W}


def _layer_weights(KW, l):
    return {k: (v[l] if k in ("sc_w", "ssd_w", "ffn_w") else (v, l)) for k, v in KW.items()}


def _local_step(x, positions, target, W, S, ex=None):
    T = x.shape[0]
    tm = min(ROW_BLOCK, T)
    tmf = min(2 * ROW_BLOCK, T)
    tm_ffn = min(FFN_ROWS, T)
    cosf, sina, sinb = _rope_tables(positions)
    if ex is None:
        KW = _kernel_weights(W)
    else:
        early = _all_gather_weights(ex.shard(0, "early"))
    saved = []
    xl = x
    for l in range(DEPTH):
        lw = _layer_weights(KW, l) if ex is None else _kernel_weights(ex.weights(early, "early"))
        g_pre = S["norm_mix_pre"][l].reshape(1, -1)
        g_post = S["norm_mix_post"][l].reshape(1, -1)
        g_fpre = S["norm_ffn_pre"][l].reshape(1, -1)
        g_fpost = S["norm_ffn_post"][l].reshape(1, -1)
        qn = S["mla_q_norm"][l].reshape(1, -1)
        kvn = S["mla_kv_norm"][l].reshape(1, -1)
        ssd_b = S["ssd_conv_b"][l].reshape(1, -1)
        ssd_par = _row8(jnp.pad(S["ssd_dt_bias"][l], (0, LANE - SSD_HEADS)), jnp.pad(S["ssd_a_log"][l], (0, LANE - SSD_HEADS)),
                        jnp.pad(S["ssd_d"][l], (0, LANE - SSD_HEADS)))
        ssd_nw = S["ssd_norm"][l].reshape(1, -1)
        ffn_b = S["ffn_conv_b"][l].reshape(1, -1)

        (h1,) = _rows(lambda i, n, *v: _f_premix(*v), T, tmf, [_cur(xl)], [_cst(g_pre)], [_out(D_MODEL, BF16)], [], "pre_mix_norm")
        zin = _mm(h1, lw["w_in"], "nt", F32, "mm_in")
        qlat, kvlat = _rows(lambda i, n, *v: _f_mla_pre(*v), T, tmf, [_cur(zin, Q_LORA, 0), _cur(zin, KV_LORA, Z_CKV // KV_LORA)],
                            [_cst(qn), _cst(kvn)], [_out(Q_LORA, BF16), _out(KV_LORA, BF16)], [], "mla_pre_norm")
        qpad = _mm(qlat, lw["w_q"], "nn", F32, "mm_q_up")
        kvpad = _mm(kvlat, lw["w_kv"], "nn", BF16, "mm_kv_up")
        qr, kr = _rows(_k_rope_fwd, T, tmf, [_cur(qpad), _cur(kvpad, HEADS * HP, 0), _cur(zin, LANE, Z_KR // LANE),
                                            _cur(cosf), _cur(sina), _cur(sinb)], [],
                       [_out(HEADS * HP, BF16), _out(HEADS * HP, BF16)], [], "mla_rope")
        if ex is None:
            o = _flash_fwd(qr, kr, kvpad, T)
        else:
            nlate = len(ex.layouts["late"])
            o, got = _flash_fwd(qr, kr, kvpad, T, carry=ex.shard(l, "late") + (ex.shard(l + 1, "early") if l + 1 < DEPTH else []))
            lw.update(_kernel_weights(ex.weights(got[:nlate], "late")))
            early = got[nlate:]
        (yconv,) = _rows(_k_sconv_fwd, T, tmf, [_cur(zin, SC_DIM, Z_SCB // SC_DIM), _cur(zin, SC_DIM, Z_SCC // SC_DIM),
                                               _cur(zin, SC_DIM, Z_SCH // SC_DIM), _halo(zin, "prev", SC_DIM, Z_SCC // SC_DIM),
                                               _halo(zin, "prev", SC_DIM, Z_SCH // SC_DIM)], [_cst(lw["sc_w"])],
                         [_out(SC_DIM, F32)], [], "short_conv_fwd")
        (xbc,) = _rows(_k_ssdconv_fwd, T, tmf, [_cur(zin, SSD_CONV_DIM, Z_XBC // SSD_CONV_DIM),
                                               _halo(zin, "prev", SSD_CONV_DIM, Z_XBC // SSD_CONV_DIM)],
                       [_cst(lw["ssd_w"]), _cst(ssd_b)], [_out(SSD_CONV_DIM, F32)], [], "ssd_conv_fwd")
        yscan, states = _ssd_fwd(xbc, zin, ssd_par, T, Z_DT // LANE)
        (yssd,) = _rows(lambda i, n, *v: _f_ssd_gate(*v), T, tmf, [_cur(yscan), _cur(zin, SSD_DIM, Z_SSZ // SSD_DIM)], [_cst(ssd_nw)],
                        [_out(SSD_DIM, F32)], [], "ssd_gate_fwd")
        ycat = jnp.concatenate([o.astype(BF16), yconv.astype(BF16), yssd.astype(BF16)], axis=1)
        mixed = _mm(ycat, lw["w_out"], "nn", F32, "mm_out", tm=MM_TM)
        x1, h2 = _rows(lambda i, n, *v: _f_post_mix(*v), T, tmf, [_cur(xl), _cur(mixed)], [_cst(g_post), _cst(g_fpre)],
                       [_out(D_MODEL, F32), _out(D_MODEL, BF16)], [], "post_mix_fwd")
        upre = _mm(h2, lw["w_up"], "nn", F32, "mm_up")
        nt = FFN_DIM // FFN_TILE
        gcol, ucol = (lambda j: j), (lambda j: j + nt)
        (act,) = _rows(_k_ffnact_fwd, T, tm_ffn,
                       [(upre, FFN_TILE, gcol, "cur"), (upre, FFN_TILE, ucol, "cur"), (upre, FFN_TILE, gcol, "prev"),
                        (upre, FFN_TILE, ucol, "prev")],
                       [(lw["ffn_w"], FFN_TILE, gcol), (lw["ffn_w"], FFN_TILE, ucol), (ffn_b, FFN_TILE, gcol), (ffn_b, FFN_TILE, ucol)],
                       [(FFN_DIM, BF16, FFN_TILE, gcol)], [], "ffn_act_fwd", ncol=nt)
        dn = _mm(act, lw["w_down"], "nn", F32, "mm_down")
        (x2,) = _rows(lambda i, n, *v: _f_post_ffn(*v), T, tmf, [_cur(x1), _cur(dn)], [_cst(g_fpost)], [_out(D_MODEL, F32)], [], "post_ffn_fwd")
        saved.append(dict(lw=lw, x=xl, h1=h1, zin=zin, qlat=qlat, kvlat=kvlat, qr=qr, kr=kr, kvpad=kvpad, o=o, xbc=xbc,
                          yscan=yscan, states=states, ycat=ycat, mixed=mixed, x1=x1, h2=h2, upre=upre, act=act, dn=dn,
                          g_pre=g_pre, g_post=g_post, g_fpre=g_fpre, g_fpost=g_fpost, qn=qn, kvn=kvn, ssd_b=ssd_b,
                          ssd_par=ssd_par, ssd_nw=ssd_nw, ffn_b=ffn_b))
        xl = x2

    gx, loss_part = _rows(_k_loss, T, tmf, [_cur(xl), _cur(target)], [], [_out(D_MODEL, F32)], [_acc(1, LANE)], "loss_head")

    GW = {k: [None] * DEPTH for k in ("w_in", "mla_w_q_up", "mla_w_kv_up", "sc_conv_w", "ssd_conv_w", "w_out", "ffn_w_up",
                                      "ffn_conv_w", "ffn_w_down")}
    GS = {k: [None] * DEPTH for k in ("norm_mix_pre", "norm_mix_post", "norm_ffn_pre", "norm_ffn_post", "mla_q_norm", "mla_kv_norm",
                                      "ssd_conv_b", "ssd_dt_bias", "ssd_a_log", "ssd_d", "ssd_norm", "ffn_conv_b")}
    nt = FFN_DIM // FFN_TILE
    gcol, ucol = (lambda j: j), (lambda j: j + nt)
    pending = None
    for l in reversed(range(DEPTH)):
        s = saved[l]
        lw = s["lw"]
        gx1, ddn, dgf = _rows_vjp(_f_post_ffn, T, tm, [s["x1"], s["dn"]], [s["g_fpost"]], [gx], [F32, BF16], "post_ffn_bwd")
        GS["norm_ffn_post"][l] = dgf[0]
        dact = _mm(ddn, lw["w_down"], "nt", F32, "mm_down_dx")
        GW["ffn_w_down"][l] = _mm(s["act"], ddn, "tn", BF16, "mm_down_dw")
        up = s["upre"]
        dug, duu, dwg, dwu, dbg, dbu = _rows(
            _k_ffnact_bwd, T, tm_ffn,
            [(up, FFN_TILE, gcol, "cur"), (up, FFN_TILE, ucol, "cur"), (dact, FFN_TILE, gcol, "cur"), (up, FFN_TILE, gcol, "prev"),
             (up, FFN_TILE, ucol, "prev"), (up, FFN_TILE, gcol, "next"), (up, FFN_TILE, ucol, "next"), (dact, FFN_TILE, gcol, "next")],
            [(lw["ffn_w"], FFN_TILE, gcol), (lw["ffn_w"], FFN_TILE, ucol), (s["ffn_b"], FFN_TILE, gcol), (s["ffn_b"], FFN_TILE, ucol)],
            [(FFN_DIM, BF16, FFN_TILE, gcol)] * 2,
            [(HALO, FFN_DIM, FFN_TILE, gcol)] * 2 + [(1, FFN_DIM, FFN_TILE, gcol)] * 2, "ffn_act_bwd", ncol=nt)
        GW["ffn_conv_w"][l] = jnp.concatenate([dwg[:3], dwu[:3]], axis=1)
        GS["ffn_conv_b"][l] = jnp.concatenate([dbg[0], dbu[0]])
        dh2 = _mm((dug, duu), lw["w_up"], "nt", F32, "mm_up_dx")
        GW["ffn_w_up"][l] = (_mm(s["h2"], dug, "tn", BF16, "mm_up_dw_gate"), _mm(s["h2"], duu, "tn", BF16, "mm_up_dw_up"))
        gx0, dmixed, dgp, dgf = _rows_vjp(_f_post_mix, T, tm, [s["x"], s["mixed"]], [s["g_post"], s["g_fpre"]], [gx1, dh2],
                                          [F32, BF16], "post_mix_bwd")
        GS["norm_mix_post"][l], GS["norm_ffn_pre"][l] = dgp[0], dgf[0]
        dycat = _mm(dmixed, lw["w_out"], "nt", F32, "mm_out_dx")
        GW["w_out"][l] = _unpad_out_rows(_mm(s["ycat"], dmixed, "tn", BF16, "mm_out_dw"))
        zin = s["zin"]
        dyscan, dz, dnw = _rows(_vjp_wrap(_f_ssd_gate, 2, 1), T, tm,
                                [_cur(s["yscan"]), _cur(zin, SSD_DIM, Z_SSZ // SSD_DIM), _cur(dycat, SSD_DIM, (HEADS * HP + SC_DIM) // SSD_DIM)],
                                [_cst(s["ssd_nw"])], [_out(SSD_DIM, F32), _out(SSD_DIM, BF16)], [_acc(1, SSD_DIM)], "ssd_gate_bwd")
        GS["ssd_norm"][l] = dnw[0]
        dxbc, ddtraw, dpar = _ssd_bwd(s["xbc"], zin, s["ssd_par"], s["states"], dyscan, T, Z_DT // LANE)
        GS["ssd_dt_bias"][l], GS["ssd_a_log"][l], GS["ssd_d"][l] = dpar[0, :SSD_HEADS], dpar[1, :SSD_HEADS], dpar[2, :SSD_HEADS]
        xb = Z_XBC // SSD_CONV_DIM
        dxraw, dsw, dsb = _rows(_k_ssdconv_bwd, T, tm,
                                [_cur(zin, SSD_CONV_DIM, xb), _cur(dxbc), _halo(zin, "prev", SSD_CONV_DIM, xb),
                                 _halo(zin, "next", SSD_CONV_DIM, xb), _halo(dxbc, "next")],
                                [_cst(lw["ssd_w"]), _cst(s["ssd_b"])], [_out(SSD_CONV_DIM, BF16)],
                                [_acc(HALO, SSD_CONV_DIM), _acc(1, SSD_CONV_DIM)], "ssd_conv_bwd")
        GW["ssd_conv_w"][l] = dsw[:4]
        GS["ssd_conv_b"][l] = dsb[0]
        cb = (HEADS * HP) // SC_DIM
        dscb, dscc, dsch, dscw = _rows(_k_sconv_bwd, T, tm,
                                       [_cur(zin, SC_DIM, Z_SCB // SC_DIM), _cur(zin, SC_DIM, Z_SCC // SC_DIM),
                                        _cur(zin, SC_DIM, Z_SCH // SC_DIM), _cur(dycat, SC_DIM, cb),
                                        _halo(zin, "prev", SC_DIM, Z_SCC // SC_DIM), _halo(zin, "prev", SC_DIM, Z_SCH // SC_DIM),
                                        _halo(zin, "next", SC_DIM, Z_SCB // SC_DIM), _halo(dycat, "next", SC_DIM, cb)],
                                       [_cst(lw["sc_w"])], [_out(SC_DIM, BF16)] * 3, [_acc(HALO, SC_DIM)], "short_conv_bwd")
        GW["sc_conv_w"][l] = dscw[:3]
        if ex is None:
            dq, dk, dv = _flash_bwd(s["qr"], s["kr"], s["kvpad"], s["o"], dycat, T)
        else:
            sums = ex.submit([({n: GW[n][l] for ns in LATE for n in ns}, "late")] + ([(pending, "early")] if pending else []))
            late = sums[0]
            dq, dk, dv, parts = _flash_bwd(s["qr"], s["kr"], s["kvpad"], s["o"], dycat, T, carry=[p for ps in sums for p in ps])
            ex.collect(l, "late", parts[:len(late)])
            if pending:
                ex.collect(l + 1, "early", parts[len(late):])
        dqpad, dkvpad, dkr = _rows(_k_rope_bwd, T, tm, [_cur(dq), _cur(dk), _cur(dv), _cur(cosf), _cur(sina), _cur(sinb)], [],
                                   [_out(HEADS * HP, BF16), _out(2 * HEADS * HP, BF16), _out(LANE, BF16)], [], "mla_rope_bwd")
        dqlat = _mm(dqpad, lw["w_q"], "nt", F32, "mm_q_dx")
        GW["mla_w_q_up"][l] = _unpad_heads(_mm(s["qlat"], dqpad, "tn", BF16, "mm_q_dw"), NOPE + ROPE)
        dkvlat = _mm(dkvpad, lw["w_kv"], "nt", F32, "mm_kv_dx")
        GW["mla_w_kv_up"][l] = _unpad_kv(_mm(s["kvlat"], dkvpad, "tn", BF16, "mm_kv_dw"))
        dcq, dckv, dqn, dkvn = _rows(_vjp_wrap(_f_mla_pre, 2, 2), T, tm,
                                     [_cur(zin, Q_LORA, 0), _cur(zin, KV_LORA, Z_CKV // KV_LORA), _cur(dqlat), _cur(dkvlat)],
                                     [_cst(s["qn"]), _cst(s["kvn"])], [_out(Q_LORA, BF16), _out(KV_LORA, BF16)],
                                     [_acc(1, Q_LORA), _acc(1, KV_LORA)], "mla_pre_bwd")
        GS["mla_q_norm"][l], GS["mla_kv_norm"][l] = dqn[0], dkvn[0]
        dzin = jnp.concatenate([dcq, dckv, dkr, dscb, dscc, dsch, dz, dxraw, ddtraw.astype(BF16), jnp.zeros((T, ZIN - Z_DT - LANE), BF16)], axis=1)
        dh1 = _mm(dzin, lw["w_in"], "nn", F32, "mm_in_dx")
        GW["w_in"][l] = _unpad_rows_in(_mm(dzin, s["h1"], "tn", BF16, "mm_in_dw"))
        gx, dgp = _rows(_vjp_wrap(_f_premix, 1, 1, add_first=True), T, tm, [_cur(s["x"]), _cur(dh1), _cur(gx0)], [_cst(s["g_pre"])],
                        [_out(D_MODEL, F32)], [_acc(1, D_MODEL)], "pre_mix_bwd")
        GS["norm_mix_pre"][l] = dgp[0]
        if ex is not None:
            pending = {n: GW[n][l] for ns in EARLY for n in ns}
    if ex is not None:
        ex.collect(0, "early", _rs_chip_exchange(ex.submit([(pending, "early")])[0]))
    GS = {k: jnp.stack(v) for k, v in GS.items()}
    return loss_part[0, 0], gx, GW, GS


WEIGHTS = ("norm_mix_pre", "norm_mix_post", "norm_ffn_pre", "norm_ffn_post", "w_in", "mla_q_norm", "mla_w_q_up", "mla_kv_norm",
           "mla_w_kv_up", "sc_conv_w", "ssd_conv_w", "ssd_conv_b", "ssd_dt_bias", "ssd_a_log", "ssd_d", "ssd_norm", "w_out",
           "ffn_w_up", "ffn_conv_w", "ffn_conv_b", "ffn_w_down")
SHARDED = (("w_in", 2), ("mla_w_q_up", 2), ("mla_w_kv_up", 2), ("sc_conv_w", 2), ("ssd_conv_w", 2), ("w_out", 1),
           ("ffn_w_up", 2), ("ffn_conv_w", 2), ("ffn_w_down", 1))
SMALL = tuple(n for n in WEIGHTS if n not in dict(SHARDED))
N_CHIPS = 4
N_DEV = 8
ROW_ALIGN = 64
SLAB_ALIGN = 16
EARLY = (("w_in", "mla_w_q_up", "mla_w_kv_up", "sc_conv_w", "ssd_conv_w"),)
LATE = (("ffn_w_down", "w_out"), ("ffn_w_up", "ffn_conv_w"))
TRANSPOSED = ("w_in",)


def _is_rows(shape, width):
    return shape[-1] == width and math.prod(shape[:-1]) % SLAB_ALIGN == 0


def _is_short(shape, width):
    return len(shape) == 2 and shape[1] == width and not _is_rows(shape, width)


def _slab_rows(shape, width):
    if _is_rows(shape, width):
        return math.prod(shape[:-1])
    if _is_short(shape, width):
        return -(-shape[0] // SLAB_ALIGN) * SLAB_ALIGN
    return -(-math.prod(shape) // (width * SLAB_ALIGN)) * SLAB_ALIGN


def _slab(piece, width, dtype, lead=0):
    ld, shape = piece.shape[:lead], piece.shape[lead:]
    rows = _slab_rows(shape, width)
    if _is_rows(shape, width):
        return piece.astype(dtype).reshape(ld + (rows, width))
    if _is_short(shape, width):
        return jnp.pad(piece.astype(dtype), [(0, 0)] * lead + [(0, rows - shape[0]), (0, 0)])
    flat = piece.astype(dtype).reshape(ld + (-1,))
    return jnp.pad(flat, [(0, 0)] * lead + [(0, rows * width - flat.shape[-1])]).reshape(ld + (rows, width))


def _unslab(slab, shape, lead=0):
    ld = slab.shape[:lead]
    if _is_rows(shape, slab.shape[-1]):
        return slab.reshape(ld + tuple(shape))
    if _is_short(shape, slab.shape[-1]):
        return slab[..., :shape[0], :]
    return slab.reshape(ld + (-1,))[..., :math.prod(shape)].reshape(ld + tuple(shape))


def _layout(shapes, names, width):
    ents, off = [], 0
    for n in names:
        shp = tuple(shapes[n])
        todo = [(None, False, shp), (None, True, shp)] if n.endswith("conv_w") else [(l, False, shp[1:]) for l in range(shp[0])]
        for l, lo, ps in todo:
            r = _slab_rows(ps, width)
            ents.append((n, l, lo, ps, off, r))
            off += r
    return width, -(-off // ROW_ALIGN) * ROW_ALIGN, ents


def _pack(layout, piece, dtype, lead=0):
    width, rows, ents = layout
    slabs, ld = [], None
    for n, l, lo, ps, off, r in ents:
        p = piece(n, l, lo)
        slabs.append(None if p is None else _slab(p, width, dtype, lead))
        ld = ld if p is None else p.shape[:lead]
    used = ents[-1][4] + ents[-1][5]
    slabs = [jnp.zeros(ld + (e[5], width), dtype) if s is None else s for s, e in zip(slabs, ents)]
    if rows > used:
        slabs.append(jnp.zeros(ld + (rows - used, width), dtype))
    return jnp.concatenate(slabs, axis=lead)


ANY = pl.BlockSpec(memory_space=pl.ANY)


def _pos():
    return lax.axis_index("x"), lax.axis_index("y"), lax.axis_index("c")


def _other_chips(x, y):
    return ((1 - x, y), (x, 1 - y), (1 - x, 1 - y))


def _remote(src, dst, ssem, rsem, dev):
    return pltpu.make_async_remote_copy(src_ref=src, dst_ref=dst, send_sem=ssem, recv_sem=rsem, device_id=dev, device_id_type=MESH)


AG_CHUNKS = 2


def _chip_index():
    return 2 * lax.axis_index("x") + lax.axis_index("y")


def _ag_sems(nbuf):
    return [pltpu.SemaphoreType.DMA((nbuf * 3 * AG_CHUNKS,))] * 4


def _ag_plan(w_refs, out_refs, sems):
    isend, irecv, dsend, drecv = sems
    x, y, c = _pos()
    k = 2 * x + y
    sib = (x, y, 1 - c)
    sends, lands, forwards, finals = [], [], [], []
    s = 0
    for w_ref, out_ref in zip(w_refs, out_refs):
        H = w_ref.shape[0] // 2
        CH = H // AG_CHUNKS
        for cx, cy in _other_chips(x, y):
            for ch in range(AG_CHUNKS):
                mine = out_ref.at[k, pl.ds(c * H + ch * CH, CH), :]
                near = out_ref.at[2 * cx + cy, pl.ds(c * H + ch * CH, CH), :]
                far = out_ref.at[2 * cx + cy, pl.ds((1 - c) * H + ch * CH, CH), :]
                sends.append(_remote(w_ref.at[pl.ds(c * H + ch * CH, CH), :], mine, isend.at[s], irecv.at[s], (cx, cy, c)))
                lands.append(_remote(near, near, isend.at[s], irecv.at[s], (cx, cy, c)))
                forwards.append(_remote(near, near, dsend.at[s], drecv.at[s], sib))
                finals.append(_remote(far, far, dsend.at[s], drecv.at[s], sib))
                s += 1
    return sends, lands, forwards, finals


def _own_slot(got, own):
    return lax.dynamic_update_slice(got, own[None], (_chip_index(), 0, 0))


def _all_gather_weights(ws):
    nb = len(ws)

    def body(*refs):
        sends, lands, forwards, finals = _ag_plan(refs[:nb], refs[nb:2 * nb], refs[2 * nb:])
        for cp in sends:
            cp.start()
        for land, fw in zip(lands, forwards):
            land.wait_recv()
            fw.start()
        for cp in finals:
            cp.wait_recv()
        for cp in sends + forwards:
            cp.wait_send()

    got = pl.pallas_call(
        body, name="all_gather_weights", in_specs=[ANY] * nb, out_specs=[ANY] * nb,
        out_shape=[jax.ShapeDtypeStruct((N_CHIPS,) + w.shape, w.dtype) for w in ws], scratch_shapes=_ag_sems(nb),
    )(*ws)
    return [_own_slot(g, w) for g, w in zip(got, ws)]


def _rs_pair_exchange(gs):
    nb = len(gs)

    def body(*refs):
        g_refs, got_refs, (ssem, rsem) = refs[:nb], refs[nb:2 * nb], refs[2 * nb:]
        x, y, c = _pos()
        cps = []
        for b, (g_ref, got_ref) in enumerate(zip(g_refs, got_refs)):
            H = g_ref.shape[1] // 2
            for kk in range(N_CHIPS):
                s = b * N_CHIPS + kk
                cps.append(_remote(g_ref.at[kk, pl.ds((1 - c) * H, H), :], got_ref.at[kk], ssem.at[s], rsem.at[s], (x, y, 1 - c)))
        for cp in cps:
            cp.start()
        for cp in cps:
            cp.wait()

    return pl.pallas_call(
        body, name="rs_pair_exchange", in_specs=[ANY] * nb, out_specs=[ANY] * nb,
        out_shape=[jax.ShapeDtypeStruct((N_CHIPS, g.shape[1] // 2, g.shape[2]), g.dtype) for g in gs],
        scratch_shapes=[pltpu.SemaphoreType.DMA((nb * N_CHIPS,))] * 2,
    )(*gs)


def _chip_sems(nbuf):
    return [pltpu.SemaphoreType.DMA((nbuf * 3,))] * 2


def _chip_plan(p_refs, out_refs, sems):
    ssem, rsem = sems
    x, y, c = _pos()
    sends, lands = [], []
    s = 0
    for p_ref, out_ref in zip(p_refs, out_refs):
        for cx, cy in _other_chips(x, y):
            sends.append(_remote(p_ref.at[2 * cx + cy], out_ref.at[2 * x + y], ssem.at[s], rsem.at[s], (cx, cy, c)))
            land = out_ref.at[2 * cx + cy]
            lands.append(_remote(land, land, ssem.at[s], rsem.at[s], (cx, cy, c)))
            s += 1
    return sends, lands


def _chip_parts(got, ps):
    k = _chip_index()
    return [lax.dynamic_update_slice(g, lax.dynamic_slice_in_dim(p, k, 1, axis=0), (k, 0, 0)) for g, p in zip(got, ps)]


def _rs_chip_exchange(ps):
    nb = len(ps)

    def body(*refs):
        sends, lands = _chip_plan(refs[:nb], refs[nb:2 * nb], refs[2 * nb:])
        for cp in sends:
            cp.start()
        for cp in lands:
            cp.wait_recv()
        for cp in sends:
            cp.wait_send()

    got = pl.pallas_call(
        body, name="rs_chip_exchange", in_specs=[ANY] * nb, out_specs=[ANY] * nb,
        out_shape=[jax.ShapeDtypeStruct(p.shape, p.dtype) for p in ps], scratch_shapes=_chip_sems(nb),
    )(*ps)
    return _chip_parts(got, ps)


def _rs_pair_share(fs):
    nb = len(fs)

    def body(*refs):
        f_refs, out_refs, (ssem, rsem) = refs[:nb], refs[nb:2 * nb], refs[2 * nb:]
        x, y, c = _pos()
        sends, lands = [], []
        for b, (f_ref, out_ref) in enumerate(zip(f_refs, out_refs)):
            sends.append(_remote(f_ref, out_ref.at[c], ssem.at[b], rsem.at[b], (x, y, 1 - c)))
            land = out_ref.at[1 - c]
            lands.append(_remote(land, land, ssem.at[b], rsem.at[b], (x, y, 1 - c)))
        for cp in sends:
            cp.start()
        for cp in lands:
            cp.wait_recv()
        for cp in sends:
            cp.wait_send()

    got = pl.pallas_call(
        body, name="rs_pair_share", in_specs=[ANY] * nb, out_specs=[ANY] * nb,
        out_shape=[jax.ShapeDtypeStruct((2,) + f.shape, f.dtype) for f in fs],
        scratch_shapes=[pltpu.SemaphoreType.DMA((nb,))] * 2,
    )(*fs)
    return [lax.dynamic_update_slice(g, f[None], (lax.axis_index("c"), 0, 0)) for g, f in zip(got, fs)]


def _all_reduce_small(s):
    r, C = s.shape

    def body(s_ref, o_ref, buf, ssem, rsem):
        x, y, c = _pos()
        me = 4 * x + 2 * y + c
        buf[me] = s_ref[...]
        cps = []
        for m in range(1, N_DEV):
            mx, my, mc = (m >> 2) & 1, (m >> 1) & 1, m & 1
            peer = (x ^ mx, y ^ my, c ^ mc)
            cp = _remote(s_ref, buf.at[me], ssem.at[m - 1], rsem.at[m - 1], peer)
            cp.start()
            cps.append(cp)
        for m in range(1, N_DEV):
            mx, my, mc = (m >> 2) & 1, (m >> 1) & 1, m & 1
            src = 4 * (x ^ mx) + 2 * (y ^ my) + (c ^ mc)
            _remote(s_ref, buf.at[src], ssem.at[m - 1], rsem.at[m - 1], (x ^ mx, y ^ my, c ^ mc)).wait_recv()
        for cp in cps:
            cp.wait_send()
        acc = buf[0]
        for j in range(1, N_DEV):
            acc = acc + buf[j]
        o_ref[...] = acc

    return pl.pallas_call(
        body, name="all_reduce_small", in_specs=[pl.BlockSpec(memory_space=pltpu.VMEM)],
        out_specs=pl.BlockSpec(memory_space=pltpu.VMEM), out_shape=jax.ShapeDtypeStruct((r, C), F32),
        scratch_shapes=[pltpu.VMEM((N_DEV, r, C), F32), pltpu.SemaphoreType.DMA((N_DEV - 1,)), pltpu.SemaphoreType.DMA((N_DEV - 1,))],
    )(s)


def _rtile(n, pref):
    if n <= pref:
        return n
    t = (pref // 16) * 16
    while t >= 16:
        if n % t == 0:
            return t
        t -= 16
    raise ValueError(f"no row tile for {n}")


def _rs_pair_sums(gpks):
    gots = _rs_pair_exchange(gpks)
    out = []
    for gpk, got in zip(gpks, gots):
        _, R, C = gpk.shape
        H = R // 2
        own = lax.dynamic_index_in_dim(gpk.reshape(N_CHIPS, 2, H, C), lax.axis_index("c"), axis=1, keepdims=False)
        (part,) = _rows(lambda i, n, a, b: (a.astype(F32) + b.astype(F32),), N_CHIPS * H, _rtile(N_CHIPS * H, 512),
                        [_cur(own.reshape(N_CHIPS * H, C)), _cur(got.reshape(N_CHIPS * H, C))], [], [_out(C, BF16)], [], "rs_pair_add")
        out.append(part.reshape(N_CHIPS, H, C))
    return out


def _rs_chip_sums(parts):
    def add4(i, n, a, b, c, d):
        return (((a.astype(F32) + b.astype(F32)) + c.astype(F32)) + d.astype(F32),)

    out = []
    for p in parts:
        _, H, C = p.shape
        tm = _rtile(H, 1024)
        (red,) = _rows(add4, H, tm, [(p.reshape(N_CHIPS * H, C), C, functools.partial(_const, v=0), j * (H // tm)) for j in range(N_CHIPS)],
                       [], [_out(C, F32)], [], "rs_chip_add")
        out.append(red)
    return out


class _Exchange:
    def __init__(self, a):
        self.a = a
        self.axis = {n: (1 if n in TRANSPOSED else ax) for n, ax in SHARDED}
        shapes = {n: (1,) + tuple(self.packed(n, a[n]).shape[1:]) for n in self.axis}
        widths = lambda names: shapes[names[0]][-1] if names[0] == "ffn_w_up" else PACK_COLS
        self.layouts = {"early": [_layout(shapes, ns, widths(ns)) for ns in EARLY], "late": [_layout(shapes, ns, widths(ns)) for ns in LATE]}
        self.reduced = {}

    @staticmethod
    def packed(n, w):
        return jnp.swapaxes(w, -1, -2) if n in TRANSPOSED else w

    def shard(self, l, group):
        def piece(n, li, lo):
            w = self.packed(n, self.a[n][l:l + 1] if li is None else self.a[n][l])
            return w - w.astype(BF16).astype(F32) if lo else w
        return [_pack(lay, piece, BF16) for lay in self.layouts[group]]

    def weights(self, gathered, group):
        W, resid = {}, {}
        for (width, rows, ents), g in zip(self.layouts[group], gathered):
            for n, li, lo, ps, off, r in ents:
                parts = _unslab(g[:, off:off + r], ps, lead=1)
                ax = self.axis[n] + (1 if li is None else 0)
                full = jnp.moveaxis(parts, 0, ax - 1)
                full = full.reshape(full.shape[:ax - 1] + (-1,) + full.shape[ax + 1:])
                (resid if lo else W)[n] = full[0] if li is None else full
        for n in resid:
            W[n] = W[n].astype(F32) + resid[n].astype(F32)
        return W

    def submit(self, jobs):
        def by_chip(g, ax, parts=N_CHIPS):
            g = g.reshape(g.shape[:ax] + (parts, g.shape[ax] // parts) + g.shape[ax + 1:])
            return jnp.moveaxis(g, ax, 0)

        def pieces_of(GW):
            def piece(n, li, lo):
                if lo:
                    return None
                g = GW[n]
                if isinstance(g, tuple):
                    return jnp.concatenate([by_chip(h, self.axis[n] - 1, N_CHIPS // 2) for h in g])
                return by_chip(g[None], self.axis[n]) if li is None else by_chip(g, self.axis[n] - 1)
            return piece

        sums = _rs_pair_sums([_pack(lay, pieces_of(GW), BF16, lead=1) for GW, group in jobs for lay in self.layouts[group]])
        out, at = [], 0
        for _, group in jobs:
            out.append(sums[at:at + len(self.layouts[group])])
            at += len(self.layouts[group])
        return out

    def collect(self, l, group, parts):
        self.reduced[l, group] = _rs_chip_sums(parts)

    def finish(self):
        keys = [(l, g) for l in range(DEPTH) for g in self.layouts]
        flat = _rs_pair_share([f for key in keys for f in self.reduced[key]])
        both, at = {}, 0
        for key in keys:
            both[key] = flat[at:at + len(self.layouts[key[1]])]
            at += len(self.layouts[key[1]])
        grads = {}
        for group, lays in self.layouts.items():
            for b, (width, rows, ents) in enumerate(lays):
                for n, li, lo, ps, off, r in ents:
                    if not lo:
                        per_layer = [self.packed(n, _unslab(both[l, group][b].reshape(rows, width)[off:off + r], ps)) for l in range(DEPTH)]
                        grads[n] = jnp.concatenate(per_layer) if li is None else jnp.stack(per_layer)
        return grads


def _adam(w, g, m, v, name, g_row=0):
    shp = w.shape
    two = lambda a: a.reshape(-1, shp[-1])
    rows = math.prod(shp[:-1])
    tm = _rtile(rows, ROW_BLOCK if shp[-1] <= PACK_COLS else ROW_BLOCK // 2)
    assert g_row % tm == 0
    g_in = (two(g), shp[-1], functools.partial(_const, v=0), g_row // tm)
    res = _rows(_k_adam, rows, tm, [_cur(two(w)), g_in, _cur(two(m)), _cur(two(v))], [], [_out(shp[-1], F32)] * 4, [], name)
    return tuple(r.reshape(shp) for r in res)


def _pack_flat(parts, rows):
    flat = jnp.concatenate([p.astype(F32).reshape(-1) for p in parts])
    return jnp.pad(flat, (0, rows * PACK_COLS - flat.shape[0])).reshape(rows, PACK_COLS)


def _unpack_flat(buf, shapes):
    flat, out, off = buf.reshape(-1), [], 0
    for shp in shapes:
        n = math.prod(shp)
        out.append(flat[off:off + n].reshape(shp))
        off += n
    return out


def kernel(x, positions, norm_mix_pre, norm_mix_post, norm_ffn_pre, norm_ffn_post, w_in, mla_q_norm, mla_w_q_up, mla_kv_norm, mla_w_kv_up, sc_conv_w, ssd_conv_w, ssd_conv_b, ssd_dt_bias, ssd_a_log, ssd_d, ssd_norm, w_out, ffn_w_up, ffn_conv_w, ffn_conv_b, ffn_w_down, loss_target, m_norm_mix_pre, m_norm_mix_post, m_norm_ffn_pre, m_norm_ffn_post, m_w_in, m_mla_q_norm, m_mla_w_q_up, m_mla_kv_norm, m_mla_w_kv_up, m_sc_conv_w, m_ssd_conv_w, m_ssd_conv_b, m_ssd_dt_bias, m_ssd_a_log, m_ssd_d, m_ssd_norm, m_w_out, m_ffn_w_up, m_ffn_conv_w, m_ffn_conv_b, m_ffn_w_down, v_norm_mix_pre, v_norm_mix_post, v_norm_ffn_pre, v_norm_ffn_post, v_w_in, v_mla_q_norm, v_mla_w_q_up, v_mla_kv_norm, v_mla_w_kv_up, v_sc_conv_w, v_ssd_conv_w, v_ssd_conv_b, v_ssd_dt_bias, v_ssd_a_log, v_ssd_d, v_ssd_norm, v_w_out, v_ffn_w_up, v_ffn_conv_w, v_ffn_conv_b, v_ffn_w_down):
    a = dict(locals())
    ex = _Exchange(a)
    S = {n: a[n] for n in SMALL}
    loss_part, gx, _, GS = _local_step(a["x"][0], a["positions"][0], a["loss_target"][0], None, S, ex)

    grads, delta, new_m, new_v = {}, {}, {}, {}
    for n, g in ex.finish().items():
        grads[n], delta[n], new_m[n], new_v[n] = _adam(a[n], g, a["m_" + n], a["v_" + n], "adamw_" + n)

    small_shapes = [a[n].shape for n in SMALL]
    rs = -(-(sum(math.prod(s) for s in small_shapes) + 1) // (PACK_COLS * SLAB_ALIGN)) * SLAB_ALIGN
    red = _all_reduce_small(_pack_flat([GS[n] for n in SMALL] + [loss_part.reshape(1)], rs))
    loss = _unpack_flat(red, small_shapes + [(1,)])[-1][0]
    pk = lambda pre: _pack_flat([a[pre + n] for n in SMALL], rs)
    for dst, buf in zip((grads, delta, new_m, new_v), _adam(pk(""), red, pk("m_"), pk("v_"), "adamw_small")):
        dst.update(zip(SMALL, _unpack_flat(buf, small_shapes)))

    return (loss, gx[None], *[grads[n] for n in WEIGHTS], *[delta[n] for n in WEIGHTS], *[new_m[n] for n in WEIGHTS],
            *[new_v[n] for n in WEIGHTS])
```
